```python
import math
import jax
import jax.numpy as jnp
from jax import lax
import numpy as np

D_MODEL = 1024
BATCH = 1
SEQ = 16384
DEPTH = 2

GRID_W = 64
CTX_LEN = 256
NORM_EPS = 1e-6

S5_GROUP = 16
S5_GROUPS = 32
S5_WIDTH = S5_GROUP * S5_GROUPS
S5_STATE = 64
S5_DT_MIN = 1e-3
S5_DT_MAX = 1e-1
S5_LAMBDA_RE_MAX = -1e-4
NA_HEADS = 8
NA_HEAD_DIM = 64
NA_WIDTH = NA_HEADS * NA_HEAD_DIM
NA_KH_MAX = 8
NA_KW = 16
IN_WIDTH = S5_WIDTH + 3 * NA_WIDTH
MIX_WIDTH = S5_WIDTH + NA_WIDTH
FOURIER_GROUPS = 4
N_EXPERTS = 64
N_EXPERT_GROUPS = 8
TOPK_GROUPS = 4
TOP_K = 8
EXPERT_FF = 256
SHARED_FF = 256
ROUTED_SCALE = 2.5
EXPERT_BLOCK = 128

kernel_name = 'hybrid_s5_natten_fnet_moe_dit'


def rms_norm(x):
    xf = x.astype(jnp.float32)
    return (xf * lax.rsqrt(jnp.mean(xf * xf, axis=-1, keepdims=True) + NORM_EPS)).astype(x.dtype)


def modulate(h, shift, scale):
    return rms_norm(h) * (1 + scale) + shift


def adaln(cond, w, b):
    return jnp.split(jax.nn.silu(cond) @ w + b, 6, axis=-1)


def _linear_recurrence(left, right):
    a_l, b_l = left
    a_r, b_r = right
    return a_l * a_r, a_r * b_l + b_r


def s5_discretize(lam_re, lam_im, log_dt, b_re, b_im):
    f32 = jnp.float32
    lam = lax.complex(jnp.minimum(lam_re.astype(f32), S5_LAMBDA_RE_MAX), lam_im.astype(f32))
    dt = jnp.exp(log_dt.astype(f32))[:, None]
    lam_bar = jnp.exp(lam * dt)
    b_bar = ((lam_bar - 1.0) / lam)[..., None] * lax.complex(b_re.astype(f32), b_im.astype(f32))
    return lam_bar, b_bar


def s5_scan(u, lam_bar, b_bar, init, reverse):
    bu = jnp.einsum('btgc,gpc->btgp', u.astype(jnp.complex64), b_bar)
    entry = -1 if reverse else 0
    bu = bu.at[:, entry].add(lam_bar * init)
    a = jnp.broadcast_to(lam_bar, bu.shape)
    _, states = lax.associative_scan(_linear_recurrence, (a, bu), reverse=reverse, axis=1)
    return states


def s5_readout(st_f, st_b, c_f, c_b, d, u, w_glu, b_glu, out_dtype):
    y = (jnp.real(jnp.einsum('btgp,gcp->btgc', st_f, c_f))
         + jnp.real(jnp.einsum('btgp,gcp->btgc', st_b, c_b)) + d * u)
    g = jax.nn.gelu(y.reshape(u.shape[0], u.shape[1], S5_WIDTH).astype(out_dtype))
    return g * jax.nn.sigmoid(g @ w_glu + b_glu)


def neighbourhood_attention(q, k, v, k_ctx, v_ctx, rpb, rows):
    bsz = q.shape[0]
    kh = min(NA_KH_MAX, rows)
    n_local = kh * NA_KW
    scale = NA_HEAD_DIM ** -0.5
    q_col = np.arange(GRID_W)
    col_start = np.clip(q_col - NA_KW // 2, 0, GRID_W - NA_KW)
    key_cols = col_start[:, None] + np.arange(NA_KW)
    rpb_col = rpb[:, :, key_cols - q_col[:, None] + NA_KW - 1]

    def one_row(r):
        r0 = jnp.clip(r - kh // 2, 0, rows - kh)
        key_rows = r0 + jnp.arange(kh)
        idx = (key_rows[None, :, None] * GRID_W + key_cols[:, None, :]).reshape(GRID_W, n_local)
        q_r = lax.dynamic_slice_in_dim(q, r * GRID_W, GRID_W, axis=1)
        k_r = k[:, idx]
        v_r = v[:, idx]
        bias = rpb_col[:, key_rows - r + NA_KH_MAX - 1]
        bias = jnp.transpose(bias, (0, 2, 1, 3)).reshape(NA_HEADS, GRID_W, n_local)
        s_loc = jnp.einsum('bqhd,bqkhd->bhqk', q_r, k_r).astype(jnp.float32) * scale + bias
        s_ctx = jnp.einsum('bqhd,bkhd->bhqk', q_r, k_ctx).astype(jnp.float32) * scale
        p = jax.nn.softmax(jnp.concatenate([s_loc, s_ctx], axis=-1), axis=-1).astype(v.dtype)
        o = jnp.einsum('bhqk,bqkhd->bqhd', p[..., :n_local], v_r)
        return o + jnp.einsum('bhqk,bkhd->bqhd', p[..., n_local:], v_ctx)

    out = lax.map(one_row, jnp.arange(rows))
    return jnp.moveaxis(out, 0, 1).reshape(bsz, rows * GRID_W, NA_WIDTH)


def context_attention(q, k, v):
    s = jnp.einsum('bqhd,bkhd->bhqk', q, k).astype(jnp.float32) * NA_HEAD_DIM ** -0.5
    p = jax.nn.softmax(s, axis=-1).astype(v.dtype)
    o = jnp.einsum('bhqk,bkhd->bqhd', p, v)
    return o.reshape(q.shape[0], q.shape[1], NA_WIDTH)


def even_mixer(a_c, a_l, rows, w_in, lam_re, lam_im, log_dt, b_re, b_im, c_re, c_im, s5_d,
               w_glu, b_glu, q_gain, k_gain, rpb, w_out, need_ctx):
    def project(a):
        bsz, t = a.shape[:2]
        z = a @ w_in
        u = z[..., :S5_WIDTH].astype(jnp.float32).reshape(bsz, t, S5_GROUPS, S5_GROUP)
        qkv = z[..., S5_WIDTH:].reshape(bsz, t, 3, NA_HEADS, NA_HEAD_DIM)
        return u, qkv[:, :, 0], rms_norm(qkv[:, :, 1]) * k_gain, qkv[:, :, 2]

    u_c, qr_c, k_c, v_c = project(a_c)
    u_l, qr_l, k_l, v_l = project(a_l)

    f32 = jnp.float32
    lam_f, bb_f = s5_discretize(lam_re[0], lam_im[0], log_dt[0], b_re[0], b_im[0])
    lam_b, bb_b = s5_discretize(lam_re[1], lam_im[1], log_dt[1], b_re[1], b_im[1])
    cc_f = lax.complex(c_re[0].astype(f32), c_im[0].astype(f32))
    cc_b = lax.complex(c_re[1].astype(f32), c_im[1].astype(f32))
    d32 = s5_d.astype(f32)
    zero = jnp.zeros((u_c.shape[0], S5_GROUPS, S5_STATE), jnp.complex64)
    st_cf = s5_scan(u_c, lam_f, bb_f, zero, False)
    st_cb = s5_scan(u_c, lam_b, bb_b, zero, True)
    st_lf = s5_scan(u_l, lam_f, bb_f, st_cf[:, -1], False)
    st_lb = s5_scan(u_l, lam_b, bb_b, st_cb[:, 0], True)
    s5_l = s5_readout(st_lf, st_lb, cc_f, cc_b, d32, u_l, w_glu, b_glu, a_l.dtype)
    na_l = neighbourhood_attention(rms_norm(qr_l) * q_gain, k_l, v_l, k_c, v_c, rpb, rows)
    y_l = jnp.concatenate([s5_l, na_l], axis=-1) @ w_out
    if not need_ctx:
        return None, y_l
    s5_c = s5_readout(st_cf, st_cb, cc_f, cc_b, d32, u_c, w_glu, b_glu, a_c.dtype)
    na_c = context_attention(rms_norm(qr_c) * q_gain, k_c, v_c)
    y_c = jnp.concatenate([s5_c, na_c], axis=-1) @ w_out
    return y_c, y_l


def fourier_mixer(a, w, b):
    bsz, t, d = a.shape
    hg = a.astype(jnp.float32).reshape(bsz, t, FOURIER_GROUPS, d // FOURIER_GROUPS)
    f = jnp.real(jnp.fft.fft2(hg, axes=(1, 3), norm='ortho'))
    return f.reshape(bsz, t, d).astype(a.dtype) @ w + b


def routed_experts(h, e_idx, e_w, w_gate, w_up, w_down):
    n, d = h.shape
    n_exp = w_gate.shape[0]
    nk = n * TOP_K
    flat_e = e_idx.reshape(-1)
    flat_tok = jnp.repeat(jnp.arange(n, dtype=jnp.int32), TOP_K)
    flat_w = e_w.reshape(-1)
    order = jnp.argsort(flat_e)
    se, st, sw = flat_e[order], flat_tok[order], flat_w[order]
    counts = jnp.bincount(flat_e, length=n_exp)
    padded = (counts + EXPERT_BLOCK - 1) // EXPERT_BLOCK * EXPERT_BLOCK
    pad_end = jnp.cumsum(padded)
    pad_start = pad_end - padded
    start = jnp.cumsum(counts) - counts
    dest = pad_start[se] + (jnp.arange(nk) - start[se])
    n_blocks = (nk + n_exp * (EXPERT_BLOCK - 1) + EXPERT_BLOCK - 1) // EXPERT_BLOCK
    cap = n_blocks * EXPERT_BLOCK
    buf_tok = jnp.full((cap,), n, jnp.int32).at[dest].set(st)
    buf_w = jnp.zeros((cap,), h.dtype).at[dest].set(sw)
    blk_e = jnp.minimum(jnp.searchsorted(pad_end, jnp.arange(n_blocks) * EXPERT_BLOCK, side='right'),
                        n_exp - 1)
    h_pad = jnp.concatenate([h, jnp.zeros((1, d), h.dtype)], axis=0)

    def one_block(args):
        tok, wgt, e = args
        xb = h_pad[tok]
        hid = jax.nn.silu(xb @ w_gate[e]) * (xb @ w_up[e])
        return (hid @ w_down[e]) * wgt[:, None]

    y = lax.map(one_block, (buf_tok.reshape(n_blocks, EXPERT_BLOCK),
                            buf_w.reshape(n_blocks, EXPERT_BLOCK), blk_e))
    out = jnp.zeros((n + 1, d), h.dtype).at[buf_tok].add(y.reshape(cap, d))
    return out[:n]


def moe_ffn(h, w_router, router_bias, w_eg, w_eu, w_ed, w_sg, w_su, w_sd):
    n = h.shape[0]
    scores = jax.nn.sigmoid((h @ w_router).astype(jnp.float32))
    sel = scores + router_bias.astype(jnp.float32)
    grp = sel.reshape(n, N_EXPERT_GROUPS, N_EXPERTS // N_EXPERT_GROUPS)
    grp_score = jnp.sum(lax.top_k(grp, 2)[0], axis=-1)
    _, top_grp = lax.top_k(grp_score, TOPK_GROUPS)
    grp_keep = jnp.any(top_grp[:, :, None] == jnp.arange(N_EXPERT_GROUPS), axis=1)
    keep = jnp.repeat(grp_keep, N_EXPERTS // N_EXPERT_GROUPS, axis=1)
    _, idx = lax.top_k(jnp.where(keep, sel, -jnp.inf), TOP_K)
    gate = jnp.take_along_axis(scores, idx, axis=1)
    gate = ROUTED_SCALE * gate / jnp.sum(gate, axis=-1, keepdims=True)
    routed = routed_experts(h, idx, gate.astype(h.dtype), w_eg, w_eu, w_ed)
    shared = (jax.nn.silu(h @ w_sg) * (h @ w_su)) @ w_sd
    return routed + shared


def setup_inputs(seed: int = 0) -> dict:
    key = jax.random.key(seed)
    keys = iter(jax.random.split(key, 40))
    f32 = jnp.float32
    d = D_MODEL
    n_even = (DEPTH + 1) // 2
    n_odd = DEPTH // 2

    def nrm(shape, scale):
        return jax.random.normal(next(keys), shape, f32) * scale

    s5_shape = (n_even, 2, S5_GROUPS, S5_STATE)
    c_scale = S5_STATE ** -0.25
    return {
        'x': nrm((BATCH, SEQ, d), 1.0),
        'c': nrm((BATCH, d), 1.0),
        'ctx': nrm((BATCH, CTX_LEN, d), 1.0),
        'c_ctx': nrm((d,), 1.0),
        'w_ada': nrm((DEPTH, d, 6 * d), 0.5 * d ** -0.5),
        'b_ada': nrm((DEPTH, 6 * d), 0.02),
        'w_in': nrm((n_even, d, IN_WIDTH), d ** -0.5),
        's5_lam_re': -0.5 + nrm(s5_shape, 0.01),
        's5_lam_im': jnp.pi * jnp.arange(S5_STATE, dtype=f32) + nrm(s5_shape, 0.01),
        's5_log_dt': jax.random.uniform(next(keys), (n_even, 2, S5_GROUPS), f32,
                                        math.log(S5_DT_MIN), math.log(S5_DT_MAX)),
        's5_b_re': nrm((n_even, 2, S5_GROUPS, S5_STATE, S5_GROUP), (2 * S5_GROUP) ** -0.5),
        's5_b_im': nrm((n_even, 2, S5_GROUPS, S5_STATE, S5_GROUP), (2 * S5_GROUP) ** -0.5),
        's5_c_re': nrm((n_even, 2, S5_GROUPS, S5_GROUP, S5_STATE), c_scale),
        's5_c_im': nrm((n_even, 2, S5_GROUPS, S5_GROUP, S5_STATE), c_scale),
        's5_d': nrm((n_even, S5_GROUPS, S5_GROUP), 1.0),
        's5_w_glu': nrm((n_even, S5_WIDTH, S5_WIDTH), S5_WIDTH ** -0.5),
        's5_b_glu': nrm((n_even, S5_WIDTH), 0.01),
        'na_q_gain': 1.0 + nrm((n_even, NA_HEAD_DIM), 0.01),
        'na_k_gain': 1.0 + nrm((n_even, NA_HEAD_DIM), 0.01),
        'na_rpb': nrm((n_even, NA_HEADS, 2 * NA_KH_MAX - 1, 2 * NA_KW - 1), 0.1),
        'w_mix_out': nrm((n_even, MIX_WIDTH, d), MIX_WIDTH ** -0.5),
        'w_fourier_out': nrm((n_odd, d, d), d ** -0.5),
        'b_fourier_out': nrm((n_odd, d), 0.01),
        'w_router': nrm((DEPTH, d, N_EXPERTS), d ** -0.5),
        'router_bias': nrm((DEPTH, N_EXPERTS), 0.01),
        'w_exp_gate': nrm((DEPTH, N_EXPERTS, d, EXPERT_FF), d ** -0.5),
        'w_exp_up': nrm((DEPTH, N_EXPERTS, d, EXPERT_FF), d ** -0.5),
        'w_exp_down': nrm((DEPTH, N_EXPERTS, EXPERT_FF, d), EXPERT_FF ** -0.5),
        'w_sh_gate': nrm((DEPTH, d, SHARED_FF), d ** -0.5),
        'w_sh_up': nrm((DEPTH, d, SHARED_FF), d ** -0.5),
        'w_sh_down': nrm((DEPTH, SHARED_FF, d), SHARED_FF ** -0.5),
    }


def reference(x, c, ctx, c_ctx, w_ada, b_ada, w_in, s5_lam_re, s5_lam_im, s5_log_dt,
              s5_b_re, s5_b_im, s5_c_re, s5_c_im, s5_d, s5_w_glu, s5_b_glu, na_q_gain,
              na_k_gain, na_rpb, w_mix_out, w_fourier_out, b_fourier_out, w_router,
              router_bias, w_exp_gate, w_exp_up, w_exp_down, w_sh_gate, w_sh_up, w_sh_down):
    d = x.shape[-1]
    rows = x.shape[1] // GRID_W
    h_l, h_c = x, ctx
    for i in range(DEPTH):
        need_ctx = any(j % 2 == 0 for j in range(i + 1, DEPTH))
        ml = [m[:, None, :] for m in adaln(c, w_ada[i], b_ada[i])]
        a_l = modulate(h_l, ml[0], ml[1])
        if i % 2 == 0 or need_ctx:
            mc = adaln(c_ctx, w_ada[i], b_ada[i])
        if i % 2 == 0:
            e = i // 2
            y_c, y_l = even_mixer(modulate(h_c, mc[0], mc[1]), a_l, rows, w_in[e],
                                  s5_lam_re[e], s5_lam_im[e], s5_log_dt[e], s5_b_re[e],
                                  s5_b_im[e], s5_c_re[e], s5_c_im[e], s5_d[e], s5_w_glu[e],
                                  s5_b_glu[e], na_q_gain[e], na_k_gain[e], na_rpb[e],
                                  w_mix_out[e], need_ctx)
        else:
            o = i // 2
            y_l = fourier_mixer(a_l, w_fourier_out[o], b_fourier_out[o])
            y_c = (fourier_mixer(modulate(h_c, mc[0], mc[1]), w_fourier_out[o], b_fourier_out[o])
                   if need_ctx else None)
        h_l = h_l + ml[2] * y_l
        f_l = modulate(h_l, ml[3], ml[4]).reshape(-1, d)
        moe_w = (w_router[i], router_bias[i], w_exp_gate[i], w_exp_up[i], w_exp_down[i],
                 w_sh_gate[i], w_sh_up[i], w_sh_down[i])
        if need_ctx:
            h_c = h_c + mc[2] * y_c
            f_c = modulate(h_c, mc[3], mc[4]).reshape(-1, d)
            n_c = f_c.shape[0]
            ffn = moe_ffn(jnp.concatenate([f_c, f_l], axis=0), *moe_w)
            h_c = h_c + mc[5] * ffn[:n_c].reshape(h_c.shape)
            h_l = h_l + ml[5] * ffn[n_c:].reshape(h_l.shape)
        else:
            h_l = h_l + ml[5] * moe_ffn(f_l, *moe_w).reshape(h_l.shape)
    return h_l
```

```python
import functools
import math

import numpy as np
import jax
import jax.numpy as jnp
from jax import lax
from jax.experimental import pallas as pl
from jax.experimental.pallas import tpu as pltpu

F32 = jnp.float32
BF16 = jnp.bfloat16
I32 = jnp.int32
HIGHEST = lax.Precision.HIGHEST

GRID_W = 64
NORM_EPS = 1e-6
S5_GROUP = 16
S5_STATE = 64
S5_LAMBDA_RE_MAX = -1e-4
S5_CHUNK = 16
NA_HEADS = 8
NA_HEAD_DIM = 64
NA_KH = 8
NA_KW = 16
FOURIER_GROUPS = 4
N_EXPERT_GROUPS = 8
TOPK_GROUPS = 4
TOP_K = 8
ROUTED_SCALE = 2.5
EXPERT_ROWS = 256
NEG_BIG = -1e30

VMEM_LIMIT_BYTES = 56 * 1024 * 1024


def _params(*sem):
    return pltpu.CompilerParams(dimension_semantics=sem or None,
                                vmem_limit_bytes=VMEM_LIMIT_BYTES)


def _rms(x):
    return x * lax.rsqrt(jnp.mean(x * x, axis=-1, keepdims=True) + NORM_EPS)


def _silu(x):
    return x * jax.nn.sigmoid(x)


def _ada_kernel(c_ref, w_ref, b_ref, o_ref):
    o_ref[0] = jnp.dot(_silu(c_ref[...]), w_ref[0], preferred_element_type=F32,
                       precision=HIGHEST) + b_ref[0]


def adaln_all(cond8, w_ada, b_ada):
    n_layers, d, n6 = w_ada.shape
    tn = n6 // 4
    return pl.pallas_call(
        _ada_kernel,
        grid=(n_layers, n6 // tn),
        in_specs=[pl.BlockSpec((8, d), lambda l, j: (0, 0)),
                  pl.BlockSpec((1, d, tn), lambda l, j: (l, 0, j)),
                  pl.BlockSpec((1, 1, tn), lambda l, j: (l, 0, j))],
        out_specs=pl.BlockSpec((1, 8, tn), lambda l, j: (l, 0, j)),
        out_shape=jax.ShapeDtypeStruct((n_layers, 8, n6), F32),
        compiler_params=_params("parallel", "parallel"),
    )(cond8, w_ada, b_ada.reshape(n_layers, 1, n6))


def _inproj_kernel(x_ref, sh_ref, sc_ref, w_ref, seg_ref, qg_ref, kg_ref,
                   u_ref, q_ref, k_ref, v_ref):
    a = _rms(x_ref[...]) * (1.0 + sc_ref[...]) + sh_ref[...]
    z = jnp.dot(a.astype(BF16), w_ref[...], preferred_element_type=F32)
    s5w = u_ref.shape[1]
    naw = q_ref.shape[1]

    def head_norm(t, gain):
        ss = jnp.dot((t * t).astype(BF16), seg_ref[...], preferred_element_type=F32)
        return t * lax.rsqrt(ss * (1.0 / NA_HEAD_DIM) + NORM_EPS) * gain

    u_ref[...] = z[:, :s5w].astype(BF16)
    q_ref[...] = head_norm(z[:, s5w:s5w + naw], qg_ref[...]).astype(BF16)
    k_ref[...] = head_norm(z[:, s5w + naw:s5w + 2 * naw], kg_ref[...]).astype(BF16)
    v_ref[...] = z[:, s5w + 2 * naw:].astype(BF16)


def in_projection(x, shift, scale, w_in_bf16, seg_ones, q_gain_row, k_gain_row, s5w, naw):
    t, d = x.shape
    tm = min(512, t)
    row = lambda i: (i, 0)
    fixed = lambda i: (0, 0)
    return pl.pallas_call(
        _inproj_kernel,
        grid=(t // tm,),
        in_specs=[pl.BlockSpec((tm, d), row),
                  pl.BlockSpec((1, d), fixed), pl.BlockSpec((1, d), fixed),
                  pl.BlockSpec(w_in_bf16.shape, fixed),
                  pl.BlockSpec(seg_ones.shape, fixed),
                  pl.BlockSpec((1, naw), fixed), pl.BlockSpec((1, naw), fixed)],
        out_specs=[pl.BlockSpec((tm, s5w), row), pl.BlockSpec((tm, naw), row),
                   pl.BlockSpec((tm, naw), row), pl.BlockSpec((tm, naw), row)],
        out_shape=[jax.ShapeDtypeStruct((t, s5w), BF16)] + [jax.ShapeDtypeStruct((t, naw), BF16)] * 3,
        compiler_params=_params("parallel"),
    )(x, shift, scale, w_in_bf16, seg_ones, q_gain_row, k_gain_row)


def s5_matrices(lam_re, lam_im, log_dt, b_re, b_im, c_re, c_im, d_skip):
    L = S5_CHUNK
    taus = jnp.arange(L + 1, dtype=F32)

    def direction(i):
        lam = lax.complex(jnp.minimum(lam_re[i].astype(F32), S5_LAMBDA_RE_MAX), lam_im[i].astype(F32))
        ldt = lam * jnp.exp(log_dt[i].astype(F32))[:, None]
        lam_bar = jnp.exp(ldt)
        b_bar = ((lam_bar - 1.0) / lam)[..., None] * lax.complex(b_re[i].astype(F32), b_im[i].astype(F32))
        cc = lax.complex(c_re[i].astype(F32), c_im[i].astype(F32))
        powers = jnp.exp(ldt[None] * taus[:, None, None])
        resp = jnp.real(jnp.einsum('gcp,tgp,gpd->gtcd', cc, powers[:L], b_bar))
        return powers, b_bar, cc, resp

    pw_f, bb_f, cc_f, k_f = direction(0)
    pw_b, bb_b, cc_b, k_b = direction(1)
    g, p = pw_f.shape[1:]
    c = S5_GROUP
    s_idx = jnp.arange(L)[:, None]
    l_idx = jnp.arange(L)[None, :]
    diff = l_idx - s_idx
    tf = jnp.where((diff >= 0)[None, :, :, None, None], k_f[:, jnp.clip(diff, 0, L - 1)], 0.0)
    tb = jnp.where((diff <= 0)[None, :, :, None, None], k_b[:, jnp.clip(-diff, 0, L - 1)], 0.0)
    skip = (jnp.eye(L, dtype=F32)[None, :, :, None, None]
            * (jnp.eye(c, dtype=F32)[None] * d_skip.astype(F32)[:, :, None])[:, None, None])
    toep = jnp.transpose(tf + tb + skip, (0, 1, 4, 2, 3)).reshape(g, L * c, L * c)

    def state_in(powers_sel, b_bar):
        w = powers_sel[:, :, :, None] * b_bar[None]
        return jnp.transpose(w, (1, 0, 3, 2)).reshape(g, L * c, p)

    wf = state_in(pw_f[L - 1 - jnp.arange(L)], bb_f)
    wb = state_in(pw_b[jnp.arange(L)], bb_b)
    w_state = jnp.concatenate([jnp.real(wf), jnp.imag(wf), jnp.imag(wf), jnp.real(wf),
                               jnp.real(wb), jnp.imag(wb), jnp.imag(wb), jnp.real(wb)], axis=-1)

    def state_out(powers_sel, cc):
        r = powers_sel[:, :, None, :] * cc[None]
        return jnp.transpose(r, (1, 3, 0, 2)).reshape(g, p, L * c)

    rf = state_out(pw_f[1 + jnp.arange(L)], cc_f)
    rb = state_out(pw_b[L - jnp.arange(L)], cc_b)
    r_state = jnp.concatenate([jnp.real(rf), -jnp.imag(rf), jnp.real(rb), -jnp.imag(rb)], axis=1)

    def mult(a):
        ar, ai = jnp.real(a), jnp.imag(a)
        return jnp.stack([jnp.concatenate([ar, ar], -1), jnp.concatenate([-ai, ai], -1),
                          jnp.concatenate([ai, -ai], -1)])

    return toep, w_state, r_state, mult(pw_f[L]), mult(pw_b[L])


def _s5_state_in_kernel(u_ref, w_ref, f1_ref, f2_ref, b1_ref, b2_ref):
    s = jnp.dot(u_ref[0], w_ref[0], preferred_element_type=F32)
    n = f1_ref.shape[2]
    f1_ref[0] = s[:, :n]
    f2_ref[0] = s[:, n:2 * n]
    b1_ref[0] = s[:, 2 * n:3 * n]
    b2_ref[0] = s[:, 3 * n:]


def s5_state_increments(u_chunks, w_state):
    g, nc, lc = u_chunks.shape
    n = w_state.shape[2] // 4
    blk = lambda i: (i, 0, 0)
    out = jax.ShapeDtypeStruct((g, nc, n), F32)
    return pl.pallas_call(
        _s5_state_in_kernel,
        grid=(g,),
        in_specs=[pl.BlockSpec((1, nc, lc), blk), pl.BlockSpec((1, lc, 4 * n), blk)],
        out_specs=[pl.BlockSpec((1, nc, n), blk)] * 4,
        out_shape=[out] * 4,
        compiler_params=_params("parallel"),
    )(u_chunks, w_state)


def _s5_scan_kernel(s1_ref, s2_ref, m_ref, x_ref, v1_ref, v2_ref, *, reverse):
    @pl.when(pl.program_id(0) == 0)
    def _():
        v1_ref[...] = jnp.zeros_like(v1_ref)
        v2_ref[...] = jnp.zeros_like(v2_ref)

    a1, a2, a3 = m_ref[0], m_ref[1], m_ref[2]
    cb = s1_ref.shape[0]

    def body(j, carry):
        v1, v2 = carry
        jj = cb - 1 - j if reverse else j
        x_ref[jj] = v1
        return (a1 * v1 + a2 * v2 + s1_ref[jj], a1 * v2 + a3 * v1 + s2_ref[jj])

    v1, v2 = lax.fori_loop(0, cb, body, (v1_ref[...], v2_ref[...]))
    v1_ref[...] = v1
    v2_ref[...] = v2


def s5_chunk_scan(s1, s2, mult, reverse):
    nc, g, n = s1.shape
    cb = 96 if nc % 96 == 0 else nc
    nb = nc // cb
    blk = (lambda i: (nb - 1 - i, 0, 0)) if reverse else (lambda i: (i, 0, 0))
    return pl.pallas_call(
        functools.partial(_s5_scan_kernel, reverse=reverse),
        grid=(nb,),
        in_specs=[pl.BlockSpec((cb, g, n), blk), pl.BlockSpec((cb, g, n), blk),
                  pl.BlockSpec((3, g, n), lambda i: (0, 0, 0))],
        out_specs=pl.BlockSpec((cb, g, n), blk),
        out_shape=jax.ShapeDtypeStruct((nc, g, n), F32),
        scratch_shapes=[pltpu.VMEM((g, n), F32), pltpu.VMEM((g, n), F32)],
        compiler_params=_params("arbitrary"),
    )(s1, s2, mult)


def _s5_readout_kernel(u_ref, t_ref, xf_ref, xb_ref, r_ref, y_ref, *, first, count):
    u = u_ref[0, first:first + count, :]
    xin = jnp.concatenate([xf_ref[0, first:first + count, :], xb_ref[0, first:first + count, :]], axis=-1)
    y_ref[0] = (jnp.dot(u, t_ref[0], preferred_element_type=F32)
                + jnp.dot(xin.astype(BF16), r_ref[0], preferred_element_type=F32))


def s5_readout(u_chunks, toep, xin_f, xin_b, r_state, first, count):
    g, nc, lc = u_chunks.shape
    n = xin_f.shape[2]
    blk = lambda i: (i, 0, 0)
    return pl.pallas_call(
        functools.partial(_s5_readout_kernel, first=first, count=count),
        grid=(g,),
        in_specs=[pl.BlockSpec((1, nc, lc), blk), pl.BlockSpec((1, lc, lc), blk),
                  pl.BlockSpec((1, nc, n), blk), pl.BlockSpec((1, nc, n), blk),
                  pl.BlockSpec((1, 2 * n, lc), blk)],
        out_specs=pl.BlockSpec((1, count, lc), blk),
        out_shape=jax.ShapeDtypeStruct((g, count, lc), F32),
        compiler_params=_params("parallel"),
    )(u_chunks, toep, xin_f, xin_b, r_state)


def s5_mixer(u_ctx, u_lat, mats):
    toep, w_state, r_state, mult_f, mult_b = mats
    L, c = S5_CHUNK, S5_GROUP
    g = toep.shape[0]
    seq = jnp.concatenate([u_ctx, u_lat, u_ctx], axis=0)
    nc = seq.shape[0] // L
    n_ctx, n_lat = u_ctx.shape[0] // L, u_lat.shape[0] // L
    u_chunks = jnp.transpose(seq.reshape(nc, L, g, c), (2, 0, 1, 3)).reshape(g, nc, L * c)
    incs = s5_state_increments(u_chunks, w_state.astype(BF16))
    f1, f2, b1, b2 = [jnp.transpose(a, (1, 0, 2)) for a in incs]
    xin_f = jnp.transpose(s5_chunk_scan(f1, f2, mult_f, False), (1, 0, 2))
    xin_b = jnp.transpose(s5_chunk_scan(b1, b2, mult_b, True), (1, 0, 2))
    y = s5_readout(u_chunks, toep.astype(BF16), xin_f, xin_b, r_state.astype(BF16), n_ctx, n_lat)
    return jnp.transpose(y.reshape(g, n_lat, L, c), (1, 2, 0, 3)).reshape(n_lat * L, g * c)


def na_bias_table(rpb):
    q_col = np.arange(GRID_W)
    col_start = np.clip(q_col - NA_KW // 2, 0, GRID_W - NA_KW)
    key_col = np.arange(GRID_W)
    off = key_col[None, :] - col_start[:, None]
    valid = (off >= 0) & (off < NA_KW)
    rel_col = np.clip(key_col[None, :] - q_col[:, None] + NA_KW - 1, 0, 2 * NA_KW - 2)
    full = jnp.where(valid[None, None], rpb.astype(F32)[:, :, rel_col], NEG_BIG)
    variants = []
    for d in range(NA_KH):
        rows = [full[:, i - d + NA_KH - 1] for i in range(NA_KH)]
        variants.append(jnp.concatenate(rows, axis=-1))
    return jnp.stack(variants)


def _na_kernel(*refs, scale):
    q_ref = refs[0]
    k_refs = refs[1:1 + NA_KH]
    v_refs = refs[1 + NA_KH:1 + 2 * NA_KH]
    kc_ref, vc_ref, b_ref, o_ref = refs[1 + 2 * NA_KH:]
    nt = (((1,), (1,)), ((), ()))
    for h in range(q_ref.shape[0]):
        q = q_ref[h]
        kk = jnp.concatenate([r[h] for r in k_refs], axis=0)
        vv = jnp.concatenate([r[h] for r in v_refs], axis=0)
        s = lax.dot_general(q, kk, nt, preferred_element_type=F32) * scale + b_ref[0, h]
        sc = lax.dot_general(q, kc_ref[h], nt, preferred_element_type=F32) * scale
        m = jnp.maximum(jnp.max(s, axis=-1, keepdims=True), jnp.max(sc, axis=-1, keepdims=True))
        p = jnp.exp(s - m)
        pc = jnp.exp(sc - m)
        den = jnp.sum(p, axis=-1, keepdims=True) + jnp.sum(pc, axis=-1, keepdims=True)
        o = (jnp.dot(p.astype(BF16), vv, preferred_element_type=F32)
             + jnp.dot(pc.astype(BF16), vc_ref[h], preferred_element_type=F32))
        o_ref[h] = (o / den).astype(o_ref.dtype)


def neighbourhood_attention(q, k, v, k_ctx, v_ctx, bias_table):
    h, t, hd = q.shape
    rows = t // GRID_W
    kh = min(NA_KH, rows)
    assert kh == NA_KH
    first = lambda r: jnp.clip(r - kh // 2, 0, rows - kh)
    row_blk = pl.BlockSpec((h, GRID_W, hd), lambda r: (0, r, 0))
    key_blks = [pl.BlockSpec((h, GRID_W, hd), functools.partial(lambda r, i: (0, first(r) + i, 0), i=i))
                for i in range(kh)]
    ctx_blk = pl.BlockSpec(k_ctx.shape, lambda r: (0, 0, 0))
    bias_blk = pl.BlockSpec((1,) + bias_table.shape[1:], lambda r: (r - first(r), 0, 0, 0))
    return pl.pallas_call(
        functools.partial(_na_kernel, scale=hd ** -0.5),
        grid=(rows,),
        in_specs=[row_blk] + key_blks + key_blks + [ctx_blk, ctx_blk, bias_blk],
        out_specs=row_blk,
        out_shape=jax.ShapeDtypeStruct((h, t, hd), BF16),
        compiler_params=_params("parallel"),
    )(q, *([k] * kh), *([v] * kh), k_ctx, v_ctx, bias_table)


def time_dft_tables(t):
    a_len = 1 << (int(math.log2(t)) // 2)
    b_len = t // a_len
    ka = np.arange(a_len)[:, None]
    tok = b_len * np.arange(a_len)[None, :]
    ang1 = -2.0 * np.pi * ((ka * (tok[None] + np.arange(b_len)[:, None, None])) % t) / t
    stage1 = np.concatenate([np.cos(ang1), np.sin(ang1)], axis=1) / math.sqrt(t)
    ang2 = 2.0 * np.pi * ((np.arange(b_len)[:, None] * np.arange(b_len)[None, :]) % b_len) / b_len
    c2, s2 = np.cos(ang2), np.sin(ang2)
    stage2 = np.block([[c2, s2], [-s2, c2]])
    return jnp.asarray(stage1, BF16), jnp.asarray(stage2, BF16), a_len, b_len


def _time_dft_kernel(x_ref, m_ref, w2_ref, zre_ref, zim_ref, yre_ref, yim_ref, *, a_len, b_len):
    i = pl.program_id(1)
    bb = m_ref.shape[0]

    def stage1(jb, carry):
        b = i * bb + jb
        xb = x_ref[pl.ds(b, a_len, stride=b_len), :]
        y = jnp.dot(m_ref[jb], xb.astype(BF16), preferred_element_type=F32)
        row = pl.multiple_of(b * a_len, a_len)
        yre_ref[pl.ds(row, a_len), :] = y[:a_len]
        yim_ref[pl.ds(row, a_len), :] = y[a_len:]
        return carry

    lax.fori_loop(0, bb, stage1, 0)

    @pl.when(i == pl.num_programs(1) - 1)
    def _():
        def stage2(ka, carry):
            rows = pl.ds(ka, b_len, stride=a_len)
            y = jnp.concatenate([yre_ref[rows, :], yim_ref[rows, :]], axis=0)
            z = jnp.dot(w2_ref[...], y.astype(BF16), preferred_element_type=F32)
            yre_ref[rows, :] = z[:b_len]
            yim_ref[rows, :] = z[b_len:]
            return carry

        lax.fori_loop(0, a_len, stage2, 0)
        zre_ref[...] = yre_ref[...].astype(zre_ref.dtype)
        zim_ref[...] = yim_ref[...].astype(zim_ref.dtype)


def time_dft(x):
    t, d = x.shape
    stage1, stage2, a_len, b_len = time_dft_tables(t)
    lanes = 128
    bb = min(16, b_len)
    out = jax.ShapeDtypeStruct((t, d), BF16)
    return pl.pallas_call(
        functools.partial(_time_dft_kernel, a_len=a_len, b_len=b_len),
        grid=(d // lanes, b_len // bb),
        in_specs=[pl.BlockSpec((t, lanes), lambda j, i: (0, j)),
                  pl.BlockSpec((bb, 2 * a_len, a_len), lambda j, i: (i, 0, 0)),
                  pl.BlockSpec(stage2.shape, lambda j, i: (0, 0))],
        out_specs=[pl.BlockSpec((t, lanes), lambda j, i: (0, j))] * 2,
        out_shape=[out, out],
        scratch_shapes=[pltpu.VMEM((t, lanes), F32), pltpu.VMEM((t, lanes), F32)],
        compiler_params=_params("parallel", "arbitrary"),
    )(x, stage1, stage2)


def channel_dft_tables(c):
    ang = 2.0 * np.pi * ((np.arange(c)[:, None] * np.arange(c)[None, :]) % c) / c
    return (jnp.asarray(np.cos(ang) / math.sqrt(c), BF16), jnp.asarray(np.sin(ang) / math.sqrt(c), BF16))


def _ffn_prologue(h, shf_ref, scf_ref, wr_ref, wsg_ref, wsu_ref, wsd_ref, h_ref, f_ref, lg_ref, shared_ref):
    h_ref[...] = h
    f = _rms(h) * (1.0 + scf_ref[...]) + shf_ref[...]
    f_ref[...] = f
    lg_ref[...] = lax.dot_general(wr_ref[...], f, (((1,), (1,)), ((), ())),
                                  preferred_element_type=F32, precision=HIGHEST)
    fb = f.astype(BF16)
    hid = (_silu(jnp.dot(fb, wsg_ref[...], preferred_element_type=F32))
           * jnp.dot(fb, wsu_ref[...], preferred_element_type=F32))
    shared_ref[...] = jnp.dot(hid.astype(BF16), wsd_ref[...], preferred_element_type=F32).astype(shared_ref.dtype)


def _gelu_tanh(x):
    return 0.5 * x * (1.0 + jnp.tanh(math.sqrt(2.0 / math.pi) * (x + 0.044715 * (x * x * x))))


def _even_post_kernel(y_ref, na_ref, x_ref, wglu_ref, bglu_ref, wo_ref, gm_ref, *rest):
    g = _gelu_tanh(y_ref[...])
    gate = jax.nn.sigmoid(jnp.dot(g.astype(BF16), wglu_ref[...], preferred_element_type=F32) + bglu_ref[...])
    s5 = (g * gate).astype(BF16)
    w = s5.shape[1]
    mix = (jnp.dot(s5, wo_ref[:w, :], preferred_element_type=F32)
           + jnp.dot(na_ref[...], wo_ref[w:, :], preferred_element_type=F32))
    _ffn_prologue(x_ref[...] + gm_ref[...] * mix, *rest)


def _odd_post_kernel(zre_ref, zim_ref, h_ref_in, cc_ref, sc_ref, wf_ref, bf_ref, gm_ref, *rest):
    c = cc_ref.shape[0]
    parts = []
    for grp in range(zre_ref.shape[1] // c):
        cols = slice(grp * c, (grp + 1) * c)
        parts.append(jnp.dot(zre_ref[:, cols], cc_ref[...], preferred_element_type=F32)
                     + jnp.dot(zim_ref[:, cols], sc_ref[...], preferred_element_type=F32))
    fr = jnp.concatenate(parts, axis=-1).astype(BF16)
    mix = jnp.dot(fr, wf_ref[...], preferred_element_type=F32) + bf_ref[...]
    _ffn_prologue(h_ref_in[...] + gm_ref[...] * mix, *rest)


def _post_call(body, row_inputs, fixed_inputs, t, d, n_exp):
    tm = min(256, t)
    row = lambda i: (i, 0)
    fixed = lambda i: (0, 0)
    in_specs = ([pl.BlockSpec((tm, a.shape[1]), row) for a in row_inputs]
                + [pl.BlockSpec(a.shape, fixed) for a in fixed_inputs])
    return pl.pallas_call(
        body,
        grid=(t // tm,),
        in_specs=in_specs,
        out_specs=[pl.BlockSpec((tm, d), row), pl.BlockSpec((tm, d), row),
                   pl.BlockSpec((n_exp, tm), lambda i: (0, i)), pl.BlockSpec((tm, d), row)],
        out_shape=[jax.ShapeDtypeStruct((t, d), F32), jax.ShapeDtypeStruct((t, d), F32),
                   jax.ShapeDtypeStruct((n_exp, t), F32), jax.ShapeDtypeStruct((t, d), BF16)],
        compiler_params=_params("parallel"),
    )(*row_inputs, *fixed_inputs)


def _route_kernel(lg_ref, bias_ref, tri_ref, idx_ref, gate_ref, rank_ref, cnt_ref, run_ref):
    @pl.when(pl.program_id(0) == 0)
    def _():
        run_ref[...] = jnp.zeros_like(run_ref)

    scores = jax.nn.sigmoid(lg_ref[...])
    n_exp, tb = scores.shape
    sel = scores + bias_ref[...]
    gsz = n_exp // N_EXPERT_GROUPS
    member = lax.broadcasted_iota(I32, (gsz, tb), 0)
    gscore = []
    for grp in range(N_EXPERT_GROUPS):
        xg = sel[grp * gsz:(grp + 1) * gsz, :]
        m1 = jnp.max(xg, axis=0, keepdims=True)
        first = jnp.min(jnp.where(xg == m1, member, gsz), axis=0, keepdims=True)
        m2 = jnp.max(jnp.where(member == first, -jnp.inf, xg), axis=0, keepdims=True)
        gscore.append(m1 + m2)
    keep_rows = []
    for grp in range(N_EXPERT_GROUPS):
        beaten = jnp.zeros((1, tb), F32)
        for other in range(N_EXPERT_GROUPS):
            if other == grp:
                continue
            wins = (gscore[other] >= gscore[grp]) if other < grp else (gscore[other] > gscore[grp])
            beaten = beaten + jnp.where(wins, 1.0, 0.0)
        keep_rows.append(jnp.broadcast_to(beaten < TOPK_GROUPS, (gsz, tb)))
    masked = jnp.where(jnp.concatenate(keep_rows, axis=0), sel, -jnp.inf)

    expert = lax.broadcasted_iota(I32, (n_exp, tb), 0)
    picks, gates, hots = [], [], []
    chosen = jnp.zeros((n_exp, tb), F32)
    for _ in range(TOP_K):
        m = jnp.max(masked, axis=0, keepdims=True)
        pick = jnp.min(jnp.where(masked == m, expert, n_exp), axis=0, keepdims=True)
        hot = expert == pick
        picks.append(pick)
        hots.append(hot)
        gates.append(jnp.sum(jnp.where(hot, scores, 0.0), axis=0, keepdims=True))
        chosen = jnp.where(hot, 1.0, chosen)
        masked = jnp.where(hot, -jnp.inf, masked)
    total = gates[0]
    for gk in gates[1:]:
        total = total + gk
    ahead = jnp.dot(chosen.astype(BF16), tri_ref[...], preferred_element_type=F32) + run_ref[...]
    for k in range(TOP_K):
        idx_ref[k:k + 1, :] = picks[k]
        gate_ref[k:k + 1, :] = ROUTED_SCALE * gates[k] / total
        rank_ref[k:k + 1, :] = jnp.sum(jnp.where(hots[k], ahead, 0.0), axis=0, keepdims=True).astype(I32)
    run_ref[...] = run_ref[...] + jnp.sum(chosen, axis=1, keepdims=True)
    cnt_ref[...] = jnp.broadcast_to(run_ref[...], cnt_ref.shape)


def route(logits_t, router_bias):
    n_exp, t = logits_t.shape
    tb = min(512, t)
    tri = jnp.asarray(np.triu(np.ones((tb, tb), np.float32), k=1), BF16)
    tok = lambda i: (0, i)
    idx, gate, rank, cnt = pl.pallas_call(
        _route_kernel,
        grid=(t // tb,),
        in_specs=[pl.BlockSpec((n_exp, tb), tok), pl.BlockSpec((n_exp, 1), lambda i: (0, 0)),
                  pl.BlockSpec((tb, tb), lambda i: (0, 0))],
        out_specs=[pl.BlockSpec((TOP_K, tb), tok)] * 3 + [pl.BlockSpec((n_exp, 128), lambda i: (0, 0))],
        out_shape=[jax.ShapeDtypeStruct((TOP_K, t), I32), jax.ShapeDtypeStruct((TOP_K, t), F32),
                   jax.ShapeDtypeStruct((TOP_K, t), I32), jax.ShapeDtypeStruct((n_exp, 128), F32)],
        scratch_shapes=[pltpu.VMEM((n_exp, 1), F32)],
        compiler_params=_params("arbitrary"),
    )(logits_t, router_bias.astype(F32).reshape(n_exp, 1), tri)
    return idx, gate, rank, cnt[:, 0].astype(I32)


def dispatch_plan(idx, rank, counts, n_blocks):
    n_exp = counts.shape[0]
    padded = (counts + EXPERT_ROWS - 1) // EXPERT_ROWS * EXPERT_ROWS
    pad_end = jnp.cumsum(padded)
    dest = (pad_end - padded)[idx] + rank
    n_valid = (pad_end[-1] // EXPERT_ROWS).astype(I32)
    blk = jnp.arange(n_blocks, dtype=I32)
    blk_e = jnp.minimum(jnp.searchsorted(pad_end, blk * EXPERT_ROWS, side='right'), n_exp - 1).astype(I32)
    blk_e = jnp.where(blk < n_valid, blk_e, blk_e[jnp.maximum(n_valid - 1, 0)])
    return dest.astype(I32), blk_e, n_valid.reshape(1)


def _token_tiles(dest, tm):
    k, t = dest.shape
    return jnp.transpose(dest.reshape(k, t // tm, tm), (1, 0, 2))


def _dispatch_kernel(dest_ref, f_ref, rows_in, rows_out, sem):
    del rows_in
    tm = f_ref.shape[0]

    def body(n, carry):
        for k in range(TOP_K):
            pltpu.make_async_copy(f_ref.at[pl.ds(n, 1), :],
                                  rows_out.at[pl.ds(dest_ref[0, k, n], 1), :], sem).start()
        return carry

    lax.fori_loop(0, tm, body, 0)
    for k in range(TOP_K):
        pltpu.make_async_copy(f_ref, rows_out.at[pl.ds(0, tm), :], sem).wait()


def dispatch(f, dest, cap):
    t, d = f.shape
    tm = min(256, t)
    return pl.pallas_call(
        _dispatch_kernel,
        grid=(t // tm,),
        in_specs=[pl.BlockSpec((1, TOP_K, tm), lambda i: (i, 0, 0), memory_space=pltpu.SMEM),
                  pl.BlockSpec((tm, d), lambda i: (i, 0)),
                  pl.BlockSpec(memory_space=pl.ANY)],
        out_specs=pl.BlockSpec(memory_space=pl.ANY),
        out_shape=jax.ShapeDtypeStruct((cap, d), f.dtype),
        scratch_shapes=[pltpu.SemaphoreType.DMA(())],
        input_output_aliases={2: 0},
        compiler_params=_params("arbitrary"),
    )(_token_tiles(dest, tm), f, jnp.zeros((cap, d), f.dtype))


def _experts_kernel(be_ref, nv_ref, x_ref, wg_ref, wu_ref, wd_ref, y_ref):
    del be_ref

    @pl.when(pl.program_id(0) < nv_ref[0])
    def _():
        x = x_ref[...].astype(BF16)
        hid = (_silu(jnp.dot(x, wg_ref[0], preferred_element_type=F32))
               * jnp.dot(x, wu_ref[0], preferred_element_type=F32))
        y_ref[...] = jnp.dot(hid.astype(BF16), wd_ref[0], preferred_element_type=F32)

    @pl.when(pl.program_id(0) >= nv_ref[0])
    def _():
        y_ref[...] = jnp.zeros_like(y_ref)


def experts(rows, blk_e, n_valid, w_gate, w_up, w_down):
    cap, d = rows.shape
    ff = w_gate.shape[2]
    n_blocks = cap // EXPERT_ROWS
    live = lambda b, be, nv: (jnp.minimum(b, nv[0] - 1), 0)
    return pl.pallas_call(
        _experts_kernel,
        grid_spec=pltpu.PrefetchScalarGridSpec(
            num_scalar_prefetch=2,
            grid=(n_blocks,),
            in_specs=[pl.BlockSpec((EXPERT_ROWS, d), live),
                      pl.BlockSpec((1, d, ff), lambda b, be, nv: (be[b], 0, 0)),
                      pl.BlockSpec((1, d, ff), lambda b, be, nv: (be[b], 0, 0)),
                      pl.BlockSpec((1, ff, d), lambda b, be, nv: (be[b], 0, 0))],
            out_specs=pl.BlockSpec((EXPERT_ROWS, d), lambda b, be, nv: (b, 0)),
        ),
        out_shape=jax.ShapeDtypeStruct((cap, d), F32),
        compiler_params=_params("arbitrary"),
    )(blk_e, n_valid, rows, w_gate, w_up, w_down)


def _combine_kernel(dest_ref, gate_ref, h_ref, shared_ref, gf_ref, sh_ref, sc_ref, y_hbm,
                    h_out, a_out, ybuf, sem):
    tm = h_ref.shape[0]

    def body(n, carry):
        for k in range(TOP_K):
            pltpu.make_async_copy(y_hbm.at[pl.ds(dest_ref[0, k, n], 1), :],
                                  ybuf.at[k, pl.ds(n, 1), :], sem).start()
        return carry

    lax.fori_loop(0, tm, body, 0)
    for k in range(TOP_K):
        pltpu.make_async_copy(y_hbm.at[pl.ds(0, tm), :], ybuf.at[k], sem).wait()
    routed = gate_ref[:, 0:1] * ybuf[0]
    for k in range(1, TOP_K):
        routed = routed + gate_ref[:, k:k + 1] * ybuf[k]
    h = h_ref[...] + gf_ref[...] * (routed + shared_ref[...].astype(F32))
    h_out[...] = h
    a_out[...] = _rms(h) * (1.0 + sc_ref[...]) + sh_ref[...]


def combine(y_rows, dest, gate_tk, h, shared, gate_ffn, next_shift, next_scale):
    t, d = h.shape
    tm = min(32, t)
    row = lambda i: (i, 0)
    fixed = lambda i: (0, 0)
    return pl.pallas_call(
        _combine_kernel,
        grid=(t // tm,),
        in_specs=[pl.BlockSpec((1, TOP_K, tm), lambda i: (i, 0, 0), memory_space=pltpu.SMEM),
                  pl.BlockSpec((tm, TOP_K), row), pl.BlockSpec((tm, d), row), pl.BlockSpec((tm, d), row),
                  pl.BlockSpec((1, d), fixed), pl.BlockSpec((1, d), fixed), pl.BlockSpec((1, d), fixed),
                  pl.BlockSpec(memory_space=pl.ANY)],
        out_specs=[pl.BlockSpec((tm, d), row), pl.BlockSpec((tm, d), row)],
        out_shape=[jax.ShapeDtypeStruct((t, d), F32), jax.ShapeDtypeStruct((t, d), F32)],
        scratch_shapes=[pltpu.VMEM((TOP_K, tm, d), F32), pltpu.SemaphoreType.DMA(())],
        compiler_params=_params("arbitrary"),
    )(_token_tiles(dest, tm), gate_tk, h, shared, gate_ffn, next_shift, next_scale, y_rows)


def moe_tail(h, f, logits_t, shared, router_bias, w_gate, w_up, w_down, gate_ffn, next_shift, next_scale):
    t, d = h.shape
    n_exp = w_gate.shape[0]
    n_blocks = -(-(t * TOP_K + n_exp * (EXPERT_ROWS - 1)) // EXPERT_ROWS)
    idx, gate, rank, counts = route(logits_t, router_bias)
    dest, blk_e, n_valid = dispatch_plan(idx, rank, counts, n_blocks)
    rows = dispatch(f, dest, n_blocks * EXPERT_ROWS)
    y_rows = experts(rows, blk_e, n_valid, w_gate.astype(BF16), w_up.astype(BF16), w_down.astype(BF16))
    return combine(y_rows, dest, gate.T, h, shared, gate_ffn, next_shift, next_scale)


def kernel(x, c, ctx, c_ctx, w_ada, b_ada, w_in, s5_lam_re, s5_lam_im, s5_log_dt, s5_b_re, s5_b_im,
           s5_c_re, s5_c_im, s5_d, s5_w_glu, s5_b_glu, na_q_gain, na_k_gain, na_rpb, w_mix_out,
           w_fourier_out, b_fourier_out, w_router, router_bias, w_exp_gate, w_exp_up, w_exp_down,
           w_sh_gate, w_sh_up, w_sh_down):
    bsz, t, d = x.shape
    assert bsz == 1 and w_ada.shape[0] == 2
    n_exp = w_router.shape[2]
    s5w = s5_w_glu.shape[1]
    naw = w_in.shape[2] - s5w
    naw //= 3
    heads = naw // NA_HEAD_DIM

    cond8 = jnp.zeros((8, d), F32).at[0].set(c[0].astype(F32)).at[1].set(c_ctx.astype(F32))
    ada = adaln_all(cond8, w_ada, b_ada)
    mod = lambda layer, who, j: ada[layer, who:who + 1, j * d:(j + 1) * d]

    def ffn_weights(i):
        return (mod(i, 0, 3), mod(i, 0, 4), jnp.transpose(w_router[i]).astype(F32),
                w_sh_gate[i].astype(BF16), w_sh_up[i].astype(BF16), w_sh_down[i].astype(BF16))

    h0 = x[0]
    seg = jnp.asarray(np.kron(np.eye(heads), np.ones((NA_HEAD_DIM, NA_HEAD_DIM))), BF16)
    w_in_b = w_in[0].astype(BF16)
    qg = jnp.tile(na_q_gain[0].astype(F32), heads)[None]
    kg = jnp.tile(na_k_gain[0].astype(F32), heads)[None]
    u_c, _, k_c, v_c = in_projection(ctx[0], mod(0, 1, 0), mod(0, 1, 1), w_in_b, seg, qg, kg, s5w, naw)
    u_l, q_l, k_l, v_l = in_projection(h0, mod(0, 0, 0), mod(0, 0, 1), w_in_b, seg, qg, kg, s5w, naw)
    mats = s5_matrices(s5_lam_re[0], s5_lam_im[0], s5_log_dt[0], s5_b_re[0], s5_b_im[0],
                       s5_c_re[0], s5_c_im[0], s5_d[0])
    y_s5 = s5_mixer(u_c, u_l, mats)
    head_major = lambda a: jnp.transpose(a.reshape(a.shape[0], heads, NA_HEAD_DIM), (1, 0, 2))
    na = neighbourhood_attention(head_major(q_l), head_major(k_l), head_major(v_l),
                                 head_major(k_c), head_major(v_c), na_bias_table(na_rpb[0]))
    na = jnp.transpose(na, (1, 0, 2)).reshape(t, naw)
    h1, f1, lg1, sh1 = _post_call(
        _even_post_kernel, [y_s5, na, h0],
        [s5_w_glu[0].astype(BF16), s5_b_glu[0].astype(F32)[None], w_mix_out[0].astype(BF16), mod(0, 0, 2),
         *ffn_weights(0)], t, d, n_exp)
    h2, a1 = moe_tail(h1, f1, lg1, sh1, router_bias[0], w_exp_gate[0], w_exp_up[0], w_exp_down[0],
                      mod(0, 0, 5), mod(1, 0, 0), mod(1, 0, 1))

    zre, zim = time_dft(a1)
    cc, sc = channel_dft_tables(d // FOURIER_GROUPS)
    h3, f3, lg3, sh3 = _post_call(
        _odd_post_kernel, [zre, zim, h2],
        [cc, sc, w_fourier_out[0].astype(BF16), b_fourier_out[0].astype(F32)[None], mod(1, 0, 2),
         *ffn_weights(1)], t, d, n_exp)
    zero_row = jnp.zeros((1, d), F32)
    out, _ = moe_tail(h3, f3, lg3, sh3, router_bias[1], w_exp_gate[1], w_exp_up[1], w_exp_down[1],
                      mod(1, 0, 5), zero_row, zero_row)
    return out[None]
```

```python
import functools
import math

import numpy as np
import jax
import jax.numpy as jnp
from jax import lax
from jax.experimental import pallas as pl
from jax.experimental.pallas import tpu as pltpu

F32 = jnp.float32
BF16 = jnp.bfloat16
I32 = jnp.int32
U32 = jnp.uint32
HIGHEST = lax.Precision.HIGHEST

GRID_W = 64
NORM_EPS = 1e-6
S5_GROUP = 16
S5_STATE = 64
S5_LAMBDA_RE_MAX = -1e-4
S5_CHUNK = 16
NA_HEADS = 8
NA_HEAD_DIM = 64
NA_KH = 8
NA_KW = 16
FOURIER_GROUPS = 4
N_EXPERT_GROUPS = 8
TOPK_GROUPS = 4
TOP_K = 8
ROUTED_SCALE = 2.5
EXPERT_ROWS = 256
NEG_BIG = -1e30

VMEM_LIMIT_BYTES = 56 * 1024 * 1024


def _params(*sem):
    return pltpu.CompilerParams(dimension_semantics=sem or None,
                                vmem_limit_bytes=VMEM_LIMIT_BYTES)


def _rms(x):
    return x * lax.rsqrt(jnp.mean(x * x, axis=-1, keepdims=True) + NORM_EPS)


def _silu(x):
    return x * jax.nn.sigmoid(x)


def _pack_rows(x):
    n = x.shape[1] // 2
    lo = lax.bitcast_convert_type(x[:, :n].astype(BF16).astype(F32), U32)
    hi = lax.bitcast_convert_type(x[:, n:].astype(BF16).astype(F32), U32)
    return (hi & jnp.uint32(0xFFFF0000)) | (lo >> 16)


def _unpack_rows(w):
    lo = lax.bitcast_convert_type(w << 16, F32)
    hi = lax.bitcast_convert_type(w & jnp.uint32(0xFFFF0000), F32)
    return jnp.concatenate([lo, hi], axis=1)


def _ada_kernel(c_ref, w_ref, b_ref, o_ref):
    o_ref[0] = jnp.dot(_silu(c_ref[...]), w_ref[0], preferred_element_type=F32,
                       precision=HIGHEST) + b_ref[0]


def adaln_all(cond8, w_ada, b_ada):
    n_layers, d, n6 = w_ada.shape
    tn = n6 // 4
    return pl.pallas_call(
        _ada_kernel,
        grid=(n_layers, n6 // tn),
        in_specs=[pl.BlockSpec((8, d), lambda l, j: (0, 0)),
                  pl.BlockSpec((1, d, tn), lambda l, j: (l, 0, j)),
                  pl.BlockSpec((1, 1, tn), lambda l, j: (l, 0, j))],
        out_specs=pl.BlockSpec((1, 8, tn), lambda l, j: (l, 0, j)),
        out_shape=jax.ShapeDtypeStruct((n_layers, 8, n6), F32),
        compiler_params=_params("parallel", "parallel"),
    )(cond8, w_ada, b_ada.reshape(n_layers, 1, n6))


def _inproj_kernel(x_ref, sh_ref, sc_ref, w_ref, seg_ref, qg_ref, kg_ref,
                   u_ref, q_ref, k_ref, v_ref):
    a = _rms(x_ref[...]) * (1.0 + sc_ref[...]) + sh_ref[...]
    z = jnp.dot(a.astype(BF16), w_ref[...], preferred_element_type=F32)
    s5w = u_ref.shape[1]
    naw = q_ref.shape[1]

    def head_norm(t, gain):
        ss = jnp.dot((t * t).astype(BF16), seg_ref[...], preferred_element_type=F32)
        return t * lax.rsqrt(ss * (1.0 / NA_HEAD_DIM) + NORM_EPS) * gain

    u_ref[...] = z[:, :s5w].astype(BF16)
    q_ref[...] = head_norm(z[:, s5w:s5w + naw], qg_ref[...]).astype(BF16)
    k_ref[...] = head_norm(z[:, s5w + naw:s5w + 2 * naw], kg_ref[...]).astype(BF16)
    v_ref[...] = z[:, s5w + 2 * naw:].astype(BF16)


def in_projection(x, shift, scale, w_in_bf16, seg_ones, q_gain_row, k_gain_row, s5w, naw):
    t, d = x.shape
    tm = min(512, t)
    row = lambda i: (i, 0)
    fixed = lambda i: (0, 0)
    return pl.pallas_call(
        _inproj_kernel,
        grid=(t // tm,),
        in_specs=[pl.BlockSpec((tm, d), row),
                  pl.BlockSpec((1, d), fixed), pl.BlockSpec((1, d), fixed),
                  pl.BlockSpec(w_in_bf16.shape, fixed),
                  pl.BlockSpec(seg_ones.shape, fixed),
                  pl.BlockSpec((1, naw), fixed), pl.BlockSpec((1, naw), fixed)],
        out_specs=[pl.BlockSpec((tm, s5w), row), pl.BlockSpec((tm, naw), row),
                   pl.BlockSpec((tm, naw), row), pl.BlockSpec((tm, naw), row)],
        out_shape=[jax.ShapeDtypeStruct((t, s5w), BF16)] + [jax.ShapeDtypeStruct((t, naw), BF16)] * 3,
        compiler_params=_params("parallel"),
    )(x, shift, scale, w_in_bf16, seg_ones, q_gain_row, k_gain_row)


def s5_matrices(lam_re, lam_im, log_dt, b_re, b_im, c_re, c_im, d_skip):
    L = S5_CHUNK
    taus = jnp.arange(L + 1, dtype=F32)

    def direction(i):
        lam = lax.complex(jnp.minimum(lam_re[i].astype(F32), S5_LAMBDA_RE_MAX), lam_im[i].astype(F32))
        ldt = lam * jnp.exp(log_dt[i].astype(F32))[:, None]
        lam_bar = jnp.exp(ldt)
        b_bar = ((lam_bar - 1.0) / lam)[..., None] * lax.complex(b_re[i].astype(F32), b_im[i].astype(F32))
        cc = lax.complex(c_re[i].astype(F32), c_im[i].astype(F32))
        powers = jnp.exp(ldt[None] * taus[:, None, None])
        resp = jnp.real(jnp.einsum('gcp,tgp,gpd->gtcd', cc, powers[:L], b_bar))
        return powers, b_bar, cc, resp

    pw_f, bb_f, cc_f, k_f = direction(0)
    pw_b, bb_b, cc_b, k_b = direction(1)
    g, p = pw_f.shape[1:]
    c = S5_GROUP
    s_idx = jnp.arange(L)[:, None]
    l_idx = jnp.arange(L)[None, :]
    diff = l_idx - s_idx
    tf = jnp.where((diff >= 0)[None, :, :, None, None], k_f[:, jnp.clip(diff, 0, L - 1)], 0.0)
    tb = jnp.where((diff <= 0)[None, :, :, None, None], k_b[:, jnp.clip(-diff, 0, L - 1)], 0.0)
    skip = (jnp.eye(L, dtype=F32)[None, :, :, None, None]
            * (jnp.eye(c, dtype=F32)[None] * d_skip.astype(F32)[:, :, None])[:, None, None])
    toep = jnp.transpose(tf + tb + skip, (0, 1, 4, 2, 3)).reshape(g, L * c, L * c)

    def state_in(powers_sel, b_bar):
        w = powers_sel[:, :, :, None] * b_bar[None]
        return jnp.transpose(w, (1, 0, 3, 2)).reshape(g, L * c, p)

    wf = state_in(pw_f[L - 1 - jnp.arange(L)], bb_f)
    wb = state_in(pw_b[jnp.arange(L)], bb_b)
    w_state = jnp.concatenate([jnp.real(wf), jnp.imag(wf), jnp.imag(wf), jnp.real(wf),
                               jnp.real(wb), jnp.imag(wb), jnp.imag(wb), jnp.real(wb)], axis=-1)

    def state_out(powers_sel, cc):
        r = powers_sel[:, :, None, :] * cc[None]
        return jnp.transpose(r, (1, 3, 0, 2)).reshape(g, p, L * c)

    rf = state_out(pw_f[1 + jnp.arange(L)], cc_f)
    rb = state_out(pw_b[L - jnp.arange(L)], cc_b)
    r_state = jnp.concatenate([jnp.real(rf), -jnp.imag(rf), jnp.real(rb), -jnp.imag(rb)], axis=1)

    def mult(a):
        ar, ai = jnp.real(a), jnp.imag(a)
        return jnp.stack([jnp.concatenate([ar, ar], -1), jnp.concatenate([-ai, ai], -1),
                          jnp.concatenate([ai, -ai], -1)])

    return toep, w_state, r_state, mult(pw_f[L]), mult(pw_b[L])


def _s5_state_in_kernel(u_ref, w_ref, f1_ref, f2_ref, b1_ref, b2_ref):
    s = jnp.dot(u_ref[0], w_ref[0], preferred_element_type=F32)
    n = f1_ref.shape[2]
    f1_ref[0] = s[:, :n]
    f2_ref[0] = s[:, n:2 * n]
    b1_ref[0] = s[:, 2 * n:3 * n]
    b2_ref[0] = s[:, 3 * n:]


def s5_state_increments(u_chunks, w_state):
    g, nc, lc = u_chunks.shape
    n = w_state.shape[2] // 4
    blk = lambda i: (i, 0, 0)
    out = jax.ShapeDtypeStruct((g, nc, n), F32)
    return pl.pallas_call(
        _s5_state_in_kernel,
        grid=(g,),
        in_specs=[pl.BlockSpec((1, nc, lc), blk), pl.BlockSpec((1, lc, 4 * n), blk)],
        out_specs=[pl.BlockSpec((1, nc, n), blk)] * 4,
        out_shape=[out] * 4,
        compiler_params=_params("parallel"),
    )(u_chunks, w_state)


def _s5_scan_kernel(s1_ref, s2_ref, m_ref, x_ref, v1_ref, v2_ref, *, reverse):
    @pl.when(pl.program_id(0) == 0)
    def _():
        v1_ref[...] = jnp.zeros_like(v1_ref)
        v2_ref[...] = jnp.zeros_like(v2_ref)

    a1, a2, a3 = m_ref[0], m_ref[1], m_ref[2]
    cb = s1_ref.shape[0]

    def body(j, carry):
        v1, v2 = carry
        jj = cb - 1 - j if reverse else j
        x_ref[jj] = v1
        return (a1 * v1 + a2 * v2 + s1_ref[jj], a1 * v2 + a3 * v1 + s2_ref[jj])

    v1, v2 = lax.fori_loop(0, cb, body, (v1_ref[...], v2_ref[...]))
    v1_ref[...] = v1
    v2_ref[...] = v2


def s5_chunk_scan(s1, s2, mult, reverse):
    nc, g, n = s1.shape
    cb = 96 if nc % 96 == 0 else nc
    nb = nc // cb
    blk = (lambda i: (nb - 1 - i, 0, 0)) if reverse else (lambda i: (i, 0, 0))
    return pl.pallas_call(
        functools.partial(_s5_scan_kernel, reverse=reverse),
        grid=(nb,),
        in_specs=[pl.BlockSpec((cb, g, n), blk), pl.BlockSpec((cb, g, n), blk),
                  pl.BlockSpec((3, g, n), lambda i: (0, 0, 0))],
        out_specs=pl.BlockSpec((cb, g, n), blk),
        out_shape=jax.ShapeDtypeStruct((nc, g, n), F32),
        scratch_shapes=[pltpu.VMEM((g, n), F32), pltpu.VMEM((g, n), F32)],
        compiler_params=_params("arbitrary"),
    )(s1, s2, mult)


def _s5_readout_kernel(u_ref, t_ref, xf_ref, xb_ref, r_ref, y_ref, *, first, count):
    u = u_ref[0, first:first + count, :]
    xin = jnp.concatenate([xf_ref[0, first:first + count, :], xb_ref[0, first:first + count, :]], axis=-1)
    y_ref[0] = (jnp.dot(u, t_ref[0], preferred_element_type=F32)
                + jnp.dot(xin.astype(BF16), r_ref[0], preferred_element_type=F32))


def s5_readout(u_chunks, toep, xin_f, xin_b, r_state, first, count):
    g, nc, lc = u_chunks.shape
    n = xin_f.shape[2]
    blk = lambda i: (i, 0, 0)
    return pl.pallas_call(
        functools.partial(_s5_readout_kernel, first=first, count=count),
        grid=(g,),
        in_specs=[pl.BlockSpec((1, nc, lc), blk), pl.BlockSpec((1, lc, lc), blk),
                  pl.BlockSpec((1, nc, n), blk), pl.BlockSpec((1, nc, n), blk),
                  pl.BlockSpec((1, 2 * n, lc), blk)],
        out_specs=pl.BlockSpec((1, count, lc), blk),
        out_shape=jax.ShapeDtypeStruct((g, count, lc), F32),
        compiler_params=_params("parallel"),
    )(u_chunks, toep, xin_f, xin_b, r_state)


def s5_mixer(u_ctx, u_lat, mats):
    toep, w_state, r_state, mult_f, mult_b = mats
    L, c = S5_CHUNK, S5_GROUP
    g = toep.shape[0]
    seq = jnp.concatenate([u_ctx, u_lat, u_ctx], axis=0)
    nc = seq.shape[0] // L
    n_ctx, n_lat = u_ctx.shape[0] // L, u_lat.shape[0] // L
    u_chunks = jnp.transpose(seq.reshape(nc, L, g, c), (2, 0, 1, 3)).reshape(g, nc, L * c)
    incs = s5_state_increments(u_chunks, w_state.astype(BF16))
    f1, f2, b1, b2 = [jnp.transpose(a, (1, 0, 2)) for a in incs]
    xin_f = jnp.transpose(s5_chunk_scan(f1, f2, mult_f, False), (1, 0, 2))
    xin_b = jnp.transpose(s5_chunk_scan(b1, b2, mult_b, True), (1, 0, 2))
    y = s5_readout(u_chunks, toep.astype(BF16), xin_f, xin_b, r_state.astype(BF16), n_ctx, n_lat)
    return jnp.transpose(y.reshape(g, n_lat, L, c), (1, 2, 0, 3)).reshape(n_lat * L, g * c)


def na_bias_table(rpb):
    q_col = np.arange(GRID_W)
    col_start = np.clip(q_col - NA_KW // 2, 0, GRID_W - NA_KW)
    key_col = np.arange(GRID_W)
    off = key_col[None, :] - col_start[:, None]
    valid = (off >= 0) & (off < NA_KW)
    rel_col = np.clip(key_col[None, :] - q_col[:, None] + NA_KW - 1, 0, 2 * NA_KW - 2)
    full = jnp.where(valid[None, None], rpb.astype(F32)[:, :, rel_col], NEG_BIG)
    variants = []
    for d in range(NA_KH):
        rows = [full[:, i - d + NA_KH - 1] for i in range(NA_KH)]
        variants.append(jnp.concatenate(rows, axis=-1))
    return jnp.stack(variants)


def _na_kernel(*refs, scale):
    q_ref = refs[0]
    k_refs = refs[1:1 + NA_KH]
    v_refs = refs[1 + NA_KH:1 + 2 * NA_KH]
    kc_ref, vc_ref, b_ref, o_ref = refs[1 + 2 * NA_KH:]
    nt = (((1,), (1,)), ((), ()))
    for h in range(q_ref.shape[0]):
        q = q_ref[h]
        kk = jnp.concatenate([r[h] for r in k_refs], axis=0)
        vv = jnp.concatenate([r[h] for r in v_refs], axis=0)
        s = lax.dot_general(q, kk, nt, preferred_element_type=F32) * scale + b_ref[0, h]
        sc = lax.dot_general(q, kc_ref[h], nt, preferred_element_type=F32) * scale
        m = jnp.maximum(jnp.max(s, axis=-1, keepdims=True), jnp.max(sc, axis=-1, keepdims=True))
        p = jnp.exp(s - m)
        pc = jnp.exp(sc - m)
        den = jnp.sum(p, axis=-1, keepdims=True) + jnp.sum(pc, axis=-1, keepdims=True)
        o = (jnp.dot(p.astype(BF16), vv, preferred_element_type=F32)
             + jnp.dot(pc.astype(BF16), vc_ref[h], preferred_element_type=F32))
        o_ref[h] = (o / den).astype(o_ref.dtype)


def neighbourhood_attention(q, k, v, k_ctx, v_ctx, bias_table):
    h, t, hd = q.shape
    rows = t // GRID_W
    kh = min(NA_KH, rows)
    assert kh == NA_KH
    first = lambda r: jnp.clip(r - kh // 2, 0, rows - kh)
    row_blk = pl.BlockSpec((h, GRID_W, hd), lambda r: (0, r, 0))
    key_blks = [pl.BlockSpec((h, GRID_W, hd), functools.partial(lambda r, i: (0, first(r) + i, 0), i=i))
                for i in range(kh)]
    ctx_blk = pl.BlockSpec(k_ctx.shape, lambda r: (0, 0, 0))
    bias_blk = pl.BlockSpec((1,) + bias_table.shape[1:], lambda r: (r - first(r), 0, 0, 0))
    return pl.pallas_call(
        functools.partial(_na_kernel, scale=hd ** -0.5),
        grid=(rows,),
        in_specs=[row_blk] + key_blks + key_blks + [ctx_blk, ctx_blk, bias_blk],
        out_specs=row_blk,
        out_shape=jax.ShapeDtypeStruct((h, t, hd), BF16),
        compiler_params=_params("parallel"),
    )(q, *([k] * kh), *([v] * kh), k_ctx, v_ctx, bias_table)


def time_dft_tables(t):
    a_len = 1 << (int(math.log2(t)) // 2)
    b_len = t // a_len
    ka = np.arange(a_len)[:, None]
    tok = b_len * np.arange(a_len)[None, :]
    ang1 = -2.0 * np.pi * ((ka * (tok[None] + np.arange(b_len)[:, None, None])) % t) / t
    stage1 = np.concatenate([np.cos(ang1), np.sin(ang1)], axis=1) / math.sqrt(t)
    ang2 = 2.0 * np.pi * ((np.arange(b_len)[:, None] * np.arange(b_len)[None, :]) % b_len) / b_len
    c2, s2 = np.cos(ang2), np.sin(ang2)
    stage2 = np.block([[c2, s2], [-s2, c2]])
    return jnp.asarray(stage1, BF16), jnp.asarray(stage2, BF16), a_len, b_len


def _time_dft_kernel(x_ref, m_ref, w2_ref, zre_ref, zim_ref, yre_ref, yim_ref, *, a_len, b_len):
    i = pl.program_id(1)
    bb = m_ref.shape[0]

    def stage1(jb, carry):
        b = i * bb + jb
        xb = x_ref[pl.ds(b, a_len, stride=b_len), :]
        y = jnp.dot(m_ref[jb], xb.astype(BF16), preferred_element_type=F32)
        row = pl.multiple_of(b * a_len, a_len)
        yre_ref[pl.ds(row, a_len), :] = y[:a_len]
        yim_ref[pl.ds(row, a_len), :] = y[a_len:]
        return carry

    lax.fori_loop(0, bb, stage1, 0)

    @pl.when(i == pl.num_programs(1) - 1)
    def _():
        def stage2(ka, carry):
            rows = pl.ds(ka, b_len, stride=a_len)
            y = jnp.concatenate([yre_ref[rows, :], yim_ref[rows, :]], axis=0)
            z = jnp.dot(w2_ref[...], y.astype(BF16), preferred_element_type=F32)
            yre_ref[rows, :] = z[:b_len]
            yim_ref[rows, :] = z[b_len:]
            return carry

        lax.fori_loop(0, a_len, stage2, 0)
        zre_ref[...] = yre_ref[...].astype(zre_ref.dtype)
        zim_ref[...] = yim_ref[...].astype(zim_ref.dtype)


def time_dft(x):
    t, d = x.shape
    stage1, stage2, a_len, b_len = time_dft_tables(t)
    lanes = 128
    bb = min(16, b_len)
    out = jax.ShapeDtypeStruct((t, d), BF16)
    return pl.pallas_call(
        functools.partial(_time_dft_kernel, a_len=a_len, b_len=b_len),
        grid=(d // lanes, b_len // bb),
        in_specs=[pl.BlockSpec((t, lanes), lambda j, i: (0, j)),
                  pl.BlockSpec((bb, 2 * a_len, a_len), lambda j, i: (i, 0, 0)),
                  pl.BlockSpec(stage2.shape, lambda j, i: (0, 0))],
        out_specs=[pl.BlockSpec((t, lanes), lambda j, i: (0, j))] * 2,
        out_shape=[out, out],
        scratch_shapes=[pltpu.VMEM((t, lanes), F32), pltpu.VMEM((t, lanes), F32)],
        compiler_params=_params("parallel", "arbitrary"),
    )(x, stage1, stage2)


def channel_dft_tables(c):
    ang = 2.0 * np.pi * ((np.arange(c)[:, None] * np.arange(c)[None, :]) % c) / c
    return (jnp.asarray(np.cos(ang) / math.sqrt(c), BF16), jnp.asarray(np.sin(ang) / math.sqrt(c), BF16))


def _ffn_prologue(h, shf_ref, scf_ref, wr_ref, wsg_ref, wsu_ref, wsd_ref, h_ref, f_ref, lg_ref, shared_ref):
    h_ref[...] = h
    f = _rms(h) * (1.0 + scf_ref[...]) + shf_ref[...]
    f_ref[...] = _pack_rows(f)
    lg_ref[...] = lax.dot_general(wr_ref[...], f, (((1,), (1,)), ((), ())),
                                  preferred_element_type=F32, precision=HIGHEST)
    fb = f.astype(BF16)
    hid = (_silu(jnp.dot(fb, wsg_ref[...], preferred_element_type=F32))
           * jnp.dot(fb, wsu_ref[...], preferred_element_type=F32))
    shared_ref[...] = jnp.dot(hid.astype(BF16), wsd_ref[...], preferred_element_type=F32).astype(shared_ref.dtype)


def _gelu_tanh(x):
    return 0.5 * x * (1.0 + jnp.tanh(math.sqrt(2.0 / math.pi) * (x + 0.044715 * (x * x * x))))


def _even_post_kernel(y_ref, na_ref, x_ref, wglu_ref, bglu_ref, wo_ref, gm_ref, *rest):
    g = _gelu_tanh(y_ref[...])
    gate = jax.nn.sigmoid(jnp.dot(g.astype(BF16), wglu_ref[...], preferred_element_type=F32) + bglu_ref[...])
    s5 = (g * gate).astype(BF16)
    w = s5.shape[1]
    mix = (jnp.dot(s5, wo_ref[:w, :], preferred_element_type=F32)
           + jnp.dot(na_ref[...], wo_ref[w:, :], preferred_element_type=F32))
    _ffn_prologue(x_ref[...] + gm_ref[...] * mix, *rest)


def _odd_post_kernel(zre_ref, zim_ref, h_ref_in, cc_ref, sc_ref, wf_ref, bf_ref, gm_ref, *rest):
    c = cc_ref.shape[0]
    parts = []
    for grp in range(zre_ref.shape[1] // c):
        cols = slice(grp * c, (grp + 1) * c)
        parts.append(jnp.dot(zre_ref[:, cols], cc_ref[...], preferred_element_type=F32)
                     + jnp.dot(zim_ref[:, cols], sc_ref[...], preferred_element_type=F32))
    fr = jnp.concatenate(parts, axis=-1).astype(BF16)
    mix = jnp.dot(fr, wf_ref[...], preferred_element_type=F32) + bf_ref[...]
    _ffn_prologue(h_ref_in[...] + gm_ref[...] * mix, *rest)


def _post_call(body, row_inputs, fixed_inputs, t, d, n_exp):
    tm = min(256, t)
    row = lambda i: (i, 0)
    fixed = lambda i: (0, 0)
    in_specs = ([pl.BlockSpec((tm, a.shape[1]), row) for a in row_inputs]
                + [pl.BlockSpec(a.shape, fixed) for a in fixed_inputs])
    return pl.pallas_call(
        body,
        grid=(t // tm,),
        in_specs=in_specs,
        out_specs=[pl.BlockSpec((tm, d), row), pl.BlockSpec((tm, d // 2), row),
                   pl.BlockSpec((n_exp, tm), lambda i: (0, i)), pl.BlockSpec((tm, d), row)],
        out_shape=[jax.ShapeDtypeStruct((t, d), F32), jax.ShapeDtypeStruct((t, d // 2), U32),
                   jax.ShapeDtypeStruct((n_exp, t), F32), jax.ShapeDtypeStruct((t, d), BF16)],
        compiler_params=_params("parallel"),
    )(*row_inputs, *fixed_inputs)


def _route_kernel(lg_ref, bias_ref, tri_ref, idx_ref, gate_ref, rank_ref, cnt_ref, run_ref):
    @pl.when(pl.program_id(0) == 0)
    def _():
        run_ref[...] = jnp.zeros_like(run_ref)

    scores = jax.nn.sigmoid(lg_ref[...])
    n_exp, tb = scores.shape
    sel = scores + bias_ref[...]
    gsz = n_exp // N_EXPERT_GROUPS
    member = lax.broadcasted_iota(I32, (gsz, tb), 0)
    gscore = []
    for grp in range(N_EXPERT_GROUPS):
        xg = sel[grp * gsz:(grp + 1) * gsz, :]
        m1 = jnp.max(xg, axis=0, keepdims=True)
        first = jnp.min(jnp.where(xg == m1, member, gsz), axis=0, keepdims=True)
        m2 = jnp.max(jnp.where(member == first, -jnp.inf, xg), axis=0, keepdims=True)
        gscore.append(m1 + m2)
    keep_rows = []
    for grp in range(N_EXPERT_GROUPS):
        beaten = jnp.zeros((1, tb), F32)
        for other in range(N_EXPERT_GROUPS):
            if other == grp:
                continue
            wins = (gscore[other] >= gscore[grp]) if other < grp else (gscore[other] > gscore[grp])
            beaten = beaten + jnp.where(wins, 1.0, 0.0)
        keep_rows.append(jnp.broadcast_to(beaten < TOPK_GROUPS, (gsz, tb)))
    masked = jnp.where(jnp.concatenate(keep_rows, axis=0), sel, -jnp.inf)

    expert = lax.broadcasted_iota(I32, (n_exp, tb), 0)
    picks, gates, hots = [], [], []
    chosen = jnp.zeros((n_exp, tb), F32)
    for _ in range(TOP_K):
        m = jnp.max(masked, axis=0, keepdims=True)
        pick = jnp.min(jnp.where(masked == m, expert, n_exp), axis=0, keepdims=True)
        hot = expert == pick
        picks.append(pick)
        hots.append(hot)
        gates.append(jnp.sum(jnp.where(hot, scores, 0.0), axis=0, keepdims=True))
        chosen = jnp.where(hot, 1.0, chosen)
        masked = jnp.where(hot, -jnp.inf, masked)
    total = gates[0]
    for gk in gates[1:]:
        total = total + gk
    ahead = jnp.dot(chosen.astype(BF16), tri_ref[...], preferred_element_type=F32) + run_ref[...]
    for k in range(TOP_K):
        idx_ref[k:k + 1, :] = picks[k]
        gate_ref[k:k + 1, :] = ROUTED_SCALE * gates[k] / total
        rank_ref[k:k + 1, :] = jnp.sum(jnp.where(hots[k], ahead, 0.0), axis=0, keepdims=True).astype(I32)
    run_ref[...] = run_ref[...] + jnp.sum(chosen, axis=1, keepdims=True)
    cnt_ref[...] = jnp.broadcast_to(run_ref[...], cnt_ref.shape)


def route(logits_t, router_bias):
    n_exp, t = logits_t.shape
    tb = min(512, t)
    tri = jnp.asarray(np.triu(np.ones((tb, tb), np.float32), k=1), BF16)
    tok = lambda i: (0, i)
    idx, gate, rank, cnt = pl.pallas_call(
        _route_kernel,
        grid=(t // tb,),
        in_specs=[pl.BlockSpec((n_exp, tb), tok), pl.BlockSpec((n_exp, 1), lambda i: (0, 0)),
                  pl.BlockSpec((tb, tb), lambda i: (0, 0))],
        out_specs=[pl.BlockSpec((TOP_K, tb), tok)] * 3 + [pl.BlockSpec((n_exp, 128), lambda i: (0, 0))],
        out_shape=[jax.ShapeDtypeStruct((TOP_K, t), I32), jax.ShapeDtypeStruct((TOP_K, t), F32),
                   jax.ShapeDtypeStruct((TOP_K, t), I32), jax.ShapeDtypeStruct((n_exp, 128), F32)],
        scratch_shapes=[pltpu.VMEM((n_exp, 1), F32)],
        compiler_params=_params("arbitrary"),
    )(logits_t, router_bias.astype(F32).reshape(n_exp, 1), tri)
    return idx, gate, rank, cnt[:, 0].astype(I32)


def dispatch_plan(idx, rank, counts, n_blocks):
    n_exp = counts.shape[0]
    padded = (counts + EXPERT_ROWS - 1) // EXPERT_ROWS * EXPERT_ROWS
    pad_end = jnp.cumsum(padded)
    pad_start = pad_end - padded
    experts_iota = jnp.arange(n_exp, dtype=I32)
    dest = jnp.sum(jnp.where(idx[..., None] == experts_iota, pad_start, 0), axis=-1) + rank
    n_valid = (pad_end[-1] // EXPERT_ROWS).astype(I32)
    blk = jnp.minimum(jnp.arange(n_blocks, dtype=I32), n_valid - 1)
    blk_e = jnp.sum((pad_end[None, :] <= blk[:, None] * EXPERT_ROWS).astype(I32), axis=1)
    return dest.astype(I32), jnp.minimum(blk_e, n_exp - 1).astype(I32), n_valid.reshape(1)


def _token_tiles(dest, tm):
    k, t = dest.shape
    return jnp.transpose(dest.reshape(k, t // tm, tm), (1, 0, 2))


def _dispatch_kernel(dest_ref, f_ref, rows_in, rows_out, sem):
    del rows_in
    tm = f_ref.shape[0]

    def body(n, carry):
        for k in range(TOP_K):
            pltpu.make_async_copy(f_ref.at[pl.ds(n, 1), :],
                                  rows_out.at[pl.ds(dest_ref[0, k, n], 1), :], sem).start()
        return carry

    lax.fori_loop(0, tm, body, 0)
    for k in range(TOP_K):
        pltpu.make_async_copy(f_ref, rows_out.at[pl.ds(0, tm), :], sem).wait()


def dispatch(f, dest, cap):
    t, d = f.shape
    tm = min(256, t)
    return pl.pallas_call(
        _dispatch_kernel,
        grid=(t // tm,),
        in_specs=[pl.BlockSpec((1, TOP_K, tm), lambda i: (i, 0, 0), memory_space=pltpu.SMEM),
                  pl.BlockSpec((tm, d), lambda i: (i, 0)),
                  pl.BlockSpec(memory_space=pl.ANY)],
        out_specs=pl.BlockSpec(memory_space=pl.ANY),
        out_shape=jax.ShapeDtypeStruct((cap, d), f.dtype),
        scratch_shapes=[pltpu.SemaphoreType.DMA(())],
        input_output_aliases={2: 0},
        compiler_params=_params("arbitrary"),
    )(_token_tiles(dest, tm), f, jnp.zeros((cap, d), f.dtype))


def _experts_kernel(be_ref, nv_ref, x_ref, wg_ref, wu_ref, wd_ref, y_ref, wg_bf, wu_bf, wd_bf):
    b = pl.program_id(0)

    @pl.when((b == 0) | (be_ref[b] != be_ref[jnp.maximum(b - 1, 0)]))
    def _():
        wg_bf[...] = wg_ref[0].astype(BF16)
        wu_bf[...] = wu_ref[0].astype(BF16)
        wd_bf[...] = wd_ref[0].astype(BF16)

    @pl.when(b < nv_ref[0])
    def _():
        x = _unpack_rows(x_ref[...]).astype(BF16)
        hid = (_silu(jnp.dot(x, wg_bf[...], preferred_element_type=F32))
               * jnp.dot(x, wu_bf[...], preferred_element_type=F32))
        y_ref[...] = _pack_rows(jnp.dot(hid.astype(BF16), wd_bf[...], preferred_element_type=F32))

    @pl.when(b >= nv_ref[0])
    def _():
        y_ref[...] = jnp.zeros_like(y_ref)


def experts(rows, blk_e, n_valid, w_gate, w_up, w_down):
    cap, dh = rows.shape
    _, d, ff = w_gate.shape
    n_blocks = cap // EXPERT_ROWS
    live = lambda b, be, nv: (jnp.minimum(b, nv[0] - 1), 0)
    return pl.pallas_call(
        _experts_kernel,
        grid_spec=pltpu.PrefetchScalarGridSpec(
            num_scalar_prefetch=2,
            grid=(n_blocks,),
            in_specs=[pl.BlockSpec((EXPERT_ROWS, dh), live),
                      pl.BlockSpec((1, d, ff), lambda b, be, nv: (be[b], 0, 0)),
                      pl.BlockSpec((1, d, ff), lambda b, be, nv: (be[b], 0, 0)),
                      pl.BlockSpec((1, ff, d), lambda b, be, nv: (be[b], 0, 0))],
            out_specs=pl.BlockSpec((EXPERT_ROWS, dh), lambda b, be, nv: (b, 0)),
            scratch_shapes=[pltpu.VMEM((d, ff), BF16), pltpu.VMEM((d, ff), BF16), pltpu.VMEM((ff, d), BF16)],
        ),
        out_shape=jax.ShapeDtypeStruct((cap, dh), U32),
        compiler_params=_params("arbitrary"),
    )(blk_e, n_valid, rows, w_gate, w_up, w_down)


def _combine_kernel(dest_ref, gate_ref, h_ref, shared_ref, gf_ref, sh_ref, sc_ref, y_hbm,
                    h_out, a_out, ybuf, sem):
    tm = h_ref.shape[0]

    def body(n, carry):
        for k in range(TOP_K):
            pltpu.make_async_copy(y_hbm.at[pl.ds(dest_ref[0, k, n], 1), :],
                                  ybuf.at[k, pl.ds(n, 1), :], sem).start()
        return carry

    lax.fori_loop(0, tm, body, 0)
    for k in range(TOP_K):
        pltpu.make_async_copy(y_hbm.at[pl.ds(0, tm), :], ybuf.at[k], sem).wait()
    routed = gate_ref[:, 0:1] * _unpack_rows(ybuf[0])
    for k in range(1, TOP_K):
        routed = routed + gate_ref[:, k:k + 1] * _unpack_rows(ybuf[k])
    h = h_ref[...] + gf_ref[...] * (routed + shared_ref[...].astype(F32))
    h_out[...] = h
    a_out[...] = _rms(h) * (1.0 + sc_ref[...]) + sh_ref[...]


def combine(y_rows, dest, gate_tk, h, shared, gate_ffn, next_shift, next_scale):
    t, d = h.shape
    tm = min(32, t)
    row = lambda i: (i, 0)
    fixed = lambda i: (0, 0)
    return pl.pallas_call(
        _combine_kernel,
        grid=(t // tm,),
        in_specs=[pl.BlockSpec((1, TOP_K, tm), lambda i: (i, 0, 0), memory_space=pltpu.SMEM),
                  pl.BlockSpec((tm, TOP_K), row), pl.BlockSpec((tm, d), row), pl.BlockSpec((tm, d), row),
                  pl.BlockSpec((1, d), fixed), pl.BlockSpec((1, d), fixed), pl.BlockSpec((1, d), fixed),
                  pl.BlockSpec(memory_space=pl.ANY)],
        out_specs=[pl.BlockSpec((tm, d), row), pl.BlockSpec((tm, d), row)],
        out_shape=[jax.ShapeDtypeStruct((t, d), F32), jax.ShapeDtypeStruct((t, d), F32)],
        scratch_shapes=[pltpu.VMEM((TOP_K, tm, d // 2), U32), pltpu.SemaphoreType.DMA(())],
        compiler_params=_params("arbitrary"),
    )(_token_tiles(dest, tm), gate_tk, h, shared, gate_ffn, next_shift, next_scale, y_rows)


def moe_tail(h, f, logits_t, shared, router_bias, w_gate, w_up, w_down, gate_ffn, next_shift, next_scale):
    t, d = h.shape
    n_exp = w_gate.shape[0]
    n_blocks = -(-(t * TOP_K + n_exp * (EXPERT_ROWS - 1)) // EXPERT_ROWS)
    idx, gate, rank, counts = route(logits_t, router_bias)
    dest, blk_e, n_valid = dispatch_plan(idx, rank, counts, n_blocks)
    rows = dispatch(f, dest, n_blocks * EXPERT_ROWS)
    y_rows = experts(rows, blk_e, n_valid, w_gate, w_up, w_down)
    return combine(y_rows, dest, gate.T, h, shared, gate_ffn, next_shift, next_scale)


def kernel(x, c, ctx, c_ctx, w_ada, b_ada, w_in, s5_lam_re, s5_lam_im, s5_log_dt, s5_b_re, s5_b_im,
           s5_c_re, s5_c_im, s5_d, s5_w_glu, s5_b_glu, na_q_gain, na_k_gain, na_rpb, w_mix_out,
           w_fourier_out, b_fourier_out, w_router, router_bias, w_exp_gate, w_exp_up, w_exp_down,
           w_sh_gate, w_sh_up, w_sh_down):
    bsz, t, d = x.shape
    assert bsz == 1 and w_ada.shape[0] == 2
    n_exp = w_router.shape[2]
    s5w = s5_w_glu.shape[1]
    naw = w_in.shape[2] - s5w
    naw //= 3
    heads = naw // NA_HEAD_DIM

    cond8 = jnp.zeros((8, d), F32).at[0].set(c[0].astype(F32)).at[1].set(c_ctx.astype(F32))
    ada = adaln_all(cond8, w_ada, b_ada)
    mod = lambda layer, who, j: ada[layer, who:who + 1, j * d:(j + 1) * d]

    def ffn_weights(i):
        return (mod(i, 0, 3), mod(i, 0, 4), jnp.transpose(w_router[i]).astype(F32),
                w_sh_gate[i].astype(BF16), w_sh_up[i].astype(BF16), w_sh_down[i].astype(BF16))

    h0 = x[0]
    seg = jnp.asarray(np.kron(np.eye(heads), np.ones((NA_HEAD_DIM, NA_HEAD_DIM))), BF16)
    w_in_b = w_in[0].astype(BF16)
    qg = jnp.tile(na_q_gain[0].astype(F32), heads)[None]
    kg = jnp.tile(na_k_gain[0].astype(F32), heads)[None]
    u_c, _, k_c, v_c = in_projection(ctx[0], mod(0, 1, 0), mod(0, 1, 1), w_in_b, seg, qg, kg, s5w, naw)
    u_l, q_l, k_l, v_l = in_projection(h0, mod(0, 0, 0), mod(0, 0, 1), w_in_b, seg, qg, kg, s5w, naw)
    mats = s5_matrices(s5_lam_re[0], s5_lam_im[0], s5_log_dt[0], s5_b_re[0], s5_b_im[0],
                       s5_c_re[0], s5_c_im[0], s5_d[0])
    y_s5 = s5_mixer(u_c, u_l, mats)
    head_major = lambda a: jnp.transpose(a.reshape(a.shape[0], heads, NA_HEAD_DIM), (1, 0, 2))
    na = neighbourhood_attention(head_major(q_l), head_major(k_l), head_major(v_l),
                                 head_major(k_c), head_major(v_c), na_bias_table(na_rpb[0]))
    na = jnp.transpose(na, (1, 0, 2)).reshape(t, naw)
    h1, f1, lg1, sh1 = _post_call(
        _even_post_kernel, [y_s5, na, h0],
        [s5_w_glu[0].astype(BF16), s5_b_glu[0].astype(F32)[None], w_mix_out[0].astype(BF16), mod(0, 0, 2),
         *ffn_weights(0)], t, d, n_exp)
    h2, a1 = moe_tail(h1, f1, lg1, sh1, router_bias[0], w_exp_gate[0], w_exp_up[0], w_exp_down[0],
                      mod(0, 0, 5), mod(1, 0, 0), mod(1, 0, 1))

    zre, zim = time_dft(a1)
    cc, sc = channel_dft_tables(d // FOURIER_GROUPS)
    h3, f3, lg3, sh3 = _post_call(
        _odd_post_kernel, [zre, zim, h2],
        [cc, sc, w_fourier_out[0].astype(BF16), b_fourier_out[0].astype(F32)[None], mod(1, 0, 2),
         *ffn_weights(1)], t, d, n_exp)
    zero_row = jnp.zeros((1, d), F32)
    out, _ = moe_tail(h3, f3, lg3, sh3, router_bias[1], w_exp_gate[1], w_exp_up[1], w_exp_down[1],
                      mod(1, 0, 5), zero_row, zero_row)
    return out[None]
```

```python
import functools
import math

import numpy as np
import jax
import jax.numpy as jnp
from jax import lax
from jax.experimental import pallas as pl
from jax.experimental.pallas import tpu as pltpu
from jax.experimental.pallas import tpu_sc as plsc

F32 = jnp.float32
BF16 = jnp.bfloat16
I32 = jnp.int32
U32 = jnp.uint32
HIGHEST = lax.Precision.HIGHEST

GRID_W = 64
NORM_EPS = 1e-6
S5_GROUP = 16
S5_STATE = 64
S5_LAMBDA_RE_MAX = -1e-4
S5_CHUNK = 16
NA_HEADS = 8
NA_HEAD_DIM = 64
NA_KH = 8
NA_KW = 16
FOURIER_GROUPS = 4
N_EXPERT_GROUPS = 8
TOPK_GROUPS = 4
TOP_K = 8
ROUTED_SCALE = 2.5
EXPERT_ROWS = 256
SC_WINDOW = 128
NEG_BIG = -1e30

VMEM_LIMIT_BYTES = 56 * 1024 * 1024


def _params(*sem):
    return pltpu.CompilerParams(dimension_semantics=sem or None,
                                vmem_limit_bytes=VMEM_LIMIT_BYTES)


def _rms(x):
    return x * lax.rsqrt(jnp.mean(x * x, axis=-1, keepdims=True) + NORM_EPS)


def _silu(x):
    return x * jax.nn.sigmoid(x)


def _pack_rows(x):
    n = x.shape[1] // 2
    lo = lax.bitcast_convert_type(x[:, :n].astype(BF16).astype(F32), U32)
    hi = lax.bitcast_convert_type(x[:, n:].astype(BF16).astype(F32), U32)
    return (hi & jnp.uint32(0xFFFF0000)) | (lo >> 16)


def _unpack_rows(w):
    lo = lax.bitcast_convert_type(w << 16, F32)
    hi = lax.bitcast_convert_type(w & jnp.uint32(0xFFFF0000), F32)
    return jnp.concatenate([lo, hi], axis=1)


def _ada_kernel(c_ref, w_ref, b_ref, o_ref):
    o_ref[0] = jnp.dot(_silu(c_ref[...]), w_ref[0], preferred_element_type=F32,
                       precision=HIGHEST) + b_ref[0]


def adaln_all(cond8, w_ada, b_ada):
    n_layers, d, n6 = w_ada.shape
    tn = n6 // 4
    return pl.pallas_call(
        _ada_kernel,
        grid=(n_layers, n6 // tn),
        in_specs=[pl.BlockSpec((8, d), lambda l, j: (0, 0)),
                  pl.BlockSpec((1, d, tn), lambda l, j: (l, 0, j)),
                  pl.BlockSpec((1, 1, tn), lambda l, j: (l, 0, j))],
        out_specs=pl.BlockSpec((1, 8, tn), lambda l, j: (l, 0, j)),
        out_shape=jax.ShapeDtypeStruct((n_layers, 8, n6), F32),
        compiler_params=_params("parallel", "parallel"),
    )(cond8, w_ada, b_ada.reshape(n_layers, 1, n6))


def _inproj_kernel(x_ref, sh_ref, sc_ref, w_ref, seg_ref, qg_ref, kg_ref,
                   u_ref, q_ref, k_ref, v_ref):
    a = _rms(x_ref[...]) * (1.0 + sc_ref[...]) + sh_ref[...]
    z = jnp.dot(a.astype(BF16), w_ref[...], preferred_element_type=F32)
    s5w = u_ref.shape[1]
    naw = q_ref.shape[1]

    def head_norm(t, gain):
        ss = jnp.dot((t * t).astype(BF16), seg_ref[...], preferred_element_type=F32)
        return t * lax.rsqrt(ss * (1.0 / NA_HEAD_DIM) + NORM_EPS) * gain

    u_ref[...] = z[:, :s5w].astype(BF16)
    q_ref[...] = head_norm(z[:, s5w:s5w + naw], qg_ref[...]).astype(BF16)
    k_ref[...] = head_norm(z[:, s5w + naw:s5w + 2 * naw], kg_ref[...]).astype(BF16)
    v_ref[...] = z[:, s5w + 2 * naw:].astype(BF16)


def in_projection(x, shift, scale, w_in_bf16, seg_ones, q_gain_row, k_gain_row, s5w, naw):
    t, d = x.shape
    tm = min(512, t)
    row = lambda i: (i, 0)
    fixed = lambda i: (0, 0)
    return pl.pallas_call(
        _inproj_kernel,
        grid=(t // tm,),
        in_specs=[pl.BlockSpec((tm, d), row),
                  pl.BlockSpec((1, d), fixed), pl.BlockSpec((1, d), fixed),
                  pl.BlockSpec(w_in_bf16.shape, fixed),
                  pl.BlockSpec(seg_ones.shape, fixed),
                  pl.BlockSpec((1, naw), fixed), pl.BlockSpec((1, naw), fixed)],
        out_specs=[pl.BlockSpec((tm, s5w), row), pl.BlockSpec((tm, naw), row),
                   pl.BlockSpec((tm, naw), row), pl.BlockSpec((tm, naw), row)],
        out_shape=[jax.ShapeDtypeStruct((t, s5w), BF16)] + [jax.ShapeDtypeStruct((t, naw), BF16)] * 3,
        compiler_params=_params("parallel"),
    )(x, shift, scale, w_in_bf16, seg_ones, q_gain_row, k_gain_row)


def s5_matrices(lam_re, lam_im, log_dt, b_re, b_im, c_re, c_im, d_skip):
    L = S5_CHUNK
    taus = jnp.arange(L + 1, dtype=F32)

    def direction(i):
        lam = lax.complex(jnp.minimum(lam_re[i].astype(F32), S5_LAMBDA_RE_MAX), lam_im[i].astype(F32))
        ldt = lam * jnp.exp(log_dt[i].astype(F32))[:, None]
        lam_bar = jnp.exp(ldt)
        b_bar = ((lam_bar - 1.0) / lam)[..., None] * lax.complex(b_re[i].astype(F32), b_im[i].astype(F32))
        cc = lax.complex(c_re[i].astype(F32), c_im[i].astype(F32))
        powers = jnp.exp(ldt[None] * taus[:, None, None])
        resp = jnp.real(jnp.einsum('gcp,tgp,gpd->gtcd', cc, powers[:L], b_bar))
        return powers, b_bar, cc, resp

    pw_f, bb_f, cc_f, k_f = direction(0)
    pw_b, bb_b, cc_b, k_b = direction(1)
    g, p = pw_f.shape[1:]
    c = S5_GROUP
    s_idx = jnp.arange(L)[:, None]
    l_idx = jnp.arange(L)[None, :]
    diff = l_idx - s_idx
    tf = jnp.where((diff >= 0)[None, :, :, None, None], k_f[:, jnp.clip(diff, 0, L - 1)], 0.0)
    tb = jnp.where((diff <= 0)[None, :, :, None, None], k_b[:, jnp.clip(-diff, 0, L - 1)], 0.0)
    skip = (jnp.eye(L, dtype=F32)[None, :, :, None, None]
            * (jnp.eye(c, dtype=F32)[None] * d_skip.astype(F32)[:, :, None])[:, None, None])
    toep = jnp.transpose(tf + tb + skip, (0, 1, 4, 2, 3)).reshape(g, L * c, L * c)

    def state_in(powers_sel, b_bar):
        w = powers_sel[:, :, :, None] * b_bar[None]
        return jnp.transpose(w, (1, 0, 3, 2)).reshape(g, L * c, p)

    wf = state_in(pw_f[L - 1 - jnp.arange(L)], bb_f)
    wb = state_in(pw_b[jnp.arange(L)], bb_b)
    w_state = jnp.concatenate([jnp.real(wf), jnp.imag(wf), jnp.imag(wf), jnp.real(wf),
                               jnp.real(wb), jnp.imag(wb), jnp.imag(wb), jnp.real(wb)], axis=-1)

    def state_out(powers_sel, cc):
        r = powers_sel[:, :, None, :] * cc[None]
        return jnp.transpose(r, (1, 3, 0, 2)).reshape(g, p, L * c)

    rf = state_out(pw_f[1 + jnp.arange(L)], cc_f)
    rb = state_out(pw_b[L - jnp.arange(L)], cc_b)
    r_state = jnp.concatenate([jnp.real(rf), -jnp.imag(rf), jnp.real(rb), -jnp.imag(rb)], axis=1)

    def mult(a):
        ar, ai = jnp.real(a), jnp.imag(a)
        return jnp.stack([jnp.concatenate([ar, ar], -1), jnp.concatenate([-ai, ai], -1),
                          jnp.concatenate([ai, -ai], -1)])

    return toep, w_state, r_state, mult(pw_f[L]), mult(pw_b[L])


def _s5_state_in_kernel(u_ref, w_ref, f1_ref, f2_ref, b1_ref, b2_ref):
    s = jnp.dot(u_ref[0], w_ref[0], preferred_element_type=F32)
    n = f1_ref.shape[2]
    f1_ref[0] = s[:, :n]
    f2_ref[0] = s[:, n:2 * n]
    b1_ref[0] = s[:, 2 * n:3 * n]
    b2_ref[0] = s[:, 3 * n:]


def s5_state_increments(u_chunks, w_state):
    g, nc, lc = u_chunks.shape
    n = w_state.shape[2] // 4
    blk = lambda i: (i, 0, 0)
    out = jax.ShapeDtypeStruct((g, nc, n), F32)
    return pl.pallas_call(
        _s5_state_in_kernel,
        grid=(g,),
        in_specs=[pl.BlockSpec((1, nc, lc), blk), pl.BlockSpec((1, lc, 4 * n), blk)],
        out_specs=[pl.BlockSpec((1, nc, n), blk)] * 4,
        out_shape=[out] * 4,
        compiler_params=_params("parallel"),
    )(u_chunks, w_state)


def _s5_scan_kernel(s1_ref, s2_ref, m_ref, x_ref, v1_ref, v2_ref, *, reverse):
    @pl.when(pl.program_id(0) == 0)
    def _():
        v1_ref[...] = jnp.zeros_like(v1_ref)
        v2_ref[...] = jnp.zeros_like(v2_ref)

    a1, a2, a3 = m_ref[0], m_ref[1], m_ref[2]
    cb = s1_ref.shape[0]

    def body(j, carry):
        v1, v2 = carry
        jj = cb - 1 - j if reverse else j
        x_ref[jj] = v1
        return (a1 * v1 + a2 * v2 + s1_ref[jj], a1 * v2 + a3 * v1 + s2_ref[jj])

    v1, v2 = lax.fori_loop(0, cb, body, (v1_ref[...], v2_ref[...]))
    v1_ref[...] = v1
    v2_ref[...] = v2


def s5_chunk_scan(s1, s2, mult, reverse):
    nc, g, n = s1.shape
    cb = 96 if nc % 96 == 0 else nc
    nb = nc // cb
    blk = (lambda i: (nb - 1 - i, 0, 0)) if reverse else (lambda i: (i, 0, 0))
    return pl.pallas_call(
        functools.partial(_s5_scan_kernel, reverse=reverse),
        grid=(nb,),
        in_specs=[pl.BlockSpec((cb, g, n), blk), pl.BlockSpec((cb, g, n), blk),
                  pl.BlockSpec((3, g, n), lambda i: (0, 0, 0))],
        out_specs=pl.BlockSpec((cb, g, n), blk),
        out_shape=jax.ShapeDtypeStruct((nc, g, n), F32),
        scratch_shapes=[pltpu.VMEM((g, n), F32), pltpu.VMEM((g, n), F32)],
        compiler_params=_params("arbitrary"),
    )(s1, s2, mult)


def _s5_readout_kernel(u_ref, t_ref, xf_ref, xb_ref, r_ref, y_ref, *, first, count):
    u = u_ref[0, first:first + count, :]
    xin = jnp.concatenate([xf_ref[0, first:first + count, :], xb_ref[0, first:first + count, :]], axis=-1)
    y_ref[0] = (jnp.dot(u, t_ref[0], preferred_element_type=F32)
                + jnp.dot(xin.astype(BF16), r_ref[0], preferred_element_type=F32))


def s5_readout(u_chunks, toep, xin_f, xin_b, r_state, first, count):
    g, nc, lc = u_chunks.shape
    n = xin_f.shape[2]
    blk = lambda i: (i, 0, 0)
    return pl.pallas_call(
        functools.partial(_s5_readout_kernel, first=first, count=count),
        grid=(g,),
        in_specs=[pl.BlockSpec((1, nc, lc), blk), pl.BlockSpec((1, lc, lc), blk),
                  pl.BlockSpec((1, nc, n), blk), pl.BlockSpec((1, nc, n), blk),
                  pl.BlockSpec((1, 2 * n, lc), blk)],
        out_specs=pl.BlockSpec((1, count, lc), blk),
        out_shape=jax.ShapeDtypeStruct((g, count, lc), F32),
        compiler_params=_params("parallel"),
    )(u_chunks, toep, xin_f, xin_b, r_state)


def s5_mixer(u_ctx, u_lat, mats):
    toep, w_state, r_state, mult_f, mult_b = mats
    L, c = S5_CHUNK, S5_GROUP
    g = toep.shape[0]
    seq = jnp.concatenate([u_ctx, u_lat, u_ctx], axis=0)
    nc = seq.shape[0] // L
    n_ctx, n_lat = u_ctx.shape[0] // L, u_lat.shape[0] // L
    u_chunks = jnp.transpose(seq.reshape(nc, L, g, c), (2, 0, 1, 3)).reshape(g, nc, L * c)
    incs = s5_state_increments(u_chunks, w_state.astype(BF16))
    f1, f2, b1, b2 = [jnp.transpose(a, (1, 0, 2)) for a in incs]
    xin_f = jnp.transpose(s5_chunk_scan(f1, f2, mult_f, False), (1, 0, 2))
    xin_b = jnp.transpose(s5_chunk_scan(b1, b2, mult_b, True), (1, 0, 2))
    y = s5_readout(u_chunks, toep.astype(BF16), xin_f, xin_b, r_state.astype(BF16), n_ctx, n_lat)
    return jnp.transpose(y.reshape(g, n_lat, L, c), (1, 2, 0, 3)).reshape(n_lat * L, g * c)


def na_bias_table(rpb):
    q_col = np.arange(GRID_W)
    col_start = np.clip(q_col - NA_KW // 2, 0, GRID_W - NA_KW)
    key_col = np.arange(GRID_W)
    off = key_col[None, :] - col_start[:, None]
    valid = (off >= 0) & (off < NA_KW)
    rel_col = np.clip(key_col[None, :] - q_col[:, None] + NA_KW - 1, 0, 2 * NA_KW - 2)
    full = jnp.where(valid[None, None], rpb.astype(F32)[:, :, rel_col], NEG_BIG)
    variants = []
    for d in range(NA_KH):
        rows = [full[:, i - d + NA_KH - 1] for i in range(NA_KH)]
        variants.append(jnp.concatenate(rows, axis=-1))
    return jnp.stack(variants)


def _na_kernel(*refs, scale):
    q_ref = refs[0]
    k_refs = refs[1:1 + NA_KH]
    v_refs = refs[1 + NA_KH:1 + 2 * NA_KH]
    kc_ref, vc_ref, b_ref, o_ref = refs[1 + 2 * NA_KH:]
    nt = (((1,), (1,)), ((), ()))
    for h in range(q_ref.shape[0]):
        q = q_ref[h]
        kk = jnp.concatenate([r[h] for r in k_refs], axis=0)
        vv = jnp.concatenate([r[h] for r in v_refs], axis=0)
        s = lax.dot_general(q, kk, nt, preferred_element_type=F32) * scale + b_ref[0, h]
        sc = lax.dot_general(q, kc_ref[h], nt, preferred_element_type=F32) * scale
        m = jnp.maximum(jnp.max(s, axis=-1, keepdims=True), jnp.max(sc, axis=-1, keepdims=True))
        p = jnp.exp(s - m)
        pc = jnp.exp(sc - m)
        den = jnp.sum(p, axis=-1, keepdims=True) + jnp.sum(pc, axis=-1, keepdims=True)
        o = (jnp.dot(p.astype(BF16), vv, preferred_element_type=F32)
             + jnp.dot(pc.astype(BF16), vc_ref[h], preferred_element_type=F32))
        o_ref[h] = (o / den).astype(o_ref.dtype)


def neighbourhood_attention(q, k, v, k_ctx, v_ctx, bias_table):
    h, t, hd = q.shape
    rows = t // GRID_W
    kh = min(NA_KH, rows)
    assert kh == NA_KH
    first = lambda r: jnp.clip(r - kh // 2, 0, rows - kh)
    row_blk = pl.BlockSpec((h, GRID_W, hd), lambda r: (0, r, 0))
    key_blks = [pl.BlockSpec((h, GRID_W, hd), functools.partial(lambda r, i: (0, first(r) + i, 0), i=i))
                for i in range(kh)]
    ctx_blk = pl.BlockSpec(k_ctx.shape, lambda r: (0, 0, 0))
    bias_blk = pl.BlockSpec((1,) + bias_table.shape[1:], lambda r: (r - first(r), 0, 0, 0))
    return pl.pallas_call(
        functools.partial(_na_kernel, scale=hd ** -0.5),
        grid=(rows,),
        in_specs=[row_blk] + key_blks + key_blks + [ctx_blk, ctx_blk, bias_blk],
        out_specs=row_blk,
        out_shape=jax.ShapeDtypeStruct((h, t, hd), BF16),
        compiler_params=_params("parallel"),
    )(q, *([k] * kh), *([v] * kh), k_ctx, v_ctx, bias_table)


def time_dft_tables(t):
    a_len = 1 << (int(math.log2(t)) // 2)
    b_len = t // a_len
    ka = np.arange(a_len)[:, None]
    tok = b_len * np.arange(a_len)[None, :]
    ang1 = -2.0 * np.pi * ((ka * (tok[None] + np.arange(b_len)[:, None, None])) % t) / t
    stage1 = np.concatenate([np.cos(ang1), np.sin(ang1)], axis=1) / math.sqrt(t)
    ang2 = 2.0 * np.pi * ((np.arange(b_len)[:, None] * np.arange(b_len)[None, :]) % b_len) / b_len
    c2, s2 = np.cos(ang2), np.sin(ang2)
    stage2 = np.block([[c2, s2], [-s2, c2]])
    return jnp.asarray(stage1, BF16), jnp.asarray(stage2, BF16), a_len, b_len


def _time_dft_kernel(x_ref, m_ref, w2_ref, zre_ref, zim_ref, yre_ref, yim_ref, *, a_len, b_len):
    i = pl.program_id(1)
    bb = m_ref.shape[0]

    def stage1(jb, carry):
        b = i * bb + jb
        xb = x_ref[pl.ds(b, a_len, stride=b_len), :]
        y = jnp.dot(m_ref[jb], xb.astype(BF16), preferred_element_type=F32)
        row = pl.multiple_of(b * a_len, a_len)
        yre_ref[pl.ds(row, a_len), :] = y[:a_len]
        yim_ref[pl.ds(row, a_len), :] = y[a_len:]
        return carry

    lax.fori_loop(0, bb, stage1, 0)

    @pl.when(i == pl.num_programs(1) - 1)
    def _():
        def stage2(ka, carry):
            rows = pl.ds(ka, b_len, stride=a_len)
            y = jnp.concatenate([yre_ref[rows, :], yim_ref[rows, :]], axis=0)
            z = jnp.dot(w2_ref[...], y.astype(BF16), preferred_element_type=F32)
            yre_ref[rows, :] = z[:b_len]
            yim_ref[rows, :] = z[b_len:]
            return carry

        lax.fori_loop(0, a_len, stage2, 0)
        zre_ref[...] = yre_ref[...].astype(zre_ref.dtype)
        zim_ref[...] = yim_ref[...].astype(zim_ref.dtype)


def time_dft(x):
    t, d = x.shape
    stage1, stage2, a_len, b_len = time_dft_tables(t)
    lanes = 128
    bb = min(16, b_len)
    out = jax.ShapeDtypeStruct((t, d), BF16)
    return pl.pallas_call(
        functools.partial(_time_dft_kernel, a_len=a_len, b_len=b_len),
        grid=(d // lanes, b_len // bb),
        in_specs=[pl.BlockSpec((t, lanes), lambda j, i: (0, j)),
                  pl.BlockSpec((bb, 2 * a_len, a_len), lambda j, i: (i, 0, 0)),
                  pl.BlockSpec(stage2.shape, lambda j, i: (0, 0))],
        out_specs=[pl.BlockSpec((t, lanes), lambda j, i: (0, j))] * 2,
        out_shape=[out, out],
        scratch_shapes=[pltpu.VMEM((t, lanes), F32), pltpu.VMEM((t, lanes), F32)],
        compiler_params=_params("parallel", "arbitrary"),
    )(x, stage1, stage2)


def channel_dft_tables(c):
    ang = 2.0 * np.pi * ((np.arange(c)[:, None] * np.arange(c)[None, :]) % c) / c
    return (jnp.asarray(np.cos(ang) / math.sqrt(c), BF16), jnp.asarray(np.sin(ang) / math.sqrt(c), BF16))


def _ffn_prologue(h, shf_ref, scf_ref, wr_ref, wsg_ref, wsu_ref, wsd_ref, h_ref, fa_ref, fb_ref, lg_ref,
                  shared_ref):
    h_ref[...] = h
    f = _rms(h) * (1.0 + scf_ref[...]) + shf_ref[...]
    packed = _pack_rows(f)
    half = packed.shape[1] // 2
    fa_ref[...] = packed[:, :half]
    fb_ref[...] = packed[:, half:]
    lg_ref[...] = lax.dot_general(wr_ref[...], f, (((1,), (1,)), ((), ())),
                                  preferred_element_type=F32, precision=HIGHEST)
    fb = f.astype(BF16)
    hid = (_silu(jnp.dot(fb, wsg_ref[...], preferred_element_type=F32))
           * jnp.dot(fb, wsu_ref[...], preferred_element_type=F32))
    shared_ref[...] = jnp.dot(hid.astype(BF16), wsd_ref[...], preferred_element_type=F32).astype(shared_ref.dtype)


def _gelu_tanh(x):
    return 0.5 * x * (1.0 + jnp.tanh(math.sqrt(2.0 / math.pi) * (x + 0.044715 * (x * x * x))))


def _even_post_kernel(y_ref, na_ref, x_ref, wglu_ref, bglu_ref, wo_ref, gm_ref, *rest):
    g = _gelu_tanh(y_ref[...])
    gate = jax.nn.sigmoid(jnp.dot(g.astype(BF16), wglu_ref[...], preferred_element_type=F32) + bglu_ref[...])
    s5 = (g * gate).astype(BF16)
    w = s5.shape[1]
    mix = (jnp.dot(s5, wo_ref[:w, :], preferred_element_type=F32)
           + jnp.dot(na_ref[...], wo_ref[w:, :], preferred_element_type=F32))
    _ffn_prologue(x_ref[...] + gm_ref[...] * mix, *rest)


def _odd_post_kernel(zre_ref, zim_ref, h_ref_in, cc_ref, sc_ref, wf_ref, bf_ref, gm_ref, *rest):
    c = cc_ref.shape[0]
    parts = []
    for grp in range(zre_ref.shape[1] // c):
        cols = slice(grp * c, (grp + 1) * c)
        parts.append(jnp.dot(zre_ref[:, cols], cc_ref[...], preferred_element_type=F32)
                     + jnp.dot(zim_ref[:, cols], sc_ref[...], preferred_element_type=F32))
    fr = jnp.concatenate(parts, axis=-1).astype(BF16)
    mix = jnp.dot(fr, wf_ref[...], preferred_element_type=F32) + bf_ref[...]
    _ffn_prologue(h_ref_in[...] + gm_ref[...] * mix, *rest)


def _post_call(body, row_inputs, fixed_inputs, t, d, n_exp):
    tm = min(256, t)
    row = lambda i: (i, 0)
    fixed = lambda i: (0, 0)
    in_specs = ([pl.BlockSpec((tm, a.shape[1]), row) for a in row_inputs]
                + [pl.BlockSpec(a.shape, fixed) for a in fixed_inputs])
    return pl.pallas_call(
        body,
        grid=(t // tm,),
        in_specs=in_specs,
        out_specs=[pl.BlockSpec((tm, d), row), pl.BlockSpec((tm, d // 4), row), pl.BlockSpec((tm, d // 4), row),
                   pl.BlockSpec((n_exp, tm), lambda i: (0, i)), pl.BlockSpec((tm, d), row)],
        out_shape=[jax.ShapeDtypeStruct((t, d), F32),
                   jax.ShapeDtypeStruct((t, d // 4), U32), jax.ShapeDtypeStruct((t, d // 4), U32),
                   jax.ShapeDtypeStruct((n_exp, t), F32), jax.ShapeDtypeStruct((t, d), BF16)],
        compiler_params=_params("parallel"),
    )(*row_inputs, *fixed_inputs)


def _route_kernel(lg_ref, bias_ref, tri_ref, idx_ref, gate_ref, rank_ref, cnt_ref, run_ref):
    @pl.when(pl.program_id(0) == 0)
    def _():
        run_ref[...] = jnp.zeros_like(run_ref)

    scores = jax.nn.sigmoid(lg_ref[...])
    n_exp, tb = scores.shape
    sel = scores + bias_ref[...]
    gsz = n_exp // N_EXPERT_GROUPS
    member = lax.broadcasted_iota(I32, (gsz, tb), 0)
    gscore = []
    for grp in range(N_EXPERT_GROUPS):
        xg = sel[grp * gsz:(grp + 1) * gsz, :]
        m1 = jnp.max(xg, axis=0, keepdims=True)
        first = jnp.min(jnp.where(xg == m1, member, gsz), axis=0, keepdims=True)
        m2 = jnp.max(jnp.where(member == first, -jnp.inf, xg), axis=0, keepdims=True)
        gscore.append(m1 + m2)
    keep_rows = []
    for grp in range(N_EXPERT_GROUPS):
        beaten = jnp.zeros((1, tb), F32)
        for other in range(N_EXPERT_GROUPS):
            if other == grp:
                continue
            wins = (gscore[other] >= gscore[grp]) if other < grp else (gscore[other] > gscore[grp])
            beaten = beaten + jnp.where(wins, 1.0, 0.0)
        keep_rows.append(jnp.broadcast_to(beaten < TOPK_GROUPS, (gsz, tb)))
    masked = jnp.where(jnp.concatenate(keep_rows, axis=0), sel, -jnp.inf)

    expert = lax.broadcasted_iota(I32, (n_exp, tb), 0)
    picks, gates, hots = [], [], []
    chosen = jnp.zeros((n_exp, tb), F32)
    for _ in range(TOP_K):
        m = jnp.max(masked, axis=0, keepdims=True)
        pick = jnp.min(jnp.where(masked == m, expert, n_exp), axis=0, keepdims=True)
        hot = expert == pick
        picks.append(pick)
        hots.append(hot)
        gates.append(jnp.sum(jnp.where(hot, scores, 0.0), axis=0, keepdims=True))
        chosen = jnp.where(hot, 1.0, chosen)
        masked = jnp.where(hot, -jnp.inf, masked)
    total = gates[0]
    for gk in gates[1:]:
        total = total + gk
    ahead = jnp.dot(chosen.astype(BF16), tri_ref[...], preferred_element_type=F32) + run_ref[...]
    for k in range(TOP_K):
        idx_ref[k:k + 1, :] = picks[k]
        gate_ref[k:k + 1, :] = ROUTED_SCALE * gates[k] / total
        rank_ref[k:k + 1, :] = jnp.sum(jnp.where(hots[k], ahead, 0.0), axis=0, keepdims=True).astype(I32)
    run_ref[...] = run_ref[...] + jnp.sum(chosen, axis=1, keepdims=True)
    cnt_ref[...] = jnp.broadcast_to(run_ref[...], cnt_ref.shape)


def route(logits_t, router_bias):
    n_exp, t = logits_t.shape
    tb = min(512, t)
    tri = jnp.asarray(np.triu(np.ones((tb, tb), np.float32), k=1), BF16)
    tok = lambda i: (0, i)
    idx, gate, rank, cnt = pl.pallas_call(
        _route_kernel,
        grid=(t // tb,),
        in_specs=[pl.BlockSpec((n_exp, tb), tok), pl.BlockSpec((n_exp, 1), lambda i: (0, 0)),
                  pl.BlockSpec((tb, tb), lambda i: (0, 0))],
        out_specs=[pl.BlockSpec((TOP_K, tb), tok)] * 3 + [pl.BlockSpec((n_exp, 128), lambda i: (0, 0))],
        out_shape=[jax.ShapeDtypeStruct((TOP_K, t), I32), jax.ShapeDtypeStruct((TOP_K, t), F32),
                   jax.ShapeDtypeStruct((TOP_K, t), I32), jax.ShapeDtypeStruct((n_exp, 128), F32)],
        scratch_shapes=[pltpu.VMEM((n_exp, 1), F32)],
        compiler_params=_params("arbitrary"),
    )(logits_t, router_bias.astype(F32).reshape(n_exp, 1), tri)
    return idx, gate, rank, cnt[:, 0].astype(I32)


def dispatch_plan(idx, rank, counts, n_blocks):
    n_exp = counts.shape[0]
    padded = (counts + EXPERT_ROWS - 1) // EXPERT_ROWS * EXPERT_ROWS
    pad_end = jnp.cumsum(padded)
    pad_start = pad_end - padded
    experts_iota = jnp.arange(n_exp, dtype=I32)
    dest = jnp.sum(jnp.where(idx[..., None] == experts_iota, pad_start, 0), axis=-1) + rank
    n_valid = (pad_end[-1] // EXPERT_ROWS).astype(I32)
    blk = jnp.minimum(jnp.arange(n_blocks, dtype=I32), n_valid - 1)
    blk_e = jnp.sum((pad_end[None, :] <= blk[:, None] * EXPERT_ROWS).astype(I32), axis=1)
    return dest.astype(I32), jnp.minimum(blk_e, n_exp - 1).astype(I32), n_valid.reshape(1)


def _sc_mesh():
    return plsc.VectorSubcoreMesh(core_axis_name="core", subcore_axis_name="subcore")


def dispatch(f, dest, cap):
    t, w = f.shape
    n_choice = dest.shape[0]

    @functools.partial(pl.kernel, out_type=jax.ShapeDtypeStruct((cap, w), f.dtype), mesh=_sc_mesh(),
                       scratch_types=[])
    def scatter_rows(x_hbm, i_hbm, o_hbm):
        def body(x_vmem, i_vmem):
            pltpu.sync_copy(x_vmem, o_hbm.at[i_vmem.at[0]])

        pltpu.emit_pipeline(
            body,
            grid=(t // SC_WINDOW, n_choice),
            in_specs=[pl.BlockSpec((SC_WINDOW, w), lambda i, k: (i, 0)),
                      pl.BlockSpec((1, SC_WINDOW), lambda i, k: (k, i))],
            out_specs=[],
            core_axis_name=("core", "subcore"),
            dimension_semantics=(pltpu.PARALLEL, pltpu.ARBITRARY),
        )(x_hbm, i_hbm)

    return scatter_rows(f, dest)


def gather_rows(rows, index_row):
    n = index_row.shape[1]
    w = rows.shape[1]

    @functools.partial(pl.kernel, out_type=jax.ShapeDtypeStruct((n, w), rows.dtype), mesh=_sc_mesh(),
                       scratch_types=[])
    def gather(y_hbm, i_hbm, o_hbm):
        def body(i_vmem, o_vmem):
            pltpu.sync_copy(y_hbm.at[i_vmem.at[0]], o_vmem)

        pltpu.emit_pipeline(
            body,
            grid=(n // SC_WINDOW,),
            in_specs=[pl.BlockSpec((1, SC_WINDOW), lambda i: (0, i))],
            out_specs=[pl.BlockSpec((SC_WINDOW, w), lambda i: (i, 0))],
            core_axis_name=("core", "subcore"),
            dimension_semantics=(pltpu.PARALLEL,),
        )(i_hbm, o_hbm)

    return gather(rows, index_row)


def _experts_kernel(be_ref, nv_ref, xa_ref, xb_ref, wg_ref, wu_ref, wd_ref, ya_ref, yb_ref, wg_bf, wu_bf, wd_bf):
    b = pl.program_id(0)

    @pl.when((b == 0) | (be_ref[b] != be_ref[jnp.maximum(b - 1, 0)]))
    def _():
        wg_bf[...] = wg_ref[0].astype(BF16)
        wu_bf[...] = wu_ref[0].astype(BF16)
        wd_bf[...] = wd_ref[0].astype(BF16)

    @pl.when(b < nv_ref[0])
    def _():
        x = _unpack_rows(jnp.concatenate([xa_ref[...], xb_ref[...]], axis=1)).astype(BF16)
        hid = (_silu(jnp.dot(x, wg_bf[...], preferred_element_type=F32))
               * jnp.dot(x, wu_bf[...], preferred_element_type=F32))
        packed = _pack_rows(jnp.dot(hid.astype(BF16), wd_bf[...], preferred_element_type=F32))
        half = packed.shape[1] // 2
        ya_ref[...] = packed[:, :half]
        yb_ref[...] = packed[:, half:]

    @pl.when(b >= nv_ref[0])
    def _():
        ya_ref[...] = jnp.zeros_like(ya_ref)
        yb_ref[...] = jnp.zeros_like(yb_ref)


def experts(rows_a, rows_b, blk_e, n_valid, w_gate, w_up, w_down):
    cap, dq = rows_a.shape
    _, d, ff = w_gate.shape
    n_blocks = cap // EXPERT_ROWS
    live = lambda b, be, nv: (jnp.minimum(b, nv[0] - 1), 0)
    out_blk = pl.BlockSpec((EXPERT_ROWS, dq), lambda b, be, nv: (b, 0))
    out = jax.ShapeDtypeStruct((cap, dq), U32)
    return pl.pallas_call(
        _experts_kernel,
        grid_spec=pltpu.PrefetchScalarGridSpec(
            num_scalar_prefetch=2,
            grid=(n_blocks,),
            in_specs=[pl.BlockSpec((EXPERT_ROWS, dq), live), pl.BlockSpec((EXPERT_ROWS, dq), live),
                      pl.BlockSpec((1, d, ff), lambda b, be, nv: (be[b], 0, 0)),
                      pl.BlockSpec((1, d, ff), lambda b, be, nv: (be[b], 0, 0)),
                      pl.BlockSpec((1, ff, d), lambda b, be, nv: (be[b], 0, 0))],
            out_specs=[out_blk, out_blk],
            scratch_shapes=[pltpu.VMEM((d, ff), BF16), pltpu.VMEM((d, ff), BF16), pltpu.VMEM((ff, d), BF16)],
        ),
        out_shape=[out, out],
        compiler_params=_params("arbitrary"),
    )(blk_e, n_valid, rows_a, rows_b, w_gate, w_up, w_down)


def _combine_kernel(gate_ref, h_ref, shared_ref, gf_ref, sh_ref, sc_ref, *rest):
    ya_refs, yb_refs, (h_out, a_out) = rest[:TOP_K], rest[TOP_K:2 * TOP_K], rest[2 * TOP_K:]
    routed = None
    for k in range(TOP_K):
        y = _unpack_rows(jnp.concatenate([ya_refs[k][0], yb_refs[k][0]], axis=1))
        routed = gate_ref[:, k:k + 1] * y if routed is None else routed + gate_ref[:, k:k + 1] * y
    h = h_ref[...] + gf_ref[...] * (routed + shared_ref[...].astype(F32))
    h_out[...] = h
    a_out[...] = _rms(h) * (1.0 + sc_ref[...]) + sh_ref[...]


def combine(y_a, y_b, dest, gate_tk, h, shared, gate_ffn, next_shift, next_scale):
    t, d = h.shape
    n_choice = dest.shape[0]
    flat = dest.reshape(1, n_choice * t)
    picked_a = gather_rows(y_a, flat).reshape(n_choice, t, d // 4)
    picked_b = gather_rows(y_b, flat).reshape(n_choice, t, d // 4)
    tm = min(256, t)
    row = lambda i: (i, 0)
    fixed = lambda i: (0, 0)
    choice = [pl.BlockSpec((1, tm, d // 4), functools.partial(lambda i, k: (k, i, 0), k=k))
              for k in range(n_choice)] * 2
    return pl.pallas_call(
        _combine_kernel,
        grid=(t // tm,),
        in_specs=[pl.BlockSpec((tm, n_choice), row), pl.BlockSpec((tm, d), row), pl.BlockSpec((tm, d), row),
                  pl.BlockSpec((1, d), fixed), pl.BlockSpec((1, d), fixed), pl.BlockSpec((1, d), fixed)] + choice,
        out_specs=[pl.BlockSpec((tm, d), row), pl.BlockSpec((tm, d), row)],
        out_shape=[jax.ShapeDtypeStruct((t, d), F32), jax.ShapeDtypeStruct((t, d), F32)],
        compiler_params=_params("parallel"),
    )(gate_tk, h, shared, gate_ffn, next_shift, next_scale, *([picked_a] * n_choice), *([picked_b] * n_choice))


def moe_tail(h, f_a, f_b, logits_t, shared, router_bias, w_gate, w_up, w_down, gate_ffn, next_shift, next_scale):
    t, d = h.shape
    n_exp = w_gate.shape[0]
    n_blocks = -(-(t * TOP_K + n_exp * (EXPERT_ROWS - 1)) // EXPERT_ROWS)
    cap = n_blocks * EXPERT_ROWS
    idx, gate, rank, counts = route(logits_t, router_bias)
    dest, blk_e, n_valid = dispatch_plan(idx, rank, counts, n_blocks)
    y_a, y_b = experts(dispatch(f_a, dest, cap), dispatch(f_b, dest, cap), blk_e, n_valid, w_gate, w_up, w_down)
    return combine(y_a, y_b, dest, gate.T, h, shared, gate_ffn, next_shift, next_scale)


def kernel(x, c, ctx, c_ctx, w_ada, b_ada, w_in, s5_lam_re, s5_lam_im, s5_log_dt, s5_b_re, s5_b_im,
           s5_c_re, s5_c_im, s5_d, s5_w_glu, s5_b_glu, na_q_gain, na_k_gain, na_rpb, w_mix_out,
           w_fourier_out, b_fourier_out, w_router, router_bias, w_exp_gate, w_exp_up, w_exp_down,
           w_sh_gate, w_sh_up, w_sh_down):
    bsz, t, d = x.shape
    assert bsz == 1 and w_ada.shape[0] == 2
    n_exp = w_router.shape[2]
    s5w = s5_w_glu.shape[1]
    naw = w_in.shape[2] - s5w
    naw //= 3
    heads = naw // NA_HEAD_DIM

    cond8 = jnp.zeros((8, d), F32).at[0].set(c[0].astype(F32)).at[1].set(c_ctx.astype(F32))
    ada = adaln_all(cond8, w_ada, b_ada)
    mod = lambda layer, who, j: ada[layer, who:who + 1, j * d:(j + 1) * d]

    def ffn_weights(i):
        return (mod(i, 0, 3), mod(i, 0, 4), jnp.transpose(w_router[i]).astype(F32),
                w_sh_gate[i].astype(BF16), w_sh_up[i].astype(BF16), w_sh_down[i].astype(BF16))

    h0 = x[0]
    seg = jnp.asarray(np.kron(np.eye(heads), np.ones((NA_HEAD_DIM, NA_HEAD_DIM))), BF16)
    w_in_b = w_in[0].astype(BF16)
    qg = jnp.tile(na_q_gain[0].astype(F32), heads)[None]
    kg = jnp.tile(na_k_gain[0].astype(F32), heads)[None]
    u_c, _, k_c, v_c = in_projection(ctx[0], mod(0, 1, 0), mod(0, 1, 1), w_in_b, seg, qg, kg, s5w, naw)
    u_l, q_l, k_l, v_l = in_projection(h0, mod(0, 0, 0), mod(0, 0, 1), w_in_b, seg, qg, kg, s5w, naw)
    mats = s5_matrices(s5_lam_re[0], s5_lam_im[0], s5_log_dt[0], s5_b_re[0], s5_b_im[0],
                       s5_c_re[0], s5_c_im[0], s5_d[0])
    y_s5 = s5_mixer(u_c, u_l, mats)
    head_major = lambda a: jnp.transpose(a.reshape(a.shape[0], heads, NA_HEAD_DIM), (1, 0, 2))
    na = neighbourhood_attention(head_major(q_l), head_major(k_l), head_major(v_l),
                                 head_major(k_c), head_major(v_c), na_bias_table(na_rpb[0]))
    na = jnp.transpose(na, (1, 0, 2)).reshape(t, naw)
    h1, f1a, f1b, lg1, sh1 = _post_call(
        _even_post_kernel, [y_s5, na, h0],
        [s5_w_glu[0].astype(BF16), s5_b_glu[0].astype(F32)[None], w_mix_out[0].astype(BF16), mod(0, 0, 2),
         *ffn_weights(0)], t, d, n_exp)
    h2, a1 = moe_tail(h1, f1a, f1b, lg1, sh1, router_bias[0], w_exp_gate[0], w_exp_up[0], w_exp_down[0],
                      mod(0, 0, 5), mod(1, 0, 0), mod(1, 0, 1))

    zre, zim = time_dft(a1)
    cc, sc = channel_dft_tables(d // FOURIER_GROUPS)
    h3, f3a, f3b, lg3, sh3 = _post_call(
        _odd_post_kernel, [zre, zim, h2],
        [cc, sc, w_fourier_out[0].astype(BF16), b_fourier_out[0].astype(F32)[None], mod(1, 0, 2),
         *ffn_weights(1)], t, d, n_exp)
    zero_row = jnp.zeros((1, d), F32)
    out, _ = moe_tail(h3, f3a, f3b, lg3, sh3, router_bias[1], w_exp_gate[1], w_exp_up[1], w_exp_down[1],
                      mod(1, 0, 5), zero_row, zero_row)
    return out[None]
```

```python
import functools
import math

import numpy as np
import jax
import jax.numpy as jnp
from jax import lax
from jax.experimental import pallas as pl
from jax.experimental.pallas import tpu as pltpu
from jax.experimental.pallas import tpu_sc as plsc

F32 = jnp.float32
BF16 = jnp.bfloat16
I32 = jnp.int32
U32 = jnp.uint32
HIGHEST = lax.Precision.HIGHEST

GRID_W = 64
NORM_EPS = 1e-6
S5_GROUP = 16
S5_STATE = 64
S5_LAMBDA_RE_MAX = -1e-4
S5_CHUNK = 16
NA_HEADS = 8
NA_HEAD_DIM = 64
NA_KH = 8
NA_KW = 16
FOURIER_GROUPS = 4
N_EXPERT_GROUPS = 8
TOPK_GROUPS = 4
TOP_K = 8
ROUTED_SCALE = 2.5
EXPERT_ROWS = 512
SC_WINDOW = 128
NEG_BIG = -1e30

VMEM_LIMIT_BYTES = 56 * 1024 * 1024


def _params(*sem):
    return pltpu.CompilerParams(dimension_semantics=sem or None,
                                vmem_limit_bytes=VMEM_LIMIT_BYTES)


def _rms(x):
    return x * lax.rsqrt(jnp.mean(x * x, axis=-1, keepdims=True) + NORM_EPS)


def _silu(x):
    return x * jax.nn.sigmoid(x)


def _pack_pair(lo, hi):
    lo = lax.bitcast_convert_type(lo.astype(BF16).astype(F32), U32)
    hi = lax.bitcast_convert_type(hi.astype(BF16).astype(F32), U32)
    return (hi & jnp.uint32(0xFFFF0000)) | (lo >> 16)


def _unpack_pair(w):
    return (lax.bitcast_convert_type(w << 16, F32), lax.bitcast_convert_type(w & jnp.uint32(0xFFFF0000), F32))


def _pack_rows(x):
    n = x.shape[1] // 2
    return _pack_pair(x[:, :n], x[:, n:])


def _unpack_rows(w):
    return jnp.concatenate(_unpack_pair(w), axis=1)


def _ada_kernel(c_ref, w_ref, b_ref, o_ref):
    o_ref[0] = jnp.dot(_silu(c_ref[...]), w_ref[0], preferred_element_type=F32,
                       precision=HIGHEST) + b_ref[0]


def adaln_all(cond8, w_ada, b_ada):
    n_layers, d, n6 = w_ada.shape
    tn = n6 // 4
    return pl.pallas_call(
        _ada_kernel,
        grid=(n_layers, n6 // tn),
        in_specs=[pl.BlockSpec((8, d), lambda l, j: (0, 0)),
                  pl.BlockSpec((1, d, tn), lambda l, j: (l, 0, j)),
                  pl.BlockSpec((1, 1, tn), lambda l, j: (l, 0, j))],
        out_specs=pl.BlockSpec((1, 8, tn), lambda l, j: (l, 0, j)),
        out_shape=jax.ShapeDtypeStruct((n_layers, 8, n6), F32),
        compiler_params=_params("parallel", "parallel"),
    )(cond8, w_ada, b_ada.reshape(n_layers, 1, n6))


def _inproj_kernel(x_ref, sh_ref, sc_ref, w_ref, seg_ref, qg_ref, kg_ref,
                   u_ref, q_ref, k_ref, v_ref):
    a = _rms(x_ref[...]) * (1.0 + sc_ref[...]) + sh_ref[...]
    z = jnp.dot(a.astype(BF16), w_ref[...], preferred_element_type=F32)
    s5w = u_ref.shape[1]
    naw = q_ref.shape[1]

    def head_norm(t, gain):
        ss = jnp.dot((t * t).astype(BF16), seg_ref[...], preferred_element_type=F32)
        return t * lax.rsqrt(ss * (1.0 / NA_HEAD_DIM) + NORM_EPS) * gain

    u_ref[...] = z[:, :s5w].astype(BF16)
    q_ref[...] = head_norm(z[:, s5w:s5w + naw], qg_ref[...]).astype(BF16)
    k_ref[...] = head_norm(z[:, s5w + naw:s5w + 2 * naw], kg_ref[...]).astype(BF16)
    v_ref[...] = z[:, s5w + 2 * naw:].astype(BF16)


def in_projection(x, shift, scale, w_in_bf16, seg_ones, q_gain_row, k_gain_row, s5w, naw):
    t, d = x.shape
    tm = min(512, t)
    row = lambda i: (i, 0)
    fixed = lambda i: (0, 0)
    return pl.pallas_call(
        _inproj_kernel,
        grid=(t // tm,),
        in_specs=[pl.BlockSpec((tm, d), row),
                  pl.BlockSpec((1, d), fixed), pl.BlockSpec((1, d), fixed),
                  pl.BlockSpec(w_in_bf16.shape, fixed),
                  pl.BlockSpec(seg_ones.shape, fixed),
                  pl.BlockSpec((1, naw), fixed), pl.BlockSpec((1, naw), fixed)],
        out_specs=[pl.BlockSpec((tm, s5w), row), pl.BlockSpec((tm, naw), row),
                   pl.BlockSpec((tm, naw), row), pl.BlockSpec((tm, naw), row)],
        out_shape=[jax.ShapeDtypeStruct((t, s5w), BF16)] + [jax.ShapeDtypeStruct((t, naw), BF16)] * 3,
        compiler_params=_params("parallel"),
    )(x, shift, scale, w_in_bf16, seg_ones, q_gain_row, k_gain_row)


def s5_matrices(lam_re, lam_im, log_dt, b_re, b_im, c_re, c_im, d_skip):
    L = S5_CHUNK
    taus = jnp.arange(L + 1, dtype=F32)

    def direction(i):
        lam = lax.complex(jnp.minimum(lam_re[i].astype(F32), S5_LAMBDA_RE_MAX), lam_im[i].astype(F32))
        ldt = lam * jnp.exp(log_dt[i].astype(F32))[:, None]
        lam_bar = jnp.exp(ldt)
        b_bar = ((lam_bar - 1.0) / lam)[..., None] * lax.complex(b_re[i].astype(F32), b_im[i].astype(F32))
        cc = lax.complex(c_re[i].astype(F32), c_im[i].astype(F32))
        powers = jnp.exp(ldt[None] * taus[:, None, None])
        resp = jnp.real(jnp.einsum('gcp,tgp,gpd->gtcd', cc, powers[:L], b_bar))
        return powers, b_bar, cc, resp

    pw_f, bb_f, cc_f, k_f = direction(0)
    pw_b, bb_b, cc_b, k_b = direction(1)
    g, p = pw_f.shape[1:]
    c = S5_GROUP
    s_idx = jnp.arange(L)[:, None]
    l_idx = jnp.arange(L)[None, :]
    diff = l_idx - s_idx
    tf = jnp.where((diff >= 0)[None, :, :, None, None], k_f[:, jnp.clip(diff, 0, L - 1)], 0.0)
    tb = jnp.where((diff <= 0)[None, :, :, None, None], k_b[:, jnp.clip(-diff, 0, L - 1)], 0.0)
    skip = (jnp.eye(L, dtype=F32)[None, :, :, None, None]
            * (jnp.eye(c, dtype=F32)[None] * d_skip.astype(F32)[:, :, None])[:, None, None])
    toep = jnp.transpose(tf + tb + skip, (0, 1, 4, 2, 3)).reshape(g, L * c, L * c)

    def state_in(powers_sel, b_bar):
        w = powers_sel[:, :, :, None] * b_bar[None]
        return jnp.transpose(w, (1, 0, 3, 2)).reshape(g, L * c, p)

    wf = state_in(pw_f[L - 1 - jnp.arange(L)], bb_f)
    wb = state_in(pw_b[jnp.arange(L)], bb_b)
    w_state = jnp.concatenate([jnp.real(wf), jnp.imag(wf), jnp.imag(wf), jnp.real(wf),
                               jnp.real(wb), jnp.imag(wb), jnp.imag(wb), jnp.real(wb)], axis=-1)

    def state_out(powers_sel, cc):
        r = powers_sel[:, :, None, :] * cc[None]
        return jnp.transpose(r, (1, 3, 0, 2)).reshape(g, p, L * c)

    rf = state_out(pw_f[1 + jnp.arange(L)], cc_f)
    rb = state_out(pw_b[L - jnp.arange(L)], cc_b)
    r_state = jnp.concatenate([jnp.real(rf), -jnp.imag(rf), jnp.real(rb), -jnp.imag(rb)], axis=1)

    def mult(a):
        ar, ai = jnp.real(a), jnp.imag(a)
        return jnp.stack([jnp.concatenate([ar, ar], -1), jnp.concatenate([-ai, ai], -1),
                          jnp.concatenate([ai, -ai], -1)])

    return toep, w_state, r_state, mult(pw_f[L]), mult(pw_b[L])


def _s5_state_in_kernel(u_ref, w_ref, f1_ref, f2_ref, b1_ref, b2_ref):
    s = jnp.dot(u_ref[0], w_ref[0], preferred_element_type=F32)
    n = f1_ref.shape[2]
    f1_ref[0] = s[:, :n]
    f2_ref[0] = s[:, n:2 * n]
    b1_ref[0] = s[:, 2 * n:3 * n]
    b2_ref[0] = s[:, 3 * n:]


def s5_state_increments(u_chunks, w_state):
    g, nc, lc = u_chunks.shape
    n = w_state.shape[2] // 4
    blk = lambda i: (i, 0, 0)
    out = jax.ShapeDtypeStruct((g, nc, n), F32)
    return pl.pallas_call(
        _s5_state_in_kernel,
        grid=(g,),
        in_specs=[pl.BlockSpec((1, nc, lc), blk), pl.BlockSpec((1, lc, 4 * n), blk)],
        out_specs=[pl.BlockSpec((1, nc, n), blk)] * 4,
        out_shape=[out] * 4,
        compiler_params=_params("parallel"),
    )(u_chunks, w_state)


def _s5_scan_kernel(s1_ref, s2_ref, m_ref, x_ref, v1_ref, v2_ref, *, reverse):
    @pl.when(pl.program_id(0) == 0)
    def _():
        v1_ref[...] = jnp.zeros_like(v1_ref)
        v2_ref[...] = jnp.zeros_like(v2_ref)

    a1, a2, a3 = m_ref[0], m_ref[1], m_ref[2]
    cb = s1_ref.shape[0]

    def body(j, carry):
        v1, v2 = carry
        jj = cb - 1 - j if reverse else j
        x_ref[jj] = v1
        return (a1 * v1 + a2 * v2 + s1_ref[jj], a1 * v2 + a3 * v1 + s2_ref[jj])

    v1, v2 = lax.fori_loop(0, cb, body, (v1_ref[...], v2_ref[...]))
    v1_ref[...] = v1
    v2_ref[...] = v2


def s5_chunk_scan(s1, s2, mult, reverse):
    nc, g, n = s1.shape
    cb = 96 if nc % 96 == 0 else nc
    nb = nc // cb
    blk = (lambda i: (nb - 1 - i, 0, 0)) if reverse else (lambda i: (i, 0, 0))
    return pl.pallas_call(
        functools.partial(_s5_scan_kernel, reverse=reverse),
        grid=(nb,),
        in_specs=[pl.BlockSpec((cb, g, n), blk), pl.BlockSpec((cb, g, n), blk),
                  pl.BlockSpec((3, g, n), lambda i: (0, 0, 0))],
        out_specs=pl.BlockSpec((cb, g, n), blk),
        out_shape=jax.ShapeDtypeStruct((nc, g, n), F32),
        scratch_shapes=[pltpu.VMEM((g, n), F32), pltpu.VMEM((g, n), F32)],
        compiler_params=_params("arbitrary"),
    )(s1, s2, mult)


def _s5_readout_kernel(u_ref, t_ref, xf_ref, xb_ref, r_ref, y_ref, *, first, count):
    u = u_ref[0, first:first + count, :]
    xin = jnp.concatenate([xf_ref[0, first:first + count, :], xb_ref[0, first:first + count, :]], axis=-1)
    y_ref[0] = (jnp.dot(u, t_ref[0], preferred_element_type=F32)
                + jnp.dot(xin.astype(BF16), r_ref[0], preferred_element_type=F32))


def s5_readout(u_chunks, toep, xin_f, xin_b, r_state, first, count):
    g, nc, lc = u_chunks.shape
    n = xin_f.shape[2]
    blk = lambda i: (i, 0, 0)
    return pl.pallas_call(
        functools.partial(_s5_readout_kernel, first=first, count=count),
        grid=(g,),
        in_specs=[pl.BlockSpec((1, nc, lc), blk), pl.BlockSpec((1, lc, lc), blk),
                  pl.BlockSpec((1, nc, n), blk), pl.BlockSpec((1, nc, n), blk),
                  pl.BlockSpec((1, 2 * n, lc), blk)],
        out_specs=pl.BlockSpec((1, count, lc), blk),
        out_shape=jax.ShapeDtypeStruct((g, count, lc), F32),
        compiler_params=_params("parallel"),
    )(u_chunks, toep, xin_f, xin_b, r_state)


def s5_mixer(u_ctx, u_lat, mats):
    toep, w_state, r_state, mult_f, mult_b = mats
    L, c = S5_CHUNK, S5_GROUP
    g = toep.shape[0]
    seq = jnp.concatenate([u_ctx, u_lat, u_ctx], axis=0)
    nc = seq.shape[0] // L
    n_ctx, n_lat = u_ctx.shape[0] // L, u_lat.shape[0] // L
    u_chunks = jnp.transpose(seq.reshape(nc, L, g, c), (2, 0, 1, 3)).reshape(g, nc, L * c)
    incs = s5_state_increments(u_chunks, w_state.astype(BF16))
    f1, f2, b1, b2 = [jnp.transpose(a, (1, 0, 2)) for a in incs]
    xin_f = jnp.transpose(s5_chunk_scan(f1, f2, mult_f, False), (1, 0, 2))
    xin_b = jnp.transpose(s5_chunk_scan(b1, b2, mult_b, True), (1, 0, 2))
    y = s5_readout(u_chunks, toep.astype(BF16), xin_f, xin_b, r_state.astype(BF16), n_ctx, n_lat)
    return jnp.transpose(y.reshape(g, n_lat, L, c), (1, 2, 0, 3)).reshape(n_lat * L, g * c)


def na_bias_table(rpb):
    q_col = np.arange(GRID_W)
    col_start = np.clip(q_col - NA_KW // 2, 0, GRID_W - NA_KW)
    key_col = np.arange(GRID_W)
    off = key_col[None, :] - col_start[:, None]
    valid = (off >= 0) & (off < NA_KW)
    rel_col = np.clip(key_col[None, :] - q_col[:, None] + NA_KW - 1, 0, 2 * NA_KW - 2)
    full = jnp.where(valid[None, None], rpb.astype(F32)[:, :, rel_col], NEG_BIG)
    variants = []
    for d in range(NA_KH):
        rows = [full[:, i - d + NA_KH - 1] for i in range(NA_KH)]
        variants.append(jnp.concatenate(rows, axis=-1))
    return jnp.stack(variants)


def _na_kernel(*refs, scale):
    q_ref = refs[0]
    k_refs = refs[1:1 + NA_KH]
    v_refs = refs[1 + NA_KH:1 + 2 * NA_KH]
    kc_ref, vc_ref, b_ref, o_ref = refs[1 + 2 * NA_KH:]
    q = q_ref[...] * scale
    kk = jnp.concatenate([r[...] for r in k_refs], axis=1)
    vv = jnp.concatenate([r[...] for r in v_refs], axis=1)
    s = jnp.einsum('hqd,hkd->hqk', q, kk, preferred_element_type=F32) + b_ref[0]
    sc = jnp.einsum('hqd,hkd->hqk', q, kc_ref[...], preferred_element_type=F32)
    m = jnp.maximum(jnp.max(s, axis=-1, keepdims=True), jnp.max(sc, axis=-1, keepdims=True))
    p = jnp.exp(s - m)
    pc = jnp.exp(sc - m)
    den = jnp.sum(p, axis=-1, keepdims=True) + jnp.sum(pc, axis=-1, keepdims=True)
    o = (jnp.einsum('hqk,hkd->hqd', p.astype(BF16), vv, preferred_element_type=F32)
         + jnp.einsum('hqk,hkd->hqd', pc.astype(BF16), vc_ref[...], preferred_element_type=F32))
    o_ref[...] = (o / den).astype(o_ref.dtype)


def neighbourhood_attention(q, k, v, k_ctx, v_ctx, bias_table):
    h, t, hd = q.shape
    rows = t // GRID_W
    kh = min(NA_KH, rows)
    assert kh == NA_KH
    first = lambda r: jnp.clip(r - kh // 2, 0, rows - kh)
    row_blk = pl.BlockSpec((h, GRID_W, hd), lambda r: (0, r, 0))
    key_blks = [pl.BlockSpec((h, GRID_W, hd), functools.partial(lambda r, i: (0, first(r) + i, 0), i=i))
                for i in range(kh)]
    ctx_blk = pl.BlockSpec(k_ctx.shape, lambda r: (0, 0, 0))
    bias_blk = pl.BlockSpec((1,) + bias_table.shape[1:], lambda r: (r - first(r), 0, 0, 0))
    return pl.pallas_call(
        functools.partial(_na_kernel, scale=hd ** -0.5),
        grid=(rows,),
        in_specs=[row_blk] + key_blks + key_blks + [ctx_blk, ctx_blk, bias_blk],
        out_specs=row_blk,
        out_shape=jax.ShapeDtypeStruct((h, t, hd), BF16),
        compiler_params=_params("parallel"),
    )(q, *([k] * kh), *([v] * kh), k_ctx, v_ctx, bias_table)


def time_dft_tables(t):
    a_len = 1 << (int(math.log2(t)) // 2)
    b_len = t // a_len
    ka = np.arange(a_len)[:, None]
    tok = b_len * np.arange(a_len)[None, :]
    ang1 = -2.0 * np.pi * ((ka * (tok[None] + np.arange(b_len)[:, None, None])) % t) / t
    stage1 = np.concatenate([np.cos(ang1), np.sin(ang1)], axis=1) / math.sqrt(t)
    ang2 = 2.0 * np.pi * ((np.arange(b_len)[:, None] * np.arange(b_len)[None, :]) % b_len) / b_len
    c2, s2 = np.cos(ang2), np.sin(ang2)
    stage2 = np.block([[c2, s2], [-s2, c2]])
    return jnp.asarray(stage1, BF16), jnp.asarray(stage2, BF16), a_len, b_len


def _time_dft_kernel(x_ref, m_ref, w2_ref, zre_ref, zim_ref, y_ref, z_ref, *, a_len, b_len):
    i = pl.program_id(1)
    bb = m_ref.shape[0]

    def stage1(jb, carry):
        b = i * bb + jb
        xb = x_ref[pl.ds(b, a_len, stride=b_len), :]
        y = jnp.dot(m_ref[jb], xb.astype(BF16), preferred_element_type=F32)
        row = pl.multiple_of(b * a_len, a_len)
        y_ref[pl.ds(row, a_len), :] = _pack_pair(y[:a_len], y[a_len:])
        return carry

    lax.fori_loop(0, bb, stage1, 0, unroll=4)

    @pl.when(i == pl.num_programs(1) - 1)
    def _():
        def stage2(ka, carry):
            rows = pl.ds(ka, b_len, stride=a_len)
            yre, yim = _unpack_pair(y_ref[rows, :])
            y = jnp.concatenate([yre, yim], axis=0).astype(BF16)
            z = jnp.dot(w2_ref[...], y, preferred_element_type=F32)
            z_ref[rows, :] = _pack_pair(z[:b_len], z[b_len:])
            return carry

        lax.fori_loop(0, a_len, stage2, 0, unroll=4)
        zre, zim = _unpack_pair(z_ref[...])
        zre_ref[...] = zre.astype(zre_ref.dtype)
        zim_ref[...] = zim.astype(zim_ref.dtype)


def time_dft(x):
    t, d = x.shape
    stage1, stage2, a_len, b_len = time_dft_tables(t)
    lanes = 128
    bb = min(16, b_len)
    out = jax.ShapeDtypeStruct((t, d), BF16)
    return pl.pallas_call(
        functools.partial(_time_dft_kernel, a_len=a_len, b_len=b_len),
        grid=(d // lanes, b_len // bb),
        in_specs=[pl.BlockSpec((t, lanes), lambda j, i: (0, j)),
                  pl.BlockSpec((bb, 2 * a_len, a_len), lambda j, i: (i, 0, 0)),
                  pl.BlockSpec(stage2.shape, lambda j, i: (0, 0))],
        out_specs=[pl.BlockSpec((t, lanes), lambda j, i: (0, j))] * 2,
        out_shape=[out, out],
        scratch_shapes=[pltpu.VMEM((t, lanes), U32), pltpu.VMEM((t, lanes), U32)],
        compiler_params=_params("parallel", "arbitrary"),
    )(x, stage1, stage2)


def channel_dft_tables(c):
    ang = 2.0 * np.pi * ((np.arange(c)[:, None] * np.arange(c)[None, :]) % c) / c
    return (jnp.asarray(np.cos(ang) / math.sqrt(c), BF16), jnp.asarray(np.sin(ang) / math.sqrt(c), BF16))


def _ffn_prologue(h, shf_ref, scf_ref, wr_ref, wsg_ref, wsu_ref, wsd_ref, h_ref, fa_ref, fb_ref, lg_ref,
                  shared_ref):
    h_ref[...] = h
    f = _rms(h) * (1.0 + scf_ref[...]) + shf_ref[...]
    packed = _pack_rows(f)
    half = packed.shape[1] // 2
    fa_ref[...] = packed[:, :half]
    fb_ref[...] = packed[:, half:]
    lg_ref[...] = lax.dot_general(wr_ref[...], f, (((1,), (1,)), ((), ())),
                                  preferred_element_type=F32, precision=HIGHEST)
    fb = f.astype(BF16)
    hid = (_silu(jnp.dot(fb, wsg_ref[...], preferred_element_type=F32))
           * jnp.dot(fb, wsu_ref[...], preferred_element_type=F32))
    shared_ref[...] = jnp.dot(hid.astype(BF16), wsd_ref[...], preferred_element_type=F32).astype(shared_ref.dtype)


def _gelu_tanh(x):
    return 0.5 * x * (1.0 + jnp.tanh(math.sqrt(2.0 / math.pi) * (x + 0.044715 * (x * x * x))))


def _even_post_kernel(y_ref, na_ref, x_ref, wglu_ref, bglu_ref, wo_ref, gm_ref, *rest):
    g = _gelu_tanh(y_ref[...])
    gate = jax.nn.sigmoid(jnp.dot(g.astype(BF16), wglu_ref[...], preferred_element_type=F32) + bglu_ref[...])
    s5 = (g * gate).astype(BF16)
    w = s5.shape[1]
    mix = (jnp.dot(s5, wo_ref[:w, :], preferred_element_type=F32)
           + jnp.dot(na_ref[...], wo_ref[w:, :], preferred_element_type=F32))
    _ffn_prologue(x_ref[...] + gm_ref[...] * mix, *rest)


def _odd_post_kernel(zre_ref, zim_ref, h_ref_in, cc_ref, sc_ref, wf_ref, bf_ref, gm_ref, *rest):
    c = cc_ref.shape[0]
    parts = []
    for grp in range(zre_ref.shape[1] // c):
        cols = slice(grp * c, (grp + 1) * c)
        parts.append(jnp.dot(zre_ref[:, cols], cc_ref[...], preferred_element_type=F32)
                     + jnp.dot(zim_ref[:, cols], sc_ref[...], preferred_element_type=F32))
    fr = jnp.concatenate(parts, axis=-1).astype(BF16)
    mix = jnp.dot(fr, wf_ref[...], preferred_element_type=F32) + bf_ref[...]
    _ffn_prologue(h_ref_in[...] + gm_ref[...] * mix, *rest)


def _post_call(body, row_inputs, fixed_inputs, t, d, n_exp):
    tm = min(256, t)
    row = lambda i: (i, 0)
    fixed = lambda i: (0, 0)
    in_specs = ([pl.BlockSpec((tm, a.shape[1]), row) for a in row_inputs]
                + [pl.BlockSpec(a.shape, fixed) for a in fixed_inputs])
    return pl.pallas_call(
        body,
        grid=(t // tm,),
        in_specs=in_specs,
        out_specs=[pl.BlockSpec((tm, d), row), pl.BlockSpec((tm, d // 4), row), pl.BlockSpec((tm, d // 4), row),
                   pl.BlockSpec((n_exp, tm), lambda i: (0, i)), pl.BlockSpec((tm, d), row)],
        out_shape=[jax.ShapeDtypeStruct((t, d), F32),
                   jax.ShapeDtypeStruct((t, d // 4), U32), jax.ShapeDtypeStruct((t, d // 4), U32),
                   jax.ShapeDtypeStruct((n_exp, t), F32), jax.ShapeDtypeStruct((t, d), BF16)],
        compiler_params=_params("parallel"),
    )(*row_inputs, *fixed_inputs)


def _route_kernel(lg_ref, bias_ref, tri_ref, idx_ref, gate_ref, rank_ref, cnt_ref, run_ref):
    @pl.when(pl.program_id(0) == 0)
    def _():
        run_ref[...] = jnp.zeros_like(run_ref)

    scores = jax.nn.sigmoid(lg_ref[...])
    n_exp, tb = scores.shape
    sel = scores + bias_ref[...]
    gsz = n_exp // N_EXPERT_GROUPS
    member = lax.broadcasted_iota(I32, (gsz, tb), 0)
    gscore = []
    for grp in range(N_EXPERT_GROUPS):
        xg = sel[grp * gsz:(grp + 1) * gsz, :]
        m1 = jnp.max(xg, axis=0, keepdims=True)
        first = jnp.min(jnp.where(xg == m1, member, gsz), axis=0, keepdims=True)
        m2 = jnp.max(jnp.where(member == first, -jnp.inf, xg), axis=0, keepdims=True)
        gscore.append(m1 + m2)
    keep_rows = []
    for grp in range(N_EXPERT_GROUPS):
        beaten = jnp.zeros((1, tb), F32)
        for other in range(N_EXPERT_GROUPS):
            if other == grp:
                continue
            wins = (gscore[other] >= gscore[grp]) if other < grp else (gscore[other] > gscore[grp])
            beaten = beaten + jnp.where(wins, 1.0, 0.0)
        keep_rows.append(jnp.broadcast_to(beaten < TOPK_GROUPS, (gsz, tb)))
    masked = jnp.where(jnp.concatenate(keep_rows, axis=0), sel, -jnp.inf)

    expert = lax.broadcasted_iota(I32, (n_exp, tb), 0)
    picks, gates, hots = [], [], []
    chosen = jnp.zeros((n_exp, tb), F32)
    for _ in range(TOP_K):
        m = jnp.max(masked, axis=0, keepdims=True)
        pick = jnp.min(jnp.where(masked == m, expert, n_exp), axis=0, keepdims=True)
        hot = expert == pick
        picks.append(pick)
        hots.append(hot)
        gates.append(jnp.sum(jnp.where(hot, scores, 0.0), axis=0, keepdims=True))
        chosen = jnp.where(hot, 1.0, chosen)
        masked = jnp.where(hot, -jnp.inf, masked)
    total = gates[0]
    for gk in gates[1:]:
        total = total + gk
    ahead = jnp.dot(chosen.astype(BF16), tri_ref[...], preferred_element_type=F32) + run_ref[...]
    for k in range(TOP_K):
        idx_ref[k:k + 1, :] = picks[k]
        gate_ref[k:k + 1, :] = ROUTED_SCALE * gates[k] / total
        rank_ref[k:k + 1, :] = jnp.sum(jnp.where(hots[k], ahead, 0.0), axis=0, keepdims=True).astype(I32)
    run_ref[...] = run_ref[...] + jnp.sum(chosen, axis=1, keepdims=True)
    cnt_ref[...] = jnp.broadcast_to(run_ref[...], cnt_ref.shape)


def route(logits_t, router_bias):
    n_exp, t = logits_t.shape
    tb = min(512, t)
    tri = jnp.asarray(np.triu(np.ones((tb, tb), np.float32), k=1), BF16)
    tok = lambda i: (0, i)
    idx, gate, rank, cnt = pl.pallas_call(
        _route_kernel,
        grid=(t // tb,),
        in_specs=[pl.BlockSpec((n_exp, tb), tok), pl.BlockSpec((n_exp, 1), lambda i: (0, 0)),
                  pl.BlockSpec((tb, tb), lambda i: (0, 0))],
        out_specs=[pl.BlockSpec((TOP_K, tb), tok)] * 3 + [pl.BlockSpec((n_exp, 128), lambda i: (0, 0))],
        out_shape=[jax.ShapeDtypeStruct((TOP_K, t), I32), jax.ShapeDtypeStruct((TOP_K, t), F32),
                   jax.ShapeDtypeStruct((TOP_K, t), I32), jax.ShapeDtypeStruct((n_exp, 128), F32)],
        scratch_shapes=[pltpu.VMEM((n_exp, 1), F32)],
        compiler_params=_params("arbitrary"),
    )(logits_t, router_bias.astype(F32).reshape(n_exp, 1), tri)
    return idx, gate, rank, cnt[:, 0].astype(I32)


def dispatch_plan(idx, rank, counts, n_blocks):
    n_exp = counts.shape[0]
    padded = (counts + EXPERT_ROWS - 1) // EXPERT_ROWS * EXPERT_ROWS
    pad_end = jnp.cumsum(padded)
    pad_start = pad_end - padded
    experts_iota = jnp.arange(n_exp, dtype=I32)
    dest = jnp.sum(jnp.where(idx[..., None] == experts_iota, pad_start, 0), axis=-1) + rank
    n_valid = (pad_end[-1] // EXPERT_ROWS).astype(I32)
    blk = jnp.minimum(jnp.arange(n_blocks, dtype=I32), n_valid - 1)
    blk_e = jnp.sum((pad_end[None, :] <= blk[:, None] * EXPERT_ROWS).astype(I32), axis=1)
    return dest.astype(I32), jnp.minimum(blk_e, n_exp - 1).astype(I32), n_valid.reshape(1)


def _sc_mesh():
    return plsc.VectorSubcoreMesh(core_axis_name="core", subcore_axis_name="subcore")


def dispatch(f, dest, cap):
    t, w = f.shape
    n_choice = dest.shape[0]

    @functools.partial(pl.kernel, out_type=jax.ShapeDtypeStruct((cap, w), f.dtype), mesh=_sc_mesh(),
                       scratch_types=[])
    def scatter_rows(x_hbm, i_hbm, o_hbm):
        def body(x_vmem, i_vmem):
            pltpu.sync_copy(x_vmem, o_hbm.at[i_vmem.at[0]])

        pltpu.emit_pipeline(
            body,
            grid=(t // SC_WINDOW, n_choice),
            in_specs=[pl.BlockSpec((SC_WINDOW, w), lambda i, k: (i, 0)),
                      pl.BlockSpec((1, SC_WINDOW), lambda i, k: (k, i))],
            out_specs=[],
            core_axis_name=("core", "subcore"),
            dimension_semantics=(pltpu.PARALLEL, pltpu.ARBITRARY),
        )(x_hbm, i_hbm)

    return scatter_rows(f, dest)


def gather_rows(rows, index_row):
    n = index_row.shape[1]
    w = rows.shape[1]

    @functools.partial(pl.kernel, out_type=jax.ShapeDtypeStruct((n, w), rows.dtype), mesh=_sc_mesh(),
                       scratch_types=[])
    def gather(y_hbm, i_hbm, o_hbm):
        def body(i_vmem, o_vmem):
            pltpu.sync_copy(y_hbm.at[i_vmem.at[0]], o_vmem)

        pltpu.emit_pipeline(
            body,
            grid=(n // SC_WINDOW,),
            in_specs=[pl.BlockSpec((1, SC_WINDOW), lambda i: (0, i))],
            out_specs=[pl.BlockSpec((SC_WINDOW, w), lambda i: (i, 0))],
            core_axis_name=("core", "subcore"),
            dimension_semantics=(pltpu.PARALLEL,),
        )(i_hbm, o_hbm)

    return gather(rows, index_row)


def _experts_kernel(be_ref, nv_ref, xa_ref, xb_ref, wg_ref, wu_ref, wd_ref, ya_ref, yb_ref, wg_bf, wu_bf, wd_bf):
    b = pl.program_id(0)

    @pl.when((b == 0) | (be_ref[b] != be_ref[jnp.maximum(b - 1, 0)]))
    def _():
        wg_bf[...] = wg_ref[0].astype(BF16)
        wu_bf[...] = wu_ref[0].astype(BF16)
        wd_bf[...] = wd_ref[0].astype(BF16)

    @pl.when(b < nv_ref[0])
    def _():
        x = _unpack_rows(jnp.concatenate([xa_ref[...], xb_ref[...]], axis=1)).astype(BF16)
        hid = (_silu(jnp.dot(x, wg_bf[...], preferred_element_type=F32))
               * jnp.dot(x, wu_bf[...], preferred_element_type=F32))
        packed = _pack_rows(jnp.dot(hid.astype(BF16), wd_bf[...], preferred_element_type=F32))
        half = packed.shape[1] // 2
        ya_ref[...] = packed[:, :half]
        yb_ref[...] = packed[:, half:]

    @pl.when(b >= nv_ref[0])
    def _():
        ya_ref[...] = jnp.zeros_like(ya_ref)
        yb_ref[...] = jnp.zeros_like(yb_ref)


def experts(rows_a, rows_b, blk_e, n_valid, w_gate, w_up, w_down):
    cap, dq = rows_a.shape
    _, d, ff = w_gate.shape
    n_blocks = cap // EXPERT_ROWS
    live = lambda b, be, nv: (jnp.minimum(b, nv[0] - 1), 0)
    out_blk = pl.BlockSpec((EXPERT_ROWS, dq), lambda b, be, nv: (b, 0))
    out = jax.ShapeDtypeStruct((cap, dq), U32)
    return pl.pallas_call(
        _experts_kernel,
        grid_spec=pltpu.PrefetchScalarGridSpec(
            num_scalar_prefetch=2,
            grid=(n_blocks,),
            in_specs=[pl.BlockSpec((EXPERT_ROWS, dq), live), pl.BlockSpec((EXPERT_ROWS, dq), live),
                      pl.BlockSpec((1, d, ff), lambda b, be, nv: (be[b], 0, 0)),
                      pl.BlockSpec((1, d, ff), lambda b, be, nv: (be[b], 0, 0)),
                      pl.BlockSpec((1, ff, d), lambda b, be, nv: (be[b], 0, 0))],
            out_specs=[out_blk, out_blk],
            scratch_shapes=[pltpu.VMEM((d, ff), BF16), pltpu.VMEM((d, ff), BF16), pltpu.VMEM((ff, d), BF16)],
        ),
        out_shape=[out, out],
        compiler_params=_params("arbitrary"),
    )(blk_e, n_valid, rows_a, rows_b, w_gate, w_up, w_down)


def _combine_kernel(gate_ref, h_ref, shared_ref, gf_ref, sh_ref, sc_ref, *rest):
    ya_refs, yb_refs, (h_out, a_out) = rest[:TOP_K], rest[TOP_K:2 * TOP_K], rest[2 * TOP_K:]
    routed = None
    for k in range(TOP_K):
        y = _unpack_rows(jnp.concatenate([ya_refs[k][0], yb_refs[k][0]], axis=1))
        routed = gate_ref[:, k:k + 1] * y if routed is None else routed + gate_ref[:, k:k + 1] * y
    h = h_ref[...] + gf_ref[...] * (routed + shared_ref[...].astype(F32))
    h_out[...] = h
    a_out[...] = _rms(h) * (1.0 + sc_ref[...]) + sh_ref[...]


def combine(y_a, y_b, dest, gate_tk, h, shared, gate_ffn, next_shift, next_scale):
    t, d = h.shape
    n_choice = dest.shape[0]
    flat = dest.reshape(1, n_choice * t)
    picked_a = gather_rows(y_a, flat).reshape(n_choice, t, d // 4)
    picked_b = gather_rows(y_b, flat).reshape(n_choice, t, d // 4)
    tm = min(256, t)
    row = lambda i: (i, 0)
    fixed = lambda i: (0, 0)
    choice = [pl.BlockSpec((1, tm, d // 4), functools.partial(lambda i, k: (k, i, 0), k=k))
              for k in range(n_choice)] * 2
    return pl.pallas_call(
        _combine_kernel,
        grid=(t // tm,),
        in_specs=[pl.BlockSpec((tm, n_choice), row), pl.BlockSpec((tm, d), row), pl.BlockSpec((tm, d), row),
                  pl.BlockSpec((1, d), fixed), pl.BlockSpec((1, d), fixed), pl.BlockSpec((1, d), fixed)] + choice,
        out_specs=[pl.BlockSpec((tm, d), row), pl.BlockSpec((tm, d), row)],
        out_shape=[jax.ShapeDtypeStruct((t, d), F32), jax.ShapeDtypeStruct((t, d), F32)],
        compiler_params=_params("parallel"),
    )(gate_tk, h, shared, gate_ffn, next_shift, next_scale, *([picked_a] * n_choice), *([picked_b] * n_choice))


def moe_tail(h, f_a, f_b, logits_t, shared, router_bias, w_gate, w_up, w_down, gate_ffn, next_shift, next_scale):
    t, d = h.shape
    n_exp = w_gate.shape[0]
    n_blocks = -(-(t * TOP_K + n_exp * (EXPERT_ROWS - 1)) // EXPERT_ROWS)
    cap = n_blocks * EXPERT_ROWS
    idx, gate, rank, counts = route(logits_t, router_bias)
    dest, blk_e, n_valid = dispatch_plan(idx, rank, counts, n_blocks)
    y_a, y_b = experts(dispatch(f_a, dest, cap), dispatch(f_b, dest, cap), blk_e, n_valid, w_gate, w_up, w_down)
    return combine(y_a, y_b, dest, gate.T, h, shared, gate_ffn, next_shift, next_scale)


def kernel(x, c, ctx, c_ctx, w_ada, b_ada, w_in, s5_lam_re, s5_lam_im, s5_log_dt, s5_b_re, s5_b_im,
           s5_c_re, s5_c_im, s5_d, s5_w_glu, s5_b_glu, na_q_gain, na_k_gain, na_rpb, w_mix_out,
           w_fourier_out, b_fourier_out, w_router, router_bias, w_exp_gate, w_exp_up, w_exp_down,
           w_sh_gate, w_sh_up, w_sh_down):
    bsz, t, d = x.shape
    assert bsz == 1 and w_ada.shape[0] == 2
    n_exp = w_router.shape[2]
    s5w = s5_w_glu.shape[1]
    naw = w_in.shape[2] - s5w
    naw //= 3
    heads = naw // NA_HEAD_DIM

    cond8 = jnp.zeros((8, d), F32).at[0].set(c[0].astype(F32)).at[1].set(c_ctx.astype(F32))
    ada = adaln_all(cond8, w_ada, b_ada)
    mod = lambda layer, who, j: ada[layer, who:who + 1, j * d:(j + 1) * d]

    def ffn_weights(i):
        return (mod(i, 0, 3), mod(i, 0, 4), jnp.transpose(w_router[i]).astype(F32),
                w_sh_gate[i].astype(BF16), w_sh_up[i].astype(BF16), w_sh_down[i].astype(BF16))

    h0 = x[0]
    seg = jnp.asarray(np.kron(np.eye(heads), np.ones((NA_HEAD_DIM, NA_HEAD_DIM))), BF16)
    w_in_b = w_in[0].astype(BF16)
    qg = jnp.tile(na_q_gain[0].astype(F32), heads)[None]
    kg = jnp.tile(na_k_gain[0].astype(F32), heads)[None]
    u_c, _, k_c, v_c = in_projection(ctx[0], mod(0, 1, 0), mod(0, 1, 1), w_in_b, seg, qg, kg, s5w, naw)
    u_l, q_l, k_l, v_l = in_projection(h0, mod(0, 0, 0), mod(0, 0, 1), w_in_b, seg, qg, kg, s5w, naw)
    mats = s5_matrices(s5_lam_re[0], s5_lam_im[0], s5_log_dt[0], s5_b_re[0], s5_b_im[0],
                       s5_c_re[0], s5_c_im[0], s5_d[0])
    y_s5 = s5_mixer(u_c, u_l, mats)
    head_major = lambda a: jnp.transpose(a.reshape(a.shape[0], heads, NA_HEAD_DIM), (1, 0, 2))
    na = neighbourhood_attention(head_major(q_l), head_major(k_l), head_major(v_l),
                                 head_major(k_c), head_major(v_c), na_bias_table(na_rpb[0]))
    na = jnp.transpose(na, (1, 0, 2)).reshape(t, naw)
    h1, f1a, f1b, lg1, sh1 = _post_call(
        _even_post_kernel, [y_s5, na, h0],
        [s5_w_glu[0].astype(BF16), s5_b_glu[0].astype(F32)[None], w_mix_out[0].astype(BF16), mod(0, 0, 2),
         *ffn_weights(0)], t, d, n_exp)
    h2, a1 = moe_tail(h1, f1a, f1b, lg1, sh1, router_bias[0], w_exp_gate[0], w_exp_up[0], w_exp_down[0],
                      mod(0, 0, 5), mod(1, 0, 0), mod(1, 0, 1))

    zre, zim = time_dft(a1)
    cc, sc = channel_dft_tables(d // FOURIER_GROUPS)
    h3, f3a, f3b, lg3, sh3 = _post_call(
        _odd_post_kernel, [zre, zim, h2],
        [cc, sc, w_fourier_out[0].astype(BF16), b_fourier_out[0].astype(F32)[None], mod(1, 0, 2),
         *ffn_weights(1)], t, d, n_exp)
    zero_row = jnp.zeros((1, d), F32)
    out, _ = moe_tail(h3, f3a, f3b, lg3, sh3, router_bias[1], w_exp_gate[1], w_exp_up[1], w_exp_down[1],
                      mod(1, 0, 5), zero_row, zero_row)
    return out[None]
```

```python
import functools
import math

import numpy as np
import jax
import jax.numpy as jnp
from jax import lax
from jax.experimental import pallas as pl
from jax.experimental.pallas import tpu as pltpu
from jax.experimental.pallas import tpu_sc as plsc

F32 = jnp.float32
BF16 = jnp.bfloat16
I32 = jnp.int32
U32 = jnp.uint32
HIGHEST = lax.Precision.HIGHEST

LANES = 128
GRID_W = 64
NORM_EPS = 1e-6
S5_GROUP = 16
S5_STATE = 64
S5_LAMBDA_RE_MAX = -1e-4
S5_CHUNK = 16
S5_TILE = 128
NA_HEADS = 8
NA_HEAD_DIM = 64
NA_KH = 8
NA_KW = 16
FOURIER_GROUPS = 4
N_EXPERT_GROUPS = 8
TOPK_GROUPS = 4
TOP_K = 8
ROUTED_SCALE = 2.5
EXPERT_ROWS = 512
SC_WINDOW = 128
NEG_BIG = -1e30

VMEM_LIMIT_BYTES = 56 * 1024 * 1024


def _params(*sem):
    return pltpu.CompilerParams(dimension_semantics=sem or None,
                                vmem_limit_bytes=VMEM_LIMIT_BYTES)


def _rms(x):
    return x * lax.rsqrt(jnp.mean(x * x, axis=-1, keepdims=True) + NORM_EPS)


def _silu(x):
    return x * jax.nn.sigmoid(x)


def _pack_pair(lo, hi):
    lo = lax.bitcast_convert_type(lo.astype(BF16).astype(F32), U32)
    hi = lax.bitcast_convert_type(hi.astype(BF16).astype(F32), U32)
    return (hi & jnp.uint32(0xFFFF0000)) | (lo >> 16)


def _unpack_pair(w):
    return (lax.bitcast_convert_type(w << 16, F32), lax.bitcast_convert_type(w & jnp.uint32(0xFFFF0000), F32))


def _pack_rows(x):
    n = x.shape[1] // 2
    return _pack_pair(x[:, :n], x[:, n:])


def _unpack_rows(w):
    return jnp.concatenate(_unpack_pair(w), axis=1)


def _ada_kernel(c_ref, w_ref, b_ref, o_ref):
    o_ref[0] = jnp.dot(_silu(c_ref[...]), w_ref[0], preferred_element_type=F32,
                       precision=HIGHEST) + b_ref[0]


def adaln_all(cond8, w_ada, b_ada):
    n_layers, d, n6 = w_ada.shape
    tn = n6 // 4
    return pl.pallas_call(
        _ada_kernel,
        grid=(n_layers, n6 // tn),
        in_specs=[pl.BlockSpec((8, d), lambda l, j: (0, 0)),
                  pl.BlockSpec((1, d, tn), lambda l, j: (l, 0, j)),
                  pl.BlockSpec((1, 1, tn), lambda l, j: (l, 0, j))],
        out_specs=pl.BlockSpec((1, 8, tn), lambda l, j: (l, 0, j)),
        out_shape=jax.ShapeDtypeStruct((n_layers, 8, n6), F32),
        compiler_params=_params("parallel", "parallel"),
    )(cond8, w_ada, b_ada.reshape(n_layers, 1, n6))


def _inproj_kernel(x_ref, sh_ref, sc_ref, w_ref, seg_ref, qg_ref, kg_ref,
                   u_ref, q_ref, k_ref, v_ref):
    a = _rms(x_ref[...]) * (1.0 + sc_ref[...]) + sh_ref[...]
    z = jnp.dot(a.astype(BF16), w_ref[...], preferred_element_type=F32)
    s5w = u_ref.shape[1]
    naw = q_ref.shape[1]

    def head_norm(t, gain):
        ss = jnp.dot((t * t).astype(BF16), seg_ref[...], preferred_element_type=F32)
        return t * lax.rsqrt(ss * (1.0 / NA_HEAD_DIM) + NORM_EPS) * gain

    u_ref[...] = z[:, :s5w]
    q_ref[...] = head_norm(z[:, s5w:s5w + naw], qg_ref[...]).astype(BF16)
    k_ref[...] = head_norm(z[:, s5w + naw:s5w + 2 * naw], kg_ref[...]).astype(BF16)
    v_ref[...] = z[:, s5w + 2 * naw:].astype(BF16)


def in_projection(x, shift, scale, w_in_bf16, seg_ones, q_gain_row, k_gain_row, s5w, naw):
    t, d = x.shape
    tm = min(512, t)
    row = lambda i: (i, 0)
    fixed = lambda i: (0, 0)
    return pl.pallas_call(
        _inproj_kernel,
        grid=(t // tm,),
        in_specs=[pl.BlockSpec((tm, d), row),
                  pl.BlockSpec((1, d), fixed), pl.BlockSpec((1, d), fixed),
                  pl.BlockSpec(w_in_bf16.shape, fixed),
                  pl.BlockSpec(seg_ones.shape, fixed),
                  pl.BlockSpec((1, naw), fixed), pl.BlockSpec((1, naw), fixed)],
        out_specs=[pl.BlockSpec((tm, s5w), row), pl.BlockSpec((tm, naw), row),
                   pl.BlockSpec((tm, naw), row), pl.BlockSpec((tm, naw), row)],
        out_shape=[jax.ShapeDtypeStruct((t, s5w), F32)] + [jax.ShapeDtypeStruct((t, naw), BF16)] * 3,
        compiler_params=_params("parallel"),
    )(x, shift, scale, w_in_bf16, seg_ones, q_gain_row, k_gain_row)


def s5_matrices(lam_re, lam_im, log_dt, b_re, b_im, c_re, c_im, d_skip):
    L = S5_CHUNK
    taus = jnp.arange(L + 1, dtype=F32)

    def direction(i):
        lam = lax.complex(jnp.minimum(lam_re[i].astype(F32), S5_LAMBDA_RE_MAX), lam_im[i].astype(F32))
        ldt = lam * jnp.exp(log_dt[i].astype(F32))[:, None]
        lam_bar = jnp.exp(ldt)
        b_bar = ((lam_bar - 1.0) / lam)[..., None] * lax.complex(b_re[i].astype(F32), b_im[i].astype(F32))
        cc = lax.complex(c_re[i].astype(F32), c_im[i].astype(F32))
        powers = jnp.exp(ldt[None] * taus[:, None, None])
        resp = jnp.real(jnp.einsum('gcp,tgp,gpd->gtcd', cc, powers[:L], b_bar, precision=HIGHEST))
        return powers, b_bar, cc, resp

    pw_f, bb_f, cc_f, k_f = direction(0)
    pw_b, bb_b, cc_b, k_b = direction(1)
    g, p = pw_f.shape[1:]
    c = S5_GROUP
    diff = np.arange(L)[None, :] - np.arange(L)[:, None]
    lag = np.arange(L)[:, None, None]
    tf = jnp.einsum('tsl,gtcd->gslcd', jnp.asarray(lag == diff[None], F32), k_f, precision=HIGHEST)
    tb = jnp.einsum('tsl,gtcd->gslcd', jnp.asarray(lag == -diff[None], F32), k_b, precision=HIGHEST)
    skip = (jnp.eye(L, dtype=F32)[None, :, :, None, None]
            * (jnp.eye(c, dtype=F32)[None] * d_skip.astype(F32)[:, :, None])[:, None, None])
    toep = jnp.transpose(tf + tb + skip, (0, 1, 4, 2, 3)).reshape(g, L * c, L * c)

    def state_in(powers_sel, b_bar):
        w = powers_sel[:, :, :, None] * b_bar[None]
        return jnp.transpose(w, (1, 0, 3, 2)).reshape(g, L * c, p)

    wf = state_in(pw_f[L - 1 - jnp.arange(L)], bb_f)
    wb = state_in(pw_b[jnp.arange(L)], bb_b)
    w_state = jnp.concatenate([jnp.real(wf), jnp.imag(wf), jnp.imag(wf), jnp.real(wf),
                               jnp.real(wb), jnp.imag(wb), jnp.imag(wb), jnp.real(wb)], axis=-1)

    def state_out(powers_sel, cc):
        r = powers_sel[:, :, None, :] * cc[None]
        return jnp.transpose(r, (1, 3, 0, 2)).reshape(g, p, L * c)

    rf = state_out(pw_f[1 + jnp.arange(L)], cc_f)
    rb = state_out(pw_b[L - jnp.arange(L)], cc_b)
    r_state = jnp.concatenate([jnp.real(rf), -jnp.imag(rf), jnp.real(rb), -jnp.imag(rb)], axis=1)

    def mult(a):
        ar, ai = jnp.real(a), jnp.imag(a)
        return jnp.stack([jnp.concatenate([ar, ar], -1), jnp.concatenate([-ai, ai], -1),
                          jnp.concatenate([ai, -ai], -1)])

    return toep, w_state, r_state, mult(pw_f[L]), mult(pw_b[L])


def _s5_pack_kernel(*refs):
    u_refs, (wt_ref, ut_ref, f1_ref, f2_ref, b1_ref, b2_ref) = refs[:-6], refs[-6:]
    L = S5_CHUNK
    g, lc, nck = ut_ref.shape
    c = lc // L
    gs = g // len(u_refs)
    for s in range(L):
        for j, u_ref in enumerate(u_refs):
            step_s = u_ref[pl.ds(s, nck, stride=L), :]
            ut_ref[j * gs:(j + 1) * gs, s * c:(s + 1) * c, :] = (
                jnp.transpose(step_s).astype(BF16).reshape(gs, c, nck))
    n = f1_ref.shape[2]
    for gi in range(g):
        inc = jnp.dot(wt_ref[gi], ut_ref[gi], preferred_element_type=F32)
        for j, ref in enumerate((f1_ref, f2_ref, b1_ref, b2_ref)):
            ref[:, gi, :] = jnp.transpose(inc[j * n:(j + 1) * n, :])


def s5_pack(u, wt_state):
    t, w = u.shape
    g, n4, lc = wt_state.shape
    nc = t // S5_CHUNK
    tile = min(S5_TILE, nc)
    inc = jax.ShapeDtypeStruct((nc, g, n4 // 4), F32)
    inc_blk = pl.BlockSpec((tile, g, n4 // 4), lambda i: (i, 0, 0))
    return pl.pallas_call(
        _s5_pack_kernel,
        grid=(nc // tile,),
        in_specs=[pl.BlockSpec((tile * S5_CHUNK, LANES), functools.partial(lambda i, j: (i, j), j=j))
                  for j in range(w // LANES)] + [pl.BlockSpec(wt_state.shape, lambda i: (0, 0, 0))],
        out_specs=[pl.BlockSpec((g, lc, tile), lambda i: (0, 0, i))] + [inc_blk] * 4,
        out_shape=[jax.ShapeDtypeStruct((g, lc, nc), BF16)] + [inc] * 4,
        compiler_params=_params("parallel"),
    )(*([u] * (w // LANES)), wt_state)


def _s5_scan_kernel(s1_ref, s2_ref, m_ref, init_ref, x_ref, last_ref, v1_ref, v2_ref, *, reverse):
    @pl.when(pl.program_id(0) == 0)
    def _():
        v1_ref[...] = init_ref[0]
        v2_ref[...] = init_ref[1]

    a1, a2, a3 = m_ref[0], m_ref[1], m_ref[2]
    cb = s1_ref.shape[0]

    def body(j, carry):
        v1, v2 = carry
        jj = cb - 1 - j if reverse else j
        x_ref[jj] = v1
        return (a1 * v1 + a2 * v2 + s1_ref[jj], a1 * v2 + a3 * v1 + s2_ref[jj])

    v1, v2 = lax.fori_loop(0, cb, body, (v1_ref[...], v2_ref[...]))
    v1_ref[...] = v1
    v2_ref[...] = v2
    last_ref[...] = v1


def s5_chunk_scan(s1, s2, mult, init, reverse):
    nc, g, n = s1.shape
    cb = min(S5_TILE, nc)
    nb = nc // cb
    blk = (lambda i: (nb - 1 - i, 0, 0)) if reverse else (lambda i: (i, 0, 0))
    return pl.pallas_call(
        functools.partial(_s5_scan_kernel, reverse=reverse),
        grid=(nb,),
        in_specs=[pl.BlockSpec((cb, g, n), blk), pl.BlockSpec((cb, g, n), blk),
                  pl.BlockSpec((3, g, n), lambda i: (0, 0, 0)), pl.BlockSpec((2, g, n), lambda i: (0, 0, 0))],
        out_specs=[pl.BlockSpec((cb, g, n), blk), pl.BlockSpec((g, n), lambda i: (0, 0))],
        out_shape=[jax.ShapeDtypeStruct((nc, g, n), F32), jax.ShapeDtypeStruct((g, n), F32)],
        scratch_shapes=[pltpu.VMEM((g, n), F32), pltpu.VMEM((g, n), F32)],
        compiler_params=_params("arbitrary"),
    )(s1, s2, mult, init)


def _s5_readout_kernel(ut_ref, tt_ref, rt_ref, xf_ref, xb_ref, y_ref, yt_ref, *slab_refs):
    L = S5_CHUNK
    g, lc, nck = ut_ref.shape
    c = lc // L
    gs = g // len(slab_refs)
    for gi in range(g):
        xin_t = jnp.concatenate([jnp.transpose(xf_ref[:, gi, :]), jnp.transpose(xb_ref[:, gi, :])], axis=0)
        yt_ref[gi] = (jnp.dot(tt_ref[gi], ut_ref[gi], preferred_element_type=F32)
                      + jnp.dot(rt_ref[gi], xin_t.astype(BF16), preferred_element_type=F32))
    for j, slab in enumerate(slab_refs):
        for l in range(L):
            step_l = yt_ref[j * gs:(j + 1) * gs, l * c:(l + 1) * c, :].reshape(gs * c, nck)
            slab[pl.ds(l, nck, stride=L), :] = jnp.transpose(step_l)
        y_ref[:, j * gs * c:(j + 1) * gs * c] = slab[...]


def s5_readout(ut, toep_t, r_state_t, xin_f, xin_b):
    g, lc, nc = ut.shape
    n = xin_f.shape[2]
    tile = min(S5_TILE, nc)
    fixed = lambda i: (0, 0, 0)
    state_blk = pl.BlockSpec((tile, g, n), lambda i: (i, 0, 0))
    return pl.pallas_call(
        _s5_readout_kernel,
        grid=(nc // tile,),
        in_specs=[pl.BlockSpec((g, lc, tile), lambda i: (0, 0, i)),
                  pl.BlockSpec(toep_t.shape, fixed), pl.BlockSpec(r_state_t.shape, fixed), state_blk, state_blk],
        out_specs=pl.BlockSpec((tile * S5_CHUNK, g * lc // S5_CHUNK), lambda i: (i, 0)),
        out_shape=jax.ShapeDtypeStruct((nc * S5_CHUNK, g * lc // S5_CHUNK), F32),
        scratch_shapes=[pltpu.VMEM((g, lc, tile), F32)]
        + [pltpu.VMEM((tile * S5_CHUNK, LANES), F32)] * (g * lc // S5_CHUNK // LANES),
        compiler_params=_params("parallel"),
    )(ut, toep_t, r_state_t, xin_f, xin_b)


def s5_mixer(u_ctx, u_lat, mats):
    toep, w_state, r_state, mult_f, mult_b = mats
    L = S5_CHUNK
    g, n = mult_f.shape[1:]
    swap = lambda a: jnp.transpose(a, (0, 2, 1)).astype(BF16)
    wt_state, toep_t, r_state_t = swap(w_state), swap(toep), swap(r_state)
    halves = lambda v: jnp.stack([v, jnp.roll(v, n // 2, axis=-1)])
    n_ctx = u_ctx.shape[0] // L
    ctx_chunks = -(-(n_ctx + 1) // S5_TILE) * S5_TILE
    ctx_pad = jnp.zeros((ctx_chunks * L, u_ctx.shape[1]), F32).at[:u_ctx.shape[0]].set(u_ctx)
    _, cf1, cf2, cb1, cb2 = s5_pack(ctx_pad, wt_state)
    zero = jnp.zeros((2, g, n), F32)
    ctx_f, _ = s5_chunk_scan(cf1, cf2, mult_f, zero, False)
    _, ctx_b_last = s5_chunk_scan(cb1, cb2, mult_b, zero, True)
    ut, f1, f2, b1, b2 = s5_pack(u_lat, wt_state)
    xin_f, _ = s5_chunk_scan(f1, f2, mult_f, halves(ctx_f[n_ctx]), False)
    xin_b, _ = s5_chunk_scan(b1, b2, mult_b, halves(ctx_b_last), True)
    return s5_readout(ut, toep_t, r_state_t, xin_f, xin_b)


def na_bias_table(rpb):
    q_col = np.arange(GRID_W)
    col_start = np.clip(q_col - NA_KW // 2, 0, GRID_W - NA_KW)
    key_col = np.arange(GRID_W)
    off = key_col[None, :] - col_start[:, None]
    valid = (off >= 0) & (off < NA_KW)
    rel_col = np.clip(key_col[None, :] - q_col[:, None] + NA_KW - 1, 0, 2 * NA_KW - 2)
    pick = jnp.asarray(np.arange(2 * NA_KW - 1)[:, None, None] == rel_col[None], F32)
    full = jnp.where(valid[None, None], jnp.einsum('hrj,jck->hrck', rpb.astype(F32), pick, precision=HIGHEST),
                     NEG_BIG)
    variants = []
    for d in range(NA_KH):
        rows = [full[:, i - d + NA_KH - 1] for i in range(NA_KH)]
        variants.append(jnp.concatenate(rows, axis=-1))
    return jnp.stack(variants)


def _na_kernel(*refs, scale):
    q_ref = refs[0]
    k_refs = refs[1:1 + NA_KH]
    v_refs = refs[1 + NA_KH:1 + 2 * NA_KH]
    kc_ref, vc_ref, b_ref, o_ref = refs[1 + 2 * NA_KH:]
    q = q_ref[...] * scale
    kk = jnp.concatenate([r[...] for r in k_refs], axis=1)
    vv = jnp.concatenate([r[...] for r in v_refs], axis=1)
    s = jnp.einsum('hqd,hkd->hqk', q, kk, preferred_element_type=F32) + b_ref[0]
    sc = jnp.einsum('hqd,hkd->hqk', q, kc_ref[...], preferred_element_type=F32)
    m = jnp.maximum(jnp.max(s, axis=-1, keepdims=True), jnp.max(sc, axis=-1, keepdims=True))
    p = jnp.exp(s - m)
    pc = jnp.exp(sc - m)
    den = jnp.sum(p, axis=-1, keepdims=True) + jnp.sum(pc, axis=-1, keepdims=True)
    o = (jnp.einsum('hqk,hkd->hqd', p.astype(BF16), vv, preferred_element_type=F32)
         + jnp.einsum('hqk,hkd->hqd', pc.astype(BF16), vc_ref[...], preferred_element_type=F32))
    o_ref[...] = (o / den).astype(o_ref.dtype)


def neighbourhood_attention(q, k, v, k_ctx, v_ctx, bias_table):
    h, t, hd = q.shape
    rows = t // GRID_W
    kh = min(NA_KH, rows)
    assert kh == NA_KH
    first = lambda r: jnp.clip(r - kh // 2, 0, rows - kh)
    row_blk = pl.BlockSpec((h, GRID_W, hd), lambda r: (0, r, 0))
    key_blks = [pl.BlockSpec((h, GRID_W, hd), functools.partial(lambda r, i: (0, first(r) + i, 0), i=i))
                for i in range(kh)]
    ctx_blk = pl.BlockSpec(k_ctx.shape, lambda r: (0, 0, 0))
    bias_blk = pl.BlockSpec((1,) + bias_table.shape[1:], lambda r: (r - first(r), 0, 0, 0))
    return pl.pallas_call(
        functools.partial(_na_kernel, scale=hd ** -0.5),
        grid=(rows,),
        in_specs=[row_blk] + key_blks + key_blks + [ctx_blk, ctx_blk, bias_blk],
        out_specs=row_blk,
        out_shape=jax.ShapeDtypeStruct((h, t, hd), BF16),
        compiler_params=_params("parallel"),
    )(q, *([k] * kh), *([v] * kh), k_ctx, v_ctx, bias_table)


def time_dft_tables(t):
    a_len = 1 << (int(math.log2(t)) // 2)
    b_len = t // a_len
    ka = np.arange(a_len)[:, None]
    tok = b_len * np.arange(a_len)[None, :]
    ang1 = -2.0 * np.pi * ((ka * (tok[None] + np.arange(b_len)[:, None, None])) % t) / t
    stage1 = np.concatenate([np.cos(ang1), np.sin(ang1)], axis=1) / math.sqrt(t)
    ang2 = 2.0 * np.pi * ((np.arange(b_len)[:, None] * np.arange(b_len)[None, :]) % b_len) / b_len
    c2, s2 = np.cos(ang2), np.sin(ang2)
    stage2 = np.block([[c2, s2], [-s2, c2]])
    return jnp.asarray(stage1, BF16), jnp.asarray(stage2, BF16), a_len, b_len


def _time_dft_kernel(x_ref, m_ref, w2_ref, zre_ref, zim_ref, y_ref, z_ref, *, a_len, b_len):
    i = pl.program_id(1)
    bb = m_ref.shape[0]

    def stage1(jb, carry):
        b = i * bb + jb
        xb = x_ref[pl.ds(b, a_len, stride=b_len), :]
        y = jnp.dot(m_ref[jb], xb.astype(BF16), preferred_element_type=F32)
        row = pl.multiple_of(b * a_len, a_len)
        y_ref[pl.ds(row, a_len), :] = _pack_pair(y[:a_len], y[a_len:])
        return carry

    lax.fori_loop(0, bb, stage1, 0, unroll=4)

    @pl.when(i == pl.num_programs(1) - 1)
    def _():
        def stage2(ka, carry):
            rows = pl.ds(ka, b_len, stride=a_len)
            yre, yim = _unpack_pair(y_ref[rows, :])
            y = jnp.concatenate([yre, yim], axis=0).astype(BF16)
            z = jnp.dot(w2_ref[...], y, preferred_element_type=F32)
            z_ref[rows, :] = _pack_pair(z[:b_len], z[b_len:])
            return carry

        lax.fori_loop(0, a_len, stage2, 0, unroll=4)
        zre, zim = _unpack_pair(z_ref[...])
        zre_ref[...] = zre.astype(zre_ref.dtype)
        zim_ref[...] = zim.astype(zim_ref.dtype)


def time_dft(x):
    t, d = x.shape
    stage1, stage2, a_len, b_len = time_dft_tables(t)
    lanes = 128
    bb = min(16, b_len)
    out = jax.ShapeDtypeStruct((t, d), BF16)
    return pl.pallas_call(
        functools.partial(_time_dft_kernel, a_len=a_len, b_len=b_len),
        grid=(d // lanes, b_len // bb),
        in_specs=[pl.BlockSpec((t, lanes), lambda j, i: (0, j)),
                  pl.BlockSpec((bb, 2 * a_len, a_len), lambda j, i: (i, 0, 0)),
                  pl.BlockSpec(stage2.shape, lambda j, i: (0, 0))],
        out_specs=[pl.BlockSpec((t, lanes), lambda j, i: (0, j))] * 2,
        out_shape=[out, out],
        scratch_shapes=[pltpu.VMEM((t, lanes), U32), pltpu.VMEM((t, lanes), U32)],
        compiler_params=_params("parallel", "arbitrary"),
    )(x, stage1, stage2)


def channel_dft_tables(c):
    ang = 2.0 * np.pi * ((np.arange(c)[:, None] * np.arange(c)[None, :]) % c) / c
    return (jnp.asarray(np.cos(ang) / math.sqrt(c), BF16), jnp.asarray(np.sin(ang) / math.sqrt(c), BF16))


def _ffn_prologue(h, shf_ref, scf_ref, wr_ref, wsg_ref, wsu_ref, wsd_ref, h_ref, fa_ref, fb_ref, lg_ref,
                  shared_ref):
    h_ref[...] = h
    f = _rms(h) * (1.0 + scf_ref[...]) + shf_ref[...]
    packed = _pack_rows(f)
    half = packed.shape[1] // 2
    fa_ref[...] = packed[:, :half]
    fb_ref[...] = packed[:, half:]
    lg_ref[...] = lax.dot_general(wr_ref[...], f, (((1,), (1,)), ((), ())),
                                  preferred_element_type=F32, precision=HIGHEST)
    fb = f.astype(BF16)
    hid = (_silu(jnp.dot(fb, wsg_ref[...], preferred_element_type=F32))
           * jnp.dot(fb, wsu_ref[...], preferred_element_type=F32))
    shared_ref[...] = jnp.dot(hid.astype(BF16), wsd_ref[...], preferred_element_type=F32).astype(shared_ref.dtype)


def _gelu_tanh(x):
    return 0.5 * x * (1.0 + jnp.tanh(math.sqrt(2.0 / math.pi) * (x + 0.044715 * (x * x * x))))


def _even_post_kernel(y_ref, na_ref, x_ref, wglu_ref, bglu_ref, wo_ref, gm_ref, *rest):
    g = _gelu_tanh(y_ref[...])
    gate = jax.nn.sigmoid(jnp.dot(g.astype(BF16), wglu_ref[...], preferred_element_type=F32) + bglu_ref[...])
    s5 = (g * gate).astype(BF16)
    w = s5.shape[1]
    mix = (jnp.dot(s5, wo_ref[:w, :], preferred_element_type=F32)
           + jnp.dot(na_ref[...], wo_ref[w:, :], preferred_element_type=F32))
    _ffn_prologue(x_ref[...] + gm_ref[...] * mix, *rest)


def _odd_post_kernel(zre_ref, zim_ref, h_ref_in, cc_ref, sc_ref, wf_ref, bf_ref, gm_ref, *rest):
    c = cc_ref.shape[0]
    parts = []
    for grp in range(zre_ref.shape[1] // c):
        cols = slice(grp * c, (grp + 1) * c)
        parts.append(jnp.dot(zre_ref[:, cols], cc_ref[...], preferred_element_type=F32)
                     + jnp.dot(zim_ref[:, cols], sc_ref[...], preferred_element_type=F32))
    fr = jnp.concatenate(parts, axis=-1).astype(BF16)
    mix = jnp.dot(fr, wf_ref[...], preferred_element_type=F32) + bf_ref[...]
    _ffn_prologue(h_ref_in[...] + gm_ref[...] * mix, *rest)


def _post_call(body, row_inputs, fixed_inputs, t, d, n_exp):
    tm = min(256, t)
    row = lambda i: (i, 0)
    fixed = lambda i: (0, 0)
    in_specs = ([pl.BlockSpec((tm, a.shape[1]), row) for a in row_inputs]
                + [pl.BlockSpec(a.shape, fixed) for a in fixed_inputs])
    return pl.pallas_call(
        body,
        grid=(t // tm,),
        in_specs=in_specs,
        out_specs=[pl.BlockSpec((tm, d), row), pl.BlockSpec((tm, d // 4), row), pl.BlockSpec((tm, d // 4), row),
                   pl.BlockSpec((n_exp, tm), lambda i: (0, i)), pl.BlockSpec((tm, d), row)],
        out_shape=[jax.ShapeDtypeStruct((t, d), F32),
                   jax.ShapeDtypeStruct((t, d // 4), U32), jax.ShapeDtypeStruct((t, d // 4), U32),
                   jax.ShapeDtypeStruct((n_exp, t), F32), jax.ShapeDtypeStruct((t, d), BF16)],
        compiler_params=_params("parallel"),
    )(*row_inputs, *fixed_inputs)


def _route_kernel(lg_ref, bias_ref, tri_ref, idx_ref, gate_ref, rank_ref, cnt_ref, run_ref):
    @pl.when(pl.program_id(0) == 0)
    def _():
        run_ref[...] = jnp.zeros_like(run_ref)

    scores = jax.nn.sigmoid(lg_ref[...])
    n_exp, tb = scores.shape
    sel = scores + bias_ref[...]
    gsz = n_exp // N_EXPERT_GROUPS
    member = lax.broadcasted_iota(I32, (gsz, tb), 0)
    gscore = []
    for grp in range(N_EXPERT_GROUPS):
        xg = sel[grp * gsz:(grp + 1) * gsz, :]
        m1 = jnp.max(xg, axis=0, keepdims=True)
        first = jnp.min(jnp.where(xg == m1, member, gsz), axis=0, keepdims=True)
        m2 = jnp.max(jnp.where(member == first, -jnp.inf, xg), axis=0, keepdims=True)
        gscore.append(m1 + m2)
    keep_rows = []
    for grp in range(N_EXPERT_GROUPS):
        beaten = jnp.zeros((1, tb), F32)
        for other in range(N_EXPERT_GROUPS):
            if other == grp:
                continue
            wins = (gscore[other] >= gscore[grp]) if other < grp else (gscore[other] > gscore[grp])
            beaten = beaten + jnp.where(wins, 1.0, 0.0)
        keep_rows.append(jnp.broadcast_to(beaten < TOPK_GROUPS, (gsz, tb)))
    masked = jnp.where(jnp.concatenate(keep_rows, axis=0), sel, -jnp.inf)

    expert = lax.broadcasted_iota(I32, (n_exp, tb), 0)
    picks, gates, hots = [], [], []
    chosen = jnp.zeros((n_exp, tb), F32)
    for _ in range(TOP_K):
        m = jnp.max(masked, axis=0, keepdims=True)
        pick = jnp.min(jnp.where(masked == m, expert, n_exp), axis=0, keepdims=True)
        hot = expert == pick
        picks.append(pick)
        hots.append(hot)
        gates.append(jnp.sum(jnp.where(hot, scores, 0.0), axis=0, keepdims=True))
        chosen = jnp.where(hot, 1.0, chosen)
        masked = jnp.where(hot, -jnp.inf, masked)
    total = gates[0]
    for gk in gates[1:]:
        total = total + gk
    ahead = jnp.dot(chosen.astype(BF16), tri_ref[...], preferred_element_type=F32) + run_ref[...]
    for k in range(TOP_K):
        idx_ref[k:k + 1, :] = picks[k]
        gate_ref[k:k + 1, :] = ROUTED_SCALE * gates[k] / total
        rank_ref[k:k + 1, :] = jnp.sum(jnp.where(hots[k], ahead, 0.0), axis=0, keepdims=True).astype(I32)
    run_ref[...] = run_ref[...] + jnp.sum(chosen, axis=1, keepdims=True)
    cnt_ref[...] = jnp.broadcast_to(run_ref[...], cnt_ref.shape)


def route(logits_t, router_bias):
    n_exp, t = logits_t.shape
    tb = min(512, t)
    tri = jnp.asarray(np.triu(np.ones((tb, tb), np.float32), k=1), BF16)
    tok = lambda i: (0, i)
    idx, gate, rank, cnt = pl.pallas_call(
        _route_kernel,
        grid=(t // tb,),
        in_specs=[pl.BlockSpec((n_exp, tb), tok), pl.BlockSpec((n_exp, 1), lambda i: (0, 0)),
                  pl.BlockSpec((tb, tb), lambda i: (0, 0))],
        out_specs=[pl.BlockSpec((TOP_K, tb), tok)] * 3 + [pl.BlockSpec((n_exp, 128), lambda i: (0, 0))],
        out_shape=[jax.ShapeDtypeStruct((TOP_K, t), I32), jax.ShapeDtypeStruct((TOP_K, t), F32),
                   jax.ShapeDtypeStruct((TOP_K, t), I32), jax.ShapeDtypeStruct((n_exp, 128), F32)],
        scratch_shapes=[pltpu.VMEM((n_exp, 1), F32)],
        compiler_params=_params("arbitrary"),
    )(logits_t, router_bias.astype(F32).reshape(n_exp, 1), tri)
    return idx, gate, rank, cnt[:, 0].astype(I32)


def dispatch_plan(idx, rank, counts, n_blocks):
    n_exp = counts.shape[0]
    padded = (counts + EXPERT_ROWS - 1) // EXPERT_ROWS * EXPERT_ROWS
    pad_end = jnp.cumsum(padded)
    pad_start = pad_end - padded
    experts_iota = jnp.arange(n_exp, dtype=I32)
    dest = jnp.sum(jnp.where(idx[..., None] == experts_iota, pad_start, 0), axis=-1) + rank
    n_valid = (pad_end[-1] // EXPERT_ROWS).astype(I32)
    blk = jnp.minimum(jnp.arange(n_blocks, dtype=I32), n_valid - 1)
    blk_e = jnp.sum((pad_end[None, :] <= blk[:, None] * EXPERT_ROWS).astype(I32), axis=1)
    return dest.astype(I32), jnp.minimum(blk_e, n_exp - 1).astype(I32), n_valid.reshape(1)


def _sc_mesh():
    return plsc.VectorSubcoreMesh(core_axis_name="core", subcore_axis_name="subcore")


def dispatch(f, dest, cap):
    t, w = f.shape
    n_choice = dest.shape[0]

    @functools.partial(pl.kernel, out_type=jax.ShapeDtypeStruct((cap, w), f.dtype), mesh=_sc_mesh(),
                       scratch_types=[])
    def scatter_rows(x_hbm, i_hbm, o_hbm):
        def body(x_vmem, i_vmem):
            pltpu.sync_copy(x_vmem, o_hbm.at[i_vmem.at[0]])

        pltpu.emit_pipeline(
            body,
            grid=(t // SC_WINDOW, n_choice),
            in_specs=[pl.BlockSpec((SC_WINDOW, w), lambda i, k: (i, 0)),
                      pl.BlockSpec((1, SC_WINDOW), lambda i, k: (k, i))],
            out_specs=[],
            core_axis_name=("core", "subcore"),
            dimension_semantics=(pltpu.PARALLEL, pltpu.ARBITRARY),
        )(x_hbm, i_hbm)

    return scatter_rows(f, dest)


def gather_rows(rows, index_row):
    n = index_row.shape[1]
    w = rows.shape[1]

    @functools.partial(pl.kernel, out_type=jax.ShapeDtypeStruct((n, w), rows.dtype), mesh=_sc_mesh(),
                       scratch_types=[])
    def gather(y_hbm, i_hbm, o_hbm):
        def body(i_vmem, o_vmem):
            pltpu.sync_copy(y_hbm.at[i_vmem.at[0]], o_vmem)

        pltpu.emit_pipeline(
            body,
            grid=(n // SC_WINDOW,),
            in_specs=[pl.BlockSpec((1, SC_WINDOW), lambda i: (0, i))],
            out_specs=[pl.BlockSpec((SC_WINDOW, w), lambda i: (i, 0))],
            core_axis_name=("core", "subcore"),
            dimension_semantics=(pltpu.PARALLEL,),
        )(i_hbm, o_hbm)

    return gather(rows, index_row)


def _experts_kernel(be_ref, nv_ref, xa_ref, xb_ref, wg_ref, wu_ref, wd_ref, ya_ref, yb_ref, wg_bf, wu_bf, wd_bf):
    b = pl.program_id(0)

    @pl.when((b == 0) | (be_ref[b] != be_ref[jnp.maximum(b - 1, 0)]))
    def _():
        wg_bf[...] = wg_ref[0].astype(BF16)
        wu_bf[...] = wu_ref[0].astype(BF16)
        wd_bf[...] = wd_ref[0].astype(BF16)

    @pl.when(b < nv_ref[0])
    def _():
        x = _unpack_rows(jnp.concatenate([xa_ref[...], xb_ref[...]], axis=1)).astype(BF16)
        hid = (_silu(jnp.dot(x, wg_bf[...], preferred_element_type=F32))
               * jnp.dot(x, wu_bf[...], preferred_element_type=F32))
        packed = _pack_rows(jnp.dot(hid.astype(BF16), wd_bf[...], preferred_element_type=F32))
        half = packed.shape[1] // 2
        ya_ref[...] = packed[:, :half]
        yb_ref[...] = packed[:, half:]

    @pl.when(b >= nv_ref[0])
    def _():
        ya_ref[...] = jnp.zeros_like(ya_ref)
        yb_ref[...] = jnp.zeros_like(yb_ref)


def experts(rows_a, rows_b, blk_e, n_valid, w_gate, w_up, w_down):
    cap, dq = rows_a.shape
    _, d, ff = w_gate.shape
    n_blocks = cap // EXPERT_ROWS
    live = lambda b, be, nv: (jnp.minimum(b, nv[0] - 1), 0)
    out_blk = pl.BlockSpec((EXPERT_ROWS, dq), lambda b, be, nv: (b, 0))
    out = jax.ShapeDtypeStruct((cap, dq), U32)
    return pl.pallas_call(
        _experts_kernel,
        grid_spec=pltpu.PrefetchScalarGridSpec(
            num_scalar_prefetch=2,
            grid=(n_blocks,),
            in_specs=[pl.BlockSpec((EXPERT_ROWS, dq), live), pl.BlockSpec((EXPERT_ROWS, dq), live),
                      pl.BlockSpec((1, d, ff), lambda b, be, nv: (be[b], 0, 0)),
                      pl.BlockSpec((1, d, ff), lambda b, be, nv: (be[b], 0, 0)),
                      pl.BlockSpec((1, ff, d), lambda b, be, nv: (be[b], 0, 0))],
            out_specs=[out_blk, out_blk],
            scratch_shapes=[pltpu.VMEM((d, ff), BF16), pltpu.VMEM((d, ff), BF16), pltpu.VMEM((ff, d), BF16)],
        ),
        out_shape=[out, out],
        compiler_params=_params("arbitrary"),
    )(blk_e, n_valid, rows_a, rows_b, w_gate, w_up, w_down)


def _combine_kernel(gate_ref, h_ref, shared_ref, gf_ref, sh_ref, sc_ref, *rest):
    ya_refs, yb_refs, (h_out, a_out) = rest[:TOP_K], rest[TOP_K:2 * TOP_K], rest[2 * TOP_K:]
    routed = None
    for k in range(TOP_K):
        y = _unpack_rows(jnp.concatenate([ya_refs[k][0], yb_refs[k][0]], axis=1))
        routed = gate_ref[:, k:k + 1] * y if routed is None else routed + gate_ref[:, k:k + 1] * y
    h = h_ref[...] + gf_ref[...] * (routed + shared_ref[...].astype(F32))
    h_out[...] = h
    a_out[...] = _rms(h) * (1.0 + sc_ref[...]) + sh_ref[...]


def combine(y_a, y_b, dest, gate_tk, h, shared, gate_ffn, next_shift, next_scale):
    t, d = h.shape
    n_choice = dest.shape[0]
    flat = dest.reshape(1, n_choice * t)
    picked_a = gather_rows(y_a, flat).reshape(n_choice, t, d // 4)
    picked_b = gather_rows(y_b, flat).reshape(n_choice, t, d // 4)
    tm = min(256, t)
    row = lambda i: (i, 0)
    fixed = lambda i: (0, 0)
    choice = [pl.BlockSpec((1, tm, d // 4), functools.partial(lambda i, k: (k, i, 0), k=k))
              for k in range(n_choice)] * 2
    return pl.pallas_call(
        _combine_kernel,
        grid=(t // tm,),
        in_specs=[pl.BlockSpec((tm, n_choice), row), pl.BlockSpec((tm, d), row), pl.BlockSpec((tm, d), row),
                  pl.BlockSpec((1, d), fixed), pl.BlockSpec((1, d), fixed), pl.BlockSpec((1, d), fixed)] + choice,
        out_specs=[pl.BlockSpec((tm, d), row), pl.BlockSpec((tm, d), row)],
        out_shape=[jax.ShapeDtypeStruct((t, d), F32), jax.ShapeDtypeStruct((t, d), F32)],
        compiler_params=_params("parallel"),
    )(gate_tk, h, shared, gate_ffn, next_shift, next_scale, *([picked_a] * n_choice), *([picked_b] * n_choice))


def moe_tail(h, f_a, f_b, logits_t, shared, router_bias, w_gate, w_up, w_down, gate_ffn, next_shift, next_scale):
    t, d = h.shape
    n_exp = w_gate.shape[0]
    n_blocks = -(-(t * TOP_K + n_exp * (EXPERT_ROWS - 1)) // EXPERT_ROWS)
    cap = n_blocks * EXPERT_ROWS
    idx, gate, rank, counts = route(logits_t, router_bias)
    dest, blk_e, n_valid = dispatch_plan(idx, rank, counts, n_blocks)
    y_a, y_b = experts(dispatch(f_a, dest, cap), dispatch(f_b, dest, cap), blk_e, n_valid, w_gate, w_up, w_down)
    return combine(y_a, y_b, dest, gate.T, h, shared, gate_ffn, next_shift, next_scale)


def kernel(x, c, ctx, c_ctx, w_ada, b_ada, w_in, s5_lam_re, s5_lam_im, s5_log_dt, s5_b_re, s5_b_im,
           s5_c_re, s5_c_im, s5_d, s5_w_glu, s5_b_glu, na_q_gain, na_k_gain, na_rpb, w_mix_out,
           w_fourier_out, b_fourier_out, w_router, router_bias, w_exp_gate, w_exp_up, w_exp_down,
           w_sh_gate, w_sh_up, w_sh_down):
    bsz, t, d = x.shape
    assert bsz == 1 and w_ada.shape[0] == 2
    n_exp = w_router.shape[2]
    s5w = s5_w_glu.shape[1]
    naw = w_in.shape[2] - s5w
    naw //= 3
    heads = naw // NA_HEAD_DIM

    cond8 = jnp.zeros((8, d), F32).at[0].set(c[0].astype(F32)).at[1].set(c_ctx.astype(F32))
    ada = adaln_all(cond8, w_ada, b_ada)
    mod = lambda layer, who, j: ada[layer, who:who + 1, j * d:(j + 1) * d]

    def ffn_weights(i):
        return (mod(i, 0, 3), mod(i, 0, 4), jnp.transpose(w_router[i]).astype(F32),
                w_sh_gate[i].astype(BF16), w_sh_up[i].astype(BF16), w_sh_down[i].astype(BF16))

    h0 = x[0]
    seg = jnp.asarray(np.kron(np.eye(heads), np.ones((NA_HEAD_DIM, NA_HEAD_DIM))), BF16)
    w_in_b = w_in[0].astype(BF16)
    qg = jnp.tile(na_q_gain[0].astype(F32), heads)[None]
    kg = jnp.tile(na_k_gain[0].astype(F32), heads)[None]
    u_c, _, k_c, v_c = in_projection(ctx[0], mod(0, 1, 0), mod(0, 1, 1), w_in_b, seg, qg, kg, s5w, naw)
    u_l, q_l, k_l, v_l = in_projection(h0, mod(0, 0, 0), mod(0, 0, 1), w_in_b, seg, qg, kg, s5w, naw)
    mats = s5_matrices(s5_lam_re[0], s5_lam_im[0], s5_log_dt[0], s5_b_re[0], s5_b_im[0],
                       s5_c_re[0], s5_c_im[0], s5_d[0])
    y_s5 = s5_mixer(u_c, u_l, mats)
    head_major = lambda a: jnp.transpose(a.reshape(a.shape[0], heads, NA_HEAD_DIM), (1, 0, 2))
    na = neighbourhood_attention(head_major(q_l), head_major(k_l), head_major(v_l),
                                 head_major(k_c), head_major(v_c), na_bias_table(na_rpb[0]))
    na = jnp.transpose(na, (1, 0, 2)).reshape(t, naw)
    h1, f1a, f1b, lg1, sh1 = _post_call(
        _even_post_kernel, [y_s5, na, h0],
        [s5_w_glu[0].astype(BF16), s5_b_glu[0].astype(F32)[None], w_mix_out[0].astype(BF16), mod(0, 0, 2),
         *ffn_weights(0)], t, d, n_exp)
    h2, a1 = moe_tail(h1, f1a, f1b, lg1, sh1, router_bias[0], w_exp_gate[0], w_exp_up[0], w_exp_down[0],
                      mod(0, 0, 5), mod(1, 0, 0), mod(1, 0, 1))

    zre, zim = time_dft(a1)
    cc, sc = channel_dft_tables(d // FOURIER_GROUPS)
    h3, f3a, f3b, lg3, sh3 = _post_call(
        _odd_post_kernel, [zre, zim, h2],
        [cc, sc, w_fourier_out[0].astype(BF16), b_fourier_out[0].astype(F32)[None], mod(1, 0, 2),
         *ffn_weights(1)], t, d, n_exp)
    zero_row = jnp.zeros((1, d), F32)
    out, _ = moe_tail(h3, f3a, f3b, lg3, sh3, router_bias[1], w_exp_gate[1], w_exp_up[1], w_exp_down[1],
                      mod(1, 0, 5), zero_row, zero_row)
    return out[None]
```

```python
import functools
import math

import numpy as np
import jax
import jax.numpy as jnp
from jax import lax
from jax.experimental import pallas as pl
from jax.experimental.pallas import tpu as pltpu
from jax.experimental.pallas import tpu_sc as plsc

F32 = jnp.float32
BF16 = jnp.bfloat16
I32 = jnp.int32
U32 = jnp.uint32
HIGHEST = lax.Precision.HIGHEST

LANES = 128
GRID_W = 64
NORM_EPS = 1e-6
S5_GROUP = 16
S5_STATE = 64
S5_LAMBDA_RE_MAX = -1e-4
S5_CHUNK = 16
S5_TILE = 128
NA_HEADS = 8
NA_HEAD_DIM = 64
NA_KH = 8
NA_KW = 16
FOURIER_GROUPS = 4
N_EXPERT_GROUPS = 8
TOPK_GROUPS = 4
TOP_K = 8
ROUTED_SCALE = 2.5
EXPERT_ROWS = 512
SC_WINDOW = 128
NEG_BIG = -1e30

VMEM_LIMIT_BYTES = 56 * 1024 * 1024


def _params(*sem):
    return pltpu.CompilerParams(dimension_semantics=sem or None,
                                vmem_limit_bytes=VMEM_LIMIT_BYTES)


def _rms(x):
    return x * lax.rsqrt(jnp.mean(x * x, axis=-1, keepdims=True) + NORM_EPS)


def _silu(x):
    return x * jax.nn.sigmoid(x)


def _pack_pair(lo, hi):
    lo = lax.bitcast_convert_type(lo.astype(BF16).astype(F32), U32)
    hi = lax.bitcast_convert_type(hi.astype(BF16).astype(F32), U32)
    return (hi & jnp.uint32(0xFFFF0000)) | (lo >> 16)


def _unpack_pair(w):
    return (lax.bitcast_convert_type(w << 16, F32), lax.bitcast_convert_type(w & jnp.uint32(0xFFFF0000), F32))


def _pack_rows(x):
    n = x.shape[1] // 2
    return _pack_pair(x[:, :n], x[:, n:])


def _unpack_rows(w):
    return jnp.concatenate(_unpack_pair(w), axis=1)


def _ada_kernel(c_ref, w_ref, b_ref, o_ref):
    o_ref[0] = jnp.dot(_silu(c_ref[...]), w_ref[0], preferred_element_type=F32,
                       precision=HIGHEST) + b_ref[0]


def adaln_all(cond8, w_ada, b_ada):
    n_layers, d, n6 = w_ada.shape
    tn = n6 // 4
    return pl.pallas_call(
        _ada_kernel,
        grid=(n_layers, n6 // tn),
        in_specs=[pl.BlockSpec((8, d), lambda l, j: (0, 0)),
                  pl.BlockSpec((1, d, tn), lambda l, j: (l, 0, j)),
                  pl.BlockSpec((1, 1, tn), lambda l, j: (l, 0, j))],
        out_specs=pl.BlockSpec((1, 8, tn), lambda l, j: (l, 0, j)),
        out_shape=jax.ShapeDtypeStruct((n_layers, 8, n6), F32),
        compiler_params=_params("parallel", "parallel"),
    )(cond8, w_ada, b_ada.reshape(n_layers, 1, n6))


def _inproj_kernel(x_ref, sh_ref, sc_ref, w_ref, seg_ref, qg_ref, kg_ref,
                   u_ref, q_ref, k_ref, v_ref):
    a = _rms(x_ref[...]) * (1.0 + sc_ref[...]) + sh_ref[...]
    z = jnp.dot(a.astype(BF16), w_ref[...], preferred_element_type=F32)
    s5w = u_ref.shape[1]
    naw = q_ref.shape[1]

    def head_norm(t, gain):
        ss = jnp.dot((t * t).astype(BF16), seg_ref[...], preferred_element_type=F32)
        return t * lax.rsqrt(ss * (1.0 / NA_HEAD_DIM) + NORM_EPS) * gain

    u_ref[...] = z[:, :s5w]
    q_ref[...] = head_norm(z[:, s5w:s5w + naw], qg_ref[...]).astype(BF16)
    k_ref[...] = head_norm(z[:, s5w + naw:s5w + 2 * naw], kg_ref[...]).astype(BF16)
    v_ref[...] = z[:, s5w + 2 * naw:].astype(BF16)


def in_projection(x, shift, scale, w_in_bf16, seg_ones, q_gain_row, k_gain_row, s5w, naw):
    t, d = x.shape
    tm = min(512, t)
    row = lambda i: (i, 0)
    fixed = lambda i: (0, 0)
    return pl.pallas_call(
        _inproj_kernel,
        grid=(t // tm,),
        in_specs=[pl.BlockSpec((tm, d), row),
                  pl.BlockSpec((1, d), fixed), pl.BlockSpec((1, d), fixed),
                  pl.BlockSpec(w_in_bf16.shape, fixed),
                  pl.BlockSpec(seg_ones.shape, fixed),
                  pl.BlockSpec((1, naw), fixed), pl.BlockSpec((1, naw), fixed)],
        out_specs=[pl.BlockSpec((tm, s5w), row), pl.BlockSpec((tm, naw), row),
                   pl.BlockSpec((tm, naw), row), pl.BlockSpec((tm, naw), row)],
        out_shape=[jax.ShapeDtypeStruct((t, s5w), F32)] + [jax.ShapeDtypeStruct((t, naw), BF16)] * 3,
        compiler_params=_params("parallel"),
    )(x, shift, scale, w_in_bf16, seg_ones, q_gain_row, k_gain_row)


def s5_matrices(lam_re, lam_im, log_dt, b_re, b_im, c_re, c_im, d_skip):
    L = S5_CHUNK
    taus = jnp.arange(L + 1, dtype=F32)

    def direction(i):
        lam = lax.complex(jnp.minimum(lam_re[i].astype(F32), S5_LAMBDA_RE_MAX), lam_im[i].astype(F32))
        ldt = lam * jnp.exp(log_dt[i].astype(F32))[:, None]
        lam_bar = jnp.exp(ldt)
        b_bar = ((lam_bar - 1.0) / lam)[..., None] * lax.complex(b_re[i].astype(F32), b_im[i].astype(F32))
        cc = lax.complex(c_re[i].astype(F32), c_im[i].astype(F32))
        powers = jnp.exp(ldt[None] * taus[:, None, None])
        resp = jnp.real(jnp.einsum('gcp,tgp,gpd->gtcd', cc, powers[:L], b_bar, precision=HIGHEST))
        return powers, b_bar, cc, resp

    pw_f, bb_f, cc_f, k_f = direction(0)
    pw_b, bb_b, cc_b, k_b = direction(1)
    g, p = pw_f.shape[1:]
    c = S5_GROUP
    diff = np.arange(L)[None, :] - np.arange(L)[:, None]
    lag = np.arange(L)[:, None, None]
    tf = jnp.einsum('tsl,gtcd->gslcd', jnp.asarray(lag == diff[None], F32), k_f, precision=HIGHEST)
    tb = jnp.einsum('tsl,gtcd->gslcd', jnp.asarray(lag == -diff[None], F32), k_b, precision=HIGHEST)
    skip = (jnp.eye(L, dtype=F32)[None, :, :, None, None]
            * (jnp.eye(c, dtype=F32)[None] * d_skip.astype(F32)[:, :, None])[:, None, None])
    toep = jnp.transpose(tf + tb + skip, (0, 1, 4, 2, 3)).reshape(g, L * c, L * c)

    def state_in(powers_sel, b_bar):
        w = powers_sel[:, :, :, None] * b_bar[None]
        return jnp.transpose(w, (1, 0, 3, 2)).reshape(g, L * c, p)

    wf = state_in(pw_f[L - 1 - jnp.arange(L)], bb_f)
    wb = state_in(pw_b[jnp.arange(L)], bb_b)
    w_state = jnp.concatenate([jnp.real(wf), jnp.imag(wf), jnp.imag(wf), jnp.real(wf),
                               jnp.real(wb), jnp.imag(wb), jnp.imag(wb), jnp.real(wb)], axis=-1)

    def state_out(powers_sel, cc):
        r = powers_sel[:, :, None, :] * cc[None]
        return jnp.transpose(r, (1, 3, 0, 2)).reshape(g, p, L * c)

    rf = state_out(pw_f[1 + jnp.arange(L)], cc_f)
    rb = state_out(pw_b[L - jnp.arange(L)], cc_b)
    r_state = jnp.concatenate([jnp.real(rf), -jnp.imag(rf), jnp.real(rb), -jnp.imag(rb)], axis=1)

    def mult(a):
        ar, ai = jnp.real(a), jnp.imag(a)
        return jnp.stack([jnp.concatenate([ar, ar], -1), jnp.concatenate([-ai, ai], -1),
                          jnp.concatenate([ai, -ai], -1)])

    return toep, w_state, r_state, mult(pw_f[L]), mult(pw_b[L])


def _s5_pack_kernel(*refs):
    u_refs, (wt_ref, ut_ref, f1_ref, f2_ref, b1_ref, b2_ref) = refs[:-6], refs[-6:]
    L = S5_CHUNK
    g, lc, nck = ut_ref.shape
    c = lc // L
    gs = g // len(u_refs)
    for s in range(L):
        for j, u_ref in enumerate(u_refs):
            step_s = u_ref[pl.ds(s, nck, stride=L), :]
            ut_ref[j * gs:(j + 1) * gs, s * c:(s + 1) * c, :] = (
                jnp.transpose(step_s).astype(BF16).reshape(gs, c, nck))
    n = f1_ref.shape[2]
    for gi in range(g):
        inc = jnp.dot(wt_ref[gi], ut_ref[gi], preferred_element_type=F32)
        for j, ref in enumerate((f1_ref, f2_ref, b1_ref, b2_ref)):
            ref[:, gi, :] = jnp.transpose(inc[j * n:(j + 1) * n, :])


def s5_pack(u, wt_state):
    t, w = u.shape
    g, n4, lc = wt_state.shape
    nc = t // S5_CHUNK
    tile = min(S5_TILE, nc)
    inc = jax.ShapeDtypeStruct((nc, g, n4 // 4), F32)
    inc_blk = pl.BlockSpec((tile, g, n4 // 4), lambda i: (i, 0, 0))
    return pl.pallas_call(
        _s5_pack_kernel,
        grid=(nc // tile,),
        in_specs=[pl.BlockSpec((tile * S5_CHUNK, LANES), functools.partial(lambda i, j: (i, j), j=j))
                  for j in range(w // LANES)] + [pl.BlockSpec(wt_state.shape, lambda i: (0, 0, 0))],
        out_specs=[pl.BlockSpec((g, lc, tile), lambda i: (0, 0, i))] + [inc_blk] * 4,
        out_shape=[jax.ShapeDtypeStruct((g, lc, nc), BF16)] + [inc] * 4,
        compiler_params=_params("parallel"),
    )(*([u] * (w // LANES)), wt_state)


def _s5_scan_kernel(s1_ref, s2_ref, m_ref, init_ref, x_ref, last_ref, v1_ref, v2_ref, *, reverse):
    @pl.when(pl.program_id(0) == 0)
    def _():
        v1_ref[...] = init_ref[0]
        v2_ref[...] = init_ref[1]

    a1, a2, a3 = m_ref[0], m_ref[1], m_ref[2]
    cb = s1_ref.shape[0]

    def body(j, carry):
        v1, v2 = carry
        jj = cb - 1 - j if reverse else j
        x_ref[jj] = v1
        return (a1 * v1 + a2 * v2 + s1_ref[jj], a1 * v2 + a3 * v1 + s2_ref[jj])

    v1, v2 = lax.fori_loop(0, cb, body, (v1_ref[...], v2_ref[...]))
    v1_ref[...] = v1
    v2_ref[...] = v2
    last_ref[...] = v1


def s5_chunk_scan(s1, s2, mult, init, reverse):
    nc, g, n = s1.shape
    cb = min(S5_TILE, nc)
    nb = nc // cb
    blk = (lambda i: (nb - 1 - i, 0, 0)) if reverse else (lambda i: (i, 0, 0))
    return pl.pallas_call(
        functools.partial(_s5_scan_kernel, reverse=reverse),
        grid=(nb,),
        in_specs=[pl.BlockSpec((cb, g, n), blk), pl.BlockSpec((cb, g, n), blk),
                  pl.BlockSpec((3, g, n), lambda i: (0, 0, 0)), pl.BlockSpec((2, g, n), lambda i: (0, 0, 0))],
        out_specs=[pl.BlockSpec((cb, g, n), blk), pl.BlockSpec((g, n), lambda i: (0, 0))],
        out_shape=[jax.ShapeDtypeStruct((nc, g, n), F32), jax.ShapeDtypeStruct((g, n), F32)],
        scratch_shapes=[pltpu.VMEM((g, n), F32), pltpu.VMEM((g, n), F32)],
        compiler_params=_params("arbitrary"),
    )(s1, s2, mult, init)


def _s5_readout_kernel(ut_ref, tt_ref, rt_ref, xf_ref, xb_ref, y_ref, yt_ref, *slab_refs):
    L = S5_CHUNK
    g, lc, nck = ut_ref.shape
    c = lc // L
    gs = g // len(slab_refs)
    for gi in range(g):
        xin_t = jnp.concatenate([jnp.transpose(xf_ref[:, gi, :]), jnp.transpose(xb_ref[:, gi, :])], axis=0)
        yt_ref[gi] = (jnp.dot(tt_ref[gi], ut_ref[gi], preferred_element_type=F32)
                      + jnp.dot(rt_ref[gi], xin_t.astype(BF16), preferred_element_type=F32))
    for j, slab in enumerate(slab_refs):
        for l in range(L):
            step_l = yt_ref[j * gs:(j + 1) * gs, l * c:(l + 1) * c, :].reshape(gs * c, nck)
            slab[pl.ds(l, nck, stride=L), :] = jnp.transpose(step_l)
        y_ref[:, j * gs * c:(j + 1) * gs * c] = slab[...]


def s5_readout(ut, toep_t, r_state_t, xin_f, xin_b):
    g, lc, nc = ut.shape
    n = xin_f.shape[2]
    tile = min(S5_TILE, nc)
    fixed = lambda i: (0, 0, 0)
    state_blk = pl.BlockSpec((tile, g, n), lambda i: (i, 0, 0))
    return pl.pallas_call(
        _s5_readout_kernel,
        grid=(nc // tile,),
        in_specs=[pl.BlockSpec((g, lc, tile), lambda i: (0, 0, i)),
                  pl.BlockSpec(toep_t.shape, fixed), pl.BlockSpec(r_state_t.shape, fixed), state_blk, state_blk],
        out_specs=pl.BlockSpec((tile * S5_CHUNK, g * lc // S5_CHUNK), lambda i: (i, 0)),
        out_shape=jax.ShapeDtypeStruct((nc * S5_CHUNK, g * lc // S5_CHUNK), F32),
        scratch_shapes=[pltpu.VMEM((g, lc, tile), F32)]
        + [pltpu.VMEM((tile * S5_CHUNK, LANES), F32)] * (g * lc // S5_CHUNK // LANES),
        compiler_params=_params("parallel"),
    )(ut, toep_t, r_state_t, xin_f, xin_b)


def s5_mixer(u_ctx, u_lat, mats):
    toep, w_state, r_state, mult_f, mult_b = mats
    L = S5_CHUNK
    g, n = mult_f.shape[1:]
    swap = lambda a: jnp.transpose(a, (0, 2, 1)).astype(BF16)
    wt_state, toep_t, r_state_t = swap(w_state), swap(toep), swap(r_state)
    halves = lambda v: jnp.stack([v, jnp.roll(v, n // 2, axis=-1)])
    n_ctx = u_ctx.shape[0] // L
    ctx_chunks = -(-(n_ctx + 1) // S5_TILE) * S5_TILE
    ctx_pad = jnp.zeros((ctx_chunks * L, u_ctx.shape[1]), F32).at[:u_ctx.shape[0]].set(u_ctx)
    _, cf1, cf2, cb1, cb2 = s5_pack(ctx_pad, wt_state)
    zero = jnp.zeros((2, g, n), F32)
    ctx_f, _ = s5_chunk_scan(cf1, cf2, mult_f, zero, False)
    _, ctx_b_last = s5_chunk_scan(cb1, cb2, mult_b, zero, True)
    ut, f1, f2, b1, b2 = s5_pack(u_lat, wt_state)
    xin_f, _ = s5_chunk_scan(f1, f2, mult_f, halves(ctx_f[n_ctx]), False)
    xin_b, _ = s5_chunk_scan(b1, b2, mult_b, halves(ctx_b_last), True)
    return s5_readout(ut, toep_t, r_state_t, xin_f, xin_b)


def na_bias_table(rpb):
    q_col = np.arange(GRID_W)
    col_start = np.clip(q_col - NA_KW // 2, 0, GRID_W - NA_KW)
    key_col = np.arange(GRID_W)
    off = key_col[None, :] - col_start[:, None]
    valid = (off >= 0) & (off < NA_KW)
    rel_col = np.clip(key_col[None, :] - q_col[:, None] + NA_KW - 1, 0, 2 * NA_KW - 2)
    pick = jnp.asarray(np.arange(2 * NA_KW - 1)[:, None, None] == rel_col[None], F32)
    full = jnp.where(valid[None, None], jnp.einsum('hrj,jck->hrck', rpb.astype(F32), pick, precision=HIGHEST),
                     NEG_BIG)
    variants = []
    for d in range(NA_KH):
        rows = [full[:, i - d + NA_KH - 1] for i in range(NA_KH)]
        variants.append(jnp.concatenate(rows, axis=-1))
    return jnp.stack(variants)


def _na_kernel(*refs, scale):
    q_ref = refs[0]
    k_refs = refs[1:1 + NA_KH]
    v_refs = refs[1 + NA_KH:1 + 2 * NA_KH]
    kc_ref, vc_ref, b_ref, o_ref = refs[1 + 2 * NA_KH:]
    q = q_ref[...] * scale
    kk = jnp.concatenate([r[...] for r in k_refs], axis=1)
    vv = jnp.concatenate([r[...] for r in v_refs], axis=1)
    s = jnp.einsum('hqd,hkd->hqk', q, kk, preferred_element_type=F32) + b_ref[0]
    sc = jnp.einsum('hqd,hkd->hqk', q, kc_ref[...], preferred_element_type=F32)
    m = jnp.maximum(jnp.max(s, axis=-1, keepdims=True), jnp.max(sc, axis=-1, keepdims=True))
    p = jnp.exp(s - m)
    pc = jnp.exp(sc - m)
    den = jnp.sum(p, axis=-1, keepdims=True) + jnp.sum(pc, axis=-1, keepdims=True)
    o = (jnp.einsum('hqk,hkd->hqd', p.astype(BF16), vv, preferred_element_type=F32)
         + jnp.einsum('hqk,hkd->hqd', pc.astype(BF16), vc_ref[...], preferred_element_type=F32))
    o_ref[...] = (o / den).astype(o_ref.dtype)


def neighbourhood_attention(q, k, v, k_ctx, v_ctx, bias_table):
    h, t, hd = q.shape
    rows = t // GRID_W
    kh = min(NA_KH, rows)
    assert kh == NA_KH
    first = lambda r: jnp.clip(r - kh // 2, 0, rows - kh)
    row_blk = pl.BlockSpec((h, GRID_W, hd), lambda r: (0, r, 0))
    key_blks = [pl.BlockSpec((h, GRID_W, hd), functools.partial(lambda r, i: (0, first(r) + i, 0), i=i))
                for i in range(kh)]
    ctx_blk = pl.BlockSpec(k_ctx.shape, lambda r: (0, 0, 0))
    bias_blk = pl.BlockSpec((1,) + bias_table.shape[1:], lambda r: (r - first(r), 0, 0, 0))
    return pl.pallas_call(
        functools.partial(_na_kernel, scale=hd ** -0.5),
        grid=(rows,),
        in_specs=[row_blk] + key_blks + key_blks + [ctx_blk, ctx_blk, bias_blk],
        out_specs=row_blk,
        out_shape=jax.ShapeDtypeStruct((h, t, hd), BF16),
        compiler_params=_params("parallel"),
    )(q, *([k] * kh), *([v] * kh), k_ctx, v_ctx, bias_table)


def time_dft_tables(t):
    a_len = 1 << (int(math.log2(t)) // 2)
    b_len = t // a_len
    ka = np.arange(a_len)[:, None]
    tok = b_len * np.arange(a_len)[None, :]
    ang1 = -2.0 * np.pi * ((ka * (tok[None] + np.arange(b_len)[:, None, None])) % t) / t
    stage1 = np.concatenate([np.cos(ang1), np.sin(ang1)], axis=1) / math.sqrt(t)
    ang2 = 2.0 * np.pi * ((np.arange(b_len)[:, None] * np.arange(b_len)[None, :]) % b_len) / b_len
    c2, s2 = np.cos(ang2), np.sin(ang2)
    stage2 = np.block([[c2, s2], [-s2, c2]])
    return jnp.asarray(stage1, BF16), jnp.asarray(stage2, BF16), a_len, b_len


def _time_dft_kernel(x_ref, m_ref, w2_ref, zre_ref, zim_ref, y_ref, z_ref, *, a_len, b_len):
    i = pl.program_id(1)
    bb = m_ref.shape[0]

    def stage1(jb, carry):
        b = i * bb + jb
        xb = x_ref[pl.ds(b, a_len, stride=b_len), :]
        y = jnp.dot(m_ref[jb], xb.astype(BF16), preferred_element_type=F32)
        row = pl.multiple_of(b * a_len, a_len)
        y_ref[pl.ds(row, a_len), :] = _pack_pair(y[:a_len], y[a_len:])
        return carry

    lax.fori_loop(0, bb, stage1, 0, unroll=4)

    @pl.when(i == pl.num_programs(1) - 1)
    def _():
        def stage2(ka, carry):
            rows = pl.ds(ka, b_len, stride=a_len)
            yre, yim = _unpack_pair(y_ref[rows, :])
            y = jnp.concatenate([yre, yim], axis=0).astype(BF16)
            z = jnp.dot(w2_ref[...], y, preferred_element_type=F32)
            z_ref[rows, :] = _pack_pair(z[:b_len], z[b_len:])
            return carry

        lax.fori_loop(0, a_len, stage2, 0, unroll=4)
        zre, zim = _unpack_pair(z_ref[...])
        zre_ref[...] = zre.astype(zre_ref.dtype)
        zim_ref[...] = zim.astype(zim_ref.dtype)


def time_dft(x):
    t, d = x.shape
    stage1, stage2, a_len, b_len = time_dft_tables(t)
    lanes = 128
    bb = min(16, b_len)
    out = jax.ShapeDtypeStruct((t, d), BF16)
    return pl.pallas_call(
        functools.partial(_time_dft_kernel, a_len=a_len, b_len=b_len),
        grid=(d // lanes, b_len // bb),
        in_specs=[pl.BlockSpec((t, lanes), lambda j, i: (0, j)),
                  pl.BlockSpec((bb, 2 * a_len, a_len), lambda j, i: (i, 0, 0)),
                  pl.BlockSpec(stage2.shape, lambda j, i: (0, 0))],
        out_specs=[pl.BlockSpec((t, lanes), lambda j, i: (0, j))] * 2,
        out_shape=[out, out],
        scratch_shapes=[pltpu.VMEM((t, lanes), U32), pltpu.VMEM((t, lanes), U32)],
        compiler_params=_params("parallel", "arbitrary"),
    )(x, stage1, stage2)


def channel_dft_tables(c):
    ang = 2.0 * np.pi * ((np.arange(c)[:, None] * np.arange(c)[None, :]) % c) / c
    return (jnp.asarray(np.cos(ang) / math.sqrt(c), BF16), jnp.asarray(np.sin(ang) / math.sqrt(c), BF16))


def _shared_expert_kernel(fa_ref, fb_ref, wsg_ref, wsu_ref, wsd_ref, o_ref):
    fb = _unpack_rows(jnp.concatenate([fa_ref[...], fb_ref[...]], axis=1)).astype(BF16)
    hid = (_silu(jnp.dot(fb, wsg_ref[...], preferred_element_type=F32))
           * jnp.dot(fb, wsu_ref[...], preferred_element_type=F32))
    o_ref[...] = jnp.dot(hid.astype(BF16), wsd_ref[...], preferred_element_type=F32).astype(o_ref.dtype)


def shared_expert(f_a, f_b, w_gate, w_up, w_down):
    t, dq = f_a.shape
    d = w_down.shape[1]
    tm = min(512, t)
    row = lambda i: (i, 0)
    fixed = lambda i: (0, 0)
    return pl.pallas_call(
        _shared_expert_kernel,
        grid=(t // tm,),
        in_specs=[pl.BlockSpec((tm, dq), row), pl.BlockSpec((tm, dq), row),
                  pl.BlockSpec(w_gate.shape, fixed), pl.BlockSpec(w_up.shape, fixed), pl.BlockSpec(w_down.shape, fixed)],
        out_specs=pl.BlockSpec((tm, d), row),
        out_shape=jax.ShapeDtypeStruct((t, d), BF16),
        compiler_params=_params("parallel"),
    )(f_a, f_b, w_gate, w_up, w_down)


def _ffn_prologue(h, shf_ref, scf_ref, wr_ref, h_ref, fa_ref, fb_ref, lg_ref):
    h_ref[...] = h
    f = _rms(h) * (1.0 + scf_ref[...]) + shf_ref[...]
    packed = _pack_rows(f)
    half = packed.shape[1] // 2
    fa_ref[...] = packed[:, :half]
    fb_ref[...] = packed[:, half:]
    lg_ref[...] = lax.dot_general(wr_ref[...], f, (((1,), (1,)), ((), ())),
                                  preferred_element_type=F32, precision=HIGHEST)


def _gelu_tanh(x):
    return 0.5 * x * (1.0 + jnp.tanh(math.sqrt(2.0 / math.pi) * (x + 0.044715 * (x * x * x))))


def _even_post_kernel(y_ref, na_ref, x_ref, wglu_ref, bglu_ref, wo_ref, gm_ref, *rest):
    g = _gelu_tanh(y_ref[...])
    gate = jax.nn.sigmoid(jnp.dot(g.astype(BF16), wglu_ref[...], preferred_element_type=F32) + bglu_ref[...])
    s5 = (g * gate).astype(BF16)
    w = s5.shape[1]
    mix = (jnp.dot(s5, wo_ref[:w, :], preferred_element_type=F32)
           + jnp.dot(na_ref[...], wo_ref[w:, :], preferred_element_type=F32))
    _ffn_prologue(x_ref[...] + gm_ref[...] * mix, *rest)


def _odd_post_kernel(zre_ref, zim_ref, h_ref_in, cc_ref, sc_ref, wf_ref, bf_ref, gm_ref, *rest):
    c = cc_ref.shape[0]
    parts = []
    for grp in range(zre_ref.shape[1] // c):
        cols = slice(grp * c, (grp + 1) * c)
        parts.append(jnp.dot(zre_ref[:, cols], cc_ref[...], preferred_element_type=F32)
                     + jnp.dot(zim_ref[:, cols], sc_ref[...], preferred_element_type=F32))
    fr = jnp.concatenate(parts, axis=-1).astype(BF16)
    mix = jnp.dot(fr, wf_ref[...], preferred_element_type=F32) + bf_ref[...]
    _ffn_prologue(h_ref_in[...] + gm_ref[...] * mix, *rest)


def _post_call(body, row_inputs, fixed_inputs, t, d, n_exp):
    tm = min(256, t)
    row = lambda i: (i, 0)
    fixed = lambda i: (0, 0)
    in_specs = ([pl.BlockSpec((tm, a.shape[1]), row) for a in row_inputs]
                + [pl.BlockSpec(a.shape, fixed) for a in fixed_inputs])
    return pl.pallas_call(
        body,
        grid=(t // tm,),
        in_specs=in_specs,
        out_specs=[pl.BlockSpec((tm, d), row), pl.BlockSpec((tm, d // 4), row), pl.BlockSpec((tm, d // 4), row),
                   pl.BlockSpec((n_exp, tm), lambda i: (0, i))],
        out_shape=[jax.ShapeDtypeStruct((t, d), F32),
                   jax.ShapeDtypeStruct((t, d // 4), U32), jax.ShapeDtypeStruct((t, d // 4), U32),
                   jax.ShapeDtypeStruct((n_exp, t), F32)],
        compiler_params=_params("parallel"),
    )(*row_inputs, *fixed_inputs)


def _route_kernel(lg_ref, bias_ref, tri_ref, idx_ref, gate_ref, rank_ref, cnt_ref, run_ref):
    @pl.when(pl.program_id(0) == 0)
    def _():
        run_ref[...] = jnp.zeros_like(run_ref)

    scores = jax.nn.sigmoid(lg_ref[...])
    n_exp, tb = scores.shape
    sel = scores + bias_ref[...]
    gsz = n_exp // N_EXPERT_GROUPS
    member = lax.broadcasted_iota(I32, (gsz, tb), 0)
    gscore = []
    for grp in range(N_EXPERT_GROUPS):
        xg = sel[grp * gsz:(grp + 1) * gsz, :]
        m1 = jnp.max(xg, axis=0, keepdims=True)
        first = jnp.min(jnp.where(xg == m1, member, gsz), axis=0, keepdims=True)
        m2 = jnp.max(jnp.where(member == first, -jnp.inf, xg), axis=0, keepdims=True)
        gscore.append(m1 + m2)
    keep_rows = []
    for grp in range(N_EXPERT_GROUPS):
        beaten = jnp.zeros((1, tb), F32)
        for other in range(N_EXPERT_GROUPS):
            if other == grp:
                continue
            wins = (gscore[other] >= gscore[grp]) if other < grp else (gscore[other] > gscore[grp])
            beaten = beaten + jnp.where(wins, 1.0, 0.0)
        keep_rows.append(jnp.broadcast_to(beaten < TOPK_GROUPS, (gsz, tb)))
    masked = jnp.where(jnp.concatenate(keep_rows, axis=0), sel, -jnp.inf)

    expert = lax.broadcasted_iota(I32, (n_exp, tb), 0)
    picks, gates, hots = [], [], []
    chosen = jnp.zeros((n_exp, tb), F32)
    for _ in range(TOP_K):
        m = jnp.max(masked, axis=0, keepdims=True)
        pick = jnp.min(jnp.where(masked == m, expert, n_exp), axis=0, keepdims=True)
        hot = expert == pick
        picks.append(pick)
        hots.append(hot)
        gates.append(jnp.sum(jnp.where(hot, scores, 0.0), axis=0, keepdims=True))
        chosen = jnp.where(hot, 1.0, chosen)
        masked = jnp.where(hot, -jnp.inf, masked)
    total = gates[0]
    for gk in gates[1:]:
        total = total + gk
    ahead = jnp.dot(chosen.astype(BF16), tri_ref[...], preferred_element_type=F32) + run_ref[...]
    for k in range(TOP_K):
        idx_ref[k:k + 1, :] = picks[k]
        gate_ref[k:k + 1, :] = ROUTED_SCALE * gates[k] / total
        rank_ref[k:k + 1, :] = jnp.sum(jnp.where(hots[k], ahead, 0.0), axis=0, keepdims=True).astype(I32)
    run_ref[...] = run_ref[...] + jnp.sum(chosen, axis=1, keepdims=True)
    cnt_ref[...] = jnp.broadcast_to(run_ref[...], cnt_ref.shape)


def route(logits_t, router_bias):
    n_exp, t = logits_t.shape
    tb = min(512, t)
    tri = jnp.asarray(np.triu(np.ones((tb, tb), np.float32), k=1), BF16)
    tok = lambda i: (0, i)
    idx, gate, rank, cnt = pl.pallas_call(
        _route_kernel,
        grid=(t // tb,),
        in_specs=[pl.BlockSpec((n_exp, tb), tok), pl.BlockSpec((n_exp, 1), lambda i: (0, 0)),
                  pl.BlockSpec((tb, tb), lambda i: (0, 0))],
        out_specs=[pl.BlockSpec((TOP_K, tb), tok)] * 3 + [pl.BlockSpec((n_exp, 128), lambda i: (0, 0))],
        out_shape=[jax.ShapeDtypeStruct((TOP_K, t), I32), jax.ShapeDtypeStruct((TOP_K, t), F32),
                   jax.ShapeDtypeStruct((TOP_K, t), I32), jax.ShapeDtypeStruct((n_exp, 128), F32)],
        scratch_shapes=[pltpu.VMEM((n_exp, 1), F32)],
        compiler_params=_params("arbitrary"),
    )(logits_t, router_bias.astype(F32).reshape(n_exp, 1), tri)
    return idx, gate, rank, cnt[:, 0].astype(I32)


def dispatch_plan(idx, rank, counts, n_blocks):
    n_exp = counts.shape[0]
    padded = (counts + EXPERT_ROWS - 1) // EXPERT_ROWS * EXPERT_ROWS
    pad_end = jnp.cumsum(padded)
    pad_start = pad_end - padded
    experts_iota = jnp.arange(n_exp, dtype=I32)
    dest = jnp.sum(jnp.where(idx[..., None] == experts_iota, pad_start, 0), axis=-1) + rank
    n_valid = (pad_end[-1] // EXPERT_ROWS).astype(I32)
    blk = jnp.minimum(jnp.arange(n_blocks, dtype=I32), n_valid - 1)
    blk_e = jnp.sum((pad_end[None, :] <= blk[:, None] * EXPERT_ROWS).astype(I32), axis=1)
    return dest.astype(I32), jnp.minimum(blk_e, n_exp - 1).astype(I32), n_valid.reshape(1)


def _sc_mesh():
    return plsc.VectorSubcoreMesh(core_axis_name="core", subcore_axis_name="subcore")


def dispatch(f, dest, cap):
    t, w = f.shape
    n_choice = dest.shape[0]

    @functools.partial(pl.kernel, out_type=jax.ShapeDtypeStruct((cap, w), f.dtype), mesh=_sc_mesh(),
                       scratch_types=[])
    def scatter_rows(x_hbm, i_hbm, o_hbm):
        def body(x_vmem, i_vmem):
            pltpu.sync_copy(x_vmem, o_hbm.at[i_vmem.at[0]])

        pltpu.emit_pipeline(
            body,
            grid=(t // SC_WINDOW, n_choice),
            in_specs=[pl.BlockSpec((SC_WINDOW, w), lambda i, k: (i, 0)),
                      pl.BlockSpec((1, SC_WINDOW), lambda i, k: (k, i))],
            out_specs=[],
            core_axis_name=("core", "subcore"),
            dimension_semantics=(pltpu.PARALLEL, pltpu.ARBITRARY),
        )(x_hbm, i_hbm)

    return scatter_rows(f, dest)


def gather_rows(rows, index_row):
    n = index_row.shape[1]
    w = rows.shape[1]

    @functools.partial(pl.kernel, out_type=jax.ShapeDtypeStruct((n, w), rows.dtype), mesh=_sc_mesh(),
                       scratch_types=[])
    def gather(y_hbm, i_hbm, o_hbm):
        def body(i_vmem, o_vmem):
            pltpu.sync_copy(y_hbm.at[i_vmem.at[0]], o_vmem)

        pltpu.emit_pipeline(
            body,
            grid=(n // SC_WINDOW,),
            in_specs=[pl.BlockSpec((1, SC_WINDOW), lambda i: (0, i))],
            out_specs=[pl.BlockSpec((SC_WINDOW, w), lambda i: (i, 0))],
            core_axis_name=("core", "subcore"),
            dimension_semantics=(pltpu.PARALLEL,),
        )(i_hbm, o_hbm)

    return gather(rows, index_row)


def _experts_kernel(be_ref, nv_ref, xa_ref, xb_ref, wg_ref, wu_ref, wd_ref, ya_ref, yb_ref, wg_bf, wu_bf, wd_bf):
    b = pl.program_id(0)

    @pl.when((b == 0) | (be_ref[b] != be_ref[jnp.maximum(b - 1, 0)]))
    def _():
        wg_bf[...] = wg_ref[0, 0].astype(BF16)
        wu_bf[...] = wu_ref[0, 0].astype(BF16)
        wd_bf[...] = wd_ref[0, 0].astype(BF16)

    @pl.when(b < nv_ref[0])
    def _():
        x = _unpack_rows(jnp.concatenate([xa_ref[...], xb_ref[...]], axis=1)).astype(BF16)
        hid = (_silu(jnp.dot(x, wg_bf[...], preferred_element_type=F32))
               * jnp.dot(x, wu_bf[...], preferred_element_type=F32))
        packed = _pack_rows(jnp.dot(hid.astype(BF16), wd_bf[...], preferred_element_type=F32))
        half = packed.shape[1] // 2
        ya_ref[...] = packed[:, :half]
        yb_ref[...] = packed[:, half:]

    @pl.when(b >= nv_ref[0])
    def _():
        ya_ref[...] = jnp.zeros_like(ya_ref)
        yb_ref[...] = jnp.zeros_like(yb_ref)


def experts(rows_a, rows_b, blk_e, n_valid, layer, w_gate, w_up, w_down):
    cap, dq = rows_a.shape
    _, _, d, ff = w_gate.shape
    n_blocks = cap // EXPERT_ROWS
    live = lambda b, be, nv: (jnp.minimum(b, nv[0] - 1), 0)
    out_blk = pl.BlockSpec((EXPERT_ROWS, dq), lambda b, be, nv: (b, 0))
    out = jax.ShapeDtypeStruct((cap, dq), U32)
    return pl.pallas_call(
        _experts_kernel,
        grid_spec=pltpu.PrefetchScalarGridSpec(
            num_scalar_prefetch=2,
            grid=(n_blocks,),
            in_specs=[pl.BlockSpec((EXPERT_ROWS, dq), live), pl.BlockSpec((EXPERT_ROWS, dq), live),
                      pl.BlockSpec((1, 1, d, ff), lambda b, be, nv: (layer, be[b], 0, 0)),
                      pl.BlockSpec((1, 1, d, ff), lambda b, be, nv: (layer, be[b], 0, 0)),
                      pl.BlockSpec((1, 1, ff, d), lambda b, be, nv: (layer, be[b], 0, 0))],
            out_specs=[out_blk, out_blk],
            scratch_shapes=[pltpu.VMEM((d, ff), BF16), pltpu.VMEM((d, ff), BF16), pltpu.VMEM((ff, d), BF16)],
        ),
        out_shape=[out, out],
        compiler_params=_params("arbitrary"),
    )(blk_e, n_valid, rows_a, rows_b, w_gate, w_up, w_down)


def _combine_kernel(gate_ref, h_ref, shared_ref, gf_ref, sh_ref, sc_ref, *rest):
    ya_refs, yb_refs, outs = rest[:TOP_K], rest[TOP_K:2 * TOP_K], rest[2 * TOP_K:]
    routed = None
    for k in range(TOP_K):
        y = _unpack_rows(jnp.concatenate([ya_refs[k][0], yb_refs[k][0]], axis=1))
        routed = gate_ref[:, k:k + 1] * y if routed is None else routed + gate_ref[:, k:k + 1] * y
    h = h_ref[...] + gf_ref[...] * (routed + shared_ref[...].astype(F32))
    outs[0][...] = h
    if len(outs) > 1:
        outs[1][...] = _rms(h) * (1.0 + sc_ref[...]) + sh_ref[...]


def combine(y_a, y_b, dest, gate_tk, h, shared, gate_ffn, next_shift, next_scale, with_next):
    t, d = h.shape
    n_out = 2 if with_next else 1
    n_choice = dest.shape[0]
    flat = dest.reshape(1, n_choice * t)
    picked_a = gather_rows(y_a, flat).reshape(n_choice, t, d // 4)
    picked_b = gather_rows(y_b, flat).reshape(n_choice, t, d // 4)
    tm = min(256, t)
    row = lambda i: (i, 0)
    fixed = lambda i: (0, 0)
    choice = [pl.BlockSpec((1, tm, d // 4), functools.partial(lambda i, k: (k, i, 0), k=k))
              for k in range(n_choice)] * 2
    return pl.pallas_call(
        _combine_kernel,
        grid=(t // tm,),
        in_specs=[pl.BlockSpec((tm, n_choice), row), pl.BlockSpec((tm, d), row), pl.BlockSpec((tm, d), row),
                  pl.BlockSpec((1, d), fixed), pl.BlockSpec((1, d), fixed), pl.BlockSpec((1, d), fixed)] + choice,
        out_specs=[pl.BlockSpec((tm, d), row)] * n_out,
        out_shape=[jax.ShapeDtypeStruct((t, d), F32)] * n_out,
        compiler_params=_params("parallel"),
    )(gate_tk, h, shared, gate_ffn, next_shift, next_scale, *([picked_a] * n_choice), *([picked_b] * n_choice))


def moe_tail(h, f_a, f_b, logits_t, shared_weights, router_bias, layer, w_gate, w_up, w_down, gate_ffn,
             next_shift, next_scale, with_next):
    t, d = h.shape
    n_exp = w_gate.shape[1]
    n_blocks = -(-(t * TOP_K + n_exp * (EXPERT_ROWS - 1)) // EXPERT_ROWS)
    cap = n_blocks * EXPERT_ROWS
    idx, gate, rank, counts = route(logits_t, router_bias)
    dest, blk_e, n_valid = dispatch_plan(idx, rank, counts, n_blocks)
    rows_a, rows_b = dispatch(f_a, dest, cap), dispatch(f_b, dest, cap)
    shared = shared_expert(f_a, f_b, *shared_weights)
    y_a, y_b = experts(rows_a, rows_b, blk_e, n_valid, layer, w_gate, w_up, w_down)
    return combine(y_a, y_b, dest, gate.T, h, shared, gate_ffn, next_shift, next_scale, with_next)


def kernel(x, c, ctx, c_ctx, w_ada, b_ada, w_in, s5_lam_re, s5_lam_im, s5_log_dt, s5_b_re, s5_b_im,
           s5_c_re, s5_c_im, s5_d, s5_w_glu, s5_b_glu, na_q_gain, na_k_gain, na_rpb, w_mix_out,
           w_fourier_out, b_fourier_out, w_router, router_bias, w_exp_gate, w_exp_up, w_exp_down,
           w_sh_gate, w_sh_up, w_sh_down):
    bsz, t, d = x.shape
    assert bsz == 1 and w_ada.shape[0] == 2
    n_exp = w_router.shape[2]
    s5w = s5_w_glu.shape[1]
    naw = w_in.shape[2] - s5w
    naw //= 3
    heads = naw // NA_HEAD_DIM

    cond8 = jnp.zeros((8, d), F32).at[0].set(c[0].astype(F32)).at[1].set(c_ctx.astype(F32))
    ada = adaln_all(cond8, w_ada, b_ada)
    mod = lambda layer, who, j: ada[layer, who:who + 1, j * d:(j + 1) * d]

    def ffn_weights(i):
        return mod(i, 0, 3), mod(i, 0, 4), jnp.transpose(w_router[i]).astype(F32)

    def shared_weights(i):
        return w_sh_gate[i].astype(BF16), w_sh_up[i].astype(BF16), w_sh_down[i].astype(BF16)

    h0 = x[0]
    seg = jnp.asarray(np.kron(np.eye(heads), np.ones((NA_HEAD_DIM, NA_HEAD_DIM))), BF16)
    w_in_b = w_in[0].astype(BF16)
    qg = jnp.tile(na_q_gain[0].astype(F32), heads)[None]
    kg = jnp.tile(na_k_gain[0].astype(F32), heads)[None]
    u_c, _, k_c, v_c = in_projection(ctx[0], mod(0, 1, 0), mod(0, 1, 1), w_in_b, seg, qg, kg, s5w, naw)
    u_l, q_l, k_l, v_l = in_projection(h0, mod(0, 0, 0), mod(0, 0, 1), w_in_b, seg, qg, kg, s5w, naw)
    mats = s5_matrices(s5_lam_re[0], s5_lam_im[0], s5_log_dt[0], s5_b_re[0], s5_b_im[0],
                       s5_c_re[0], s5_c_im[0], s5_d[0])
    y_s5 = s5_mixer(u_c, u_l, mats)
    head_major = lambda a: jnp.transpose(a.reshape(a.shape[0], heads, NA_HEAD_DIM), (1, 0, 2))
    na = neighbourhood_attention(head_major(q_l), head_major(k_l), head_major(v_l),
                                 head_major(k_c), head_major(v_c), na_bias_table(na_rpb[0]))
    na = jnp.transpose(na, (1, 0, 2)).reshape(t, naw)
    h1, f1a, f1b, lg1 = _post_call(
        _even_post_kernel, [y_s5, na, h0],
        [s5_w_glu[0].astype(BF16), s5_b_glu[0].astype(F32)[None], w_mix_out[0].astype(BF16), mod(0, 0, 2),
         *ffn_weights(0)], t, d, n_exp)
    h2, a1 = moe_tail(h1, f1a, f1b, lg1, shared_weights(0), router_bias[0], 0, w_exp_gate, w_exp_up, w_exp_down,
                      mod(0, 0, 5), mod(1, 0, 0), mod(1, 0, 1), True)

    zre, zim = time_dft(a1)
    cc, sc = channel_dft_tables(d // FOURIER_GROUPS)
    h3, f3a, f3b, lg3 = _post_call(
        _odd_post_kernel, [zre, zim, h2],
        [cc, sc, w_fourier_out[0].astype(BF16), b_fourier_out[0].astype(F32)[None], mod(1, 0, 2),
         *ffn_weights(1)], t, d, n_exp)
    zero_row = jnp.zeros((1, d), F32)
    (out,) = moe_tail(h3, f3a, f3b, lg3, shared_weights(1), router_bias[1], 1, w_exp_gate, w_exp_up, w_exp_down,
                      mod(1, 0, 5), zero_row, zero_row, False)
    return out[None]
```

```python
import functools
import math

import numpy as np
import jax
import jax.numpy as jnp
from jax import lax
from jax.experimental import pallas as pl
from jax.experimental.pallas import tpu as pltpu
from jax.experimental.pallas import tpu_sc as plsc

F32 = jnp.float32
BF16 = jnp.bfloat16
I32 = jnp.int32
U32 = jnp.uint32
HIGHEST = lax.Precision.HIGHEST

LANES = 128
GRID_W = 64
NORM_EPS = 1e-6
S5_GROUP = 16
S5_STATE = 64
S5_LAMBDA_RE_MAX = -1e-4
S5_CHUNK = 16
S5_TILE = 128
NA_HEADS = 8
NA_HEAD_DIM = 64
NA_KH = 8
NA_KW = 16
FOURIER_GROUPS = 4
N_EXPERT_GROUPS = 8
TOPK_GROUPS = 4
TOP_K = 8
ROUTED_SCALE = 2.5
EXPERT_ROWS = 512
COMBINE_PARTS = 2
SC_WINDOW = 128
NEG_BIG = -1e30

VMEM_LIMIT_BYTES = 56 * 1024 * 1024


def _params(*sem):
    return pltpu.CompilerParams(dimension_semantics=sem or None,
                                vmem_limit_bytes=VMEM_LIMIT_BYTES)


def _rms(x):
    return x * lax.rsqrt(jnp.mean(x * x, axis=-1, keepdims=True) + NORM_EPS)


def _silu(x):
    return x * jax.nn.sigmoid(x)


def _pack_pair(lo, hi):
    lo = lax.bitcast_convert_type(lo.astype(BF16).astype(F32), U32)
    hi = lax.bitcast_convert_type(hi.astype(BF16).astype(F32), U32)
    return (hi & jnp.uint32(0xFFFF0000)) | (lo >> 16)


def _unpack_pair(w):
    return (lax.bitcast_convert_type(w << 16, F32), lax.bitcast_convert_type(w & jnp.uint32(0xFFFF0000), F32))


def _pack_rows(x):
    n = x.shape[1] // 2
    return _pack_pair(x[:, :n], x[:, n:])


def _unpack_rows(w):
    return jnp.concatenate(_unpack_pair(w), axis=1)


def _ada_kernel(c_ref, w_ref, b_ref, o_ref):
    o_ref[0] = jnp.dot(_silu(c_ref[...]), w_ref[0], preferred_element_type=F32,
                       precision=HIGHEST) + b_ref[0]


def adaln_all(cond8, w_ada, b_ada):
    n_layers, d, n6 = w_ada.shape
    tn = n6 // 4
    return pl.pallas_call(
        _ada_kernel,
        grid=(n_layers, n6 // tn),
        in_specs=[pl.BlockSpec((8, d), lambda l, j: (0, 0)),
                  pl.BlockSpec((1, d, tn), lambda l, j: (l, 0, j)),
                  pl.BlockSpec((1, 1, tn), lambda l, j: (l, 0, j))],
        out_specs=pl.BlockSpec((1, 8, tn), lambda l, j: (l, 0, j)),
        out_shape=jax.ShapeDtypeStruct((n_layers, 8, n6), F32),
        compiler_params=_params("parallel", "parallel"),
    )(cond8, w_ada, b_ada.reshape(n_layers, 1, n6))


def _inproj_kernel(x_ref, sh_ref, sc_ref, w_ref, seg_ref, qg_ref, kg_ref,
                   u_ref, q_ref, k_ref, v_ref):
    a = _rms(x_ref[...]) * (1.0 + sc_ref[...]) + sh_ref[...]
    z = jnp.dot(a.astype(BF16), w_ref[...], preferred_element_type=F32)
    s5w = u_ref.shape[1]
    naw = q_ref.shape[1]

    def head_norm(t, gain):
        ss = jnp.dot((t * t).astype(BF16), seg_ref[...], preferred_element_type=F32)
        return t * lax.rsqrt(ss * (1.0 / NA_HEAD_DIM) + NORM_EPS) * gain

    u_ref[...] = z[:, :s5w]
    q_ref[...] = head_norm(z[:, s5w:s5w + naw], qg_ref[...]).astype(BF16)
    k_ref[...] = head_norm(z[:, s5w + naw:s5w + 2 * naw], kg_ref[...]).astype(BF16)
    v_ref[...] = z[:, s5w + 2 * naw:].astype(BF16)


def in_projection(x, shift, scale, w_in_bf16, seg_ones, q_gain_row, k_gain_row, s5w, naw):
    t, d = x.shape
    tm = min(512, t)
    row = lambda i: (i, 0)
    fixed = lambda i: (0, 0)
    return pl.pallas_call(
        _inproj_kernel,
        grid=(t // tm,),
        in_specs=[pl.BlockSpec((tm, d), row),
                  pl.BlockSpec((1, d), fixed), pl.BlockSpec((1, d), fixed),
                  pl.BlockSpec(w_in_bf16.shape, fixed),
                  pl.BlockSpec(seg_ones.shape, fixed),
                  pl.BlockSpec((1, naw), fixed), pl.BlockSpec((1, naw), fixed)],
        out_specs=[pl.BlockSpec((tm, s5w), row), pl.BlockSpec((tm, naw), row),
                   pl.BlockSpec((tm, naw), row), pl.BlockSpec((tm, naw), row)],
        out_shape=[jax.ShapeDtypeStruct((t, s5w), F32)] + [jax.ShapeDtypeStruct((t, naw), BF16)] * 3,
        compiler_params=_params("parallel"),
    )(x, shift, scale, w_in_bf16, seg_ones, q_gain_row, k_gain_row)


def s5_matrices(lam_re, lam_im, log_dt, b_re, b_im, c_re, c_im, d_skip):
    L = S5_CHUNK
    taus = jnp.arange(L + 1, dtype=F32)

    def direction(i):
        lam = lax.complex(jnp.minimum(lam_re[i].astype(F32), S5_LAMBDA_RE_MAX), lam_im[i].astype(F32))
        ldt = lam * jnp.exp(log_dt[i].astype(F32))[:, None]
        lam_bar = jnp.exp(ldt)
        b_bar = ((lam_bar - 1.0) / lam)[..., None] * lax.complex(b_re[i].astype(F32), b_im[i].astype(F32))
        cc = lax.complex(c_re[i].astype(F32), c_im[i].astype(F32))
        powers = jnp.exp(ldt[None] * taus[:, None, None])
        resp = jnp.real(jnp.einsum('gcp,tgp,gpd->gtcd', cc, powers[:L], b_bar, precision=HIGHEST))
        return powers, b_bar, cc, resp

    pw_f, bb_f, cc_f, k_f = direction(0)
    pw_b, bb_b, cc_b, k_b = direction(1)
    g, p = pw_f.shape[1:]
    c = S5_GROUP
    diff = np.arange(L)[None, :] - np.arange(L)[:, None]
    lag = np.arange(L)[:, None, None]
    tf = jnp.einsum('tsl,gtcd->gslcd', jnp.asarray(lag == diff[None], F32), k_f, precision=HIGHEST)
    tb = jnp.einsum('tsl,gtcd->gslcd', jnp.asarray(lag == -diff[None], F32), k_b, precision=HIGHEST)
    skip = (jnp.eye(L, dtype=F32)[None, :, :, None, None]
            * (jnp.eye(c, dtype=F32)[None] * d_skip.astype(F32)[:, :, None])[:, None, None])
    toep = jnp.transpose(tf + tb + skip, (0, 1, 4, 2, 3)).reshape(g, L * c, L * c)

    def state_in(powers_sel, b_bar):
        w = powers_sel[:, :, :, None] * b_bar[None]
        return jnp.transpose(w, (1, 0, 3, 2)).reshape(g, L * c, p)

    wf = state_in(pw_f[L - 1 - jnp.arange(L)], bb_f)
    wb = state_in(pw_b[jnp.arange(L)], bb_b)
    w_state = jnp.concatenate([jnp.real(wf), jnp.imag(wf), jnp.imag(wf), jnp.real(wf),
                               jnp.real(wb), jnp.imag(wb), jnp.imag(wb), jnp.real(wb)], axis=-1)

    def state_out(powers_sel, cc):
        r = powers_sel[:, :, None, :] * cc[None]
        return jnp.transpose(r, (1, 3, 0, 2)).reshape(g, p, L * c)

    rf = state_out(pw_f[1 + jnp.arange(L)], cc_f)
    rb = state_out(pw_b[L - jnp.arange(L)], cc_b)
    r_state = jnp.concatenate([jnp.real(rf), -jnp.imag(rf), jnp.real(rb), -jnp.imag(rb)], axis=1)

    def mult(a):
        ar, ai = jnp.real(a), jnp.imag(a)
        return jnp.stack([jnp.concatenate([ar, ar], -1), jnp.concatenate([-ai, ai], -1),
                          jnp.concatenate([ai, -ai], -1)])

    return toep, w_state, r_state, mult(pw_f[L]), mult(pw_b[L])


def _s5_pack_kernel(*refs):
    u_refs, (wt_ref, ut_ref, f1_ref, f2_ref, b1_ref, b2_ref) = refs[:-6], refs[-6:]
    L = S5_CHUNK
    g, lc, nck = ut_ref.shape
    c = lc // L
    gs = g // len(u_refs)
    for s in range(L):
        for j, u_ref in enumerate(u_refs):
            step_s = u_ref[pl.ds(s, nck, stride=L), :]
            ut_ref[j * gs:(j + 1) * gs, s * c:(s + 1) * c, :] = (
                jnp.transpose(step_s).astype(BF16).reshape(gs, c, nck))
    n = f1_ref.shape[2]
    for gi in range(g):
        inc = jnp.dot(wt_ref[gi], ut_ref[gi], preferred_element_type=F32)
        for j, ref in enumerate((f1_ref, f2_ref, b1_ref, b2_ref)):
            ref[:, gi, :] = jnp.transpose(inc[j * n:(j + 1) * n, :])


def s5_pack(u, wt_state):
    t, w = u.shape
    g, n4, lc = wt_state.shape
    nc = t // S5_CHUNK
    tile = min(S5_TILE, nc)
    inc = jax.ShapeDtypeStruct((nc, g, n4 // 4), F32)
    inc_blk = pl.BlockSpec((tile, g, n4 // 4), lambda i: (i, 0, 0))
    return pl.pallas_call(
        _s5_pack_kernel,
        grid=(nc // tile,),
        in_specs=[pl.BlockSpec((tile * S5_CHUNK, LANES), functools.partial(lambda i, j: (i, j), j=j))
                  for j in range(w // LANES)] + [pl.BlockSpec(wt_state.shape, lambda i: (0, 0, 0))],
        out_specs=[pl.BlockSpec((g, lc, tile), lambda i: (0, 0, i))] + [inc_blk] * 4,
        out_shape=[jax.ShapeDtypeStruct((g, lc, nc), BF16)] + [inc] * 4,
        compiler_params=_params("parallel"),
    )(*([u] * (w // LANES)), wt_state)


def _s5_scan_kernel(s1_ref, s2_ref, m_ref, init_ref, x_ref, last_ref, v1_ref, v2_ref, *, reverse):
    @pl.when(pl.program_id(0) == 0)
    def _():
        v1_ref[...] = init_ref[0]
        v2_ref[...] = init_ref[1]

    a1, a2, a3 = m_ref[0], m_ref[1], m_ref[2]
    cb = s1_ref.shape[0]

    def body(j, carry):
        v1, v2 = carry
        jj = cb - 1 - j if reverse else j
        x_ref[jj] = v1
        return (a1 * v1 + a2 * v2 + s1_ref[jj], a1 * v2 + a3 * v1 + s2_ref[jj])

    v1, v2 = lax.fori_loop(0, cb, body, (v1_ref[...], v2_ref[...]))
    v1_ref[...] = v1
    v2_ref[...] = v2
    last_ref[...] = v1


def s5_chunk_scan(s1, s2, mult, init, reverse):
    nc, g, n = s1.shape
    cb = min(S5_TILE, nc)
    nb = nc // cb
    blk = (lambda i: (nb - 1 - i, 0, 0)) if reverse else (lambda i: (i, 0, 0))
    return pl.pallas_call(
        functools.partial(_s5_scan_kernel, reverse=reverse),
        grid=(nb,),
        in_specs=[pl.BlockSpec((cb, g, n), blk), pl.BlockSpec((cb, g, n), blk),
                  pl.BlockSpec((3, g, n), lambda i: (0, 0, 0)), pl.BlockSpec((2, g, n), lambda i: (0, 0, 0))],
        out_specs=[pl.BlockSpec((cb, g, n), blk), pl.BlockSpec((g, n), lambda i: (0, 0))],
        out_shape=[jax.ShapeDtypeStruct((nc, g, n), F32), jax.ShapeDtypeStruct((g, n), F32)],
        scratch_shapes=[pltpu.VMEM((g, n), F32), pltpu.VMEM((g, n), F32)],
        compiler_params=_params("arbitrary"),
    )(s1, s2, mult, init)


def _s5_readout_kernel(ut_ref, tt_ref, rt_ref, xf_ref, xb_ref, y_ref, yt_ref, *slab_refs):
    L = S5_CHUNK
    g, lc, nck = ut_ref.shape
    c = lc // L
    gs = g // len(slab_refs)
    for gi in range(g):
        xin_t = jnp.concatenate([jnp.transpose(xf_ref[:, gi, :]), jnp.transpose(xb_ref[:, gi, :])], axis=0)
        yt_ref[gi] = (jnp.dot(tt_ref[gi], ut_ref[gi], preferred_element_type=F32)
                      + jnp.dot(rt_ref[gi], xin_t.astype(BF16), preferred_element_type=F32))
    for j, slab in enumerate(slab_refs):
        for l in range(L):
            step_l = yt_ref[j * gs:(j + 1) * gs, l * c:(l + 1) * c, :].reshape(gs * c, nck)
            slab[pl.ds(l, nck, stride=L), :] = jnp.transpose(step_l)
        y_ref[:, j * gs * c:(j + 1) * gs * c] = slab[...]


def s5_readout(ut, toep_t, r_state_t, xin_f, xin_b):
    g, lc, nc = ut.shape
    n = xin_f.shape[2]
    tile = min(S5_TILE, nc)
    fixed = lambda i: (0, 0, 0)
    state_blk = pl.BlockSpec((tile, g, n), lambda i: (i, 0, 0))
    return pl.pallas_call(
        _s5_readout_kernel,
        grid=(nc // tile,),
        in_specs=[pl.BlockSpec((g, lc, tile), lambda i: (0, 0, i)),
                  pl.BlockSpec(toep_t.shape, fixed), pl.BlockSpec(r_state_t.shape, fixed), state_blk, state_blk],
        out_specs=pl.BlockSpec((tile * S5_CHUNK, g * lc // S5_CHUNK), lambda i: (i, 0)),
        out_shape=jax.ShapeDtypeStruct((nc * S5_CHUNK, g * lc // S5_CHUNK), F32),
        scratch_shapes=[pltpu.VMEM((g, lc, tile), F32)]
        + [pltpu.VMEM((tile * S5_CHUNK, LANES), F32)] * (g * lc // S5_CHUNK // LANES),
        compiler_params=_params("parallel"),
    )(ut, toep_t, r_state_t, xin_f, xin_b)


def s5_mixer(u_ctx, u_lat, mats):
    toep, w_state, r_state, mult_f, mult_b = mats
    L = S5_CHUNK
    g, n = mult_f.shape[1:]
    swap = lambda a: jnp.transpose(a, (0, 2, 1)).astype(BF16)
    wt_state, toep_t, r_state_t = swap(w_state), swap(toep), swap(r_state)
    halves = lambda v: jnp.stack([v, jnp.roll(v, n // 2, axis=-1)])
    n_ctx = u_ctx.shape[0] // L
    ctx_chunks = -(-(n_ctx + 1) // S5_TILE) * S5_TILE
    ctx_pad = jnp.zeros((ctx_chunks * L, u_ctx.shape[1]), F32).at[:u_ctx.shape[0]].set(u_ctx)
    _, cf1, cf2, cb1, cb2 = s5_pack(ctx_pad, wt_state)
    zero = jnp.zeros((2, g, n), F32)
    ctx_f, _ = s5_chunk_scan(cf1, cf2, mult_f, zero, False)
    _, ctx_b_last = s5_chunk_scan(cb1, cb2, mult_b, zero, True)
    ut, f1, f2, b1, b2 = s5_pack(u_lat, wt_state)
    xin_f, _ = s5_chunk_scan(f1, f2, mult_f, halves(ctx_f[n_ctx]), False)
    xin_b, _ = s5_chunk_scan(b1, b2, mult_b, halves(ctx_b_last), True)
    return s5_readout(ut, toep_t, r_state_t, xin_f, xin_b)


def na_bias_table(rpb):
    q_col = np.arange(GRID_W)
    col_start = np.clip(q_col - NA_KW // 2, 0, GRID_W - NA_KW)
    key_col = np.arange(GRID_W)
    off = key_col[None, :] - col_start[:, None]
    valid = (off >= 0) & (off < NA_KW)
    rel_col = np.clip(key_col[None, :] - q_col[:, None] + NA_KW - 1, 0, 2 * NA_KW - 2)
    pick = jnp.asarray(np.arange(2 * NA_KW - 1)[:, None, None] == rel_col[None], F32)
    full = jnp.where(valid[None, None], jnp.einsum('hrj,jck->hrck', rpb.astype(F32), pick, precision=HIGHEST),
                     NEG_BIG)
    variants = []
    for d in range(NA_KH):
        rows = [full[:, i - d + NA_KH - 1] for i in range(NA_KH)]
        variants.append(jnp.concatenate(rows, axis=-1))
    return jnp.stack(variants)


def _na_kernel(*refs, scale):
    q_ref = refs[0]
    k_refs = refs[1:1 + NA_KH]
    v_refs = refs[1 + NA_KH:1 + 2 * NA_KH]
    kc_ref, vc_ref, b_ref, o_ref = refs[1 + 2 * NA_KH:]
    hd = NA_HEAD_DIM
    heads = q_ref.shape[1] // hd
    split = lambda a: jnp.stack([a[:, i * hd:(i + 1) * hd] for i in range(heads)])
    q = split(q_ref[...] * scale)
    kk = split(jnp.concatenate([r[...] for r in k_refs], axis=0))
    vv = split(jnp.concatenate([r[...] for r in v_refs], axis=0))
    s = jnp.einsum('hqd,hkd->hqk', q, kk, preferred_element_type=F32) + b_ref[0]
    sc = jnp.einsum('hqd,hkd->hqk', q, split(kc_ref[...]), preferred_element_type=F32)
    m = jnp.maximum(jnp.max(s, axis=-1, keepdims=True), jnp.max(sc, axis=-1, keepdims=True))
    p = jnp.exp(s - m)
    pc = jnp.exp(sc - m)
    den = jnp.sum(p, axis=-1, keepdims=True) + jnp.sum(pc, axis=-1, keepdims=True)
    o = (jnp.einsum('hqk,hkd->hqd', p.astype(BF16), vv, preferred_element_type=F32)
         + jnp.einsum('hqk,hkd->hqd', pc.astype(BF16), split(vc_ref[...]), preferred_element_type=F32)) / den
    o_ref[...] = jnp.concatenate([o[i] for i in range(heads)], axis=1).astype(o_ref.dtype)


def neighbourhood_attention(q, k, v, k_ctx, v_ctx, bias_table):
    t, w = q.shape
    rows = t // GRID_W
    kh = min(NA_KH, rows)
    assert kh == NA_KH
    first = lambda r: jnp.clip(r - kh // 2, 0, rows - kh)
    row_blk = pl.BlockSpec((GRID_W, w), lambda r: (r, 0))
    key_blks = [pl.BlockSpec((GRID_W, w), functools.partial(lambda r, i: (first(r) + i, 0), i=i))
                for i in range(kh)]
    ctx_blk = pl.BlockSpec(k_ctx.shape, lambda r: (0, 0))
    bias_blk = pl.BlockSpec((1,) + bias_table.shape[1:], lambda r: (r - first(r), 0, 0, 0))
    return pl.pallas_call(
        functools.partial(_na_kernel, scale=NA_HEAD_DIM ** -0.5),
        grid=(rows,),
        in_specs=[row_blk] + key_blks + key_blks + [ctx_blk, ctx_blk, bias_blk],
        out_specs=row_blk,
        out_shape=jax.ShapeDtypeStruct((t, w), BF16),
        compiler_params=_params("parallel"),
    )(q, *([k] * kh), *([v] * kh), k_ctx, v_ctx, bias_table)


def time_dft_tables(t):
    a_len = 1 << (int(math.log2(t)) // 2)
    b_len = t // a_len
    ka = np.arange(a_len)[:, None]
    tok = b_len * np.arange(a_len)[None, :]
    ang1 = -2.0 * np.pi * ((ka * (tok[None] + np.arange(b_len)[:, None, None])) % t) / t
    stage1 = np.concatenate([np.cos(ang1), np.sin(ang1)], axis=1) / math.sqrt(t)
    ang2 = 2.0 * np.pi * ((np.arange(b_len)[:, None] * np.arange(b_len)[None, :]) % b_len) / b_len
    c2, s2 = np.cos(ang2), np.sin(ang2)
    stage2 = np.block([[c2, s2], [-s2, c2]])
    return jnp.asarray(stage1, BF16), jnp.asarray(stage2, BF16), a_len, b_len


def _time_dft_kernel(x_ref, m_ref, w2_ref, zre_ref, zim_ref, y_ref, z_ref, *, a_len, b_len):
    i = pl.program_id(1)
    bb = m_ref.shape[0]

    def stage1(jb, carry):
        b = i * bb + jb
        xb = x_ref[pl.ds(b, a_len, stride=b_len), :]
        y = jnp.dot(m_ref[jb], xb.astype(BF16), preferred_element_type=F32)
        row = pl.multiple_of(b * a_len, a_len)
        y_ref[pl.ds(row, a_len), :] = _pack_pair(y[:a_len], y[a_len:])
        return carry

    lax.fori_loop(0, bb, stage1, 0, unroll=4)

    @pl.when(i == pl.num_programs(1) - 1)
    def _():
        def stage2(ka, carry):
            rows = pl.ds(ka, b_len, stride=a_len)
            yre, yim = _unpack_pair(y_ref[rows, :])
            y = jnp.concatenate([yre, yim], axis=0).astype(BF16)
            z = jnp.dot(w2_ref[...], y, preferred_element_type=F32)
            z_ref[rows, :] = _pack_pair(z[:b_len], z[b_len:])
            return carry

        lax.fori_loop(0, a_len, stage2, 0, unroll=4)
        zre, zim = _unpack_pair(z_ref[...])
        zre_ref[...] = zre.astype(zre_ref.dtype)
        zim_ref[...] = zim.astype(zim_ref.dtype)


def time_dft(x):
    t, d = x.shape
    stage1, stage2, a_len, b_len = time_dft_tables(t)
    lanes = 128
    bb = min(16, b_len)
    out = jax.ShapeDtypeStruct((t, d), BF16)
    return pl.pallas_call(
        functools.partial(_time_dft_kernel, a_len=a_len, b_len=b_len),
        grid=(d // lanes, b_len // bb),
        in_specs=[pl.BlockSpec((t, lanes), lambda j, i: (0, j)),
                  pl.BlockSpec((bb, 2 * a_len, a_len), lambda j, i: (i, 0, 0)),
                  pl.BlockSpec(stage2.shape, lambda j, i: (0, 0))],
        out_specs=[pl.BlockSpec((t, lanes), lambda j, i: (0, j))] * 2,
        out_shape=[out, out],
        scratch_shapes=[pltpu.VMEM((t, lanes), U32), pltpu.VMEM((t, lanes), U32)],
        compiler_params=_params("parallel", "arbitrary"),
    )(x, stage1, stage2)


def channel_dft_tables(c):
    ang = 2.0 * np.pi * ((np.arange(c)[:, None] * np.arange(c)[None, :]) % c) / c
    return (jnp.asarray(np.cos(ang) / math.sqrt(c), BF16), jnp.asarray(np.sin(ang) / math.sqrt(c), BF16))


def _shared_expert_kernel(fa_ref, fb_ref, wsg_ref, wsu_ref, wsd_ref, o_ref):
    fb = _unpack_rows(jnp.concatenate([fa_ref[...], fb_ref[...]], axis=1)).astype(BF16)
    hid = (_silu(jnp.dot(fb, wsg_ref[...], preferred_element_type=F32))
           * jnp.dot(fb, wsu_ref[...], preferred_element_type=F32))
    o_ref[...] = jnp.dot(hid.astype(BF16), wsd_ref[...], preferred_element_type=F32).astype(o_ref.dtype)


def shared_expert(f_a, f_b, w_gate, w_up, w_down):
    t, dq = f_a.shape
    d = w_down.shape[1]
    tm = min(512, t)
    row = lambda i: (i, 0)
    fixed = lambda i: (0, 0)
    return pl.pallas_call(
        _shared_expert_kernel,
        grid=(t // tm,),
        in_specs=[pl.BlockSpec((tm, dq), row), pl.BlockSpec((tm, dq), row),
                  pl.BlockSpec(w_gate.shape, fixed), pl.BlockSpec(w_up.shape, fixed), pl.BlockSpec(w_down.shape, fixed)],
        out_specs=pl.BlockSpec((tm, d), row),
        out_shape=jax.ShapeDtypeStruct((t, d), BF16),
        compiler_params=_params("parallel"),
    )(f_a, f_b, w_gate, w_up, w_down)


def _ffn_prologue(h, shf_ref, scf_ref, wr_ref, h_ref, fa_ref, fb_ref, lg_ref):
    h_ref[...] = h
    f = _rms(h) * (1.0 + scf_ref[...]) + shf_ref[...]
    packed = _pack_rows(f)
    half = packed.shape[1] // 2
    fa_ref[...] = packed[:, :half]
    fb_ref[...] = packed[:, half:]
    lg_ref[...] = lax.dot_general(wr_ref[...], f, (((1,), (1,)), ((), ())),
                                  preferred_element_type=F32, precision=HIGHEST)


def _gelu_tanh(x):
    return 0.5 * x * (1.0 + jnp.tanh(math.sqrt(2.0 / math.pi) * (x + 0.044715 * (x * x * x))))


def _even_post_kernel(y_ref, na_ref, x_ref, wglu_ref, bglu_ref, wo_ref, gm_ref, *rest):
    g = _gelu_tanh(y_ref[...])
    gate = jax.nn.sigmoid(jnp.dot(g.astype(BF16), wglu_ref[...], preferred_element_type=F32) + bglu_ref[...])
    s5 = (g * gate).astype(BF16)
    w = s5.shape[1]
    mix = (jnp.dot(s5, wo_ref[:w, :], preferred_element_type=F32)
           + jnp.dot(na_ref[...], wo_ref[w:, :], preferred_element_type=F32))
    _ffn_prologue(x_ref[...] + gm_ref[...] * mix, *rest)


def _odd_post_kernel(zre_ref, zim_ref, h_ref_in, cc_ref, sc_ref, wf_ref, bf_ref, gm_ref, *rest):
    c = cc_ref.shape[0]
    parts = []
    for grp in range(zre_ref.shape[1] // c):
        cols = slice(grp * c, (grp + 1) * c)
        parts.append(jnp.dot(zre_ref[:, cols], cc_ref[...], preferred_element_type=F32)
                     + jnp.dot(zim_ref[:, cols], sc_ref[...], preferred_element_type=F32))
    fr = jnp.concatenate(parts, axis=-1).astype(BF16)
    mix = jnp.dot(fr, wf_ref[...], preferred_element_type=F32) + bf_ref[...]
    _ffn_prologue(h_ref_in[...] + gm_ref[...] * mix, *rest)


def _post_call(body, row_inputs, fixed_inputs, t, d, n_exp):
    tm = min(512, t)
    row = lambda i: (i, 0)
    fixed = lambda i: (0, 0)
    in_specs = ([pl.BlockSpec((tm, a.shape[1]), row) for a in row_inputs]
                + [pl.BlockSpec(a.shape, fixed) for a in fixed_inputs])
    return pl.pallas_call(
        body,
        grid=(t // tm,),
        in_specs=in_specs,
        out_specs=[pl.BlockSpec((tm, d), row), pl.BlockSpec((tm, d // 4), row), pl.BlockSpec((tm, d // 4), row),
                   pl.BlockSpec((n_exp, tm), lambda i: (0, i))],
        out_shape=[jax.ShapeDtypeStruct((t, d), F32),
                   jax.ShapeDtypeStruct((t, d // 4), U32), jax.ShapeDtypeStruct((t, d // 4), U32),
                   jax.ShapeDtypeStruct((n_exp, t), F32)],
        compiler_params=_params("parallel"),
    )(*row_inputs, *fixed_inputs)


def _route_kernel(lg_ref, bias_ref, tri_ref, idx_ref, gate_ref, rank_ref, cnt_ref, run_ref):
    @pl.when(pl.program_id(0) == 0)
    def _():
        run_ref[...] = jnp.zeros_like(run_ref)

    scores = jax.nn.sigmoid(lg_ref[...])
    n_exp, tb = scores.shape
    sel = scores + bias_ref[...]
    gsz = n_exp // N_EXPERT_GROUPS
    member = lax.broadcasted_iota(I32, (gsz, tb), 0)
    gscore = []
    for grp in range(N_EXPERT_GROUPS):
        xg = sel[grp * gsz:(grp + 1) * gsz, :]
        m1 = jnp.max(xg, axis=0, keepdims=True)
        first = jnp.min(jnp.where(xg == m1, member, gsz), axis=0, keepdims=True)
        m2 = jnp.max(jnp.where(member == first, -jnp.inf, xg), axis=0, keepdims=True)
        gscore.append(m1 + m2)
    keep_rows = []
    for grp in range(N_EXPERT_GROUPS):
        beaten = jnp.zeros((1, tb), F32)
        for other in range(N_EXPERT_GROUPS):
            if other == grp:
                continue
            wins = (gscore[other] >= gscore[grp]) if other < grp else (gscore[other] > gscore[grp])
            beaten = beaten + jnp.where(wins, 1.0, 0.0)
        keep_rows.append(jnp.broadcast_to(beaten < TOPK_GROUPS, (gsz, tb)))
    masked = jnp.where(jnp.concatenate(keep_rows, axis=0), sel, -jnp.inf)

    expert = lax.broadcasted_iota(I32, (n_exp, tb), 0)
    picks, gates, hots = [], [], []
    chosen = jnp.zeros((n_exp, tb), F32)
    for _ in range(TOP_K):
        m = jnp.max(masked, axis=0, keepdims=True)
        pick = jnp.min(jnp.where(masked == m, expert, n_exp), axis=0, keepdims=True)
        hot = expert == pick
        picks.append(pick)
        hots.append(hot)
        gates.append(jnp.sum(jnp.where(hot, scores, 0.0), axis=0, keepdims=True))
        chosen = jnp.where(hot, 1.0, chosen)
        masked = jnp.where(hot, -jnp.inf, masked)
    total = gates[0]
    for gk in gates[1:]:
        total = total + gk
    ahead = jnp.dot(chosen.astype(BF16), tri_ref[...], preferred_element_type=F32) + run_ref[...]
    for k in range(TOP_K):
        idx_ref[k:k + 1, :] = picks[k]
        gate_ref[k:k + 1, :] = ROUTED_SCALE * gates[k] / total
        rank_ref[k:k + 1, :] = jnp.sum(jnp.where(hots[k], ahead, 0.0), axis=0, keepdims=True).astype(I32)
    run_ref[...] = run_ref[...] + jnp.sum(chosen, axis=1, keepdims=True)
    cnt_ref[...] = jnp.broadcast_to(run_ref[...], cnt_ref.shape)


def route(logits_t, router_bias):
    n_exp, t = logits_t.shape
    tb = min(512, t)
    tri = jnp.asarray(np.triu(np.ones((tb, tb), np.float32), k=1), BF16)
    tok = lambda i: (0, i)
    idx, gate, rank, cnt = pl.pallas_call(
        _route_kernel,
        grid=(t // tb,),
        in_specs=[pl.BlockSpec((n_exp, tb), tok), pl.BlockSpec((n_exp, 1), lambda i: (0, 0)),
                  pl.BlockSpec((tb, tb), lambda i: (0, 0))],
        out_specs=[pl.BlockSpec((TOP_K, tb), tok)] * 3 + [pl.BlockSpec((n_exp, 128), lambda i: (0, 0))],
        out_shape=[jax.ShapeDtypeStruct((TOP_K, t), I32), jax.ShapeDtypeStruct((TOP_K, t), F32),
                   jax.ShapeDtypeStruct((TOP_K, t), I32), jax.ShapeDtypeStruct((n_exp, 128), F32)],
        scratch_shapes=[pltpu.VMEM((n_exp, 1), F32)],
        compiler_params=_params("arbitrary"),
    )(logits_t, router_bias.astype(F32).reshape(n_exp, 1), tri)
    return idx, gate, rank, cnt[:, 0].astype(I32)


def dispatch_plan(idx, rank, counts, n_blocks):
    n_exp = counts.shape[0]
    padded = (counts + EXPERT_ROWS - 1) // EXPERT_ROWS * EXPERT_ROWS
    pad_end = jnp.cumsum(padded)
    pad_start = pad_end - padded
    experts_iota = jnp.arange(n_exp, dtype=I32)
    dest = jnp.sum(jnp.where(idx[..., None] == experts_iota, pad_start, 0), axis=-1) + rank
    n_valid = (pad_end[-1] // EXPERT_ROWS).astype(I32)
    blk = jnp.minimum(jnp.arange(n_blocks, dtype=I32), n_valid - 1)
    blk_e = jnp.sum((pad_end[None, :] <= blk[:, None] * EXPERT_ROWS).astype(I32), axis=1)
    return dest.astype(I32), jnp.minimum(blk_e, n_exp - 1).astype(I32), n_valid.reshape(1)


def _sc_mesh():
    return plsc.VectorSubcoreMesh(core_axis_name="core", subcore_axis_name="subcore")


def dispatch(f, dest, cap):
    t, w = f.shape
    n_choice = dest.shape[0]

    @functools.partial(pl.kernel, out_type=jax.ShapeDtypeStruct((cap, w), f.dtype), mesh=_sc_mesh(),
                       scratch_types=[])
    def scatter_rows(x_hbm, i_hbm, o_hbm):
        def body(x_vmem, i_vmem):
            pltpu.sync_copy(x_vmem, o_hbm.at[i_vmem.at[0]])

        pltpu.emit_pipeline(
            body,
            grid=(t // SC_WINDOW, n_choice),
            in_specs=[pl.BlockSpec((SC_WINDOW, w), lambda i, k: (i, 0)),
                      pl.BlockSpec((1, SC_WINDOW), lambda i, k: (k, i))],
            out_specs=[],
            core_axis_name=("core", "subcore"),
            dimension_semantics=(pltpu.PARALLEL, pltpu.ARBITRARY),
        )(x_hbm, i_hbm)

    return scatter_rows(f, dest)


def gather_rows(rows, index_row):
    n = index_row.shape[1]
    w = rows.shape[1]

    @functools.partial(pl.kernel, out_type=jax.ShapeDtypeStruct((n, w), rows.dtype), mesh=_sc_mesh(),
                       scratch_types=[])
    def gather(y_hbm, i_hbm, o_hbm):
        def body(i_vmem, o_vmem):
            pltpu.sync_copy(y_hbm.at[i_vmem.at[0]], o_vmem)

        pltpu.emit_pipeline(
            body,
            grid=(n // SC_WINDOW,),
            in_specs=[pl.BlockSpec((1, SC_WINDOW), lambda i: (0, i))],
            out_specs=[pl.BlockSpec((SC_WINDOW, w), lambda i: (i, 0))],
            core_axis_name=("core", "subcore"),
            dimension_semantics=(pltpu.PARALLEL,),
        )(i_hbm, o_hbm)

    return gather(rows, index_row)


def _experts_kernel(be_ref, nv_ref, xa_ref, xb_ref, wg_ref, wu_ref, wd_ref, ya_ref, yb_ref, wg_bf, wu_bf, wd_bf):
    b = pl.program_id(0)

    @pl.when((b == 0) | (be_ref[b] != be_ref[jnp.maximum(b - 1, 0)]))
    def _():
        wg_bf[...] = wg_ref[0, 0].astype(BF16)
        wu_bf[...] = wu_ref[0, 0].astype(BF16)
        wd_bf[...] = wd_ref[0, 0].astype(BF16)

    @pl.when(b < nv_ref[0])
    def _():
        x = _unpack_rows(jnp.concatenate([xa_ref[...], xb_ref[...]], axis=1)).astype(BF16)
        hid = (_silu(jnp.dot(x, wg_bf[...], preferred_element_type=F32))
               * jnp.dot(x, wu_bf[...], preferred_element_type=F32))
        packed = _pack_rows(jnp.dot(hid.astype(BF16), wd_bf[...], preferred_element_type=F32))
        half = packed.shape[1] // 2
        ya_ref[...] = packed[:, :half]
        yb_ref[...] = packed[:, half:]

    @pl.when(b >= nv_ref[0])
    def _():
        ya_ref[...] = jnp.zeros_like(ya_ref)
        yb_ref[...] = jnp.zeros_like(yb_ref)


def experts(rows_a, rows_b, blk_e, n_valid, layer, w_gate, w_up, w_down):
    cap, dq = rows_a.shape
    _, _, d, ff = w_gate.shape
    n_blocks = cap // EXPERT_ROWS
    live = lambda b, be, nv: (jnp.minimum(b, nv[0] - 1), 0)
    out_blk = pl.BlockSpec((EXPERT_ROWS, dq), lambda b, be, nv: (b, 0))
    out = jax.ShapeDtypeStruct((cap, dq), U32)
    return pl.pallas_call(
        _experts_kernel,
        grid_spec=pltpu.PrefetchScalarGridSpec(
            num_scalar_prefetch=2,
            grid=(n_blocks,),
            in_specs=[pl.BlockSpec((EXPERT_ROWS, dq), live), pl.BlockSpec((EXPERT_ROWS, dq), live),
                      pl.BlockSpec((1, 1, d, ff), lambda b, be, nv: (layer, be[b], 0, 0)),
                      pl.BlockSpec((1, 1, d, ff), lambda b, be, nv: (layer, be[b], 0, 0)),
                      pl.BlockSpec((1, 1, ff, d), lambda b, be, nv: (layer, be[b], 0, 0))],
            out_specs=[out_blk, out_blk],
            scratch_shapes=[pltpu.VMEM((d, ff), BF16), pltpu.VMEM((d, ff), BF16), pltpu.VMEM((ff, d), BF16)],
        ),
        out_shape=[out, out],
        compiler_params=_params("arbitrary"),
    )(blk_e, n_valid, rows_a, rows_b, w_gate, w_up, w_down)


def _combine_kernel(gate_ref, h_ref, shared_ref, gf_ref, sh_ref, sc_ref, *rest, n_prev):
    ya_refs, yb_refs, outs = rest[:TOP_K], rest[TOP_K:2 * TOP_K], rest[2 * TOP_K + n_prev:]
    routed = None
    for k in range(TOP_K):
        y = _unpack_rows(jnp.concatenate([ya_refs[k][0], yb_refs[k][0]], axis=1))
        routed = gate_ref[:, k:k + 1] * y if routed is None else routed + gate_ref[:, k:k + 1] * y
    h = h_ref[...] + gf_ref[...] * (routed + shared_ref[...].astype(F32))
    outs[0][...] = h
    if len(outs) > 1:
        outs[1][...] = _rms(h) * (1.0 + sc_ref[...]) + sh_ref[...]


def combine(y_a, y_b, dest, gate_tk, h, shared, gate_ffn, next_shift, next_scale, with_next):
    t, d = h.shape
    n_out = 2 if with_next else 1
    n_choice = dest.shape[0]
    tm = min(256, t)
    parts = COMBINE_PARTS if t % (COMBINE_PARTS * tm) == 0 else 1
    tp = t // parts
    fixed = lambda i: (0, 0)
    choice = [pl.BlockSpec((1, tm, d // 4), functools.partial(lambda i, k: (k, i, 0), k=k))
              for k in range(n_choice)] * 2
    outs = []
    for part in range(parts):
        flat = dest[:, part * tp:(part + 1) * tp].reshape(1, n_choice * tp)
        picked_a = gather_rows(y_a, flat).reshape(n_choice, tp, d // 4)
        picked_b = gather_rows(y_b, flat).reshape(n_choice, tp, d // 4)
        row = functools.partial(lambda i, off: (i + off, 0), off=part * (tp // tm))
        outs = pl.pallas_call(
            functools.partial(_combine_kernel, n_prev=len(outs)),
            grid=(tp // tm,),
            in_specs=[pl.BlockSpec((tm, n_choice), row), pl.BlockSpec((tm, d), row), pl.BlockSpec((tm, d), row),
                      pl.BlockSpec((1, d), fixed), pl.BlockSpec((1, d), fixed), pl.BlockSpec((1, d), fixed)]
            + choice + [pl.BlockSpec(memory_space=pl.ANY)] * len(outs),
            out_specs=[pl.BlockSpec((tm, d), row)] * n_out,
            out_shape=[jax.ShapeDtypeStruct((t, d), F32)] * n_out,
            input_output_aliases={6 + 2 * n_choice + j: j for j in range(len(outs))},
            compiler_params=_params("parallel"),
        )(gate_tk, h, shared, gate_ffn, next_shift, next_scale, *([picked_a] * n_choice), *([picked_b] * n_choice),
          *outs)
    return outs


def moe_tail(h, f_a, f_b, logits_t, shared_weights, router_bias, layer, w_gate, w_up, w_down, gate_ffn,
             next_shift, next_scale, with_next):
    t, d = h.shape
    n_exp = w_gate.shape[1]
    n_blocks = -(-(t * TOP_K + n_exp * (EXPERT_ROWS - 1)) // EXPERT_ROWS)
    cap = n_blocks * EXPERT_ROWS
    idx, gate, rank, counts = route(logits_t, router_bias)
    dest, blk_e, n_valid = dispatch_plan(idx, rank, counts, n_blocks)
    rows_a, rows_b = dispatch(f_a, dest, cap), dispatch(f_b, dest, cap)
    shared = shared_expert(f_a, f_b, *shared_weights)
    y_a, y_b = experts(rows_a, rows_b, blk_e, n_valid, layer, w_gate, w_up, w_down)
    return combine(y_a, y_b, dest, gate.T, h, shared, gate_ffn, next_shift, next_scale, with_next)


def kernel(x, c, ctx, c_ctx, w_ada, b_ada, w_in, s5_lam_re, s5_lam_im, s5_log_dt, s5_b_re, s5_b_im,
           s5_c_re, s5_c_im, s5_d, s5_w_glu, s5_b_glu, na_q_gain, na_k_gain, na_rpb, w_mix_out,
           w_fourier_out, b_fourier_out, w_router, router_bias, w_exp_gate, w_exp_up, w_exp_down,
           w_sh_gate, w_sh_up, w_sh_down):
    bsz, t, d = x.shape
    assert bsz == 1 and w_ada.shape[0] == 2
    n_exp = w_router.shape[2]
    s5w = s5_w_glu.shape[1]
    naw = w_in.shape[2] - s5w
    naw //= 3
    heads = naw // NA_HEAD_DIM

    cond8 = jnp.zeros((8, d), F32).at[0].set(c[0].astype(F32)).at[1].set(c_ctx.astype(F32))
    ada = adaln_all(cond8, w_ada, b_ada)
    mod = lambda layer, who, j: ada[layer, who:who + 1, j * d:(j + 1) * d]

    def ffn_weights(i):
        return mod(i, 0, 3), mod(i, 0, 4), jnp.transpose(w_router[i]).astype(F32)

    def shared_weights(i):
        return w_sh_gate[i].astype(BF16), w_sh_up[i].astype(BF16), w_sh_down[i].astype(BF16)

    h0 = x[0]
    seg = jnp.asarray(np.kron(np.eye(heads), np.ones((NA_HEAD_DIM, NA_HEAD_DIM))), BF16)
    w_in_b = w_in[0].astype(BF16)
    qg = jnp.tile(na_q_gain[0].astype(F32), heads)[None]
    kg = jnp.tile(na_k_gain[0].astype(F32), heads)[None]
    u_c, _, k_c, v_c = in_projection(ctx[0], mod(0, 1, 0), mod(0, 1, 1), w_in_b, seg, qg, kg, s5w, naw)
    u_l, q_l, k_l, v_l = in_projection(h0, mod(0, 0, 0), mod(0, 0, 1), w_in_b, seg, qg, kg, s5w, naw)
    mats = s5_matrices(s5_lam_re[0], s5_lam_im[0], s5_log_dt[0], s5_b_re[0], s5_b_im[0],
                       s5_c_re[0], s5_c_im[0], s5_d[0])
    y_s5 = s5_mixer(u_c, u_l, mats)
    na = neighbourhood_attention(q_l, k_l, v_l, k_c, v_c, na_bias_table(na_rpb[0]))
    h1, f1a, f1b, lg1 = _post_call(
        _even_post_kernel, [y_s5, na, h0],
        [s5_w_glu[0].astype(BF16), s5_b_glu[0].astype(F32)[None], w_mix_out[0].astype(BF16), mod(0, 0, 2),
         *ffn_weights(0)], t, d, n_exp)
    h2, a1 = moe_tail(h1, f1a, f1b, lg1, shared_weights(0), router_bias[0], 0, w_exp_gate, w_exp_up, w_exp_down,
                      mod(0, 0, 5), mod(1, 0, 0), mod(1, 0, 1), True)

    zre, zim = time_dft(a1)
    cc, sc = channel_dft_tables(d // FOURIER_GROUPS)
    h3, f3a, f3b, lg3 = _post_call(
        _odd_post_kernel, [zre, zim, h2],
        [cc, sc, w_fourier_out[0].astype(BF16), b_fourier_out[0].astype(F32)[None], mod(1, 0, 2),
         *ffn_weights(1)], t, d, n_exp)
    zero_row = jnp.zeros((1, d), F32)
    (out,) = moe_tail(h3, f3a, f3b, lg3, shared_weights(1), router_bias[1], 1, w_exp_gate, w_exp_up, w_exp_down,
                      mod(1, 0, 5), zero_row, zero_row, False)
    return out[None]
```

```python
import functools
import math

import numpy as np
import jax
import jax.numpy as jnp
from jax import lax
from jax.experimental import pallas as pl
from jax.experimental.pallas import tpu as pltpu
from jax.experimental.pallas import tpu_sc as plsc

F32 = jnp.float32
BF16 = jnp.bfloat16
I32 = jnp.int32
U32 = jnp.uint32
HIGHEST = lax.Precision.HIGHEST

MXU_DEPTH = 256
LANES = 128
GRID_W = 64
NORM_EPS = 1e-6
S5_GROUP = 16
S5_STATE = 64
S5_LAMBDA_RE_MAX = -1e-4
S5_CHUNK = 16
S5_TILE = 128
NA_HEADS = 8
NA_HEAD_DIM = 64
NA_KH = 8
NA_KW = 16
FOURIER_GROUPS = 4
N_EXPERT_GROUPS = 8
TOPK_GROUPS = 4
TOP_K = 8
ROUTED_SCALE = 2.5
EXPERT_ROWS = 512
SC_WINDOW = 128
NEG_BIG = -1e30

VMEM_LIMIT_BYTES = 56 * 1024 * 1024


def _params(*sem):
    return pltpu.CompilerParams(dimension_semantics=sem or None,
                                vmem_limit_bytes=VMEM_LIMIT_BYTES)


def _rms(x):
    return x * lax.rsqrt(jnp.mean(x * x, axis=-1, keepdims=True) + NORM_EPS)


def _silu(x):
    return x * jax.nn.sigmoid(x)


def _pack_pair(lo, hi):
    lo = lax.bitcast_convert_type(lo.astype(BF16).astype(F32), U32)
    hi = lax.bitcast_convert_type(hi.astype(BF16).astype(F32), U32)
    return (hi & jnp.uint32(0xFFFF0000)) | (lo >> 16)


def _unpack_pair(w):
    return (lax.bitcast_convert_type(w << 16, F32), lax.bitcast_convert_type(w & jnp.uint32(0xFFFF0000), F32))


def _pack_rows(x):
    n = x.shape[1] // 2
    return _pack_pair(x[:, :n], x[:, n:])


def _unpack_rows(w):
    return jnp.concatenate(_unpack_pair(w), axis=1)


def _ada_kernel(c_ref, w_ref, b_ref, o_ref):
    o_ref[0] = jnp.dot(_silu(c_ref[...]), w_ref[0], preferred_element_type=F32,
                       precision=HIGHEST) + b_ref[0]


def adaln_all(cond8, w_ada, b_ada):
    n_layers, d, n6 = w_ada.shape
    tn = n6 // 4
    return pl.pallas_call(
        _ada_kernel,
        grid=(n_layers, n6 // tn),
        in_specs=[pl.BlockSpec((8, d), lambda l, j: (0, 0)),
                  pl.BlockSpec((1, d, tn), lambda l, j: (l, 0, j)),
                  pl.BlockSpec((1, 1, tn), lambda l, j: (l, 0, j))],
        out_specs=pl.BlockSpec((1, 8, tn), lambda l, j: (l, 0, j)),
        out_shape=jax.ShapeDtypeStruct((n_layers, 8, n6), F32),
        compiler_params=_params("parallel", "parallel"),
    )(cond8, w_ada, b_ada.reshape(n_layers, 1, n6))


def _inproj_kernel(x_ref, sh_ref, sc_ref, w_ref, seg_ref, qg_ref, kg_ref,
                   u_ref, q_ref, k_ref, v_ref):
    a = _rms(x_ref[...]) * (1.0 + sc_ref[...]) + sh_ref[...]
    z = jnp.dot(a.astype(BF16), w_ref[...], preferred_element_type=F32)
    s5w = u_ref.shape[1]
    naw = q_ref.shape[1]

    def head_norm(t, gain):
        ss = jnp.dot((t * t).astype(BF16), seg_ref[...], preferred_element_type=F32)
        return t * lax.rsqrt(ss * (1.0 / NA_HEAD_DIM) + NORM_EPS) * gain

    u_ref[...] = z[:, :s5w]
    q_ref[...] = head_norm(z[:, s5w:s5w + naw], qg_ref[...]).astype(BF16)
    k_ref[...] = head_norm(z[:, s5w + naw:s5w + 2 * naw], kg_ref[...]).astype(BF16)
    v_ref[...] = z[:, s5w + 2 * naw:].astype(BF16)


def in_projection(x, shift, scale, w_in_bf16, seg_ones, q_gain_row, k_gain_row, s5w, naw):
    t, d = x.shape
    tm = min(512, t)
    row = lambda i: (i, 0)
    fixed = lambda i: (0, 0)
    return pl.pallas_call(
        _inproj_kernel,
        grid=(t // tm,),
        in_specs=[pl.BlockSpec((tm, d), row),
                  pl.BlockSpec((1, d), fixed), pl.BlockSpec((1, d), fixed),
                  pl.BlockSpec(w_in_bf16.shape, fixed),
                  pl.BlockSpec(seg_ones.shape, fixed),
                  pl.BlockSpec((1, naw), fixed), pl.BlockSpec((1, naw), fixed)],
        out_specs=[pl.BlockSpec((tm, s5w), row), pl.BlockSpec((tm, naw), row),
                   pl.BlockSpec((tm, naw), row), pl.BlockSpec((tm, naw), row)],
        out_shape=[jax.ShapeDtypeStruct((t, s5w), F32)] + [jax.ShapeDtypeStruct((t, naw), BF16)] * 3,
        compiler_params=_params("parallel"),
    )(x, shift, scale, w_in_bf16, seg_ones, q_gain_row, k_gain_row)


def s5_matrices(lam_re, lam_im, log_dt, b_re, b_im, c_re, c_im, d_skip):
    L = S5_CHUNK
    taus = jnp.arange(L + 1, dtype=F32)

    def direction(i):
        lam = lax.complex(jnp.minimum(lam_re[i].astype(F32), S5_LAMBDA_RE_MAX), lam_im[i].astype(F32))
        ldt = lam * jnp.exp(log_dt[i].astype(F32))[:, None]
        lam_bar = jnp.exp(ldt)
        b_bar = ((lam_bar - 1.0) / lam)[..., None] * lax.complex(b_re[i].astype(F32), b_im[i].astype(F32))
        cc = lax.complex(c_re[i].astype(F32), c_im[i].astype(F32))
        powers = jnp.exp(ldt[None] * taus[:, None, None])
        resp = jnp.real(jnp.einsum('gcp,tgp,gpd->gtcd', cc, powers[:L], b_bar, precision=HIGHEST))
        return powers, b_bar, cc, resp

    pw_f, bb_f, cc_f, k_f = direction(0)
    pw_b, bb_b, cc_b, k_b = direction(1)
    g, p = pw_f.shape[1:]
    c = S5_GROUP
    diff = np.arange(L)[None, :] - np.arange(L)[:, None]
    lag = np.arange(L)[:, None, None]
    tf = jnp.einsum('tsl,gtcd->gslcd', jnp.asarray(lag == diff[None], F32), k_f, precision=HIGHEST)
    tb = jnp.einsum('tsl,gtcd->gslcd', jnp.asarray(lag == -diff[None], F32), k_b, precision=HIGHEST)
    skip = (jnp.eye(L, dtype=F32)[None, :, :, None, None]
            * (jnp.eye(c, dtype=F32)[None] * d_skip.astype(F32)[:, :, None])[:, None, None])
    toep = jnp.transpose(tf + tb + skip, (0, 1, 4, 2, 3)).reshape(g, L * c, L * c)

    def state_in(powers_sel, b_bar):
        w = powers_sel[:, :, :, None] * b_bar[None]
        return jnp.transpose(w, (1, 0, 3, 2)).reshape(g, L * c, p)

    wf = state_in(pw_f[L - 1 - jnp.arange(L)], bb_f)
    wb = state_in(pw_b[jnp.arange(L)], bb_b)
    w_state = jnp.concatenate([jnp.real(wf), jnp.imag(wf), jnp.imag(wf), jnp.real(wf),
                               jnp.real(wb), jnp.imag(wb), jnp.imag(wb), jnp.real(wb)], axis=-1)

    def state_out(powers_sel, cc):
        r = powers_sel[:, :, None, :] * cc[None]
        return jnp.transpose(r, (1, 3, 0, 2)).reshape(g, p, L * c)

    rf = state_out(pw_f[1 + jnp.arange(L)], cc_f)
    rb = state_out(pw_b[L - jnp.arange(L)], cc_b)
    r_state = jnp.concatenate([jnp.real(rf), -jnp.imag(rf), jnp.real(rb), -jnp.imag(rb)], axis=1)

    def mult(a):
        ar, ai = jnp.real(a), jnp.imag(a)
        return jnp.stack([jnp.concatenate([ar, ar], -1), jnp.concatenate([-ai, ai], -1),
                          jnp.concatenate([ai, -ai], -1)])

    return toep, w_state, r_state, mult(pw_f[L]), mult(pw_b[L])


def _s5_pack_kernel(*refs):
    u_refs, (wt_ref, ut_ref, f1_ref, f2_ref, b1_ref, b2_ref) = refs[:-6], refs[-6:]
    L = S5_CHUNK
    g, lc, nck = ut_ref.shape
    c = lc // L
    gs = g // len(u_refs)
    for s in range(L):
        for j, u_ref in enumerate(u_refs):
            step_s = u_ref[pl.ds(s, nck, stride=L), :]
            ut_ref[j * gs:(j + 1) * gs, s * c:(s + 1) * c, :] = (
                jnp.transpose(step_s).astype(BF16).reshape(gs, c, nck))
    n = f1_ref.shape[2]
    for gi in range(g):
        inc = jnp.dot(wt_ref[gi], ut_ref[gi], preferred_element_type=F32)
        for j, ref in enumerate((f1_ref, f2_ref, b1_ref, b2_ref)):
            ref[:, gi, :] = jnp.transpose(inc[j * n:(j + 1) * n, :])


def s5_pack(u, wt_state):
    t, w = u.shape
    g, n4, lc = wt_state.shape
    nc = t // S5_CHUNK
    tile = min(S5_TILE, nc)
    inc = jax.ShapeDtypeStruct((nc, g, n4 // 4), F32)
    inc_blk = pl.BlockSpec((tile, g, n4 // 4), lambda i: (i, 0, 0))
    return pl.pallas_call(
        _s5_pack_kernel,
        grid=(nc // tile,),
        in_specs=[pl.BlockSpec((tile * S5_CHUNK, LANES), functools.partial(lambda i, j: (i, j), j=j))
                  for j in range(w // LANES)] + [pl.BlockSpec(wt_state.shape, lambda i: (0, 0, 0))],
        out_specs=[pl.BlockSpec((g, lc, tile), lambda i: (0, 0, i))] + [inc_blk] * 4,
        out_shape=[jax.ShapeDtypeStruct((g, lc, nc), BF16)] + [inc] * 4,
        compiler_params=_params("parallel"),
    )(*([u] * (w // LANES)), wt_state)


def _s5_scan_kernel(s1_ref, s2_ref, m_ref, init_ref, x_ref, last_ref, v1_ref, v2_ref, *, reverse):
    @pl.when(pl.program_id(0) == 0)
    def _():
        v1_ref[...] = init_ref[0]
        v2_ref[...] = init_ref[1]

    a1, a2, a3 = m_ref[0], m_ref[1], m_ref[2]
    cb = s1_ref.shape[0]

    def body(j, carry):
        v1, v2 = carry
        jj = cb - 1 - j if reverse else j
        x_ref[jj] = v1
        return (a1 * v1 + a2 * v2 + s1_ref[jj], a1 * v2 + a3 * v1 + s2_ref[jj])

    v1, v2 = lax.fori_loop(0, cb, body, (v1_ref[...], v2_ref[...]))
    v1_ref[...] = v1
    v2_ref[...] = v2
    last_ref[...] = v1


def s5_chunk_scan(s1, s2, mult, init, reverse):
    nc, g, n = s1.shape
    cb = min(S5_TILE, nc)
    nb = nc // cb
    blk = (lambda i: (nb - 1 - i, 0, 0)) if reverse else (lambda i: (i, 0, 0))
    return pl.pallas_call(
        functools.partial(_s5_scan_kernel, reverse=reverse),
        grid=(nb,),
        in_specs=[pl.BlockSpec((cb, g, n), blk), pl.BlockSpec((cb, g, n), blk),
                  pl.BlockSpec((3, g, n), lambda i: (0, 0, 0)), pl.BlockSpec((2, g, n), lambda i: (0, 0, 0))],
        out_specs=[pl.BlockSpec((cb, g, n), blk), pl.BlockSpec((g, n), lambda i: (0, 0))],
        out_shape=[jax.ShapeDtypeStruct((nc, g, n), F32), jax.ShapeDtypeStruct((g, n), F32)],
        scratch_shapes=[pltpu.VMEM((g, n), F32), pltpu.VMEM((g, n), F32)],
        compiler_params=_params("arbitrary"),
    )(s1, s2, mult, init)


def _s5_readout_kernel(ut_ref, tt_ref, rt_ref, xf_ref, xb_ref, y_ref, yt_ref, *slab_refs):
    L = S5_CHUNK
    g, lc, nck = ut_ref.shape
    c = lc // L
    gs = g // len(slab_refs)
    for gi in range(g):
        xin_t = jnp.concatenate([jnp.transpose(xf_ref[:, gi, :]), jnp.transpose(xb_ref[:, gi, :])], axis=0)
        yt_ref[gi] = (jnp.dot(tt_ref[gi], ut_ref[gi], preferred_element_type=F32)
                      + jnp.dot(rt_ref[gi], xin_t.astype(BF16), preferred_element_type=F32))
    for j, slab in enumerate(slab_refs):
        for l in range(L):
            step_l = yt_ref[j * gs:(j + 1) * gs, l * c:(l + 1) * c, :].reshape(gs * c, nck)
            slab[pl.ds(l, nck, stride=L), :] = jnp.transpose(step_l)
        y_ref[:, j * gs * c:(j + 1) * gs * c] = slab[...]


def s5_readout(ut, toep_t, r_state_t, xin_f, xin_b):
    g, lc, nc = ut.shape
    n = xin_f.shape[2]
    tile = min(S5_TILE, nc)
    fixed = lambda i: (0, 0, 0)
    state_blk = pl.BlockSpec((tile, g, n), lambda i: (i, 0, 0))
    return pl.pallas_call(
        _s5_readout_kernel,
        grid=(nc // tile,),
        in_specs=[pl.BlockSpec((g, lc, tile), lambda i: (0, 0, i)),
                  pl.BlockSpec(toep_t.shape, fixed), pl.BlockSpec(r_state_t.shape, fixed), state_blk, state_blk],
        out_specs=pl.BlockSpec((tile * S5_CHUNK, g * lc // S5_CHUNK), lambda i: (i, 0)),
        out_shape=jax.ShapeDtypeStruct((nc * S5_CHUNK, g * lc // S5_CHUNK), F32),
        scratch_shapes=[pltpu.VMEM((g, lc, tile), F32)]
        + [pltpu.VMEM((tile * S5_CHUNK, LANES), F32)] * (g * lc // S5_CHUNK // LANES),
        compiler_params=_params("parallel"),
    )(ut, toep_t, r_state_t, xin_f, xin_b)


def s5_mixer(u_ctx, u_lat, mats):
    toep, w_state, r_state, mult_f, mult_b = mats
    L = S5_CHUNK
    g, n = mult_f.shape[1:]
    swap = lambda a: jnp.transpose(a, (0, 2, 1)).astype(BF16)
    wt_state, toep_t, r_state_t = swap(w_state), swap(toep), swap(r_state)
    halves = lambda v: jnp.stack([v, jnp.roll(v, n // 2, axis=-1)])
    n_ctx = u_ctx.shape[0] // L
    ctx_chunks = -(-(n_ctx + 1) // S5_TILE) * S5_TILE
    ctx_pad = jnp.zeros((ctx_chunks * L, u_ctx.shape[1]), F32).at[:u_ctx.shape[0]].set(u_ctx)
    _, cf1, cf2, cb1, cb2 = s5_pack(ctx_pad, wt_state)
    zero = jnp.zeros((2, g, n), F32)
    ctx_f, _ = s5_chunk_scan(cf1, cf2, mult_f, zero, False)
    _, ctx_b_last = s5_chunk_scan(cb1, cb2, mult_b, zero, True)
    ut, f1, f2, b1, b2 = s5_pack(u_lat, wt_state)
    xin_f, _ = s5_chunk_scan(f1, f2, mult_f, halves(ctx_f[n_ctx]), False)
    xin_b, _ = s5_chunk_scan(b1, b2, mult_b, halves(ctx_b_last), True)
    return s5_readout(ut, toep_t, r_state_t, xin_f, xin_b)


def na_bias_table(rpb):
    q_col = np.arange(GRID_W)
    col_start = np.clip(q_col - NA_KW // 2, 0, GRID_W - NA_KW)
    key_col = np.arange(GRID_W)
    off = key_col[None, :] - col_start[:, None]
    valid = (off >= 0) & (off < NA_KW)
    rel_col = np.clip(key_col[None, :] - q_col[:, None] + NA_KW - 1, 0, 2 * NA_KW - 2)
    pick = jnp.asarray(np.arange(2 * NA_KW - 1)[:, None, None] == rel_col[None], F32)
    full = jnp.where(valid[None, None], jnp.einsum('hrj,jck->hrck', rpb.astype(F32), pick, precision=HIGHEST),
                     NEG_BIG)
    variants = []
    for d in range(NA_KH):
        rows = [full[:, i - d + NA_KH - 1] for i in range(NA_KH)]
        variants.append(jnp.concatenate(rows, axis=-1))
    return jnp.stack(variants)


def _na_kernel(*refs, scale):
    q_ref = refs[0]
    k_refs = refs[1:1 + NA_KH]
    v_refs = refs[1 + NA_KH:1 + 2 * NA_KH]
    kc_ref, vc_ref, b_ref, o_ref = refs[1 + 2 * NA_KH:]
    hd = NA_HEAD_DIM
    width = q_ref.shape[1]
    span = min(MXU_DEPTH, width)
    nt = (((1,), (1,)), ((), ()))
    q = q_ref[...] * scale
    kk = jnp.concatenate([r[...] for r in k_refs], axis=0)
    vv = jnp.concatenate([r[...] for r in v_refs], axis=0)
    kc, vc = kc_ref[...], vc_ref[...]
    nq = q.shape[0]
    per = span // hd
    lane = lax.broadcasted_iota(I32, (nq, span), 1)
    own = [(lane >= j * hd) & (lane < (j + 1) * hd) for j in range(per)]
    out_cols = []
    for c0 in range(0, width, span):
        cols = slice(c0, c0 + span)
        qs = jnp.concatenate([jnp.where(own[j], q[:, cols], jnp.zeros_like(q[:, cols])) for j in range(per)], axis=0)
        h0 = c0 // hd
        bias = b_ref[0, h0:h0 + per].reshape(per * nq, kk.shape[0])
        s = lax.dot_general(qs, kk[:, cols], nt, preferred_element_type=F32) + bias
        sc = lax.dot_general(qs, kc[:, cols], nt, preferred_element_type=F32)
        m = jnp.maximum(jnp.max(s, axis=-1, keepdims=True), jnp.max(sc, axis=-1, keepdims=True))
        p = jnp.exp(s - m)
        pc = jnp.exp(sc - m)
        den = jnp.sum(p, axis=-1, keepdims=True) + jnp.sum(pc, axis=-1, keepdims=True)
        o = (jnp.dot(p.astype(BF16), vv[:, cols], preferred_element_type=F32)
             + jnp.dot(pc.astype(BF16), vc[:, cols], preferred_element_type=F32)) / den
        acc = jnp.zeros((nq, span), F32)
        for j in range(per):
            acc = jnp.where(own[j], o[j * nq:(j + 1) * nq], acc)
        out_cols.append(acc)
    o_ref[...] = jnp.concatenate(out_cols, axis=1).astype(o_ref.dtype)


def neighbourhood_attention(q, k, v, k_ctx, v_ctx, bias_table):
    t, w = q.shape
    rows = t // GRID_W
    kh = min(NA_KH, rows)
    assert kh == NA_KH
    first = lambda r: jnp.clip(r - kh // 2, 0, rows - kh)
    row_blk = pl.BlockSpec((GRID_W, w), lambda r: (r, 0))
    key_blks = [pl.BlockSpec((GRID_W, w), functools.partial(lambda r, i: (first(r) + i, 0), i=i))
                for i in range(kh)]
    ctx_blk = pl.BlockSpec(k_ctx.shape, lambda r: (0, 0))
    bias_blk = pl.BlockSpec((1,) + bias_table.shape[1:], lambda r: (r - first(r), 0, 0, 0))
    return pl.pallas_call(
        functools.partial(_na_kernel, scale=NA_HEAD_DIM ** -0.5),
        grid=(rows,),
        in_specs=[row_blk] + key_blks + key_blks + [ctx_blk, ctx_blk, bias_blk],
        out_specs=row_blk,
        out_shape=jax.ShapeDtypeStruct((t, w), BF16),
        compiler_params=_params("parallel"),
    )(q, *([k] * kh), *([v] * kh), k_ctx, v_ctx, bias_table)


def time_dft_tables(t):
    a_len = 1 << (int(math.log2(t)) // 2)
    b_len = t // a_len
    ka = np.arange(a_len)[:, None]
    tok = b_len * np.arange(a_len)[None, :]
    ang1 = -2.0 * np.pi * ((ka * (tok[None] + np.arange(b_len)[:, None, None])) % t) / t
    stage1 = np.concatenate([np.cos(ang1), np.sin(ang1)], axis=1) / math.sqrt(t)
    ang2 = 2.0 * np.pi * ((np.arange(b_len)[:, None] * np.arange(b_len)[None, :]) % b_len) / b_len
    c2, s2 = np.cos(ang2), np.sin(ang2)
    stage2 = np.block([[c2, s2], [-s2, c2]])
    return jnp.asarray(stage1, BF16), jnp.asarray(stage2, BF16), a_len, b_len


def _time_dft_kernel(x_ref, m_ref, w2_ref, zre_ref, zim_ref, y_ref, z_ref, *, a_len, b_len):
    i = pl.program_id(1)
    bb = m_ref.shape[0]

    def stage1(jb, carry):
        b = i * bb + jb
        xb = x_ref[pl.ds(b, a_len, stride=b_len), :]
        y = jnp.dot(m_ref[jb], xb.astype(BF16), preferred_element_type=F32)
        row = pl.multiple_of(b * a_len, a_len)
        y_ref[pl.ds(row, a_len), :] = _pack_pair(y[:a_len], y[a_len:])
        return carry

    lax.fori_loop(0, bb, stage1, 0, unroll=4)

    @pl.when(i == pl.num_programs(1) - 1)
    def _():
        def stage2(ka, carry):
            rows = pl.ds(ka, b_len, stride=a_len)
            yre, yim = _unpack_pair(y_ref[rows, :])
            y = jnp.concatenate([yre, yim], axis=0).astype(BF16)
            z = jnp.dot(w2_ref[...], y, preferred_element_type=F32)
            z_ref[rows, :] = _pack_pair(z[:b_len], z[b_len:])
            return carry

        lax.fori_loop(0, a_len, stage2, 0, unroll=4)
        zre, zim = _unpack_pair(z_ref[...])
        zre_ref[...] = zre.astype(zre_ref.dtype)
        zim_ref[...] = zim.astype(zim_ref.dtype)


def time_dft(x):
    t, d = x.shape
    stage1, stage2, a_len, b_len = time_dft_tables(t)
    lanes = 128
    bb = min(16, b_len)
    out = jax.ShapeDtypeStruct((t, d), BF16)
    return pl.pallas_call(
        functools.partial(_time_dft_kernel, a_len=a_len, b_len=b_len),
        grid=(d // lanes, b_len // bb),
        in_specs=[pl.BlockSpec((t, lanes), lambda j, i: (0, j)),
                  pl.BlockSpec((bb, 2 * a_len, a_len), lambda j, i: (i, 0, 0)),
                  pl.BlockSpec(stage2.shape, lambda j, i: (0, 0))],
        out_specs=[pl.BlockSpec((t, lanes), lambda j, i: (0, j))] * 2,
        out_shape=[out, out],
        scratch_shapes=[pltpu.VMEM((t, lanes), U32), pltpu.VMEM((t, lanes), U32)],
        compiler_params=_params("parallel", "arbitrary"),
    )(x, stage1, stage2)


def channel_dft_tables(c):
    ang = 2.0 * np.pi * ((np.arange(c)[:, None] * np.arange(c)[None, :]) % c) / c
    return (jnp.asarray(np.cos(ang) / math.sqrt(c), BF16), jnp.asarray(np.sin(ang) / math.sqrt(c), BF16))


def _shared_expert_kernel(fa_ref, fb_ref, wsg_ref, wsu_ref, wsd_ref, o_ref):
    fb = _unpack_rows(jnp.concatenate([fa_ref[...], fb_ref[...]], axis=1)).astype(BF16)
    hid = (_silu(jnp.dot(fb, wsg_ref[...], preferred_element_type=F32))
           * jnp.dot(fb, wsu_ref[...], preferred_element_type=F32))
    o_ref[...] = jnp.dot(hid.astype(BF16), wsd_ref[...], preferred_element_type=F32).astype(o_ref.dtype)


def shared_expert(f_a, f_b, w_gate, w_up, w_down):
    t, dq = f_a.shape
    d = w_down.shape[1]
    tm = min(512, t)
    row = lambda i: (i, 0)
    fixed = lambda i: (0, 0)
    return pl.pallas_call(
        _shared_expert_kernel,
        grid=(t // tm,),
        in_specs=[pl.BlockSpec((tm, dq), row), pl.BlockSpec((tm, dq), row),
                  pl.BlockSpec(w_gate.shape, fixed), pl.BlockSpec(w_up.shape, fixed), pl.BlockSpec(w_down.shape, fixed)],
        out_specs=pl.BlockSpec((tm, d), row),
        out_shape=jax.ShapeDtypeStruct((t, d), BF16),
        compiler_params=_params("parallel"),
    )(f_a, f_b, w_gate, w_up, w_down)


def _ffn_prologue(h, shf_ref, scf_ref, wr_ref, h_ref, fa_ref, fb_ref, lg_ref):
    h_ref[...] = h
    f = _rms(h) * (1.0 + scf_ref[...]) + shf_ref[...]
    packed = _pack_rows(f)
    half = packed.shape[1] // 2
    fa_ref[...] = packed[:, :half]
    fb_ref[...] = packed[:, half:]
    lg_ref[...] = lax.dot_general(wr_ref[...], f, (((1,), (1,)), ((), ())),
                                  preferred_element_type=F32, precision=HIGHEST)


def _gelu_tanh(x):
    return 0.5 * x * (1.0 + jnp.tanh(math.sqrt(2.0 / math.pi) * (x + 0.044715 * (x * x * x))))


def _even_post_kernel(y_ref, na_ref, x_ref, wglu_ref, bglu_ref, wo_ref, gm_ref, *rest):
    g = _gelu_tanh(y_ref[...])
    gate = jax.nn.sigmoid(jnp.dot(g.astype(BF16), wglu_ref[...], preferred_element_type=F32) + bglu_ref[...])
    s5 = (g * gate).astype(BF16)
    w = s5.shape[1]
    mix = (jnp.dot(s5, wo_ref[:w, :], preferred_element_type=F32)
           + jnp.dot(na_ref[...], wo_ref[w:, :], preferred_element_type=F32))
    _ffn_prologue(x_ref[...] + gm_ref[...] * mix, *rest)


def _odd_post_kernel(zre_ref, zim_ref, h_ref_in, cc_ref, sc_ref, wf_ref, bf_ref, gm_ref, *rest):
    c = cc_ref.shape[0]
    parts = []
    for grp in range(zre_ref.shape[1] // c):
        cols = slice(grp * c, (grp + 1) * c)
        parts.append(jnp.dot(zre_ref[:, cols], cc_ref[...], preferred_element_type=F32)
                     + jnp.dot(zim_ref[:, cols], sc_ref[...], preferred_element_type=F32))
    fr = jnp.concatenate(parts, axis=-1).astype(BF16)
    mix = jnp.dot(fr, wf_ref[...], preferred_element_type=F32) + bf_ref[...]
    _ffn_prologue(h_ref_in[...] + gm_ref[...] * mix, *rest)


def _post_call(body, row_inputs, fixed_inputs, t, d, n_exp):
    tm = min(512, t)
    row = lambda i: (i, 0)
    fixed = lambda i: (0, 0)
    in_specs = ([pl.BlockSpec((tm, a.shape[1]), row) for a in row_inputs]
                + [pl.BlockSpec(a.shape, fixed) for a in fixed_inputs])
    return pl.pallas_call(
        body,
        grid=(t // tm,),
        in_specs=in_specs,
        out_specs=[pl.BlockSpec((tm, d), row), pl.BlockSpec((tm, d // 4), row), pl.BlockSpec((tm, d // 4), row),
                   pl.BlockSpec((n_exp, tm), lambda i: (0, i))],
        out_shape=[jax.ShapeDtypeStruct((t, d), F32),
                   jax.ShapeDtypeStruct((t, d // 4), U32), jax.ShapeDtypeStruct((t, d // 4), U32),
                   jax.ShapeDtypeStruct((n_exp, t), F32)],
        compiler_params=_params("parallel"),
    )(*row_inputs, *fixed_inputs)


def _route_kernel(lg_ref, bias_ref, tri_ref, idx_ref, gate_ref, rank_ref, cnt_ref, run_ref):
    @pl.when(pl.program_id(0) == 0)
    def _():
        run_ref[...] = jnp.zeros_like(run_ref)

    scores = jax.nn.sigmoid(lg_ref[...])
    n_exp, tb = scores.shape
    sel = scores + bias_ref[...]
    gsz = n_exp // N_EXPERT_GROUPS
    member = lax.broadcasted_iota(I32, (gsz, tb), 0)
    gscore = []
    for grp in range(N_EXPERT_GROUPS):
        xg = sel[grp * gsz:(grp + 1) * gsz, :]
        m1 = jnp.max(xg, axis=0, keepdims=True)
        first = jnp.min(jnp.where(xg == m1, member, gsz), axis=0, keepdims=True)
        m2 = jnp.max(jnp.where(member == first, -jnp.inf, xg), axis=0, keepdims=True)
        gscore.append(m1 + m2)
    keep_rows = []
    for grp in range(N_EXPERT_GROUPS):
        beaten = jnp.zeros((1, tb), F32)
        for other in range(N_EXPERT_GROUPS):
            if other == grp:
                continue
            wins = (gscore[other] >= gscore[grp]) if other < grp else (gscore[other] > gscore[grp])
            beaten = beaten + jnp.where(wins, 1.0, 0.0)
        keep_rows.append(jnp.broadcast_to(beaten < TOPK_GROUPS, (gsz, tb)))
    masked = jnp.where(jnp.concatenate(keep_rows, axis=0), sel, -jnp.inf)

    expert = lax.broadcasted_iota(I32, (n_exp, tb), 0)
    picks, gates, hots = [], [], []
    chosen = jnp.zeros((n_exp, tb), F32)
    for _ in range(TOP_K):
        m = jnp.max(masked, axis=0, keepdims=True)
        pick = jnp.min(jnp.where(masked == m, expert, n_exp), axis=0, keepdims=True)
        hot = expert == pick
        picks.append(pick)
        hots.append(hot)
        gates.append(jnp.sum(jnp.where(hot, scores, 0.0), axis=0, keepdims=True))
        chosen = jnp.where(hot, 1.0, chosen)
        masked = jnp.where(hot, -jnp.inf, masked)
    total = gates[0]
    for gk in gates[1:]:
        total = total + gk
    ahead = jnp.dot(chosen.astype(BF16), tri_ref[...], preferred_element_type=F32) + run_ref[...]
    for k in range(TOP_K):
        idx_ref[k:k + 1, :] = picks[k]
        gate_ref[k:k + 1, :] = ROUTED_SCALE * gates[k] / total
        rank_ref[k:k + 1, :] = jnp.sum(jnp.where(hots[k], ahead, 0.0), axis=0, keepdims=True).astype(I32)
    run_ref[...] = run_ref[...] + jnp.sum(chosen, axis=1, keepdims=True)
    cnt_ref[...] = jnp.broadcast_to(run_ref[...], cnt_ref.shape)


def route(logits_t, router_bias):
    n_exp, t = logits_t.shape
    tb = min(512, t)
    tri = jnp.asarray(np.triu(np.ones((tb, tb), np.float32), k=1), BF16)
    tok = lambda i: (0, i)
    idx, gate, rank, cnt = pl.pallas_call(
        _route_kernel,
        grid=(t // tb,),
        in_specs=[pl.BlockSpec((n_exp, tb), tok), pl.BlockSpec((n_exp, 1), lambda i: (0, 0)),
                  pl.BlockSpec((tb, tb), lambda i: (0, 0))],
        out_specs=[pl.BlockSpec((TOP_K, tb), tok)] * 3 + [pl.BlockSpec((n_exp, 128), lambda i: (0, 0))],
        out_shape=[jax.ShapeDtypeStruct((TOP_K, t), I32), jax.ShapeDtypeStruct((TOP_K, t), F32),
                   jax.ShapeDtypeStruct((TOP_K, t), I32), jax.ShapeDtypeStruct((n_exp, 128), F32)],
        scratch_shapes=[pltpu.VMEM((n_exp, 1), F32)],
        compiler_params=_params("arbitrary"),
    )(logits_t, router_bias.astype(F32).reshape(n_exp, 1), tri)
    return idx, gate, rank, cnt[:, 0].astype(I32)


def dispatch_plan(idx, rank, counts, n_blocks):
    n_exp = counts.shape[0]
    padded = (counts + EXPERT_ROWS - 1) // EXPERT_ROWS * EXPERT_ROWS
    pad_end = jnp.cumsum(padded)
    pad_start = pad_end - padded
    experts_iota = jnp.arange(n_exp, dtype=I32)
    dest = jnp.sum(jnp.where(idx[..., None] == experts_iota, pad_start, 0), axis=-1) + rank
    n_valid = (pad_end[-1] // EXPERT_ROWS).astype(I32)
    blk = jnp.minimum(jnp.arange(n_blocks, dtype=I32), n_valid - 1)
    blk_e = jnp.sum((pad_end[None, :] <= blk[:, None] * EXPERT_ROWS).astype(I32), axis=1)
    return dest.astype(I32), jnp.minimum(blk_e, n_exp - 1).astype(I32), n_valid.reshape(1)


def _sc_mesh():
    return plsc.VectorSubcoreMesh(core_axis_name="core", subcore_axis_name="subcore")


def dispatch(f, dest, cap):
    t, w = f.shape
    n_choice = dest.shape[0]

    @functools.partial(pl.kernel, out_type=jax.ShapeDtypeStruct((cap, w), f.dtype), mesh=_sc_mesh(),
                       scratch_types=[])
    def scatter_rows(x_hbm, i_hbm, o_hbm):
        def body(x_vmem, i_vmem):
            pltpu.sync_copy(x_vmem, o_hbm.at[i_vmem.at[0]])

        pltpu.emit_pipeline(
            body,
            grid=(t // SC_WINDOW, n_choice),
            in_specs=[pl.BlockSpec((SC_WINDOW, w), lambda i, k: (i, 0)),
                      pl.BlockSpec((1, SC_WINDOW), lambda i, k: (k, i))],
            out_specs=[],
            core_axis_name=("core", "subcore"),
            dimension_semantics=(pltpu.PARALLEL, pltpu.ARBITRARY),
        )(x_hbm, i_hbm)

    return scatter_rows(f, dest)


def gather_rows(rows, index_row):
    n = index_row.shape[1]
    w = rows.shape[1]

    @functools.partial(pl.kernel, out_type=jax.ShapeDtypeStruct((n, w), rows.dtype), mesh=_sc_mesh(),
                       scratch_types=[])
    def gather(y_hbm, i_hbm, o_hbm):
        def body(i_vmem, o_vmem):
            pltpu.sync_copy(y_hbm.at[i_vmem.at[0]], o_vmem)

        pltpu.emit_pipeline(
            body,
            grid=(n // SC_WINDOW,),
            in_specs=[pl.BlockSpec((1, SC_WINDOW), lambda i: (0, i))],
            out_specs=[pl.BlockSpec((SC_WINDOW, w), lambda i: (i, 0))],
            core_axis_name=("core", "subcore"),
            dimension_semantics=(pltpu.PARALLEL,),
        )(i_hbm, o_hbm)

    return gather(rows, index_row)


def _experts_kernel(be_ref, nv_ref, xa_ref, xb_ref, wg_ref, wu_ref, wd_ref, ya_ref, yb_ref, wg_bf, wu_bf, wd_bf):
    b = pl.program_id(0)

    @pl.when((b == 0) | (be_ref[b] != be_ref[jnp.maximum(b - 1, 0)]))
    def _():
        wg_bf[...] = wg_ref[0, 0].astype(BF16)
        wu_bf[...] = wu_ref[0, 0].astype(BF16)
        wd_bf[...] = wd_ref[0, 0].astype(BF16)

    @pl.when(b < nv_ref[0])
    def _():
        x = _unpack_rows(jnp.concatenate([xa_ref[...], xb_ref[...]], axis=1)).astype(BF16)
        hid = (_silu(jnp.dot(x, wg_bf[...], preferred_element_type=F32))
               * jnp.dot(x, wu_bf[...], preferred_element_type=F32))
        packed = _pack_rows(jnp.dot(hid.astype(BF16), wd_bf[...], preferred_element_type=F32))
        half = packed.shape[1] // 2
        ya_ref[...] = packed[:, :half]
        yb_ref[...] = packed[:, half:]

    @pl.when(b >= nv_ref[0])
    def _():
        ya_ref[...] = jnp.zeros_like(ya_ref)
        yb_ref[...] = jnp.zeros_like(yb_ref)


def experts(rows_a, rows_b, blk_e, n_valid, layer, w_gate, w_up, w_down):
    cap, dq = rows_a.shape
    _, _, d, ff = w_gate.shape
    n_blocks = cap // EXPERT_ROWS
    live = lambda b, be, nv: (jnp.minimum(b, nv[0] - 1), 0)
    out_blk = pl.BlockSpec((EXPERT_ROWS, dq), lambda b, be, nv: (b, 0))
    out = jax.ShapeDtypeStruct((cap, dq), U32)
    return pl.pallas_call(
        _experts_kernel,
        grid_spec=pltpu.PrefetchScalarGridSpec(
            num_scalar_prefetch=2,
            grid=(n_blocks,),
            in_specs=[pl.BlockSpec((EXPERT_ROWS, dq), live), pl.BlockSpec((EXPERT_ROWS, dq), live),
                      pl.BlockSpec((1, 1, d, ff), lambda b, be, nv: (layer, be[b], 0, 0)),
                      pl.BlockSpec((1, 1, d, ff), lambda b, be, nv: (layer, be[b], 0, 0)),
                      pl.BlockSpec((1, 1, ff, d), lambda b, be, nv: (layer, be[b], 0, 0))],
            out_specs=[out_blk, out_blk],
            scratch_shapes=[pltpu.VMEM((d, ff), BF16), pltpu.VMEM((d, ff), BF16), pltpu.VMEM((ff, d), BF16)],
        ),
        out_shape=[out, out],
        compiler_params=_params("arbitrary"),
    )(blk_e, n_valid, rows_a, rows_b, w_gate, w_up, w_down)


def _combine_kernel(gate_ref, h_ref, shared_ref, gf_ref, sh_ref, sc_ref, *rest):
    ya_refs, yb_refs, outs = rest[:TOP_K], rest[TOP_K:2 * TOP_K], rest[2 * TOP_K:]
    routed = None
    for k in range(TOP_K):
        y = _unpack_rows(jnp.concatenate([ya_refs[k][0], yb_refs[k][0]], axis=1))
        routed = gate_ref[:, k:k + 1] * y if routed is None else routed + gate_ref[:, k:k + 1] * y
    h = h_ref[...] + gf_ref[...] * (routed + shared_ref[...].astype(F32))
    outs[0][...] = h
    if len(outs) > 1:
        outs[1][...] = _rms(h) * (1.0 + sc_ref[...]) + sh_ref[...]


def combine(y_a, y_b, dest, gate_tk, h, shared, gate_ffn, next_shift, next_scale, with_next):
    t, d = h.shape
    n_out = 2 if with_next else 1
    n_choice = dest.shape[0]
    flat = dest.reshape(1, n_choice * t)
    picked_a = gather_rows(y_a, flat).reshape(n_choice, t, d // 4)
    picked_b = gather_rows(y_b, flat).reshape(n_choice, t, d // 4)
    tm = min(256, t)
    row = lambda i: (i, 0)
    fixed = lambda i: (0, 0)
    choice = [pl.BlockSpec((1, tm, d // 4), functools.partial(lambda i, k: (k, i, 0), k=k))
              for k in range(n_choice)] * 2
    return pl.pallas_call(
        _combine_kernel,
        grid=(t // tm,),
        in_specs=[pl.BlockSpec((tm, n_choice), row), pl.BlockSpec((tm, d), row), pl.BlockSpec((tm, d), row),
                  pl.BlockSpec((1, d), fixed), pl.BlockSpec((1, d), fixed), pl.BlockSpec((1, d), fixed)] + choice,
        out_specs=[pl.BlockSpec((tm, d), row)] * n_out,
        out_shape=[jax.ShapeDtypeStruct((t, d), F32)] * n_out,
        compiler_params=_params("parallel"),
    )(gate_tk, h, shared, gate_ffn, next_shift, next_scale, *([picked_a] * n_choice), *([picked_b] * n_choice))


def moe_tail(h, f_a, f_b, logits_t, shared_weights, router_bias, layer, w_gate, w_up, w_down, gate_ffn,
             next_shift, next_scale, with_next):
    t, d = h.shape
    n_exp = w_gate.shape[1]
    n_blocks = -(-(t * TOP_K + n_exp * (EXPERT_ROWS - 1)) // EXPERT_ROWS)
    cap = n_blocks * EXPERT_ROWS
    idx, gate, rank, counts = route(logits_t, router_bias)
    dest, blk_e, n_valid = dispatch_plan(idx, rank, counts, n_blocks)
    rows_a, rows_b = dispatch(f_a, dest, cap), dispatch(f_b, dest, cap)
    shared = shared_expert(f_a, f_b, *shared_weights)
    y_a, y_b = experts(rows_a, rows_b, blk_e, n_valid, layer, w_gate, w_up, w_down)
    return combine(y_a, y_b, dest, gate.T, h, shared, gate_ffn, next_shift, next_scale, with_next)


def kernel(x, c, ctx, c_ctx, w_ada, b_ada, w_in, s5_lam_re, s5_lam_im, s5_log_dt, s5_b_re, s5_b_im,
           s5_c_re, s5_c_im, s5_d, s5_w_glu, s5_b_glu, na_q_gain, na_k_gain, na_rpb, w_mix_out,
           w_fourier_out, b_fourier_out, w_router, router_bias, w_exp_gate, w_exp_up, w_exp_down,
           w_sh_gate, w_sh_up, w_sh_down):
    bsz, t, d = x.shape
    assert bsz == 1 and w_ada.shape[0] == 2
    n_exp = w_router.shape[2]
    s5w = s5_w_glu.shape[1]
    naw = w_in.shape[2] - s5w
    naw //= 3
    heads = naw // NA_HEAD_DIM

    cond8 = jnp.zeros((8, d), F32).at[0].set(c[0].astype(F32)).at[1].set(c_ctx.astype(F32))
    ada = adaln_all(cond8, w_ada, b_ada)
    mod = lambda layer, who, j: ada[layer, who:who + 1, j * d:(j + 1) * d]

    def ffn_weights(i):
        return mod(i, 0, 3), mod(i, 0, 4), jnp.transpose(w_router[i]).astype(F32)

    def shared_weights(i):
        return w_sh_gate[i].astype(BF16), w_sh_up[i].astype(BF16), w_sh_down[i].astype(BF16)

    h0 = x[0]
    seg = jnp.asarray(np.kron(np.eye(heads), np.ones((NA_HEAD_DIM, NA_HEAD_DIM))), BF16)
    w_in_b = w_in[0].astype(BF16)
    qg = jnp.tile(na_q_gain[0].astype(F32), heads)[None]
    kg = jnp.tile(na_k_gain[0].astype(F32), heads)[None]
    u_c, _, k_c, v_c = in_projection(ctx[0], mod(0, 1, 0), mod(0, 1, 1), w_in_b, seg, qg, kg, s5w, naw)
    u_l, q_l, k_l, v_l = in_projection(h0, mod(0, 0, 0), mod(0, 0, 1), w_in_b, seg, qg, kg, s5w, naw)
    mats = s5_matrices(s5_lam_re[0], s5_lam_im[0], s5_log_dt[0], s5_b_re[0], s5_b_im[0],
                       s5_c_re[0], s5_c_im[0], s5_d[0])
    y_s5 = s5_mixer(u_c, u_l, mats)
    na = neighbourhood_attention(q_l, k_l, v_l, k_c, v_c, na_bias_table(na_rpb[0]))
    h1, f1a, f1b, lg1 = _post_call(
        _even_post_kernel, [y_s5, na, h0],
        [s5_w_glu[0].astype(BF16), s5_b_glu[0].astype(F32)[None], w_mix_out[0].astype(BF16), mod(0, 0, 2),
         *ffn_weights(0)], t, d, n_exp)
    h2, a1 = moe_tail(h1, f1a, f1b, lg1, shared_weights(0), router_bias[0], 0, w_exp_gate, w_exp_up, w_exp_down,
                      mod(0, 0, 5), mod(1, 0, 0), mod(1, 0, 1), True)

    zre, zim = time_dft(a1)
    cc, sc = channel_dft_tables(d // FOURIER_GROUPS)
    h3, f3a, f3b, lg3 = _post_call(
        _odd_post_kernel, [zre, zim, h2],
        [cc, sc, w_fourier_out[0].astype(BF16), b_fourier_out[0].astype(F32)[None], mod(1, 0, 2),
         *ffn_weights(1)], t, d, n_exp)
    zero_row = jnp.zeros((1, d), F32)
    (out,) = moe_tail(h3, f3a, f3b, lg3, shared_weights(1), router_bias[1], 1, w_exp_gate, w_exp_up, w_exp_down,
                      mod(1, 0, 5), zero_row, zero_row, False)
    return out[None]
```

```python
import functools
import math

import numpy as np
import jax
import jax.numpy as jnp
from jax import lax
from jax.experimental import pallas as pl
from jax.experimental.pallas import tpu as pltpu
from jax.experimental.pallas import tpu_sc as plsc

F32 = jnp.float32
BF16 = jnp.bfloat16
I32 = jnp.int32
U32 = jnp.uint32
HIGHEST = lax.Precision.HIGHEST

MXU_DEPTH = 256
LANES = 128
GRID_W = 64
NORM_EPS = 1e-6
S5_GROUP = 16
S5_STATE = 64
S5_LAMBDA_RE_MAX = -1e-4
S5_CHUNK = 16
S5_TILE = 128
NA_HEADS = 8
NA_HEAD_DIM = 64
NA_KH = 8
NA_KW = 16
FOURIER_GROUPS = 4
N_EXPERT_GROUPS = 8
TOPK_GROUPS = 4
TOP_K = 8
ROUTED_SCALE = 2.5
EXPERT_ROWS = 512
SC_WINDOW = 128
NEG_BIG = -1e30

VMEM_LIMIT_BYTES = 56 * 1024 * 1024


def _params(*sem):
    return pltpu.CompilerParams(dimension_semantics=sem or None,
                                vmem_limit_bytes=VMEM_LIMIT_BYTES)


def _rms(x):
    return x * lax.rsqrt(jnp.mean(x * x, axis=-1, keepdims=True) + NORM_EPS)


def _silu(x):
    return x * jax.nn.sigmoid(x)


def _pack_pair(lo, hi):
    lo = lax.bitcast_convert_type(lo.astype(BF16).astype(F32), U32)
    hi = lax.bitcast_convert_type(hi.astype(BF16).astype(F32), U32)
    return (hi & jnp.uint32(0xFFFF0000)) | (lo >> 16)


def _unpack_pair(w):
    return (lax.bitcast_convert_type(w << 16, F32), lax.bitcast_convert_type(w & jnp.uint32(0xFFFF0000), F32))


def _pack_rows(x):
    n = x.shape[1] // 2
    return _pack_pair(x[:, :n], x[:, n:])


def _unpack_rows(w):
    return jnp.concatenate(_unpack_pair(w), axis=1)


def _ada_kernel(c_ref, w_ref, b_ref, o_ref):
    o_ref[0] = jnp.dot(_silu(c_ref[...]), w_ref[0], preferred_element_type=F32,
                       precision=HIGHEST) + b_ref[0]


def adaln_all(cond8, w_ada, b_ada):
    n_layers, d, n6 = w_ada.shape
    tn = n6 // 4
    return pl.pallas_call(
        _ada_kernel,
        grid=(n_layers, n6 // tn),
        in_specs=[pl.BlockSpec((8, d), lambda l, j: (0, 0)),
                  pl.BlockSpec((1, d, tn), lambda l, j: (l, 0, j)),
                  pl.BlockSpec((1, 1, tn), lambda l, j: (l, 0, j))],
        out_specs=pl.BlockSpec((1, 8, tn), lambda l, j: (l, 0, j)),
        out_shape=jax.ShapeDtypeStruct((n_layers, 8, n6), F32),
        compiler_params=_params("parallel", "parallel"),
    )(cond8, w_ada, b_ada.reshape(n_layers, 1, n6))


def _inproj_kernel(x_ref, sh_ref, sc_ref, w_ref, seg_ref, qg_ref, kg_ref,
                   u_ref, q_ref, k_ref, v_ref):
    a = _rms(x_ref[...]) * (1.0 + sc_ref[...]) + sh_ref[...]
    z = jnp.dot(a.astype(BF16), w_ref[...], preferred_element_type=F32)
    s5w = u_ref.shape[1]
    naw = q_ref.shape[1]

    def head_norm(t, gain):
        ss = jnp.dot((t * t).astype(BF16), seg_ref[...], preferred_element_type=F32)
        return t * lax.rsqrt(ss * (1.0 / NA_HEAD_DIM) + NORM_EPS) * gain

    u_ref[...] = z[:, :s5w]
    q_ref[...] = head_norm(z[:, s5w:s5w + naw], qg_ref[...]).astype(BF16)
    k_ref[...] = head_norm(z[:, s5w + naw:s5w + 2 * naw], kg_ref[...]).astype(BF16)
    v_ref[...] = z[:, s5w + 2 * naw:].astype(BF16)


def in_projection(x, shift, scale, w_in_bf16, seg_ones, q_gain_row, k_gain_row, s5w, naw):
    t, d = x.shape
    tm = min(512, t)
    row = lambda i: (i, 0)
    fixed = lambda i: (0, 0)
    return pl.pallas_call(
        _inproj_kernel,
        grid=(t // tm,),
        in_specs=[pl.BlockSpec((tm, d), row),
                  pl.BlockSpec((1, d), fixed), pl.BlockSpec((1, d), fixed),
                  pl.BlockSpec(w_in_bf16.shape, fixed),
                  pl.BlockSpec(seg_ones.shape, fixed),
                  pl.BlockSpec((1, naw), fixed), pl.BlockSpec((1, naw), fixed)],
        out_specs=[pl.BlockSpec((tm, s5w), row), pl.BlockSpec((tm, naw), row),
                   pl.BlockSpec((tm, naw), row), pl.BlockSpec((tm, naw), row)],
        out_shape=[jax.ShapeDtypeStruct((t, s5w), F32)] + [jax.ShapeDtypeStruct((t, naw), BF16)] * 3,
        compiler_params=_params("parallel"),
    )(x, shift, scale, w_in_bf16, seg_ones, q_gain_row, k_gain_row)


def s5_matrices(lam_re, lam_im, log_dt, b_re, b_im, c_re, c_im, d_skip):
    L = S5_CHUNK
    taus = jnp.arange(L + 1, dtype=F32)

    def direction(i):
        lam = lax.complex(jnp.minimum(lam_re[i].astype(F32), S5_LAMBDA_RE_MAX), lam_im[i].astype(F32))
        ldt = lam * jnp.exp(log_dt[i].astype(F32))[:, None]
        lam_bar = jnp.exp(ldt)
        b_bar = ((lam_bar - 1.0) / lam)[..., None] * lax.complex(b_re[i].astype(F32), b_im[i].astype(F32))
        cc = lax.complex(c_re[i].astype(F32), c_im[i].astype(F32))
        powers = jnp.exp(ldt[None] * taus[:, None, None])
        resp = jnp.real(jnp.einsum('gcp,tgp,gpd->gtcd', cc, powers[:L], b_bar, precision=HIGHEST))
        return powers, b_bar, cc, resp

    pw_f, bb_f, cc_f, k_f = direction(0)
    pw_b, bb_b, cc_b, k_b = direction(1)
    g, p = pw_f.shape[1:]
    c = S5_GROUP
    diff = np.arange(L)[None, :] - np.arange(L)[:, None]
    lag = np.arange(L)[:, None, None]
    tf = jnp.einsum('tsl,gtcd->gslcd', jnp.asarray(lag == diff[None], F32), k_f, precision=HIGHEST)
    tb = jnp.einsum('tsl,gtcd->gslcd', jnp.asarray(lag == -diff[None], F32), k_b, precision=HIGHEST)
    skip = (jnp.eye(L, dtype=F32)[None, :, :, None, None]
            * (jnp.eye(c, dtype=F32)[None] * d_skip.astype(F32)[:, :, None])[:, None, None])
    toep = jnp.transpose(tf + tb + skip, (0, 1, 4, 2, 3)).reshape(g, L * c, L * c)

    def state_in(powers_sel, b_bar):
        w = powers_sel[:, :, :, None] * b_bar[None]
        return jnp.transpose(w, (1, 0, 3, 2)).reshape(g, L * c, p)

    wf = state_in(pw_f[L - 1 - jnp.arange(L)], bb_f)
    wb = state_in(pw_b[jnp.arange(L)], bb_b)
    w_state = jnp.concatenate([jnp.real(wf), jnp.imag(wf), jnp.imag(wf), jnp.real(wf),
                               jnp.real(wb), jnp.imag(wb), jnp.imag(wb), jnp.real(wb)], axis=-1)

    def state_out(powers_sel, cc):
        r = powers_sel[:, :, None, :] * cc[None]
        return jnp.transpose(r, (1, 3, 0, 2)).reshape(g, p, L * c)

    rf = state_out(pw_f[1 + jnp.arange(L)], cc_f)
    rb = state_out(pw_b[L - jnp.arange(L)], cc_b)
    r_state = jnp.concatenate([jnp.real(rf), -jnp.imag(rf), jnp.real(rb), -jnp.imag(rb)], axis=1)

    def mult(a):
        ar, ai = jnp.real(a), jnp.imag(a)
        return jnp.stack([jnp.concatenate([ar, ar], -1), jnp.concatenate([-ai, ai], -1),
                          jnp.concatenate([ai, -ai], -1)])

    return toep, w_state, r_state, mult(pw_f[L]), mult(pw_b[L])


def _s5_pack_kernel(*refs):
    u_refs, (wt_ref, ut_ref, f1_ref, f2_ref, b1_ref, b2_ref) = refs[:-6], refs[-6:]
    L = S5_CHUNK
    g, lc, nck = ut_ref.shape
    c = lc // L
    gs = g // len(u_refs)
    for s in range(L):
        for j, u_ref in enumerate(u_refs):
            step_s = u_ref[pl.ds(s, nck, stride=L), :]
            ut_ref[j * gs:(j + 1) * gs, s * c:(s + 1) * c, :] = (
                jnp.transpose(step_s).astype(BF16).reshape(gs, c, nck))
    n = f1_ref.shape[2]
    for gi in range(g):
        inc = jnp.dot(wt_ref[gi], ut_ref[gi], preferred_element_type=F32)
        for j, ref in enumerate((f1_ref, f2_ref, b1_ref, b2_ref)):
            ref[:, gi, :] = jnp.transpose(inc[j * n:(j + 1) * n, :])


def s5_pack(u, wt_state):
    t, w = u.shape
    g, n4, lc = wt_state.shape
    nc = t // S5_CHUNK
    tile = min(S5_TILE, nc)
    inc = jax.ShapeDtypeStruct((nc, g, n4 // 4), F32)
    inc_blk = pl.BlockSpec((tile, g, n4 // 4), lambda i: (i, 0, 0))
    return pl.pallas_call(
        _s5_pack_kernel,
        grid=(nc // tile,),
        in_specs=[pl.BlockSpec((tile * S5_CHUNK, LANES), functools.partial(lambda i, j: (i, j), j=j))
                  for j in range(w // LANES)] + [pl.BlockSpec(wt_state.shape, lambda i: (0, 0, 0))],
        out_specs=[pl.BlockSpec((g, lc, tile), lambda i: (0, 0, i))] + [inc_blk] * 4,
        out_shape=[jax.ShapeDtypeStruct((g, lc, nc), BF16)] + [inc] * 4,
        compiler_params=_params("parallel"),
    )(*([u] * (w // LANES)), wt_state)


def _s5_scan_kernel(s1_ref, s2_ref, m_ref, init_ref, x_ref, last_ref, v1_ref, v2_ref, *, reverse):
    @pl.when(pl.program_id(0) == 0)
    def _():
        v1_ref[...] = init_ref[0]
        v2_ref[...] = init_ref[1]

    a1, a2, a3 = m_ref[0], m_ref[1], m_ref[2]
    cb = s1_ref.shape[0]

    def body(j, carry):
        v1, v2 = carry
        jj = cb - 1 - j if reverse else j
        x_ref[jj] = v1
        return (a1 * v1 + a2 * v2 + s1_ref[jj], a1 * v2 + a3 * v1 + s2_ref[jj])

    v1, v2 = lax.fori_loop(0, cb, body, (v1_ref[...], v2_ref[...]))
    v1_ref[...] = v1
    v2_ref[...] = v2
    last_ref[...] = v1


def s5_chunk_scan(s1, s2, mult, init, reverse):
    nc, g, n = s1.shape
    cb = min(S5_TILE, nc)
    nb = nc // cb
    blk = (lambda i: (nb - 1 - i, 0, 0)) if reverse else (lambda i: (i, 0, 0))
    return pl.pallas_call(
        functools.partial(_s5_scan_kernel, reverse=reverse),
        grid=(nb,),
        in_specs=[pl.BlockSpec((cb, g, n), blk), pl.BlockSpec((cb, g, n), blk),
                  pl.BlockSpec((3, g, n), lambda i: (0, 0, 0)), pl.BlockSpec((2, g, n), lambda i: (0, 0, 0))],
        out_specs=[pl.BlockSpec((cb, g, n), blk), pl.BlockSpec((g, n), lambda i: (0, 0))],
        out_shape=[jax.ShapeDtypeStruct((nc, g, n), F32), jax.ShapeDtypeStruct((g, n), F32)],
        scratch_shapes=[pltpu.VMEM((g, n), F32), pltpu.VMEM((g, n), F32)],
        compiler_params=_params("arbitrary"),
    )(s1, s2, mult, init)


def _s5_readout_kernel(ut_ref, tt_ref, rt_ref, xf_ref, xb_ref, y_ref, yt_ref, *slab_refs):
    L = S5_CHUNK
    g, lc, nck = ut_ref.shape
    c = lc // L
    gs = g // len(slab_refs)
    for gi in range(g):
        xin_t = jnp.concatenate([jnp.transpose(xf_ref[:, gi, :]), jnp.transpose(xb_ref[:, gi, :])], axis=0)
        yt_ref[gi] = (jnp.dot(tt_ref[gi], ut_ref[gi], preferred_element_type=F32)
                      + jnp.dot(rt_ref[gi], xin_t.astype(BF16), preferred_element_type=F32))
    for j, slab in enumerate(slab_refs):
        for l in range(L):
            step_l = yt_ref[j * gs:(j + 1) * gs, l * c:(l + 1) * c, :].reshape(gs * c, nck)
            slab[pl.ds(l, nck, stride=L), :] = jnp.transpose(step_l)
        y_ref[:, j * gs * c:(j + 1) * gs * c] = slab[...]


def s5_readout(ut, toep_t, r_state_t, xin_f, xin_b):
    g, lc, nc = ut.shape
    n = xin_f.shape[2]
    tile = min(S5_TILE, nc)
    fixed = lambda i: (0, 0, 0)
    state_blk = pl.BlockSpec((tile, g, n), lambda i: (i, 0, 0))
    return pl.pallas_call(
        _s5_readout_kernel,
        grid=(nc // tile,),
        in_specs=[pl.BlockSpec((g, lc, tile), lambda i: (0, 0, i)),
                  pl.BlockSpec(toep_t.shape, fixed), pl.BlockSpec(r_state_t.shape, fixed), state_blk, state_blk],
        out_specs=pl.BlockSpec((tile * S5_CHUNK, g * lc // S5_CHUNK), lambda i: (i, 0)),
        out_shape=jax.ShapeDtypeStruct((nc * S5_CHUNK, g * lc // S5_CHUNK), F32),
        scratch_shapes=[pltpu.VMEM((g, lc, tile), F32)]
        + [pltpu.VMEM((tile * S5_CHUNK, LANES), F32)] * (g * lc // S5_CHUNK // LANES),
        compiler_params=_params("parallel"),
    )(ut, toep_t, r_state_t, xin_f, xin_b)


def s5_mixer(u_ctx, u_lat, mats):
    toep, w_state, r_state, mult_f, mult_b = mats
    L = S5_CHUNK
    g, n = mult_f.shape[1:]
    swap = lambda a: jnp.transpose(a, (0, 2, 1)).astype(BF16)
    wt_state, toep_t, r_state_t = swap(w_state), swap(toep), swap(r_state)
    halves = lambda v: jnp.stack([v, jnp.roll(v, n // 2, axis=-1)])
    n_ctx = u_ctx.shape[0] // L
    ctx_chunks = -(-(n_ctx + 1) // S5_TILE) * S5_TILE
    ctx_pad = jnp.pad(u_ctx, ((0, ctx_chunks * L - u_ctx.shape[0]), (0, 0)))
    _, cf1, cf2, cb1, cb2 = s5_pack(ctx_pad, wt_state)
    zero = jnp.zeros((2, g, n), F32)
    ctx_f, _ = s5_chunk_scan(cf1, cf2, mult_f, zero, False)
    _, ctx_b_last = s5_chunk_scan(cb1, cb2, mult_b, zero, True)
    ut, f1, f2, b1, b2 = s5_pack(u_lat, wt_state)
    xin_f, _ = s5_chunk_scan(f1, f2, mult_f, halves(ctx_f[n_ctx]), False)
    xin_b, _ = s5_chunk_scan(b1, b2, mult_b, halves(ctx_b_last), True)
    return s5_readout(ut, toep_t, r_state_t, xin_f, xin_b)


def na_bias_table(rpb):
    q_col = np.arange(GRID_W)
    col_start = np.clip(q_col - NA_KW // 2, 0, GRID_W - NA_KW)
    key_col = np.arange(GRID_W)
    off = key_col[None, :] - col_start[:, None]
    valid = (off >= 0) & (off < NA_KW)
    rel_col = np.clip(key_col[None, :] - q_col[:, None] + NA_KW - 1, 0, 2 * NA_KW - 2)
    pick_col = jnp.asarray(np.arange(2 * NA_KW - 1)[:, None, None] == rel_col[None], F32)
    rel_row = np.arange(NA_KH)[None, :] - np.arange(NA_KH)[:, None] + NA_KH - 1
    pick_row = jnp.asarray(np.arange(2 * NA_KH - 1)[:, None, None] == rel_row[None], F32)
    table = jnp.einsum('hrj,rdi,jck->dhcik', rpb.astype(F32), pick_row, pick_col, precision=HIGHEST)
    table = table.reshape(NA_KH, rpb.shape[0], GRID_W, NA_KH * GRID_W)
    return jnp.where(np.tile(valid, (1, NA_KH))[None, None], table, NEG_BIG)


def _na_kernel(*refs, scale):
    q_ref = refs[0]
    k_refs = refs[1:1 + NA_KH]
    v_refs = refs[1 + NA_KH:1 + 2 * NA_KH]
    kc_ref, vc_ref, b_ref, o_ref = refs[1 + 2 * NA_KH:]
    hd = NA_HEAD_DIM
    width = q_ref.shape[1]
    span = min(MXU_DEPTH, width)
    nt = (((1,), (1,)), ((), ()))
    q = q_ref[...] * scale
    kk = jnp.concatenate([r[...] for r in k_refs], axis=0)
    vv = jnp.concatenate([r[...] for r in v_refs], axis=0)
    kc, vc = kc_ref[...], vc_ref[...]
    nq = q.shape[0]
    per = span // hd
    lane = lax.broadcasted_iota(I32, (nq, span), 1)
    own = [(lane >= j * hd) & (lane < (j + 1) * hd) for j in range(per)]
    out_cols = []
    for c0 in range(0, width, span):
        cols = slice(c0, c0 + span)
        qs = jnp.concatenate([jnp.where(own[j], q[:, cols], jnp.zeros_like(q[:, cols])) for j in range(per)], axis=0)
        h0 = c0 // hd
        bias = b_ref[0, h0:h0 + per].reshape(per * nq, kk.shape[0])
        s = lax.dot_general(qs, kk[:, cols], nt, preferred_element_type=F32) + bias
        sc = lax.dot_general(qs, kc[:, cols], nt, preferred_element_type=F32)
        m = jnp.maximum(jnp.max(s, axis=-1, keepdims=True), jnp.max(sc, axis=-1, keepdims=True))
        p = jnp.exp(s - m)
        pc = jnp.exp(sc - m)
        den = jnp.sum(p, axis=-1, keepdims=True) + jnp.sum(pc, axis=-1, keepdims=True)
        o = (jnp.dot(p.astype(BF16), vv[:, cols], preferred_element_type=F32)
             + jnp.dot(pc.astype(BF16), vc[:, cols], preferred_element_type=F32)) / den
        acc = jnp.zeros((nq, span), F32)
        for j in range(per):
            acc = jnp.where(own[j], o[j * nq:(j + 1) * nq], acc)
        out_cols.append(acc)
    o_ref[...] = jnp.concatenate(out_cols, axis=1).astype(o_ref.dtype)


def neighbourhood_attention(q, k, v, k_ctx, v_ctx, bias_table):
    t, w = q.shape
    rows = t // GRID_W
    kh = min(NA_KH, rows)
    assert kh == NA_KH
    first = lambda r: jnp.clip(r - kh // 2, 0, rows - kh)
    row_blk = pl.BlockSpec((GRID_W, w), lambda r: (r, 0))
    key_blks = [pl.BlockSpec((GRID_W, w), functools.partial(lambda r, i: (first(r) + i, 0), i=i))
                for i in range(kh)]
    ctx_blk = pl.BlockSpec(k_ctx.shape, lambda r: (0, 0))
    bias_blk = pl.BlockSpec((1,) + bias_table.shape[1:], lambda r: (r - first(r), 0, 0, 0))
    return pl.pallas_call(
        functools.partial(_na_kernel, scale=NA_HEAD_DIM ** -0.5),
        grid=(rows,),
        in_specs=[row_blk] + key_blks + key_blks + [ctx_blk, ctx_blk, bias_blk],
        out_specs=row_blk,
        out_shape=jax.ShapeDtypeStruct((t, w), BF16),
        compiler_params=_params("parallel"),
    )(q, *([k] * kh), *([v] * kh), k_ctx, v_ctx, bias_table)


def time_dft_tables(t):
    a_len = 1 << (int(math.log2(t)) // 2)
    b_len = t // a_len
    ka = np.arange(a_len)[:, None]
    tok = b_len * np.arange(a_len)[None, :]
    ang1 = -2.0 * np.pi * ((ka * (tok[None] + np.arange(b_len)[:, None, None])) % t) / t
    stage1 = np.concatenate([np.cos(ang1), np.sin(ang1)], axis=1) / math.sqrt(t)
    ang2 = 2.0 * np.pi * ((np.arange(b_len)[:, None] * np.arange(b_len)[None, :]) % b_len) / b_len
    c2, s2 = np.cos(ang2), np.sin(ang2)
    stage2 = np.block([[c2, s2], [-s2, c2]])
    return jnp.asarray(stage1, BF16), jnp.asarray(stage2, BF16), a_len, b_len


def _time_dft_kernel(x_ref, m_ref, w2_ref, zre_ref, zim_ref, y_ref, z_ref, *, a_len, b_len):
    i = pl.program_id(1)
    bb = m_ref.shape[0]

    def stage1(jb, carry):
        b = i * bb + jb
        xb = x_ref[pl.ds(b, a_len, stride=b_len), :]
        y = jnp.dot(m_ref[jb], xb.astype(BF16), preferred_element_type=F32)
        row = pl.multiple_of(b * a_len, a_len)
        y_ref[pl.ds(row, a_len), :] = _pack_pair(y[:a_len], y[a_len:])
        return carry

    lax.fori_loop(0, bb, stage1, 0, unroll=4)

    @pl.when(i == pl.num_programs(1) - 1)
    def _():
        def stage2(ka, carry):
            rows = pl.ds(ka, b_len, stride=a_len)
            yre, yim = _unpack_pair(y_ref[rows, :])
            y = jnp.concatenate([yre, yim], axis=0).astype(BF16)
            z = jnp.dot(w2_ref[...], y, preferred_element_type=F32)
            z_ref[rows, :] = _pack_pair(z[:b_len], z[b_len:])
            return carry

        lax.fori_loop(0, a_len, stage2, 0, unroll=4)
        zre, zim = _unpack_pair(z_ref[...])
        zre_ref[...] = zre.astype(zre_ref.dtype)
        zim_ref[...] = zim.astype(zim_ref.dtype)


def time_dft(x):
    t, d = x.shape
    stage1, stage2, a_len, b_len = time_dft_tables(t)
    lanes = 128
    bb = min(16, b_len)
    out = jax.ShapeDtypeStruct((t, d), BF16)
    return pl.pallas_call(
        functools.partial(_time_dft_kernel, a_len=a_len, b_len=b_len),
        grid=(d // lanes, b_len // bb),
        in_specs=[pl.BlockSpec((t, lanes), lambda j, i: (0, j)),
                  pl.BlockSpec((bb, 2 * a_len, a_len), lambda j, i: (i, 0, 0)),
                  pl.BlockSpec(stage2.shape, lambda j, i: (0, 0))],
        out_specs=[pl.BlockSpec((t, lanes), lambda j, i: (0, j))] * 2,
        out_shape=[out, out],
        scratch_shapes=[pltpu.VMEM((t, lanes), U32), pltpu.VMEM((t, lanes), U32)],
        compiler_params=_params("parallel", "arbitrary"),
    )(x, stage1, stage2)


def channel_dft_tables(c):
    ang = 2.0 * np.pi * ((np.arange(c)[:, None] * np.arange(c)[None, :]) % c) / c
    return (jnp.asarray(np.cos(ang) / math.sqrt(c), BF16), jnp.asarray(np.sin(ang) / math.sqrt(c), BF16))


def _shared_expert_kernel(fa_ref, fb_ref, wsg_ref, wsu_ref, wsd_ref, o_ref):
    fb = _unpack_rows(jnp.concatenate([fa_ref[...], fb_ref[...]], axis=1)).astype(BF16)
    hid = (_silu(jnp.dot(fb, wsg_ref[...], preferred_element_type=F32))
           * jnp.dot(fb, wsu_ref[...], preferred_element_type=F32))
    o_ref[...] = jnp.dot(hid.astype(BF16), wsd_ref[...], preferred_element_type=F32).astype(o_ref.dtype)


def shared_expert(f_a, f_b, w_gate, w_up, w_down):
    t, dq = f_a.shape
    d = w_down.shape[1]
    tm = min(512, t)
    row = lambda i: (i, 0)
    fixed = lambda i: (0, 0)
    return pl.pallas_call(
        _shared_expert_kernel,
        grid=(t // tm,),
        in_specs=[pl.BlockSpec((tm, dq), row), pl.BlockSpec((tm, dq), row),
                  pl.BlockSpec(w_gate.shape, fixed), pl.BlockSpec(w_up.shape, fixed), pl.BlockSpec(w_down.shape, fixed)],
        out_specs=pl.BlockSpec((tm, d), row),
        out_shape=jax.ShapeDtypeStruct((t, d), BF16),
        compiler_params=_params("parallel"),
    )(f_a, f_b, w_gate, w_up, w_down)


def _ffn_prologue(h, shf_ref, scf_ref, wr_ref, h_ref, fa_ref, fb_ref, lg_ref):
    h_ref[...] = h
    f = _rms(h) * (1.0 + scf_ref[...]) + shf_ref[...]
    packed = _pack_rows(f)
    half = packed.shape[1] // 2
    fa_ref[...] = packed[:, :half]
    fb_ref[...] = packed[:, half:]
    lg_ref[...] = lax.dot_general(wr_ref[...], f, (((1,), (1,)), ((), ())),
                                  preferred_element_type=F32, precision=HIGHEST)


def _gelu_tanh(x):
    return 0.5 * x * (1.0 + jnp.tanh(math.sqrt(2.0 / math.pi) * (x + 0.044715 * (x * x * x))))


def _even_post_kernel(y_ref, na_ref, x_ref, wglu_ref, bglu_ref, wo_ref, gm_ref, *rest):
    g = _gelu_tanh(y_ref[...])
    gate = jax.nn.sigmoid(jnp.dot(g.astype(BF16), wglu_ref[...], preferred_element_type=F32) + bglu_ref[...])
    s5 = (g * gate).astype(BF16)
    w = s5.shape[1]
    mix = (jnp.dot(s5, wo_ref[:w, :], preferred_element_type=F32)
           + jnp.dot(na_ref[...], wo_ref[w:, :], preferred_element_type=F32))
    _ffn_prologue(x_ref[...] + gm_ref[...] * mix, *rest)


def _odd_post_kernel(zre_ref, zim_ref, h_ref_in, cc_ref, sc_ref, wf_ref, bf_ref, gm_ref, *rest):
    c = cc_ref.shape[0]
    parts = []
    for grp in range(zre_ref.shape[1] // c):
        cols = slice(grp * c, (grp + 1) * c)
        parts.append(jnp.dot(zre_ref[:, cols], cc_ref[...], preferred_element_type=F32)
                     + jnp.dot(zim_ref[:, cols], sc_ref[...], preferred_element_type=F32))
    fr = jnp.concatenate(parts, axis=-1).astype(BF16)
    mix = jnp.dot(fr, wf_ref[...], preferred_element_type=F32) + bf_ref[...]
    _ffn_prologue(h_ref_in[...] + gm_ref[...] * mix, *rest)


def _post_call(body, row_inputs, fixed_inputs, t, d, n_exp):
    tm = min(512, t)
    row = lambda i: (i, 0)
    fixed = lambda i: (0, 0)
    in_specs = ([pl.BlockSpec((tm, a.shape[1]), row) for a in row_inputs]
                + [pl.BlockSpec(a.shape, fixed) for a in fixed_inputs])
    return pl.pallas_call(
        body,
        grid=(t // tm,),
        in_specs=in_specs,
        out_specs=[pl.BlockSpec((tm, d), row), pl.BlockSpec((tm, d // 4), row), pl.BlockSpec((tm, d // 4), row),
                   pl.BlockSpec((n_exp, tm), lambda i: (0, i))],
        out_shape=[jax.ShapeDtypeStruct((t, d), F32),
                   jax.ShapeDtypeStruct((t, d // 4), U32), jax.ShapeDtypeStruct((t, d // 4), U32),
                   jax.ShapeDtypeStruct((n_exp, t), F32)],
        compiler_params=_params("parallel"),
    )(*row_inputs, *fixed_inputs)


def _route_kernel(lg_ref, bias_ref, tri_ref, idx_ref, gate_ref, rank_ref, cnt_ref, run_ref):
    @pl.when(pl.program_id(0) == 0)
    def _():
        run_ref[...] = jnp.zeros_like(run_ref)

    scores = jax.nn.sigmoid(lg_ref[...])
    n_exp, tb = scores.shape
    sel = scores + bias_ref[...]
    gsz = n_exp // N_EXPERT_GROUPS
    member = lax.broadcasted_iota(I32, (gsz, tb), 0)
    gscore = []
    for grp in range(N_EXPERT_GROUPS):
        xg = sel[grp * gsz:(grp + 1) * gsz, :]
        m1 = jnp.max(xg, axis=0, keepdims=True)
        first = jnp.min(jnp.where(xg == m1, member, gsz), axis=0, keepdims=True)
        m2 = jnp.max(jnp.where(member == first, -jnp.inf, xg), axis=0, keepdims=True)
        gscore.append(m1 + m2)
    keep_rows = []
    for grp in range(N_EXPERT_GROUPS):
        beaten = jnp.zeros((1, tb), F32)
        for other in range(N_EXPERT_GROUPS):
            if other == grp:
                continue
            wins = (gscore[other] >= gscore[grp]) if other < grp else (gscore[other] > gscore[grp])
            beaten = beaten + jnp.where(wins, 1.0, 0.0)
        keep_rows.append(jnp.broadcast_to(beaten < TOPK_GROUPS, (gsz, tb)))
    masked = jnp.where(jnp.concatenate(keep_rows, axis=0), sel, -jnp.inf)

    expert = lax.broadcasted_iota(I32, (n_exp, tb), 0)
    picks, gates, hots = [], [], []
    chosen = jnp.zeros((n_exp, tb), F32)
    for _ in range(TOP_K):
        m = jnp.max(masked, axis=0, keepdims=True)
        pick = jnp.min(jnp.where(masked == m, expert, n_exp), axis=0, keepdims=True)
        hot = expert == pick
        picks.append(pick)
        hots.append(hot)
        gates.append(jnp.sum(jnp.where(hot, scores, 0.0), axis=0, keepdims=True))
        chosen = jnp.where(hot, 1.0, chosen)
        masked = jnp.where(hot, -jnp.inf, masked)
    total = gates[0]
    for gk in gates[1:]:
        total = total + gk
    ahead = jnp.dot(chosen.astype(BF16), tri_ref[...], preferred_element_type=F32) + run_ref[...]
    for k in range(TOP_K):
        idx_ref[k:k + 1, :] = picks[k]
        gate_ref[k:k + 1, :] = ROUTED_SCALE * gates[k] / total
        rank_ref[k:k + 1, :] = jnp.sum(jnp.where(hots[k], ahead, 0.0), axis=0, keepdims=True).astype(I32)
    run_ref[...] = run_ref[...] + jnp.sum(chosen, axis=1, keepdims=True)
    cnt_ref[...] = jnp.broadcast_to(run_ref[...], cnt_ref.shape)


def route(logits_t, router_bias):
    n_exp, t = logits_t.shape
    tb = min(512, t)
    tri = jnp.asarray(np.triu(np.ones((tb, tb), np.float32), k=1), BF16)
    tok = lambda i: (0, i)
    idx, gate, rank, cnt = pl.pallas_call(
        _route_kernel,
        grid=(t // tb,),
        in_specs=[pl.BlockSpec((n_exp, tb), tok), pl.BlockSpec((n_exp, 1), lambda i: (0, 0)),
                  pl.BlockSpec((tb, tb), lambda i: (0, 0))],
        out_specs=[pl.BlockSpec((TOP_K, tb), tok)] * 3 + [pl.BlockSpec((n_exp, 128), lambda i: (0, 0))],
        out_shape=[jax.ShapeDtypeStruct((TOP_K, t), I32), jax.ShapeDtypeStruct((TOP_K, t), F32),
                   jax.ShapeDtypeStruct((TOP_K, t), I32), jax.ShapeDtypeStruct((n_exp, 128), F32)],
        scratch_shapes=[pltpu.VMEM((n_exp, 1), F32)],
        compiler_params=_params("arbitrary"),
    )(logits_t, router_bias.astype(F32).reshape(n_exp, 1), tri)
    return idx, gate, rank, cnt[:, 0].astype(I32)


def dispatch_plan(idx, rank, counts, n_blocks):
    n_exp = counts.shape[0]
    padded = (counts + EXPERT_ROWS - 1) // EXPERT_ROWS * EXPERT_ROWS
    pad_end = jnp.cumsum(padded)
    pad_start = pad_end - padded
    experts_iota = jnp.arange(n_exp, dtype=I32)
    dest = jnp.sum(jnp.where(idx[..., None] == experts_iota, pad_start, 0), axis=-1) + rank
    n_valid = (pad_end[-1] // EXPERT_ROWS).astype(I32)
    blk = jnp.minimum(jnp.arange(n_blocks, dtype=I32), n_valid - 1)
    blk_e = jnp.sum((pad_end[None, :] <= blk[:, None] * EXPERT_ROWS).astype(I32), axis=1)
    return dest.astype(I32), jnp.minimum(blk_e, n_exp - 1).astype(I32), n_valid.reshape(1)


def _sc_mesh():
    return plsc.VectorSubcoreMesh(core_axis_name="core", subcore_axis_name="subcore")


def dispatch(f, dest, cap):
    t, w = f.shape
    n_choice = dest.shape[0]

    @functools.partial(pl.kernel, out_type=jax.ShapeDtypeStruct((cap, w), f.dtype), mesh=_sc_mesh(),
                       scratch_types=[])
    def scatter_rows(x_hbm, i_hbm, o_hbm):
        def body(x_vmem, i_vmem):
            pltpu.sync_copy(x_vmem, o_hbm.at[i_vmem.at[0]])

        pltpu.emit_pipeline(
            body,
            grid=(t // SC_WINDOW, n_choice),
            in_specs=[pl.BlockSpec((SC_WINDOW, w), lambda i, k: (i, 0)),
                      pl.BlockSpec((1, SC_WINDOW), lambda i, k: (k, i))],
            out_specs=[],
            core_axis_name=("core", "subcore"),
            dimension_semantics=(pltpu.PARALLEL, pltpu.ARBITRARY),
        )(x_hbm, i_hbm)

    return scatter_rows(f, dest)


def gather_rows(rows, index_row):
    n = index_row.shape[1]
    w = rows.shape[1]

    @functools.partial(pl.kernel, out_type=jax.ShapeDtypeStruct((n, w), rows.dtype), mesh=_sc_mesh(),
                       scratch_types=[])
    def gather(y_hbm, i_hbm, o_hbm):
        def body(i_vmem, o_vmem):
            pltpu.sync_copy(y_hbm.at[i_vmem.at[0]], o_vmem)

        pltpu.emit_pipeline(
            body,
            grid=(n // SC_WINDOW,),
            in_specs=[pl.BlockSpec((1, SC_WINDOW), lambda i: (0, i))],
            out_specs=[pl.BlockSpec((SC_WINDOW, w), lambda i: (i, 0))],
            core_axis_name=("core", "subcore"),
            dimension_semantics=(pltpu.PARALLEL,),
        )(i_hbm, o_hbm)

    return gather(rows, index_row)


def _experts_kernel(be_ref, xa_ref, xb_ref, wg_ref, wu_ref, wd_ref, ya_ref, yb_ref, wg_bf, wu_bf, wd_bf):
    b = pl.program_id(0)

    @pl.when((b == 0) | (be_ref[b] != be_ref[jnp.maximum(b - 1, 0)]))
    def _():
        wg_bf[...] = wg_ref[0, 0].astype(BF16)
        wu_bf[...] = wu_ref[0, 0].astype(BF16)
        wd_bf[...] = wd_ref[0, 0].astype(BF16)

    x = _unpack_rows(jnp.concatenate([xa_ref[...], xb_ref[...]], axis=1)).astype(BF16)
    hid = (_silu(jnp.dot(x, wg_bf[...], preferred_element_type=F32))
           * jnp.dot(x, wu_bf[...], preferred_element_type=F32))
    packed = _pack_rows(jnp.dot(hid.astype(BF16), wd_bf[...], preferred_element_type=F32))
    half = packed.shape[1] // 2
    ya_ref[...] = packed[:, :half]
    yb_ref[...] = packed[:, half:]


def experts(rows_a, rows_b, blk_e, n_valid, layer, w_gate, w_up, w_down):
    cap, dq = rows_a.shape
    _, _, d, ff = w_gate.shape
    blk = pl.BlockSpec((EXPERT_ROWS, dq), lambda b, be: (b, 0))
    out = jax.ShapeDtypeStruct((cap, dq), U32)
    return pl.pallas_call(
        _experts_kernel,
        grid_spec=pltpu.PrefetchScalarGridSpec(
            num_scalar_prefetch=1,
            grid=(n_valid[0],),
            in_specs=[blk, blk,
                      pl.BlockSpec((1, 1, d, ff), lambda b, be: (layer, be[b], 0, 0)),
                      pl.BlockSpec((1, 1, d, ff), lambda b, be: (layer, be[b], 0, 0)),
                      pl.BlockSpec((1, 1, ff, d), lambda b, be: (layer, be[b], 0, 0))],
            out_specs=[blk, blk],
            scratch_shapes=[pltpu.VMEM((d, ff), BF16), pltpu.VMEM((d, ff), BF16), pltpu.VMEM((ff, d), BF16)],
        ),
        out_shape=[out, out],
        compiler_params=_params("arbitrary"),
    )(blk_e, rows_a, rows_b, w_gate, w_up, w_down)


def _combine_kernel(gate_ref, h_ref, shared_ref, gf_ref, sh_ref, sc_ref, *rest):
    ya_refs, yb_refs, outs = rest[:TOP_K], rest[TOP_K:2 * TOP_K], rest[2 * TOP_K:]
    routed = None
    for k in range(TOP_K):
        y = _unpack_rows(jnp.concatenate([ya_refs[k][0], yb_refs[k][0]], axis=1))
        routed = gate_ref[:, k:k + 1] * y if routed is None else routed + gate_ref[:, k:k + 1] * y
    h = h_ref[...] + gf_ref[...] * (routed + shared_ref[...].astype(F32))
    outs[0][...] = h
    if len(outs) > 1:
        outs[1][...] = _rms(h) * (1.0 + sc_ref[...]) + sh_ref[...]


def combine(y_a, y_b, dest, gate_tk, h, shared, gate_ffn, next_shift, next_scale, with_next):
    t, d = h.shape
    n_out = 2 if with_next else 1
    n_choice = dest.shape[0]
    flat = dest.reshape(1, n_choice * t)
    picked_a = gather_rows(y_a, flat).reshape(n_choice, t, d // 4)
    picked_b = gather_rows(y_b, flat).reshape(n_choice, t, d // 4)
    tm = min(256, t)
    row = lambda i: (i, 0)
    fixed = lambda i: (0, 0)
    choice = [pl.BlockSpec((1, tm, d // 4), functools.partial(lambda i, k: (k, i, 0), k=k))
              for k in range(n_choice)] * 2
    return pl.pallas_call(
        _combine_kernel,
        grid=(t // tm,),
        in_specs=[pl.BlockSpec((tm, n_choice), row), pl.BlockSpec((tm, d), row), pl.BlockSpec((tm, d), row),
                  pl.BlockSpec((1, d), fixed), pl.BlockSpec((1, d), fixed), pl.BlockSpec((1, d), fixed)] + choice,
        out_specs=[pl.BlockSpec((tm, d), row)] * n_out,
        out_shape=[jax.ShapeDtypeStruct((t, d), F32)] * n_out,
        compiler_params=_params("parallel"),
    )(gate_tk, h, shared, gate_ffn, next_shift, next_scale, *([picked_a] * n_choice), *([picked_b] * n_choice))


def moe_tail(h, f_a, f_b, logits_t, shared_weights, router_bias, layer, w_gate, w_up, w_down, gate_ffn,
             next_shift, next_scale, with_next):
    t, d = h.shape
    n_exp = w_gate.shape[1]
    n_blocks = -(-(t * TOP_K + n_exp * (EXPERT_ROWS - 1)) // EXPERT_ROWS)
    cap = n_blocks * EXPERT_ROWS
    idx, gate, rank, counts = route(logits_t, router_bias)
    dest, blk_e, n_valid = dispatch_plan(idx, rank, counts, n_blocks)
    rows_a, rows_b = dispatch(f_a, dest, cap), dispatch(f_b, dest, cap)
    shared = shared_expert(f_a, f_b, *shared_weights)
    y_a, y_b = experts(rows_a, rows_b, blk_e, n_valid, layer, w_gate, w_up, w_down)
    return combine(y_a, y_b, dest, gate.T, h, shared, gate_ffn, next_shift, next_scale, with_next)


def kernel(x, c, ctx, c_ctx, w_ada, b_ada, w_in, s5_lam_re, s5_lam_im, s5_log_dt, s5_b_re, s5_b_im,
           s5_c_re, s5_c_im, s5_d, s5_w_glu, s5_b_glu, na_q_gain, na_k_gain, na_rpb, w_mix_out,
           w_fourier_out, b_fourier_out, w_router, router_bias, w_exp_gate, w_exp_up, w_exp_down,
           w_sh_gate, w_sh_up, w_sh_down):
    bsz, t, d = x.shape
    assert bsz == 1 and w_ada.shape[0] == 2
    n_exp = w_router.shape[2]
    s5w = s5_w_glu.shape[1]
    naw = w_in.shape[2] - s5w
    naw //= 3
    heads = naw // NA_HEAD_DIM

    cond8 = jnp.concatenate([c[:1].astype(F32), c_ctx.astype(F32)[None], jnp.zeros((6, d), F32)], axis=0)
    ada = adaln_all(cond8, w_ada, b_ada)
    mod = lambda layer, who, j: ada[layer, who:who + 1, j * d:(j + 1) * d]

    def ffn_weights(i):
        return mod(i, 0, 3), mod(i, 0, 4), jnp.transpose(w_router[i]).astype(F32)

    def shared_weights(i):
        return w_sh_gate[i].astype(BF16), w_sh_up[i].astype(BF16), w_sh_down[i].astype(BF16)

    h0 = x[0]
    seg = jnp.asarray(np.kron(np.eye(heads), np.ones((NA_HEAD_DIM, NA_HEAD_DIM))), BF16)
    w_in_b = w_in[0].astype(BF16)
    qg = jnp.tile(na_q_gain[0].astype(F32), heads)[None]
    kg = jnp.tile(na_k_gain[0].astype(F32), heads)[None]
    u_c, _, k_c, v_c = in_projection(ctx[0], mod(0, 1, 0), mod(0, 1, 1), w_in_b, seg, qg, kg, s5w, naw)
    u_l, q_l, k_l, v_l = in_projection(h0, mod(0, 0, 0), mod(0, 0, 1), w_in_b, seg, qg, kg, s5w, naw)
    mats = s5_matrices(s5_lam_re[0], s5_lam_im[0], s5_log_dt[0], s5_b_re[0], s5_b_im[0],
                       s5_c_re[0], s5_c_im[0], s5_d[0])
    y_s5 = s5_mixer(u_c, u_l, mats)
    na = neighbourhood_attention(q_l, k_l, v_l, k_c, v_c, na_bias_table(na_rpb[0]))
    h1, f1a, f1b, lg1 = _post_call(
        _even_post_kernel, [y_s5, na, h0],
        [s5_w_glu[0].astype(BF16), s5_b_glu[0].astype(F32)[None], w_mix_out[0].astype(BF16), mod(0, 0, 2),
         *ffn_weights(0)], t, d, n_exp)
    h2, a1 = moe_tail(h1, f1a, f1b, lg1, shared_weights(0), router_bias[0], 0, w_exp_gate, w_exp_up, w_exp_down,
                      mod(0, 0, 5), mod(1, 0, 0), mod(1, 0, 1), True)

    zre, zim = time_dft(a1)
    cc, sc = channel_dft_tables(d // FOURIER_GROUPS)
    h3, f3a, f3b, lg3 = _post_call(
        _odd_post_kernel, [zre, zim, h2],
        [cc, sc, w_fourier_out[0].astype(BF16), b_fourier_out[0].astype(F32)[None], mod(1, 0, 2),
         *ffn_weights(1)], t, d, n_exp)
    zero_row = jnp.zeros((1, d), F32)
    (out,) = moe_tail(h3, f3a, f3b, lg3, shared_weights(1), router_bias[1], 1, w_exp_gate, w_exp_up, w_exp_down,
                      mod(1, 0, 5), zero_row, zero_row, False)
    return out[None]
```

```python
import functools
import math

import numpy as np
import jax
import jax.numpy as jnp
from jax import lax
from jax.experimental import pallas as pl
from jax.experimental.pallas import tpu as pltpu
from jax.experimental.pallas import tpu_sc as plsc

F32 = jnp.float32
BF16 = jnp.bfloat16
I32 = jnp.int32
U32 = jnp.uint32
HIGHEST = lax.Precision.HIGHEST

MXU_DEPTH = 256
LANES = 128
GRID_W = 64
NORM_EPS = 1e-6
S5_GROUP = 16
S5_STATE = 64
S5_LAMBDA_RE_MAX = -1e-4
S5_CHUNK = 16
S5_TILE = 128
NA_HEADS = 8
NA_HEAD_DIM = 64
NA_KH = 8
NA_KW = 16
FOURIER_GROUPS = 4
N_EXPERT_GROUPS = 8
TOPK_GROUPS = 4
TOP_K = 8
ROUTED_SCALE = 2.5
EXPERT_ROWS = 1024
SC_WINDOW = 128
NEG_BIG = -1e30

VMEM_LIMIT_BYTES = 56 * 1024 * 1024


def _params(*sem):
    return pltpu.CompilerParams(dimension_semantics=sem or None,
                                vmem_limit_bytes=VMEM_LIMIT_BYTES)


def _rms(x):
    return x * lax.rsqrt(jnp.mean(x * x, axis=-1, keepdims=True) + NORM_EPS)


def _silu(x):
    return x * jax.nn.sigmoid(x)


def _pack_pair(lo, hi):
    lo = lax.bitcast_convert_type(lo.astype(BF16).astype(F32), U32)
    hi = lax.bitcast_convert_type(hi.astype(BF16).astype(F32), U32)
    return (hi & jnp.uint32(0xFFFF0000)) | (lo >> 16)


def _unpack_pair(w):
    return (lax.bitcast_convert_type(w << 16, F32), lax.bitcast_convert_type(w & jnp.uint32(0xFFFF0000), F32))


def _pack_rows(x):
    n = x.shape[1] // 2
    return _pack_pair(x[:, :n], x[:, n:])


def _unpack_rows(w):
    return jnp.concatenate(_unpack_pair(w), axis=1)


def _ada_kernel(c_ref, w_ref, b_ref, o_ref):
    o_ref[0] = jnp.dot(_silu(c_ref[...]), w_ref[0], preferred_element_type=F32,
                       precision=HIGHEST) + b_ref[0]


def adaln_all(cond8, w_ada, b_ada):
    n_layers, d, n6 = w_ada.shape
    tn = n6 // 4
    return pl.pallas_call(
        _ada_kernel,
        grid=(n_layers, n6 // tn),
        in_specs=[pl.BlockSpec((8, d), lambda l, j: (0, 0)),
                  pl.BlockSpec((1, d, tn), lambda l, j: (l, 0, j)),
                  pl.BlockSpec((1, 1, tn), lambda l, j: (l, 0, j))],
        out_specs=pl.BlockSpec((1, 8, tn), lambda l, j: (l, 0, j)),
        out_shape=jax.ShapeDtypeStruct((n_layers, 8, n6), F32),
        compiler_params=_params("parallel", "parallel"),
    )(cond8, w_ada, b_ada.reshape(n_layers, 1, n6))


def _inproj_kernel(x_ref, sh_ref, sc_ref, w_ref, seg_ref, qg_ref, kg_ref,
                   u_ref, q_ref, k_ref, v_ref):
    a = _rms(x_ref[...]) * (1.0 + sc_ref[...]) + sh_ref[...]
    z = jnp.dot(a.astype(BF16), w_ref[...], preferred_element_type=F32)
    s5w = u_ref.shape[1]
    naw = q_ref.shape[1]

    def head_norm(t, gain):
        ss = jnp.dot((t * t).astype(BF16), seg_ref[...], preferred_element_type=F32)
        return t * lax.rsqrt(ss * (1.0 / NA_HEAD_DIM) + NORM_EPS) * gain

    u_ref[...] = z[:, :s5w]
    q_ref[...] = head_norm(z[:, s5w:s5w + naw], qg_ref[...]).astype(BF16)
    k_ref[...] = head_norm(z[:, s5w + naw:s5w + 2 * naw], kg_ref[...]).astype(BF16)
    v_ref[...] = z[:, s5w + 2 * naw:].astype(BF16)


def in_projection(x, shift, scale, w_in_bf16, seg_ones, q_gain_row, k_gain_row, s5w, naw):
    t, d = x.shape
    tm = min(512, t)
    row = lambda i: (i, 0)
    fixed = lambda i: (0, 0)
    return pl.pallas_call(
        _inproj_kernel,
        grid=(t // tm,),
        in_specs=[pl.BlockSpec((tm, d), row),
                  pl.BlockSpec((1, d), fixed), pl.BlockSpec((1, d), fixed),
                  pl.BlockSpec(w_in_bf16.shape, fixed),
                  pl.BlockSpec(seg_ones.shape, fixed),
                  pl.BlockSpec((1, naw), fixed), pl.BlockSpec((1, naw), fixed)],
        out_specs=[pl.BlockSpec((tm, s5w), row), pl.BlockSpec((tm, naw), row),
                   pl.BlockSpec((tm, naw), row), pl.BlockSpec((tm, naw), row)],
        out_shape=[jax.ShapeDtypeStruct((t, s5w), F32)] + [jax.ShapeDtypeStruct((t, naw), BF16)] * 3,
        compiler_params=_params("parallel"),
    )(x, shift, scale, w_in_bf16, seg_ones, q_gain_row, k_gain_row)


def s5_matrices(lam_re, lam_im, log_dt, b_re, b_im, c_re, c_im, d_skip):
    L = S5_CHUNK
    taus = jnp.arange(L + 1, dtype=F32)

    def direction(i):
        lam = lax.complex(jnp.minimum(lam_re[i].astype(F32), S5_LAMBDA_RE_MAX), lam_im[i].astype(F32))
        ldt = lam * jnp.exp(log_dt[i].astype(F32))[:, None]
        lam_bar = jnp.exp(ldt)
        b_bar = ((lam_bar - 1.0) / lam)[..., None] * lax.complex(b_re[i].astype(F32), b_im[i].astype(F32))
        cc = lax.complex(c_re[i].astype(F32), c_im[i].astype(F32))
        powers = jnp.exp(ldt[None] * taus[:, None, None])
        resp = jnp.real(jnp.einsum('gcp,tgp,gpd->gtcd', cc, powers[:L], b_bar, precision=HIGHEST))
        return powers, b_bar, cc, resp

    pw_f, bb_f, cc_f, k_f = direction(0)
    pw_b, bb_b, cc_b, k_b = direction(1)
    g, p = pw_f.shape[1:]
    c = S5_GROUP
    diff = np.arange(L)[None, :] - np.arange(L)[:, None]
    lag = np.arange(L)[:, None, None]
    tf = jnp.einsum('tsl,gtcd->gslcd', jnp.asarray(lag == diff[None], F32), k_f, precision=HIGHEST)
    tb = jnp.einsum('tsl,gtcd->gslcd', jnp.asarray(lag == -diff[None], F32), k_b, precision=HIGHEST)
    skip = (jnp.eye(L, dtype=F32)[None, :, :, None, None]
            * (jnp.eye(c, dtype=F32)[None] * d_skip.astype(F32)[:, :, None])[:, None, None])
    toep = jnp.transpose(tf + tb + skip, (0, 1, 4, 2, 3)).reshape(g, L * c, L * c)

    def state_in(powers_sel, b_bar):
        w = powers_sel[:, :, :, None] * b_bar[None]
        return jnp.transpose(w, (1, 0, 3, 2)).reshape(g, L * c, p)

    wf = state_in(pw_f[L - 1 - jnp.arange(L)], bb_f)
    wb = state_in(pw_b[jnp.arange(L)], bb_b)
    w_state = jnp.concatenate([jnp.real(wf), jnp.imag(wf), jnp.imag(wf), jnp.real(wf),
                               jnp.real(wb), jnp.imag(wb), jnp.imag(wb), jnp.real(wb)], axis=-1)

    def state_out(powers_sel, cc):
        r = powers_sel[:, :, None, :] * cc[None]
        return jnp.transpose(r, (1, 3, 0, 2)).reshape(g, p, L * c)

    rf = state_out(pw_f[1 + jnp.arange(L)], cc_f)
    rb = state_out(pw_b[L - jnp.arange(L)], cc_b)
    r_state = jnp.concatenate([jnp.real(rf), -jnp.imag(rf), jnp.real(rb), -jnp.imag(rb)], axis=1)

    def mult(a):
        ar, ai = jnp.real(a), jnp.imag(a)
        return jnp.stack([jnp.concatenate([ar, ar], -1), jnp.concatenate([-ai, ai], -1),
                          jnp.concatenate([ai, -ai], -1)])

    return toep, w_state, r_state, mult(pw_f[L]), mult(pw_b[L])


def _s5_pack_kernel(*refs):
    u_refs, (wt_ref, ut_ref, f1_ref, f2_ref, b1_ref, b2_ref) = refs[:-6], refs[-6:]
    L = S5_CHUNK
    g, lc, nck = ut_ref.shape
    c = lc // L
    gs = g // len(u_refs)
    for s in range(L):
        for j, u_ref in enumerate(u_refs):
            step_s = u_ref[pl.ds(s, nck, stride=L), :]
            ut_ref[j * gs:(j + 1) * gs, s * c:(s + 1) * c, :] = (
                jnp.transpose(step_s).astype(BF16).reshape(gs, c, nck))
    n = f1_ref.shape[2]
    for gi in range(g):
        inc = jnp.dot(wt_ref[gi], ut_ref[gi], preferred_element_type=F32)
        for j, ref in enumerate((f1_ref, f2_ref, b1_ref, b2_ref)):
            ref[:, gi, :] = jnp.transpose(inc[j * n:(j + 1) * n, :])


def s5_pack(u, wt_state):
    t, w = u.shape
    g, n4, lc = wt_state.shape
    nc = t // S5_CHUNK
    tile = min(S5_TILE, nc)
    inc = jax.ShapeDtypeStruct((nc, g, n4 // 4), F32)
    inc_blk = pl.BlockSpec((tile, g, n4 // 4), lambda i: (i, 0, 0))
    return pl.pallas_call(
        _s5_pack_kernel,
        grid=(nc // tile,),
        in_specs=[pl.BlockSpec((tile * S5_CHUNK, LANES), functools.partial(lambda i, j: (i, j), j=j))
                  for j in range(w // LANES)] + [pl.BlockSpec(wt_state.shape, lambda i: (0, 0, 0))],
        out_specs=[pl.BlockSpec((g, lc, tile), lambda i: (0, 0, i))] + [inc_blk] * 4,
        out_shape=[jax.ShapeDtypeStruct((g, lc, nc), BF16)] + [inc] * 4,
        compiler_params=_params("parallel"),
    )(*([u] * (w // LANES)), wt_state)


def _s5_scan_kernel(s1_ref, s2_ref, m_ref, init_ref, x_ref, last_ref, v1_ref, v2_ref, *, reverse):
    @pl.when(pl.program_id(0) == 0)
    def _():
        v1_ref[...] = init_ref[0]
        v2_ref[...] = init_ref[1]

    a1, a2, a3 = m_ref[0], m_ref[1], m_ref[2]
    cb = s1_ref.shape[0]

    def body(j, carry):
        v1, v2 = carry
        jj = cb - 1 - j if reverse else j
        x_ref[jj] = v1
        return (a1 * v1 + a2 * v2 + s1_ref[jj], a1 * v2 + a3 * v1 + s2_ref[jj])

    v1, v2 = lax.fori_loop(0, cb, body, (v1_ref[...], v2_ref[...]))
    v1_ref[...] = v1
    v2_ref[...] = v2
    last_ref[...] = v1


def s5_chunk_scan(s1, s2, mult, init, reverse):
    nc, g, n = s1.shape
    cb = min(S5_TILE, nc)
    nb = nc // cb
    blk = (lambda i: (nb - 1 - i, 0, 0)) if reverse else (lambda i: (i, 0, 0))
    return pl.pallas_call(
        functools.partial(_s5_scan_kernel, reverse=reverse),
        grid=(nb,),
        in_specs=[pl.BlockSpec((cb, g, n), blk), pl.BlockSpec((cb, g, n), blk),
                  pl.BlockSpec((3, g, n), lambda i: (0, 0, 0)), pl.BlockSpec((2, g, n), lambda i: (0, 0, 0))],
        out_specs=[pl.BlockSpec((cb, g, n), blk), pl.BlockSpec((g, n), lambda i: (0, 0))],
        out_shape=[jax.ShapeDtypeStruct((nc, g, n), F32), jax.ShapeDtypeStruct((g, n), F32)],
        scratch_shapes=[pltpu.VMEM((g, n), F32), pltpu.VMEM((g, n), F32)],
        compiler_params=_params("arbitrary"),
    )(s1, s2, mult, init)


def _s5_readout_kernel(ut_ref, tt_ref, rt_ref, xf_ref, xb_ref, y_ref, yt_ref, *slab_refs):
    L = S5_CHUNK
    g, lc, nck = ut_ref.shape
    c = lc // L
    gs = g // len(slab_refs)
    for gi in range(g):
        xin_t = jnp.concatenate([jnp.transpose(xf_ref[:, gi, :]), jnp.transpose(xb_ref[:, gi, :])], axis=0)
        yt_ref[gi] = (jnp.dot(tt_ref[gi], ut_ref[gi], preferred_element_type=F32)
                      + jnp.dot(rt_ref[gi], xin_t.astype(BF16), preferred_element_type=F32))
    for j, slab in enumerate(slab_refs):
        for l in range(L):
            step_l = yt_ref[j * gs:(j + 1) * gs, l * c:(l + 1) * c, :].reshape(gs * c, nck)
            slab[pl.ds(l, nck, stride=L), :] = jnp.transpose(step_l)
        y_ref[:, j * gs * c:(j + 1) * gs * c] = slab[...]


def s5_readout(ut, toep_t, r_state_t, xin_f, xin_b):
    g, lc, nc = ut.shape
    n = xin_f.shape[2]
    tile = min(S5_TILE, nc)
    fixed = lambda i: (0, 0, 0)
    state_blk = pl.BlockSpec((tile, g, n), lambda i: (i, 0, 0))
    return pl.pallas_call(
        _s5_readout_kernel,
        grid=(nc // tile,),
        in_specs=[pl.BlockSpec((g, lc, tile), lambda i: (0, 0, i)),
                  pl.BlockSpec(toep_t.shape, fixed), pl.BlockSpec(r_state_t.shape, fixed), state_blk, state_blk],
        out_specs=pl.BlockSpec((tile * S5_CHUNK, g * lc // S5_CHUNK), lambda i: (i, 0)),
        out_shape=jax.ShapeDtypeStruct((nc * S5_CHUNK, g * lc // S5_CHUNK), F32),
        scratch_shapes=[pltpu.VMEM((g, lc, tile), F32)]
        + [pltpu.VMEM((tile * S5_CHUNK, LANES), F32)] * (g * lc // S5_CHUNK // LANES),
        compiler_params=_params("parallel"),
    )(ut, toep_t, r_state_t, xin_f, xin_b)


def s5_mixer(u_ctx, u_lat, mats):
    toep, w_state, r_state, mult_f, mult_b = mats
    L = S5_CHUNK
    g, n = mult_f.shape[1:]
    swap = lambda a: jnp.transpose(a, (0, 2, 1)).astype(BF16)
    wt_state, toep_t, r_state_t = swap(w_state), swap(toep), swap(r_state)
    halves = lambda v: jnp.stack([v, jnp.roll(v, n // 2, axis=-1)])
    n_ctx = u_ctx.shape[0] // L
    ctx_chunks = -(-(n_ctx + 1) // S5_TILE) * S5_TILE
    ctx_pad = jnp.pad(u_ctx, ((0, ctx_chunks * L - u_ctx.shape[0]), (0, 0)))
    _, cf1, cf2, cb1, cb2 = s5_pack(ctx_pad, wt_state)
    zero = jnp.zeros((2, g, n), F32)
    ctx_f, _ = s5_chunk_scan(cf1, cf2, mult_f, zero, False)
    _, ctx_b_last = s5_chunk_scan(cb1, cb2, mult_b, zero, True)
    ut, f1, f2, b1, b2 = s5_pack(u_lat, wt_state)
    xin_f, _ = s5_chunk_scan(f1, f2, mult_f, halves(ctx_f[n_ctx]), False)
    xin_b, _ = s5_chunk_scan(b1, b2, mult_b, halves(ctx_b_last), True)
    return s5_readout(ut, toep_t, r_state_t, xin_f, xin_b)


def na_bias_table(rpb):
    q_col = np.arange(GRID_W)
    col_start = np.clip(q_col - NA_KW // 2, 0, GRID_W - NA_KW)
    key_col = np.arange(GRID_W)
    off = key_col[None, :] - col_start[:, None]
    valid = (off >= 0) & (off < NA_KW)
    rel_col = np.clip(key_col[None, :] - q_col[:, None] + NA_KW - 1, 0, 2 * NA_KW - 2)
    pick_col = jnp.asarray(np.arange(2 * NA_KW - 1)[:, None, None] == rel_col[None], F32)
    rel_row = np.arange(NA_KH)[None, :] - np.arange(NA_KH)[:, None] + NA_KH - 1
    pick_row = jnp.asarray(np.arange(2 * NA_KH - 1)[:, None, None] == rel_row[None], F32)
    table = jnp.einsum('hrj,rdi,jck->dhcik', rpb.astype(F32), pick_row, pick_col, precision=HIGHEST)
    table = table.reshape(NA_KH, rpb.shape[0], GRID_W, NA_KH * GRID_W)
    return jnp.where(np.tile(valid, (1, NA_KH))[None, None], table, NEG_BIG)


def _na_kernel(*refs, scale):
    q_ref = refs[0]
    k_refs = refs[1:1 + NA_KH]
    v_refs = refs[1 + NA_KH:1 + 2 * NA_KH]
    kc_ref, vc_ref, b_ref, o_ref = refs[1 + 2 * NA_KH:]
    hd = NA_HEAD_DIM
    width = q_ref.shape[1]
    span = min(MXU_DEPTH, width)
    nt = (((1,), (1,)), ((), ()))
    q = q_ref[...] * scale
    kk = jnp.concatenate([r[...] for r in k_refs], axis=0)
    vv = jnp.concatenate([r[...] for r in v_refs], axis=0)
    kc, vc = kc_ref[...], vc_ref[...]
    nq = q.shape[0]
    per = span // hd
    lane = lax.broadcasted_iota(I32, (nq, span), 1)
    own = [(lane >= j * hd) & (lane < (j + 1) * hd) for j in range(per)]
    out_cols = []
    for c0 in range(0, width, span):
        cols = slice(c0, c0 + span)
        qs = jnp.concatenate([jnp.where(own[j], q[:, cols], jnp.zeros_like(q[:, cols])) for j in range(per)], axis=0)
        h0 = c0 // hd
        bias = b_ref[0, h0:h0 + per].reshape(per * nq, kk.shape[0])
        s = lax.dot_general(qs, kk[:, cols], nt, preferred_element_type=F32) + bias
        sc = lax.dot_general(qs, kc[:, cols], nt, preferred_element_type=F32)
        m = jnp.maximum(jnp.max(s, axis=-1, keepdims=True), jnp.max(sc, axis=-1, keepdims=True))
        p = jnp.exp(s - m)
        pc = jnp.exp(sc - m)
        den = jnp.sum(p, axis=-1, keepdims=True) + jnp.sum(pc, axis=-1, keepdims=True)
        o = (jnp.dot(p.astype(BF16), vv[:, cols], preferred_element_type=F32)
             + jnp.dot(pc.astype(BF16), vc[:, cols], preferred_element_type=F32)) / den
        acc = jnp.zeros((nq, span), F32)
        for j in range(per):
            acc = jnp.where(own[j], o[j * nq:(j + 1) * nq], acc)
        out_cols.append(acc)
    o_ref[...] = jnp.concatenate(out_cols, axis=1).astype(o_ref.dtype)


def neighbourhood_attention(q, k, v, k_ctx, v_ctx, bias_table):
    t, w = q.shape
    rows = t // GRID_W
    kh = min(NA_KH, rows)
    assert kh == NA_KH
    first = lambda r: jnp.clip(r - kh // 2, 0, rows - kh)
    row_blk = pl.BlockSpec((GRID_W, w), lambda r: (r, 0))
    key_blks = [pl.BlockSpec((GRID_W, w), functools.partial(lambda r, i: (first(r) + i, 0), i=i))
                for i in range(kh)]
    ctx_blk = pl.BlockSpec(k_ctx.shape, lambda r: (0, 0))
    bias_blk = pl.BlockSpec((1,) + bias_table.shape[1:], lambda r: (r - first(r), 0, 0, 0))
    return pl.pallas_call(
        functools.partial(_na_kernel, scale=NA_HEAD_DIM ** -0.5),
        grid=(rows,),
        in_specs=[row_blk] + key_blks + key_blks + [ctx_blk, ctx_blk, bias_blk],
        out_specs=row_blk,
        out_shape=jax.ShapeDtypeStruct((t, w), BF16),
        compiler_params=_params("parallel"),
    )(q, *([k] * kh), *([v] * kh), k_ctx, v_ctx, bias_table)


def time_dft_tables(t):
    a_len = 1 << (int(math.log2(t)) // 2)
    b_len = t // a_len
    ka = np.arange(a_len)[:, None]
    tok = b_len * np.arange(a_len)[None, :]
    ang1 = -2.0 * np.pi * ((ka * (tok[None] + np.arange(b_len)[:, None, None])) % t) / t
    stage1 = np.concatenate([np.cos(ang1), np.sin(ang1)], axis=1) / math.sqrt(t)
    ang2 = 2.0 * np.pi * ((np.arange(b_len)[:, None] * np.arange(b_len)[None, :]) % b_len) / b_len
    c2, s2 = np.cos(ang2), np.sin(ang2)
    stage2 = np.block([[c2, s2], [-s2, c2]])
    return jnp.asarray(stage1, BF16), jnp.asarray(stage2, BF16), a_len, b_len


def _time_dft_kernel(x_ref, m_ref, w2_ref, zre_ref, zim_ref, y_ref, z_ref, *, a_len, b_len):
    i = pl.program_id(1)
    bb = m_ref.shape[0]

    def stage1(jb, carry):
        b = i * bb + jb
        xb = x_ref[pl.ds(b, a_len, stride=b_len), :]
        y = jnp.dot(m_ref[jb], xb.astype(BF16), preferred_element_type=F32)
        row = pl.multiple_of(b * a_len, a_len)
        y_ref[pl.ds(row, a_len), :] = _pack_pair(y[:a_len], y[a_len:])
        return carry

    lax.fori_loop(0, bb, stage1, 0, unroll=4)

    @pl.when(i == pl.num_programs(1) - 1)
    def _():
        def stage2(ka, carry):
            rows = pl.ds(ka, b_len, stride=a_len)
            yre, yim = _unpack_pair(y_ref[rows, :])
            y = jnp.concatenate([yre, yim], axis=0).astype(BF16)
            z = jnp.dot(w2_ref[...], y, preferred_element_type=F32)
            z_ref[rows, :] = _pack_pair(z[:b_len], z[b_len:])
            return carry

        lax.fori_loop(0, a_len, stage2, 0, unroll=4)
        zre, zim = _unpack_pair(z_ref[...])
        zre_ref[...] = zre.astype(zre_ref.dtype)
        zim_ref[...] = zim.astype(zim_ref.dtype)


def time_dft(x):
    t, d = x.shape
    stage1, stage2, a_len, b_len = time_dft_tables(t)
    lanes = 128
    bb = min(16, b_len)
    out = jax.ShapeDtypeStruct((t, d), BF16)
    return pl.pallas_call(
        functools.partial(_time_dft_kernel, a_len=a_len, b_len=b_len),
        grid=(d // lanes, b_len // bb),
        in_specs=[pl.BlockSpec((t, lanes), lambda j, i: (0, j)),
                  pl.BlockSpec((bb, 2 * a_len, a_len), lambda j, i: (i, 0, 0)),
                  pl.BlockSpec(stage2.shape, lambda j, i: (0, 0))],
        out_specs=[pl.BlockSpec((t, lanes), lambda j, i: (0, j))] * 2,
        out_shape=[out, out],
        scratch_shapes=[pltpu.VMEM((t, lanes), U32), pltpu.VMEM((t, lanes), U32)],
        compiler_params=_params("parallel", "arbitrary"),
    )(x, stage1, stage2)


def channel_dft_tables(c):
    ang = 2.0 * np.pi * ((np.arange(c)[:, None] * np.arange(c)[None, :]) % c) / c
    return (jnp.asarray(np.cos(ang) / math.sqrt(c), BF16), jnp.asarray(np.sin(ang) / math.sqrt(c), BF16))


def _ffn_prologue(h, shf_ref, scf_ref, wr_ref, wsg_ref, wsu_ref, wsd_ref, h_ref, fa_ref, fb_ref, lg_ref,
                  shared_ref):
    h_ref[...] = h
    f = _rms(h) * (1.0 + scf_ref[...]) + shf_ref[...]
    packed = _pack_rows(f)
    half = packed.shape[1] // 2
    fa_ref[...] = packed[:, :half]
    fb_ref[...] = packed[:, half:]
    lg_ref[...] = lax.dot_general(wr_ref[...], f, (((1,), (1,)), ((), ())),
                                  preferred_element_type=F32, precision=HIGHEST)
    fb = f.astype(BF16)
    hid = (_silu(jnp.dot(fb, wsg_ref[...], preferred_element_type=F32))
           * jnp.dot(fb, wsu_ref[...], preferred_element_type=F32))
    shared_ref[...] = jnp.dot(hid.astype(BF16), wsd_ref[...], preferred_element_type=F32).astype(shared_ref.dtype)


def _gelu_tanh(x):
    return 0.5 * x * (1.0 + jnp.tanh(math.sqrt(2.0 / math.pi) * (x + 0.044715 * (x * x * x))))


def _even_post_kernel(y_ref, na_ref, x_ref, wglu_ref, bglu_ref, wo_ref, gm_ref, *rest):
    g = _gelu_tanh(y_ref[...])
    gate = jax.nn.sigmoid(jnp.dot(g.astype(BF16), wglu_ref[...], preferred_element_type=F32) + bglu_ref[...])
    s5 = (g * gate).astype(BF16)
    w = s5.shape[1]
    mix = (jnp.dot(s5, wo_ref[:w, :], preferred_element_type=F32)
           + jnp.dot(na_ref[...], wo_ref[w:, :], preferred_element_type=F32))
    _ffn_prologue(x_ref[...] + gm_ref[...] * mix, *rest)


def _odd_post_kernel(zre_ref, zim_ref, h_ref_in, cc_ref, sc_ref, wf_ref, bf_ref, gm_ref, *rest):
    c = cc_ref.shape[0]
    parts = []
    for grp in range(zre_ref.shape[1] // c):
        cols = slice(grp * c, (grp + 1) * c)
        parts.append(jnp.dot(zre_ref[:, cols], cc_ref[...], preferred_element_type=F32)
                     + jnp.dot(zim_ref[:, cols], sc_ref[...], preferred_element_type=F32))
    fr = jnp.concatenate(parts, axis=-1).astype(BF16)
    mix = jnp.dot(fr, wf_ref[...], preferred_element_type=F32) + bf_ref[...]
    _ffn_prologue(h_ref_in[...] + gm_ref[...] * mix, *rest)


def _post_call(body, row_inputs, fixed_inputs, t, d, n_exp):
    tm = min(512, t)
    row = lambda i: (i, 0)
    fixed = lambda i: (0, 0)
    in_specs = ([pl.BlockSpec((tm, a.shape[1]), row) for a in row_inputs]
                + [pl.BlockSpec(a.shape, fixed) for a in fixed_inputs])
    return pl.pallas_call(
        body,
        grid=(t // tm,),
        in_specs=in_specs,
        out_specs=[pl.BlockSpec((tm, d), row), pl.BlockSpec((tm, d // 4), row), pl.BlockSpec((tm, d // 4), row),
                   pl.BlockSpec((n_exp, tm), lambda i: (0, i)), pl.BlockSpec((tm, d), row)],
        out_shape=[jax.ShapeDtypeStruct((t, d), F32),
                   jax.ShapeDtypeStruct((t, d // 4), U32), jax.ShapeDtypeStruct((t, d // 4), U32),
                   jax.ShapeDtypeStruct((n_exp, t), F32), jax.ShapeDtypeStruct((t, d), BF16)],
        compiler_params=_params("parallel"),
    )(*row_inputs, *fixed_inputs)


def _route_kernel(lg_ref, bias_ref, tri_ref, idx_ref, gate_ref, rank_ref, cnt_ref, run_ref):
    @pl.when(pl.program_id(0) == 0)
    def _():
        run_ref[...] = jnp.zeros_like(run_ref)

    scores = jax.nn.sigmoid(lg_ref[...])
    n_exp, tb = scores.shape
    sel = scores + bias_ref[...]
    gsz = n_exp // N_EXPERT_GROUPS
    member = lax.broadcasted_iota(I32, (gsz, tb), 0)
    gscore = []
    for grp in range(N_EXPERT_GROUPS):
        xg = sel[grp * gsz:(grp + 1) * gsz, :]
        m1 = jnp.max(xg, axis=0, keepdims=True)
        first = jnp.min(jnp.where(xg == m1, member, gsz), axis=0, keepdims=True)
        m2 = jnp.max(jnp.where(member == first, -jnp.inf, xg), axis=0, keepdims=True)
        gscore.append(m1 + m2)
    keep_rows = []
    for grp in range(N_EXPERT_GROUPS):
        beaten = jnp.zeros((1, tb), F32)
        for other in range(N_EXPERT_GROUPS):
            if other == grp:
                continue
            wins = (gscore[other] >= gscore[grp]) if other < grp else (gscore[other] > gscore[grp])
            beaten = beaten + jnp.where(wins, 1.0, 0.0)
        keep_rows.append(jnp.broadcast_to(beaten < TOPK_GROUPS, (gsz, tb)))
    masked = jnp.where(jnp.concatenate(keep_rows, axis=0), sel, -jnp.inf)

    expert = lax.broadcasted_iota(I32, (n_exp, tb), 0)
    picks, gates, hots = [], [], []
    chosen = jnp.zeros((n_exp, tb), F32)
    for _ in range(TOP_K):
        m = jnp.max(masked, axis=0, keepdims=True)
        pick = jnp.min(jnp.where(masked == m, expert, n_exp), axis=0, keepdims=True)
        hot = expert == pick
        picks.append(pick)
        hots.append(hot)
        gates.append(jnp.sum(jnp.where(hot, scores, 0.0), axis=0, keepdims=True))
        chosen = jnp.where(hot, 1.0, chosen)
        masked = jnp.where(hot, -jnp.inf, masked)
    total = gates[0]
    for gk in gates[1:]:
        total = total + gk
    ahead = jnp.dot(chosen.astype(BF16), tri_ref[...], preferred_element_type=F32) + run_ref[...]
    for k in range(TOP_K):
        idx_ref[k:k + 1, :] = picks[k]
        gate_ref[k:k + 1, :] = ROUTED_SCALE * gates[k] / total
        rank_ref[k:k + 1, :] = jnp.sum(jnp.where(hots[k], ahead, 0.0), axis=0, keepdims=True).astype(I32)
    run_ref[...] = run_ref[...] + jnp.sum(chosen, axis=1, keepdims=True)
    cnt_ref[...] = jnp.broadcast_to(run_ref[...], cnt_ref.shape)


def route(logits_t, router_bias):
    n_exp, t = logits_t.shape
    tb = min(512, t)
    tri = jnp.asarray(np.triu(np.ones((tb, tb), np.float32), k=1), BF16)
    tok = lambda i: (0, i)
    idx, gate, rank, cnt = pl.pallas_call(
        _route_kernel,
        grid=(t // tb,),
        in_specs=[pl.BlockSpec((n_exp, tb), tok), pl.BlockSpec((n_exp, 1), lambda i: (0, 0)),
                  pl.BlockSpec((tb, tb), lambda i: (0, 0))],
        out_specs=[pl.BlockSpec((TOP_K, tb), tok)] * 3 + [pl.BlockSpec((n_exp, 128), lambda i: (0, 0))],
        out_shape=[jax.ShapeDtypeStruct((TOP_K, t), I32), jax.ShapeDtypeStruct((TOP_K, t), F32),
                   jax.ShapeDtypeStruct((TOP_K, t), I32), jax.ShapeDtypeStruct((n_exp, 128), F32)],
        scratch_shapes=[pltpu.VMEM((n_exp, 1), F32)],
        compiler_params=_params("arbitrary"),
    )(logits_t, router_bias.astype(F32).reshape(n_exp, 1), tri)
    return idx, gate, rank, cnt[:, 0].astype(I32)


def dispatch_plan(idx, rank, counts, n_blocks):
    n_exp = counts.shape[0]
    padded = (counts + EXPERT_ROWS - 1) // EXPERT_ROWS * EXPERT_ROWS
    pad_end = jnp.cumsum(padded)
    pad_start = pad_end - padded
    experts_iota = jnp.arange(n_exp, dtype=I32)
    dest = jnp.sum(jnp.where(idx[..., None] == experts_iota, pad_start, 0), axis=-1) + rank
    n_valid = (pad_end[-1] // EXPERT_ROWS).astype(I32)
    blk = jnp.minimum(jnp.arange(n_blocks, dtype=I32), n_valid - 1)
    blk_e = jnp.sum((pad_end[None, :] <= blk[:, None] * EXPERT_ROWS).astype(I32), axis=1)
    return dest.astype(I32), jnp.minimum(blk_e, n_exp - 1).astype(I32), n_valid.reshape(1)


def _sc_mesh():
    return plsc.VectorSubcoreMesh(core_axis_name="core", subcore_axis_name="subcore")


def dispatch(f, dest, cap):
    t, w = f.shape
    n_choice = dest.shape[0]

    @functools.partial(pl.kernel, out_type=jax.ShapeDtypeStruct((cap, w), f.dtype), mesh=_sc_mesh(),
                       scratch_types=[])
    def scatter_rows(x_hbm, i_hbm, o_hbm):
        def body(x_vmem, i_vmem):
            pltpu.sync_copy(x_vmem, o_hbm.at[i_vmem.at[0]])

        pltpu.emit_pipeline(
            body,
            grid=(t // SC_WINDOW, n_choice),
            in_specs=[pl.BlockSpec((SC_WINDOW, w), lambda i, k: (i, 0)),
                      pl.BlockSpec((1, SC_WINDOW), lambda i, k: (k, i))],
            out_specs=[],
            core_axis_name=("core", "subcore"),
            dimension_semantics=(pltpu.PARALLEL, pltpu.ARBITRARY),
        )(x_hbm, i_hbm)

    return scatter_rows(f, dest)


def gather_rows(rows, index_row):
    n = index_row.shape[1]
    w = rows.shape[1]

    @functools.partial(pl.kernel, out_type=jax.ShapeDtypeStruct((n, w), rows.dtype), mesh=_sc_mesh(),
                       scratch_types=[])
    def gather(y_hbm, i_hbm, o_hbm):
        def body(i_vmem, o_vmem):
            pltpu.sync_copy(y_hbm.at[i_vmem.at[0]], o_vmem)

        pltpu.emit_pipeline(
            body,
            grid=(n // SC_WINDOW,),
            in_specs=[pl.BlockSpec((1, SC_WINDOW), lambda i: (0, i))],
            out_specs=[pl.BlockSpec((SC_WINDOW, w), lambda i: (i, 0))],
            core_axis_name=("core", "subcore"),
            dimension_semantics=(pltpu.PARALLEL,),
        )(i_hbm, o_hbm)

    return gather(rows, index_row)


def _experts_kernel(be_ref, xa_ref, xb_ref, wg_ref, wu_ref, wd_ref, ya_ref, yb_ref, wg_bf, wu_bf, wd_bf):
    b = pl.program_id(0)

    @pl.when((b == 0) | (be_ref[b] != be_ref[jnp.maximum(b - 1, 0)]))
    def _():
        wg_bf[...] = wg_ref[0, 0].astype(BF16)
        wu_bf[...] = wu_ref[0, 0].astype(BF16)
        wd_bf[...] = wd_ref[0, 0].astype(BF16)

    x = _unpack_rows(jnp.concatenate([xa_ref[...], xb_ref[...]], axis=1)).astype(BF16)
    hid = (_silu(jnp.dot(x, wg_bf[...], preferred_element_type=F32))
           * jnp.dot(x, wu_bf[...], preferred_element_type=F32))
    packed = _pack_rows(jnp.dot(hid.astype(BF16), wd_bf[...], preferred_element_type=F32))
    half = packed.shape[1] // 2
    ya_ref[...] = packed[:, :half]
    yb_ref[...] = packed[:, half:]


def experts(rows_a, rows_b, blk_e, n_valid, layer, w_gate, w_up, w_down):
    cap, dq = rows_a.shape
    _, _, d, ff = w_gate.shape
    blk = pl.BlockSpec((EXPERT_ROWS, dq), lambda b, be: (b, 0))
    out = jax.ShapeDtypeStruct((cap, dq), U32)
    return pl.pallas_call(
        _experts_kernel,
        grid_spec=pltpu.PrefetchScalarGridSpec(
            num_scalar_prefetch=1,
            grid=(n_valid[0],),
            in_specs=[blk, blk,
                      pl.BlockSpec((1, 1, d, ff), lambda b, be: (layer, be[b], 0, 0)),
                      pl.BlockSpec((1, 1, d, ff), lambda b, be: (layer, be[b], 0, 0)),
                      pl.BlockSpec((1, 1, ff, d), lambda b, be: (layer, be[b], 0, 0))],
            out_specs=[blk, blk],
            scratch_shapes=[pltpu.VMEM((d, ff), BF16), pltpu.VMEM((d, ff), BF16), pltpu.VMEM((ff, d), BF16)],
        ),
        out_shape=[out, out],
        compiler_params=_params("arbitrary"),
    )(blk_e, rows_a, rows_b, w_gate, w_up, w_down)


def _combine_kernel(gate_ref, h_ref, shared_ref, gf_ref, sh_ref, sc_ref, *rest):
    ya_refs, yb_refs, outs = rest[:TOP_K], rest[TOP_K:2 * TOP_K], rest[2 * TOP_K:]
    routed = None
    for k in range(TOP_K):
        y = _unpack_rows(jnp.concatenate([ya_refs[k][0], yb_refs[k][0]], axis=1))
        routed = gate_ref[:, k:k + 1] * y if routed is None else routed + gate_ref[:, k:k + 1] * y
    h = h_ref[...] + gf_ref[...] * (routed + shared_ref[...].astype(F32))
    outs[0][...] = h
    if len(outs) > 1:
        outs[1][...] = _rms(h) * (1.0 + sc_ref[...]) + sh_ref[...]


def combine(y_a, y_b, dest, gate_tk, h, shared, gate_ffn, next_shift, next_scale, with_next):
    t, d = h.shape
    n_out = 2 if with_next else 1
    n_choice = dest.shape[0]
    flat = dest.reshape(1, n_choice * t)
    picked_a = gather_rows(y_a, flat).reshape(n_choice, t, d // 4)
    picked_b = gather_rows(y_b, flat).reshape(n_choice, t, d // 4)
    tm = min(256, t)
    row = lambda i: (i, 0)
    fixed = lambda i: (0, 0)
    choice = [pl.BlockSpec((1, tm, d // 4), functools.partial(lambda i, k: (k, i, 0), k=k))
              for k in range(n_choice)] * 2
    return pl.pallas_call(
        _combine_kernel,
        grid=(t // tm,),
        in_specs=[pl.BlockSpec((tm, n_choice), row), pl.BlockSpec((tm, d), row), pl.BlockSpec((tm, d), row),
                  pl.BlockSpec((1, d), fixed), pl.BlockSpec((1, d), fixed), pl.BlockSpec((1, d), fixed)] + choice,
        out_specs=[pl.BlockSpec((tm, d), row)] * n_out,
        out_shape=[jax.ShapeDtypeStruct((t, d), F32)] * n_out,
        compiler_params=_params("parallel"),
    )(gate_tk, h, shared, gate_ffn, next_shift, next_scale, *([picked_a] * n_choice), *([picked_b] * n_choice))


def moe_tail(h, f_a, f_b, logits_t, shared, router_bias, layer, w_gate, w_up, w_down, gate_ffn,
             next_shift, next_scale, with_next):
    t, d = h.shape
    n_exp = w_gate.shape[1]
    n_blocks = -(-(t * TOP_K + n_exp * (EXPERT_ROWS - 1)) // EXPERT_ROWS)
    cap = n_blocks * EXPERT_ROWS
    idx, gate, rank, counts = route(logits_t, router_bias)
    dest, blk_e, n_valid = dispatch_plan(idx, rank, counts, n_blocks)
    rows_a, rows_b = dispatch(f_a, dest, cap), dispatch(f_b, dest, cap)
    y_a, y_b = experts(rows_a, rows_b, blk_e, n_valid, layer, w_gate, w_up, w_down)
    return combine(y_a, y_b, dest, gate.T, h, shared, gate_ffn, next_shift, next_scale, with_next)


def kernel(x, c, ctx, c_ctx, w_ada, b_ada, w_in, s5_lam_re, s5_lam_im, s5_log_dt, s5_b_re, s5_b_im,
           s5_c_re, s5_c_im, s5_d, s5_w_glu, s5_b_glu, na_q_gain, na_k_gain, na_rpb, w_mix_out,
           w_fourier_out, b_fourier_out, w_router, router_bias, w_exp_gate, w_exp_up, w_exp_down,
           w_sh_gate, w_sh_up, w_sh_down):
    bsz, t, d = x.shape
    assert bsz == 1 and w_ada.shape[0] == 2
    n_exp = w_router.shape[2]
    s5w = s5_w_glu.shape[1]
    naw = w_in.shape[2] - s5w
    naw //= 3
    heads = naw // NA_HEAD_DIM

    cond8 = jnp.concatenate([c[:1].astype(F32), c_ctx.astype(F32)[None], jnp.zeros((6, d), F32)], axis=0)
    ada = adaln_all(cond8, w_ada, b_ada)
    mod = lambda layer, who, j: ada[layer, who:who + 1, j * d:(j + 1) * d]

    def ffn_weights(i):
        return (mod(i, 0, 3), mod(i, 0, 4), jnp.transpose(w_router[i]).astype(F32),
                w_sh_gate[i].astype(BF16), w_sh_up[i].astype(BF16), w_sh_down[i].astype(BF16))

    h0 = x[0]
    seg = jnp.asarray(np.kron(np.eye(heads), np.ones((NA_HEAD_DIM, NA_HEAD_DIM))), BF16)
    w_in_b = w_in[0].astype(BF16)
    qg = jnp.tile(na_q_gain[0].astype(F32), heads)[None]
    kg = jnp.tile(na_k_gain[0].astype(F32), heads)[None]
    u_c, _, k_c, v_c = in_projection(ctx[0], mod(0, 1, 0), mod(0, 1, 1), w_in_b, seg, qg, kg, s5w, naw)
    u_l, q_l, k_l, v_l = in_projection(h0, mod(0, 0, 0), mod(0, 0, 1), w_in_b, seg, qg, kg, s5w, naw)
    mats = s5_matrices(s5_lam_re[0], s5_lam_im[0], s5_log_dt[0], s5_b_re[0], s5_b_im[0],
                       s5_c_re[0], s5_c_im[0], s5_d[0])
    y_s5 = s5_mixer(u_c, u_l, mats)
    na = neighbourhood_attention(q_l, k_l, v_l, k_c, v_c, na_bias_table(na_rpb[0]))
    h1, f1a, f1b, lg1, sh1 = _post_call(
        _even_post_kernel, [y_s5, na, h0],
        [s5_w_glu[0].astype(BF16), s5_b_glu[0].astype(F32)[None], w_mix_out[0].astype(BF16), mod(0, 0, 2),
         *ffn_weights(0)], t, d, n_exp)
    h2, a1 = moe_tail(h1, f1a, f1b, lg1, sh1, router_bias[0], 0, w_exp_gate, w_exp_up, w_exp_down,
                      mod(0, 0, 5), mod(1, 0, 0), mod(1, 0, 1), True)

    zre, zim = time_dft(a1)
    cc, sc = channel_dft_tables(d // FOURIER_GROUPS)
    h3, f3a, f3b, lg3, sh3 = _post_call(
        _odd_post_kernel, [zre, zim, h2],
        [cc, sc, w_fourier_out[0].astype(BF16), b_fourier_out[0].astype(F32)[None], mod(1, 0, 2),
         *ffn_weights(1)], t, d, n_exp)
    zero_row = jnp.zeros((1, d), F32)
    (out,) = moe_tail(h3, f3a, f3b, lg3, sh3, router_bias[1], 1, w_exp_gate, w_exp_up, w_exp_down,
                      mod(1, 0, 5), zero_row, zero_row, False)
    return out[None]
```

```python
import functools
import math

import numpy as np
import jax
import jax.numpy as jnp
from jax import lax
from jax.experimental import pallas as pl
from jax.experimental.pallas import tpu as pltpu
from jax.experimental.pallas import tpu_sc as plsc

F32 = jnp.float32
BF16 = jnp.bfloat16
I32 = jnp.int32
U32 = jnp.uint32
HIGHEST = lax.Precision.HIGHEST

MXU_DEPTH = 256
LANES = 128
GRID_W = 64
NORM_EPS = 1e-6
S5_GROUP = 16
S5_STATE = 64
S5_LAMBDA_RE_MAX = -1e-4
S5_CHUNK = 16
S5_TILE = 128
NA_HEADS = 8
NA_HEAD_DIM = 64
NA_KH = 8
NA_KW = 16
FOURIER_GROUPS = 4
N_EXPERT_GROUPS = 8
TOPK_GROUPS = 4
TOP_K = 8
ROUTED_SCALE = 2.5
EXPERT_ROWS = 1024
ROW_PARTS = 2
SC_WINDOW = 128
NEG_BIG = -1e30

VMEM_LIMIT_BYTES = 56 * 1024 * 1024


def _params(*sem):
    return pltpu.CompilerParams(dimension_semantics=sem or None,
                                vmem_limit_bytes=VMEM_LIMIT_BYTES)


def _rms(x):
    return x * lax.rsqrt(jnp.mean(x * x, axis=-1, keepdims=True) + NORM_EPS)


def _silu(x):
    return x * jax.nn.sigmoid(x)


def _pack_pair(lo, hi):
    lo = lax.bitcast_convert_type(lo.astype(BF16).astype(F32), U32)
    hi = lax.bitcast_convert_type(hi.astype(BF16).astype(F32), U32)
    return (hi & jnp.uint32(0xFFFF0000)) | (lo >> 16)


def _unpack_pair(w):
    return (lax.bitcast_convert_type(w << 16, F32), lax.bitcast_convert_type(w & jnp.uint32(0xFFFF0000), F32))


def _pack_rows(x):
    n = x.shape[1] // 2
    return _pack_pair(x[:, :n], x[:, n:])


def _unpack_rows(w):
    return jnp.concatenate(_unpack_pair(w), axis=1)


def _ada_kernel(c_ref, w_ref, b_ref, o_ref):
    o_ref[0] = jnp.dot(_silu(c_ref[...]), w_ref[0], preferred_element_type=F32,
                       precision=HIGHEST) + b_ref[0]


def adaln_all(cond8, w_ada, b_ada):
    n_layers, d, n6 = w_ada.shape
    tn = n6 // 4
    return pl.pallas_call(
        _ada_kernel,
        grid=(n_layers, n6 // tn),
        in_specs=[pl.BlockSpec((8, d), lambda l, j: (0, 0)),
                  pl.BlockSpec((1, d, tn), lambda l, j: (l, 0, j)),
                  pl.BlockSpec((1, 1, tn), lambda l, j: (l, 0, j))],
        out_specs=pl.BlockSpec((1, 8, tn), lambda l, j: (l, 0, j)),
        out_shape=jax.ShapeDtypeStruct((n_layers, 8, n6), F32),
        compiler_params=_params("parallel", "parallel"),
    )(cond8, w_ada, b_ada.reshape(n_layers, 1, n6))


def _inproj_kernel(x_ref, sh_ref, sc_ref, w_ref, seg_ref, qg_ref, kg_ref,
                   u_ref, q_ref, k_ref, v_ref):
    a = _rms(x_ref[...]) * (1.0 + sc_ref[...]) + sh_ref[...]
    z = jnp.dot(a.astype(BF16), w_ref[...], preferred_element_type=F32)
    s5w = u_ref.shape[1]
    naw = q_ref.shape[1]

    def head_norm(t, gain):
        ss = jnp.dot((t * t).astype(BF16), seg_ref[...], preferred_element_type=F32)
        return t * lax.rsqrt(ss * (1.0 / NA_HEAD_DIM) + NORM_EPS) * gain

    u_ref[...] = z[:, :s5w]
    q_ref[...] = head_norm(z[:, s5w:s5w + naw], qg_ref[...]).astype(BF16)
    k_ref[...] = head_norm(z[:, s5w + naw:s5w + 2 * naw], kg_ref[...]).astype(BF16)
    v_ref[...] = z[:, s5w + 2 * naw:].astype(BF16)


def in_projection(x, shift, scale, w_in_bf16, seg_ones, q_gain_row, k_gain_row, s5w, naw):
    t, d = x.shape
    tm = min(512, t)
    row = lambda i: (i, 0)
    fixed = lambda i: (0, 0)
    return pl.pallas_call(
        _inproj_kernel,
        grid=(t // tm,),
        in_specs=[pl.BlockSpec((tm, d), row),
                  pl.BlockSpec((1, d), fixed), pl.BlockSpec((1, d), fixed),
                  pl.BlockSpec(w_in_bf16.shape, fixed),
                  pl.BlockSpec(seg_ones.shape, fixed),
                  pl.BlockSpec((1, naw), fixed), pl.BlockSpec((1, naw), fixed)],
        out_specs=[pl.BlockSpec((tm, s5w), row), pl.BlockSpec((tm, naw), row),
                   pl.BlockSpec((tm, naw), row), pl.BlockSpec((tm, naw), row)],
        out_shape=[jax.ShapeDtypeStruct((t, s5w), F32)] + [jax.ShapeDtypeStruct((t, naw), BF16)] * 3,
        compiler_params=_params("parallel"),
    )(x, shift, scale, w_in_bf16, seg_ones, q_gain_row, k_gain_row)


def s5_matrices(lam_re, lam_im, log_dt, b_re, b_im, c_re, c_im, d_skip):
    L = S5_CHUNK
    taus = jnp.arange(L + 1, dtype=F32)

    def direction(i):
        lam = lax.complex(jnp.minimum(lam_re[i].astype(F32), S5_LAMBDA_RE_MAX), lam_im[i].astype(F32))
        ldt = lam * jnp.exp(log_dt[i].astype(F32))[:, None]
        lam_bar = jnp.exp(ldt)
        b_bar = ((lam_bar - 1.0) / lam)[..., None] * lax.complex(b_re[i].astype(F32), b_im[i].astype(F32))
        cc = lax.complex(c_re[i].astype(F32), c_im[i].astype(F32))
        powers = jnp.exp(ldt[None] * taus[:, None, None])
        resp = jnp.real(jnp.einsum('gcp,tgp,gpd->gtcd', cc, powers[:L], b_bar, precision=HIGHEST))
        return powers, b_bar, cc, resp

    pw_f, bb_f, cc_f, k_f = direction(0)
    pw_b, bb_b, cc_b, k_b = direction(1)
    g, p = pw_f.shape[1:]
    c = S5_GROUP
    diff = np.arange(L)[None, :] - np.arange(L)[:, None]
    lag = np.arange(L)[:, None, None]
    tf = jnp.einsum('tsl,gtcd->gslcd', jnp.asarray(lag == diff[None], F32), k_f, precision=HIGHEST)
    tb = jnp.einsum('tsl,gtcd->gslcd', jnp.asarray(lag == -diff[None], F32), k_b, precision=HIGHEST)
    skip = (jnp.eye(L, dtype=F32)[None, :, :, None, None]
            * (jnp.eye(c, dtype=F32)[None] * d_skip.astype(F32)[:, :, None])[:, None, None])
    toep = jnp.transpose(tf + tb + skip, (0, 1, 4, 2, 3)).reshape(g, L * c, L * c)

    def state_in(powers_sel, b_bar):
        w = powers_sel[:, :, :, None] * b_bar[None]
        return jnp.transpose(w, (1, 0, 3, 2)).reshape(g, L * c, p)

    wf = state_in(pw_f[L - 1 - jnp.arange(L)], bb_f)
    wb = state_in(pw_b[jnp.arange(L)], bb_b)
    w_state = jnp.concatenate([jnp.real(wf), jnp.imag(wf), jnp.imag(wf), jnp.real(wf),
                               jnp.real(wb), jnp.imag(wb), jnp.imag(wb), jnp.real(wb)], axis=-1)

    def state_out(powers_sel, cc):
        r = powers_sel[:, :, None, :] * cc[None]
        return jnp.transpose(r, (1, 3, 0, 2)).reshape(g, p, L * c)

    rf = state_out(pw_f[1 + jnp.arange(L)], cc_f)
    rb = state_out(pw_b[L - jnp.arange(L)], cc_b)
    r_state = jnp.concatenate([jnp.real(rf), -jnp.imag(rf), jnp.real(rb), -jnp.imag(rb)], axis=1)

    def mult(a):
        ar, ai = jnp.real(a), jnp.imag(a)
        return jnp.stack([jnp.concatenate([ar, ar], -1), jnp.concatenate([-ai, ai], -1),
                          jnp.concatenate([ai, -ai], -1)])

    return toep, w_state, r_state, mult(pw_f[L]), mult(pw_b[L])


def _s5_pack_kernel(*refs):
    u_refs, (wt_ref, ut_ref, f1_ref, f2_ref, b1_ref, b2_ref) = refs[:-6], refs[-6:]
    L = S5_CHUNK
    g, lc, nck = ut_ref.shape
    c = lc // L
    gs = g // len(u_refs)
    for s in range(L):
        for j, u_ref in enumerate(u_refs):
            step_s = u_ref[pl.ds(s, nck, stride=L), :]
            ut_ref[j * gs:(j + 1) * gs, s * c:(s + 1) * c, :] = (
                jnp.transpose(step_s).astype(BF16).reshape(gs, c, nck))
    n = f1_ref.shape[2]
    for gi in range(g):
        inc = jnp.dot(wt_ref[gi], ut_ref[gi], preferred_element_type=F32)
        for j, ref in enumerate((f1_ref, f2_ref, b1_ref, b2_ref)):
            ref[:, gi, :] = jnp.transpose(inc[j * n:(j + 1) * n, :])


def s5_pack(u, wt_state):
    t, w = u.shape
    g, n4, lc = wt_state.shape
    nc = t // S5_CHUNK
    tile = min(S5_TILE, nc)
    inc = jax.ShapeDtypeStruct((nc, g, n4 // 4), F32)
    inc_blk = pl.BlockSpec((tile, g, n4 // 4), lambda i: (i, 0, 0))
    return pl.pallas_call(
        _s5_pack_kernel,
        grid=(nc // tile,),
        in_specs=[pl.BlockSpec((tile * S5_CHUNK, LANES), functools.partial(lambda i, j: (i, j), j=j))
                  for j in range(w // LANES)] + [pl.BlockSpec(wt_state.shape, lambda i: (0, 0, 0))],
        out_specs=[pl.BlockSpec((g, lc, tile), lambda i: (0, 0, i))] + [inc_blk] * 4,
        out_shape=[jax.ShapeDtypeStruct((g, lc, nc), BF16)] + [inc] * 4,
        compiler_params=_params("parallel"),
    )(*([u] * (w // LANES)), wt_state)


def _s5_scan_kernel(s1_ref, s2_ref, m_ref, init_ref, x_ref, last_ref, v1_ref, v2_ref, *, reverse):
    @pl.when(pl.program_id(0) == 0)
    def _():
        v1_ref[...] = init_ref[0]
        v2_ref[...] = init_ref[1]

    a1, a2, a3 = m_ref[0], m_ref[1], m_ref[2]
    cb = s1_ref.shape[0]

    def body(j, carry):
        v1, v2 = carry
        jj = cb - 1 - j if reverse else j
        x_ref[jj] = v1
        return (a1 * v1 + a2 * v2 + s1_ref[jj], a1 * v2 + a3 * v1 + s2_ref[jj])

    v1, v2 = lax.fori_loop(0, cb, body, (v1_ref[...], v2_ref[...]))
    v1_ref[...] = v1
    v2_ref[...] = v2
    last_ref[...] = v1


def s5_chunk_scan(s1, s2, mult, init, reverse):
    nc, g, n = s1.shape
    cb = min(S5_TILE, nc)
    nb = nc // cb
    blk = (lambda i: (nb - 1 - i, 0, 0)) if reverse else (lambda i: (i, 0, 0))
    return pl.pallas_call(
        functools.partial(_s5_scan_kernel, reverse=reverse),
        grid=(nb,),
        in_specs=[pl.BlockSpec((cb, g, n), blk), pl.BlockSpec((cb, g, n), blk),
                  pl.BlockSpec((3, g, n), lambda i: (0, 0, 0)), pl.BlockSpec((2, g, n), lambda i: (0, 0, 0))],
        out_specs=[pl.BlockSpec((cb, g, n), blk), pl.BlockSpec((g, n), lambda i: (0, 0))],
        out_shape=[jax.ShapeDtypeStruct((nc, g, n), F32), jax.ShapeDtypeStruct((g, n), F32)],
        scratch_shapes=[pltpu.VMEM((g, n), F32), pltpu.VMEM((g, n), F32)],
        compiler_params=_params("arbitrary"),
    )(s1, s2, mult, init)


def _s5_readout_kernel(ut_ref, tt_ref, rt_ref, xf_ref, xb_ref, y_ref, yt_ref, *slab_refs):
    L = S5_CHUNK
    g, lc, nck = ut_ref.shape
    c = lc // L
    gs = g // len(slab_refs)
    for gi in range(g):
        xin_t = jnp.concatenate([jnp.transpose(xf_ref[:, gi, :]), jnp.transpose(xb_ref[:, gi, :])], axis=0)
        yt_ref[gi] = (jnp.dot(tt_ref[gi], ut_ref[gi], preferred_element_type=F32)
                      + jnp.dot(rt_ref[gi], xin_t.astype(BF16), preferred_element_type=F32))
    for j, slab in enumerate(slab_refs):
        for l in range(L):
            step_l = yt_ref[j * gs:(j + 1) * gs, l * c:(l + 1) * c, :].reshape(gs * c, nck)
            slab[pl.ds(l, nck, stride=L), :] = jnp.transpose(step_l)
        y_ref[:, j * gs * c:(j + 1) * gs * c] = slab[...]


def s5_readout(ut, toep_t, r_state_t, xin_f, xin_b):
    g, lc, nc = ut.shape
    n = xin_f.shape[2]
    tile = min(S5_TILE, nc)
    fixed = lambda i: (0, 0, 0)
    state_blk = pl.BlockSpec((tile, g, n), lambda i: (i, 0, 0))
    return pl.pallas_call(
        _s5_readout_kernel,
        grid=(nc // tile,),
        in_specs=[pl.BlockSpec((g, lc, tile), lambda i: (0, 0, i)),
                  pl.BlockSpec(toep_t.shape, fixed), pl.BlockSpec(r_state_t.shape, fixed), state_blk, state_blk],
        out_specs=pl.BlockSpec((tile * S5_CHUNK, g * lc // S5_CHUNK), lambda i: (i, 0)),
        out_shape=jax.ShapeDtypeStruct((nc * S5_CHUNK, g * lc // S5_CHUNK), F32),
        scratch_shapes=[pltpu.VMEM((g, lc, tile), F32)]
        + [pltpu.VMEM((tile * S5_CHUNK, LANES), F32)] * (g * lc // S5_CHUNK // LANES),
        compiler_params=_params("parallel"),
    )(ut, toep_t, r_state_t, xin_f, xin_b)


def s5_mixer(u_ctx, u_lat, mats):
    toep, w_state, r_state, mult_f, mult_b = mats
    L = S5_CHUNK
    g, n = mult_f.shape[1:]
    swap = lambda a: jnp.transpose(a, (0, 2, 1)).astype(BF16)
    wt_state, toep_t, r_state_t = swap(w_state), swap(toep), swap(r_state)
    halves = lambda v: jnp.stack([v, jnp.roll(v, n // 2, axis=-1)])
    n_ctx = u_ctx.shape[0] // L
    ctx_chunks = -(-(n_ctx + 1) // S5_TILE) * S5_TILE
    ctx_pad = jnp.pad(u_ctx, ((0, ctx_chunks * L - u_ctx.shape[0]), (0, 0)))
    _, cf1, cf2, cb1, cb2 = s5_pack(ctx_pad, wt_state)
    zero = jnp.zeros((2, g, n), F32)
    ctx_f, _ = s5_chunk_scan(cf1, cf2, mult_f, zero, False)
    _, ctx_b_last = s5_chunk_scan(cb1, cb2, mult_b, zero, True)
    ut, f1, f2, b1, b2 = s5_pack(u_lat, wt_state)
    xin_f, _ = s5_chunk_scan(f1, f2, mult_f, halves(ctx_f[n_ctx]), False)
    xin_b, _ = s5_chunk_scan(b1, b2, mult_b, halves(ctx_b_last), True)
    return s5_readout(ut, toep_t, r_state_t, xin_f, xin_b)


def na_bias_table(rpb):
    q_col = np.arange(GRID_W)
    col_start = np.clip(q_col - NA_KW // 2, 0, GRID_W - NA_KW)
    key_col = np.arange(GRID_W)
    off = key_col[None, :] - col_start[:, None]
    valid = (off >= 0) & (off < NA_KW)
    rel_col = np.clip(key_col[None, :] - q_col[:, None] + NA_KW - 1, 0, 2 * NA_KW - 2)
    pick_col = jnp.asarray(np.arange(2 * NA_KW - 1)[:, None, None] == rel_col[None], F32)
    rel_row = np.arange(NA_KH)[None, :] - np.arange(NA_KH)[:, None] + NA_KH - 1
    pick_row = jnp.asarray(np.arange(2 * NA_KH - 1)[:, None, None] == rel_row[None], F32)
    table = jnp.einsum('hrj,rdi,jck->dhcik', rpb.astype(F32), pick_row, pick_col, precision=HIGHEST)
    table = table.reshape(NA_KH, rpb.shape[0], GRID_W, NA_KH * GRID_W)
    return jnp.where(np.tile(valid, (1, NA_KH))[None, None], table, NEG_BIG)


def _na_kernel(*refs, scale):
    q_ref = refs[0]
    k_refs = refs[1:1 + NA_KH]
    v_refs = refs[1 + NA_KH:1 + 2 * NA_KH]
    kc_ref, vc_ref, b_ref, o_ref = refs[1 + 2 * NA_KH:]
    hd = NA_HEAD_DIM
    width = q_ref.shape[1]
    span = min(MXU_DEPTH, width)
    nt = (((1,), (1,)), ((), ()))
    q = q_ref[...] * scale
    kk = jnp.concatenate([r[...] for r in k_refs], axis=0)
    vv = jnp.concatenate([r[...] for r in v_refs], axis=0)
    kc, vc = kc_ref[...], vc_ref[...]
    nq = q.shape[0]
    per = span // hd
    lane = lax.broadcasted_iota(I32, (nq, span), 1)
    own = [(lane >= j * hd) & (lane < (j + 1) * hd) for j in range(per)]
    out_cols = []
    for c0 in range(0, width, span):
        cols = slice(c0, c0 + span)
        qs = jnp.concatenate([jnp.where(own[j], q[:, cols], jnp.zeros_like(q[:, cols])) for j in range(per)], axis=0)
        h0 = c0 // hd
        bias = b_ref[0, h0:h0 + per].reshape(per * nq, kk.shape[0])
        s = lax.dot_general(qs, kk[:, cols], nt, preferred_element_type=F32) + bias
        sc = lax.dot_general(qs, kc[:, cols], nt, preferred_element_type=F32)
        m = jnp.maximum(jnp.max(s, axis=-1, keepdims=True), jnp.max(sc, axis=-1, keepdims=True))
        p = jnp.exp(s - m)
        pc = jnp.exp(sc - m)
        den = jnp.sum(p, axis=-1, keepdims=True) + jnp.sum(pc, axis=-1, keepdims=True)
        o = (jnp.dot(p.astype(BF16), vv[:, cols], preferred_element_type=F32)
             + jnp.dot(pc.astype(BF16), vc[:, cols], preferred_element_type=F32)) / den
        acc = jnp.zeros((nq, span), F32)
        for j in range(per):
            acc = jnp.where(own[j], o[j * nq:(j + 1) * nq], acc)
        out_cols.append(acc)
    o_ref[...] = jnp.concatenate(out_cols, axis=1).astype(o_ref.dtype)


def neighbourhood_attention(q, k, v, k_ctx, v_ctx, bias_table):
    t, w = q.shape
    rows = t // GRID_W
    kh = min(NA_KH, rows)
    assert kh == NA_KH
    first = lambda r: jnp.clip(r - kh // 2, 0, rows - kh)
    row_blk = pl.BlockSpec((GRID_W, w), lambda r: (r, 0))
    key_blks = [pl.BlockSpec((GRID_W, w), functools.partial(lambda r, i: (first(r) + i, 0), i=i))
                for i in range(kh)]
    ctx_blk = pl.BlockSpec(k_ctx.shape, lambda r: (0, 0))
    bias_blk = pl.BlockSpec((1,) + bias_table.shape[1:], lambda r: (r - first(r), 0, 0, 0))
    return pl.pallas_call(
        functools.partial(_na_kernel, scale=NA_HEAD_DIM ** -0.5),
        grid=(rows,),
        in_specs=[row_blk] + key_blks + key_blks + [ctx_blk, ctx_blk, bias_blk],
        out_specs=row_blk,
        out_shape=jax.ShapeDtypeStruct((t, w), BF16),
        compiler_params=_params("parallel"),
    )(q, *([k] * kh), *([v] * kh), k_ctx, v_ctx, bias_table)


def time_dft_tables(t):
    a_len = 1 << (int(math.log2(t)) // 2)
    b_len = t // a_len
    ka = np.arange(a_len)[:, None]
    tok = b_len * np.arange(a_len)[None, :]
    ang1 = -2.0 * np.pi * ((ka * (tok[None] + np.arange(b_len)[:, None, None])) % t) / t
    stage1 = np.concatenate([np.cos(ang1), np.sin(ang1)], axis=1) / math.sqrt(t)
    ang2 = 2.0 * np.pi * ((np.arange(b_len)[:, None] * np.arange(b_len)[None, :]) % b_len) / b_len
    c2, s2 = np.cos(ang2), np.sin(ang2)
    stage2 = np.block([[c2, s2], [-s2, c2]])
    return jnp.asarray(stage1, BF16), jnp.asarray(stage2, BF16), a_len, b_len


def _time_dft_kernel(x_ref, m_ref, w2_ref, zre_ref, zim_ref, y_ref, z_ref, *, a_len, b_len):
    i = pl.program_id(1)
    bb = m_ref.shape[0]

    def stage1(jb, carry):
        b = i * bb + jb
        xb = x_ref[pl.ds(b, a_len, stride=b_len), :]
        y = jnp.dot(m_ref[jb], xb.astype(BF16), preferred_element_type=F32)
        row = pl.multiple_of(b * a_len, a_len)
        y_ref[pl.ds(row, a_len), :] = _pack_pair(y[:a_len], y[a_len:])
        return carry

    lax.fori_loop(0, bb, stage1, 0, unroll=4)

    @pl.when(i == pl.num_programs(1) - 1)
    def _():
        def stage2(ka, carry):
            rows = pl.ds(ka, b_len, stride=a_len)
            yre, yim = _unpack_pair(y_ref[rows, :])
            y = jnp.concatenate([yre, yim], axis=0).astype(BF16)
            z = jnp.dot(w2_ref[...], y, preferred_element_type=F32)
            z_ref[rows, :] = _pack_pair(z[:b_len], z[b_len:])
            return carry

        lax.fori_loop(0, a_len, stage2, 0, unroll=4)
        zre, zim = _unpack_pair(z_ref[...])
        zre_ref[...] = zre.astype(zre_ref.dtype)
        zim_ref[...] = zim.astype(zim_ref.dtype)


def time_dft(x):
    t, d = x.shape
    stage1, stage2, a_len, b_len = time_dft_tables(t)
    lanes = 128
    bb = min(16, b_len)
    out = jax.ShapeDtypeStruct((t, d), BF16)
    return pl.pallas_call(
        functools.partial(_time_dft_kernel, a_len=a_len, b_len=b_len),
        grid=(d // lanes, b_len // bb),
        in_specs=[pl.BlockSpec((t, lanes), lambda j, i: (0, j)),
                  pl.BlockSpec((bb, 2 * a_len, a_len), lambda j, i: (i, 0, 0)),
                  pl.BlockSpec(stage2.shape, lambda j, i: (0, 0))],
        out_specs=[pl.BlockSpec((t, lanes), lambda j, i: (0, j))] * 2,
        out_shape=[out, out],
        scratch_shapes=[pltpu.VMEM((t, lanes), U32), pltpu.VMEM((t, lanes), U32)],
        compiler_params=_params("parallel", "arbitrary"),
    )(x, stage1, stage2)


def channel_dft_tables(c):
    ang = 2.0 * np.pi * ((np.arange(c)[:, None] * np.arange(c)[None, :]) % c) / c
    return (jnp.asarray(np.cos(ang) / math.sqrt(c), BF16), jnp.asarray(np.sin(ang) / math.sqrt(c), BF16))


def _ffn_prologue(h, shf_ref, scf_ref, wr_ref, wsg_ref, wsu_ref, wsd_ref, h_ref, f_ref, lg_ref, shared_ref):
    h_ref[...] = h
    f = _rms(h) * (1.0 + scf_ref[...]) + shf_ref[...]
    packed = _pack_rows(f)
    half = packed.shape[1] // 2
    f_ref[0] = packed[:, :half]
    f_ref[1] = packed[:, half:]
    lg_ref[...] = lax.dot_general(wr_ref[...], f, (((1,), (1,)), ((), ())),
                                  preferred_element_type=F32, precision=HIGHEST)
    fb = f.astype(BF16)
    hid = (_silu(jnp.dot(fb, wsg_ref[...], preferred_element_type=F32))
           * jnp.dot(fb, wsu_ref[...], preferred_element_type=F32))
    shared_ref[...] = jnp.dot(hid.astype(BF16), wsd_ref[...], preferred_element_type=F32).astype(shared_ref.dtype)


def _gelu_tanh(x):
    return 0.5 * x * (1.0 + jnp.tanh(math.sqrt(2.0 / math.pi) * (x + 0.044715 * (x * x * x))))


def _even_post_kernel(y_ref, na_ref, x_ref, wglu_ref, bglu_ref, wo_ref, gm_ref, *rest):
    g = _gelu_tanh(y_ref[...])
    gate = jax.nn.sigmoid(jnp.dot(g.astype(BF16), wglu_ref[...], preferred_element_type=F32) + bglu_ref[...])
    s5 = (g * gate).astype(BF16)
    w = s5.shape[1]
    mix = (jnp.dot(s5, wo_ref[:w, :], preferred_element_type=F32)
           + jnp.dot(na_ref[...], wo_ref[w:, :], preferred_element_type=F32))
    _ffn_prologue(x_ref[...] + gm_ref[...] * mix, *rest)


def _odd_post_kernel(zre_ref, zim_ref, h_ref_in, cc_ref, sc_ref, wf_ref, bf_ref, gm_ref, *rest):
    c = cc_ref.shape[0]
    parts = []
    for grp in range(zre_ref.shape[1] // c):
        cols = slice(grp * c, (grp + 1) * c)
        parts.append(jnp.dot(zre_ref[:, cols], cc_ref[...], preferred_element_type=F32)
                     + jnp.dot(zim_ref[:, cols], sc_ref[...], preferred_element_type=F32))
    fr = jnp.concatenate(parts, axis=-1).astype(BF16)
    mix = jnp.dot(fr, wf_ref[...], preferred_element_type=F32) + bf_ref[...]
    _ffn_prologue(h_ref_in[...] + gm_ref[...] * mix, *rest)


def _post_call(body, row_inputs, fixed_inputs, t, d, n_exp):
    tm = min(512, t)
    row = lambda i: (i, 0)
    fixed = lambda i: (0, 0)
    in_specs = ([pl.BlockSpec((tm, a.shape[1]), row) for a in row_inputs]
                + [pl.BlockSpec(a.shape, fixed) for a in fixed_inputs])
    return pl.pallas_call(
        body,
        grid=(t // tm,),
        in_specs=in_specs,
        out_specs=[pl.BlockSpec((tm, d), row), pl.BlockSpec((ROW_PARTS, tm, d // 4), lambda i: (0, i, 0)),
                   pl.BlockSpec((n_exp, tm), lambda i: (0, i)), pl.BlockSpec((tm, d), row)],
        out_shape=[jax.ShapeDtypeStruct((t, d), F32), jax.ShapeDtypeStruct((ROW_PARTS, t, d // 4), U32),
                   jax.ShapeDtypeStruct((n_exp, t), F32), jax.ShapeDtypeStruct((t, d), BF16)],
        compiler_params=_params("parallel"),
    )(*row_inputs, *fixed_inputs)


def _route_kernel(lg_ref, bias_ref, tri_ref, idx_ref, gate_ref, rank_ref, cnt_ref, run_ref):
    @pl.when(pl.program_id(0) == 0)
    def _():
        run_ref[...] = jnp.zeros_like(run_ref)

    scores = jax.nn.sigmoid(lg_ref[...])
    n_exp, tb = scores.shape
    sel = scores + bias_ref[...]
    gsz = n_exp // N_EXPERT_GROUPS
    member = lax.broadcasted_iota(I32, (gsz, tb), 0)
    gscore = []
    for grp in range(N_EXPERT_GROUPS):
        xg = sel[grp * gsz:(grp + 1) * gsz, :]
        m1 = jnp.max(xg, axis=0, keepdims=True)
        first = jnp.min(jnp.where(xg == m1, member, gsz), axis=0, keepdims=True)
        m2 = jnp.max(jnp.where(member == first, -jnp.inf, xg), axis=0, keepdims=True)
        gscore.append(m1 + m2)
    keep_rows = []
    for grp in range(N_EXPERT_GROUPS):
        beaten = jnp.zeros((1, tb), F32)
        for other in range(N_EXPERT_GROUPS):
            if other == grp:
                continue
            wins = (gscore[other] >= gscore[grp]) if other < grp else (gscore[other] > gscore[grp])
            beaten = beaten + jnp.where(wins, 1.0, 0.0)
        keep_rows.append(jnp.broadcast_to(beaten < TOPK_GROUPS, (gsz, tb)))
    masked = jnp.where(jnp.concatenate(keep_rows, axis=0), sel, -jnp.inf)

    expert = lax.broadcasted_iota(I32, (n_exp, tb), 0)
    picks, gates, hots = [], [], []
    chosen = jnp.zeros((n_exp, tb), F32)
    for _ in range(TOP_K):
        m = jnp.max(masked, axis=0, keepdims=True)
        pick = jnp.min(jnp.where(masked == m, expert, n_exp), axis=0, keepdims=True)
        hot = expert == pick
        picks.append(pick)
        hots.append(hot)
        gates.append(jnp.sum(jnp.where(hot, scores, 0.0), axis=0, keepdims=True))
        chosen = jnp.where(hot, 1.0, chosen)
        masked = jnp.where(hot, -jnp.inf, masked)
    total = gates[0]
    for gk in gates[1:]:
        total = total + gk
    ahead = jnp.dot(chosen.astype(BF16), tri_ref[...], preferred_element_type=F32) + run_ref[...]
    for k in range(TOP_K):
        idx_ref[k:k + 1, :] = picks[k]
        gate_ref[k:k + 1, :] = ROUTED_SCALE * gates[k] / total
        rank_ref[k:k + 1, :] = jnp.sum(jnp.where(hots[k], ahead, 0.0), axis=0, keepdims=True).astype(I32)
    run_ref[...] = run_ref[...] + jnp.sum(chosen, axis=1, keepdims=True)
    cnt_ref[...] = jnp.broadcast_to(run_ref[...], cnt_ref.shape)


def route(logits_t, router_bias):
    n_exp, t = logits_t.shape
    tb = min(512, t)
    tri = jnp.asarray(np.triu(np.ones((tb, tb), np.float32), k=1), BF16)
    tok = lambda i: (0, i)
    idx, gate, rank, cnt = pl.pallas_call(
        _route_kernel,
        grid=(t // tb,),
        in_specs=[pl.BlockSpec((n_exp, tb), tok), pl.BlockSpec((n_exp, 1), lambda i: (0, 0)),
                  pl.BlockSpec((tb, tb), lambda i: (0, 0))],
        out_specs=[pl.BlockSpec((TOP_K, tb), tok)] * 3 + [pl.BlockSpec((n_exp, 128), lambda i: (0, 0))],
        out_shape=[jax.ShapeDtypeStruct((TOP_K, t), I32), jax.ShapeDtypeStruct((TOP_K, t), F32),
                   jax.ShapeDtypeStruct((TOP_K, t), I32), jax.ShapeDtypeStruct((n_exp, 128), F32)],
        scratch_shapes=[pltpu.VMEM((n_exp, 1), F32)],
        compiler_params=_params("arbitrary"),
    )(logits_t, router_bias.astype(F32).reshape(n_exp, 1), tri)
    return idx, gate, rank, cnt[:, 0].astype(I32)


def dispatch_plan(idx, rank, counts, n_blocks):
    n_exp = counts.shape[0]
    padded = (counts + EXPERT_ROWS - 1) // EXPERT_ROWS * EXPERT_ROWS
    pad_end = jnp.cumsum(padded)
    pad_start = pad_end - padded
    experts_iota = jnp.arange(n_exp, dtype=I32)
    dest = jnp.sum(jnp.where(idx[..., None] == experts_iota, pad_start, 0), axis=-1) + rank
    n_valid = (pad_end[-1] // EXPERT_ROWS).astype(I32)
    blk = jnp.minimum(jnp.arange(n_blocks, dtype=I32), n_valid - 1)
    blk_e = jnp.sum((pad_end[None, :] <= blk[:, None] * EXPERT_ROWS).astype(I32), axis=1)
    return dest.astype(I32), jnp.minimum(blk_e, n_exp - 1).astype(I32), n_valid.reshape(1)


def _sc_mesh():
    return plsc.VectorSubcoreMesh(core_axis_name="core", subcore_axis_name="subcore")


def dispatch(f, dest, cap):
    t, w = f.shape
    n_choice = dest.shape[0]

    @functools.partial(pl.kernel, out_type=jax.ShapeDtypeStruct((cap, w), f.dtype), mesh=_sc_mesh(),
                       scratch_types=[])
    def scatter_rows(x_hbm, i_hbm, o_hbm):
        def body(x_vmem, i_vmem):
            pltpu.sync_copy(x_vmem, o_hbm.at[i_vmem.at[0]])

        pltpu.emit_pipeline(
            body,
            grid=(t // SC_WINDOW, n_choice),
            in_specs=[pl.BlockSpec((SC_WINDOW, w), lambda i, k: (i, 0)),
                      pl.BlockSpec((1, SC_WINDOW), lambda i, k: (k, i))],
            out_specs=[],
            core_axis_name=("core", "subcore"),
            dimension_semantics=(pltpu.PARALLEL, pltpu.ARBITRARY),
        )(x_hbm, i_hbm)

    return scatter_rows(f, dest)


def gather_rows(rows, index_row):
    n = index_row.shape[1]
    w = rows.shape[1]

    @functools.partial(pl.kernel, out_type=jax.ShapeDtypeStruct((n, w), rows.dtype), mesh=_sc_mesh(),
                       scratch_types=[])
    def gather(y_hbm, i_hbm, o_hbm):
        def body(i_vmem, o_vmem):
            pltpu.sync_copy(y_hbm.at[i_vmem.at[0]], o_vmem)

        pltpu.emit_pipeline(
            body,
            grid=(n // SC_WINDOW,),
            in_specs=[pl.BlockSpec((1, SC_WINDOW), lambda i: (0, i))],
            out_specs=[pl.BlockSpec((SC_WINDOW, w), lambda i: (i, 0))],
            core_axis_name=("core", "subcore"),
            dimension_semantics=(pltpu.PARALLEL,),
        )(i_hbm, o_hbm)

    return gather(rows, index_row)


def _experts_kernel(be_ref, x_ref, wg_ref, wu_ref, wd_ref, y_ref, wg_bf, wu_bf, wd_bf):
    b = pl.program_id(0)

    @pl.when((b == 0) | (be_ref[b] != be_ref[jnp.maximum(b - 1, 0)]))
    def _():
        wg_bf[...] = wg_ref[0, 0].astype(BF16)
        wu_bf[...] = wu_ref[0, 0].astype(BF16)
        wd_bf[...] = wd_ref[0, 0].astype(BF16)

    x = _unpack_rows(jnp.concatenate([x_ref[i] for i in range(ROW_PARTS)], axis=1)).astype(BF16)
    hid = (_silu(jnp.dot(x, wg_bf[...], preferred_element_type=F32))
           * jnp.dot(x, wu_bf[...], preferred_element_type=F32))
    packed = _pack_rows(jnp.dot(hid.astype(BF16), wd_bf[...], preferred_element_type=F32))
    dq = packed.shape[1] // ROW_PARTS
    for i in range(ROW_PARTS):
        y_ref[i] = packed[:, i * dq:(i + 1) * dq]


def experts(rows, blk_e, n_valid, layer, w_gate, w_up, w_down):
    parts, cap, dq = rows.shape
    _, _, d, ff = w_gate.shape
    blk = pl.BlockSpec((parts, EXPERT_ROWS, dq), lambda b, be: (0, b, 0))
    return pl.pallas_call(
        _experts_kernel,
        grid_spec=pltpu.PrefetchScalarGridSpec(
            num_scalar_prefetch=1,
            grid=(n_valid[0],),
            in_specs=[blk,
                      pl.BlockSpec((1, 1, d, ff), lambda b, be: (layer, be[b], 0, 0)),
                      pl.BlockSpec((1, 1, d, ff), lambda b, be: (layer, be[b], 0, 0)),
                      pl.BlockSpec((1, 1, ff, d), lambda b, be: (layer, be[b], 0, 0))],
            out_specs=blk,
            scratch_shapes=[pltpu.VMEM((d, ff), BF16), pltpu.VMEM((d, ff), BF16), pltpu.VMEM((ff, d), BF16)],
        ),
        out_shape=jax.ShapeDtypeStruct(rows.shape, U32),
        compiler_params=_params("arbitrary"),
    )(blk_e, rows, w_gate, w_up, w_down)


def _combine_kernel(gate_ref, h_ref, shared_ref, gf_ref, sh_ref, sc_ref, *rest):
    y_refs, outs = rest[:TOP_K], rest[TOP_K:]
    routed = None
    for k in range(TOP_K):
        y = _unpack_rows(jnp.concatenate([y_refs[k][i, 0] for i in range(ROW_PARTS)], axis=1))
        routed = gate_ref[:, k:k + 1] * y if routed is None else routed + gate_ref[:, k:k + 1] * y
    h = h_ref[...] + gf_ref[...] * (routed + shared_ref[...].astype(F32))
    outs[0][...] = h
    if len(outs) > 1:
        outs[1][...] = _rms(h) * (1.0 + sc_ref[...]) + sh_ref[...]


def combine(y_rows, dest, gate_tk, h, shared, gate_ffn, next_shift, next_scale, with_next):
    t, d = h.shape
    parts, cap, dq = y_rows.shape
    n_out = 2 if with_next else 1
    n_choice = dest.shape[0]
    flat = (dest[None] + (jnp.arange(parts, dtype=I32) * cap)[:, None, None]).reshape(1, parts * n_choice * t)
    picked = gather_rows(y_rows.reshape(parts * cap, dq), flat).reshape(parts, n_choice, t, dq)
    tm = min(256, t)
    row = lambda i: (i, 0)
    fixed = lambda i: (0, 0)
    choice = [pl.BlockSpec((parts, 1, tm, dq), functools.partial(lambda i, k: (0, k, i, 0), k=k))
              for k in range(n_choice)]
    return pl.pallas_call(
        _combine_kernel,
        grid=(t // tm,),
        in_specs=[pl.BlockSpec((tm, n_choice), row), pl.BlockSpec((tm, d), row), pl.BlockSpec((tm, d), row),
                  pl.BlockSpec((1, d), fixed), pl.BlockSpec((1, d), fixed), pl.BlockSpec((1, d), fixed)] + choice,
        out_specs=[pl.BlockSpec((tm, d), row)] * n_out,
        out_shape=[jax.ShapeDtypeStruct((t, d), F32)] * n_out,
        compiler_params=_params("parallel"),
    )(gate_tk, h, shared, gate_ffn, next_shift, next_scale, *([picked] * n_choice))


def moe_tail(h, f_rows, logits_t, shared, router_bias, layer, w_gate, w_up, w_down, gate_ffn,
             next_shift, next_scale, with_next):
    t, d = h.shape
    parts, _, dq = f_rows.shape
    n_exp = w_gate.shape[1]
    n_blocks = -(-(t * TOP_K + n_exp * (EXPERT_ROWS - 1)) // EXPERT_ROWS)
    cap = n_blocks * EXPERT_ROWS
    idx, gate, rank, counts = route(logits_t, router_bias)
    dest, blk_e, n_valid = dispatch_plan(idx, rank, counts, n_blocks)
    dest_all = jnp.concatenate([dest + i * cap for i in range(parts)], axis=1)
    rows = dispatch(f_rows.reshape(parts * t, dq), dest_all, parts * cap).reshape(parts, cap, dq)
    y_rows = experts(rows, blk_e, n_valid, layer, w_gate, w_up, w_down)
    return combine(y_rows, dest, gate.T, h, shared, gate_ffn, next_shift, next_scale, with_next)


def kernel(x, c, ctx, c_ctx, w_ada, b_ada, w_in, s5_lam_re, s5_lam_im, s5_log_dt, s5_b_re, s5_b_im,
           s5_c_re, s5_c_im, s5_d, s5_w_glu, s5_b_glu, na_q_gain, na_k_gain, na_rpb, w_mix_out,
           w_fourier_out, b_fourier_out, w_router, router_bias, w_exp_gate, w_exp_up, w_exp_down,
           w_sh_gate, w_sh_up, w_sh_down):
    bsz, t, d = x.shape
    assert bsz == 1 and w_ada.shape[0] == 2
    n_exp = w_router.shape[2]
    s5w = s5_w_glu.shape[1]
    naw = w_in.shape[2] - s5w
    naw //= 3
    heads = naw // NA_HEAD_DIM

    cond8 = jnp.concatenate([c[:1].astype(F32), c_ctx.astype(F32)[None], jnp.zeros((6, d), F32)], axis=0)
    ada = adaln_all(cond8, w_ada, b_ada)
    mod = lambda layer, who, j: ada[layer, who:who + 1, j * d:(j + 1) * d]

    def ffn_weights(i):
        return (mod(i, 0, 3), mod(i, 0, 4), jnp.transpose(w_router[i]).astype(F32),
                w_sh_gate[i].astype(BF16), w_sh_up[i].astype(BF16), w_sh_down[i].astype(BF16))

    h0 = x[0]
    seg = jnp.asarray(np.kron(np.eye(heads), np.ones((NA_HEAD_DIM, NA_HEAD_DIM))), BF16)
    w_in_b = w_in[0].astype(BF16)
    qg = jnp.tile(na_q_gain[0].astype(F32), heads)[None]
    kg = jnp.tile(na_k_gain[0].astype(F32), heads)[None]
    u_c, _, k_c, v_c = in_projection(ctx[0], mod(0, 1, 0), mod(0, 1, 1), w_in_b, seg, qg, kg, s5w, naw)
    u_l, q_l, k_l, v_l = in_projection(h0, mod(0, 0, 0), mod(0, 0, 1), w_in_b, seg, qg, kg, s5w, naw)
    mats = s5_matrices(s5_lam_re[0], s5_lam_im[0], s5_log_dt[0], s5_b_re[0], s5_b_im[0],
                       s5_c_re[0], s5_c_im[0], s5_d[0])
    y_s5 = s5_mixer(u_c, u_l, mats)
    na = neighbourhood_attention(q_l, k_l, v_l, k_c, v_c, na_bias_table(na_rpb[0]))
    h1, f1, lg1, sh1 = _post_call(
        _even_post_kernel, [y_s5, na, h0],
        [s5_w_glu[0].astype(BF16), s5_b_glu[0].astype(F32)[None], w_mix_out[0].astype(BF16), mod(0, 0, 2),
         *ffn_weights(0)], t, d, n_exp)
    h2, a1 = moe_tail(h1, f1, lg1, sh1, router_bias[0], 0, w_exp_gate, w_exp_up, w_exp_down,
                      mod(0, 0, 5), mod(1, 0, 0), mod(1, 0, 1), True)

    zre, zim = time_dft(a1)
    cc, sc = channel_dft_tables(d // FOURIER_GROUPS)
    h3, f3, lg3, sh3 = _post_call(
        _odd_post_kernel, [zre, zim, h2],
        [cc, sc, w_fourier_out[0].astype(BF16), b_fourier_out[0].astype(F32)[None], mod(1, 0, 2),
         *ffn_weights(1)], t, d, n_exp)
    zero_row = jnp.zeros((1, d), F32)
    (out,) = moe_tail(h3, f3, lg3, sh3, router_bias[1], 1, w_exp_gate, w_exp_up, w_exp_down,
                      mod(1, 0, 5), zero_row, zero_row, False)
    return out[None]
```

```python
import functools
import math

import numpy as np
import jax
import jax.numpy as jnp
from jax import lax
from jax.experimental import pallas as pl
from jax.experimental.pallas import tpu as pltpu
from jax.experimental.pallas import tpu_sc as plsc

F32 = jnp.float32
BF16 = jnp.bfloat16
I32 = jnp.int32
U32 = jnp.uint32
HIGHEST = lax.Precision.HIGHEST

MXU_DEPTH = 256
LANES = 128
GRID_W = 64
NORM_EPS = 1e-6
S5_GROUP = 16
S5_STATE = 64
S5_LAMBDA_RE_MAX = -1e-4
S5_CHUNK = 16
S5_TILE = 128
NA_HEADS = 8
NA_HEAD_DIM = 64
NA_KH = 8
NA_KW = 16
FOURIER_GROUPS = 4
N_EXPERT_GROUPS = 8
TOPK_GROUPS = 4
TOP_K = 8
ROUTED_SCALE = 2.5
EXPERT_ROWS = 1024
ROW_PARTS = 2
SC_WINDOW = 128
SC_GATHER_SPLIT = 4
NEG_BIG = -1e30

VMEM_LIMIT_BYTES = 56 * 1024 * 1024


def _params(*sem):
    return pltpu.CompilerParams(dimension_semantics=sem or None,
                                vmem_limit_bytes=VMEM_LIMIT_BYTES)


def _rms(x):
    return x * lax.rsqrt(jnp.mean(x * x, axis=-1, keepdims=True) + NORM_EPS)


def _silu(x):
    return x * jax.nn.sigmoid(x)


def _pack_pair(lo, hi):
    lo = lax.bitcast_convert_type(lo.astype(BF16).astype(F32), U32)
    hi = lax.bitcast_convert_type(hi.astype(BF16).astype(F32), U32)
    return (hi & jnp.uint32(0xFFFF0000)) | (lo >> 16)


def _unpack_pair(w):
    return (lax.bitcast_convert_type(w << 16, F32), lax.bitcast_convert_type(w & jnp.uint32(0xFFFF0000), F32))


def _pack_rows(x):
    n = x.shape[1] // 2
    return _pack_pair(x[:, :n], x[:, n:])


def _unpack_rows(w):
    return jnp.concatenate(_unpack_pair(w), axis=1)


def _ada_kernel(c_ref, w_ref, b_ref, o_ref):
    o_ref[0] = jnp.dot(_silu(c_ref[...]), w_ref[0], preferred_element_type=F32,
                       precision=HIGHEST) + b_ref[0]


def adaln_all(cond8, w_ada, b_ada):
    n_layers, d, n6 = w_ada.shape
    tn = n6 // 4
    return pl.pallas_call(
        _ada_kernel,
        grid=(n_layers, n6 // tn),
        in_specs=[pl.BlockSpec((8, d), lambda l, j: (0, 0)),
                  pl.BlockSpec((1, d, tn), lambda l, j: (l, 0, j)),
                  pl.BlockSpec((1, 1, tn), lambda l, j: (l, 0, j))],
        out_specs=pl.BlockSpec((1, 8, tn), lambda l, j: (l, 0, j)),
        out_shape=jax.ShapeDtypeStruct((n_layers, 8, n6), F32),
        compiler_params=_params("parallel", "parallel"),
    )(cond8, w_ada, b_ada.reshape(n_layers, 1, n6))


def _inproj_kernel(x_ref, sh_ref, sc_ref, w_ref, seg_ref, qg_ref, kg_ref,
                   u_ref, q_ref, k_ref, v_ref):
    a = _rms(x_ref[...]) * (1.0 + sc_ref[...]) + sh_ref[...]
    z = jnp.dot(a.astype(BF16), w_ref[...], preferred_element_type=F32)
    s5w = u_ref.shape[1]
    naw = q_ref.shape[1]

    def head_norm(t, gain):
        ss = jnp.dot((t * t).astype(BF16), seg_ref[...], preferred_element_type=F32)
        return t * lax.rsqrt(ss * (1.0 / NA_HEAD_DIM) + NORM_EPS) * gain

    u_ref[...] = z[:, :s5w]
    q_ref[...] = head_norm(z[:, s5w:s5w + naw], qg_ref[...]).astype(BF16)
    k_ref[...] = head_norm(z[:, s5w + naw:s5w + 2 * naw], kg_ref[...]).astype(BF16)
    v_ref[...] = z[:, s5w + 2 * naw:].astype(BF16)


def in_projection(x, shift, scale, w_in_bf16, seg_ones, q_gain_row, k_gain_row, s5w, naw):
    t, d = x.shape
    tm = min(512, t)
    row = lambda i: (i, 0)
    fixed = lambda i: (0, 0)
    return pl.pallas_call(
        _inproj_kernel,
        grid=(t // tm,),
        in_specs=[pl.BlockSpec((tm, d), row),
                  pl.BlockSpec((1, d), fixed), pl.BlockSpec((1, d), fixed),
                  pl.BlockSpec(w_in_bf16.shape, fixed),
                  pl.BlockSpec(seg_ones.shape, fixed),
                  pl.BlockSpec((1, naw), fixed), pl.BlockSpec((1, naw), fixed)],
        out_specs=[pl.BlockSpec((tm, s5w), row), pl.BlockSpec((tm, naw), row),
                   pl.BlockSpec((tm, naw), row), pl.BlockSpec((tm, naw), row)],
        out_shape=[jax.ShapeDtypeStruct((t, s5w), F32)] + [jax.ShapeDtypeStruct((t, naw), BF16)] * 3,
        compiler_params=_params("parallel"),
    )(x, shift, scale, w_in_bf16, seg_ones, q_gain_row, k_gain_row)


def s5_matrices(lam_re, lam_im, log_dt, b_re, b_im, c_re, c_im, d_skip):
    L = S5_CHUNK
    taus = jnp.arange(L + 1, dtype=F32)

    def direction(i):
        lam = lax.complex(jnp.minimum(lam_re[i].astype(F32), S5_LAMBDA_RE_MAX), lam_im[i].astype(F32))
        ldt = lam * jnp.exp(log_dt[i].astype(F32))[:, None]
        lam_bar = jnp.exp(ldt)
        b_bar = ((lam_bar - 1.0) / lam)[..., None] * lax.complex(b_re[i].astype(F32), b_im[i].astype(F32))
        cc = lax.complex(c_re[i].astype(F32), c_im[i].astype(F32))
        powers = jnp.exp(ldt[None] * taus[:, None, None])
        resp = jnp.real(jnp.einsum('gcp,tgp,gpd->gtcd', cc, powers[:L], b_bar, precision=HIGHEST))
        return powers, b_bar, cc, resp

    pw_f, bb_f, cc_f, k_f = direction(0)
    pw_b, bb_b, cc_b, k_b = direction(1)
    g, p = pw_f.shape[1:]
    c = S5_GROUP
    diff = np.arange(L)[None, :] - np.arange(L)[:, None]
    lag = np.arange(L)[:, None, None]
    tf = jnp.einsum('tsl,gtcd->gslcd', jnp.asarray(lag == diff[None], F32), k_f, precision=HIGHEST)
    tb = jnp.einsum('tsl,gtcd->gslcd', jnp.asarray(lag == -diff[None], F32), k_b, precision=HIGHEST)
    skip = (jnp.eye(L, dtype=F32)[None, :, :, None, None]
            * (jnp.eye(c, dtype=F32)[None] * d_skip.astype(F32)[:, :, None])[:, None, None])
    toep = jnp.transpose(tf + tb + skip, (0, 1, 4, 2, 3)).reshape(g, L * c, L * c)

    def state_in(powers_sel, b_bar):
        w = powers_sel[:, :, :, None] * b_bar[None]
        return jnp.transpose(w, (1, 0, 3, 2)).reshape(g, L * c, p)

    wf = state_in(pw_f[L - 1 - jnp.arange(L)], bb_f)
    wb = state_in(pw_b[jnp.arange(L)], bb_b)
    w_state = jnp.concatenate([jnp.real(wf), jnp.imag(wf), jnp.imag(wf), jnp.real(wf),
                               jnp.real(wb), jnp.imag(wb), jnp.imag(wb), jnp.real(wb)], axis=-1)

    def state_out(powers_sel, cc):
        r = powers_sel[:, :, None, :] * cc[None]
        return jnp.transpose(r, (1, 3, 0, 2)).reshape(g, p, L * c)

    rf = state_out(pw_f[1 + jnp.arange(L)], cc_f)
    rb = state_out(pw_b[L - jnp.arange(L)], cc_b)
    r_state = jnp.concatenate([jnp.real(rf), -jnp.imag(rf), jnp.real(rb), -jnp.imag(rb)], axis=1)

    def mult(a):
        ar, ai = jnp.real(a), jnp.imag(a)
        return jnp.stack([jnp.concatenate([ar, ar], -1), jnp.concatenate([-ai, ai], -1),
                          jnp.concatenate([ai, -ai], -1)])

    return toep, w_state, r_state, mult(pw_f[L]), mult(pw_b[L])


def _s5_pack_kernel(*refs):
    u_refs, (wt_ref, ut_ref, f1_ref, f2_ref, b1_ref, b2_ref) = refs[:-6], refs[-6:]
    L = S5_CHUNK
    g, lc, nck = ut_ref.shape
    c = lc // L
    gs = g // len(u_refs)
    for s in range(L):
        for j, u_ref in enumerate(u_refs):
            step_s = u_ref[pl.ds(s, nck, stride=L), :]
            ut_ref[j * gs:(j + 1) * gs, s * c:(s + 1) * c, :] = (
                jnp.transpose(step_s).astype(BF16).reshape(gs, c, nck))
    n = f1_ref.shape[2]
    for gi in range(g):
        inc = jnp.dot(wt_ref[gi], ut_ref[gi], preferred_element_type=F32)
        for j, ref in enumerate((f1_ref, f2_ref, b1_ref, b2_ref)):
            ref[:, gi, :] = jnp.transpose(inc[j * n:(j + 1) * n, :])


def s5_pack(u, wt_state):
    t, w = u.shape
    g, n4, lc = wt_state.shape
    nc = t // S5_CHUNK
    tile = min(S5_TILE, nc)
    inc = jax.ShapeDtypeStruct((nc, g, n4 // 4), F32)
    inc_blk = pl.BlockSpec((tile, g, n4 // 4), lambda i: (i, 0, 0))
    return pl.pallas_call(
        _s5_pack_kernel,
        grid=(nc // tile,),
        in_specs=[pl.BlockSpec((tile * S5_CHUNK, LANES), functools.partial(lambda i, j: (i, j), j=j))
                  for j in range(w // LANES)] + [pl.BlockSpec(wt_state.shape, lambda i: (0, 0, 0))],
        out_specs=[pl.BlockSpec((g, lc, tile), lambda i: (0, 0, i))] + [inc_blk] * 4,
        out_shape=[jax.ShapeDtypeStruct((g, lc, nc), BF16)] + [inc] * 4,
        compiler_params=_params("parallel"),
    )(*([u] * (w // LANES)), wt_state)


def _s5_scan_kernel(s1_ref, s2_ref, m_ref, init_ref, x_ref, last_ref, v1_ref, v2_ref, *, reverse):
    @pl.when(pl.program_id(0) == 0)
    def _():
        v1_ref[...] = init_ref[0]
        v2_ref[...] = init_ref[1]

    a1, a2, a3 = m_ref[0], m_ref[1], m_ref[2]
    cb = s1_ref.shape[0]

    def body(j, carry):
        v1, v2 = carry
        jj = cb - 1 - j if reverse else j
        x_ref[jj] = v1
        return (a1 * v1 + a2 * v2 + s1_ref[jj], a1 * v2 + a3 * v1 + s2_ref[jj])

    v1, v2 = lax.fori_loop(0, cb, body, (v1_ref[...], v2_ref[...]))
    v1_ref[...] = v1
    v2_ref[...] = v2
    last_ref[...] = v1


def s5_chunk_scan(s1, s2, mult, init, reverse):
    nc, g, n = s1.shape
    cb = min(S5_TILE, nc)
    nb = nc // cb
    blk = (lambda i: (nb - 1 - i, 0, 0)) if reverse else (lambda i: (i, 0, 0))
    return pl.pallas_call(
        functools.partial(_s5_scan_kernel, reverse=reverse),
        grid=(nb,),
        in_specs=[pl.BlockSpec((cb, g, n), blk), pl.BlockSpec((cb, g, n), blk),
                  pl.BlockSpec((3, g, n), lambda i: (0, 0, 0)), pl.BlockSpec((2, g, n), lambda i: (0, 0, 0))],
        out_specs=[pl.BlockSpec((cb, g, n), blk), pl.BlockSpec((g, n), lambda i: (0, 0))],
        out_shape=[jax.ShapeDtypeStruct((nc, g, n), F32), jax.ShapeDtypeStruct((g, n), F32)],
        scratch_shapes=[pltpu.VMEM((g, n), F32), pltpu.VMEM((g, n), F32)],
        compiler_params=_params("arbitrary"),
    )(s1, s2, mult, init)


def _s5_readout_kernel(ut_ref, tt_ref, rt_ref, xf_ref, xb_ref, y_ref, yt_ref, *slab_refs):
    L = S5_CHUNK
    g, lc, nck = ut_ref.shape
    c = lc // L
    gs = g // len(slab_refs)
    for gi in range(g):
        xin_t = jnp.concatenate([jnp.transpose(xf_ref[:, gi, :]), jnp.transpose(xb_ref[:, gi, :])], axis=0)
        yt_ref[gi] = (jnp.dot(tt_ref[gi], ut_ref[gi], preferred_element_type=F32)
                      + jnp.dot(rt_ref[gi], xin_t.astype(BF16), preferred_element_type=F32))
    for j, slab in enumerate(slab_refs):
        for l in range(L):
            step_l = yt_ref[j * gs:(j + 1) * gs, l * c:(l + 1) * c, :].reshape(gs * c, nck)
            slab[pl.ds(l, nck, stride=L), :] = jnp.transpose(step_l)
        y_ref[:, j * gs * c:(j + 1) * gs * c] = slab[...]


def s5_readout(ut, toep_t, r_state_t, xin_f, xin_b):
    g, lc, nc = ut.shape
    n = xin_f.shape[2]
    tile = min(S5_TILE, nc)
    fixed = lambda i: (0, 0, 0)
    state_blk = pl.BlockSpec((tile, g, n), lambda i: (i, 0, 0))
    return pl.pallas_call(
        _s5_readout_kernel,
        grid=(nc // tile,),
        in_specs=[pl.BlockSpec((g, lc, tile), lambda i: (0, 0, i)),
                  pl.BlockSpec(toep_t.shape, fixed), pl.BlockSpec(r_state_t.shape, fixed), state_blk, state_blk],
        out_specs=pl.BlockSpec((tile * S5_CHUNK, g * lc // S5_CHUNK), lambda i: (i, 0)),
        out_shape=jax.ShapeDtypeStruct((nc * S5_CHUNK, g * lc // S5_CHUNK), F32),
        scratch_shapes=[pltpu.VMEM((g, lc, tile), F32)]
        + [pltpu.VMEM((tile * S5_CHUNK, LANES), F32)] * (g * lc // S5_CHUNK // LANES),
        compiler_params=_params("parallel"),
    )(ut, toep_t, r_state_t, xin_f, xin_b)


def s5_mixer(u_ctx, u_lat, mats):
    toep, w_state, r_state, mult_f, mult_b = mats
    L = S5_CHUNK
    g, n = mult_f.shape[1:]
    swap = lambda a: jnp.transpose(a, (0, 2, 1)).astype(BF16)
    wt_state, toep_t, r_state_t = swap(w_state), swap(toep), swap(r_state)
    halves = lambda v: jnp.stack([v, jnp.roll(v, n // 2, axis=-1)])
    n_ctx = u_ctx.shape[0] // L
    ctx_chunks = -(-(n_ctx + 1) // S5_TILE) * S5_TILE
    ctx_pad = jnp.pad(u_ctx, ((0, ctx_chunks * L - u_ctx.shape[0]), (0, 0)))
    _, cf1, cf2, cb1, cb2 = s5_pack(ctx_pad, wt_state)
    zero = jnp.zeros((2, g, n), F32)
    ctx_f, _ = s5_chunk_scan(cf1, cf2, mult_f, zero, False)
    _, ctx_b_last = s5_chunk_scan(cb1, cb2, mult_b, zero, True)
    ut, f1, f2, b1, b2 = s5_pack(u_lat, wt_state)
    xin_f, _ = s5_chunk_scan(f1, f2, mult_f, halves(ctx_f[n_ctx]), False)
    xin_b, _ = s5_chunk_scan(b1, b2, mult_b, halves(ctx_b_last), True)
    return s5_readout(ut, toep_t, r_state_t, xin_f, xin_b)


def na_bias_table(rpb):
    q_col = np.arange(GRID_W)
    col_start = np.clip(q_col - NA_KW // 2, 0, GRID_W - NA_KW)
    key_col = np.arange(GRID_W)
    off = key_col[None, :] - col_start[:, None]
    valid = (off >= 0) & (off < NA_KW)
    rel_col = np.clip(key_col[None, :] - q_col[:, None] + NA_KW - 1, 0, 2 * NA_KW - 2)
    pick_col = jnp.asarray(np.arange(2 * NA_KW - 1)[:, None, None] == rel_col[None], F32)
    rel_row = np.arange(NA_KH)[None, :] - np.arange(NA_KH)[:, None] + NA_KH - 1
    pick_row = jnp.asarray(np.arange(2 * NA_KH - 1)[:, None, None] == rel_row[None], F32)
    table = jnp.einsum('hrj,rdi,jck->dhcik', rpb.astype(F32), pick_row, pick_col, precision=HIGHEST)
    table = table.reshape(NA_KH, rpb.shape[0], GRID_W, NA_KH * GRID_W)
    return jnp.where(np.tile(valid, (1, NA_KH))[None, None], table, NEG_BIG)


def _na_kernel(*refs, scale):
    q_ref = refs[0]
    k_refs = refs[1:1 + NA_KH]
    v_refs = refs[1 + NA_KH:1 + 2 * NA_KH]
    kc_ref, vc_ref, b_ref, o_ref = refs[1 + 2 * NA_KH:]
    hd = NA_HEAD_DIM
    width = q_ref.shape[1]
    span = min(MXU_DEPTH, width)
    nt = (((1,), (1,)), ((), ()))
    q = q_ref[...] * scale
    kk = jnp.concatenate([r[...] for r in k_refs], axis=0)
    vv = jnp.concatenate([r[...] for r in v_refs], axis=0)
    kc, vc = kc_ref[...], vc_ref[...]
    nq = q.shape[0]
    per = span // hd
    lane = lax.broadcasted_iota(I32, (nq, span), 1)
    own = [(lane >= j * hd) & (lane < (j + 1) * hd) for j in range(per)]
    out_cols = []
    for c0 in range(0, width, span):
        cols = slice(c0, c0 + span)
        qs = jnp.concatenate([jnp.where(own[j], q[:, cols], jnp.zeros_like(q[:, cols])) for j in range(per)], axis=0)
        h0 = c0 // hd
        bias = b_ref[0, h0:h0 + per].reshape(per * nq, kk.shape[0])
        s = lax.dot_general(qs, kk[:, cols], nt, preferred_element_type=F32) + bias
        sc = lax.dot_general(qs, kc[:, cols], nt, preferred_element_type=F32)
        m = jnp.maximum(jnp.max(s, axis=-1, keepdims=True), jnp.max(sc, axis=-1, keepdims=True))
        p = jnp.exp(s - m)
        pc = jnp.exp(sc - m)
        den = jnp.sum(p, axis=-1, keepdims=True) + jnp.sum(pc, axis=-1, keepdims=True)
        o = (jnp.dot(p.astype(BF16), vv[:, cols], preferred_element_type=F32)
             + jnp.dot(pc.astype(BF16), vc[:, cols], preferred_element_type=F32)) / den
        acc = jnp.zeros((nq, span), F32)
        for j in range(per):
            acc = jnp.where(own[j], o[j * nq:(j + 1) * nq], acc)
        out_cols.append(acc)
    o_ref[...] = jnp.concatenate(out_cols, axis=1).astype(o_ref.dtype)


def neighbourhood_attention(q, k, v, k_ctx, v_ctx, bias_table):
    t, w = q.shape
    rows = t // GRID_W
    kh = min(NA_KH, rows)
    assert kh == NA_KH
    first = lambda r: jnp.clip(r - kh // 2, 0, rows - kh)
    row_blk = pl.BlockSpec((GRID_W, w), lambda r: (r, 0))
    key_blks = [pl.BlockSpec((GRID_W, w), functools.partial(lambda r, i: (first(r) + i, 0), i=i))
                for i in range(kh)]
    ctx_blk = pl.BlockSpec(k_ctx.shape, lambda r: (0, 0))
    bias_blk = pl.BlockSpec((1,) + bias_table.shape[1:], lambda r: (r - first(r), 0, 0, 0))
    return pl.pallas_call(
        functools.partial(_na_kernel, scale=NA_HEAD_DIM ** -0.5),
        grid=(rows,),
        in_specs=[row_blk] + key_blks + key_blks + [ctx_blk, ctx_blk, bias_blk],
        out_specs=row_blk,
        out_shape=jax.ShapeDtypeStruct((t, w), BF16),
        compiler_params=_params("parallel"),
    )(q, *([k] * kh), *([v] * kh), k_ctx, v_ctx, bias_table)


def time_dft_tables(t):
    a_len = 1 << (int(math.log2(t)) // 2)
    b_len = t // a_len
    ka = np.arange(a_len)[:, None]
    tok = b_len * np.arange(a_len)[None, :]
    ang1 = -2.0 * np.pi * ((ka * (tok[None] + np.arange(b_len)[:, None, None])) % t) / t
    stage1 = np.concatenate([np.cos(ang1), np.sin(ang1)], axis=1) / math.sqrt(t)
    ang2 = 2.0 * np.pi * ((np.arange(b_len)[:, None] * np.arange(b_len)[None, :]) % b_len) / b_len
    c2, s2 = np.cos(ang2), np.sin(ang2)
    stage2 = np.block([[c2, s2], [-s2, c2]])
    return jnp.asarray(stage1, BF16), jnp.asarray(stage2, BF16), a_len, b_len


def _time_dft_kernel(x_ref, m_ref, w2_ref, zre_ref, zim_ref, y_ref, z_ref, *, a_len, b_len):
    i = pl.program_id(1)
    bb = m_ref.shape[0]

    def stage1(jb, carry):
        b = i * bb + jb
        xb = x_ref[pl.ds(b, a_len, stride=b_len), :]
        y = jnp.dot(m_ref[jb], xb.astype(BF16), preferred_element_type=F32)
        row = pl.multiple_of(b * a_len, a_len)
        y_ref[pl.ds(row, a_len), :] = _pack_pair(y[:a_len], y[a_len:])
        return carry

    lax.fori_loop(0, bb, stage1, 0, unroll=4)

    @pl.when(i == pl.num_programs(1) - 1)
    def _():
        def stage2(ka, carry):
            rows = pl.ds(ka, b_len, stride=a_len)
            yre, yim = _unpack_pair(y_ref[rows, :])
            y = jnp.concatenate([yre, yim], axis=0).astype(BF16)
            z = jnp.dot(w2_ref[...], y, preferred_element_type=F32)
            z_ref[rows, :] = _pack_pair(z[:b_len], z[b_len:])
            return carry

        lax.fori_loop(0, a_len, stage2, 0, unroll=4)
        zre, zim = _unpack_pair(z_ref[...])
        zre_ref[...] = zre.astype(zre_ref.dtype)
        zim_ref[...] = zim.astype(zim_ref.dtype)


def time_dft(x):
    t, d = x.shape
    stage1, stage2, a_len, b_len = time_dft_tables(t)
    lanes = 128
    bb = min(16, b_len)
    out = jax.ShapeDtypeStruct((t, d), BF16)
    return pl.pallas_call(
        functools.partial(_time_dft_kernel, a_len=a_len, b_len=b_len),
        grid=(d // lanes, b_len // bb),
        in_specs=[pl.BlockSpec((t, lanes), lambda j, i: (0, j)),
                  pl.BlockSpec((bb, 2 * a_len, a_len), lambda j, i: (i, 0, 0)),
                  pl.BlockSpec(stage2.shape, lambda j, i: (0, 0))],
        out_specs=[pl.BlockSpec((t, lanes), lambda j, i: (0, j))] * 2,
        out_shape=[out, out],
        scratch_shapes=[pltpu.VMEM((t, lanes), U32), pltpu.VMEM((t, lanes), U32)],
        compiler_params=_params("parallel", "arbitrary"),
    )(x, stage1, stage2)


def channel_dft_tables(c):
    ang = 2.0 * np.pi * ((np.arange(c)[:, None] * np.arange(c)[None, :]) % c) / c
    return (jnp.asarray(np.cos(ang) / math.sqrt(c), BF16), jnp.asarray(np.sin(ang) / math.sqrt(c), BF16))


def _ffn_prologue(h, shf_ref, scf_ref, wr_ref, wsg_ref, wsu_ref, wsd_ref, h_ref, f_ref, lg_ref, shared_ref):
    h_ref[...] = h
    f = _rms(h) * (1.0 + scf_ref[...]) + shf_ref[...]
    packed = _pack_rows(f)
    half = packed.shape[1] // 2
    f_ref[0] = packed[:, :half]
    f_ref[1] = packed[:, half:]
    lg_ref[...] = lax.dot_general(wr_ref[...], f, (((1,), (1,)), ((), ())),
                                  preferred_element_type=F32, precision=HIGHEST)
    fb = f.astype(BF16)
    hid = (_silu(jnp.dot(fb, wsg_ref[...], preferred_element_type=F32))
           * jnp.dot(fb, wsu_ref[...], preferred_element_type=F32))
    shared_ref[...] = jnp.dot(hid.astype(BF16), wsd_ref[...], preferred_element_type=F32).astype(shared_ref.dtype)


def _gelu_tanh(x):
    return 0.5 * x * (1.0 + jnp.tanh(math.sqrt(2.0 / math.pi) * (x + 0.044715 * (x * x * x))))


def _even_post_kernel(y_ref, na_ref, x_ref, wglu_ref, bglu_ref, wo_ref, gm_ref, *rest):
    g = _gelu_tanh(y_ref[...])
    gate = jax.nn.sigmoid(jnp.dot(g.astype(BF16), wglu_ref[...], preferred_element_type=F32) + bglu_ref[...])
    s5 = (g * gate).astype(BF16)
    w = s5.shape[1]
    mix = (jnp.dot(s5, wo_ref[:w, :], preferred_element_type=F32)
           + jnp.dot(na_ref[...], wo_ref[w:, :], preferred_element_type=F32))
    _ffn_prologue(x_ref[...] + gm_ref[...] * mix, *rest)


def _odd_post_kernel(zre_ref, zim_ref, h_ref_in, cc_ref, sc_ref, wf_ref, bf_ref, gm_ref, *rest):
    c = cc_ref.shape[0]
    parts = []
    for grp in range(zre_ref.shape[1] // c):
        cols = slice(grp * c, (grp + 1) * c)
        parts.append(jnp.dot(zre_ref[:, cols], cc_ref[...], preferred_element_type=F32)
                     + jnp.dot(zim_ref[:, cols], sc_ref[...], preferred_element_type=F32))
    fr = jnp.concatenate(parts, axis=-1).astype(BF16)
    mix = jnp.dot(fr, wf_ref[...], preferred_element_type=F32) + bf_ref[...]
    _ffn_prologue(h_ref_in[...] + gm_ref[...] * mix, *rest)


def _post_call(body, row_inputs, fixed_inputs, t, d, n_exp):
    tm = min(512, t)
    row = lambda i: (i, 0)
    fixed = lambda i: (0, 0)
    in_specs = ([pl.BlockSpec((tm, a.shape[1]), row) for a in row_inputs]
                + [pl.BlockSpec(a.shape, fixed) for a in fixed_inputs])
    return pl.pallas_call(
        body,
        grid=(t // tm,),
        in_specs=in_specs,
        out_specs=[pl.BlockSpec((tm, d), row), pl.BlockSpec((ROW_PARTS, tm, d // 4), lambda i: (0, i, 0)),
                   pl.BlockSpec((n_exp, tm), lambda i: (0, i)), pl.BlockSpec((tm, d), row)],
        out_shape=[jax.ShapeDtypeStruct((t, d), F32), jax.ShapeDtypeStruct((ROW_PARTS, t, d // 4), U32),
                   jax.ShapeDtypeStruct((n_exp, t), F32), jax.ShapeDtypeStruct((t, d), BF16)],
        compiler_params=_params("parallel"),
    )(*row_inputs, *fixed_inputs)


def _route_kernel(lg_ref, bias_ref, tri_ref, idx_ref, gate_ref, rank_ref, cnt_ref, run_ref):
    @pl.when(pl.program_id(0) == 0)
    def _():
        run_ref[...] = jnp.zeros_like(run_ref)

    scores = jax.nn.sigmoid(lg_ref[...])
    n_exp, tb = scores.shape
    sel = scores + bias_ref[...]
    gsz = n_exp // N_EXPERT_GROUPS
    member = lax.broadcasted_iota(I32, (gsz, tb), 0)
    gscore = []
    for grp in range(N_EXPERT_GROUPS):
        xg = sel[grp * gsz:(grp + 1) * gsz, :]
        m1 = jnp.max(xg, axis=0, keepdims=True)
        first = jnp.min(jnp.where(xg == m1, member, gsz), axis=0, keepdims=True)
        m2 = jnp.max(jnp.where(member == first, -jnp.inf, xg), axis=0, keepdims=True)
        gscore.append(m1 + m2)
    keep_rows = []
    for grp in range(N_EXPERT_GROUPS):
        beaten = jnp.zeros((1, tb), F32)
        for other in range(N_EXPERT_GROUPS):
            if other == grp:
                continue
            wins = (gscore[other] >= gscore[grp]) if other < grp else (gscore[other] > gscore[grp])
            beaten = beaten + jnp.where(wins, 1.0, 0.0)
        keep_rows.append(jnp.broadcast_to(beaten < TOPK_GROUPS, (gsz, tb)))
    masked = jnp.where(jnp.concatenate(keep_rows, axis=0), sel, -jnp.inf)

    expert = lax.broadcasted_iota(I32, (n_exp, tb), 0)
    picks, gates, hots = [], [], []
    chosen = jnp.zeros((n_exp, tb), F32)
    for _ in range(TOP_K):
        m = jnp.max(masked, axis=0, keepdims=True)
        pick = jnp.min(jnp.where(masked == m, expert, n_exp), axis=0, keepdims=True)
        hot = expert == pick
        picks.append(pick)
        hots.append(hot)
        gates.append(jnp.sum(jnp.where(hot, scores, 0.0), axis=0, keepdims=True))
        chosen = jnp.where(hot, 1.0, chosen)
        masked = jnp.where(hot, -jnp.inf, masked)
    total = gates[0]
    for gk in gates[1:]:
        total = total + gk
    ahead = jnp.dot(chosen.astype(BF16), tri_ref[...], preferred_element_type=F32) + run_ref[...]
    for k in range(TOP_K):
        idx_ref[k:k + 1, :] = picks[k]
        gate_ref[k:k + 1, :] = ROUTED_SCALE * gates[k] / total
        rank_ref[k:k + 1, :] = jnp.sum(jnp.where(hots[k], ahead, 0.0), axis=0, keepdims=True).astype(I32)
    run_ref[...] = run_ref[...] + jnp.sum(chosen, axis=1, keepdims=True)
    cnt_ref[...] = jnp.broadcast_to(run_ref[...], cnt_ref.shape)


def route(logits_t, router_bias):
    n_exp, t = logits_t.shape
    tb = min(512, t)
    tri = jnp.asarray(np.triu(np.ones((tb, tb), np.float32), k=1), BF16)
    tok = lambda i: (0, i)
    idx, gate, rank, cnt = pl.pallas_call(
        _route_kernel,
        grid=(t // tb,),
        in_specs=[pl.BlockSpec((n_exp, tb), tok), pl.BlockSpec((n_exp, 1), lambda i: (0, 0)),
                  pl.BlockSpec((tb, tb), lambda i: (0, 0))],
        out_specs=[pl.BlockSpec((TOP_K, tb), tok)] * 3 + [pl.BlockSpec((n_exp, 128), lambda i: (0, 0))],
        out_shape=[jax.ShapeDtypeStruct((TOP_K, t), I32), jax.ShapeDtypeStruct((TOP_K, t), F32),
                   jax.ShapeDtypeStruct((TOP_K, t), I32), jax.ShapeDtypeStruct((n_exp, 128), F32)],
        scratch_shapes=[pltpu.VMEM((n_exp, 1), F32)],
        compiler_params=_params("arbitrary"),
    )(logits_t, router_bias.astype(F32).reshape(n_exp, 1), tri)
    return idx, gate, rank, cnt[:, 0].astype(I32)


def dispatch_plan(idx, rank, counts, n_blocks):
    n_exp = counts.shape[0]
    padded = (counts + EXPERT_ROWS - 1) // EXPERT_ROWS * EXPERT_ROWS
    pad_end = jnp.cumsum(padded)
    pad_start = pad_end - padded
    experts_iota = jnp.arange(n_exp, dtype=I32)
    dest = jnp.sum(jnp.where(idx[..., None] == experts_iota, pad_start, 0), axis=-1) + rank
    n_valid = (pad_end[-1] // EXPERT_ROWS).astype(I32)
    blk = jnp.minimum(jnp.arange(n_blocks, dtype=I32), n_valid - 1)
    blk_e = jnp.sum((pad_end[None, :] <= blk[:, None] * EXPERT_ROWS).astype(I32), axis=1)
    return dest.astype(I32), jnp.minimum(blk_e, n_exp - 1).astype(I32), n_valid.reshape(1)


def _sc_mesh():
    return plsc.VectorSubcoreMesh(core_axis_name="core", subcore_axis_name="subcore")


def dispatch(f, dest, cap):
    t, w = f.shape
    n_choice = dest.shape[0]

    @functools.partial(pl.kernel, out_type=jax.ShapeDtypeStruct((cap, w), f.dtype), mesh=_sc_mesh(),
                       scratch_types=[pltpu.SemaphoreType.DMA])
    def scatter_rows(x_hbm, i_hbm, o_hbm, sem):
        def body(x_vmem, i_vmem):
            copies = [pltpu.async_copy(x_vmem, o_hbm.at[i_vmem.at[k]], sem) for k in range(n_choice)]
            for cp in copies:
                cp.wait()

        pltpu.emit_pipeline(
            body,
            grid=(t // SC_WINDOW,),
            in_specs=[pl.BlockSpec((SC_WINDOW, w), lambda i: (i, 0)),
                      pl.BlockSpec((n_choice, SC_WINDOW), lambda i: (0, i))],
            out_specs=[],
            core_axis_name=("core", "subcore"),
            dimension_semantics=(pltpu.PARALLEL,),
        )(x_hbm, i_hbm)

    return scatter_rows(f, dest)


def gather_rows(rows, index_row):
    n = index_row.shape[1]
    w = rows.shape[1]

    piece = SC_WINDOW // SC_GATHER_SPLIT

    @functools.partial(pl.kernel, out_type=jax.ShapeDtypeStruct((n, w), rows.dtype), mesh=_sc_mesh(),
                       scratch_types=[pltpu.SemaphoreType.DMA])
    def gather(y_hbm, i_hbm, o_hbm, sem):
        def body(i_vmem, o_vmem):
            copies = [pltpu.async_copy(y_hbm.at[i_vmem.at[0, pl.ds(j * piece, piece)]],
                                       o_vmem.at[pl.ds(j * piece, piece)], sem)
                      for j in range(SC_GATHER_SPLIT)]
            for cp in copies:
                cp.wait()

        pltpu.emit_pipeline(
            body,
            grid=(n // SC_WINDOW,),
            in_specs=[pl.BlockSpec((1, SC_WINDOW), lambda i: (0, i))],
            out_specs=[pl.BlockSpec((SC_WINDOW, w), lambda i: (i, 0))],
            core_axis_name=("core", "subcore"),
            dimension_semantics=(pltpu.PARALLEL,),
        )(i_hbm, o_hbm)

    return gather(rows, index_row)


def _experts_kernel(be_ref, x_ref, wg_ref, wu_ref, wd_ref, y_ref, wg_bf, wu_bf, wd_bf):
    b = pl.program_id(0)

    @pl.when((b == 0) | (be_ref[b] != be_ref[jnp.maximum(b - 1, 0)]))
    def _():
        wg_bf[...] = wg_ref[0, 0].astype(BF16)
        wu_bf[...] = wu_ref[0, 0].astype(BF16)
        wd_bf[...] = wd_ref[0, 0].astype(BF16)

    x = _unpack_rows(jnp.concatenate([x_ref[i] for i in range(ROW_PARTS)], axis=1)).astype(BF16)
    hid = (_silu(jnp.dot(x, wg_bf[...], preferred_element_type=F32))
           * jnp.dot(x, wu_bf[...], preferred_element_type=F32))
    packed = _pack_rows(jnp.dot(hid.astype(BF16), wd_bf[...], preferred_element_type=F32))
    dq = packed.shape[1] // ROW_PARTS
    for i in range(ROW_PARTS):
        y_ref[i] = packed[:, i * dq:(i + 1) * dq]


def experts(rows, blk_e, n_valid, layer, w_gate, w_up, w_down):
    parts, cap, dq = rows.shape
    _, _, d, ff = w_gate.shape
    blk = pl.BlockSpec((parts, EXPERT_ROWS, dq), lambda b, be: (0, b, 0))
    return pl.pallas_call(
        _experts_kernel,
        grid_spec=pltpu.PrefetchScalarGridSpec(
            num_scalar_prefetch=1,
            grid=(n_valid[0],),
            in_specs=[blk,
                      pl.BlockSpec((1, 1, d, ff), lambda b, be: (layer, be[b], 0, 0)),
                      pl.BlockSpec((1, 1, d, ff), lambda b, be: (layer, be[b], 0, 0)),
                      pl.BlockSpec((1, 1, ff, d), lambda b, be: (layer, be[b], 0, 0))],
            out_specs=blk,
            scratch_shapes=[pltpu.VMEM((d, ff), BF16), pltpu.VMEM((d, ff), BF16), pltpu.VMEM((ff, d), BF16)],
        ),
        out_shape=jax.ShapeDtypeStruct(rows.shape, U32),
        compiler_params=_params("arbitrary"),
    )(blk_e, rows, w_gate, w_up, w_down)


def _combine_kernel(gate_ref, h_ref, shared_ref, gf_ref, sh_ref, sc_ref, *rest):
    y_refs, outs = rest[:TOP_K], rest[TOP_K:]
    routed = None
    for k in range(TOP_K):
        y = _unpack_rows(jnp.concatenate([y_refs[k][i, 0] for i in range(ROW_PARTS)], axis=1))
        routed = gate_ref[:, k:k + 1] * y if routed is None else routed + gate_ref[:, k:k + 1] * y
    h = h_ref[...] + gf_ref[...] * (routed + shared_ref[...].astype(F32))
    outs[0][...] = h
    if len(outs) > 1:
        outs[1][...] = _rms(h) * (1.0 + sc_ref[...]) + sh_ref[...]


def combine(y_rows, dest, gate_tk, h, shared, gate_ffn, next_shift, next_scale, with_next):
    t, d = h.shape
    parts, cap, dq = y_rows.shape
    n_out = 2 if with_next else 1
    n_choice = dest.shape[0]
    flat = (dest[None] + (jnp.arange(parts, dtype=I32) * cap)[:, None, None]).reshape(1, parts * n_choice * t)
    picked = gather_rows(y_rows.reshape(parts * cap, dq), flat).reshape(parts, n_choice, t, dq)
    tm = min(256, t)
    row = lambda i: (i, 0)
    fixed = lambda i: (0, 0)
    choice = [pl.BlockSpec((parts, 1, tm, dq), functools.partial(lambda i, k: (0, k, i, 0), k=k))
              for k in range(n_choice)]
    return pl.pallas_call(
        _combine_kernel,
        grid=(t // tm,),
        in_specs=[pl.BlockSpec((tm, n_choice), row), pl.BlockSpec((tm, d), row), pl.BlockSpec((tm, d), row),
                  pl.BlockSpec((1, d), fixed), pl.BlockSpec((1, d), fixed), pl.BlockSpec((1, d), fixed)] + choice,
        out_specs=[pl.BlockSpec((tm, d), row)] * n_out,
        out_shape=[jax.ShapeDtypeStruct((t, d), F32)] * n_out,
        compiler_params=_params("parallel"),
    )(gate_tk, h, shared, gate_ffn, next_shift, next_scale, *([picked] * n_choice))


def moe_tail(h, f_rows, logits_t, shared, router_bias, layer, w_gate, w_up, w_down, gate_ffn,
             next_shift, next_scale, with_next):
    t, d = h.shape
    parts, _, dq = f_rows.shape
    n_exp = w_gate.shape[1]
    n_blocks = -(-(t * TOP_K + n_exp * (EXPERT_ROWS - 1)) // EXPERT_ROWS)
    cap = n_blocks * EXPERT_ROWS
    idx, gate, rank, counts = route(logits_t, router_bias)
    dest, blk_e, n_valid = dispatch_plan(idx, rank, counts, n_blocks)
    dest_all = jnp.concatenate([dest + i * cap for i in range(parts)], axis=1)
    rows = dispatch(f_rows.reshape(parts * t, dq), dest_all, parts * cap).reshape(parts, cap, dq)
    y_rows = experts(rows, blk_e, n_valid, layer, w_gate, w_up, w_down)
    return combine(y_rows, dest, gate.T, h, shared, gate_ffn, next_shift, next_scale, with_next)


def kernel(x, c, ctx, c_ctx, w_ada, b_ada, w_in, s5_lam_re, s5_lam_im, s5_log_dt, s5_b_re, s5_b_im,
           s5_c_re, s5_c_im, s5_d, s5_w_glu, s5_b_glu, na_q_gain, na_k_gain, na_rpb, w_mix_out,
           w_fourier_out, b_fourier_out, w_router, router_bias, w_exp_gate, w_exp_up, w_exp_down,
           w_sh_gate, w_sh_up, w_sh_down):
    bsz, t, d = x.shape
    assert bsz == 1 and w_ada.shape[0] == 2
    n_exp = w_router.shape[2]
    s5w = s5_w_glu.shape[1]
    naw = w_in.shape[2] - s5w
    naw //= 3
    heads = naw // NA_HEAD_DIM

    cond8 = jnp.concatenate([c[:1].astype(F32), c_ctx.astype(F32)[None], jnp.zeros((6, d), F32)], axis=0)
    ada = adaln_all(cond8, w_ada, b_ada)
    mod = lambda layer, who, j: ada[layer, who:who + 1, j * d:(j + 1) * d]

    def ffn_weights(i):
        return (mod(i, 0, 3), mod(i, 0, 4), jnp.transpose(w_router[i]).astype(F32),
                w_sh_gate[i].astype(BF16), w_sh_up[i].astype(BF16), w_sh_down[i].astype(BF16))

    h0 = x[0]
    seg = jnp.asarray(np.kron(np.eye(heads), np.ones((NA_HEAD_DIM, NA_HEAD_DIM))), BF16)
    w_in_b = w_in[0].astype(BF16)
    qg = jnp.tile(na_q_gain[0].astype(F32), heads)[None]
    kg = jnp.tile(na_k_gain[0].astype(F32), heads)[None]
    u_c, _, k_c, v_c = in_projection(ctx[0], mod(0, 1, 0), mod(0, 1, 1), w_in_b, seg, qg, kg, s5w, naw)
    u_l, q_l, k_l, v_l = in_projection(h0, mod(0, 0, 0), mod(0, 0, 1), w_in_b, seg, qg, kg, s5w, naw)
    mats = s5_matrices(s5_lam_re[0], s5_lam_im[0], s5_log_dt[0], s5_b_re[0], s5_b_im[0],
                       s5_c_re[0], s5_c_im[0], s5_d[0])
    y_s5 = s5_mixer(u_c, u_l, mats)
    na = neighbourhood_attention(q_l, k_l, v_l, k_c, v_c, na_bias_table(na_rpb[0]))
    h1, f1, lg1, sh1 = _post_call(
        _even_post_kernel, [y_s5, na, h0],
        [s5_w_glu[0].astype(BF16), s5_b_glu[0].astype(F32)[None], w_mix_out[0].astype(BF16), mod(0, 0, 2),
         *ffn_weights(0)], t, d, n_exp)
    h2, a1 = moe_tail(h1, f1, lg1, sh1, router_bias[0], 0, w_exp_gate, w_exp_up, w_exp_down,
                      mod(0, 0, 5), mod(1, 0, 0), mod(1, 0, 1), True)

    zre, zim = time_dft(a1)
    cc, sc = channel_dft_tables(d // FOURIER_GROUPS)
    h3, f3, lg3, sh3 = _post_call(
        _odd_post_kernel, [zre, zim, h2],
        [cc, sc, w_fourier_out[0].astype(BF16), b_fourier_out[0].astype(F32)[None], mod(1, 0, 2),
         *ffn_weights(1)], t, d, n_exp)
    zero_row = jnp.zeros((1, d), F32)
    (out,) = moe_tail(h3, f3, lg3, sh3, router_bias[1], 1, w_exp_gate, w_exp_up, w_exp_down,
                      mod(1, 0, 5), zero_row, zero_row, False)
    return out[None]
```

```python
import functools
import math

import numpy as np
import jax
import jax.numpy as jnp
from jax import lax
from jax.experimental import pallas as pl
from jax.experimental.pallas import tpu as pltpu
from jax.experimental.pallas import tpu_sc as plsc

F32 = jnp.float32
BF16 = jnp.bfloat16
I32 = jnp.int32
U32 = jnp.uint32
HIGHEST = lax.Precision.HIGHEST

MXU_DEPTH = 256
LANES = 128
GRID_W = 64
NORM_EPS = 1e-6
S5_GROUP = 16
S5_STATE = 64
S5_LAMBDA_RE_MAX = -1e-4
S5_CHUNK = 16
S5_TILE = 128
NA_HEADS = 8
NA_HEAD_DIM = 64
NA_KH = 8
NA_KW = 16
NA_STEP_ROWS = 2
FOURIER_GROUPS = 4
N_EXPERT_GROUPS = 8
TOPK_GROUPS = 4
TOP_K = 8
ROUTED_SCALE = 2.5
EXPERT_ROWS = 1024
ROW_PARTS = 2
SC_WINDOW = 128
SC_GATHER_SPLIT = 4
NEG_BIG = -1e30

VMEM_LIMIT_BYTES = 56 * 1024 * 1024


def _params(*sem):
    return pltpu.CompilerParams(dimension_semantics=sem or None,
                                vmem_limit_bytes=VMEM_LIMIT_BYTES)


def _rms(x):
    return x * lax.rsqrt(jnp.mean(x * x, axis=-1, keepdims=True) + NORM_EPS)


def _silu(x):
    return x * jax.nn.sigmoid(x)


def _pack_pair(lo, hi):
    lo = lax.bitcast_convert_type(lo.astype(BF16).astype(F32), U32)
    hi = lax.bitcast_convert_type(hi.astype(BF16).astype(F32), U32)
    return (hi & jnp.uint32(0xFFFF0000)) | (lo >> 16)


def _unpack_pair(w):
    return (lax.bitcast_convert_type(w << 16, F32), lax.bitcast_convert_type(w & jnp.uint32(0xFFFF0000), F32))


def _pack_rows(x):
    n = x.shape[1] // 2
    return _pack_pair(x[:, :n], x[:, n:])


def _unpack_rows(w):
    return jnp.concatenate(_unpack_pair(w), axis=1)


def _ada_kernel(c_ref, w_ref, b_ref, o_ref):
    o_ref[0] = jnp.dot(_silu(c_ref[...]), w_ref[0], preferred_element_type=F32,
                       precision=HIGHEST) + b_ref[0]


def adaln_all(cond8, w_ada, b_ada):
    n_layers, d, n6 = w_ada.shape
    tn = n6 // 4
    return pl.pallas_call(
        _ada_kernel,
        grid=(n_layers, n6 // tn),
        in_specs=[pl.BlockSpec((8, d), lambda l, j: (0, 0)),
                  pl.BlockSpec((1, d, tn), lambda l, j: (l, 0, j)),
                  pl.BlockSpec((1, 1, tn), lambda l, j: (l, 0, j))],
        out_specs=pl.BlockSpec((1, 8, tn), lambda l, j: (l, 0, j)),
        out_shape=jax.ShapeDtypeStruct((n_layers, 8, n6), F32),
        compiler_params=_params("parallel", "parallel"),
    )(cond8, w_ada, b_ada.reshape(n_layers, 1, n6))


def _inproj_kernel(x_ref, sh_ref, sc_ref, w_ref, seg_ref, qg_ref, kg_ref,
                   u_ref, q_ref, k_ref, v_ref):
    a = _rms(x_ref[...]) * (1.0 + sc_ref[...]) + sh_ref[...]
    z = jnp.dot(a.astype(BF16), w_ref[...], preferred_element_type=F32)
    s5w = u_ref.shape[1]
    naw = q_ref.shape[1]

    def head_norm(t, gain):
        ss = jnp.dot((t * t).astype(BF16), seg_ref[...], preferred_element_type=F32)
        return t * lax.rsqrt(ss * (1.0 / NA_HEAD_DIM) + NORM_EPS) * gain

    u_ref[...] = z[:, :s5w]
    q_ref[...] = head_norm(z[:, s5w:s5w + naw], qg_ref[...]).astype(BF16)
    k_ref[...] = head_norm(z[:, s5w + naw:s5w + 2 * naw], kg_ref[...]).astype(BF16)
    v_ref[...] = z[:, s5w + 2 * naw:].astype(BF16)


def in_projection(x, shift, scale, w_in_bf16, seg_ones, q_gain_row, k_gain_row, s5w, naw):
    t, d = x.shape
    tm = min(512, t)
    row = lambda i: (i, 0)
    fixed = lambda i: (0, 0)
    return pl.pallas_call(
        _inproj_kernel,
        grid=(t // tm,),
        in_specs=[pl.BlockSpec((tm, d), row),
                  pl.BlockSpec((1, d), fixed), pl.BlockSpec((1, d), fixed),
                  pl.BlockSpec(w_in_bf16.shape, fixed),
                  pl.BlockSpec(seg_ones.shape, fixed),
                  pl.BlockSpec((1, naw), fixed), pl.BlockSpec((1, naw), fixed)],
        out_specs=[pl.BlockSpec((tm, s5w), row), pl.BlockSpec((tm, naw), row),
                   pl.BlockSpec((tm, naw), row), pl.BlockSpec((tm, naw), row)],
        out_shape=[jax.ShapeDtypeStruct((t, s5w), F32)] + [jax.ShapeDtypeStruct((t, naw), BF16)] * 3,
        compiler_params=_params("parallel"),
    )(x, shift, scale, w_in_bf16, seg_ones, q_gain_row, k_gain_row)


def s5_matrices(lam_re, lam_im, log_dt, b_re, b_im, c_re, c_im, d_skip):
    L = S5_CHUNK
    taus = jnp.arange(L + 1, dtype=F32)

    def direction(i):
        lam = lax.complex(jnp.minimum(lam_re[i].astype(F32), S5_LAMBDA_RE_MAX), lam_im[i].astype(F32))
        ldt = lam * jnp.exp(log_dt[i].astype(F32))[:, None]
        lam_bar = jnp.exp(ldt)
        b_bar = ((lam_bar - 1.0) / lam)[..., None] * lax.complex(b_re[i].astype(F32), b_im[i].astype(F32))
        cc = lax.complex(c_re[i].astype(F32), c_im[i].astype(F32))
        powers = jnp.exp(ldt[None] * taus[:, None, None])
        resp = jnp.real(jnp.einsum('gcp,tgp,gpd->gtcd', cc, powers[:L], b_bar, precision=HIGHEST))
        return powers, b_bar, cc, resp

    pw_f, bb_f, cc_f, k_f = direction(0)
    pw_b, bb_b, cc_b, k_b = direction(1)
    g, p = pw_f.shape[1:]
    c = S5_GROUP
    diff = np.arange(L)[None, :] - np.arange(L)[:, None]
    lag = np.arange(L)[:, None, None]
    tf = jnp.einsum('tsl,gtcd->gslcd', jnp.asarray(lag == diff[None], F32), k_f, precision=HIGHEST)
    tb = jnp.einsum('tsl,gtcd->gslcd', jnp.asarray(lag == -diff[None], F32), k_b, precision=HIGHEST)
    skip = (jnp.eye(L, dtype=F32)[None, :, :, None, None]
            * (jnp.eye(c, dtype=F32)[None] * d_skip.astype(F32)[:, :, None])[:, None, None])
    toep = jnp.transpose(tf + tb + skip, (0, 1, 4, 2, 3)).reshape(g, L * c, L * c)

    def state_in(powers_sel, b_bar):
        w = powers_sel[:, :, :, None] * b_bar[None]
        return jnp.transpose(w, (1, 0, 3, 2)).reshape(g, L * c, p)

    wf = state_in(pw_f[L - 1 - jnp.arange(L)], bb_f)
    wb = state_in(pw_b[jnp.arange(L)], bb_b)
    w_state = jnp.concatenate([jnp.real(wf), jnp.imag(wf), jnp.imag(wf), jnp.real(wf),
                               jnp.real(wb), jnp.imag(wb), jnp.imag(wb), jnp.real(wb)], axis=-1)

    def state_out(powers_sel, cc):
        r = powers_sel[:, :, None, :] * cc[None]
        return jnp.transpose(r, (1, 3, 0, 2)).reshape(g, p, L * c)

    rf = state_out(pw_f[1 + jnp.arange(L)], cc_f)
    rb = state_out(pw_b[L - jnp.arange(L)], cc_b)
    r_state = jnp.concatenate([jnp.real(rf), -jnp.imag(rf), jnp.real(rb), -jnp.imag(rb)], axis=1)

    def mult(a):
        ar, ai = jnp.real(a), jnp.imag(a)
        return jnp.stack([jnp.concatenate([ar, ar], -1), jnp.concatenate([-ai, ai], -1),
                          jnp.concatenate([ai, -ai], -1)])

    return toep, w_state, r_state, mult(pw_f[L]), mult(pw_b[L])


def _s5_pack_kernel(*refs):
    u_refs, (wt_ref, ut_ref, f1_ref, f2_ref, b1_ref, b2_ref) = refs[:-6], refs[-6:]
    L = S5_CHUNK
    g, lc, nck = ut_ref.shape
    c = lc // L
    gs = g // len(u_refs)
    for s in range(L):
        for j, u_ref in enumerate(u_refs):
            step_s = u_ref[pl.ds(s, nck, stride=L), :]
            ut_ref[j * gs:(j + 1) * gs, s * c:(s + 1) * c, :] = (
                jnp.transpose(step_s).astype(BF16).reshape(gs, c, nck))
    n = f1_ref.shape[2]
    for gi in range(g):
        inc = jnp.dot(wt_ref[gi], ut_ref[gi], preferred_element_type=F32)
        for j, ref in enumerate((f1_ref, f2_ref, b1_ref, b2_ref)):
            ref[:, gi, :] = jnp.transpose(inc[j * n:(j + 1) * n, :])


def s5_pack(u, wt_state):
    t, w = u.shape
    g, n4, lc = wt_state.shape
    nc = t // S5_CHUNK
    tile = min(S5_TILE, nc)
    inc = jax.ShapeDtypeStruct((nc, g, n4 // 4), F32)
    inc_blk = pl.BlockSpec((tile, g, n4 // 4), lambda i: (i, 0, 0))
    return pl.pallas_call(
        _s5_pack_kernel,
        grid=(nc // tile,),
        in_specs=[pl.BlockSpec((tile * S5_CHUNK, LANES), functools.partial(lambda i, j: (i, j), j=j))
                  for j in range(w // LANES)] + [pl.BlockSpec(wt_state.shape, lambda i: (0, 0, 0))],
        out_specs=[pl.BlockSpec((g, lc, tile), lambda i: (0, 0, i))] + [inc_blk] * 4,
        out_shape=[jax.ShapeDtypeStruct((g, lc, nc), BF16)] + [inc] * 4,
        compiler_params=_params("parallel"),
    )(*([u] * (w // LANES)), wt_state)


def _s5_scan_kernel(s1_ref, s2_ref, m_ref, init_ref, x_ref, last_ref, v1_ref, v2_ref, *, reverse):
    @pl.when(pl.program_id(0) == 0)
    def _():
        v1_ref[...] = init_ref[0]
        v2_ref[...] = init_ref[1]

    a1, a2, a3 = m_ref[0], m_ref[1], m_ref[2]
    cb = s1_ref.shape[0]

    def body(j, carry):
        v1, v2 = carry
        jj = cb - 1 - j if reverse else j
        x_ref[jj] = v1
        return (a1 * v1 + a2 * v2 + s1_ref[jj], a1 * v2 + a3 * v1 + s2_ref[jj])

    v1, v2 = lax.fori_loop(0, cb, body, (v1_ref[...], v2_ref[...]))
    v1_ref[...] = v1
    v2_ref[...] = v2
    last_ref[...] = v1


def s5_chunk_scan(s1, s2, mult, init, reverse):
    nc, g, n = s1.shape
    cb = min(S5_TILE, nc)
    nb = nc // cb
    blk = (lambda i: (nb - 1 - i, 0, 0)) if reverse else (lambda i: (i, 0, 0))
    return pl.pallas_call(
        functools.partial(_s5_scan_kernel, reverse=reverse),
        grid=(nb,),
        in_specs=[pl.BlockSpec((cb, g, n), blk), pl.BlockSpec((cb, g, n), blk),
                  pl.BlockSpec((3, g, n), lambda i: (0, 0, 0)), pl.BlockSpec((2, g, n), lambda i: (0, 0, 0))],
        out_specs=[pl.BlockSpec((cb, g, n), blk), pl.BlockSpec((g, n), lambda i: (0, 0))],
        out_shape=[jax.ShapeDtypeStruct((nc, g, n), F32), jax.ShapeDtypeStruct((g, n), F32)],
        scratch_shapes=[pltpu.VMEM((g, n), F32), pltpu.VMEM((g, n), F32)],
        compiler_params=_params("arbitrary"),
    )(s1, s2, mult, init)


def _s5_readout_kernel(ut_ref, tt_ref, rt_ref, xf_ref, xb_ref, y_ref, yt_ref, *slab_refs):
    L = S5_CHUNK
    g, lc, nck = ut_ref.shape
    c = lc // L
    gs = g // len(slab_refs)
    for gi in range(g):
        xin_t = jnp.concatenate([jnp.transpose(xf_ref[:, gi, :]), jnp.transpose(xb_ref[:, gi, :])], axis=0)
        yt_ref[gi] = (jnp.dot(tt_ref[gi], ut_ref[gi], preferred_element_type=F32)
                      + jnp.dot(rt_ref[gi], xin_t.astype(BF16), preferred_element_type=F32))
    for j, slab in enumerate(slab_refs):
        for l in range(L):
            step_l = yt_ref[j * gs:(j + 1) * gs, l * c:(l + 1) * c, :].reshape(gs * c, nck)
            slab[pl.ds(l, nck, stride=L), :] = jnp.transpose(step_l)
        y_ref[:, j * gs * c:(j + 1) * gs * c] = slab[...]


def s5_readout(ut, toep_t, r_state_t, xin_f, xin_b):
    g, lc, nc = ut.shape
    n = xin_f.shape[2]
    tile = min(S5_TILE, nc)
    fixed = lambda i: (0, 0, 0)
    state_blk = pl.BlockSpec((tile, g, n), lambda i: (i, 0, 0))
    return pl.pallas_call(
        _s5_readout_kernel,
        grid=(nc // tile,),
        in_specs=[pl.BlockSpec((g, lc, tile), lambda i: (0, 0, i)),
                  pl.BlockSpec(toep_t.shape, fixed), pl.BlockSpec(r_state_t.shape, fixed), state_blk, state_blk],
        out_specs=pl.BlockSpec((tile * S5_CHUNK, g * lc // S5_CHUNK), lambda i: (i, 0)),
        out_shape=jax.ShapeDtypeStruct((nc * S5_CHUNK, g * lc // S5_CHUNK), F32),
        scratch_shapes=[pltpu.VMEM((g, lc, tile), F32)]
        + [pltpu.VMEM((tile * S5_CHUNK, LANES), F32)] * (g * lc // S5_CHUNK // LANES),
        compiler_params=_params("parallel"),
    )(ut, toep_t, r_state_t, xin_f, xin_b)


def s5_mixer(u_ctx, u_lat, mats):
    toep, w_state, r_state, mult_f, mult_b = mats
    L = S5_CHUNK
    g, n = mult_f.shape[1:]
    swap = lambda a: jnp.transpose(a, (0, 2, 1)).astype(BF16)
    wt_state, toep_t, r_state_t = swap(w_state), swap(toep), swap(r_state)
    halves = lambda v: jnp.stack([v, jnp.roll(v, n // 2, axis=-1)])
    n_ctx = u_ctx.shape[0] // L
    ctx_chunks = -(-(n_ctx + 1) // S5_TILE) * S5_TILE
    ctx_pad = jnp.pad(u_ctx, ((0, ctx_chunks * L - u_ctx.shape[0]), (0, 0)))
    _, cf1, cf2, cb1, cb2 = s5_pack(ctx_pad, wt_state)
    zero = jnp.zeros((2, g, n), F32)
    ctx_f, _ = s5_chunk_scan(cf1, cf2, mult_f, zero, False)
    _, ctx_b_last = s5_chunk_scan(cb1, cb2, mult_b, zero, True)
    ut, f1, f2, b1, b2 = s5_pack(u_lat, wt_state)
    xin_f, _ = s5_chunk_scan(f1, f2, mult_f, halves(ctx_f[n_ctx]), False)
    xin_b, _ = s5_chunk_scan(b1, b2, mult_b, halves(ctx_b_last), True)
    return s5_readout(ut, toep_t, r_state_t, xin_f, xin_b)


def _na_window(rows):
    step = NA_STEP_ROWS
    assert rows % step == 0 and (NA_KH // 2) % step == 0 and rows >= NA_KH + step
    groups = rows // step
    n_blk = (NA_KH + step) // step
    first = np.clip(np.arange(groups) - NA_KH // 2 // step, 0, groups - n_blk)
    return groups, n_blk, first


def na_bias_table(rpb, rows):
    step = NA_STEP_ROWS
    groups, n_blk, first = _na_window(rows)
    win = n_blk * step
    q_col = np.arange(GRID_W)
    col_start = np.clip(q_col - NA_KW // 2, 0, GRID_W - NA_KW)
    key_col = np.arange(GRID_W)
    off = key_col[None, :] - col_start[:, None]
    valid_col = (off >= 0) & (off < NA_KW)
    rel_col = np.clip(key_col[None, :] - q_col[:, None] + NA_KW - 1, 0, 2 * NA_KW - 2)
    q_row = step * np.arange(groups)[:, None, None] + np.arange(step)[None, :, None]
    key_row = step * first[:, None, None] + np.arange(win)[None, None, :]
    row0 = np.clip(q_row - NA_KH // 2, 0, rows - NA_KH)
    rel_row = np.where((key_row >= row0) & (key_row < row0 + NA_KH), key_row - q_row + NA_KH - 1, -1)
    variant = np.arange(groups) - first
    reps = [int(np.argmax(variant == d)) for d in range(n_blk)]
    assert all((rel_row[g] == rel_row[reps[variant[g]]]).all() for g in range(groups))
    rel_row = rel_row[reps]
    pick_row = jnp.asarray(np.arange(2 * NA_KH - 1)[:, None, None, None] == rel_row[None], F32)
    pick_col = jnp.asarray(np.arange(2 * NA_KW - 1)[:, None, None] == rel_col[None], F32)
    table = jnp.einsum('hrx,rvji,xck->vhjcik', rpb.astype(F32), pick_row, pick_col, precision=HIGHEST)
    valid = (rel_row >= 0)[:, None, :, None, :, None] & valid_col[None, None, None, :, None, :]
    return jnp.where(valid, table, NEG_BIG).reshape(n_blk, rpb.shape[0], step * GRID_W, win * GRID_W)


def _na_kernel(*refs, scale, n_blk):
    q_ref = refs[0]
    k_refs = refs[1:1 + n_blk]
    v_refs = refs[1 + n_blk:1 + 2 * n_blk]
    kc_ref, vc_ref, b_ref, o_ref = refs[1 + 2 * n_blk:]
    hd = NA_HEAD_DIM
    width = q_ref.shape[1]
    span = min(MXU_DEPTH, width)
    nt = (((1,), (1,)), ((), ()))
    q = q_ref[...] * scale
    kk = jnp.concatenate([r[...] for r in k_refs], axis=0)
    vv = jnp.concatenate([r[...] for r in v_refs], axis=0)
    kc, vc = kc_ref[...], vc_ref[...]
    nq = q.shape[0]
    per = span // hd
    lane = lax.broadcasted_iota(I32, (nq, span), 1)
    own = [(lane >= j * hd) & (lane < (j + 1) * hd) for j in range(per)]
    out_cols = []
    for c0 in range(0, width, span):
        cols = slice(c0, c0 + span)
        qs = jnp.concatenate([jnp.where(own[j], q[:, cols], jnp.zeros_like(q[:, cols])) for j in range(per)], axis=0)
        h0 = c0 // hd
        bias = b_ref[0, h0:h0 + per].reshape(per * nq, kk.shape[0])
        s = lax.dot_general(qs, kk[:, cols], nt, preferred_element_type=F32) + bias
        sc = lax.dot_general(qs, kc[:, cols], nt, preferred_element_type=F32)
        m = jnp.maximum(jnp.max(s, axis=-1, keepdims=True), jnp.max(sc, axis=-1, keepdims=True))
        p = jnp.exp(s - m)
        pc = jnp.exp(sc - m)
        den = jnp.sum(p, axis=-1, keepdims=True) + jnp.sum(pc, axis=-1, keepdims=True)
        o = (jnp.dot(p.astype(BF16), vv[:, cols], preferred_element_type=F32)
             + jnp.dot(pc.astype(BF16), vc[:, cols], preferred_element_type=F32)) / den
        acc = jnp.zeros((nq, span), F32)
        for j in range(per):
            acc = jnp.where(own[j], o[j * nq:(j + 1) * nq], acc)
        out_cols.append(acc)
    o_ref[...] = jnp.concatenate(out_cols, axis=1).astype(o_ref.dtype)


def neighbourhood_attention(q, k, v, k_ctx, v_ctx, bias_table):
    t, w = q.shape
    groups, n_blk, _ = _na_window(t // GRID_W)
    first = lambda g: jnp.clip(g - NA_KH // 2 // NA_STEP_ROWS, 0, groups - n_blk)
    blk_tokens = NA_STEP_ROWS * GRID_W
    row_blk = pl.BlockSpec((blk_tokens, w), lambda g: (g, 0))
    key_blks = [pl.BlockSpec((blk_tokens, w), functools.partial(lambda g, i: (first(g) + i, 0), i=i))
                for i in range(n_blk)]
    ctx_blk = pl.BlockSpec(k_ctx.shape, lambda g: (0, 0))
    bias_blk = pl.BlockSpec((1,) + bias_table.shape[1:], lambda g: (g - first(g), 0, 0, 0))
    return pl.pallas_call(
        functools.partial(_na_kernel, scale=NA_HEAD_DIM ** -0.5, n_blk=n_blk),
        grid=(groups,),
        in_specs=[row_blk] + key_blks + key_blks + [ctx_blk, ctx_blk, bias_blk],
        out_specs=row_blk,
        out_shape=jax.ShapeDtypeStruct((t, w), BF16),
        compiler_params=_params("parallel"),
    )(q, *([k] * n_blk), *([v] * n_blk), k_ctx, v_ctx, bias_table)


def time_dft_tables(t):
    a_len = 1 << (int(math.log2(t)) // 2)
    b_len = t // a_len
    ka = np.arange(a_len)[:, None]
    tok = b_len * np.arange(a_len)[None, :]
    ang1 = -2.0 * np.pi * ((ka * (tok[None] + np.arange(b_len)[:, None, None])) % t) / t
    stage1 = np.concatenate([np.cos(ang1), np.sin(ang1)], axis=1) / math.sqrt(t)
    ang2 = 2.0 * np.pi * ((np.arange(b_len)[:, None] * np.arange(b_len)[None, :]) % b_len) / b_len
    c2, s2 = np.cos(ang2), np.sin(ang2)
    stage2 = np.block([[c2, s2], [-s2, c2]])
    return jnp.asarray(stage1, BF16), jnp.asarray(stage2, BF16), a_len, b_len


def _time_dft_kernel(x_ref, m_ref, w2_ref, zre_ref, zim_ref, y_ref, z_ref, *, a_len, b_len):
    i = pl.program_id(1)
    bb = m_ref.shape[0]

    def stage1(jb, carry):
        b = i * bb + jb
        xb = x_ref[pl.ds(b, a_len, stride=b_len), :]
        y = jnp.dot(m_ref[jb], xb.astype(BF16), preferred_element_type=F32)
        row = pl.multiple_of(b * a_len, a_len)
        y_ref[pl.ds(row, a_len), :] = _pack_pair(y[:a_len], y[a_len:])
        return carry

    lax.fori_loop(0, bb, stage1, 0, unroll=4)

    @pl.when(i == pl.num_programs(1) - 1)
    def _():
        def stage2(ka, carry):
            rows = pl.ds(ka, b_len, stride=a_len)
            yre, yim = _unpack_pair(y_ref[rows, :])
            y = jnp.concatenate([yre, yim], axis=0).astype(BF16)
            z = jnp.dot(w2_ref[...], y, preferred_element_type=F32)
            z_ref[rows, :] = _pack_pair(z[:b_len], z[b_len:])
            return carry

        lax.fori_loop(0, a_len, stage2, 0, unroll=4)
        zre, zim = _unpack_pair(z_ref[...])
        zre_ref[...] = zre.astype(zre_ref.dtype)
        zim_ref[...] = zim.astype(zim_ref.dtype)


def time_dft(x):
    t, d = x.shape
    stage1, stage2, a_len, b_len = time_dft_tables(t)
    lanes = 128
    bb = min(16, b_len)
    out = jax.ShapeDtypeStruct((t, d), BF16)
    return pl.pallas_call(
        functools.partial(_time_dft_kernel, a_len=a_len, b_len=b_len),
        grid=(d // lanes, b_len // bb),
        in_specs=[pl.BlockSpec((t, lanes), lambda j, i: (0, j)),
                  pl.BlockSpec((bb, 2 * a_len, a_len), lambda j, i: (i, 0, 0)),
                  pl.BlockSpec(stage2.shape, lambda j, i: (0, 0))],
        out_specs=[pl.BlockSpec((t, lanes), lambda j, i: (0, j))] * 2,
        out_shape=[out, out],
        scratch_shapes=[pltpu.VMEM((t, lanes), U32), pltpu.VMEM((t, lanes), U32)],
        compiler_params=_params("parallel", "arbitrary"),
    )(x, stage1, stage2)


def channel_dft_tables(c):
    ang = 2.0 * np.pi * ((np.arange(c)[:, None] * np.arange(c)[None, :]) % c) / c
    return (jnp.asarray(np.cos(ang) / math.sqrt(c), BF16), jnp.asarray(np.sin(ang) / math.sqrt(c), BF16))


def _ffn_prologue(h, shf_ref, scf_ref, wr_ref, wsg_ref, wsu_ref, wsd_ref, h_ref, f_ref, lg_ref, shared_ref):
    h_ref[...] = h
    f = _rms(h) * (1.0 + scf_ref[...]) + shf_ref[...]
    packed = _pack_rows(f)
    half = packed.shape[1] // 2
    f_ref[0] = packed[:, :half]
    f_ref[1] = packed[:, half:]
    fb = f.astype(BF16)
    f_lo = (f - fb.astype(F32)).astype(BF16)
    nt = (((1,), (1,)), ((), ()))
    wr = wr_ref[...]
    wr_hi = wr.astype(BF16)
    wr_lo = (wr - wr_hi.astype(F32)).astype(BF16)
    lg_ref[...] = (lax.dot_general(wr_hi, fb, nt, preferred_element_type=F32)
                   + lax.dot_general(wr_hi, f_lo, nt, preferred_element_type=F32)
                   + lax.dot_general(wr_lo, fb, nt, preferred_element_type=F32))
    hid = (_silu(jnp.dot(fb, wsg_ref[...], preferred_element_type=F32))
           * jnp.dot(fb, wsu_ref[...], preferred_element_type=F32))
    shared_ref[...] = jnp.dot(hid.astype(BF16), wsd_ref[...], preferred_element_type=F32).astype(shared_ref.dtype)


def _gelu_tanh(x):
    return 0.5 * x * (1.0 + jnp.tanh(math.sqrt(2.0 / math.pi) * (x + 0.044715 * (x * x * x))))


def _even_post_kernel(y_ref, na_ref, x_ref, wglu_ref, bglu_ref, wo_ref, gm_ref, *rest):
    g = _gelu_tanh(y_ref[...])
    gate = jax.nn.sigmoid(jnp.dot(g.astype(BF16), wglu_ref[...], preferred_element_type=F32) + bglu_ref[...])
    s5 = (g * gate).astype(BF16)
    w = s5.shape[1]
    mix = (jnp.dot(s5, wo_ref[:w, :], preferred_element_type=F32)
           + jnp.dot(na_ref[...], wo_ref[w:, :], preferred_element_type=F32))
    _ffn_prologue(x_ref[...] + gm_ref[...] * mix, *rest)


def _odd_post_kernel(zre_ref, zim_ref, h_ref_in, cc_ref, sc_ref, wf_ref, bf_ref, gm_ref, *rest):
    c = cc_ref.shape[0]
    parts = []
    for grp in range(zre_ref.shape[1] // c):
        cols = slice(grp * c, (grp + 1) * c)
        parts.append(jnp.dot(zre_ref[:, cols], cc_ref[...], preferred_element_type=F32)
                     + jnp.dot(zim_ref[:, cols], sc_ref[...], preferred_element_type=F32))
    fr = jnp.concatenate(parts, axis=-1).astype(BF16)
    mix = jnp.dot(fr, wf_ref[...], preferred_element_type=F32) + bf_ref[...]
    _ffn_prologue(h_ref_in[...] + gm_ref[...] * mix, *rest)


def _post_call(body, row_inputs, fixed_inputs, t, d, n_exp):
    tm = min(512, t)
    row = lambda i: (i, 0)
    in_specs = ([pl.BlockSpec((tm, a.shape[1]), row) for a in row_inputs]
                + [pl.BlockSpec(a.shape, functools.partial(lambda i, nd: (0,) * nd, nd=a.ndim))
                   for a in fixed_inputs])
    return pl.pallas_call(
        body,
        grid=(t // tm,),
        in_specs=in_specs,
        out_specs=[pl.BlockSpec((tm, d), row), pl.BlockSpec((ROW_PARTS, tm, d // 4), lambda i: (0, i, 0)),
                   pl.BlockSpec((n_exp, tm), lambda i: (0, i)), pl.BlockSpec((tm, d), row)],
        out_shape=[jax.ShapeDtypeStruct((t, d), F32), jax.ShapeDtypeStruct((ROW_PARTS, t, d // 4), U32),
                   jax.ShapeDtypeStruct((n_exp, t), F32), jax.ShapeDtypeStruct((t, d), BF16)],
        compiler_params=_params("parallel"),
    )(*row_inputs, *fixed_inputs)


def _route_kernel(lg_ref, bias_ref, tri_ref, idx_ref, gate_ref, rank_ref, cnt_ref, run_ref):
    @pl.when(pl.program_id(0) == 0)
    def _():
        run_ref[...] = jnp.zeros_like(run_ref)

    scores = jax.nn.sigmoid(lg_ref[...])
    n_exp, tb = scores.shape
    sel = scores + bias_ref[...]
    gsz = n_exp // N_EXPERT_GROUPS
    member = lax.broadcasted_iota(I32, (gsz, tb), 0)
    gscore = []
    for grp in range(N_EXPERT_GROUPS):
        xg = sel[grp * gsz:(grp + 1) * gsz, :]
        m1 = jnp.max(xg, axis=0, keepdims=True)
        first = jnp.min(jnp.where(xg == m1, member, gsz), axis=0, keepdims=True)
        m2 = jnp.max(jnp.where(member == first, -jnp.inf, xg), axis=0, keepdims=True)
        gscore.append(m1 + m2)
    keep_rows = []
    for grp in range(N_EXPERT_GROUPS):
        beaten = jnp.zeros((1, tb), F32)
        for other in range(N_EXPERT_GROUPS):
            if other == grp:
                continue
            wins = (gscore[other] >= gscore[grp]) if other < grp else (gscore[other] > gscore[grp])
            beaten = beaten + jnp.where(wins, 1.0, 0.0)
        keep_rows.append(jnp.broadcast_to(beaten < TOPK_GROUPS, (gsz, tb)))
    masked = jnp.where(jnp.concatenate(keep_rows, axis=0), sel, -jnp.inf)

    expert = lax.broadcasted_iota(I32, (n_exp, tb), 0)
    picks, gates, hots = [], [], []
    chosen = jnp.zeros((n_exp, tb), F32)
    for _ in range(TOP_K):
        m = jnp.max(masked, axis=0, keepdims=True)
        pick = jnp.min(jnp.where(masked == m, expert, n_exp), axis=0, keepdims=True)
        hot = expert == pick
        picks.append(pick)
        hots.append(hot)
        gates.append(jnp.sum(jnp.where(hot, scores, 0.0), axis=0, keepdims=True))
        chosen = jnp.where(hot, 1.0, chosen)
        masked = jnp.where(hot, -jnp.inf, masked)
    total = gates[0]
    for gk in gates[1:]:
        total = total + gk
    ahead = jnp.dot(chosen.astype(BF16), tri_ref[...], preferred_element_type=F32) + run_ref[...]
    for k in range(TOP_K):
        idx_ref[k:k + 1, :] = picks[k]
        gate_ref[k:k + 1, :] = ROUTED_SCALE * gates[k] / total
        rank_ref[k:k + 1, :] = jnp.sum(jnp.where(hots[k], ahead, 0.0), axis=0, keepdims=True).astype(I32)
    run_ref[...] = run_ref[...] + jnp.sum(chosen, axis=1, keepdims=True)
    cnt_ref[...] = jnp.broadcast_to(run_ref[...], cnt_ref.shape)


def route(logits_t, router_bias):
    n_exp, t = logits_t.shape
    tb = min(512, t)
    tri = jnp.asarray(np.triu(np.ones((tb, tb), np.float32), k=1), BF16)
    tok = lambda i: (0, i)
    idx, gate, rank, cnt = pl.pallas_call(
        _route_kernel,
        grid=(t // tb,),
        in_specs=[pl.BlockSpec((n_exp, tb), tok), pl.BlockSpec((n_exp, 1), lambda i: (0, 0)),
                  pl.BlockSpec((tb, tb), lambda i: (0, 0))],
        out_specs=[pl.BlockSpec((TOP_K, tb), tok)] * 3 + [pl.BlockSpec((n_exp, 128), lambda i: (0, 0))],
        out_shape=[jax.ShapeDtypeStruct((TOP_K, t), I32), jax.ShapeDtypeStruct((TOP_K, t), F32),
                   jax.ShapeDtypeStruct((TOP_K, t), I32), jax.ShapeDtypeStruct((n_exp, 128), F32)],
        scratch_shapes=[pltpu.VMEM((n_exp, 1), F32)],
        compiler_params=_params("arbitrary"),
    )(logits_t, router_bias.astype(F32).reshape(n_exp, 1), tri)
    return idx, gate, rank, cnt[:, 0].astype(I32)


def dispatch_plan(idx, rank, counts, n_blocks):
    n_exp = counts.shape[0]
    padded = (counts + EXPERT_ROWS - 1) // EXPERT_ROWS * EXPERT_ROWS
    pad_end = jnp.cumsum(padded)
    pad_start = pad_end - padded
    experts_iota = jnp.arange(n_exp, dtype=I32)
    dest = jnp.sum(jnp.where(idx[..., None] == experts_iota, pad_start, 0), axis=-1) + rank
    n_valid = (pad_end[-1] // EXPERT_ROWS).astype(I32)
    blk = jnp.minimum(jnp.arange(n_blocks, dtype=I32), n_valid - 1)
    blk_e = jnp.sum((pad_end[None, :] <= blk[:, None] * EXPERT_ROWS).astype(I32), axis=1)
    return dest.astype(I32), jnp.minimum(blk_e, n_exp - 1).astype(I32), n_valid.reshape(1)


def _sc_mesh():
    return plsc.VectorSubcoreMesh(core_axis_name="core", subcore_axis_name="subcore")


def dispatch(f, dest, cap):
    t, w = f.shape
    n_choice = dest.shape[0]

    @functools.partial(pl.kernel, out_type=jax.ShapeDtypeStruct((cap, w), f.dtype), mesh=_sc_mesh(),
                       scratch_types=[pltpu.SemaphoreType.DMA])
    def scatter_rows(x_hbm, i_hbm, o_hbm, sem):
        def body(x_vmem, i_vmem):
            copies = [pltpu.async_copy(x_vmem, o_hbm.at[i_vmem.at[k]], sem) for k in range(n_choice)]
            for cp in copies:
                cp.wait()

        pltpu.emit_pipeline(
            body,
            grid=(t // SC_WINDOW,),
            in_specs=[pl.BlockSpec((SC_WINDOW, w), lambda i: (i, 0)),
                      pl.BlockSpec((n_choice, SC_WINDOW), lambda i: (0, i))],
            out_specs=[],
            core_axis_name=("core", "subcore"),
            dimension_semantics=(pltpu.PARALLEL,),
        )(x_hbm, i_hbm)

    return scatter_rows(f, dest)


def gather_rows(rows, index_row):
    n = index_row.shape[1]
    w = rows.shape[1]

    piece = SC_WINDOW // SC_GATHER_SPLIT

    @functools.partial(pl.kernel, out_type=jax.ShapeDtypeStruct((n, w), rows.dtype), mesh=_sc_mesh(),
                       scratch_types=[pltpu.SemaphoreType.DMA])
    def gather(y_hbm, i_hbm, o_hbm, sem):
        def body(i_vmem, o_vmem):
            copies = [pltpu.async_copy(y_hbm.at[i_vmem.at[0, pl.ds(j * piece, piece)]],
                                       o_vmem.at[pl.ds(j * piece, piece)], sem)
                      for j in range(SC_GATHER_SPLIT)]
            for cp in copies:
                cp.wait()

        pltpu.emit_pipeline(
            body,
            grid=(n // SC_WINDOW,),
            in_specs=[pl.BlockSpec((1, SC_WINDOW), lambda i: (0, i))],
            out_specs=[pl.BlockSpec((SC_WINDOW, w), lambda i: (i, 0))],
            core_axis_name=("core", "subcore"),
            dimension_semantics=(pltpu.PARALLEL,),
        )(i_hbm, o_hbm)

    return gather(rows, index_row)


def _experts_kernel(be_ref, x_ref, wg_ref, wu_ref, wd_ref, y_ref, wg_bf, wu_bf, wd_bf):
    b = pl.program_id(0)

    @pl.when((b == 0) | (be_ref[b] != be_ref[jnp.maximum(b - 1, 0)]))
    def _():
        wg_bf[...] = wg_ref[0, 0].astype(BF16)
        wu_bf[...] = wu_ref[0, 0].astype(BF16)
        wd_bf[...] = wd_ref[0, 0].astype(BF16)

    x = _unpack_rows(jnp.concatenate([x_ref[i] for i in range(ROW_PARTS)], axis=1)).astype(BF16)
    hid = (_silu(jnp.dot(x, wg_bf[...], preferred_element_type=F32))
           * jnp.dot(x, wu_bf[...], preferred_element_type=F32))
    packed = _pack_rows(jnp.dot(hid.astype(BF16), wd_bf[...], preferred_element_type=F32))
    dq = packed.shape[1] // ROW_PARTS
    for i in range(ROW_PARTS):
        y_ref[i] = packed[:, i * dq:(i + 1) * dq]


def experts(rows, blk_e, n_valid, layer, w_gate, w_up, w_down):
    parts, cap, dq = rows.shape
    _, _, d, ff = w_gate.shape
    blk = pl.BlockSpec((parts, EXPERT_ROWS, dq), lambda b, be: (0, b, 0))
    return pl.pallas_call(
        _experts_kernel,
        grid_spec=pltpu.PrefetchScalarGridSpec(
            num_scalar_prefetch=1,
            grid=(n_valid[0],),
            in_specs=[blk,
                      pl.BlockSpec((1, 1, d, ff), lambda b, be: (layer, be[b], 0, 0)),
                      pl.BlockSpec((1, 1, d, ff), lambda b, be: (layer, be[b], 0, 0)),
                      pl.BlockSpec((1, 1, ff, d), lambda b, be: (layer, be[b], 0, 0))],
            out_specs=blk,
            scratch_shapes=[pltpu.VMEM((d, ff), BF16), pltpu.VMEM((d, ff), BF16), pltpu.VMEM((ff, d), BF16)],
        ),
        out_shape=jax.ShapeDtypeStruct(rows.shape, U32),
        compiler_params=_params("arbitrary"),
    )(blk_e, rows, w_gate, w_up, w_down)


def _combine_kernel(gate_ref, h_ref, shared_ref, gf_ref, sh_ref, sc_ref, *rest):
    y_refs, outs = rest[:TOP_K], rest[TOP_K:]
    routed = None
    for k in range(TOP_K):
        y = _unpack_rows(jnp.concatenate([y_refs[k][i, 0] for i in range(ROW_PARTS)], axis=1))
        routed = gate_ref[:, k:k + 1] * y if routed is None else routed + gate_ref[:, k:k + 1] * y
    h = h_ref[...] + gf_ref[...] * (routed + shared_ref[...].astype(F32))
    outs[0][...] = h
    if len(outs) > 1:
        outs[1][...] = _rms(h) * (1.0 + sc_ref[...]) + sh_ref[...]


def combine(y_rows, dest, gate_tk, h, shared, gate_ffn, next_shift, next_scale, with_next):
    t, d = h.shape
    parts, cap, dq = y_rows.shape
    n_out = 2 if with_next else 1
    n_choice = dest.shape[0]
    flat = (dest[None] + (jnp.arange(parts, dtype=I32) * cap)[:, None, None]).reshape(1, parts * n_choice * t)
    picked = gather_rows(y_rows.reshape(parts * cap, dq), flat).reshape(parts, n_choice, t, dq)
    tm = min(256, t)
    row = lambda i: (i, 0)
    fixed = lambda i: (0, 0)
    choice = [pl.BlockSpec((parts, 1, tm, dq), functools.partial(lambda i, k: (0, k, i, 0), k=k))
              for k in range(n_choice)]
    return pl.pallas_call(
        _combine_kernel,
        grid=(t // tm,),
        in_specs=[pl.BlockSpec((tm, n_choice), row), pl.BlockSpec((tm, d), row), pl.BlockSpec((tm, d), row),
                  pl.BlockSpec((1, d), fixed), pl.BlockSpec((1, d), fixed), pl.BlockSpec((1, d), fixed)] + choice,
        out_specs=[pl.BlockSpec((tm, d), row)] * n_out,
        out_shape=[jax.ShapeDtypeStruct((t, d), F32)] * n_out,
        compiler_params=_params("parallel"),
    )(gate_tk, h, shared, gate_ffn, next_shift, next_scale, *([picked] * n_choice))


def moe_tail(h, f_rows, logits_t, shared, router_bias, layer, w_gate, w_up, w_down, gate_ffn,
             next_shift, next_scale, with_next):
    t, d = h.shape
    parts, _, dq = f_rows.shape
    n_exp = w_gate.shape[1]
    n_blocks = -(-(t * TOP_K + n_exp * (EXPERT_ROWS - 1)) // EXPERT_ROWS)
    cap = n_blocks * EXPERT_ROWS
    idx, gate, rank, counts = route(logits_t, router_bias)
    dest, blk_e, n_valid = dispatch_plan(idx, rank, counts, n_blocks)
    dest_all = jnp.concatenate([dest + i * cap for i in range(parts)], axis=1)
    rows = dispatch(f_rows.reshape(parts * t, dq), dest_all, parts * cap).reshape(parts, cap, dq)
    y_rows = experts(rows, blk_e, n_valid, layer, w_gate, w_up, w_down)
    return combine(y_rows, dest, gate.T, h, shared, gate_ffn, next_shift, next_scale, with_next)


def kernel(x, c, ctx, c_ctx, w_ada, b_ada, w_in, s5_lam_re, s5_lam_im, s5_log_dt, s5_b_re, s5_b_im,
           s5_c_re, s5_c_im, s5_d, s5_w_glu, s5_b_glu, na_q_gain, na_k_gain, na_rpb, w_mix_out,
           w_fourier_out, b_fourier_out, w_router, router_bias, w_exp_gate, w_exp_up, w_exp_down,
           w_sh_gate, w_sh_up, w_sh_down):
    bsz, t, d = x.shape
    assert bsz == 1 and w_ada.shape[0] == 2
    n_exp = w_router.shape[2]
    s5w = s5_w_glu.shape[1]
    naw = w_in.shape[2] - s5w
    naw //= 3
    heads = naw // NA_HEAD_DIM

    cond8 = jnp.concatenate([c[:1].astype(F32), c_ctx.astype(F32)[None], jnp.zeros((6, d), F32)], axis=0)
    ada = adaln_all(cond8, w_ada, b_ada)
    mod = lambda layer, who, j: ada[layer, who:who + 1, j * d:(j + 1) * d]

    def ffn_weights(i):
        return (mod(i, 0, 3), mod(i, 0, 4), jnp.transpose(w_router[i]).astype(F32),
                w_sh_gate[i].astype(BF16), w_sh_up[i].astype(BF16), w_sh_down[i].astype(BF16))

    h0 = x[0]
    seg = jnp.asarray(np.kron(np.eye(heads), np.ones((NA_HEAD_DIM, NA_HEAD_DIM))), BF16)
    w_in_b = w_in[0].astype(BF16)
    qg = jnp.tile(na_q_gain[0].astype(F32), heads)[None]
    kg = jnp.tile(na_k_gain[0].astype(F32), heads)[None]
    u_c, _, k_c, v_c = in_projection(ctx[0], mod(0, 1, 0), mod(0, 1, 1), w_in_b, seg, qg, kg, s5w, naw)
    u_l, q_l, k_l, v_l = in_projection(h0, mod(0, 0, 0), mod(0, 0, 1), w_in_b, seg, qg, kg, s5w, naw)
    mats = s5_matrices(s5_lam_re[0], s5_lam_im[0], s5_log_dt[0], s5_b_re[0], s5_b_im[0],
                       s5_c_re[0], s5_c_im[0], s5_d[0])
    y_s5 = s5_mixer(u_c, u_l, mats)
    na = neighbourhood_attention(q_l, k_l, v_l, k_c, v_c, na_bias_table(na_rpb[0], t // GRID_W))
    h1, f1, lg1, sh1 = _post_call(
        _even_post_kernel, [y_s5, na, h0],
        [s5_w_glu[0].astype(BF16), s5_b_glu[0].astype(F32)[None], w_mix_out[0].astype(BF16), mod(0, 0, 2),
         *ffn_weights(0)], t, d, n_exp)
    h2, a1 = moe_tail(h1, f1, lg1, sh1, router_bias[0], 0, w_exp_gate, w_exp_up, w_exp_down,
                      mod(0, 0, 5), mod(1, 0, 0), mod(1, 0, 1), True)

    zre, zim = time_dft(a1)
    cc, sc = channel_dft_tables(d // FOURIER_GROUPS)
    h3, f3, lg3, sh3 = _post_call(
        _odd_post_kernel, [zre, zim, h2],
        [cc, sc, w_fourier_out[0].astype(BF16), b_fourier_out[0].astype(F32)[None], mod(1, 0, 2),
         *ffn_weights(1)], t, d, n_exp)
    zero_row = jnp.zeros((1, d), F32)
    (out,) = moe_tail(h3, f3, lg3, sh3, router_bias[1], 1, w_exp_gate, w_exp_up, w_exp_down,
                      mod(1, 0, 5), zero_row, zero_row, False)
    return out[None]
```

```python
import functools
import math

import numpy as np
import jax
import jax.numpy as jnp
from jax import lax
from jax.experimental import pallas as pl
from jax.experimental.pallas import tpu as pltpu
from jax.experimental.pallas import tpu_sc as plsc

F32 = jnp.float32
BF16 = jnp.bfloat16
I32 = jnp.int32
U32 = jnp.uint32
HIGHEST = lax.Precision.HIGHEST

MXU_DEPTH = 256
LANES = 128
GRID_W = 64
NORM_EPS = 1e-6
S5_GROUP = 16
S5_STATE = 64
S5_LAMBDA_RE_MAX = -1e-4
S5_CHUNK = 16
S5_TILE = 128
NA_HEADS = 8
NA_HEAD_DIM = 64
NA_KH = 8
NA_KW = 16
NA_STEP_ROWS = 2
FOURIER_GROUPS = 4
N_EXPERT_GROUPS = 8
TOPK_GROUPS = 4
TOP_K = 8
ROUTED_SCALE = 2.5
EXPERT_ROWS = 1024
ROW_PARTS = 2
SC_WINDOW = 128
SC_GATHER_SPLIT = 4
NEG_BIG = -1e30

VMEM_LIMIT_BYTES = 56 * 1024 * 1024


def _params(*sem):
    return pltpu.CompilerParams(dimension_semantics=sem or None,
                                vmem_limit_bytes=VMEM_LIMIT_BYTES)


def _rms(x):
    return x * lax.rsqrt(jnp.mean(x * x, axis=-1, keepdims=True) + NORM_EPS)


def _silu(x):
    return x * jax.nn.sigmoid(x)


def _pack_pair(lo, hi):
    lo = lax.bitcast_convert_type(lo.astype(BF16).astype(F32), U32)
    hi = lax.bitcast_convert_type(hi.astype(BF16).astype(F32), U32)
    return (hi & jnp.uint32(0xFFFF0000)) | (lo >> 16)


def _unpack_pair(w):
    return (lax.bitcast_convert_type(w << 16, F32), lax.bitcast_convert_type(w & jnp.uint32(0xFFFF0000), F32))


def _pack_rows(x):
    n = x.shape[1] // 2
    return _pack_pair(x[:, :n], x[:, n:])


def _unpack_rows(w):
    return jnp.concatenate(_unpack_pair(w), axis=1)


def _ada_kernel(c_ref, w_ref, b_ref, o_ref):
    o_ref[0] = jnp.dot(_silu(c_ref[...]), w_ref[0], preferred_element_type=F32,
                       precision=HIGHEST) + b_ref[0]


def adaln_all(cond8, w_ada, b_ada):
    n_layers, d, n6 = w_ada.shape
    tn = n6 // 4
    return pl.pallas_call(
        _ada_kernel,
        grid=(n_layers, n6 // tn),
        in_specs=[pl.BlockSpec((8, d), lambda l, j: (0, 0)),
                  pl.BlockSpec((1, d, tn), lambda l, j: (l, 0, j)),
                  pl.BlockSpec((1, 1, tn), lambda l, j: (l, 0, j))],
        out_specs=pl.BlockSpec((1, 8, tn), lambda l, j: (l, 0, j)),
        out_shape=jax.ShapeDtypeStruct((n_layers, 8, n6), F32),
        compiler_params=_params("parallel", "parallel"),
    )(cond8, w_ada, b_ada.reshape(n_layers, 1, n6))


def _inproj_kernel(x_ref, sh_ref, sc_ref, w_ref, seg_ref, qg_ref, kg_ref,
                   u_ref, q_ref, k_ref, v_ref):
    a = _rms(x_ref[...]) * (1.0 + sc_ref[...]) + sh_ref[...]
    z = jnp.dot(a.astype(BF16), w_ref[...], preferred_element_type=F32)
    s5w = u_ref.shape[1]
    naw = q_ref.shape[1]

    def head_norm(t, gain):
        ss = jnp.dot((t * t).astype(BF16), seg_ref[...], preferred_element_type=F32)
        return t * lax.rsqrt(ss * (1.0 / NA_HEAD_DIM) + NORM_EPS) * gain

    u_ref[...] = z[:, :s5w]
    q_ref[...] = head_norm(z[:, s5w:s5w + naw], qg_ref[...]).astype(BF16)
    k_ref[...] = head_norm(z[:, s5w + naw:s5w + 2 * naw], kg_ref[...]).astype(BF16)
    v_ref[...] = z[:, s5w + 2 * naw:].astype(BF16)


def in_projection(x, shift, scale, w_in_bf16, seg_ones, q_gain_row, k_gain_row, s5w, naw):
    t, d = x.shape
    tm = min(512, t)
    row = lambda i: (i, 0)
    fixed = lambda i: (0, 0)
    return pl.pallas_call(
        _inproj_kernel,
        grid=(t // tm,),
        in_specs=[pl.BlockSpec((tm, d), row),
                  pl.BlockSpec((1, d), fixed), pl.BlockSpec((1, d), fixed),
                  pl.BlockSpec(w_in_bf16.shape, fixed),
                  pl.BlockSpec(seg_ones.shape, fixed),
                  pl.BlockSpec((1, naw), fixed), pl.BlockSpec((1, naw), fixed)],
        out_specs=[pl.BlockSpec((tm, s5w), row), pl.BlockSpec((tm, naw), row),
                   pl.BlockSpec((tm, naw), row), pl.BlockSpec((tm, naw), row)],
        out_shape=[jax.ShapeDtypeStruct((t, s5w), F32)] + [jax.ShapeDtypeStruct((t, naw), BF16)] * 3,
        compiler_params=_params("parallel"),
    )(x, shift, scale, w_in_bf16, seg_ones, q_gain_row, k_gain_row)


def s5_matrices(lam_re, lam_im, log_dt, b_re, b_im, c_re, c_im, d_skip):
    L = S5_CHUNK
    c = S5_GROUP
    taus = jnp.arange(L + 1, dtype=F32)

    def direction(i):
        lam = lax.complex(jnp.minimum(lam_re[i].astype(F32), S5_LAMBDA_RE_MAX), lam_im[i].astype(F32))
        ldt = lam * jnp.exp(log_dt[i].astype(F32))[:, None]
        lam_bar = jnp.exp(ldt)
        b_bar = ((lam_bar - 1.0) / lam)[..., None] * lax.complex(b_re[i].astype(F32), b_im[i].astype(F32))
        cc = lax.complex(c_re[i].astype(F32), c_im[i].astype(F32))
        powers = jnp.exp(ldt[None] * taus[:, None, None])
        resp = jnp.real(jnp.einsum('gcp,tgp,gpd->gctd', cc, powers[:L], b_bar, precision=HIGHEST))
        return jnp.transpose(powers, (1, 2, 0)), b_bar, cc, resp

    pw_f, bb_f, cc_f, k_f = direction(0)
    pw_b, bb_b, cc_b, k_b = direction(1)
    g, p = pw_f.shape[:2]
    lag0 = k_f[:, :, :1] + k_b[:, :, :1] + (jnp.eye(c, dtype=F32)[None, :, None, :] * d_skip.astype(F32)[:, :, None, None])
    by_lag = jnp.concatenate([jnp.flip(k_f[:, :, 1:], axis=2), lag0, k_b[:, :, 1:]], axis=2).reshape(g, c, (2 * L - 1) * c)
    toep_t = jnp.stack([by_lag[:, :, (L - 1 - l) * c:(2 * L - 1 - l) * c] for l in range(L)], axis=1)
    toep_t = toep_t.reshape(g, L * c, L * c)

    def state_in(pw_by_s, b_bar):
        return (pw_by_s[:, :, :, None] * b_bar[:, :, None, :]).reshape(g, p, L * c)

    wf = state_in(jnp.flip(pw_f[:, :, :L], axis=2), bb_f)
    wb = state_in(pw_b[:, :, :L], bb_b)
    w_state_t = jnp.concatenate([jnp.real(wf), jnp.imag(wf), jnp.imag(wf), jnp.real(wf),
                                 jnp.real(wb), jnp.imag(wb), jnp.imag(wb), jnp.real(wb)], axis=1)

    def state_out(pw_by_l, cc):
        return (jnp.transpose(pw_by_l, (0, 2, 1))[:, :, None, :] * cc[:, None, :, :]).reshape(g, L * c, p)

    rf = state_out(pw_f[:, :, 1:], cc_f)
    rb = state_out(jnp.flip(pw_b[:, :, 1:], axis=2), cc_b)
    r_state_t = jnp.concatenate([jnp.real(rf), -jnp.imag(rf), jnp.real(rb), -jnp.imag(rb)], axis=-1)

    def mult(a):
        ar, ai = jnp.real(a), jnp.imag(a)
        return jnp.stack([jnp.concatenate([ar, ar], -1), jnp.concatenate([-ai, ai], -1),
                          jnp.concatenate([ai, -ai], -1)])

    return (toep_t.astype(BF16), w_state_t.astype(BF16), r_state_t.astype(BF16),
            mult(pw_f[:, :, L]), mult(pw_b[:, :, L]))


def _s5_pack_kernel(*refs):
    u_refs, (wt_ref, ut_ref, f1_ref, f2_ref, b1_ref, b2_ref) = refs[:-6], refs[-6:]
    L = S5_CHUNK
    g, lc, nck = ut_ref.shape
    c = lc // L
    gs = g // len(u_refs)
    for s in range(L):
        for j, u_ref in enumerate(u_refs):
            step_s = u_ref[pl.ds(s, nck, stride=L), :]
            ut_ref[j * gs:(j + 1) * gs, s * c:(s + 1) * c, :] = (
                jnp.transpose(step_s).astype(BF16).reshape(gs, c, nck))
    n = f1_ref.shape[2]
    for gi in range(g):
        inc = jnp.dot(wt_ref[gi], ut_ref[gi], preferred_element_type=F32)
        for j, ref in enumerate((f1_ref, f2_ref, b1_ref, b2_ref)):
            ref[:, gi, :] = jnp.transpose(inc[j * n:(j + 1) * n, :])


def s5_pack(u, wt_state):
    t, w = u.shape
    g, n4, lc = wt_state.shape
    nc = t // S5_CHUNK
    tile = min(S5_TILE, nc)
    inc = jax.ShapeDtypeStruct((nc, g, n4 // 4), F32)
    inc_blk = pl.BlockSpec((tile, g, n4 // 4), lambda i: (i, 0, 0))
    return pl.pallas_call(
        _s5_pack_kernel,
        grid=(nc // tile,),
        in_specs=[pl.BlockSpec((tile * S5_CHUNK, LANES), functools.partial(lambda i, j: (i, j), j=j))
                  for j in range(w // LANES)] + [pl.BlockSpec(wt_state.shape, lambda i: (0, 0, 0))],
        out_specs=[pl.BlockSpec((g, lc, tile), lambda i: (0, 0, i))] + [inc_blk] * 4,
        out_shape=[jax.ShapeDtypeStruct((g, lc, nc), BF16)] + [inc] * 4,
        compiler_params=_params("parallel"),
    )(*([u] * (w // LANES)), wt_state)


def _s5_scan_kernel(s1_ref, s2_ref, m_ref, init_ref, x_ref, last_ref, v1_ref, v2_ref, *, reverse):
    @pl.when(pl.program_id(0) == 0)
    def _():
        v1_ref[...] = init_ref[0]
        v2_ref[...] = init_ref[1]

    a1, a2, a3 = m_ref[0], m_ref[1], m_ref[2]
    cb = s1_ref.shape[0]

    def body(j, carry):
        v1, v2 = carry
        jj = cb - 1 - j if reverse else j
        x_ref[jj] = v1
        return (a1 * v1 + a2 * v2 + s1_ref[jj], a1 * v2 + a3 * v1 + s2_ref[jj])

    v1, v2 = lax.fori_loop(0, cb, body, (v1_ref[...], v2_ref[...]))
    v1_ref[...] = v1
    v2_ref[...] = v2
    last_ref[...] = v1


def s5_chunk_scan(s1, s2, mult, init, reverse):
    nc, g, n = s1.shape
    cb = min(S5_TILE, nc)
    nb = nc // cb
    blk = (lambda i: (nb - 1 - i, 0, 0)) if reverse else (lambda i: (i, 0, 0))
    return pl.pallas_call(
        functools.partial(_s5_scan_kernel, reverse=reverse),
        grid=(nb,),
        in_specs=[pl.BlockSpec((cb, g, n), blk), pl.BlockSpec((cb, g, n), blk),
                  pl.BlockSpec((3, g, n), lambda i: (0, 0, 0)), pl.BlockSpec((2, g, n), lambda i: (0, 0, 0))],
        out_specs=[pl.BlockSpec((cb, g, n), blk), pl.BlockSpec((g, n), lambda i: (0, 0))],
        out_shape=[jax.ShapeDtypeStruct((nc, g, n), F32), jax.ShapeDtypeStruct((g, n), F32)],
        scratch_shapes=[pltpu.VMEM((g, n), F32), pltpu.VMEM((g, n), F32)],
        compiler_params=_params("arbitrary"),
    )(s1, s2, mult, init)


def _s5_readout_kernel(ut_ref, tt_ref, rt_ref, xf_ref, xb_ref, y_ref, yt_ref, *slab_refs):
    L = S5_CHUNK
    g, lc, nck = ut_ref.shape
    c = lc // L
    gs = g // len(slab_refs)
    for gi in range(g):
        xin_t = jnp.concatenate([jnp.transpose(xf_ref[:, gi, :]), jnp.transpose(xb_ref[:, gi, :])], axis=0)
        yt_ref[gi] = (jnp.dot(tt_ref[gi], ut_ref[gi], preferred_element_type=F32)
                      + jnp.dot(rt_ref[gi], xin_t.astype(BF16), preferred_element_type=F32))
    for j, slab in enumerate(slab_refs):
        for l in range(L):
            step_l = yt_ref[j * gs:(j + 1) * gs, l * c:(l + 1) * c, :].reshape(gs * c, nck)
            slab[pl.ds(l, nck, stride=L), :] = jnp.transpose(step_l)
        y_ref[:, j * gs * c:(j + 1) * gs * c] = slab[...]


def s5_readout(ut, toep_t, r_state_t, xin_f, xin_b):
    g, lc, nc = ut.shape
    n = xin_f.shape[2]
    tile = min(S5_TILE, nc)
    fixed = lambda i: (0, 0, 0)
    state_blk = pl.BlockSpec((tile, g, n), lambda i: (i, 0, 0))
    return pl.pallas_call(
        _s5_readout_kernel,
        grid=(nc // tile,),
        in_specs=[pl.BlockSpec((g, lc, tile), lambda i: (0, 0, i)),
                  pl.BlockSpec(toep_t.shape, fixed), pl.BlockSpec(r_state_t.shape, fixed), state_blk, state_blk],
        out_specs=pl.BlockSpec((tile * S5_CHUNK, g * lc // S5_CHUNK), lambda i: (i, 0)),
        out_shape=jax.ShapeDtypeStruct((nc * S5_CHUNK, g * lc // S5_CHUNK), F32),
        scratch_shapes=[pltpu.VMEM((g, lc, tile), F32)]
        + [pltpu.VMEM((tile * S5_CHUNK, LANES), F32)] * (g * lc // S5_CHUNK // LANES),
        compiler_params=_params("parallel"),
    )(ut, toep_t, r_state_t, xin_f, xin_b)


def s5_mixer(u_ctx, u_lat, mats):
    toep_t, wt_state, r_state_t, mult_f, mult_b = mats
    L = S5_CHUNK
    g, n = mult_f.shape[1:]
    halves = lambda v: jnp.stack([v, jnp.roll(v, n // 2, axis=-1)])
    n_ctx = u_ctx.shape[0] // L
    ctx_chunks = -(-(n_ctx + 1) // S5_TILE) * S5_TILE
    ctx_pad = jnp.pad(u_ctx, ((0, ctx_chunks * L - u_ctx.shape[0]), (0, 0)))
    _, cf1, cf2, cb1, cb2 = s5_pack(ctx_pad, wt_state)
    zero = jnp.zeros((2, g, n), F32)
    ctx_f, _ = s5_chunk_scan(cf1, cf2, mult_f, zero, False)
    _, ctx_b_last = s5_chunk_scan(cb1, cb2, mult_b, zero, True)
    ut, f1, f2, b1, b2 = s5_pack(u_lat, wt_state)
    xin_f, _ = s5_chunk_scan(f1, f2, mult_f, halves(ctx_f[n_ctx]), False)
    xin_b, _ = s5_chunk_scan(b1, b2, mult_b, halves(ctx_b_last), True)
    return s5_readout(ut, toep_t, r_state_t, xin_f, xin_b)


def _na_window(rows):
    step = NA_STEP_ROWS
    assert rows % step == 0 and (NA_KH // 2) % step == 0 and rows >= NA_KH + step
    groups = rows // step
    n_blk = (NA_KH + step) // step
    first = np.clip(np.arange(groups) - NA_KH // 2 // step, 0, groups - n_blk)
    return groups, n_blk, first


def na_bias_table(rpb, rows):
    step = NA_STEP_ROWS
    groups, n_blk, first = _na_window(rows)
    win = n_blk * step
    q_col = np.arange(GRID_W)
    col_start = np.clip(q_col - NA_KW // 2, 0, GRID_W - NA_KW)
    key_col = np.arange(GRID_W)
    off = key_col[None, :] - col_start[:, None]
    valid_col = (off >= 0) & (off < NA_KW)
    rel_col = np.clip(key_col[None, :] - q_col[:, None] + NA_KW - 1, 0, 2 * NA_KW - 2)
    q_row = step * np.arange(groups)[:, None, None] + np.arange(step)[None, :, None]
    key_row = step * first[:, None, None] + np.arange(win)[None, None, :]
    row0 = np.clip(q_row - NA_KH // 2, 0, rows - NA_KH)
    rel_row = np.where((key_row >= row0) & (key_row < row0 + NA_KH), key_row - q_row + NA_KH - 1, -1)
    variant = np.arange(groups) - first
    reps = [int(np.argmax(variant == d)) for d in range(n_blk)]
    assert all((rel_row[g] == rel_row[reps[variant[g]]]).all() for g in range(groups))
    rel_row = rel_row[reps]
    pick_row = jnp.asarray(np.arange(2 * NA_KH - 1)[:, None, None, None] == rel_row[None], F32)
    pick_col = jnp.asarray(np.arange(2 * NA_KW - 1)[:, None, None] == rel_col[None], F32)
    table = jnp.einsum('hrx,rvji,xck->vhjcik', rpb.astype(F32), pick_row, pick_col, precision=HIGHEST)
    valid = (rel_row >= 0)[:, None, :, None, :, None] & valid_col[None, None, None, :, None, :]
    return jnp.where(valid, table, NEG_BIG).reshape(n_blk, rpb.shape[0], step * GRID_W, win * GRID_W)


def _na_kernel(*refs, scale, n_blk):
    q_ref = refs[0]
    k_refs = refs[1:1 + n_blk]
    v_refs = refs[1 + n_blk:1 + 2 * n_blk]
    kc_ref, vc_ref, b_ref, o_ref = refs[1 + 2 * n_blk:]
    hd = NA_HEAD_DIM
    width = q_ref.shape[1]
    span = min(MXU_DEPTH, width)
    nt = (((1,), (1,)), ((), ()))
    q = q_ref[...] * scale
    kk = jnp.concatenate([r[...] for r in k_refs], axis=0)
    vv = jnp.concatenate([r[...] for r in v_refs], axis=0)
    kc, vc = kc_ref[...], vc_ref[...]
    nq = q.shape[0]
    per = span // hd
    lane = lax.broadcasted_iota(I32, (nq, span), 1)
    own = [(lane >= j * hd) & (lane < (j + 1) * hd) for j in range(per)]
    out_cols = []
    for c0 in range(0, width, span):
        cols = slice(c0, c0 + span)
        qs = jnp.concatenate([jnp.where(own[j], q[:, cols], jnp.zeros_like(q[:, cols])) for j in range(per)], axis=0)
        h0 = c0 // hd
        bias = b_ref[0, h0:h0 + per].reshape(per * nq, kk.shape[0])
        s = lax.dot_general(qs, kk[:, cols], nt, preferred_element_type=F32) + bias
        sc = lax.dot_general(qs, kc[:, cols], nt, preferred_element_type=F32)
        m = jnp.maximum(jnp.max(s, axis=-1, keepdims=True), jnp.max(sc, axis=-1, keepdims=True))
        p = jnp.exp(s - m)
        pc = jnp.exp(sc - m)
        den = jnp.sum(p, axis=-1, keepdims=True) + jnp.sum(pc, axis=-1, keepdims=True)
        o = (jnp.dot(p.astype(BF16), vv[:, cols], preferred_element_type=F32)
             + jnp.dot(pc.astype(BF16), vc[:, cols], preferred_element_type=F32)) / den
        acc = jnp.zeros((nq, span), F32)
        for j in range(per):
            acc = jnp.where(own[j], o[j * nq:(j + 1) * nq], acc)
        out_cols.append(acc)
    o_ref[...] = jnp.concatenate(out_cols, axis=1).astype(o_ref.dtype)


def neighbourhood_attention(q, k, v, k_ctx, v_ctx, bias_table):
    t, w = q.shape
    groups, n_blk, _ = _na_window(t // GRID_W)
    first = lambda g: jnp.clip(g - NA_KH // 2 // NA_STEP_ROWS, 0, groups - n_blk)
    blk_tokens = NA_STEP_ROWS * GRID_W
    row_blk = pl.BlockSpec((blk_tokens, w), lambda g: (g, 0))
    key_blks = [pl.BlockSpec((blk_tokens, w), functools.partial(lambda g, i: (first(g) + i, 0), i=i))
                for i in range(n_blk)]
    ctx_blk = pl.BlockSpec(k_ctx.shape, lambda g: (0, 0))
    bias_blk = pl.BlockSpec((1,) + bias_table.shape[1:], lambda g: (g - first(g), 0, 0, 0))
    return pl.pallas_call(
        functools.partial(_na_kernel, scale=NA_HEAD_DIM ** -0.5, n_blk=n_blk),
        grid=(groups,),
        in_specs=[row_blk] + key_blks + key_blks + [ctx_blk, ctx_blk, bias_blk],
        out_specs=row_blk,
        out_shape=jax.ShapeDtypeStruct((t, w), BF16),
        compiler_params=_params("parallel"),
    )(q, *([k] * n_blk), *([v] * n_blk), k_ctx, v_ctx, bias_table)


def time_dft_tables(t):
    a_len = 1 << (int(math.log2(t)) // 2)
    b_len = t // a_len
    ka = np.arange(a_len)[:, None]
    tok = b_len * np.arange(a_len)[None, :]
    ang1 = -2.0 * np.pi * ((ka * (tok[None] + np.arange(b_len)[:, None, None])) % t) / t
    stage1 = np.concatenate([np.cos(ang1), np.sin(ang1)], axis=1) / math.sqrt(t)
    ang2 = 2.0 * np.pi * ((np.arange(b_len)[:, None] * np.arange(b_len)[None, :]) % b_len) / b_len
    c2, s2 = np.cos(ang2), np.sin(ang2)
    stage2 = np.block([[c2, s2], [-s2, c2]])
    return jnp.asarray(stage1, BF16), jnp.asarray(stage2, BF16), a_len, b_len


def _time_dft_kernel(x_ref, m_ref, w2_ref, zre_ref, zim_ref, y_ref, z_ref, *, a_len, b_len):
    i = pl.program_id(1)
    bb = m_ref.shape[0]

    def stage1(jb, carry):
        b = i * bb + jb
        xb = x_ref[pl.ds(b, a_len, stride=b_len), :]
        y = jnp.dot(m_ref[jb], xb.astype(BF16), preferred_element_type=F32)
        row = pl.multiple_of(b * a_len, a_len)
        y_ref[pl.ds(row, a_len), :] = _pack_pair(y[:a_len], y[a_len:])
        return carry

    lax.fori_loop(0, bb, stage1, 0, unroll=16)

    @pl.when(i == pl.num_programs(1) - 1)
    def _():
        def stage2(ka, carry):
            rows = pl.ds(ka, b_len, stride=a_len)
            yre, yim = _unpack_pair(y_ref[rows, :])
            y = jnp.concatenate([yre, yim], axis=0).astype(BF16)
            z = jnp.dot(w2_ref[...], y, preferred_element_type=F32)
            z_ref[rows, :] = _pack_pair(z[:b_len], z[b_len:])
            return carry

        lax.fori_loop(0, a_len, stage2, 0, unroll=16)
        zre, zim = _unpack_pair(z_ref[...])
        zre_ref[...] = zre.astype(zre_ref.dtype)
        zim_ref[...] = zim.astype(zim_ref.dtype)


def time_dft(x):
    t, d = x.shape
    stage1, stage2, a_len, b_len = time_dft_tables(t)
    lanes = 128
    bb = min(16, b_len)
    out = jax.ShapeDtypeStruct((t, d), BF16)
    return pl.pallas_call(
        functools.partial(_time_dft_kernel, a_len=a_len, b_len=b_len),
        grid=(d // lanes, b_len // bb),
        in_specs=[pl.BlockSpec((t, lanes), lambda j, i: (0, j)),
                  pl.BlockSpec((bb, 2 * a_len, a_len), lambda j, i: (i, 0, 0)),
                  pl.BlockSpec(stage2.shape, lambda j, i: (0, 0))],
        out_specs=[pl.BlockSpec((t, lanes), lambda j, i: (0, j))] * 2,
        out_shape=[out, out],
        scratch_shapes=[pltpu.VMEM((t, lanes), U32), pltpu.VMEM((t, lanes), U32)],
        compiler_params=_params("parallel", "arbitrary"),
    )(x, stage1, stage2)


def channel_dft_tables(c):
    ang = 2.0 * np.pi * ((np.arange(c)[:, None] * np.arange(c)[None, :]) % c) / c
    return (jnp.asarray(np.cos(ang) / math.sqrt(c), BF16), jnp.asarray(np.sin(ang) / math.sqrt(c), BF16))


def _ffn_prologue(h, shf_ref, scf_ref, wr_ref, wsg_ref, wsu_ref, wsd_ref, h_ref, f_ref, lg_ref, shared_ref):
    h_ref[...] = h
    f = _rms(h) * (1.0 + scf_ref[...]) + shf_ref[...]
    packed = _pack_rows(f)
    half = packed.shape[1] // 2
    f_ref[0] = packed[:, :half]
    f_ref[1] = packed[:, half:]
    fb = f.astype(BF16)
    f_lo = (f - fb.astype(F32)).astype(BF16)
    nt = (((1,), (1,)), ((), ()))
    wr = wr_ref[...]
    wr_hi = wr.astype(BF16)
    wr_lo = (wr - wr_hi.astype(F32)).astype(BF16)
    lg_ref[...] = (lax.dot_general(wr_hi, fb, nt, preferred_element_type=F32)
                   + lax.dot_general(wr_hi, f_lo, nt, preferred_element_type=F32)
                   + lax.dot_general(wr_lo, fb, nt, preferred_element_type=F32))
    hid = (_silu(jnp.dot(fb, wsg_ref[...], preferred_element_type=F32))
           * jnp.dot(fb, wsu_ref[...], preferred_element_type=F32))
    shared_ref[...] = jnp.dot(hid.astype(BF16), wsd_ref[...], preferred_element_type=F32).astype(shared_ref.dtype)


def _gelu_tanh(x):
    return 0.5 * x * (1.0 + jnp.tanh(math.sqrt(2.0 / math.pi) * (x + 0.044715 * (x * x * x))))


def _even_post_kernel(y_ref, na_ref, x_ref, wglu_ref, bglu_ref, wo_ref, gm_ref, *rest):
    g = _gelu_tanh(y_ref[...])
    gate = jax.nn.sigmoid(jnp.dot(g.astype(BF16), wglu_ref[...], preferred_element_type=F32) + bglu_ref[...])
    s5 = (g * gate).astype(BF16)
    w = s5.shape[1]
    mix = (jnp.dot(s5, wo_ref[:w, :], preferred_element_type=F32)
           + jnp.dot(na_ref[...], wo_ref[w:, :], preferred_element_type=F32))
    _ffn_prologue(x_ref[...] + gm_ref[...] * mix, *rest)


def _odd_post_kernel(zre_ref, zim_ref, h_ref_in, cc_ref, sc_ref, wf_ref, bf_ref, gm_ref, *rest):
    c = cc_ref.shape[0]
    parts = []
    for grp in range(zre_ref.shape[1] // c):
        cols = slice(grp * c, (grp + 1) * c)
        parts.append(jnp.dot(zre_ref[:, cols], cc_ref[...], preferred_element_type=F32)
                     + jnp.dot(zim_ref[:, cols], sc_ref[...], preferred_element_type=F32))
    fr = jnp.concatenate(parts, axis=-1).astype(BF16)
    mix = jnp.dot(fr, wf_ref[...], preferred_element_type=F32) + bf_ref[...]
    _ffn_prologue(h_ref_in[...] + gm_ref[...] * mix, *rest)


def _post_call(body, row_inputs, fixed_inputs, t, d, n_exp):
    tm = min(512, t)
    row = lambda i: (i, 0)
    in_specs = ([pl.BlockSpec((tm, a.shape[1]), row) for a in row_inputs]
                + [pl.BlockSpec(a.shape, functools.partial(lambda i, nd: (0,) * nd, nd=a.ndim))
                   for a in fixed_inputs])
    return pl.pallas_call(
        body,
        grid=(t // tm,),
        in_specs=in_specs,
        out_specs=[pl.BlockSpec((tm, d), row), pl.BlockSpec((ROW_PARTS, tm, d // 4), lambda i: (0, i, 0)),
                   pl.BlockSpec((n_exp, tm), lambda i: (0, i)), pl.BlockSpec((tm, d), row)],
        out_shape=[jax.ShapeDtypeStruct((t, d), F32), jax.ShapeDtypeStruct((ROW_PARTS, t, d // 4), U32),
                   jax.ShapeDtypeStruct((n_exp, t), F32), jax.ShapeDtypeStruct((t, d), BF16)],
        compiler_params=_params("parallel"),
    )(*row_inputs, *fixed_inputs)


def _route_kernel(lg_ref, bias_ref, tri_ref, idx_ref, gate_ref, rank_ref, cnt_ref, run_ref):
    @pl.when(pl.program_id(0) == 0)
    def _():
        run_ref[...] = jnp.zeros_like(run_ref)

    scores = jax.nn.sigmoid(lg_ref[...])
    n_exp, tb = scores.shape
    sel = scores + bias_ref[...]
    gsz = n_exp // N_EXPERT_GROUPS
    member = lax.broadcasted_iota(I32, (gsz, tb), 0)
    gscore = []
    for grp in range(N_EXPERT_GROUPS):
        xg = sel[grp * gsz:(grp + 1) * gsz, :]
        m1 = jnp.max(xg, axis=0, keepdims=True)
        first = jnp.min(jnp.where(xg == m1, member, gsz), axis=0, keepdims=True)
        m2 = jnp.max(jnp.where(member == first, -jnp.inf, xg), axis=0, keepdims=True)
        gscore.append(m1 + m2)
    keep_rows = []
    for grp in range(N_EXPERT_GROUPS):
        beaten = jnp.zeros((1, tb), F32)
        for other in range(N_EXPERT_GROUPS):
            if other == grp:
                continue
            wins = (gscore[other] >= gscore[grp]) if other < grp else (gscore[other] > gscore[grp])
            beaten = beaten + jnp.where(wins, 1.0, 0.0)
        keep_rows.append(jnp.broadcast_to(beaten < TOPK_GROUPS, (gsz, tb)))
    masked = jnp.where(jnp.concatenate(keep_rows, axis=0), sel, -jnp.inf)

    expert = lax.broadcasted_iota(I32, (n_exp, tb), 0)
    picks, gates, hots = [], [], []
    chosen = jnp.zeros((n_exp, tb), F32)
    for _ in range(TOP_K):
        m = jnp.max(masked, axis=0, keepdims=True)
        pick = jnp.min(jnp.where(masked == m, expert, n_exp), axis=0, keepdims=True)
        hot = expert == pick
        picks.append(pick)
        hots.append(hot)
        gates.append(jnp.sum(jnp.where(hot, scores, 0.0), axis=0, keepdims=True))
        chosen = jnp.where(hot, 1.0, chosen)
        masked = jnp.where(hot, -jnp.inf, masked)
    total = gates[0]
    for gk in gates[1:]:
        total = total + gk
    ahead = jnp.dot(chosen.astype(BF16), tri_ref[...], preferred_element_type=F32) + run_ref[...]
    for k in range(TOP_K):
        idx_ref[k:k + 1, :] = picks[k]
        gate_ref[k:k + 1, :] = ROUTED_SCALE * gates[k] / total
        rank_ref[k:k + 1, :] = jnp.sum(jnp.where(hots[k], ahead, 0.0), axis=0, keepdims=True).astype(I32)
    run_ref[...] = run_ref[...] + jnp.sum(chosen, axis=1, keepdims=True)
    cnt_ref[...] = jnp.broadcast_to(run_ref[...], cnt_ref.shape)


def route(logits_t, router_bias):
    n_exp, t = logits_t.shape
    tb = min(512, t)
    tri = jnp.asarray(np.triu(np.ones((tb, tb), np.float32), k=1), BF16)
    tok = lambda i: (0, i)
    idx, gate, rank, cnt = pl.pallas_call(
        _route_kernel,
        grid=(t // tb,),
        in_specs=[pl.BlockSpec((n_exp, tb), tok), pl.BlockSpec((n_exp, 1), lambda i: (0, 0)),
                  pl.BlockSpec((tb, tb), lambda i: (0, 0))],
        out_specs=[pl.BlockSpec((TOP_K, tb), tok)] * 3 + [pl.BlockSpec((n_exp, 128), lambda i: (0, 0))],
        out_shape=[jax.ShapeDtypeStruct((TOP_K, t), I32), jax.ShapeDtypeStruct((TOP_K, t), F32),
                   jax.ShapeDtypeStruct((TOP_K, t), I32), jax.ShapeDtypeStruct((n_exp, 128), F32)],
        scratch_shapes=[pltpu.VMEM((n_exp, 1), F32)],
        compiler_params=_params("arbitrary"),
    )(logits_t, router_bias.astype(F32).reshape(n_exp, 1), tri)
    return idx, gate, rank, cnt[:, 0].astype(I32)


def dispatch_plan(idx, rank, counts, n_blocks):
    n_exp = counts.shape[0]
    padded = (counts + EXPERT_ROWS - 1) // EXPERT_ROWS * EXPERT_ROWS
    pad_end = jnp.cumsum(padded)
    pad_start = pad_end - padded
    experts_iota = jnp.arange(n_exp, dtype=I32)
    dest = jnp.sum(jnp.where(idx[..., None] == experts_iota, pad_start, 0), axis=-1) + rank
    n_valid = (pad_end[-1] // EXPERT_ROWS).astype(I32)
    blk = jnp.minimum(jnp.arange(n_blocks, dtype=I32), n_valid - 1)
    blk_e = jnp.sum((pad_end[None, :] <= blk[:, None] * EXPERT_ROWS).astype(I32), axis=1)
    return dest.astype(I32), jnp.minimum(blk_e, n_exp - 1).astype(I32), n_valid.reshape(1)


def _sc_mesh():
    return plsc.VectorSubcoreMesh(core_axis_name="core", subcore_axis_name="subcore")


def dispatch(f, dest, cap):
    t, w = f.shape
    n_choice = dest.shape[0]

    @functools.partial(pl.kernel, out_type=jax.ShapeDtypeStruct((cap, w), f.dtype), mesh=_sc_mesh(),
                       scratch_types=[pltpu.SemaphoreType.DMA])
    def scatter_rows(x_hbm, i_hbm, o_hbm, sem):
        def body(x_vmem, i_vmem):
            copies = [pltpu.async_copy(x_vmem, o_hbm.at[i_vmem.at[k]], sem) for k in range(n_choice)]
            for cp in copies:
                cp.wait()

        pltpu.emit_pipeline(
            body,
            grid=(t // SC_WINDOW,),
            in_specs=[pl.BlockSpec((SC_WINDOW, w), lambda i: (i, 0)),
                      pl.BlockSpec((n_choice, SC_WINDOW), lambda i: (0, i))],
            out_specs=[],
            core_axis_name=("core", "subcore"),
            dimension_semantics=(pltpu.PARALLEL,),
        )(x_hbm, i_hbm)

    return scatter_rows(f, dest)


def gather_rows(rows, index_row):
    n = index_row.shape[1]
    w = rows.shape[1]

    piece = SC_WINDOW // SC_GATHER_SPLIT

    @functools.partial(pl.kernel, out_type=jax.ShapeDtypeStruct((n, w), rows.dtype), mesh=_sc_mesh(),
                       scratch_types=[pltpu.SemaphoreType.DMA])
    def gather(y_hbm, i_hbm, o_hbm, sem):
        def body(i_vmem, o_vmem):
            copies = [pltpu.async_copy(y_hbm.at[i_vmem.at[0, pl.ds(j * piece, piece)]],
                                       o_vmem.at[pl.ds(j * piece, piece)], sem)
                      for j in range(SC_GATHER_SPLIT)]
            for cp in copies:
                cp.wait()

        pltpu.emit_pipeline(
            body,
            grid=(n // SC_WINDOW,),
            in_specs=[pl.BlockSpec((1, SC_WINDOW), lambda i: (0, i))],
            out_specs=[pl.BlockSpec((SC_WINDOW, w), lambda i: (i, 0))],
            core_axis_name=("core", "subcore"),
            dimension_semantics=(pltpu.PARALLEL,),
        )(i_hbm, o_hbm)

    return gather(rows, index_row)


def _experts_kernel(be_ref, x_ref, wg_ref, wu_ref, wd_ref, y_ref, wg_bf, wu_bf, wd_bf):
    b = pl.program_id(0)

    @pl.when((b == 0) | (be_ref[b] != be_ref[jnp.maximum(b - 1, 0)]))
    def _():
        wg_bf[...] = wg_ref[0, 0].astype(BF16)
        wu_bf[...] = wu_ref[0, 0].astype(BF16)
        wd_bf[...] = wd_ref[0, 0].astype(BF16)

    x = _unpack_rows(jnp.concatenate([x_ref[i] for i in range(ROW_PARTS)], axis=1)).astype(BF16)
    hid = (_silu(jnp.dot(x, wg_bf[...], preferred_element_type=F32))
           * jnp.dot(x, wu_bf[...], preferred_element_type=F32))
    packed = _pack_rows(jnp.dot(hid.astype(BF16), wd_bf[...], preferred_element_type=F32))
    dq = packed.shape[1] // ROW_PARTS
    for i in range(ROW_PARTS):
        y_ref[i] = packed[:, i * dq:(i + 1) * dq]


def experts(rows, blk_e, n_valid, layer, w_gate, w_up, w_down):
    parts, cap, dq = rows.shape
    _, _, d, ff = w_gate.shape
    blk = pl.BlockSpec((parts, EXPERT_ROWS, dq), lambda b, be: (0, b, 0))
    return pl.pallas_call(
        _experts_kernel,
        grid_spec=pltpu.PrefetchScalarGridSpec(
            num_scalar_prefetch=1,
            grid=(n_valid[0],),
            in_specs=[blk,
                      pl.BlockSpec((1, 1, d, ff), lambda b, be: (layer, be[b], 0, 0)),
                      pl.BlockSpec((1, 1, d, ff), lambda b, be: (layer, be[b], 0, 0)),
                      pl.BlockSpec((1, 1, ff, d), lambda b, be: (layer, be[b], 0, 0))],
            out_specs=blk,
            scratch_shapes=[pltpu.VMEM((d, ff), BF16), pltpu.VMEM((d, ff), BF16), pltpu.VMEM((ff, d), BF16)],
        ),
        out_shape=jax.ShapeDtypeStruct(rows.shape, U32),
        compiler_params=_params("arbitrary"),
    )(blk_e, rows, w_gate, w_up, w_down)


def _combine_kernel(gate_ref, h_ref, shared_ref, gf_ref, sh_ref, sc_ref, *rest):
    y_refs, outs = rest[:TOP_K], rest[TOP_K:]
    routed = None
    for k in range(TOP_K):
        y = _unpack_rows(jnp.concatenate([y_refs[k][i, 0] for i in range(ROW_PARTS)], axis=1))
        routed = gate_ref[:, k:k + 1] * y if routed is None else routed + gate_ref[:, k:k + 1] * y
    h = h_ref[...] + gf_ref[...] * (routed + shared_ref[...].astype(F32))
    outs[0][...] = h
    if len(outs) > 1:
        outs[1][...] = _rms(h) * (1.0 + sc_ref[...]) + sh_ref[...]


def combine(y_rows, dest, gate_tk, h, shared, gate_ffn, next_shift, next_scale, with_next):
    t, d = h.shape
    parts, cap, dq = y_rows.shape
    n_out = 2 if with_next else 1
    n_choice = dest.shape[0]
    flat = (dest[None] + (jnp.arange(parts, dtype=I32) * cap)[:, None, None]).reshape(1, parts * n_choice * t)
    picked = gather_rows(y_rows.reshape(parts * cap, dq), flat).reshape(parts, n_choice, t, dq)
    tm = min(256, t)
    row = lambda i: (i, 0)
    fixed = lambda i: (0, 0)
    choice = [pl.BlockSpec((parts, 1, tm, dq), functools.partial(lambda i, k: (0, k, i, 0), k=k))
              for k in range(n_choice)]
    return pl.pallas_call(
        _combine_kernel,
        grid=(t // tm,),
        in_specs=[pl.BlockSpec((tm, n_choice), row), pl.BlockSpec((tm, d), row), pl.BlockSpec((tm, d), row),
                  pl.BlockSpec((1, d), fixed), pl.BlockSpec((1, d), fixed), pl.BlockSpec((1, d), fixed)] + choice,
        out_specs=[pl.BlockSpec((tm, d), row)] * n_out,
        out_shape=[jax.ShapeDtypeStruct((t, d), F32)] * n_out,
        compiler_params=_params("parallel"),
    )(gate_tk, h, shared, gate_ffn, next_shift, next_scale, *([picked] * n_choice))


def moe_tail(h, f_rows, logits_t, shared, router_bias, layer, w_gate, w_up, w_down, gate_ffn,
             next_shift, next_scale, with_next):
    t, d = h.shape
    parts, _, dq = f_rows.shape
    n_exp = w_gate.shape[1]
    n_blocks = -(-(t * TOP_K + n_exp * (EXPERT_ROWS - 1)) // EXPERT_ROWS)
    cap = n_blocks * EXPERT_ROWS
    idx, gate, rank, counts = route(logits_t, router_bias)
    dest, blk_e, n_valid = dispatch_plan(idx, rank, counts, n_blocks)
    dest_all = jnp.concatenate([dest + i * cap for i in range(parts)], axis=1)
    rows = dispatch(f_rows.reshape(parts * t, dq), dest_all, parts * cap).reshape(parts, cap, dq)
    y_rows = experts(rows, blk_e, n_valid, layer, w_gate, w_up, w_down)
    return combine(y_rows, dest, gate.T, h, shared, gate_ffn, next_shift, next_scale, with_next)


def kernel(x, c, ctx, c_ctx, w_ada, b_ada, w_in, s5_lam_re, s5_lam_im, s5_log_dt, s5_b_re, s5_b_im,
           s5_c_re, s5_c_im, s5_d, s5_w_glu, s5_b_glu, na_q_gain, na_k_gain, na_rpb, w_mix_out,
           w_fourier_out, b_fourier_out, w_router, router_bias, w_exp_gate, w_exp_up, w_exp_down,
           w_sh_gate, w_sh_up, w_sh_down):
    bsz, t, d = x.shape
    assert bsz == 1 and w_ada.shape[0] == 2
    n_exp = w_router.shape[2]
    s5w = s5_w_glu.shape[1]
    naw = w_in.shape[2] - s5w
    naw //= 3
    heads = naw // NA_HEAD_DIM

    cond8 = jnp.concatenate([c[:1].astype(F32), c_ctx.astype(F32)[None], jnp.zeros((6, d), F32)], axis=0)
    ada = adaln_all(cond8, w_ada, b_ada)
    mod = lambda layer, who, j: ada[layer, who:who + 1, j * d:(j + 1) * d]

    def ffn_weights(i):
        return (mod(i, 0, 3), mod(i, 0, 4), jnp.transpose(w_router[i]).astype(F32),
                w_sh_gate[i].astype(BF16), w_sh_up[i].astype(BF16), w_sh_down[i].astype(BF16))

    h0 = x[0]
    seg = jnp.asarray(np.kron(np.eye(heads), np.ones((NA_HEAD_DIM, NA_HEAD_DIM))), BF16)
    w_in_b = w_in[0].astype(BF16)
    qg = jnp.tile(na_q_gain[0].astype(F32), heads)[None]
    kg = jnp.tile(na_k_gain[0].astype(F32), heads)[None]
    u_c, _, k_c, v_c = in_projection(ctx[0], mod(0, 1, 0), mod(0, 1, 1), w_in_b, seg, qg, kg, s5w, naw)
    u_l, q_l, k_l, v_l = in_projection(h0, mod(0, 0, 0), mod(0, 0, 1), w_in_b, seg, qg, kg, s5w, naw)
    mats = s5_matrices(s5_lam_re[0], s5_lam_im[0], s5_log_dt[0], s5_b_re[0], s5_b_im[0],
                       s5_c_re[0], s5_c_im[0], s5_d[0])
    y_s5 = s5_mixer(u_c, u_l, mats)
    na = neighbourhood_attention(q_l, k_l, v_l, k_c, v_c, na_bias_table(na_rpb[0], t // GRID_W))
    h1, f1, lg1, sh1 = _post_call(
        _even_post_kernel, [y_s5, na, h0],
        [s5_w_glu[0].astype(BF16), s5_b_glu[0].astype(F32)[None], w_mix_out[0].astype(BF16), mod(0, 0, 2),
         *ffn_weights(0)], t, d, n_exp)
    h2, a1 = moe_tail(h1, f1, lg1, sh1, router_bias[0], 0, w_exp_gate, w_exp_up, w_exp_down,
                      mod(0, 0, 5), mod(1, 0, 0), mod(1, 0, 1), True)

    zre, zim = time_dft(a1)
    cc, sc = channel_dft_tables(d // FOURIER_GROUPS)
    h3, f3, lg3, sh3 = _post_call(
        _odd_post_kernel, [zre, zim, h2],
        [cc, sc, w_fourier_out[0].astype(BF16), b_fourier_out[0].astype(F32)[None], mod(1, 0, 2),
         *ffn_weights(1)], t, d, n_exp)
    zero_row = jnp.zeros((1, d), F32)
    (out,) = moe_tail(h3, f3, lg3, sh3, router_bias[1], 1, w_exp_gate, w_exp_up, w_exp_down,
                      mod(1, 0, 5), zero_row, zero_row, False)
    return out[None]
```

```python
import functools
import math

import numpy as np
import jax
import jax.numpy as jnp
from jax import lax
from jax.experimental import pallas as pl
from jax.experimental.pallas import tpu as pltpu
from jax.experimental.pallas import tpu_sc as plsc

F32 = jnp.float32
BF16 = jnp.bfloat16
I32 = jnp.int32
U32 = jnp.uint32
HIGHEST = lax.Precision.HIGHEST

MXU_DEPTH = 256
LANES = 128
GRID_W = 64
NORM_EPS = 1e-6
S5_GROUP = 16
S5_STATE = 64
S5_LAMBDA_RE_MAX = -1e-4
S5_CHUNK = 16
S5_TILE = 128
NA_HEADS = 8
NA_HEAD_DIM = 64
NA_KH = 8
NA_KW = 16
NA_STEP_ROWS = 2
FOURIER_GROUPS = 4
N_EXPERT_GROUPS = 8
TOPK_GROUPS = 4
TOP_K = 8
ROUTED_SCALE = 2.5
EXPERT_ROWS = 1024
MOE_SPLIT = 2
ROW_PARTS = 2
SC_WINDOW = 128
SC_GATHER_SPLIT = 4
NEG_BIG = -1e30

VMEM_LIMIT_BYTES = 56 * 1024 * 1024


def _params(*sem):
    return pltpu.CompilerParams(dimension_semantics=sem or None,
                                vmem_limit_bytes=VMEM_LIMIT_BYTES)


def _rms(x):
    return x * lax.rsqrt(jnp.mean(x * x, axis=-1, keepdims=True) + NORM_EPS)


def _silu(x):
    return x * jax.nn.sigmoid(x)


def _pack_pair(lo, hi):
    lo = lax.bitcast_convert_type(lo.astype(BF16).astype(F32), U32)
    hi = lax.bitcast_convert_type(hi.astype(BF16).astype(F32), U32)
    return (hi & jnp.uint32(0xFFFF0000)) | (lo >> 16)


def _unpack_pair(w):
    return (lax.bitcast_convert_type(w << 16, F32), lax.bitcast_convert_type(w & jnp.uint32(0xFFFF0000), F32))


def _pack_rows(x):
    n = x.shape[1] // 2
    return _pack_pair(x[:, :n], x[:, n:])


def _unpack_rows(w):
    return jnp.concatenate(_unpack_pair(w), axis=1)


def _ada_kernel(c_ref, w_ref, b_ref, o_ref):
    o_ref[0] = jnp.dot(_silu(c_ref[...]), w_ref[0], preferred_element_type=F32,
                       precision=HIGHEST) + b_ref[0]


def adaln_all(cond8, w_ada, b_ada):
    n_layers, d, n6 = w_ada.shape
    tn = n6 // 4
    return pl.pallas_call(
        _ada_kernel,
        grid=(n_layers, n6 // tn),
        in_specs=[pl.BlockSpec((8, d), lambda l, j: (0, 0)),
                  pl.BlockSpec((1, d, tn), lambda l, j: (l, 0, j)),
                  pl.BlockSpec((1, 1, tn), lambda l, j: (l, 0, j))],
        out_specs=pl.BlockSpec((1, 8, tn), lambda l, j: (l, 0, j)),
        out_shape=jax.ShapeDtypeStruct((n_layers, 8, n6), F32),
        compiler_params=_params("parallel", "parallel"),
    )(cond8, w_ada, b_ada.reshape(n_layers, 1, n6))


def _inproj_kernel(x_ref, sh_ref, sc_ref, w_ref, seg_ref, qg_ref, kg_ref,
                   u_ref, q_ref, k_ref, v_ref):
    a = _rms(x_ref[...]) * (1.0 + sc_ref[...]) + sh_ref[...]
    z = jnp.dot(a.astype(BF16), w_ref[...], preferred_element_type=F32)
    s5w = u_ref.shape[1]
    naw = q_ref.shape[1]

    def head_norm(t, gain):
        ss = jnp.dot((t * t).astype(BF16), seg_ref[...], preferred_element_type=F32)
        return t * lax.rsqrt(ss * (1.0 / NA_HEAD_DIM) + NORM_EPS) * gain

    u_ref[...] = z[:, :s5w]
    q_ref[...] = head_norm(z[:, s5w:s5w + naw], qg_ref[...]).astype(BF16)
    k_ref[...] = head_norm(z[:, s5w + naw:s5w + 2 * naw], kg_ref[...]).astype(BF16)
    v_ref[...] = z[:, s5w + 2 * naw:].astype(BF16)


def in_projection(x, shift, scale, w_in_bf16, seg_ones, q_gain_row, k_gain_row, s5w, naw):
    t, d = x.shape
    tm = min(512, t)
    row = lambda i: (i, 0)
    fixed = lambda i: (0, 0)
    return pl.pallas_call(
        _inproj_kernel,
        grid=(t // tm,),
        in_specs=[pl.BlockSpec((tm, d), row),
                  pl.BlockSpec((1, d), fixed), pl.BlockSpec((1, d), fixed),
                  pl.BlockSpec(w_in_bf16.shape, fixed),
                  pl.BlockSpec(seg_ones.shape, fixed),
                  pl.BlockSpec((1, naw), fixed), pl.BlockSpec((1, naw), fixed)],
        out_specs=[pl.BlockSpec((tm, s5w), row), pl.BlockSpec((tm, naw), row),
                   pl.BlockSpec((tm, naw), row), pl.BlockSpec((tm, naw), row)],
        out_shape=[jax.ShapeDtypeStruct((t, s5w), F32)] + [jax.ShapeDtypeStruct((t, naw), BF16)] * 3,
        compiler_params=_params("parallel"),
    )(x, shift, scale, w_in_bf16, seg_ones, q_gain_row, k_gain_row)


def s5_matrices(lam_re, lam_im, log_dt, b_re, b_im, c_re, c_im, d_skip):
    L = S5_CHUNK
    c = S5_GROUP
    taus = jnp.arange(L + 1, dtype=F32)

    def direction(i):
        lam = lax.complex(jnp.minimum(lam_re[i].astype(F32), S5_LAMBDA_RE_MAX), lam_im[i].astype(F32))
        ldt = lam * jnp.exp(log_dt[i].astype(F32))[:, None]
        lam_bar = jnp.exp(ldt)
        b_bar = ((lam_bar - 1.0) / lam)[..., None] * lax.complex(b_re[i].astype(F32), b_im[i].astype(F32))
        cc = lax.complex(c_re[i].astype(F32), c_im[i].astype(F32))
        powers = jnp.exp(ldt[None] * taus[:, None, None])
        resp = jnp.real(jnp.einsum('gcp,tgp,gpd->gctd', cc, powers[:L], b_bar, precision=HIGHEST))
        return jnp.transpose(powers, (1, 2, 0)), b_bar, cc, resp

    pw_f, bb_f, cc_f, k_f = direction(0)
    pw_b, bb_b, cc_b, k_b = direction(1)
    g, p = pw_f.shape[:2]
    lag0 = k_f[:, :, :1] + k_b[:, :, :1] + (jnp.eye(c, dtype=F32)[None, :, None, :] * d_skip.astype(F32)[:, :, None, None])
    by_lag = jnp.concatenate([jnp.flip(k_f[:, :, 1:], axis=2), lag0, k_b[:, :, 1:]], axis=2).reshape(g, c, (2 * L - 1) * c)
    toep_t = jnp.stack([by_lag[:, :, (L - 1 - l) * c:(2 * L - 1 - l) * c] for l in range(L)], axis=1)
    toep_t = toep_t.reshape(g, L * c, L * c)

    def state_in(pw_by_s, b_bar):
        return (pw_by_s[:, :, :, None] * b_bar[:, :, None, :]).reshape(g, p, L * c)

    wf = state_in(jnp.flip(pw_f[:, :, :L], axis=2), bb_f)
    wb = state_in(pw_b[:, :, :L], bb_b)
    w_state_t = jnp.concatenate([jnp.real(wf), jnp.imag(wf), jnp.imag(wf), jnp.real(wf),
                                 jnp.real(wb), jnp.imag(wb), jnp.imag(wb), jnp.real(wb)], axis=1)

    def state_out(pw_by_l, cc):
        return (jnp.transpose(pw_by_l, (0, 2, 1))[:, :, None, :] * cc[:, None, :, :]).reshape(g, L * c, p)

    rf = state_out(pw_f[:, :, 1:], cc_f)
    rb = state_out(jnp.flip(pw_b[:, :, 1:], axis=2), cc_b)
    r_state_t = jnp.concatenate([jnp.real(rf), -jnp.imag(rf), jnp.real(rb), -jnp.imag(rb)], axis=-1)

    def mult(a):
        ar, ai = jnp.real(a), jnp.imag(a)
        return jnp.stack([jnp.concatenate([ar, ar], -1), jnp.concatenate([-ai, ai], -1),
                          jnp.concatenate([ai, -ai], -1)])

    return (toep_t.astype(BF16), w_state_t.astype(BF16), r_state_t.astype(BF16),
            mult(pw_f[:, :, L]), mult(pw_b[:, :, L]))


def _s5_pack_kernel(*refs):
    u_refs, (wt_ref, ut_ref, f1_ref, f2_ref, b1_ref, b2_ref) = refs[:-6], refs[-6:]
    L = S5_CHUNK
    g, lc, nck = ut_ref.shape
    c = lc // L
    gs = g // len(u_refs)
    for s in range(L):
        for j, u_ref in enumerate(u_refs):
            step_s = u_ref[pl.ds(s, nck, stride=L), :]
            ut_ref[j * gs:(j + 1) * gs, s * c:(s + 1) * c, :] = (
                jnp.transpose(step_s).astype(BF16).reshape(gs, c, nck))
    n = f1_ref.shape[2]
    for gi in range(g):
        inc = jnp.dot(wt_ref[gi], ut_ref[gi], preferred_element_type=F32)
        for j, ref in enumerate((f1_ref, f2_ref, b1_ref, b2_ref)):
            ref[:, gi, :] = jnp.transpose(inc[j * n:(j + 1) * n, :])


def s5_pack(u, wt_state):
    t, w = u.shape
    g, n4, lc = wt_state.shape
    nc = t // S5_CHUNK
    tile = min(S5_TILE, nc)
    inc = jax.ShapeDtypeStruct((nc, g, n4 // 4), F32)
    inc_blk = pl.BlockSpec((tile, g, n4 // 4), lambda i: (i, 0, 0))
    return pl.pallas_call(
        _s5_pack_kernel,
        grid=(nc // tile,),
        in_specs=[pl.BlockSpec((tile * S5_CHUNK, LANES), functools.partial(lambda i, j: (i, j), j=j))
                  for j in range(w // LANES)] + [pl.BlockSpec(wt_state.shape, lambda i: (0, 0, 0))],
        out_specs=[pl.BlockSpec((g, lc, tile), lambda i: (0, 0, i))] + [inc_blk] * 4,
        out_shape=[jax.ShapeDtypeStruct((g, lc, nc), BF16)] + [inc] * 4,
        compiler_params=_params("parallel"),
    )(*([u] * (w // LANES)), wt_state)


def _s5_scan_kernel(s1_ref, s2_ref, m_ref, init_ref, x_ref, last_ref, v1_ref, v2_ref, *, reverse):
    @pl.when(pl.program_id(0) == 0)
    def _():
        v1_ref[...] = init_ref[0]
        v2_ref[...] = init_ref[1]

    a1, a2, a3 = m_ref[0], m_ref[1], m_ref[2]
    cb = s1_ref.shape[0]

    def body(j, carry):
        v1, v2 = carry
        jj = cb - 1 - j if reverse else j
        x_ref[jj] = v1
        return (a1 * v1 + a2 * v2 + s1_ref[jj], a1 * v2 + a3 * v1 + s2_ref[jj])

    v1, v2 = lax.fori_loop(0, cb, body, (v1_ref[...], v2_ref[...]))
    v1_ref[...] = v1
    v2_ref[...] = v2
    last_ref[...] = v1


def s5_chunk_scan(s1, s2, mult, init, reverse):
    nc, g, n = s1.shape
    cb = min(S5_TILE, nc)
    nb = nc // cb
    blk = (lambda i: (nb - 1 - i, 0, 0)) if reverse else (lambda i: (i, 0, 0))
    return pl.pallas_call(
        functools.partial(_s5_scan_kernel, reverse=reverse),
        grid=(nb,),
        in_specs=[pl.BlockSpec((cb, g, n), blk), pl.BlockSpec((cb, g, n), blk),
                  pl.BlockSpec((3, g, n), lambda i: (0, 0, 0)), pl.BlockSpec((2, g, n), lambda i: (0, 0, 0))],
        out_specs=[pl.BlockSpec((cb, g, n), blk), pl.BlockSpec((g, n), lambda i: (0, 0))],
        out_shape=[jax.ShapeDtypeStruct((nc, g, n), F32), jax.ShapeDtypeStruct((g, n), F32)],
        scratch_shapes=[pltpu.VMEM((g, n), F32), pltpu.VMEM((g, n), F32)],
        compiler_params=_params("arbitrary"),
    )(s1, s2, mult, init)


def _s5_readout_kernel(ut_ref, tt_ref, rt_ref, xf_ref, xb_ref, y_ref, yt_ref, *slab_refs):
    L = S5_CHUNK
    g, lc, nck = ut_ref.shape
    c = lc // L
    gs = g // len(slab_refs)
    for gi in range(g):
        xin_t = jnp.concatenate([jnp.transpose(xf_ref[:, gi, :]), jnp.transpose(xb_ref[:, gi, :])], axis=0)
        yt_ref[gi] = (jnp.dot(tt_ref[gi], ut_ref[gi], preferred_element_type=F32)
                      + jnp.dot(rt_ref[gi], xin_t.astype(BF16), preferred_element_type=F32))
    for j, slab in enumerate(slab_refs):
        for l in range(L):
            step_l = yt_ref[j * gs:(j + 1) * gs, l * c:(l + 1) * c, :].reshape(gs * c, nck)
            slab[pl.ds(l, nck, stride=L), :] = jnp.transpose(step_l)
        y_ref[:, j * gs * c:(j + 1) * gs * c] = slab[...]


def s5_readout(ut, toep_t, r_state_t, xin_f, xin_b):
    g, lc, nc = ut.shape
    n = xin_f.shape[2]
    tile = min(S5_TILE, nc)
    fixed = lambda i: (0, 0, 0)
    state_blk = pl.BlockSpec((tile, g, n), lambda i: (i, 0, 0))
    return pl.pallas_call(
        _s5_readout_kernel,
        grid=(nc // tile,),
        in_specs=[pl.BlockSpec((g, lc, tile), lambda i: (0, 0, i)),
                  pl.BlockSpec(toep_t.shape, fixed), pl.BlockSpec(r_state_t.shape, fixed), state_blk, state_blk],
        out_specs=pl.BlockSpec((tile * S5_CHUNK, g * lc // S5_CHUNK), lambda i: (i, 0)),
        out_shape=jax.ShapeDtypeStruct((nc * S5_CHUNK, g * lc // S5_CHUNK), F32),
        scratch_shapes=[pltpu.VMEM((g, lc, tile), F32)]
        + [pltpu.VMEM((tile * S5_CHUNK, LANES), F32)] * (g * lc // S5_CHUNK // LANES),
        compiler_params=_params("parallel"),
    )(ut, toep_t, r_state_t, xin_f, xin_b)


def s5_mixer(u_ctx, u_lat, mats):
    toep_t, wt_state, r_state_t, mult_f, mult_b = mats
    L = S5_CHUNK
    g, n = mult_f.shape[1:]
    halves = lambda v: jnp.stack([v, jnp.roll(v, n // 2, axis=-1)])
    n_ctx = u_ctx.shape[0] // L
    ctx_chunks = -(-(n_ctx + 1) // S5_TILE) * S5_TILE
    ctx_pad = jnp.pad(u_ctx, ((0, ctx_chunks * L - u_ctx.shape[0]), (0, 0)))
    _, cf1, cf2, cb1, cb2 = s5_pack(ctx_pad, wt_state)
    zero = jnp.zeros((2, g, n), F32)
    ctx_f, _ = s5_chunk_scan(cf1, cf2, mult_f, zero, False)
    _, ctx_b_last = s5_chunk_scan(cb1, cb2, mult_b, zero, True)
    ut, f1, f2, b1, b2 = s5_pack(u_lat, wt_state)
    xin_f, _ = s5_chunk_scan(f1, f2, mult_f, halves(ctx_f[n_ctx]), False)
    xin_b, _ = s5_chunk_scan(b1, b2, mult_b, halves(ctx_b_last), True)
    return s5_readout(ut, toep_t, r_state_t, xin_f, xin_b)


def _na_window(rows):
    step = NA_STEP_ROWS
    assert rows % step == 0 and (NA_KH // 2) % step == 0 and rows >= NA_KH + step
    groups = rows // step
    n_blk = (NA_KH + step) // step
    first = np.clip(np.arange(groups) - NA_KH // 2 // step, 0, groups - n_blk)
    return groups, n_blk, first


def na_bias_table(rpb, rows):
    step = NA_STEP_ROWS
    groups, n_blk, first = _na_window(rows)
    win = n_blk * step
    q_col = np.arange(GRID_W)
    col_start = np.clip(q_col - NA_KW // 2, 0, GRID_W - NA_KW)
    key_col = np.arange(GRID_W)
    off = key_col[None, :] - col_start[:, None]
    valid_col = (off >= 0) & (off < NA_KW)
    rel_col = np.clip(key_col[None, :] - q_col[:, None] + NA_KW - 1, 0, 2 * NA_KW - 2)
    q_row = step * np.arange(groups)[:, None, None] + np.arange(step)[None, :, None]
    key_row = step * first[:, None, None] + np.arange(win)[None, None, :]
    row0 = np.clip(q_row - NA_KH // 2, 0, rows - NA_KH)
    rel_row = np.where((key_row >= row0) & (key_row < row0 + NA_KH), key_row - q_row + NA_KH - 1, -1)
    variant = np.arange(groups) - first
    reps = [int(np.argmax(variant == d)) for d in range(n_blk)]
    assert all((rel_row[g] == rel_row[reps[variant[g]]]).all() for g in range(groups))
    rel_row = rel_row[reps]
    rel_row = np.where(rel_row >= 0, rel_row, 2 * NA_KH - 1)
    rel_col = np.where(valid_col, rel_col, 2 * NA_KW - 1)
    pick_row = jnp.asarray(np.arange(2 * NA_KH)[:, None, None, None] == rel_row[None], F32)
    pick_col = jnp.asarray(np.arange(2 * NA_KW)[:, None, None] == rel_col[None], F32)
    rpb_ext = jnp.pad(rpb.astype(F32), ((0, 0), (0, 1), (0, 1)), constant_values=NEG_BIG)
    table = jnp.einsum('hrx,rvji,xck->vhjcik', rpb_ext, pick_row, pick_col, precision=HIGHEST)
    return table.reshape(n_blk, rpb.shape[0], step * GRID_W, win * GRID_W)


def _na_kernel(*refs, scale, n_blk):
    q_ref = refs[0]
    k_refs = refs[1:1 + n_blk]
    v_refs = refs[1 + n_blk:1 + 2 * n_blk]
    kc_ref, vc_ref, b_ref, o_ref = refs[1 + 2 * n_blk:]
    hd = NA_HEAD_DIM
    width = q_ref.shape[1]
    span = min(MXU_DEPTH, width)
    nt = (((1,), (1,)), ((), ()))
    q = q_ref[...] * scale
    kk = jnp.concatenate([r[...] for r in k_refs], axis=0)
    vv = jnp.concatenate([r[...] for r in v_refs], axis=0)
    kc, vc = kc_ref[...], vc_ref[...]
    nq = q.shape[0]
    per = span // hd
    lane = lax.broadcasted_iota(I32, (nq, span), 1)
    own = [(lane >= j * hd) & (lane < (j + 1) * hd) for j in range(per)]
    out_cols = []
    for c0 in range(0, width, span):
        cols = slice(c0, c0 + span)
        qs = jnp.concatenate([jnp.where(own[j], q[:, cols], jnp.zeros_like(q[:, cols])) for j in range(per)], axis=0)
        h0 = c0 // hd
        bias = b_ref[0, h0:h0 + per].reshape(per * nq, kk.shape[0])
        s = lax.dot_general(qs, kk[:, cols], nt, preferred_element_type=F32) + bias
        sc = lax.dot_general(qs, kc[:, cols], nt, preferred_element_type=F32)
        m = jnp.maximum(jnp.max(s, axis=-1, keepdims=True), jnp.max(sc, axis=-1, keepdims=True))
        p = jnp.exp(s - m)
        pc = jnp.exp(sc - m)
        den = jnp.sum(p, axis=-1, keepdims=True) + jnp.sum(pc, axis=-1, keepdims=True)
        o = (jnp.dot(p.astype(BF16), vv[:, cols], preferred_element_type=F32)
             + jnp.dot(pc.astype(BF16), vc[:, cols], preferred_element_type=F32)) / den
        acc = jnp.zeros((nq, span), F32)
        for j in range(per):
            acc = jnp.where(own[j], o[j * nq:(j + 1) * nq], acc)
        out_cols.append(acc)
    o_ref[...] = jnp.concatenate(out_cols, axis=1).astype(o_ref.dtype)


def neighbourhood_attention(q, k, v, k_ctx, v_ctx, bias_table):
    t, w = q.shape
    groups, n_blk, _ = _na_window(t // GRID_W)
    first = lambda g: jnp.clip(g - NA_KH // 2 // NA_STEP_ROWS, 0, groups - n_blk)
    blk_tokens = NA_STEP_ROWS * GRID_W
    row_blk = pl.BlockSpec((blk_tokens, w), lambda g: (g, 0))
    key_blks = [pl.BlockSpec((blk_tokens, w), functools.partial(lambda g, i: (first(g) + i, 0), i=i))
                for i in range(n_blk)]
    ctx_blk = pl.BlockSpec(k_ctx.shape, lambda g: (0, 0))
    bias_blk = pl.BlockSpec((1,) + bias_table.shape[1:], lambda g: (g - first(g), 0, 0, 0))
    return pl.pallas_call(
        functools.partial(_na_kernel, scale=NA_HEAD_DIM ** -0.5, n_blk=n_blk),
        grid=(groups,),
        in_specs=[row_blk] + key_blks + key_blks + [ctx_blk, ctx_blk, bias_blk],
        out_specs=row_blk,
        out_shape=jax.ShapeDtypeStruct((t, w), BF16),
        compiler_params=_params("parallel"),
    )(q, *([k] * n_blk), *([v] * n_blk), k_ctx, v_ctx, bias_table)


def time_dft_tables(t):
    a_len = 1 << (int(math.log2(t)) // 2)
    b_len = t // a_len
    ka = np.arange(a_len)[:, None]
    tok = b_len * np.arange(a_len)[None, :]
    ang1 = -2.0 * np.pi * ((ka * (tok[None] + np.arange(b_len)[:, None, None])) % t) / t
    stage1 = np.concatenate([np.cos(ang1), np.sin(ang1)], axis=1) / math.sqrt(t)
    ang2 = 2.0 * np.pi * ((np.arange(b_len)[:, None] * np.arange(b_len)[None, :]) % b_len) / b_len
    c2, s2 = np.cos(ang2), np.sin(ang2)
    stage2 = np.block([[c2, s2], [-s2, c2]])
    return jnp.asarray(stage1, BF16), jnp.asarray(stage2, BF16), a_len, b_len


def _time_dft_kernel(x_ref, m_ref, w2_ref, zre_ref, zim_ref, y_ref, z_ref, *, a_len, b_len):
    i = pl.program_id(1)
    bb = m_ref.shape[0]

    def stage1(jb, carry):
        b = i * bb + jb
        xb = x_ref[pl.ds(b, a_len, stride=b_len), :]
        y = jnp.dot(m_ref[jb], xb.astype(BF16), preferred_element_type=F32)
        row = pl.multiple_of(b * a_len, a_len)
        y_ref[pl.ds(row, a_len), :] = _pack_pair(y[:a_len], y[a_len:])
        return carry

    lax.fori_loop(0, bb, stage1, 0, unroll=16)

    @pl.when(i == pl.num_programs(1) - 1)
    def _():
        def stage2(ka, carry):
            rows = pl.ds(ka, b_len, stride=a_len)
            yre, yim = _unpack_pair(y_ref[rows, :])
            y = jnp.concatenate([yre, yim], axis=0).astype(BF16)
            z = jnp.dot(w2_ref[...], y, preferred_element_type=F32)
            z_ref[rows, :] = _pack_pair(z[:b_len], z[b_len:])
            return carry

        lax.fori_loop(0, a_len, stage2, 0, unroll=16)
        zre, zim = _unpack_pair(z_ref[...])
        zre_ref[...] = zre.astype(zre_ref.dtype)
        zim_ref[...] = zim.astype(zim_ref.dtype)


def time_dft(x):
    t, d = x.shape
    stage1, stage2, a_len, b_len = time_dft_tables(t)
    lanes = 128
    bb = min(16, b_len)
    out = jax.ShapeDtypeStruct((t, d), BF16)
    return pl.pallas_call(
        functools.partial(_time_dft_kernel, a_len=a_len, b_len=b_len),
        grid=(d // lanes, b_len // bb),
        in_specs=[pl.BlockSpec((t, lanes), lambda j, i: (0, j)),
                  pl.BlockSpec((bb, 2 * a_len, a_len), lambda j, i: (i, 0, 0)),
                  pl.BlockSpec(stage2.shape, lambda j, i: (0, 0))],
        out_specs=[pl.BlockSpec((t, lanes), lambda j, i: (0, j))] * 2,
        out_shape=[out, out],
        scratch_shapes=[pltpu.VMEM((t, lanes), U32), pltpu.VMEM((t, lanes), U32)],
        compiler_params=_params("parallel", "arbitrary"),
    )(x, stage1, stage2)


def channel_dft_tables(c):
    ang = 2.0 * np.pi * ((np.arange(c)[:, None] * np.arange(c)[None, :]) % c) / c
    return (jnp.asarray(np.cos(ang) / math.sqrt(c), BF16), jnp.asarray(np.sin(ang) / math.sqrt(c), BF16))


def _ffn_prologue(h, shf_ref, scf_ref, wr_ref, wsg_ref, wsu_ref, wsd_ref, h_ref, f_ref, lg_ref, shared_ref):
    h_ref[...] = h
    f = _rms(h) * (1.0 + scf_ref[...]) + shf_ref[...]
    packed = _pack_rows(f)
    half = packed.shape[1] // 2
    f_ref[0, 0] = packed[:, :half]
    f_ref[0, 1] = packed[:, half:]
    fb = f.astype(BF16)
    f_lo = (f - fb.astype(F32)).astype(BF16)
    nt = (((1,), (1,)), ((), ()))
    wr = wr_ref[...]
    wr_hi = wr.astype(BF16)
    wr_lo = (wr - wr_hi.astype(F32)).astype(BF16)
    lg_ref[0] = (lax.dot_general(wr_hi, fb, nt, preferred_element_type=F32)
                   + lax.dot_general(wr_hi, f_lo, nt, preferred_element_type=F32)
                   + lax.dot_general(wr_lo, fb, nt, preferred_element_type=F32))
    hid = (_silu(jnp.dot(fb, wsg_ref[...], preferred_element_type=F32))
           * jnp.dot(fb, wsu_ref[...], preferred_element_type=F32))
    shared_ref[...] = jnp.dot(hid.astype(BF16), wsd_ref[...], preferred_element_type=F32).astype(shared_ref.dtype)


def _gelu_tanh(x):
    return 0.5 * x * (1.0 + jnp.tanh(math.sqrt(2.0 / math.pi) * (x + 0.044715 * (x * x * x))))


def _even_post_kernel(y_ref, na_ref, x_ref, wglu_ref, bglu_ref, wo_ref, gm_ref, *rest):
    g = _gelu_tanh(y_ref[...])
    gate = jax.nn.sigmoid(jnp.dot(g.astype(BF16), wglu_ref[...], preferred_element_type=F32) + bglu_ref[...])
    s5 = (g * gate).astype(BF16)
    w = s5.shape[1]
    mix = (jnp.dot(s5, wo_ref[:w, :], preferred_element_type=F32)
           + jnp.dot(na_ref[...], wo_ref[w:, :], preferred_element_type=F32))
    _ffn_prologue(x_ref[...] + gm_ref[...] * mix, *rest)


def _odd_post_kernel(zre_ref, zim_ref, h_ref_in, cc_ref, sc_ref, wf_ref, bf_ref, gm_ref, *rest):
    c = cc_ref.shape[0]
    parts = []
    for grp in range(zre_ref.shape[1] // c):
        cols = slice(grp * c, (grp + 1) * c)
        parts.append(jnp.dot(zre_ref[:, cols], cc_ref[...], preferred_element_type=F32)
                     + jnp.dot(zim_ref[:, cols], sc_ref[...], preferred_element_type=F32))
    fr = jnp.concatenate(parts, axis=-1).astype(BF16)
    mix = jnp.dot(fr, wf_ref[...], preferred_element_type=F32) + bf_ref[...]
    _ffn_prologue(h_ref_in[...] + gm_ref[...] * mix, *rest)


def _post_call(body, row_inputs, fixed_inputs, t, d, n_exp):
    tm = min(512, t // MOE_SPLIT)
    per = t // MOE_SPLIT // tm
    row = lambda i: (i, 0)
    in_specs = ([pl.BlockSpec((tm, a.shape[1]), row) for a in row_inputs]
                + [pl.BlockSpec(a.shape, functools.partial(lambda i, nd: (0,) * nd, nd=a.ndim))
                   for a in fixed_inputs])
    return pl.pallas_call(
        body,
        grid=(t // tm,),
        in_specs=in_specs,
        out_specs=[pl.BlockSpec((tm, d), row),
                   pl.BlockSpec((1, ROW_PARTS, tm, d // 4), lambda i: (i // per, 0, i % per, 0)),
                   pl.BlockSpec((1, n_exp, tm), lambda i: (i // per, 0, i % per)), pl.BlockSpec((tm, d), row)],
        out_shape=[jax.ShapeDtypeStruct((t, d), F32),
                   jax.ShapeDtypeStruct((MOE_SPLIT, ROW_PARTS, t // MOE_SPLIT, d // 4), U32),
                   jax.ShapeDtypeStruct((MOE_SPLIT, n_exp, t // MOE_SPLIT), F32), jax.ShapeDtypeStruct((t, d), BF16)],
        compiler_params=_params("parallel"),
    )(*row_inputs, *fixed_inputs)


def _route_kernel(lg_ref, bias_ref, tri_ref, idx_ref, gate_ref, rank_ref, cnt_ref, run_ref):
    @pl.when(pl.program_id(0) == 0)
    def _():
        run_ref[...] = jnp.zeros_like(run_ref)

    scores = jax.nn.sigmoid(lg_ref[...])
    n_exp, tb = scores.shape
    sel = scores + bias_ref[...]
    gsz = n_exp // N_EXPERT_GROUPS
    member = lax.broadcasted_iota(I32, (gsz, tb), 0)
    gscore = []
    for grp in range(N_EXPERT_GROUPS):
        xg = sel[grp * gsz:(grp + 1) * gsz, :]
        m1 = jnp.max(xg, axis=0, keepdims=True)
        first = jnp.min(jnp.where(xg == m1, member, gsz), axis=0, keepdims=True)
        m2 = jnp.max(jnp.where(member == first, -jnp.inf, xg), axis=0, keepdims=True)
        gscore.append(m1 + m2)
    keep_rows = []
    for grp in range(N_EXPERT_GROUPS):
        beaten = jnp.zeros((1, tb), F32)
        for other in range(N_EXPERT_GROUPS):
            if other == grp:
                continue
            wins = (gscore[other] >= gscore[grp]) if other < grp else (gscore[other] > gscore[grp])
            beaten = beaten + jnp.where(wins, 1.0, 0.0)
        keep_rows.append(jnp.broadcast_to(beaten < TOPK_GROUPS, (gsz, tb)))
    masked = jnp.where(jnp.concatenate(keep_rows, axis=0), sel, -jnp.inf)

    expert = lax.broadcasted_iota(I32, (n_exp, tb), 0)
    picks, gates, hots = [], [], []
    chosen = jnp.zeros((n_exp, tb), F32)
    for _ in range(TOP_K):
        m = jnp.max(masked, axis=0, keepdims=True)
        pick = jnp.min(jnp.where(masked == m, expert, n_exp), axis=0, keepdims=True)
        hot = expert == pick
        picks.append(pick)
        hots.append(hot)
        gates.append(jnp.sum(jnp.where(hot, scores, 0.0), axis=0, keepdims=True))
        chosen = jnp.where(hot, 1.0, chosen)
        masked = jnp.where(hot, -jnp.inf, masked)
    total = gates[0]
    for gk in gates[1:]:
        total = total + gk
    ahead = jnp.dot(chosen.astype(BF16), tri_ref[...], preferred_element_type=F32) + run_ref[...]
    for k in range(TOP_K):
        idx_ref[k:k + 1, :] = picks[k]
        gate_ref[k:k + 1, :] = ROUTED_SCALE * gates[k] / total
        rank_ref[k:k + 1, :] = jnp.sum(jnp.where(hots[k], ahead, 0.0), axis=0, keepdims=True).astype(I32)
    run_ref[...] = run_ref[...] + jnp.sum(chosen, axis=1, keepdims=True)
    cnt_ref[...] = jnp.broadcast_to(run_ref[...], cnt_ref.shape)


def route(logits_t, router_bias):
    n_exp, t = logits_t.shape
    tb = min(512, t)
    tri = jnp.asarray(np.triu(np.ones((tb, tb), np.float32), k=1), BF16)
    tok = lambda i: (0, i)
    idx, gate, rank, cnt = pl.pallas_call(
        _route_kernel,
        grid=(t // tb,),
        in_specs=[pl.BlockSpec((n_exp, tb), tok), pl.BlockSpec((n_exp, 1), lambda i: (0, 0)),
                  pl.BlockSpec((tb, tb), lambda i: (0, 0))],
        out_specs=[pl.BlockSpec((TOP_K, tb), tok)] * 3 + [pl.BlockSpec((n_exp, 128), lambda i: (0, 0))],
        out_shape=[jax.ShapeDtypeStruct((TOP_K, t), I32), jax.ShapeDtypeStruct((TOP_K, t), F32),
                   jax.ShapeDtypeStruct((TOP_K, t), I32), jax.ShapeDtypeStruct((n_exp, 128), F32)],
        scratch_shapes=[pltpu.VMEM((n_exp, 1), F32)],
        compiler_params=_params("arbitrary"),
    )(logits_t, router_bias.astype(F32).reshape(n_exp, 1), tri)
    return idx, gate, rank, cnt[:, 0].astype(I32)


def dispatch_plan(idx, rank, counts, n_blocks):
    n_exp = counts.shape[0]
    padded = (counts + EXPERT_ROWS - 1) // EXPERT_ROWS * EXPERT_ROWS
    pad_end = jnp.cumsum(padded)
    pad_start = pad_end - padded
    experts_iota = jnp.arange(n_exp, dtype=I32)
    dest = jnp.sum(jnp.where(idx[..., None] == experts_iota, pad_start, 0), axis=-1) + rank
    n_valid = (pad_end[-1] // EXPERT_ROWS).astype(I32)
    blk = jnp.minimum(jnp.arange(n_blocks, dtype=I32), n_valid - 1)
    blk_e = jnp.sum((pad_end[None, :] <= blk[:, None] * EXPERT_ROWS).astype(I32), axis=1)
    return dest.astype(I32), jnp.minimum(blk_e, n_exp - 1).astype(I32), n_valid.reshape(1)


def _sc_mesh():
    return plsc.VectorSubcoreMesh(core_axis_name="core", subcore_axis_name="subcore")


def dispatch(f, dest, cap):
    t, w = f.shape
    n_choice = dest.shape[0]

    @functools.partial(pl.kernel, out_type=jax.ShapeDtypeStruct((cap, w), f.dtype), mesh=_sc_mesh(),
                       scratch_types=[pltpu.SemaphoreType.DMA])
    def scatter_rows(x_hbm, i_hbm, o_hbm, sem):
        def body(x_vmem, i_vmem):
            copies = [pltpu.async_copy(x_vmem, o_hbm.at[i_vmem.at[k]], sem) for k in range(n_choice)]
            for cp in copies:
                cp.wait()

        pltpu.emit_pipeline(
            body,
            grid=(t // SC_WINDOW,),
            in_specs=[pl.BlockSpec((SC_WINDOW, w), lambda i: (i, 0)),
                      pl.BlockSpec((n_choice, SC_WINDOW), lambda i: (0, i))],
            out_specs=[],
            core_axis_name=("core", "subcore"),
            dimension_semantics=(pltpu.PARALLEL,),
        )(x_hbm, i_hbm)

    return scatter_rows(f, dest)


def gather_rows(rows, index_row):
    n = index_row.shape[1]
    w = rows.shape[1]

    piece = SC_WINDOW // SC_GATHER_SPLIT

    @functools.partial(pl.kernel, out_type=jax.ShapeDtypeStruct((n, w), rows.dtype), mesh=_sc_mesh(),
                       scratch_types=[pltpu.SemaphoreType.DMA])
    def gather(y_hbm, i_hbm, o_hbm, sem):
        def body(i_vmem, o_vmem):
            copies = [pltpu.async_copy(y_hbm.at[i_vmem.at[0, pl.ds(j * piece, piece)]],
                                       o_vmem.at[pl.ds(j * piece, piece)], sem)
                      for j in range(SC_GATHER_SPLIT)]
            for cp in copies:
                cp.wait()

        pltpu.emit_pipeline(
            body,
            grid=(n // SC_WINDOW,),
            in_specs=[pl.BlockSpec((1, SC_WINDOW), lambda i: (0, i))],
            out_specs=[pl.BlockSpec((SC_WINDOW, w), lambda i: (i, 0))],
            core_axis_name=("core", "subcore"),
            dimension_semantics=(pltpu.PARALLEL,),
        )(i_hbm, o_hbm)

    return gather(rows, index_row)


def _experts_kernel(be_ref, x_ref, wg_ref, wu_ref, wd_ref, y_ref, wg_bf, wu_bf, wd_bf):
    b = pl.program_id(0)

    @pl.when((b == 0) | (be_ref[b] != be_ref[jnp.maximum(b - 1, 0)]))
    def _():
        wg_bf[...] = wg_ref[0, 0].astype(BF16)
        wu_bf[...] = wu_ref[0, 0].astype(BF16)
        wd_bf[...] = wd_ref[0, 0].astype(BF16)

    x = _unpack_rows(jnp.concatenate([x_ref[i] for i in range(ROW_PARTS)], axis=1)).astype(BF16)
    hid = (_silu(jnp.dot(x, wg_bf[...], preferred_element_type=F32))
           * jnp.dot(x, wu_bf[...], preferred_element_type=F32))
    packed = _pack_rows(jnp.dot(hid.astype(BF16), wd_bf[...], preferred_element_type=F32))
    dq = packed.shape[1] // ROW_PARTS
    for i in range(ROW_PARTS):
        y_ref[i] = packed[:, i * dq:(i + 1) * dq]


def experts(rows, blk_e, n_valid, layer, w_gate, w_up, w_down):
    parts, cap, dq = rows.shape
    _, _, d, ff = w_gate.shape
    blk = pl.BlockSpec((parts, EXPERT_ROWS, dq), lambda b, be: (0, b, 0))
    return pl.pallas_call(
        _experts_kernel,
        grid_spec=pltpu.PrefetchScalarGridSpec(
            num_scalar_prefetch=1,
            grid=(n_valid[0],),
            in_specs=[blk,
                      pl.BlockSpec((1, 1, d, ff), lambda b, be: (layer, be[b], 0, 0)),
                      pl.BlockSpec((1, 1, d, ff), lambda b, be: (layer, be[b], 0, 0)),
                      pl.BlockSpec((1, 1, ff, d), lambda b, be: (layer, be[b], 0, 0))],
            out_specs=blk,
            scratch_shapes=[pltpu.VMEM((d, ff), BF16), pltpu.VMEM((d, ff), BF16), pltpu.VMEM((ff, d), BF16)],
        ),
        out_shape=jax.ShapeDtypeStruct(rows.shape, U32),
        compiler_params=_params("arbitrary"),
    )(blk_e, rows, w_gate, w_up, w_down)


def _combine_kernel(gate_ref, h_ref, shared_ref, gf_ref, sh_ref, sc_ref, *rest, n_prev):
    y_refs, outs = rest[:TOP_K], rest[TOP_K + n_prev:]
    routed = None
    for k in range(TOP_K):
        y = _unpack_rows(jnp.concatenate([y_refs[k][i, 0] for i in range(ROW_PARTS)], axis=1))
        routed = gate_ref[:, k:k + 1] * y if routed is None else routed + gate_ref[:, k:k + 1] * y
    h = h_ref[...] + gf_ref[...] * (routed + shared_ref[...].astype(F32))
    outs[0][...] = h
    if len(outs) > 1:
        outs[1][...] = _rms(h) * (1.0 + sc_ref[...]) + sh_ref[...]


def combine(y_rows, dest, gate_tk, h, shared, gate_ffn, next_shift, next_scale, with_next, first_row, prev_outs):
    t, d = h.shape
    parts, cap, dq = y_rows.shape
    n_out = 2 if with_next else 1
    n_choice, tr = dest.shape
    flat = (dest[None] + (jnp.arange(parts, dtype=I32) * cap)[:, None, None]).reshape(1, parts * n_choice * tr)
    picked = gather_rows(y_rows.reshape(parts * cap, dq), flat).reshape(parts, n_choice, tr, dq)
    tm = min(256, tr)
    row = lambda i: (i + first_row // tm, 0)
    fixed = lambda i: (0, 0)
    choice = [pl.BlockSpec((parts, 1, tm, dq), functools.partial(lambda i, k: (0, k, i, 0), k=k))
              for k in range(n_choice)]
    return pl.pallas_call(
        functools.partial(_combine_kernel, n_prev=len(prev_outs)),
        grid=(tr // tm,),
        in_specs=[pl.BlockSpec((tm, n_choice), lambda i: (i, 0)), pl.BlockSpec((tm, d), row), pl.BlockSpec((tm, d), row),
                  pl.BlockSpec((1, d), fixed), pl.BlockSpec((1, d), fixed), pl.BlockSpec((1, d), fixed)]
        + choice + [pl.BlockSpec(memory_space=pl.ANY)] * len(prev_outs),
        out_specs=[pl.BlockSpec((tm, d), row)] * n_out,
        out_shape=[jax.ShapeDtypeStruct((t, d), F32)] * n_out,
        input_output_aliases={6 + n_choice + j: j for j in range(len(prev_outs))},
        compiler_params=_params("parallel"),
    )(gate_tk, h, shared, gate_ffn, next_shift, next_scale, *([picked] * n_choice), *prev_outs)


def moe_tail(h, f_rows, logits_t, shared, router_bias, layer, w_gate, w_up, w_down, gate_ffn,
             next_shift, next_scale, with_next):
    t, d = h.shape
    n_split, parts, tr, dq = f_rows.shape
    n_exp = w_gate.shape[1]
    n_blocks = -(-(tr * TOP_K + n_exp * (EXPERT_ROWS - 1)) // EXPERT_ROWS)
    cap = n_blocks * EXPERT_ROWS
    outs = []
    for s in range(n_split):
        idx, gate, rank, counts = route(logits_t[s], router_bias)
        dest, blk_e, n_valid = dispatch_plan(idx, rank, counts, n_blocks)
        dest_all = jnp.concatenate([dest + i * cap for i in range(parts)], axis=1)
        rows = dispatch(f_rows[s].reshape(parts * tr, dq), dest_all, parts * cap).reshape(parts, cap, dq)
        y_rows = experts(rows, blk_e, n_valid, layer, w_gate, w_up, w_down)
        outs = combine(y_rows, dest, gate.T, h, shared, gate_ffn, next_shift, next_scale, with_next, s * tr, outs)
    return outs


def kernel(x, c, ctx, c_ctx, w_ada, b_ada, w_in, s5_lam_re, s5_lam_im, s5_log_dt, s5_b_re, s5_b_im,
           s5_c_re, s5_c_im, s5_d, s5_w_glu, s5_b_glu, na_q_gain, na_k_gain, na_rpb, w_mix_out,
           w_fourier_out, b_fourier_out, w_router, router_bias, w_exp_gate, w_exp_up, w_exp_down,
           w_sh_gate, w_sh_up, w_sh_down):
    bsz, t, d = x.shape
    assert bsz == 1 and w_ada.shape[0] == 2
    n_exp = w_router.shape[2]
    s5w = s5_w_glu.shape[1]
    naw = w_in.shape[2] - s5w
    naw //= 3
    heads = naw // NA_HEAD_DIM

    cond8 = jnp.concatenate([c[:1].astype(F32), c_ctx.astype(F32)[None], jnp.zeros((6, d), F32)], axis=0)
    ada = adaln_all(cond8, w_ada, b_ada)
    mod = lambda layer, who, j: ada[layer, who:who + 1, j * d:(j + 1) * d]

    def ffn_weights(i):
        return (mod(i, 0, 3), mod(i, 0, 4), jnp.transpose(w_router[i]).astype(F32),
                w_sh_gate[i].astype(BF16), w_sh_up[i].astype(BF16), w_sh_down[i].astype(BF16))

    h0 = x[0]
    seg = jnp.asarray(np.kron(np.eye(heads), np.ones((NA_HEAD_DIM, NA_HEAD_DIM))), BF16)
    w_in_b = w_in[0].astype(BF16)
    qg = jnp.tile(na_q_gain[0].astype(F32), heads)[None]
    kg = jnp.tile(na_k_gain[0].astype(F32), heads)[None]
    u_c, _, k_c, v_c = in_projection(ctx[0], mod(0, 1, 0), mod(0, 1, 1), w_in_b, seg, qg, kg, s5w, naw)
    u_l, q_l, k_l, v_l = in_projection(h0, mod(0, 0, 0), mod(0, 0, 1), w_in_b, seg, qg, kg, s5w, naw)
    mats = s5_matrices(s5_lam_re[0], s5_lam_im[0], s5_log_dt[0], s5_b_re[0], s5_b_im[0],
                       s5_c_re[0], s5_c_im[0], s5_d[0])
    y_s5 = s5_mixer(u_c, u_l, mats)
    na = neighbourhood_attention(q_l, k_l, v_l, k_c, v_c, na_bias_table(na_rpb[0], t // GRID_W))
    h1, f1, lg1, sh1 = _post_call(
        _even_post_kernel, [y_s5, na, h0],
        [s5_w_glu[0].astype(BF16), s5_b_glu[0].astype(F32)[None], w_mix_out[0].astype(BF16), mod(0, 0, 2),
         *ffn_weights(0)], t, d, n_exp)
    h2, a1 = moe_tail(h1, f1, lg1, sh1, router_bias[0], 0, w_exp_gate, w_exp_up, w_exp_down,
                      mod(0, 0, 5), mod(1, 0, 0), mod(1, 0, 1), True)

    zre, zim = time_dft(a1)
    cc, sc = channel_dft_tables(d // FOURIER_GROUPS)
    h3, f3, lg3, sh3 = _post_call(
        _odd_post_kernel, [zre, zim, h2],
        [cc, sc, w_fourier_out[0].astype(BF16), b_fourier_out[0].astype(F32)[None], mod(1, 0, 2),
         *ffn_weights(1)], t, d, n_exp)
    zero_row = jnp.zeros((1, d), F32)
    (out,) = moe_tail(h3, f3, lg3, sh3, router_bias[1], 1, w_exp_gate, w_exp_up, w_exp_down,
                      mod(1, 0, 5), zero_row, zero_row, False)
    return out[None]
```

```python
import functools
import math

import numpy as np
import jax
import jax.numpy as jnp
from jax import lax
from jax.experimental import pallas as pl
from jax.experimental.pallas import tpu as pltpu
from jax.experimental.pallas import tpu_sc as plsc

F32 = jnp.float32
BF16 = jnp.bfloat16
I32 = jnp.int32
U32 = jnp.uint32
HIGHEST = lax.Precision.HIGHEST

MXU_DEPTH = 256
LANES = 128
GRID_W = 64
NORM_EPS = 1e-6
S5_GROUP = 16
S5_STATE = 64
S5_LAMBDA_RE_MAX = -1e-4
S5_CHUNK = 16
S5_TILE = 128
NA_HEADS = 8
NA_HEAD_DIM = 64
NA_KH = 8
NA_KW = 16
NA_STEP_ROWS = 4
FOURIER_GROUPS = 4
N_EXPERT_GROUPS = 8
TOPK_GROUPS = 4
TOP_K = 8
ROUTED_SCALE = 2.5
EXPERT_ROWS = 1024
ROW_PARTS = 2
SC_WINDOW = 128
SC_GATHER_SPLIT = 4
NEG_BIG = -1e30

VMEM_LIMIT_BYTES = 56 * 1024 * 1024


def _params(*sem):
    return pltpu.CompilerParams(dimension_semantics=sem or None,
                                vmem_limit_bytes=VMEM_LIMIT_BYTES)


def _rms(x):
    return x * lax.rsqrt(jnp.mean(x * x, axis=-1, keepdims=True) + NORM_EPS)


def _silu(x):
    return x * jax.nn.sigmoid(x)


def _pack_pair(lo, hi):
    lo = lax.bitcast_convert_type(lo.astype(BF16).astype(F32), U32)
    hi = lax.bitcast_convert_type(hi.astype(BF16).astype(F32), U32)
    return (hi & jnp.uint32(0xFFFF0000)) | (lo >> 16)


def _unpack_pair(w):
    return (lax.bitcast_convert_type(w << 16, F32), lax.bitcast_convert_type(w & jnp.uint32(0xFFFF0000), F32))


def _pack_rows(x):
    n = x.shape[1] // 2
    return _pack_pair(x[:, :n], x[:, n:])


def _unpack_rows(w):
    return jnp.concatenate(_unpack_pair(w), axis=1)


def _ada_kernel(c_ref, w_ref, b_ref, o_ref):
    o_ref[0] = jnp.dot(_silu(c_ref[...]), w_ref[0], preferred_element_type=F32,
                       precision=HIGHEST) + b_ref[0]


def adaln_all(cond8, w_ada, b_ada):
    n_layers, d, n6 = w_ada.shape
    tn = n6 // 4
    return pl.pallas_call(
        _ada_kernel,
        grid=(n_layers, n6 // tn),
        in_specs=[pl.BlockSpec((8, d), lambda l, j: (0, 0)),
                  pl.BlockSpec((1, d, tn), lambda l, j: (l, 0, j)),
                  pl.BlockSpec((1, 1, tn), lambda l, j: (l, 0, j))],
        out_specs=pl.BlockSpec((1, 8, tn), lambda l, j: (l, 0, j)),
        out_shape=jax.ShapeDtypeStruct((n_layers, 8, n6), F32),
        compiler_params=_params("parallel", "parallel"),
    )(cond8, w_ada, b_ada.reshape(n_layers, 1, n6))


def _inproj_kernel(x_ref, sh_ref, sc_ref, w_ref, seg_ref, qg_ref, kg_ref,
                   u_ref, q_ref, k_ref, v_ref):
    a = _rms(x_ref[...]) * (1.0 + sc_ref[...]) + sh_ref[...]
    z = jnp.dot(a.astype(BF16), w_ref[...], preferred_element_type=F32)
    s5w = u_ref.shape[1]
    naw = q_ref.shape[1]

    def head_norm(t, gain):
        ss = jnp.dot((t * t).astype(BF16), seg_ref[...], preferred_element_type=F32)
        return t * lax.rsqrt(ss * (1.0 / NA_HEAD_DIM) + NORM_EPS) * gain

    u_ref[...] = z[:, :s5w]
    q_ref[...] = head_norm(z[:, s5w:s5w + naw], qg_ref[...]).astype(BF16)
    k_ref[...] = head_norm(z[:, s5w + naw:s5w + 2 * naw], kg_ref[...]).astype(BF16)
    v_ref[...] = z[:, s5w + 2 * naw:].astype(BF16)


def in_projection(x, shift, scale, w_in_bf16, seg_ones, q_gain_row, k_gain_row, s5w, naw):
    t, d = x.shape
    tm = min(512, t)
    row = lambda i: (i, 0)
    fixed = lambda i: (0, 0)
    return pl.pallas_call(
        _inproj_kernel,
        grid=(t // tm,),
        in_specs=[pl.BlockSpec((tm, d), row),
                  pl.BlockSpec((1, d), fixed), pl.BlockSpec((1, d), fixed),
                  pl.BlockSpec(w_in_bf16.shape, fixed),
                  pl.BlockSpec(seg_ones.shape, fixed),
                  pl.BlockSpec((1, naw), fixed), pl.BlockSpec((1, naw), fixed)],
        out_specs=[pl.BlockSpec((tm, s5w), row), pl.BlockSpec((tm, naw), row),
                   pl.BlockSpec((tm, naw), row), pl.BlockSpec((tm, naw), row)],
        out_shape=[jax.ShapeDtypeStruct((t, s5w), F32)] + [jax.ShapeDtypeStruct((t, naw), BF16)] * 3,
        compiler_params=_params("parallel"),
    )(x, shift, scale, w_in_bf16, seg_ones, q_gain_row, k_gain_row)


def s5_matrices(lam_re, lam_im, log_dt, b_re, b_im, c_re, c_im, d_skip):
    L = S5_CHUNK
    c = S5_GROUP
    taus = jnp.arange(L + 1, dtype=F32)

    def direction(i):
        lam = lax.complex(jnp.minimum(lam_re[i].astype(F32), S5_LAMBDA_RE_MAX), lam_im[i].astype(F32))
        ldt = lam * jnp.exp(log_dt[i].astype(F32))[:, None]
        lam_bar = jnp.exp(ldt)
        b_bar = ((lam_bar - 1.0) / lam)[..., None] * lax.complex(b_re[i].astype(F32), b_im[i].astype(F32))
        cc = lax.complex(c_re[i].astype(F32), c_im[i].astype(F32))
        powers = jnp.exp(ldt[None] * taus[:, None, None])
        resp = jnp.real(jnp.einsum('gcp,tgp,gpd->gctd', cc, powers[:L], b_bar, precision=HIGHEST))
        return jnp.transpose(powers, (1, 2, 0)), b_bar, cc, resp

    pw_f, bb_f, cc_f, k_f = direction(0)
    pw_b, bb_b, cc_b, k_b = direction(1)
    g, p = pw_f.shape[:2]
    lag0 = k_f[:, :, :1] + k_b[:, :, :1] + (jnp.eye(c, dtype=F32)[None, :, None, :] * d_skip.astype(F32)[:, :, None, None])
    by_lag = jnp.concatenate([jnp.flip(k_f[:, :, 1:], axis=2), lag0, k_b[:, :, 1:]], axis=2).reshape(g, c, (2 * L - 1) * c)
    toep_t = jnp.stack([by_lag[:, :, (L - 1 - l) * c:(2 * L - 1 - l) * c] for l in range(L)], axis=1)
    toep_t = toep_t.reshape(g, L * c, L * c)

    def state_in(pw_by_s, b_bar):
        return (pw_by_s[:, :, :, None] * b_bar[:, :, None, :]).reshape(g, p, L * c)

    wf = state_in(jnp.flip(pw_f[:, :, :L], axis=2), bb_f)
    wb = state_in(pw_b[:, :, :L], bb_b)
    w_state_t = jnp.concatenate([jnp.real(wf), jnp.imag(wf), jnp.imag(wf), jnp.real(wf),
                                 jnp.real(wb), jnp.imag(wb), jnp.imag(wb), jnp.real(wb)], axis=1)

    def state_out(pw_by_l, cc):
        return (jnp.transpose(pw_by_l, (0, 2, 1))[:, :, None, :] * cc[:, None, :, :]).reshape(g, L * c, p)

    rf = state_out(pw_f[:, :, 1:], cc_f)
    rb = state_out(jnp.flip(pw_b[:, :, 1:], axis=2), cc_b)
    r_state_t = jnp.concatenate([jnp.real(rf), -jnp.imag(rf), jnp.real(rb), -jnp.imag(rb)], axis=-1)

    def mult(a):
        ar, ai = jnp.real(a), jnp.imag(a)
        return jnp.stack([jnp.concatenate([ar, ar], -1), jnp.concatenate([-ai, ai], -1),
                          jnp.concatenate([ai, -ai], -1)])

    return (toep_t.astype(BF16), w_state_t.astype(BF16), r_state_t.astype(BF16),
            mult(pw_f[:, :, L]), mult(pw_b[:, :, L]))


def _s5_pack_kernel(*refs):
    u_refs, (wt_ref, ut_ref, f1_ref, f2_ref, b1_ref, b2_ref) = refs[:-6], refs[-6:]
    L = S5_CHUNK
    g, lc, nck = ut_ref.shape
    c = lc // L
    gs = g // len(u_refs)
    for s in range(L):
        for j, u_ref in enumerate(u_refs):
            step_s = u_ref[pl.ds(s, nck, stride=L), :]
            ut_ref[j * gs:(j + 1) * gs, s * c:(s + 1) * c, :] = (
                jnp.transpose(step_s).astype(BF16).reshape(gs, c, nck))
    n = f1_ref.shape[2]
    for gi in range(g):
        inc = jnp.dot(wt_ref[gi], ut_ref[gi], preferred_element_type=F32)
        for j, ref in enumerate((f1_ref, f2_ref, b1_ref, b2_ref)):
            ref[:, gi, :] = jnp.transpose(inc[j * n:(j + 1) * n, :])


def s5_pack(u, wt_state):
    t, w = u.shape
    g, n4, lc = wt_state.shape
    nc = t // S5_CHUNK
    tile = min(S5_TILE, nc)
    inc = jax.ShapeDtypeStruct((nc, g, n4 // 4), F32)
    inc_blk = pl.BlockSpec((tile, g, n4 // 4), lambda i: (i, 0, 0))
    return pl.pallas_call(
        _s5_pack_kernel,
        grid=(nc // tile,),
        in_specs=[pl.BlockSpec((tile * S5_CHUNK, LANES), functools.partial(lambda i, j: (i, j), j=j))
                  for j in range(w // LANES)] + [pl.BlockSpec(wt_state.shape, lambda i: (0, 0, 0))],
        out_specs=[pl.BlockSpec((g, lc, tile), lambda i: (0, 0, i))] + [inc_blk] * 4,
        out_shape=[jax.ShapeDtypeStruct((g, lc, nc), BF16)] + [inc] * 4,
        compiler_params=_params("parallel"),
    )(*([u] * (w // LANES)), wt_state)


def _s5_scan_kernel(s1_ref, s2_ref, m_ref, init_ref, x_ref, last_ref, v1_ref, v2_ref, *, reverse):
    @pl.when(pl.program_id(0) == 0)
    def _():
        v1_ref[...] = init_ref[0]
        v2_ref[...] = init_ref[1]

    a1, a2, a3 = m_ref[0], m_ref[1], m_ref[2]
    cb = s1_ref.shape[0]

    def body(j, carry):
        v1, v2 = carry
        jj = cb - 1 - j if reverse else j
        x_ref[jj] = v1
        return (a1 * v1 + a2 * v2 + s1_ref[jj], a1 * v2 + a3 * v1 + s2_ref[jj])

    v1, v2 = lax.fori_loop(0, cb, body, (v1_ref[...], v2_ref[...]))
    v1_ref[...] = v1
    v2_ref[...] = v2
    last_ref[...] = v1


def s5_chunk_scan(s1, s2, mult, init, reverse):
    nc, g, n = s1.shape
    cb = min(S5_TILE, nc)
    nb = nc // cb
    blk = (lambda i: (nb - 1 - i, 0, 0)) if reverse else (lambda i: (i, 0, 0))
    return pl.pallas_call(
        functools.partial(_s5_scan_kernel, reverse=reverse),
        grid=(nb,),
        in_specs=[pl.BlockSpec((cb, g, n), blk), pl.BlockSpec((cb, g, n), blk),
                  pl.BlockSpec((3, g, n), lambda i: (0, 0, 0)), pl.BlockSpec((2, g, n), lambda i: (0, 0, 0))],
        out_specs=[pl.BlockSpec((cb, g, n), blk), pl.BlockSpec((g, n), lambda i: (0, 0))],
        out_shape=[jax.ShapeDtypeStruct((nc, g, n), F32), jax.ShapeDtypeStruct((g, n), F32)],
        scratch_shapes=[pltpu.VMEM((g, n), F32), pltpu.VMEM((g, n), F32)],
        compiler_params=_params("arbitrary"),
    )(s1, s2, mult, init)


def _s5_readout_kernel(ut_ref, tt_ref, rt_ref, xf_ref, xb_ref, y_ref, yt_ref, *slab_refs):
    L = S5_CHUNK
    g, lc, nck = ut_ref.shape
    c = lc // L
    gs = g // len(slab_refs)
    for gi in range(g):
        xin_t = jnp.concatenate([jnp.transpose(xf_ref[:, gi, :]), jnp.transpose(xb_ref[:, gi, :])], axis=0)
        yt_ref[gi] = (jnp.dot(tt_ref[gi], ut_ref[gi], preferred_element_type=F32)
                      + jnp.dot(rt_ref[gi], xin_t.astype(BF16), preferred_element_type=F32))
    for j, slab in enumerate(slab_refs):
        for l in range(L):
            step_l = yt_ref[j * gs:(j + 1) * gs, l * c:(l + 1) * c, :].reshape(gs * c, nck)
            slab[pl.ds(l, nck, stride=L), :] = jnp.transpose(step_l)
        y_ref[:, j * gs * c:(j + 1) * gs * c] = slab[...]


def s5_readout(ut, toep_t, r_state_t, xin_f, xin_b):
    g, lc, nc = ut.shape
    n = xin_f.shape[2]
    tile = min(S5_TILE, nc)
    fixed = lambda i: (0, 0, 0)
    state_blk = pl.BlockSpec((tile, g, n), lambda i: (i, 0, 0))
    return pl.pallas_call(
        _s5_readout_kernel,
        grid=(nc // tile,),
        in_specs=[pl.BlockSpec((g, lc, tile), lambda i: (0, 0, i)),
                  pl.BlockSpec(toep_t.shape, fixed), pl.BlockSpec(r_state_t.shape, fixed), state_blk, state_blk],
        out_specs=pl.BlockSpec((tile * S5_CHUNK, g * lc // S5_CHUNK), lambda i: (i, 0)),
        out_shape=jax.ShapeDtypeStruct((nc * S5_CHUNK, g * lc // S5_CHUNK), F32),
        scratch_shapes=[pltpu.VMEM((g, lc, tile), F32)]
        + [pltpu.VMEM((tile * S5_CHUNK, LANES), F32)] * (g * lc // S5_CHUNK // LANES),
        compiler_params=_params("parallel"),
    )(ut, toep_t, r_state_t, xin_f, xin_b)


def s5_mixer(u_ctx, u_lat, mats):
    toep_t, wt_state, r_state_t, mult_f, mult_b = mats
    L = S5_CHUNK
    g, n = mult_f.shape[1:]
    halves = lambda v: jnp.stack([v, jnp.roll(v, n // 2, axis=-1)])
    n_ctx = u_ctx.shape[0] // L
    ctx_chunks = -(-(n_ctx + 1) // S5_TILE) * S5_TILE
    ctx_pad = jnp.pad(u_ctx, ((0, ctx_chunks * L - u_ctx.shape[0]), (0, 0)))
    _, cf1, cf2, cb1, cb2 = s5_pack(ctx_pad, wt_state)
    zero = jnp.zeros((2, g, n), F32)
    ctx_f, _ = s5_chunk_scan(cf1, cf2, mult_f, zero, False)
    _, ctx_b_last = s5_chunk_scan(cb1, cb2, mult_b, zero, True)
    ut, f1, f2, b1, b2 = s5_pack(u_lat, wt_state)
    xin_f, _ = s5_chunk_scan(f1, f2, mult_f, halves(ctx_f[n_ctx]), False)
    xin_b, _ = s5_chunk_scan(b1, b2, mult_b, halves(ctx_b_last), True)
    return s5_readout(ut, toep_t, r_state_t, xin_f, xin_b)


def _na_window(rows):
    step = NA_STEP_ROWS
    assert rows % step == 0 and (NA_KH // 2) % step == 0 and rows >= NA_KH + step
    groups = rows // step
    n_blk = (NA_KH + step) // step
    first = np.clip(np.arange(groups) - NA_KH // 2 // step, 0, groups - n_blk)
    return groups, n_blk, first


def na_bias_table(rpb, rows):
    step = NA_STEP_ROWS
    groups, n_blk, first = _na_window(rows)
    win = n_blk * step
    q_col = np.arange(GRID_W)
    col_start = np.clip(q_col - NA_KW // 2, 0, GRID_W - NA_KW)
    key_col = np.arange(GRID_W)
    off = key_col[None, :] - col_start[:, None]
    valid_col = (off >= 0) & (off < NA_KW)
    rel_col = np.clip(key_col[None, :] - q_col[:, None] + NA_KW - 1, 0, 2 * NA_KW - 2)
    q_row = step * np.arange(groups)[:, None, None] + np.arange(step)[None, :, None]
    key_row = step * first[:, None, None] + np.arange(win)[None, None, :]
    row0 = np.clip(q_row - NA_KH // 2, 0, rows - NA_KH)
    rel_row = np.where((key_row >= row0) & (key_row < row0 + NA_KH), key_row - q_row + NA_KH - 1, -1)
    variant = np.arange(groups) - first
    reps = [int(np.argmax(variant == d)) for d in range(n_blk)]
    assert all((rel_row[g] == rel_row[reps[variant[g]]]).all() for g in range(groups))
    rel_row = rel_row[reps]
    pick_row = jnp.asarray(np.arange(2 * NA_KH - 1)[:, None, None, None] == rel_row[None], F32)
    pick_col = jnp.asarray(np.arange(2 * NA_KW - 1)[:, None, None] == rel_col[None], F32)
    table = jnp.einsum('hrx,rvji,xck->vhjcik', rpb.astype(F32), pick_row, pick_col, precision=HIGHEST)
    valid = (rel_row >= 0)[:, None, :, None, :, None] & valid_col[None, None, None, :, None, :]
    return jnp.where(valid, table, NEG_BIG).reshape(n_blk, rpb.shape[0], step * GRID_W, win * GRID_W)


def _na_kernel(*refs, scale, n_blk):
    q_ref = refs[0]
    k_refs = refs[1:1 + n_blk]
    v_refs = refs[1 + n_blk:1 + 2 * n_blk]
    kc_ref, vc_ref, b_ref, o_ref = refs[1 + 2 * n_blk:]
    hd = NA_HEAD_DIM
    width = q_ref.shape[1]
    span = min(MXU_DEPTH, width)
    nt = (((1,), (1,)), ((), ()))
    q = q_ref[...] * scale
    kk = jnp.concatenate([r[...] for r in k_refs], axis=0)
    vv = jnp.concatenate([r[...] for r in v_refs], axis=0)
    kc, vc = kc_ref[...], vc_ref[...]
    nq = q.shape[0]
    per = span // hd
    lane = lax.broadcasted_iota(I32, (nq, span), 1)
    own = [(lane >= j * hd) & (lane < (j + 1) * hd) for j in range(per)]
    out_cols = []
    for c0 in range(0, width, span):
        cols = slice(c0, c0 + span)
        qs = jnp.concatenate([jnp.where(own[j], q[:, cols], jnp.zeros_like(q[:, cols])) for j in range(per)], axis=0)
        h0 = c0 // hd
        bias = b_ref[0, h0:h0 + per].reshape(per * nq, kk.shape[0])
        s = lax.dot_general(qs, kk[:, cols], nt, preferred_element_type=F32) + bias
        sc = lax.dot_general(qs, kc[:, cols], nt, preferred_element_type=F32)
        m = jnp.maximum(jnp.max(s, axis=-1, keepdims=True), jnp.max(sc, axis=-1, keepdims=True))
        p = jnp.exp(s - m)
        pc = jnp.exp(sc - m)
        den = jnp.sum(p, axis=-1, keepdims=True) + jnp.sum(pc, axis=-1, keepdims=True)
        o = (jnp.dot(p.astype(BF16), vv[:, cols], preferred_element_type=F32)
             + jnp.dot(pc.astype(BF16), vc[:, cols], preferred_element_type=F32)) / den
        acc = jnp.zeros((nq, span), F32)
        for j in range(per):
            acc = jnp.where(own[j], o[j * nq:(j + 1) * nq], acc)
        out_cols.append(acc)
    o_ref[...] = jnp.concatenate(out_cols, axis=1).astype(o_ref.dtype)


def neighbourhood_attention(q, k, v, k_ctx, v_ctx, bias_table):
    t, w = q.shape
    groups, n_blk, _ = _na_window(t // GRID_W)
    first = lambda g: jnp.clip(g - NA_KH // 2 // NA_STEP_ROWS, 0, groups - n_blk)
    blk_tokens = NA_STEP_ROWS * GRID_W
    row_blk = pl.BlockSpec((blk_tokens, w), lambda g: (g, 0))
    key_blks = [pl.BlockSpec((blk_tokens, w), functools.partial(lambda g, i: (first(g) + i, 0), i=i))
                for i in range(n_blk)]
    ctx_blk = pl.BlockSpec(k_ctx.shape, lambda g: (0, 0))
    bias_blk = pl.BlockSpec((1,) + bias_table.shape[1:], lambda g: (g - first(g), 0, 0, 0))
    return pl.pallas_call(
        functools.partial(_na_kernel, scale=NA_HEAD_DIM ** -0.5, n_blk=n_blk),
        grid=(groups,),
        in_specs=[row_blk] + key_blks + key_blks + [ctx_blk, ctx_blk, bias_blk],
        out_specs=row_blk,
        out_shape=jax.ShapeDtypeStruct((t, w), BF16),
        compiler_params=_params("parallel"),
    )(q, *([k] * n_blk), *([v] * n_blk), k_ctx, v_ctx, bias_table)


def time_dft_tables(t):
    a_len = 1 << (int(math.log2(t)) // 2)
    b_len = t // a_len
    ka = np.arange(a_len)[:, None]
    tok = b_len * np.arange(a_len)[None, :]
    ang1 = -2.0 * np.pi * ((ka * (tok[None] + np.arange(b_len)[:, None, None])) % t) / t
    stage1 = np.concatenate([np.cos(ang1), np.sin(ang1)], axis=1) / math.sqrt(t)
    ang2 = 2.0 * np.pi * ((np.arange(b_len)[:, None] * np.arange(b_len)[None, :]) % b_len) / b_len
    c2, s2 = np.cos(ang2), np.sin(ang2)
    stage2 = np.block([[c2, s2], [-s2, c2]])
    return jnp.asarray(stage1, BF16), jnp.asarray(stage2, BF16), a_len, b_len


def _time_dft_kernel(x_ref, m_ref, w2_ref, zre_ref, zim_ref, y_ref, z_ref, *, a_len, b_len):
    i = pl.program_id(1)
    bb = m_ref.shape[0]

    def stage1(jb, carry):
        b = i * bb + jb
        xb = x_ref[pl.ds(b, a_len, stride=b_len), :]
        y = jnp.dot(m_ref[jb], xb.astype(BF16), preferred_element_type=F32)
        row = pl.multiple_of(b * a_len, a_len)
        y_ref[pl.ds(row, a_len), :] = _pack_pair(y[:a_len], y[a_len:])
        return carry

    lax.fori_loop(0, bb, stage1, 0, unroll=16)

    @pl.when(i == pl.num_programs(1) - 1)
    def _():
        def stage2(ka, carry):
            rows = pl.ds(ka, b_len, stride=a_len)
            yre, yim = _unpack_pair(y_ref[rows, :])
            y = jnp.concatenate([yre, yim], axis=0).astype(BF16)
            z = jnp.dot(w2_ref[...], y, preferred_element_type=F32)
            z_ref[rows, :] = _pack_pair(z[:b_len], z[b_len:])
            return carry

        lax.fori_loop(0, a_len, stage2, 0, unroll=16)
        zre, zim = _unpack_pair(z_ref[...])
        zre_ref[...] = zre.astype(zre_ref.dtype)
        zim_ref[...] = zim.astype(zim_ref.dtype)


def time_dft(x):
    t, d = x.shape
    stage1, stage2, a_len, b_len = time_dft_tables(t)
    lanes = 128
    bb = min(16, b_len)
    out = jax.ShapeDtypeStruct((t, d), BF16)
    return pl.pallas_call(
        functools.partial(_time_dft_kernel, a_len=a_len, b_len=b_len),
        grid=(d // lanes, b_len // bb),
        in_specs=[pl.BlockSpec((t, lanes), lambda j, i: (0, j)),
                  pl.BlockSpec((bb, 2 * a_len, a_len), lambda j, i: (i, 0, 0)),
                  pl.BlockSpec(stage2.shape, lambda j, i: (0, 0))],
        out_specs=[pl.BlockSpec((t, lanes), lambda j, i: (0, j))] * 2,
        out_shape=[out, out],
        scratch_shapes=[pltpu.VMEM((t, lanes), U32), pltpu.VMEM((t, lanes), U32)],
        compiler_params=_params("parallel", "arbitrary"),
    )(x, stage1, stage2)


def channel_dft_tables(c):
    ang = 2.0 * np.pi * ((np.arange(c)[:, None] * np.arange(c)[None, :]) % c) / c
    return (jnp.asarray(np.cos(ang) / math.sqrt(c), BF16), jnp.asarray(np.sin(ang) / math.sqrt(c), BF16))


def _ffn_prologue(h, shf_ref, scf_ref, wr_ref, wsg_ref, wsu_ref, wsd_ref, h_ref, f_ref, lg_ref, shared_ref):
    h_ref[...] = h
    f = _rms(h) * (1.0 + scf_ref[...]) + shf_ref[...]
    packed = _pack_rows(f)
    half = packed.shape[1] // 2
    f_ref[0] = packed[:, :half]
    f_ref[1] = packed[:, half:]
    fb = f.astype(BF16)
    f_lo = (f - fb.astype(F32)).astype(BF16)
    nt = (((1,), (1,)), ((), ()))
    wr = wr_ref[...]
    wr_hi = wr.astype(BF16)
    wr_lo = (wr - wr_hi.astype(F32)).astype(BF16)
    lg_ref[...] = (lax.dot_general(wr_hi, fb, nt, preferred_element_type=F32)
                   + lax.dot_general(wr_hi, f_lo, nt, preferred_element_type=F32)
                   + lax.dot_general(wr_lo, fb, nt, preferred_element_type=F32))
    hid = (_silu(jnp.dot(fb, wsg_ref[...], preferred_element_type=F32))
           * jnp.dot(fb, wsu_ref[...], preferred_element_type=F32))
    shared_ref[...] = jnp.dot(hid.astype(BF16), wsd_ref[...], preferred_element_type=F32).astype(shared_ref.dtype)


def _gelu_tanh(x):
    return 0.5 * x * (1.0 + jnp.tanh(math.sqrt(2.0 / math.pi) * (x + 0.044715 * (x * x * x))))


def _even_post_kernel(y_ref, na_ref, x_ref, wglu_ref, bglu_ref, wo_ref, gm_ref, *rest):
    g = _gelu_tanh(y_ref[...])
    gate = jax.nn.sigmoid(jnp.dot(g.astype(BF16), wglu_ref[...], preferred_element_type=F32) + bglu_ref[...])
    s5 = (g * gate).astype(BF16)
    w = s5.shape[1]
    mix = (jnp.dot(s5, wo_ref[:w, :], preferred_element_type=F32)
           + jnp.dot(na_ref[...], wo_ref[w:, :], preferred_element_type=F32))
    _ffn_prologue(x_ref[...] + gm_ref[...] * mix, *rest)


def _odd_post_kernel(zre_ref, zim_ref, h_ref_in, cc_ref, sc_ref, wf_ref, bf_ref, gm_ref, *rest):
    c = cc_ref.shape[0]
    parts = []
    for grp in range(zre_ref.shape[1] // c):
        cols = slice(grp * c, (grp + 1) * c)
        parts.append(jnp.dot(zre_ref[:, cols], cc_ref[...], preferred_element_type=F32)
                     + jnp.dot(zim_ref[:, cols], sc_ref[...], preferred_element_type=F32))
    fr = jnp.concatenate(parts, axis=-1).astype(BF16)
    mix = jnp.dot(fr, wf_ref[...], preferred_element_type=F32) + bf_ref[...]
    _ffn_prologue(h_ref_in[...] + gm_ref[...] * mix, *rest)


def _post_call(body, row_inputs, fixed_inputs, t, d, n_exp):
    tm = min(512, t)
    row = lambda i: (i, 0)
    in_specs = ([pl.BlockSpec((tm, a.shape[1]), row) for a in row_inputs]
                + [pl.BlockSpec(a.shape, functools.partial(lambda i, nd: (0,) * nd, nd=a.ndim))
                   for a in fixed_inputs])
    return pl.pallas_call(
        body,
        grid=(t // tm,),
        in_specs=in_specs,
        out_specs=[pl.BlockSpec((tm, d), row), pl.BlockSpec((ROW_PARTS, tm, d // 4), lambda i: (0, i, 0)),
                   pl.BlockSpec((n_exp, tm), lambda i: (0, i)), pl.BlockSpec((tm, d), row)],
        out_shape=[jax.ShapeDtypeStruct((t, d), F32), jax.ShapeDtypeStruct((ROW_PARTS, t, d // 4), U32),
                   jax.ShapeDtypeStruct((n_exp, t), F32), jax.ShapeDtypeStruct((t, d), BF16)],
        compiler_params=_params("parallel"),
    )(*row_inputs, *fixed_inputs)


def _route_kernel(lg_ref, bias_ref, tri_ref, idx_ref, gate_ref, rank_ref, cnt_ref, run_ref):
    @pl.when(pl.program_id(0) == 0)
    def _():
        run_ref[...] = jnp.zeros_like(run_ref)

    scores = jax.nn.sigmoid(lg_ref[...])
    n_exp, tb = scores.shape
    sel = scores + bias_ref[...]
    gsz = n_exp // N_EXPERT_GROUPS
    member = lax.broadcasted_iota(I32, (gsz, tb), 0)
    gscore = []
    for grp in range(N_EXPERT_GROUPS):
        xg = sel[grp * gsz:(grp + 1) * gsz, :]
        m1 = jnp.max(xg, axis=0, keepdims=True)
        first = jnp.min(jnp.where(xg == m1, member, gsz), axis=0, keepdims=True)
        m2 = jnp.max(jnp.where(member == first, -jnp.inf, xg), axis=0, keepdims=True)
        gscore.append(m1 + m2)
    keep_rows = []
    for grp in range(N_EXPERT_GROUPS):
        beaten = jnp.zeros((1, tb), F32)
        for other in range(N_EXPERT_GROUPS):
            if other == grp:
                continue
            wins = (gscore[other] >= gscore[grp]) if other < grp else (gscore[other] > gscore[grp])
            beaten = beaten + jnp.where(wins, 1.0, 0.0)
        keep_rows.append(jnp.broadcast_to(beaten < TOPK_GROUPS, (gsz, tb)))
    masked = jnp.where(jnp.concatenate(keep_rows, axis=0), sel, -jnp.inf)

    expert = lax.broadcasted_iota(I32, (n_exp, tb), 0)
    picks, gates, hots = [], [], []
    chosen = jnp.zeros((n_exp, tb), F32)
    for _ in range(TOP_K):
        m = jnp.max(masked, axis=0, keepdims=True)
        pick = jnp.min(jnp.where(masked == m, expert, n_exp), axis=0, keepdims=True)
        hot = expert == pick
        picks.append(pick)
        hots.append(hot)
        gates.append(jnp.sum(jnp.where(hot, scores, 0.0), axis=0, keepdims=True))
        chosen = jnp.where(hot, 1.0, chosen)
        masked = jnp.where(hot, -jnp.inf, masked)
    total = gates[0]
    for gk in gates[1:]:
        total = total + gk
    ahead = jnp.dot(chosen.astype(BF16), tri_ref[...], preferred_element_type=F32) + run_ref[...]
    for k in range(TOP_K):
        idx_ref[k:k + 1, :] = picks[k]
        gate_ref[k:k + 1, :] = ROUTED_SCALE * gates[k] / total
        rank_ref[k:k + 1, :] = jnp.sum(jnp.where(hots[k], ahead, 0.0), axis=0, keepdims=True).astype(I32)
    run_ref[...] = run_ref[...] + jnp.sum(chosen, axis=1, keepdims=True)
    cnt_ref[...] = jnp.broadcast_to(run_ref[...], cnt_ref.shape)


def route(logits_t, router_bias):
    n_exp, t = logits_t.shape
    tb = min(512, t)
    tri = jnp.asarray(np.triu(np.ones((tb, tb), np.float32), k=1), BF16)
    tok = lambda i: (0, i)
    idx, gate, rank, cnt = pl.pallas_call(
        _route_kernel,
        grid=(t // tb,),
        in_specs=[pl.BlockSpec((n_exp, tb), tok), pl.BlockSpec((n_exp, 1), lambda i: (0, 0)),
                  pl.BlockSpec((tb, tb), lambda i: (0, 0))],
        out_specs=[pl.BlockSpec((TOP_K, tb), tok)] * 3 + [pl.BlockSpec((n_exp, 128), lambda i: (0, 0))],
        out_shape=[jax.ShapeDtypeStruct((TOP_K, t), I32), jax.ShapeDtypeStruct((TOP_K, t), F32),
                   jax.ShapeDtypeStruct((TOP_K, t), I32), jax.ShapeDtypeStruct((n_exp, 128), F32)],
        scratch_shapes=[pltpu.VMEM((n_exp, 1), F32)],
        compiler_params=_params("arbitrary"),
    )(logits_t, router_bias.astype(F32).reshape(n_exp, 1), tri)
    return idx, gate, rank, cnt[:, 0].astype(I32)


def dispatch_plan(idx, rank, counts, n_blocks):
    n_exp = counts.shape[0]
    padded = (counts + EXPERT_ROWS - 1) // EXPERT_ROWS * EXPERT_ROWS
    pad_end = jnp.cumsum(padded)
    pad_start = pad_end - padded
    experts_iota = jnp.arange(n_exp, dtype=I32)
    dest = jnp.sum(jnp.where(idx[..., None] == experts_iota, pad_start, 0), axis=-1) + rank
    n_valid = (pad_end[-1] // EXPERT_ROWS).astype(I32)
    blk = jnp.minimum(jnp.arange(n_blocks, dtype=I32), n_valid - 1)
    blk_e = jnp.sum((pad_end[None, :] <= blk[:, None] * EXPERT_ROWS).astype(I32), axis=1)
    return dest.astype(I32), jnp.minimum(blk_e, n_exp - 1).astype(I32), n_valid.reshape(1)


def _sc_mesh():
    return plsc.VectorSubcoreMesh(core_axis_name="core", subcore_axis_name="subcore")


def dispatch(f, dest, cap):
    t, w = f.shape
    n_choice = dest.shape[0]

    @functools.partial(pl.kernel, out_type=jax.ShapeDtypeStruct((cap, w), f.dtype), mesh=_sc_mesh(),
                       scratch_types=[pltpu.SemaphoreType.DMA])
    def scatter_rows(x_hbm, i_hbm, o_hbm, sem):
        def body(x_vmem, i_vmem):
            copies = [pltpu.async_copy(x_vmem, o_hbm.at[i_vmem.at[k]], sem) for k in range(n_choice)]
            for cp in copies:
                cp.wait()

        pltpu.emit_pipeline(
            body,
            grid=(t // SC_WINDOW,),
            in_specs=[pl.BlockSpec((SC_WINDOW, w), lambda i: (i, 0)),
                      pl.BlockSpec((n_choice, SC_WINDOW), lambda i: (0, i))],
            out_specs=[],
            core_axis_name=("core", "subcore"),
            dimension_semantics=(pltpu.PARALLEL,),
        )(x_hbm, i_hbm)

    return scatter_rows(f, dest)


def gather_rows(rows, index_row):
    n = index_row.shape[1]
    w = rows.shape[1]

    piece = SC_WINDOW // SC_GATHER_SPLIT

    @functools.partial(pl.kernel, out_type=jax.ShapeDtypeStruct((n, w), rows.dtype), mesh=_sc_mesh(),
                       scratch_types=[pltpu.SemaphoreType.DMA])
    def gather(y_hbm, i_hbm, o_hbm, sem):
        def body(i_vmem, o_vmem):
            copies = [pltpu.async_copy(y_hbm.at[i_vmem.at[0, pl.ds(j * piece, piece)]],
                                       o_vmem.at[pl.ds(j * piece, piece)], sem)
                      for j in range(SC_GATHER_SPLIT)]
            for cp in copies:
                cp.wait()

        pltpu.emit_pipeline(
            body,
            grid=(n // SC_WINDOW,),
            in_specs=[pl.BlockSpec((1, SC_WINDOW), lambda i: (0, i))],
            out_specs=[pl.BlockSpec((SC_WINDOW, w), lambda i: (i, 0))],
            core_axis_name=("core", "subcore"),
            dimension_semantics=(pltpu.PARALLEL,),
        )(i_hbm, o_hbm)

    return gather(rows, index_row)


def _experts_kernel(be_ref, x_ref, wg_ref, wu_ref, wd_ref, y_ref, wg_bf, wu_bf, wd_bf):
    b = pl.program_id(0)

    @pl.when((b == 0) | (be_ref[b] != be_ref[jnp.maximum(b - 1, 0)]))
    def _():
        wg_bf[...] = wg_ref[0, 0].astype(BF16)
        wu_bf[...] = wu_ref[0, 0].astype(BF16)
        wd_bf[...] = wd_ref[0, 0].astype(BF16)

    x = _unpack_rows(jnp.concatenate([x_ref[i] for i in range(ROW_PARTS)], axis=1)).astype(BF16)
    hid = (_silu(jnp.dot(x, wg_bf[...], preferred_element_type=F32))
           * jnp.dot(x, wu_bf[...], preferred_element_type=F32))
    packed = _pack_rows(jnp.dot(hid.astype(BF16), wd_bf[...], preferred_element_type=F32))
    dq = packed.shape[1] // ROW_PARTS
    for i in range(ROW_PARTS):
        y_ref[i] = packed[:, i * dq:(i + 1) * dq]


def experts(rows, blk_e, n_valid, layer, w_gate, w_up, w_down):
    parts, cap, dq = rows.shape
    _, _, d, ff = w_gate.shape
    blk = pl.BlockSpec((parts, EXPERT_ROWS, dq), lambda b, be: (0, b, 0))
    return pl.pallas_call(
        _experts_kernel,
        grid_spec=pltpu.PrefetchScalarGridSpec(
            num_scalar_prefetch=1,
            grid=(n_valid[0],),
            in_specs=[blk,
                      pl.BlockSpec((1, 1, d, ff), lambda b, be: (layer, be[b], 0, 0)),
                      pl.BlockSpec((1, 1, d, ff), lambda b, be: (layer, be[b], 0, 0)),
                      pl.BlockSpec((1, 1, ff, d), lambda b, be: (layer, be[b], 0, 0))],
            out_specs=blk,
            scratch_shapes=[pltpu.VMEM((d, ff), BF16), pltpu.VMEM((d, ff), BF16), pltpu.VMEM((ff, d), BF16)],
        ),
        out_shape=jax.ShapeDtypeStruct(rows.shape, U32),
        compiler_params=_params("arbitrary"),
    )(blk_e, rows, w_gate, w_up, w_down)


def _combine_kernel(gate_ref, h_ref, shared_ref, gf_ref, sh_ref, sc_ref, *rest):
    y_refs, outs = rest[:TOP_K], rest[TOP_K:]
    routed = None
    for k in range(TOP_K):
        y = _unpack_rows(jnp.concatenate([y_refs[k][i, 0] for i in range(ROW_PARTS)], axis=1))
        routed = gate_ref[:, k:k + 1] * y if routed is None else routed + gate_ref[:, k:k + 1] * y
    h = h_ref[...] + gf_ref[...] * (routed + shared_ref[...].astype(F32))
    outs[0][...] = h
    if len(outs) > 1:
        outs[1][...] = _rms(h) * (1.0 + sc_ref[...]) + sh_ref[...]


def combine(y_rows, dest, gate_tk, h, shared, gate_ffn, next_shift, next_scale, with_next):
    t, d = h.shape
    parts, cap, dq = y_rows.shape
    n_out = 2 if with_next else 1
    n_choice = dest.shape[0]
    flat = (dest[None] + (jnp.arange(parts, dtype=I32) * cap)[:, None, None]).reshape(1, parts * n_choice * t)
    picked = gather_rows(y_rows.reshape(parts * cap, dq), flat).reshape(parts, n_choice, t, dq)
    tm = min(256, t)
    row = lambda i: (i, 0)
    fixed = lambda i: (0, 0)
    choice = [pl.BlockSpec((parts, 1, tm, dq), functools.partial(lambda i, k: (0, k, i, 0), k=k))
              for k in range(n_choice)]
    return pl.pallas_call(
        _combine_kernel,
        grid=(t // tm,),
        in_specs=[pl.BlockSpec((tm, n_choice), row), pl.BlockSpec((tm, d), row), pl.BlockSpec((tm, d), row),
                  pl.BlockSpec((1, d), fixed), pl.BlockSpec((1, d), fixed), pl.BlockSpec((1, d), fixed)] + choice,
        out_specs=[pl.BlockSpec((tm, d), row)] * n_out,
        out_shape=[jax.ShapeDtypeStruct((t, d), F32)] * n_out,
        compiler_params=_params("parallel"),
    )(gate_tk, h, shared, gate_ffn, next_shift, next_scale, *([picked] * n_choice))


def moe_tail(h, f_rows, logits_t, shared, router_bias, layer, w_gate, w_up, w_down, gate_ffn,
             next_shift, next_scale, with_next):
    t, d = h.shape
    parts, _, dq = f_rows.shape
    n_exp = w_gate.shape[1]
    n_blocks = -(-(t * TOP_K + n_exp * (EXPERT_ROWS - 1)) // EXPERT_ROWS)
    cap = n_blocks * EXPERT_ROWS
    idx, gate, rank, counts = route(logits_t, router_bias)
    dest, blk_e, n_valid = dispatch_plan(idx, rank, counts, n_blocks)
    dest_all = jnp.concatenate([dest + i * cap for i in range(parts)], axis=1)
    rows = dispatch(f_rows.reshape(parts * t, dq), dest_all, parts * cap).reshape(parts, cap, dq)
    y_rows = experts(rows, blk_e, n_valid, layer, w_gate, w_up, w_down)
    return combine(y_rows, dest, gate.T, h, shared, gate_ffn, next_shift, next_scale, with_next)


def kernel(x, c, ctx, c_ctx, w_ada, b_ada, w_in, s5_lam_re, s5_lam_im, s5_log_dt, s5_b_re, s5_b_im,
           s5_c_re, s5_c_im, s5_d, s5_w_glu, s5_b_glu, na_q_gain, na_k_gain, na_rpb, w_mix_out,
           w_fourier_out, b_fourier_out, w_router, router_bias, w_exp_gate, w_exp_up, w_exp_down,
           w_sh_gate, w_sh_up, w_sh_down):
    bsz, t, d = x.shape
    assert bsz == 1 and w_ada.shape[0] == 2
    n_exp = w_router.shape[2]
    s5w = s5_w_glu.shape[1]
    naw = w_in.shape[2] - s5w
    naw //= 3
    heads = naw // NA_HEAD_DIM

    cond8 = jnp.concatenate([c[:1].astype(F32), c_ctx.astype(F32)[None], jnp.zeros((6, d), F32)], axis=0)
    ada = adaln_all(cond8, w_ada, b_ada)
    mod = lambda layer, who, j: ada[layer, who:who + 1, j * d:(j + 1) * d]

    def ffn_weights(i):
        return (mod(i, 0, 3), mod(i, 0, 4), jnp.transpose(w_router[i]).astype(F32),
                w_sh_gate[i].astype(BF16), w_sh_up[i].astype(BF16), w_sh_down[i].astype(BF16))

    h0 = x[0]
    seg = jnp.asarray(np.kron(np.eye(heads), np.ones((NA_HEAD_DIM, NA_HEAD_DIM))), BF16)
    w_in_b = w_in[0].astype(BF16)
    qg = jnp.tile(na_q_gain[0].astype(F32), heads)[None]
    kg = jnp.tile(na_k_gain[0].astype(F32), heads)[None]
    u_c, _, k_c, v_c = in_projection(ctx[0], mod(0, 1, 0), mod(0, 1, 1), w_in_b, seg, qg, kg, s5w, naw)
    u_l, q_l, k_l, v_l = in_projection(h0, mod(0, 0, 0), mod(0, 0, 1), w_in_b, seg, qg, kg, s5w, naw)
    mats = s5_matrices(s5_lam_re[0], s5_lam_im[0], s5_log_dt[0], s5_b_re[0], s5_b_im[0],
                       s5_c_re[0], s5_c_im[0], s5_d[0])
    y_s5 = s5_mixer(u_c, u_l, mats)
    na = neighbourhood_attention(q_l, k_l, v_l, k_c, v_c, na_bias_table(na_rpb[0], t // GRID_W))
    h1, f1, lg1, sh1 = _post_call(
        _even_post_kernel, [y_s5, na, h0],
        [s5_w_glu[0].astype(BF16), s5_b_glu[0].astype(F32)[None], w_mix_out[0].astype(BF16), mod(0, 0, 2),
         *ffn_weights(0)], t, d, n_exp)
    h2, a1 = moe_tail(h1, f1, lg1, sh1, router_bias[0], 0, w_exp_gate, w_exp_up, w_exp_down,
                      mod(0, 0, 5), mod(1, 0, 0), mod(1, 0, 1), True)

    zre, zim = time_dft(a1)
    cc, sc = channel_dft_tables(d // FOURIER_GROUPS)
    h3, f3, lg3, sh3 = _post_call(
        _odd_post_kernel, [zre, zim, h2],
        [cc, sc, w_fourier_out[0].astype(BF16), b_fourier_out[0].astype(F32)[None], mod(1, 0, 2),
         *ffn_weights(1)], t, d, n_exp)
    zero_row = jnp.zeros((1, d), F32)
    (out,) = moe_tail(h3, f3, lg3, sh3, router_bias[1], 1, w_exp_gate, w_exp_up, w_exp_down,
                      mod(1, 0, 5), zero_row, zero_row, False)
    return out[None]
```

```python
import functools
import math

import numpy as np
import jax
import jax.numpy as jnp
from jax import lax
from jax.experimental import pallas as pl
from jax.experimental.pallas import tpu as pltpu
from jax.experimental.pallas import tpu_sc as plsc

F32 = jnp.float32
BF16 = jnp.bfloat16
I32 = jnp.int32
U32 = jnp.uint32
HIGHEST = lax.Precision.HIGHEST

MXU_DEPTH = 256
LANES = 128
GRID_W = 64
NORM_EPS = 1e-6
S5_GROUP = 16
S5_LAMBDA_RE_MAX = -1e-4
S5_CHUNK = 16
S5_TILE = 128
NA_HEAD_DIM = 64
NA_KH = 8
NA_KW = 16
NA_STEP_ROWS = 2
FOURIER_GROUPS = 4
N_EXPERT_GROUPS = 8
TOPK_GROUPS = 4
TOP_K = 8
ROUTED_SCALE = 2.5
EXPERT_ROWS = 1024
ROW_PARTS = 2
SC_WINDOW = 128
SC_GATHER_SPLIT = 4
NEG_BIG = -1e30

VMEM_LIMIT_BYTES = 56 * 1024 * 1024


def _params(*sem):
    return pltpu.CompilerParams(dimension_semantics=sem or None,
                                vmem_limit_bytes=VMEM_LIMIT_BYTES)


def _rms(x):
    return x * lax.rsqrt(jnp.mean(x * x, axis=-1, keepdims=True) + NORM_EPS)


def _silu(x):
    return x * jax.nn.sigmoid(x)


def _pack_pair(lo, hi):
    lo = lax.bitcast_convert_type(lo.astype(BF16).astype(F32), U32)
    hi = lax.bitcast_convert_type(hi.astype(BF16).astype(F32), U32)
    return (hi & jnp.uint32(0xFFFF0000)) | (lo >> 16)


def _unpack_pair(w):
    return (lax.bitcast_convert_type(w << 16, F32), lax.bitcast_convert_type(w & jnp.uint32(0xFFFF0000), F32))


def _pack_rows(x):
    n = x.shape[1] // 2
    return _pack_pair(x[:, :n], x[:, n:])


def _unpack_rows(w):
    return jnp.concatenate(_unpack_pair(w), axis=1)


def _ada_kernel(c_ref, w_ref, b_ref, o_ref):
    o_ref[0] = jnp.dot(_silu(c_ref[...]), w_ref[0], preferred_element_type=F32,
                       precision=HIGHEST) + b_ref[0]


def adaln_all(cond8, w_ada, b_ada):
    n_layers, d, n6 = w_ada.shape
    tn = n6 // 4
    return pl.pallas_call(
        _ada_kernel,
        grid=(n_layers, n6 // tn),
        in_specs=[pl.BlockSpec((8, d), lambda l, j: (0, 0)),
                  pl.BlockSpec((1, d, tn), lambda l, j: (l, 0, j)),
                  pl.BlockSpec((1, 1, tn), lambda l, j: (l, 0, j))],
        out_specs=pl.BlockSpec((1, 8, tn), lambda l, j: (l, 0, j)),
        out_shape=jax.ShapeDtypeStruct((n_layers, 8, n6), F32),
        compiler_params=_params("parallel", "parallel"),
    )(cond8, w_ada, b_ada.reshape(n_layers, 1, n6))


def _inproj_kernel(x_ref, sh_ref, sc_ref, w_ref, seg_ref, qg_ref, kg_ref,
                   u_ref, q_ref, k_ref, v_ref):
    a = _rms(x_ref[...]) * (1.0 + sc_ref[...]) + sh_ref[...]
    z = jnp.dot(a.astype(BF16), w_ref[...], preferred_element_type=F32)
    s5w = u_ref.shape[1]
    naw = q_ref.shape[1]

    def head_norm(t, gain):
        ss = jnp.dot((t * t).astype(BF16), seg_ref[...], preferred_element_type=F32)
        return t * lax.rsqrt(ss * (1.0 / NA_HEAD_DIM) + NORM_EPS) * gain

    u_ref[...] = z[:, :s5w]
    q_ref[...] = head_norm(z[:, s5w:s5w + naw], qg_ref[...]).astype(BF16)
    k_ref[...] = head_norm(z[:, s5w + naw:s5w + 2 * naw], kg_ref[...]).astype(BF16)
    v_ref[...] = z[:, s5w + 2 * naw:].astype(BF16)


def in_projection(x, shift, scale, w_in_bf16, seg_ones, q_gain_row, k_gain_row, s5w, naw):
    t, d = x.shape
    tm = min(512, t)
    row = lambda i: (i, 0)
    fixed = lambda i: (0, 0)
    return pl.pallas_call(
        _inproj_kernel,
        grid=(t // tm,),
        in_specs=[pl.BlockSpec((tm, d), row),
                  pl.BlockSpec((1, d), fixed), pl.BlockSpec((1, d), fixed),
                  pl.BlockSpec(w_in_bf16.shape, fixed),
                  pl.BlockSpec(seg_ones.shape, fixed),
                  pl.BlockSpec((1, naw), fixed), pl.BlockSpec((1, naw), fixed)],
        out_specs=[pl.BlockSpec((tm, s5w), row), pl.BlockSpec((tm, naw), row),
                   pl.BlockSpec((tm, naw), row), pl.BlockSpec((tm, naw), row)],
        out_shape=[jax.ShapeDtypeStruct((t, s5w), F32)] + [jax.ShapeDtypeStruct((t, naw), BF16)] * 3,
        compiler_params=_params("parallel"),
    )(x, shift, scale, w_in_bf16, seg_ones, q_gain_row, k_gain_row)


def s5_matrices(lam_re, lam_im, log_dt, b_re, b_im, c_re, c_im, d_skip):
    L = S5_CHUNK
    c = S5_GROUP
    taus = jnp.arange(L + 1, dtype=F32)

    def direction(i):
        lam = lax.complex(jnp.minimum(lam_re[i].astype(F32), S5_LAMBDA_RE_MAX), lam_im[i].astype(F32))
        ldt = lam * jnp.exp(log_dt[i].astype(F32))[:, None]
        lam_bar = jnp.exp(ldt)
        b_bar = ((lam_bar - 1.0) / lam)[..., None] * lax.complex(b_re[i].astype(F32), b_im[i].astype(F32))
        cc = lax.complex(c_re[i].astype(F32), c_im[i].astype(F32))
        powers = jnp.exp(ldt[None] * taus[:, None, None])
        resp = jnp.real(jnp.einsum('gcp,tgp,gpd->gctd', cc, powers[:L], b_bar, precision=HIGHEST))
        return jnp.transpose(powers, (1, 2, 0)), b_bar, cc, resp

    pw_f, bb_f, cc_f, k_f = direction(0)
    pw_b, bb_b, cc_b, k_b = direction(1)
    g, p = pw_f.shape[:2]
    lag0 = k_f[:, :, :1] + k_b[:, :, :1] + (jnp.eye(c, dtype=F32)[None, :, None, :] * d_skip.astype(F32)[:, :, None, None])
    by_lag = jnp.concatenate([jnp.flip(k_f[:, :, 1:], axis=2), lag0, k_b[:, :, 1:]], axis=2).reshape(g, c, (2 * L - 1) * c)
    toep_t = jnp.stack([by_lag[:, :, (L - 1 - l) * c:(2 * L - 1 - l) * c] for l in range(L)], axis=1)
    toep_t = toep_t.reshape(g, L * c, L * c)

    def state_in(pw_by_s, b_bar):
        return (pw_by_s[:, :, :, None] * b_bar[:, :, None, :]).reshape(g, p, L * c)

    wf = state_in(jnp.flip(pw_f[:, :, :L], axis=2), bb_f)
    wb = state_in(pw_b[:, :, :L], bb_b)
    w_state_t = jnp.concatenate([jnp.real(wf), jnp.imag(wf), jnp.imag(wf), jnp.real(wf),
                                 jnp.real(wb), jnp.imag(wb), jnp.imag(wb), jnp.real(wb)], axis=1)

    def state_out(pw_by_l, cc):
        return (jnp.transpose(pw_by_l, (0, 2, 1))[:, :, None, :] * cc[:, None, :, :]).reshape(g, L * c, p)

    rf = state_out(pw_f[:, :, 1:], cc_f)
    rb = state_out(jnp.flip(pw_b[:, :, 1:], axis=2), cc_b)
    r_state_t = jnp.concatenate([jnp.real(rf), -jnp.imag(rf), jnp.real(rb), -jnp.imag(rb)], axis=-1)

    def mult(a):
        ar, ai = jnp.real(a), jnp.imag(a)
        return jnp.stack([jnp.concatenate([ar, ar], -1), jnp.concatenate([-ai, ai], -1),
                          jnp.concatenate([ai, -ai], -1)])

    return (toep_t.astype(BF16), w_state_t.astype(BF16), r_state_t.astype(BF16),
            mult(pw_f[:, :, L]), mult(pw_b[:, :, L]))


def _s5_pack_kernel(*refs):
    u_refs, (wt_ref, ut_ref, f1_ref, f2_ref, b1_ref, b2_ref) = refs[:-6], refs[-6:]
    L = S5_CHUNK
    g, lc, nck = ut_ref.shape
    c = lc // L
    gs = g // len(u_refs)
    for s in range(L):
        for j, u_ref in enumerate(u_refs):
            step_s = u_ref[pl.ds(s, nck, stride=L), :]
            ut_ref[j * gs:(j + 1) * gs, s * c:(s + 1) * c, :] = (
                jnp.transpose(step_s).astype(BF16).reshape(gs, c, nck))
    n = f1_ref.shape[2]
    for gi in range(g):
        inc = jnp.dot(wt_ref[gi], ut_ref[gi], preferred_element_type=F32)
        for j, ref in enumerate((f1_ref, f2_ref, b1_ref, b2_ref)):
            ref[:, gi, :] = jnp.transpose(inc[j * n:(j + 1) * n, :])


def s5_pack(u, wt_state):
    t, w = u.shape
    g, n4, lc = wt_state.shape
    nc = t // S5_CHUNK
    tile = min(S5_TILE, nc)
    inc = jax.ShapeDtypeStruct((nc, g, n4 // 4), F32)
    inc_blk = pl.BlockSpec((tile, g, n4 // 4), lambda i: (i, 0, 0))
    return pl.pallas_call(
        _s5_pack_kernel,
        grid=(nc // tile,),
        in_specs=[pl.BlockSpec((tile * S5_CHUNK, LANES), functools.partial(lambda i, j: (i, j), j=j))
                  for j in range(w // LANES)] + [pl.BlockSpec(wt_state.shape, lambda i: (0, 0, 0))],
        out_specs=[pl.BlockSpec((g, lc, tile), lambda i: (0, 0, i))] + [inc_blk] * 4,
        out_shape=[jax.ShapeDtypeStruct((g, lc, nc), BF16)] + [inc] * 4,
        compiler_params=_params("parallel"),
    )(*([u] * (w // LANES)), wt_state)


def _s5_scan_kernel(s1_ref, s2_ref, m_ref, init_ref, x_ref, last_ref, v1_ref, v2_ref, *, reverse):
    @pl.when(pl.program_id(0) == 0)
    def _():
        v1_ref[...] = init_ref[0]
        v2_ref[...] = init_ref[1]

    a1, a2, a3 = m_ref[0], m_ref[1], m_ref[2]
    cb = s1_ref.shape[0]

    def body(j, carry):
        v1, v2 = carry
        jj = cb - 1 - j if reverse else j
        x_ref[jj] = v1
        return (a1 * v1 + a2 * v2 + s1_ref[jj], a1 * v2 + a3 * v1 + s2_ref[jj])

    v1, v2 = lax.fori_loop(0, cb, body, (v1_ref[...], v2_ref[...]))
    v1_ref[...] = v1
    v2_ref[...] = v2
    last_ref[...] = v1


def s5_chunk_scan(s1, s2, mult, init, reverse):
    nc, g, n = s1.shape
    cb = min(S5_TILE, nc)
    nb = nc // cb
    blk = (lambda i: (nb - 1 - i, 0, 0)) if reverse else (lambda i: (i, 0, 0))
    return pl.pallas_call(
        functools.partial(_s5_scan_kernel, reverse=reverse),
        grid=(nb,),
        in_specs=[pl.BlockSpec((cb, g, n), blk), pl.BlockSpec((cb, g, n), blk),
                  pl.BlockSpec((3, g, n), lambda i: (0, 0, 0)), pl.BlockSpec((2, g, n), lambda i: (0, 0, 0))],
        out_specs=[pl.BlockSpec((cb, g, n), blk), pl.BlockSpec((g, n), lambda i: (0, 0))],
        out_shape=[jax.ShapeDtypeStruct((nc, g, n), F32), jax.ShapeDtypeStruct((g, n), F32)],
        scratch_shapes=[pltpu.VMEM((g, n), F32), pltpu.VMEM((g, n), F32)],
        compiler_params=_params("arbitrary"),
    )(s1, s2, mult, init)


def _s5_readout_kernel(ut_ref, tt_ref, rt_ref, xf_ref, xb_ref, y_ref, yt_ref, *slab_refs):
    L = S5_CHUNK
    g, lc, nck = ut_ref.shape
    c = lc // L
    gs = g // len(slab_refs)
    for gi in range(g):
        xin_t = jnp.concatenate([jnp.transpose(xf_ref[:, gi, :]), jnp.transpose(xb_ref[:, gi, :])], axis=0)
        yt_ref[gi] = (jnp.dot(tt_ref[gi], ut_ref[gi], preferred_element_type=F32)
                      + jnp.dot(rt_ref[gi], xin_t.astype(BF16), preferred_element_type=F32))
    for j, slab in enumerate(slab_refs):
        for l in range(L):
            step_l = yt_ref[j * gs:(j + 1) * gs, l * c:(l + 1) * c, :].reshape(gs * c, nck)
            slab[pl.ds(l, nck, stride=L), :] = jnp.transpose(step_l)
        y_ref[:, j * gs * c:(j + 1) * gs * c] = slab[...]


def s5_readout(ut, toep_t, r_state_t, xin_f, xin_b):
    g, lc, nc = ut.shape
    n = xin_f.shape[2]
    tile = min(S5_TILE, nc)
    fixed = lambda i: (0, 0, 0)
    state_blk = pl.BlockSpec((tile, g, n), lambda i: (i, 0, 0))
    return pl.pallas_call(
        _s5_readout_kernel,
        grid=(nc // tile,),
        in_specs=[pl.BlockSpec((g, lc, tile), lambda i: (0, 0, i)),
                  pl.BlockSpec(toep_t.shape, fixed), pl.BlockSpec(r_state_t.shape, fixed), state_blk, state_blk],
        out_specs=pl.BlockSpec((tile * S5_CHUNK, g * lc // S5_CHUNK), lambda i: (i, 0)),
        out_shape=jax.ShapeDtypeStruct((nc * S5_CHUNK, g * lc // S5_CHUNK), F32),
        scratch_shapes=[pltpu.VMEM((g, lc, tile), F32)]
        + [pltpu.VMEM((tile * S5_CHUNK, LANES), F32)] * (g * lc // S5_CHUNK // LANES),
        compiler_params=_params("parallel"),
    )(ut, toep_t, r_state_t, xin_f, xin_b)


def s5_mixer(u_ctx, u_lat, mats):
    toep_t, wt_state, r_state_t, mult_f, mult_b = mats
    L = S5_CHUNK
    g, n = mult_f.shape[1:]
    halves = lambda v: jnp.stack([v, jnp.roll(v, n // 2, axis=-1)])
    n_ctx = u_ctx.shape[0] // L
    ctx_chunks = -(-(n_ctx + 1) // S5_TILE) * S5_TILE
    ctx_pad = jnp.pad(u_ctx, ((0, ctx_chunks * L - u_ctx.shape[0]), (0, 0)))
    _, cf1, cf2, cb1, cb2 = s5_pack(ctx_pad, wt_state)
    zero = jnp.zeros((2, g, n), F32)
    ctx_f, _ = s5_chunk_scan(cf1, cf2, mult_f, zero, False)
    _, ctx_b_last = s5_chunk_scan(cb1, cb2, mult_b, zero, True)
    ut, f1, f2, b1, b2 = s5_pack(u_lat, wt_state)
    xin_f, _ = s5_chunk_scan(f1, f2, mult_f, halves(ctx_f[n_ctx]), False)
    xin_b, _ = s5_chunk_scan(b1, b2, mult_b, halves(ctx_b_last), True)
    return s5_readout(ut, toep_t, r_state_t, xin_f, xin_b)


def _na_window(rows):
    step = NA_STEP_ROWS
    assert rows % step == 0 and (NA_KH // 2) % step == 0 and rows >= NA_KH + step
    groups = rows // step
    n_blk = (NA_KH + step) // step
    first = np.clip(np.arange(groups) - NA_KH // 2 // step, 0, groups - n_blk)
    return groups, n_blk, first


def na_bias_table(rpb, rows):
    step = NA_STEP_ROWS
    groups, n_blk, first = _na_window(rows)
    win = n_blk * step
    q_col = np.arange(GRID_W)
    col_start = np.clip(q_col - NA_KW // 2, 0, GRID_W - NA_KW)
    key_col = np.arange(GRID_W)
    off = key_col[None, :] - col_start[:, None]
    valid_col = (off >= 0) & (off < NA_KW)
    rel_col = np.clip(key_col[None, :] - q_col[:, None] + NA_KW - 1, 0, 2 * NA_KW - 2)
    q_row = step * np.arange(groups)[:, None, None] + np.arange(step)[None, :, None]
    key_row = step * first[:, None, None] + np.arange(win)[None, None, :]
    row0 = np.clip(q_row - NA_KH // 2, 0, rows - NA_KH)
    rel_row = np.where((key_row >= row0) & (key_row < row0 + NA_KH), key_row - q_row + NA_KH - 1, -1)
    variant = np.arange(groups) - first
    reps = [int(np.argmax(variant == d)) for d in range(n_blk)]
    assert all((rel_row[g] == rel_row[reps[variant[g]]]).all() for g in range(groups))
    rel_row = rel_row[reps]
    pick_row = jnp.asarray(np.arange(2 * NA_KH - 1)[:, None, None, None] == rel_row[None], F32)
    pick_col = jnp.asarray(np.arange(2 * NA_KW - 1)[:, None, None] == rel_col[None], F32)
    table = jnp.einsum('hrx,rvji,xck->vhjcik', rpb.astype(F32), pick_row, pick_col, precision=HIGHEST)
    valid = (rel_row >= 0)[:, None, :, None, :, None] & valid_col[None, None, None, :, None, :]
    return jnp.where(valid, table, NEG_BIG).reshape(n_blk, rpb.shape[0], step * GRID_W, win * GRID_W)


def _na_kernel(*refs, scale, n_blk):
    q_ref = refs[0]
    k_refs = refs[1:1 + n_blk]
    v_refs = refs[1 + n_blk:1 + 2 * n_blk]
    kc_ref, vc_ref, b_ref, o_ref = refs[1 + 2 * n_blk:]
    hd = NA_HEAD_DIM
    width = q_ref.shape[1]
    span = min(MXU_DEPTH, width)
    nt = (((1,), (1,)), ((), ()))
    q = q_ref[...] * scale
    kk = jnp.concatenate([r[...] for r in k_refs], axis=0)
    vv = jnp.concatenate([r[...] for r in v_refs], axis=0)
    kc, vc = kc_ref[...], vc_ref[...]
    nq = q.shape[0]
    per = span // hd
    lane = lax.broadcasted_iota(I32, (nq, span), 1)
    own = [(lane >= j * hd) & (lane < (j + 1) * hd) for j in range(per)]
    out_cols = []
    for c0 in range(0, width, span):
        cols = slice(c0, c0 + span)
        qs = jnp.concatenate([jnp.where(own[j], q[:, cols], jnp.zeros_like(q[:, cols])) for j in range(per)], axis=0)
        h0 = c0 // hd
        bias = b_ref[0, h0:h0 + per].reshape(per * nq, kk.shape[0])
        s = lax.dot_general(qs, kk[:, cols], nt, preferred_element_type=F32) + bias
        sc = lax.dot_general(qs, kc[:, cols], nt, preferred_element_type=F32)
        m = jnp.maximum(jnp.max(s, axis=-1, keepdims=True), jnp.max(sc, axis=-1, keepdims=True))
        p = jnp.exp(s - m)
        pc = jnp.exp(sc - m)
        den = jnp.sum(p, axis=-1, keepdims=True) + jnp.sum(pc, axis=-1, keepdims=True)
        o = (jnp.dot(p.astype(BF16), vv[:, cols], preferred_element_type=F32)
             + jnp.dot(pc.astype(BF16), vc[:, cols], preferred_element_type=F32)) / den
        acc = jnp.zeros((nq, span), F32)
        for j in range(per):
            acc = jnp.where(own[j], o[j * nq:(j + 1) * nq], acc)
        out_cols.append(acc)
    o_ref[...] = jnp.concatenate(out_cols, axis=1).astype(o_ref.dtype)


def neighbourhood_attention(q, k, v, k_ctx, v_ctx, bias_table):
    t, w = q.shape
    groups, n_blk, _ = _na_window(t // GRID_W)
    first = lambda g: jnp.clip(g - NA_KH // 2 // NA_STEP_ROWS, 0, groups - n_blk)
    blk_tokens = NA_STEP_ROWS * GRID_W
    row_blk = pl.BlockSpec((blk_tokens, w), lambda g: (g, 0))
    key_blks = [pl.BlockSpec((blk_tokens, w), functools.partial(lambda g, i: (first(g) + i, 0), i=i))
                for i in range(n_blk)]
    ctx_blk = pl.BlockSpec(k_ctx.shape, lambda g: (0, 0))
    bias_blk = pl.BlockSpec((1,) + bias_table.shape[1:], lambda g: (g - first(g), 0, 0, 0))
    return pl.pallas_call(
        functools.partial(_na_kernel, scale=NA_HEAD_DIM ** -0.5, n_blk=n_blk),
        grid=(groups,),
        in_specs=[row_blk] + key_blks + key_blks + [ctx_blk, ctx_blk, bias_blk],
        out_specs=row_blk,
        out_shape=jax.ShapeDtypeStruct((t, w), BF16),
        compiler_params=_params("parallel"),
    )(q, *([k] * n_blk), *([v] * n_blk), k_ctx, v_ctx, bias_table)


def time_dft_tables(t):
    a_len = 1 << (int(math.log2(t)) // 2)
    b_len = t // a_len
    ka = np.arange(a_len)[:, None]
    tok = b_len * np.arange(a_len)[None, :]
    ang1 = -2.0 * np.pi * ((ka * (tok[None] + np.arange(b_len)[:, None, None])) % t) / t
    stage1 = np.concatenate([np.cos(ang1), np.sin(ang1)], axis=1) / math.sqrt(t)
    ang2 = 2.0 * np.pi * ((np.arange(b_len)[:, None] * np.arange(b_len)[None, :]) % b_len) / b_len
    c2, s2 = np.cos(ang2), np.sin(ang2)
    stage2 = np.block([[c2, s2], [-s2, c2]])
    return jnp.asarray(stage1, BF16), jnp.asarray(stage2, BF16), a_len, b_len


def _time_dft_kernel(x_ref, m_ref, w2_ref, zre_ref, zim_ref, y_ref, z_ref, *, a_len, b_len):
    i = pl.program_id(1)
    bb = m_ref.shape[0]

    def stage1(jb, carry):
        b = i * bb + jb
        xb = x_ref[pl.ds(b, a_len, stride=b_len), :]
        y = jnp.dot(m_ref[jb], xb.astype(BF16), preferred_element_type=F32)
        row = pl.multiple_of(b * a_len, a_len)
        y_ref[pl.ds(row, a_len), :] = _pack_pair(y[:a_len], y[a_len:])
        return carry

    lax.fori_loop(0, bb, stage1, 0, unroll=16)

    @pl.when(i == pl.num_programs(1) - 1)
    def _():
        def stage2(ka, carry):
            rows = pl.ds(ka, b_len, stride=a_len)
            yre, yim = _unpack_pair(y_ref[rows, :])
            y = jnp.concatenate([yre, yim], axis=0).astype(BF16)
            z = jnp.dot(w2_ref[...], y, preferred_element_type=F32)
            z_ref[rows, :] = _pack_pair(z[:b_len], z[b_len:])
            return carry

        lax.fori_loop(0, a_len, stage2, 0, unroll=16)
        zre, zim = _unpack_pair(z_ref[...])
        zre_ref[...] = zre.astype(zre_ref.dtype)
        zim_ref[...] = zim.astype(zim_ref.dtype)


def time_dft(x):
    t, d = x.shape
    stage1, stage2, a_len, b_len = time_dft_tables(t)
    lanes = 128
    bb = min(16, b_len)
    out = jax.ShapeDtypeStruct((t, d), BF16)
    return pl.pallas_call(
        functools.partial(_time_dft_kernel, a_len=a_len, b_len=b_len),
        grid=(d // lanes, b_len // bb),
        in_specs=[pl.BlockSpec((t, lanes), lambda j, i: (0, j)),
                  pl.BlockSpec((bb, 2 * a_len, a_len), lambda j, i: (i, 0, 0)),
                  pl.BlockSpec(stage2.shape, lambda j, i: (0, 0))],
        out_specs=[pl.BlockSpec((t, lanes), lambda j, i: (0, j))] * 2,
        out_shape=[out, out],
        scratch_shapes=[pltpu.VMEM((t, lanes), U32), pltpu.VMEM((t, lanes), U32)],
        compiler_params=_params("parallel", "arbitrary"),
    )(x, stage1, stage2)


def channel_dft_tables(c):
    ang = 2.0 * np.pi * ((np.arange(c)[:, None] * np.arange(c)[None, :]) % c) / c
    return (jnp.asarray(np.cos(ang) / math.sqrt(c), BF16), jnp.asarray(np.sin(ang) / math.sqrt(c), BF16))


def _ffn_prologue(h, shf_ref, scf_ref, wr_ref, wsg_ref, wsu_ref, wsd_ref, h_ref, f_ref, lg_ref, shared_ref):
    h_ref[...] = h
    f = _rms(h) * (1.0 + scf_ref[...]) + shf_ref[...]
    packed = _pack_rows(f)
    dq = packed.shape[1] // ROW_PARTS
    for i in range(ROW_PARTS):
        f_ref[i] = packed[:, i * dq:(i + 1) * dq]
    fb = f.astype(BF16)
    f_lo = (f - fb.astype(F32)).astype(BF16)
    nt = (((1,), (1,)), ((), ()))
    wr = wr_ref[...]
    wr_hi = wr.astype(BF16)
    wr_lo = (wr - wr_hi.astype(F32)).astype(BF16)
    lg_ref[...] = (lax.dot_general(wr_hi, fb, nt, preferred_element_type=F32)
                   + lax.dot_general(wr_hi, f_lo, nt, preferred_element_type=F32)
                   + lax.dot_general(wr_lo, fb, nt, preferred_element_type=F32))
    hid = (_silu(jnp.dot(fb, wsg_ref[...], preferred_element_type=F32))
           * jnp.dot(fb, wsu_ref[...], preferred_element_type=F32))
    shared_ref[...] = jnp.dot(hid.astype(BF16), wsd_ref[...], preferred_element_type=F32).astype(shared_ref.dtype)


def _gelu_tanh(x):
    return 0.5 * x * (1.0 + jnp.tanh(math.sqrt(2.0 / math.pi) * (x + 0.044715 * (x * x * x))))


def _even_post_kernel(y_ref, na_ref, x_ref, wglu_ref, bglu_ref, wo_ref, gm_ref, *rest):
    g = _gelu_tanh(y_ref[...])
    gate = jax.nn.sigmoid(jnp.dot(g.astype(BF16), wglu_ref[...], preferred_element_type=F32) + bglu_ref[...])
    s5 = (g * gate).astype(BF16)
    w = s5.shape[1]
    mix = (jnp.dot(s5, wo_ref[:w, :], preferred_element_type=F32)
           + jnp.dot(na_ref[...], wo_ref[w:, :], preferred_element_type=F32))
    _ffn_prologue(x_ref[...] + gm_ref[...] * mix, *rest)


def _odd_post_kernel(zre_ref, zim_ref, h_ref_in, cc_ref, sc_ref, wf_ref, bf_ref, gm_ref, *rest):
    c = cc_ref.shape[0]
    parts = []
    for grp in range(zre_ref.shape[1] // c):
        cols = slice(grp * c, (grp + 1) * c)
        parts.append(jnp.dot(zre_ref[:, cols], cc_ref[...], preferred_element_type=F32)
                     + jnp.dot(zim_ref[:, cols], sc_ref[...], preferred_element_type=F32))
    fr = jnp.concatenate(parts, axis=-1).astype(BF16)
    mix = jnp.dot(fr, wf_ref[...], preferred_element_type=F32) + bf_ref[...]
    _ffn_prologue(h_ref_in[...] + gm_ref[...] * mix, *rest)


def _post_call(body, row_inputs, fixed_inputs, t, d, n_exp):
    tm = min(512, t)
    dq = d // 2 // ROW_PARTS
    row = lambda i: (i, 0)
    in_specs = ([pl.BlockSpec((tm, a.shape[1]), row) for a in row_inputs]
                + [pl.BlockSpec(a.shape, functools.partial(lambda i, nd: (0,) * nd, nd=a.ndim))
                   for a in fixed_inputs])
    return pl.pallas_call(
        body,
        grid=(t // tm,),
        in_specs=in_specs,
        out_specs=[pl.BlockSpec((tm, d), row), pl.BlockSpec((ROW_PARTS, tm, dq), lambda i: (0, i, 0)),
                   pl.BlockSpec((n_exp, tm), lambda i: (0, i)), pl.BlockSpec((tm, d), row)],
        out_shape=[jax.ShapeDtypeStruct((t, d), F32), jax.ShapeDtypeStruct((ROW_PARTS, t, dq), U32),
                   jax.ShapeDtypeStruct((n_exp, t), F32), jax.ShapeDtypeStruct((t, d), BF16)],
        compiler_params=_params("parallel"),
    )(*row_inputs, *fixed_inputs)


def _route_kernel(lg_ref, bias_ref, tri_ref, idx_ref, gate_ref, rank_ref, cnt_ref, run_ref):
    @pl.when(pl.program_id(0) == 0)
    def _():
        run_ref[...] = jnp.zeros_like(run_ref)

    scores = jax.nn.sigmoid(lg_ref[...])
    n_exp, tb = scores.shape
    sel = scores + bias_ref[...]
    gsz = n_exp // N_EXPERT_GROUPS
    member = lax.broadcasted_iota(I32, (gsz, tb), 0)
    gscore = []
    for grp in range(N_EXPERT_GROUPS):
        xg = sel[grp * gsz:(grp + 1) * gsz, :]
        m1 = jnp.max(xg, axis=0, keepdims=True)
        first = jnp.min(jnp.where(xg == m1, member, gsz), axis=0, keepdims=True)
        m2 = jnp.max(jnp.where(member == first, -jnp.inf, xg), axis=0, keepdims=True)
        gscore.append(m1 + m2)
    keep_rows = []
    for grp in range(N_EXPERT_GROUPS):
        beaten = jnp.zeros((1, tb), F32)
        for other in range(N_EXPERT_GROUPS):
            if other == grp:
                continue
            wins = (gscore[other] >= gscore[grp]) if other < grp else (gscore[other] > gscore[grp])
            beaten = beaten + jnp.where(wins, 1.0, 0.0)
        keep_rows.append(jnp.broadcast_to(beaten < TOPK_GROUPS, (gsz, tb)))
    masked = jnp.where(jnp.concatenate(keep_rows, axis=0), sel, -jnp.inf)

    expert = lax.broadcasted_iota(I32, (n_exp, tb), 0)
    picks, gates, hots = [], [], []
    chosen = jnp.zeros((n_exp, tb), F32)
    for _ in range(TOP_K):
        m = jnp.max(masked, axis=0, keepdims=True)
        pick = jnp.min(jnp.where(masked == m, expert, n_exp), axis=0, keepdims=True)
        hot = expert == pick
        picks.append(pick)
        hots.append(hot)
        gates.append(jnp.sum(jnp.where(hot, scores, 0.0), axis=0, keepdims=True))
        chosen = jnp.where(hot, 1.0, chosen)
        masked = jnp.where(hot, -jnp.inf, masked)
    total = gates[0]
    for gk in gates[1:]:
        total = total + gk
    ahead = jnp.dot(chosen.astype(BF16), tri_ref[...], preferred_element_type=F32) + run_ref[...]
    for k in range(TOP_K):
        idx_ref[k:k + 1, :] = picks[k]
        gate_ref[k:k + 1, :] = ROUTED_SCALE * gates[k] / total
        rank_ref[k:k + 1, :] = jnp.sum(jnp.where(hots[k], ahead, 0.0), axis=0, keepdims=True).astype(I32)
    run_ref[...] = run_ref[...] + jnp.sum(chosen, axis=1, keepdims=True)
    cnt_ref[...] = jnp.broadcast_to(run_ref[...], cnt_ref.shape)


def route(logits_t, router_bias):
    n_exp, t = logits_t.shape
    tb = min(512, t)
    tri = jnp.asarray(np.triu(np.ones((tb, tb), np.float32), k=1), BF16)
    tok = lambda i: (0, i)
    idx, gate, rank, cnt = pl.pallas_call(
        _route_kernel,
        grid=(t // tb,),
        in_specs=[pl.BlockSpec((n_exp, tb), tok), pl.BlockSpec((n_exp, 1), lambda i: (0, 0)),
                  pl.BlockSpec((tb, tb), lambda i: (0, 0))],
        out_specs=[pl.BlockSpec((TOP_K, tb), tok)] * 3 + [pl.BlockSpec((n_exp, 128), lambda i: (0, 0))],
        out_shape=[jax.ShapeDtypeStruct((TOP_K, t), I32), jax.ShapeDtypeStruct((TOP_K, t), F32),
                   jax.ShapeDtypeStruct((TOP_K, t), I32), jax.ShapeDtypeStruct((n_exp, 128), F32)],
        scratch_shapes=[pltpu.VMEM((n_exp, 1), F32)],
        compiler_params=_params("arbitrary"),
    )(logits_t, router_bias.astype(F32).reshape(n_exp, 1), tri)
    return idx, gate, rank, cnt[:, 0].astype(I32)


def dispatch_plan(idx, rank, counts, n_blocks):
    n_exp = counts.shape[0]
    shift = EXPERT_ROWS.bit_length() - 1
    assert EXPERT_ROWS == 1 << shift
    padded = ((counts + EXPERT_ROWS - 1) >> shift) << shift
    pad_end = jnp.cumsum(padded)
    pad_start = pad_end - padded
    experts_iota = jnp.arange(n_exp, dtype=I32)
    dest = jnp.sum(jnp.where(idx[..., None] == experts_iota, pad_start, 0), axis=-1) + rank
    n_valid = (pad_end[-1] >> shift).astype(I32)
    blk = jnp.minimum(jnp.arange(n_blocks, dtype=I32), n_valid - 1)
    blk_e = jnp.sum((pad_end[None, :] <= blk[:, None] * EXPERT_ROWS).astype(I32), axis=1)
    return dest.astype(I32), jnp.minimum(blk_e, n_exp - 1).astype(I32), n_valid.reshape(1)


def _sc_mesh():
    return plsc.VectorSubcoreMesh(core_axis_name="core", subcore_axis_name="subcore")


def dispatch(f, dest, cap):
    t, w = f.shape
    n_choice = dest.shape[0]

    @functools.partial(pl.kernel, out_type=jax.ShapeDtypeStruct((cap, w), f.dtype), mesh=_sc_mesh(),
                       scratch_types=[pltpu.SemaphoreType.DMA])
    def scatter_rows(x_hbm, i_hbm, o_hbm, sem):
        def body(x_vmem, i_vmem):
            copies = [pltpu.async_copy(x_vmem, o_hbm.at[i_vmem.at[k]], sem) for k in range(n_choice)]
            for cp in copies:
                cp.wait()

        pltpu.emit_pipeline(
            body,
            grid=(t // SC_WINDOW,),
            in_specs=[pl.BlockSpec((SC_WINDOW, w), lambda i: (i, 0)),
                      pl.BlockSpec((n_choice, SC_WINDOW), lambda i: (0, i))],
            out_specs=[],
            core_axis_name=("core", "subcore"),
            dimension_semantics=(pltpu.PARALLEL,),
        )(x_hbm, i_hbm)

    return scatter_rows(f, dest)


def gather_rows(rows, index_row):
    n = index_row.shape[1]
    w = rows.shape[1]

    piece = SC_WINDOW // SC_GATHER_SPLIT

    @functools.partial(pl.kernel, out_type=jax.ShapeDtypeStruct((n, w), rows.dtype), mesh=_sc_mesh(),
                       scratch_types=[pltpu.SemaphoreType.DMA])
    def gather(y_hbm, i_hbm, o_hbm, sem):
        def body(i_vmem, o_vmem):
            copies = [pltpu.async_copy(y_hbm.at[i_vmem.at[0, pl.ds(j * piece, piece)]],
                                       o_vmem.at[pl.ds(j * piece, piece)], sem)
                      for j in range(SC_GATHER_SPLIT)]
            for cp in copies:
                cp.wait()

        pltpu.emit_pipeline(
            body,
            grid=(n // SC_WINDOW,),
            in_specs=[pl.BlockSpec((1, SC_WINDOW), lambda i: (0, i))],
            out_specs=[pl.BlockSpec((SC_WINDOW, w), lambda i: (i, 0))],
            core_axis_name=("core", "subcore"),
            dimension_semantics=(pltpu.PARALLEL,),
        )(i_hbm, o_hbm)

    return gather(rows, index_row)


def _experts_kernel(be_ref, x_ref, wg_ref, wu_ref, wd_ref, y_ref, wg_bf, wu_bf, wd_bf):
    b = pl.program_id(0)

    @pl.when((b == 0) | (be_ref[b] != be_ref[jnp.maximum(b - 1, 0)]))
    def _():
        wg_bf[...] = wg_ref[0, 0].astype(BF16)
        wu_bf[...] = wu_ref[0, 0].astype(BF16)
        wd_bf[...] = wd_ref[0, 0].astype(BF16)

    x = _unpack_rows(jnp.concatenate([x_ref[i] for i in range(ROW_PARTS)], axis=1)).astype(BF16)
    hid = (_silu(jnp.dot(x, wg_bf[...], preferred_element_type=F32))
           * jnp.dot(x, wu_bf[...], preferred_element_type=F32))
    packed = _pack_rows(jnp.dot(hid.astype(BF16), wd_bf[...], preferred_element_type=F32))
    dq = packed.shape[1] // ROW_PARTS
    for i in range(ROW_PARTS):
        y_ref[i] = packed[:, i * dq:(i + 1) * dq]


def experts(rows, blk_e, n_valid, layer, w_gate, w_up, w_down):
    parts, cap, dq = rows.shape
    _, _, d, ff = w_gate.shape
    blk = pl.BlockSpec((parts, EXPERT_ROWS, dq), lambda b, be: (0, b, 0))
    return pl.pallas_call(
        _experts_kernel,
        grid_spec=pltpu.PrefetchScalarGridSpec(
            num_scalar_prefetch=1,
            grid=(n_valid[0],),
            in_specs=[blk,
                      pl.BlockSpec((1, 1, d, ff), lambda b, be: (layer, be[b], 0, 0)),
                      pl.BlockSpec((1, 1, d, ff), lambda b, be: (layer, be[b], 0, 0)),
                      pl.BlockSpec((1, 1, ff, d), lambda b, be: (layer, be[b], 0, 0))],
            out_specs=blk,
            scratch_shapes=[pltpu.VMEM((d, ff), BF16), pltpu.VMEM((d, ff), BF16), pltpu.VMEM((ff, d), BF16)],
        ),
        out_shape=jax.ShapeDtypeStruct(rows.shape, U32),
        compiler_params=_params("arbitrary"),
    )(blk_e, rows, w_gate, w_up, w_down)


def _combine_kernel(gate_ref, h_ref, shared_ref, gf_ref, sh_ref, sc_ref, *rest):
    y_refs, outs = rest[:TOP_K], rest[TOP_K:]
    routed = None
    for k in range(TOP_K):
        y = _unpack_rows(jnp.concatenate([y_refs[k][i, 0] for i in range(ROW_PARTS)], axis=1))
        routed = gate_ref[:, k:k + 1] * y if routed is None else routed + gate_ref[:, k:k + 1] * y
    h = h_ref[...] + gf_ref[...] * (routed + shared_ref[...].astype(F32))
    outs[0][...] = h
    if len(outs) > 1:
        outs[1][...] = _rms(h) * (1.0 + sc_ref[...]) + sh_ref[...]


def combine(y_rows, dest, gate_tk, h, shared, gate_ffn, next_shift, next_scale, with_next):
    t, d = h.shape
    parts, cap, dq = y_rows.shape
    n_out = 2 if with_next else 1
    n_choice = dest.shape[0]
    flat = (dest[None] + (jnp.arange(parts, dtype=I32) * cap)[:, None, None]).reshape(1, parts * n_choice * t)
    picked = gather_rows(y_rows.reshape(parts * cap, dq), flat).reshape(parts, n_choice, t, dq)
    tm = min(256, t)
    row = lambda i: (i, 0)
    fixed = lambda i: (0, 0)
    choice = [pl.BlockSpec((parts, 1, tm, dq), functools.partial(lambda i, k: (0, k, i, 0), k=k))
              for k in range(n_choice)]
    return pl.pallas_call(
        _combine_kernel,
        grid=(t // tm,),
        in_specs=[pl.BlockSpec((tm, n_choice), row), pl.BlockSpec((tm, d), row), pl.BlockSpec((tm, d), row),
                  pl.BlockSpec((1, d), fixed), pl.BlockSpec((1, d), fixed), pl.BlockSpec((1, d), fixed)] + choice,
        out_specs=[pl.BlockSpec((tm, d), row)] * n_out,
        out_shape=[jax.ShapeDtypeStruct((t, d), F32)] * n_out,
        compiler_params=_params("parallel"),
    )(gate_tk, h, shared, gate_ffn, next_shift, next_scale, *([picked] * n_choice))


def moe_tail(h, f_rows, logits_t, shared, router_bias, layer, w_gate, w_up, w_down, gate_ffn,
             next_shift, next_scale, with_next):
    t, d = h.shape
    parts, _, dq = f_rows.shape
    n_exp = w_gate.shape[1]
    n_blocks = -(-(t * TOP_K + n_exp * (EXPERT_ROWS - 1)) // EXPERT_ROWS)
    cap = n_blocks * EXPERT_ROWS
    idx, gate, rank, counts = route(logits_t, router_bias)
    dest, blk_e, n_valid = dispatch_plan(idx, rank, counts, n_blocks)
    dest_all = jnp.concatenate([dest + i * cap for i in range(parts)], axis=1)
    rows = dispatch(f_rows.reshape(parts * t, dq), dest_all, parts * cap).reshape(parts, cap, dq)
    y_rows = experts(rows, blk_e, n_valid, layer, w_gate, w_up, w_down)
    return combine(y_rows, dest, gate.T, h, shared, gate_ffn, next_shift, next_scale, with_next)


def kernel(x, c, ctx, c_ctx, w_ada, b_ada, w_in, s5_lam_re, s5_lam_im, s5_log_dt, s5_b_re, s5_b_im,
           s5_c_re, s5_c_im, s5_d, s5_w_glu, s5_b_glu, na_q_gain, na_k_gain, na_rpb, w_mix_out,
           w_fourier_out, b_fourier_out, w_router, router_bias, w_exp_gate, w_exp_up, w_exp_down,
           w_sh_gate, w_sh_up, w_sh_down):
    bsz, t, d = x.shape
    assert bsz == 1 and w_ada.shape[0] == 2
    n_exp = w_router.shape[2]
    s5w = s5_w_glu.shape[1]
    naw = w_in.shape[2] - s5w
    naw //= 3
    heads = naw // NA_HEAD_DIM

    cond8 = jnp.concatenate([c[:1].astype(F32), c_ctx.astype(F32)[None], jnp.zeros((6, d), F32)], axis=0)
    ada = adaln_all(cond8, w_ada, b_ada)
    mod = lambda layer, who, j: ada[layer, who:who + 1, j * d:(j + 1) * d]

    def ffn_weights(i):
        return (mod(i, 0, 3), mod(i, 0, 4), jnp.transpose(w_router[i]).astype(F32),
                w_sh_gate[i].astype(BF16), w_sh_up[i].astype(BF16), w_sh_down[i].astype(BF16))

    h0 = x[0]
    seg = jnp.asarray(np.kron(np.eye(heads), np.ones((NA_HEAD_DIM, NA_HEAD_DIM))), BF16)
    w_in_b = w_in[0].astype(BF16)
    qg = jnp.tile(na_q_gain[0].astype(F32), heads)[None]
    kg = jnp.tile(na_k_gain[0].astype(F32), heads)[None]
    u_c, _, k_c, v_c = in_projection(ctx[0], mod(0, 1, 0), mod(0, 1, 1), w_in_b, seg, qg, kg, s5w, naw)
    u_l, q_l, k_l, v_l = in_projection(h0, mod(0, 0, 0), mod(0, 0, 1), w_in_b, seg, qg, kg, s5w, naw)
    mats = s5_matrices(s5_lam_re[0], s5_lam_im[0], s5_log_dt[0], s5_b_re[0], s5_b_im[0],
                       s5_c_re[0], s5_c_im[0], s5_d[0])
    y_s5 = s5_mixer(u_c, u_l, mats)
    na = neighbourhood_attention(q_l, k_l, v_l, k_c, v_c, na_bias_table(na_rpb[0], t // GRID_W))
    h1, f1, lg1, sh1 = _post_call(
        _even_post_kernel, [y_s5, na, h0],
        [s5_w_glu[0].astype(BF16), s5_b_glu[0].astype(F32)[None], w_mix_out[0].astype(BF16), mod(0, 0, 2),
         *ffn_weights(0)], t, d, n_exp)
    h2, a1 = moe_tail(h1, f1, lg1, sh1, router_bias[0], 0, w_exp_gate, w_exp_up, w_exp_down,
                      mod(0, 0, 5), mod(1, 0, 0), mod(1, 0, 1), True)

    zre, zim = time_dft(a1)
    cc, sc = channel_dft_tables(d // FOURIER_GROUPS)
    h3, f3, lg3, sh3 = _post_call(
        _odd_post_kernel, [zre, zim, h2],
        [cc, sc, w_fourier_out[0].astype(BF16), b_fourier_out[0].astype(F32)[None], mod(1, 0, 2),
         *ffn_weights(1)], t, d, n_exp)
    zero_row = jnp.zeros((1, d), F32)
    (out,) = moe_tail(h3, f3, lg3, sh3, router_bias[1], 1, w_exp_gate, w_exp_up, w_exp_down,
                      mod(1, 0, 5), zero_row, zero_row, False)
    return out[None]
```

```python
import functools
import math

import numpy as np
import jax
import jax.numpy as jnp
from jax import lax
from jax.experimental import pallas as pl
from jax.experimental.pallas import tpu as pltpu
from jax.experimental.pallas import tpu_sc as plsc

F32 = jnp.float32
BF16 = jnp.bfloat16
I32 = jnp.int32
U32 = jnp.uint32
HIGHEST = lax.Precision.HIGHEST

MXU_DEPTH = 256
LANES = 128
GRID_W = 64
NORM_EPS = 1e-6
S5_GROUP = 16
S5_LAMBDA_RE_MAX = -1e-4
S5_CHUNK = 16
S5_TILE = 128
NA_HEAD_DIM = 64
NA_KH = 8
NA_KW = 16
NA_STEP_ROWS = 2
FOURIER_GROUPS = 4
N_EXPERT_GROUPS = 8
TOPK_GROUPS = 4
TOP_K = 8
ROUTED_SCALE = 2.5
EXPERT_ROWS = 1024
EXPERT_ROW_STEPS = 4
ROW_PARTS = 2
SC_WINDOW = 128
SC_GATHER_SPLIT = 4
NEG_BIG = -1e30

VMEM_LIMIT_BYTES = 56 * 1024 * 1024


def _params(*sem):
    return pltpu.CompilerParams(dimension_semantics=sem or None,
                                vmem_limit_bytes=VMEM_LIMIT_BYTES)


def _rms(x):
    return x * lax.rsqrt(jnp.mean(x * x, axis=-1, keepdims=True) + NORM_EPS)


def _silu(x):
    return x * jax.nn.sigmoid(x)


def _pack_pair(lo, hi):
    lo = lax.bitcast_convert_type(lo.astype(BF16).astype(F32), U32)
    hi = lax.bitcast_convert_type(hi.astype(BF16).astype(F32), U32)
    return (hi & jnp.uint32(0xFFFF0000)) | (lo >> 16)


def _unpack_pair(w):
    return (lax.bitcast_convert_type(w << 16, F32), lax.bitcast_convert_type(w & jnp.uint32(0xFFFF0000), F32))


def _pack_rows(x):
    n = x.shape[1] // 2
    return _pack_pair(x[:, :n], x[:, n:])


def _unpack_rows(w):
    return jnp.concatenate(_unpack_pair(w), axis=1)


def _ada_kernel(c_ref, w_ref, b_ref, o_ref):
    o_ref[0] = jnp.dot(_silu(c_ref[...]), w_ref[0], preferred_element_type=F32,
                       precision=HIGHEST) + b_ref[0]


def adaln_all(cond8, w_ada, b_ada):
    n_layers, d, n6 = w_ada.shape
    tn = n6 // 4
    return pl.pallas_call(
        _ada_kernel,
        grid=(n_layers, n6 // tn),
        in_specs=[pl.BlockSpec((8, d), lambda l, j: (0, 0)),
                  pl.BlockSpec((1, d, tn), lambda l, j: (l, 0, j)),
                  pl.BlockSpec((1, 1, tn), lambda l, j: (l, 0, j))],
        out_specs=pl.BlockSpec((1, 8, tn), lambda l, j: (l, 0, j)),
        out_shape=jax.ShapeDtypeStruct((n_layers, 8, n6), F32),
        compiler_params=_params("parallel", "parallel"),
    )(cond8, w_ada, b_ada.reshape(n_layers, 1, n6))


def _inproj_kernel(x_ref, sh_ref, sc_ref, w_ref, seg_ref, qg_ref, kg_ref,
                   u_ref, q_ref, k_ref, v_ref):
    a = _rms(x_ref[...]) * (1.0 + sc_ref[...]) + sh_ref[...]
    z = jnp.dot(a.astype(BF16), w_ref[...], preferred_element_type=F32)
    s5w = u_ref.shape[1]
    naw = q_ref.shape[1]

    def head_norm(t, gain):
        ss = jnp.dot((t * t).astype(BF16), seg_ref[...], preferred_element_type=F32)
        return t * lax.rsqrt(ss * (1.0 / NA_HEAD_DIM) + NORM_EPS) * gain

    u_ref[...] = z[:, :s5w]
    q_ref[...] = head_norm(z[:, s5w:s5w + naw], qg_ref[...]).astype(BF16)
    k_ref[...] = head_norm(z[:, s5w + naw:s5w + 2 * naw], kg_ref[...]).astype(BF16)
    v_ref[...] = z[:, s5w + 2 * naw:].astype(BF16)


def in_projection(x, shift, scale, w_in_bf16, seg_ones, q_gain_row, k_gain_row, s5w, naw):
    t, d = x.shape
    tm = min(512, t)
    row = lambda i: (i, 0)
    fixed = lambda i: (0, 0)
    return pl.pallas_call(
        _inproj_kernel,
        grid=(t // tm,),
        in_specs=[pl.BlockSpec((tm, d), row),
                  pl.BlockSpec((1, d), fixed), pl.BlockSpec((1, d), fixed),
                  pl.BlockSpec(w_in_bf16.shape, fixed),
                  pl.BlockSpec(seg_ones.shape, fixed),
                  pl.BlockSpec((1, naw), fixed), pl.BlockSpec((1, naw), fixed)],
        out_specs=[pl.BlockSpec((tm, s5w), row), pl.BlockSpec((tm, naw), row),
                   pl.BlockSpec((tm, naw), row), pl.BlockSpec((tm, naw), row)],
        out_shape=[jax.ShapeDtypeStruct((t, s5w), F32)] + [jax.ShapeDtypeStruct((t, naw), BF16)] * 3,
        compiler_params=_params("parallel"),
    )(x, shift, scale, w_in_bf16, seg_ones, q_gain_row, k_gain_row)


def s5_matrices(lam_re, lam_im, log_dt, b_re, b_im, c_re, c_im, d_skip):
    L = S5_CHUNK
    c = S5_GROUP
    taus = jnp.arange(L + 1, dtype=F32)

    def direction(i):
        lam = lax.complex(jnp.minimum(lam_re[i].astype(F32), S5_LAMBDA_RE_MAX), lam_im[i].astype(F32))
        ldt = lam * jnp.exp(log_dt[i].astype(F32))[:, None]
        lam_bar = jnp.exp(ldt)
        b_bar = ((lam_bar - 1.0) / lam)[..., None] * lax.complex(b_re[i].astype(F32), b_im[i].astype(F32))
        cc = lax.complex(c_re[i].astype(F32), c_im[i].astype(F32))
        powers = jnp.exp(ldt[None] * taus[:, None, None])
        resp = jnp.real(jnp.einsum('gcp,tgp,gpd->gctd', cc, powers[:L], b_bar, precision=HIGHEST))
        return jnp.transpose(powers, (1, 2, 0)), b_bar, cc, resp

    pw_f, bb_f, cc_f, k_f = direction(0)
    pw_b, bb_b, cc_b, k_b = direction(1)
    g, p = pw_f.shape[:2]
    lag0 = k_f[:, :, :1] + k_b[:, :, :1] + (jnp.eye(c, dtype=F32)[None, :, None, :] * d_skip.astype(F32)[:, :, None, None])
    by_lag = jnp.concatenate([jnp.flip(k_f[:, :, 1:], axis=2), lag0, k_b[:, :, 1:]], axis=2).reshape(g, c, (2 * L - 1) * c)
    toep_t = jnp.stack([by_lag[:, :, (L - 1 - l) * c:(2 * L - 1 - l) * c] for l in range(L)], axis=1)
    toep_t = toep_t.reshape(g, L * c, L * c)

    def state_in(pw_by_s, b_bar):
        return (pw_by_s[:, :, :, None] * b_bar[:, :, None, :]).reshape(g, p, L * c)

    wf = state_in(jnp.flip(pw_f[:, :, :L], axis=2), bb_f)
    wb = state_in(pw_b[:, :, :L], bb_b)
    w_state_t = jnp.concatenate([jnp.real(wf), jnp.imag(wf), jnp.imag(wf), jnp.real(wf),
                                 jnp.real(wb), jnp.imag(wb), jnp.imag(wb), jnp.real(wb)], axis=1)

    def state_out(pw_by_l, cc):
        return (jnp.transpose(pw_by_l, (0, 2, 1))[:, :, None, :] * cc[:, None, :, :]).reshape(g, L * c, p)

    rf = state_out(pw_f[:, :, 1:], cc_f)
    rb = state_out(jnp.flip(pw_b[:, :, 1:], axis=2), cc_b)
    r_state_t = jnp.concatenate([jnp.real(rf), -jnp.imag(rf), jnp.real(rb), -jnp.imag(rb)], axis=-1)

    def mult(a):
        ar, ai = jnp.real(a), jnp.imag(a)
        return jnp.stack([jnp.concatenate([ar, ar], -1), jnp.concatenate([-ai, ai], -1),
                          jnp.concatenate([ai, -ai], -1)])

    return (toep_t.astype(BF16), w_state_t.astype(BF16), r_state_t.astype(BF16),
            mult(pw_f[:, :, L]), mult(pw_b[:, :, L]))


def _s5_pack_kernel(*refs):
    u_refs, (wt_ref, ut_ref, f1_ref, f2_ref, b1_ref, b2_ref) = refs[:-6], refs[-6:]
    L = S5_CHUNK
    g, lc, nck = ut_ref.shape
    c = lc // L
    gs = g // len(u_refs)
    for s in range(L):
        for j, u_ref in enumerate(u_refs):
            step_s = u_ref[pl.ds(s, nck, stride=L), :]
            ut_ref[j * gs:(j + 1) * gs, s * c:(s + 1) * c, :] = (
                jnp.transpose(step_s).astype(BF16).reshape(gs, c, nck))
    n = f1_ref.shape[2]
    for gi in range(g):
        inc = jnp.dot(wt_ref[gi], ut_ref[gi], preferred_element_type=F32)
        for j, ref in enumerate((f1_ref, f2_ref, b1_ref, b2_ref)):
            ref[:, gi, :] = jnp.transpose(inc[j * n:(j + 1) * n, :])


def s5_pack(u, wt_state):
    t, w = u.shape
    g, n4, lc = wt_state.shape
    nc = t // S5_CHUNK
    tile = min(S5_TILE, nc)
    inc = jax.ShapeDtypeStruct((nc, g, n4 // 4), F32)
    inc_blk = pl.BlockSpec((tile, g, n4 // 4), lambda i: (i, 0, 0))
    return pl.pallas_call(
        _s5_pack_kernel,
        grid=(nc // tile,),
        in_specs=[pl.BlockSpec((tile * S5_CHUNK, LANES), functools.partial(lambda i, j: (i, j), j=j))
                  for j in range(w // LANES)] + [pl.BlockSpec(wt_state.shape, lambda i: (0, 0, 0))],
        out_specs=[pl.BlockSpec((g, lc, tile), lambda i: (0, 0, i))] + [inc_blk] * 4,
        out_shape=[jax.ShapeDtypeStruct((g, lc, nc), BF16)] + [inc] * 4,
        compiler_params=_params("parallel"),
    )(*([u] * (w // LANES)), wt_state)


def _s5_scan_kernel(s1_ref, s2_ref, m_ref, init_ref, x_ref, last_ref, v1_ref, v2_ref, *, reverse):
    @pl.when(pl.program_id(0) == 0)
    def _():
        v1_ref[...] = init_ref[0]
        v2_ref[...] = init_ref[1]

    a1, a2, a3 = m_ref[0], m_ref[1], m_ref[2]
    cb = s1_ref.shape[0]

    def body(j, carry):
        v1, v2 = carry
        jj = cb - 1 - j if reverse else j
        x_ref[jj] = v1
        return (a1 * v1 + a2 * v2 + s1_ref[jj], a1 * v2 + a3 * v1 + s2_ref[jj])

    v1, v2 = lax.fori_loop(0, cb, body, (v1_ref[...], v2_ref[...]))
    v1_ref[...] = v1
    v2_ref[...] = v2
    last_ref[...] = v1


def s5_chunk_scan(s1, s2, mult, init, reverse):
    nc, g, n = s1.shape
    cb = min(S5_TILE, nc)
    nb = nc // cb
    blk = (lambda i: (nb - 1 - i, 0, 0)) if reverse else (lambda i: (i, 0, 0))
    return pl.pallas_call(
        functools.partial(_s5_scan_kernel, reverse=reverse),
        grid=(nb,),
        in_specs=[pl.BlockSpec((cb, g, n), blk), pl.BlockSpec((cb, g, n), blk),
                  pl.BlockSpec((3, g, n), lambda i: (0, 0, 0)), pl.BlockSpec((2, g, n), lambda i: (0, 0, 0))],
        out_specs=[pl.BlockSpec((cb, g, n), blk), pl.BlockSpec((g, n), lambda i: (0, 0))],
        out_shape=[jax.ShapeDtypeStruct((nc, g, n), F32), jax.ShapeDtypeStruct((g, n), F32)],
        scratch_shapes=[pltpu.VMEM((g, n), F32), pltpu.VMEM((g, n), F32)],
        compiler_params=_params("arbitrary"),
    )(s1, s2, mult, init)


def _s5_readout_kernel(ut_ref, tt_ref, rt_ref, xf_ref, xb_ref, y_ref, yt_ref, *slab_refs):
    L = S5_CHUNK
    g, lc, nck = ut_ref.shape
    c = lc // L
    gs = g // len(slab_refs)
    for gi in range(g):
        xin_t = jnp.concatenate([jnp.transpose(xf_ref[:, gi, :]), jnp.transpose(xb_ref[:, gi, :])], axis=0)
        yt_ref[gi] = (jnp.dot(tt_ref[gi], ut_ref[gi], preferred_element_type=F32)
                      + jnp.dot(rt_ref[gi], xin_t.astype(BF16), preferred_element_type=F32))
    for j, slab in enumerate(slab_refs):
        for l in range(L):
            step_l = yt_ref[j * gs:(j + 1) * gs, l * c:(l + 1) * c, :].reshape(gs * c, nck)
            slab[pl.ds(l, nck, stride=L), :] = jnp.transpose(step_l)
        y_ref[:, j * gs * c:(j + 1) * gs * c] = slab[...]


def s5_readout(ut, toep_t, r_state_t, xin_f, xin_b):
    g, lc, nc = ut.shape
    n = xin_f.shape[2]
    tile = min(S5_TILE, nc)
    fixed = lambda i: (0, 0, 0)
    state_blk = pl.BlockSpec((tile, g, n), lambda i: (i, 0, 0))
    return pl.pallas_call(
        _s5_readout_kernel,
        grid=(nc // tile,),
        in_specs=[pl.BlockSpec((g, lc, tile), lambda i: (0, 0, i)),
                  pl.BlockSpec(toep_t.shape, fixed), pl.BlockSpec(r_state_t.shape, fixed), state_blk, state_blk],
        out_specs=pl.BlockSpec((tile * S5_CHUNK, g * lc // S5_CHUNK), lambda i: (i, 0)),
        out_shape=jax.ShapeDtypeStruct((nc * S5_CHUNK, g * lc // S5_CHUNK), F32),
        scratch_shapes=[pltpu.VMEM((g, lc, tile), F32)]
        + [pltpu.VMEM((tile * S5_CHUNK, LANES), F32)] * (g * lc // S5_CHUNK // LANES),
        compiler_params=_params("parallel"),
    )(ut, toep_t, r_state_t, xin_f, xin_b)


def s5_mixer(u_ctx, u_lat, mats):
    toep_t, wt_state, r_state_t, mult_f, mult_b = mats
    L = S5_CHUNK
    g, n = mult_f.shape[1:]
    halves = lambda v: jnp.stack([v, jnp.roll(v, n // 2, axis=-1)])
    n_ctx = u_ctx.shape[0] // L
    ctx_chunks = -(-(n_ctx + 1) // S5_TILE) * S5_TILE
    ctx_pad = jnp.pad(u_ctx, ((0, ctx_chunks * L - u_ctx.shape[0]), (0, 0)))
    _, cf1, cf2, cb1, cb2 = s5_pack(ctx_pad, wt_state)
    zero = jnp.zeros((2, g, n), F32)
    ctx_f, _ = s5_chunk_scan(cf1, cf2, mult_f, zero, False)
    _, ctx_b_last = s5_chunk_scan(cb1, cb2, mult_b, zero, True)
    ut, f1, f2, b1, b2 = s5_pack(u_lat, wt_state)
    xin_f, _ = s5_chunk_scan(f1, f2, mult_f, halves(ctx_f[n_ctx]), False)
    xin_b, _ = s5_chunk_scan(b1, b2, mult_b, halves(ctx_b_last), True)
    return s5_readout(ut, toep_t, r_state_t, xin_f, xin_b)


def _na_window(rows):
    step = NA_STEP_ROWS
    assert rows % step == 0 and (NA_KH // 2) % step == 0 and rows >= NA_KH + step
    groups = rows // step
    n_blk = (NA_KH + step) // step
    first = np.clip(np.arange(groups) - NA_KH // 2 // step, 0, groups - n_blk)
    return groups, n_blk, first


def na_bias_table(rpb, rows):
    step = NA_STEP_ROWS
    groups, n_blk, first = _na_window(rows)
    win = n_blk * step
    q_col = np.arange(GRID_W)
    col_start = np.clip(q_col - NA_KW // 2, 0, GRID_W - NA_KW)
    key_col = np.arange(GRID_W)
    off = key_col[None, :] - col_start[:, None]
    valid_col = (off >= 0) & (off < NA_KW)
    rel_col = np.clip(key_col[None, :] - q_col[:, None] + NA_KW - 1, 0, 2 * NA_KW - 2)
    q_row = step * np.arange(groups)[:, None, None] + np.arange(step)[None, :, None]
    key_row = step * first[:, None, None] + np.arange(win)[None, None, :]
    row0 = np.clip(q_row - NA_KH // 2, 0, rows - NA_KH)
    rel_row = np.where((key_row >= row0) & (key_row < row0 + NA_KH), key_row - q_row + NA_KH - 1, -1)
    variant = np.arange(groups) - first
    reps = [int(np.argmax(variant == d)) for d in range(n_blk)]
    assert all((rel_row[g] == rel_row[reps[variant[g]]]).all() for g in range(groups))
    rel_row = rel_row[reps]
    pick_row = jnp.asarray(np.arange(2 * NA_KH - 1)[:, None, None, None] == rel_row[None], F32)
    pick_col = jnp.asarray(np.arange(2 * NA_KW - 1)[:, None, None] == rel_col[None], F32)
    table = jnp.einsum('hrx,rvji,xck->vhjcik', rpb.astype(F32), pick_row, pick_col, precision=HIGHEST)
    valid = (rel_row >= 0)[:, None, :, None, :, None] & valid_col[None, None, None, :, None, :]
    return jnp.where(valid, table, NEG_BIG).reshape(n_blk, rpb.shape[0], step * GRID_W, win * GRID_W)


def _na_kernel(*refs, scale, n_blk):
    q_ref = refs[0]
    k_refs = refs[1:1 + n_blk]
    v_refs = refs[1 + n_blk:1 + 2 * n_blk]
    kc_ref, vc_ref, b_ref, o_ref = refs[1 + 2 * n_blk:]
    hd = NA_HEAD_DIM
    width = q_ref.shape[1]
    span = min(MXU_DEPTH, width)
    nt = (((1,), (1,)), ((), ()))
    q = q_ref[...] * scale
    kk = jnp.concatenate([r[...] for r in k_refs], axis=0)
    vv = jnp.concatenate([r[...] for r in v_refs], axis=0)
    kc, vc = kc_ref[...], vc_ref[...]
    nq = q.shape[0]
    per = span // hd
    lane = lax.broadcasted_iota(I32, (nq, span), 1)
    own = [(lane >= j * hd) & (lane < (j + 1) * hd) for j in range(per)]
    out_cols = []
    for c0 in range(0, width, span):
        cols = slice(c0, c0 + span)
        qs = jnp.concatenate([jnp.where(own[j], q[:, cols], jnp.zeros_like(q[:, cols])) for j in range(per)], axis=0)
        h0 = c0 // hd
        bias = b_ref[0, h0:h0 + per].reshape(per * nq, kk.shape[0])
        s = lax.dot_general(qs, kk[:, cols], nt, preferred_element_type=F32) + bias
        sc = lax.dot_general(qs, kc[:, cols], nt, preferred_element_type=F32)
        m = jnp.maximum(jnp.max(s, axis=-1, keepdims=True), jnp.max(sc, axis=-1, keepdims=True))
        p = jnp.exp(s - m)
        pc = jnp.exp(sc - m)
        den = jnp.sum(p, axis=-1, keepdims=True) + jnp.sum(pc, axis=-1, keepdims=True)
        o = (jnp.dot(p.astype(BF16), vv[:, cols], preferred_element_type=F32)
             + jnp.dot(pc.astype(BF16), vc[:, cols], preferred_element_type=F32)) / den
        acc = jnp.zeros((nq, span), F32)
        for j in range(per):
            acc = jnp.where(own[j], o[j * nq:(j + 1) * nq], acc)
        out_cols.append(acc)
    o_ref[...] = jnp.concatenate(out_cols, axis=1).astype(o_ref.dtype)


def neighbourhood_attention(q, k, v, k_ctx, v_ctx, bias_table):
    t, w = q.shape
    groups, n_blk, _ = _na_window(t // GRID_W)
    first = lambda g: jnp.clip(g - NA_KH // 2 // NA_STEP_ROWS, 0, groups - n_blk)
    blk_tokens = NA_STEP_ROWS * GRID_W
    row_blk = pl.BlockSpec((blk_tokens, w), lambda g: (g, 0))
    key_blks = [pl.BlockSpec((blk_tokens, w), functools.partial(lambda g, i: (first(g) + i, 0), i=i))
                for i in range(n_blk)]
    ctx_blk = pl.BlockSpec(k_ctx.shape, lambda g: (0, 0))
    bias_blk = pl.BlockSpec((1,) + bias_table.shape[1:], lambda g: (g - first(g), 0, 0, 0))
    return pl.pallas_call(
        functools.partial(_na_kernel, scale=NA_HEAD_DIM ** -0.5, n_blk=n_blk),
        grid=(groups,),
        in_specs=[row_blk] + key_blks + key_blks + [ctx_blk, ctx_blk, bias_blk],
        out_specs=row_blk,
        out_shape=jax.ShapeDtypeStruct((t, w), BF16),
        compiler_params=_params("parallel"),
    )(q, *([k] * n_blk), *([v] * n_blk), k_ctx, v_ctx, bias_table)


def time_dft_tables(t):
    a_len = 1 << (int(math.log2(t)) // 2)
    b_len = t // a_len
    ka = np.arange(a_len)[:, None]
    tok = b_len * np.arange(a_len)[None, :]
    ang1 = -2.0 * np.pi * ((ka * (tok[None] + np.arange(b_len)[:, None, None])) % t) / t
    stage1 = np.concatenate([np.cos(ang1), np.sin(ang1)], axis=1) / math.sqrt(t)
    ang2 = 2.0 * np.pi * ((np.arange(b_len)[:, None] * np.arange(b_len)[None, :]) % b_len) / b_len
    c2, s2 = np.cos(ang2), np.sin(ang2)
    stage2 = np.block([[c2, s2], [-s2, c2]])
    return jnp.asarray(stage1, BF16), jnp.asarray(stage2, BF16), a_len, b_len


def _time_dft_kernel(x_ref, m_ref, w2_ref, zre_ref, zim_ref, y_ref, z_ref, *, a_len, b_len):
    i = pl.program_id(1)
    bb = m_ref.shape[0]

    def stage1(jb, carry):
        b = i * bb + jb
        xb = x_ref[pl.ds(b, a_len, stride=b_len), :]
        y = jnp.dot(m_ref[jb], xb.astype(BF16), preferred_element_type=F32)
        row = pl.multiple_of(b * a_len, a_len)
        y_ref[pl.ds(row, a_len), :] = _pack_pair(y[:a_len], y[a_len:])
        return carry

    lax.fori_loop(0, bb, stage1, 0, unroll=16)

    @pl.when(i == pl.num_programs(1) - 1)
    def _():
        def stage2(ka, carry):
            rows = pl.ds(ka, b_len, stride=a_len)
            yre, yim = _unpack_pair(y_ref[rows, :])
            y = jnp.concatenate([yre, yim], axis=0).astype(BF16)
            z = jnp.dot(w2_ref[...], y, preferred_element_type=F32)
            z_ref[rows, :] = _pack_pair(z[:b_len], z[b_len:])
            return carry

        lax.fori_loop(0, a_len, stage2, 0, unroll=16)
        zre, zim = _unpack_pair(z_ref[...])
        zre_ref[...] = zre.astype(zre_ref.dtype)
        zim_ref[...] = zim.astype(zim_ref.dtype)


def time_dft(x):
    t, d = x.shape
    stage1, stage2, a_len, b_len = time_dft_tables(t)
    lanes = 128
    bb = min(16, b_len)
    out = jax.ShapeDtypeStruct((t, d), BF16)
    return pl.pallas_call(
        functools.partial(_time_dft_kernel, a_len=a_len, b_len=b_len),
        grid=(d // lanes, b_len // bb),
        in_specs=[pl.BlockSpec((t, lanes), lambda j, i: (0, j)),
                  pl.BlockSpec((bb, 2 * a_len, a_len), lambda j, i: (i, 0, 0)),
                  pl.BlockSpec(stage2.shape, lambda j, i: (0, 0))],
        out_specs=[pl.BlockSpec((t, lanes), lambda j, i: (0, j))] * 2,
        out_shape=[out, out],
        scratch_shapes=[pltpu.VMEM((t, lanes), U32), pltpu.VMEM((t, lanes), U32)],
        compiler_params=_params("parallel", "arbitrary"),
    )(x, stage1, stage2)


def channel_dft_tables(c):
    ang = 2.0 * np.pi * ((np.arange(c)[:, None] * np.arange(c)[None, :]) % c) / c
    return (jnp.asarray(np.cos(ang) / math.sqrt(c), BF16), jnp.asarray(np.sin(ang) / math.sqrt(c), BF16))


def _ffn_prologue(h, shf_ref, scf_ref, wr_ref, wsg_ref, wsu_ref, wsd_ref, h_ref, f_ref, lg_ref, shared_ref):
    h_ref[...] = h
    f = _rms(h) * (1.0 + scf_ref[...]) + shf_ref[...]
    packed = _pack_rows(f)
    dq = packed.shape[1] // ROW_PARTS
    for i in range(ROW_PARTS):
        f_ref[i] = packed[:, i * dq:(i + 1) * dq]
    fb = f.astype(BF16)
    f_lo = (f - fb.astype(F32)).astype(BF16)
    nt = (((1,), (1,)), ((), ()))
    wr = wr_ref[...]
    wr_hi = wr.astype(BF16)
    wr_lo = (wr - wr_hi.astype(F32)).astype(BF16)
    lg_ref[...] = (lax.dot_general(wr_hi, fb, nt, preferred_element_type=F32)
                   + lax.dot_general(wr_hi, f_lo, nt, preferred_element_type=F32)
                   + lax.dot_general(wr_lo, fb, nt, preferred_element_type=F32))
    hid = (_silu(jnp.dot(fb, wsg_ref[...], preferred_element_type=F32))
           * jnp.dot(fb, wsu_ref[...], preferred_element_type=F32))
    shared_ref[...] = jnp.dot(hid.astype(BF16), wsd_ref[...], preferred_element_type=F32).astype(shared_ref.dtype)


def _gelu_tanh(x):
    return 0.5 * x * (1.0 + jnp.tanh(math.sqrt(2.0 / math.pi) * (x + 0.044715 * (x * x * x))))


def _even_post_kernel(y_ref, na_ref, x_ref, wglu_ref, bglu_ref, wo_ref, gm_ref, *rest):
    g = _gelu_tanh(y_ref[...])
    gate = jax.nn.sigmoid(jnp.dot(g.astype(BF16), wglu_ref[...], preferred_element_type=F32) + bglu_ref[...])
    s5 = (g * gate).astype(BF16)
    w = s5.shape[1]
    mix = (jnp.dot(s5, wo_ref[:w, :], preferred_element_type=F32)
           + jnp.dot(na_ref[...], wo_ref[w:, :], preferred_element_type=F32))
    _ffn_prologue(x_ref[...] + gm_ref[...] * mix, *rest)


def _odd_post_kernel(zre_ref, zim_ref, h_ref_in, cc_ref, sc_ref, wf_ref, bf_ref, gm_ref, *rest):
    c = cc_ref.shape[0]
    parts = []
    for grp in range(zre_ref.shape[1] // c):
        cols = slice(grp * c, (grp + 1) * c)
        parts.append(jnp.dot(zre_ref[:, cols], cc_ref[...], preferred_element_type=F32)
                     + jnp.dot(zim_ref[:, cols], sc_ref[...], preferred_element_type=F32))
    fr = jnp.concatenate(parts, axis=-1).astype(BF16)
    mix = jnp.dot(fr, wf_ref[...], preferred_element_type=F32) + bf_ref[...]
    _ffn_prologue(h_ref_in[...] + gm_ref[...] * mix, *rest)


def _post_call(body, row_inputs, fixed_inputs, t, d, n_exp):
    tm = min(512, t)
    dq = d // 2 // ROW_PARTS
    row = lambda i: (i, 0)
    in_specs = ([pl.BlockSpec((tm, a.shape[1]), row) for a in row_inputs]
                + [pl.BlockSpec(a.shape, functools.partial(lambda i, nd: (0,) * nd, nd=a.ndim))
                   for a in fixed_inputs])
    return pl.pallas_call(
        body,
        grid=(t // tm,),
        in_specs=in_specs,
        out_specs=[pl.BlockSpec((tm, d), row), pl.BlockSpec((ROW_PARTS, tm, dq), lambda i: (0, i, 0)),
                   pl.BlockSpec((n_exp, tm), lambda i: (0, i)), pl.BlockSpec((tm, d), row)],
        out_shape=[jax.ShapeDtypeStruct((t, d), F32), jax.ShapeDtypeStruct((ROW_PARTS, t, dq), U32),
                   jax.ShapeDtypeStruct((n_exp, t), F32), jax.ShapeDtypeStruct((t, d), BF16)],
        compiler_params=_params("parallel"),
    )(*row_inputs, *fixed_inputs)


def _route_kernel(lg_ref, bias_ref, tri_ref, idx_ref, gate_ref, rank_ref, cnt_ref, run_ref):
    @pl.when(pl.program_id(0) == 0)
    def _():
        run_ref[...] = jnp.zeros_like(run_ref)

    scores = jax.nn.sigmoid(lg_ref[...])
    n_exp, tb = scores.shape
    sel = scores + bias_ref[...]
    gsz = n_exp // N_EXPERT_GROUPS
    member = lax.broadcasted_iota(I32, (gsz, tb), 0)
    gscore = []
    for grp in range(N_EXPERT_GROUPS):
        xg = sel[grp * gsz:(grp + 1) * gsz, :]
        m1 = jnp.max(xg, axis=0, keepdims=True)
        first = jnp.min(jnp.where(xg == m1, member, gsz), axis=0, keepdims=True)
        m2 = jnp.max(jnp.where(member == first, -jnp.inf, xg), axis=0, keepdims=True)
        gscore.append(m1 + m2)
    keep_rows = []
    for grp in range(N_EXPERT_GROUPS):
        beaten = jnp.zeros((1, tb), F32)
        for other in range(N_EXPERT_GROUPS):
            if other == grp:
                continue
            wins = (gscore[other] >= gscore[grp]) if other < grp else (gscore[other] > gscore[grp])
            beaten = beaten + jnp.where(wins, 1.0, 0.0)
        keep_rows.append(jnp.broadcast_to(beaten < TOPK_GROUPS, (gsz, tb)))
    masked = jnp.where(jnp.concatenate(keep_rows, axis=0), sel, -jnp.inf)

    expert = lax.broadcasted_iota(I32, (n_exp, tb), 0)
    picks, gates, hots = [], [], []
    chosen = jnp.zeros((n_exp, tb), F32)
    for _ in range(TOP_K):
        m = jnp.max(masked, axis=0, keepdims=True)
        pick = jnp.min(jnp.where(masked == m, expert, n_exp), axis=0, keepdims=True)
        hot = expert == pick
        picks.append(pick)
        hots.append(hot)
        gates.append(jnp.sum(jnp.where(hot, scores, 0.0), axis=0, keepdims=True))
        chosen = jnp.where(hot, 1.0, chosen)
        masked = jnp.where(hot, -jnp.inf, masked)
    total = gates[0]
    for gk in gates[1:]:
        total = total + gk
    ahead = jnp.dot(chosen.astype(BF16), tri_ref[...], preferred_element_type=F32) + run_ref[...]
    for k in range(TOP_K):
        idx_ref[k:k + 1, :] = picks[k]
        gate_ref[k:k + 1, :] = ROUTED_SCALE * gates[k] / total
        rank_ref[k:k + 1, :] = jnp.sum(jnp.where(hots[k], ahead, 0.0), axis=0, keepdims=True).astype(I32)
    run_ref[...] = run_ref[...] + jnp.sum(chosen, axis=1, keepdims=True)
    cnt_ref[...] = jnp.broadcast_to(run_ref[...], cnt_ref.shape)


def route(logits_t, router_bias):
    n_exp, t = logits_t.shape
    tb = min(512, t)
    tri = jnp.asarray(np.triu(np.ones((tb, tb), np.float32), k=1), BF16)
    tok = lambda i: (0, i)
    idx, gate, rank, cnt = pl.pallas_call(
        _route_kernel,
        grid=(t // tb,),
        in_specs=[pl.BlockSpec((n_exp, tb), tok), pl.BlockSpec((n_exp, 1), lambda i: (0, 0)),
                  pl.BlockSpec((tb, tb), lambda i: (0, 0))],
        out_specs=[pl.BlockSpec((TOP_K, tb), tok)] * 3 + [pl.BlockSpec((n_exp, 128), lambda i: (0, 0))],
        out_shape=[jax.ShapeDtypeStruct((TOP_K, t), I32), jax.ShapeDtypeStruct((TOP_K, t), F32),
                   jax.ShapeDtypeStruct((TOP_K, t), I32), jax.ShapeDtypeStruct((n_exp, 128), F32)],
        scratch_shapes=[pltpu.VMEM((n_exp, 1), F32)],
        compiler_params=_params("arbitrary"),
    )(logits_t, router_bias.astype(F32).reshape(n_exp, 1), tri)
    return idx, gate, rank, cnt[:, 0].astype(I32)


def dispatch_plan(idx, rank, counts, n_blocks):
    n_exp = counts.shape[0]
    shift = EXPERT_ROWS.bit_length() - 1
    assert EXPERT_ROWS == 1 << shift
    padded = ((counts + EXPERT_ROWS - 1) >> shift) << shift
    pad_end = jnp.cumsum(padded)
    pad_start = pad_end - padded
    experts_iota = jnp.arange(n_exp, dtype=I32)
    dest = jnp.sum(jnp.where(idx[..., None] == experts_iota, pad_start, 0), axis=-1) + rank
    n_valid = (pad_end[-1] >> shift).astype(I32)
    blk = jnp.minimum(jnp.arange(n_blocks, dtype=I32), n_valid - 1)
    blk_e = jnp.minimum(jnp.sum((pad_end[None, :] <= blk[:, None] * EXPERT_ROWS).astype(I32), axis=1), n_exp - 1)
    used_end = jnp.sum(jnp.where(blk_e[:, None] == experts_iota, pad_start + counts, 0), axis=-1)
    blk_used = jnp.clip(used_end - blk * EXPERT_ROWS, 0, EXPERT_ROWS)
    return dest.astype(I32), blk_e.astype(I32), blk_used.astype(I32), n_valid.reshape(1)


def _sc_mesh():
    return plsc.VectorSubcoreMesh(core_axis_name="core", subcore_axis_name="subcore")


def dispatch(f, dest, cap):
    t, w = f.shape
    n_choice = dest.shape[0]

    @functools.partial(pl.kernel, out_type=jax.ShapeDtypeStruct((cap, w), f.dtype), mesh=_sc_mesh(),
                       scratch_types=[pltpu.SemaphoreType.DMA])
    def scatter_rows(x_hbm, i_hbm, o_hbm, sem):
        def body(x_vmem, i_vmem):
            copies = [pltpu.async_copy(x_vmem, o_hbm.at[i_vmem.at[k]], sem) for k in range(n_choice)]
            for cp in copies:
                cp.wait()

        pltpu.emit_pipeline(
            body,
            grid=(t // SC_WINDOW,),
            in_specs=[pl.BlockSpec((SC_WINDOW, w), lambda i: (i, 0)),
                      pl.BlockSpec((n_choice, SC_WINDOW), lambda i: (0, i))],
            out_specs=[],
            core_axis_name=("core", "subcore"),
            dimension_semantics=(pltpu.PARALLEL,),
        )(x_hbm, i_hbm)

    return scatter_rows(f, dest)


def gather_rows(rows, index_row):
    n = index_row.shape[1]
    w = rows.shape[1]

    piece = SC_WINDOW // SC_GATHER_SPLIT

    @functools.partial(pl.kernel, out_type=jax.ShapeDtypeStruct((n, w), rows.dtype), mesh=_sc_mesh(),
                       scratch_types=[pltpu.SemaphoreType.DMA])
    def gather(y_hbm, i_hbm, o_hbm, sem):
        def body(i_vmem, o_vmem):
            copies = [pltpu.async_copy(y_hbm.at[i_vmem.at[0, pl.ds(j * piece, piece)]],
                                       o_vmem.at[pl.ds(j * piece, piece)], sem)
                      for j in range(SC_GATHER_SPLIT)]
            for cp in copies:
                cp.wait()

        pltpu.emit_pipeline(
            body,
            grid=(n // SC_WINDOW,),
            in_specs=[pl.BlockSpec((1, SC_WINDOW), lambda i: (0, i))],
            out_specs=[pl.BlockSpec((SC_WINDOW, w), lambda i: (i, 0))],
            core_axis_name=("core", "subcore"),
            dimension_semantics=(pltpu.PARALLEL,),
        )(i_hbm, o_hbm)

    return gather(rows, index_row)


def _experts_kernel(be_ref, used_ref, x_ref, wg_ref, wu_ref, wd_ref, y_ref, wg_bf, wu_bf, wd_bf):
    b = pl.program_id(0)

    @pl.when((b == 0) | (be_ref[b] != be_ref[jnp.maximum(b - 1, 0)]))
    def _():
        wg_bf[...] = wg_ref[0, 0].astype(BF16)
        wu_bf[...] = wu_ref[0, 0].astype(BF16)
        wd_bf[...] = wd_ref[0, 0].astype(BF16)

    def mlp(n_rows):
        x = _unpack_rows(jnp.concatenate([x_ref[i, :n_rows] for i in range(ROW_PARTS)], axis=1)).astype(BF16)
        hid = (_silu(jnp.dot(x, wg_bf[...], preferred_element_type=F32))
               * jnp.dot(x, wu_bf[...], preferred_element_type=F32))
        packed = _pack_rows(jnp.dot(hid.astype(BF16), wd_bf[...], preferred_element_type=F32))
        dq = packed.shape[1] // ROW_PARTS
        for i in range(ROW_PARTS):
            y_ref[i, :n_rows] = packed[:, i * dq:(i + 1) * dq]

    used = used_ref[b]
    step = x_ref.shape[1] // EXPERT_ROW_STEPS
    for q in range(1, EXPERT_ROW_STEPS + 1):
        pl.when((used > (q - 1) * step) & (used <= q * step))(functools.partial(mlp, q * step))


def experts(rows, blk_e, blk_used, n_valid, layer, w_gate, w_up, w_down):
    parts, cap, dq = rows.shape
    _, _, d, ff = w_gate.shape
    blk = pl.BlockSpec((parts, EXPERT_ROWS, dq), lambda b, be, used: (0, b, 0))
    weight = lambda b, be, used: (layer, be[b], 0, 0)
    return pl.pallas_call(
        _experts_kernel,
        grid_spec=pltpu.PrefetchScalarGridSpec(
            num_scalar_prefetch=2,
            grid=(n_valid[0],),
            in_specs=[blk, pl.BlockSpec((1, 1, d, ff), weight), pl.BlockSpec((1, 1, d, ff), weight),
                      pl.BlockSpec((1, 1, ff, d), weight)],
            out_specs=blk,
            scratch_shapes=[pltpu.VMEM((d, ff), BF16), pltpu.VMEM((d, ff), BF16), pltpu.VMEM((ff, d), BF16)],
        ),
        out_shape=jax.ShapeDtypeStruct(rows.shape, U32),
        compiler_params=_params("arbitrary"),
    )(blk_e, blk_used, rows, w_gate, w_up, w_down)


def _combine_kernel(gate_ref, h_ref, shared_ref, gf_ref, sh_ref, sc_ref, *rest):
    y_refs, outs = rest[:TOP_K], rest[TOP_K:]
    routed = None
    for k in range(TOP_K):
        y = _unpack_rows(jnp.concatenate([y_refs[k][i, 0] for i in range(ROW_PARTS)], axis=1))
        routed = gate_ref[:, k:k + 1] * y if routed is None else routed + gate_ref[:, k:k + 1] * y
    h = h_ref[...] + gf_ref[...] * (routed + shared_ref[...].astype(F32))
    outs[0][...] = h
    if len(outs) > 1:
        outs[1][...] = _rms(h) * (1.0 + sc_ref[...]) + sh_ref[...]


def combine(y_rows, dest, gate_tk, h, shared, gate_ffn, next_shift, next_scale, with_next):
    t, d = h.shape
    parts, cap, dq = y_rows.shape
    n_out = 2 if with_next else 1
    n_choice = dest.shape[0]
    flat = (dest[None] + (jnp.arange(parts, dtype=I32) * cap)[:, None, None]).reshape(1, parts * n_choice * t)
    picked = gather_rows(y_rows.reshape(parts * cap, dq), flat).reshape(parts, n_choice, t, dq)
    tm = min(256, t)
    row = lambda i: (i, 0)
    fixed = lambda i: (0, 0)
    choice = [pl.BlockSpec((parts, 1, tm, dq), functools.partial(lambda i, k: (0, k, i, 0), k=k))
              for k in range(n_choice)]
    return pl.pallas_call(
        _combine_kernel,
        grid=(t // tm,),
        in_specs=[pl.BlockSpec((tm, n_choice), row), pl.BlockSpec((tm, d), row), pl.BlockSpec((tm, d), row),
                  pl.BlockSpec((1, d), fixed), pl.BlockSpec((1, d), fixed), pl.BlockSpec((1, d), fixed)] + choice,
        out_specs=[pl.BlockSpec((tm, d), row)] * n_out,
        out_shape=[jax.ShapeDtypeStruct((t, d), F32)] * n_out,
        compiler_params=_params("parallel"),
    )(gate_tk, h, shared, gate_ffn, next_shift, next_scale, *([picked] * n_choice))


def moe_tail(h, f_rows, logits_t, shared, router_bias, layer, w_gate, w_up, w_down, gate_ffn,
             next_shift, next_scale, with_next):
    t, d = h.shape
    parts, _, dq = f_rows.shape
    n_exp = w_gate.shape[1]
    n_blocks = -(-(t * TOP_K + n_exp * (EXPERT_ROWS - 1)) // EXPERT_ROWS)
    cap = n_blocks * EXPERT_ROWS
    idx, gate, rank, counts = route(logits_t, router_bias)
    dest, blk_e, blk_used, n_valid = dispatch_plan(idx, rank, counts, n_blocks)
    dest_all = jnp.concatenate([dest + i * cap for i in range(parts)], axis=1)
    rows = dispatch(f_rows.reshape(parts * t, dq), dest_all, parts * cap).reshape(parts, cap, dq)
    y_rows = experts(rows, blk_e, blk_used, n_valid, layer, w_gate, w_up, w_down)
    return combine(y_rows, dest, gate.T, h, shared, gate_ffn, next_shift, next_scale, with_next)


def kernel(x, c, ctx, c_ctx, w_ada, b_ada, w_in, s5_lam_re, s5_lam_im, s5_log_dt, s5_b_re, s5_b_im,
           s5_c_re, s5_c_im, s5_d, s5_w_glu, s5_b_glu, na_q_gain, na_k_gain, na_rpb, w_mix_out,
           w_fourier_out, b_fourier_out, w_router, router_bias, w_exp_gate, w_exp_up, w_exp_down,
           w_sh_gate, w_sh_up, w_sh_down):
    bsz, t, d = x.shape
    assert bsz == 1 and w_ada.shape[0] == 2
    n_exp = w_router.shape[2]
    s5w = s5_w_glu.shape[1]
    naw = w_in.shape[2] - s5w
    naw //= 3
    heads = naw // NA_HEAD_DIM

    cond8 = jnp.concatenate([c[:1].astype(F32), c_ctx.astype(F32)[None], jnp.zeros((6, d), F32)], axis=0)
    ada = adaln_all(cond8, w_ada, b_ada)
    mod = lambda layer, who, j: ada[layer, who:who + 1, j * d:(j + 1) * d]

    def ffn_weights(i):
        return (mod(i, 0, 3), mod(i, 0, 4), jnp.transpose(w_router[i]).astype(F32),
                w_sh_gate[i].astype(BF16), w_sh_up[i].astype(BF16), w_sh_down[i].astype(BF16))

    h0 = x[0]
    seg = jnp.asarray(np.kron(np.eye(heads), np.ones((NA_HEAD_DIM, NA_HEAD_DIM))), BF16)
    w_in_b = w_in[0].astype(BF16)
    qg = jnp.tile(na_q_gain[0].astype(F32), heads)[None]
    kg = jnp.tile(na_k_gain[0].astype(F32), heads)[None]
    u_c, _, k_c, v_c = in_projection(ctx[0], mod(0, 1, 0), mod(0, 1, 1), w_in_b, seg, qg, kg, s5w, naw)
    u_l, q_l, k_l, v_l = in_projection(h0, mod(0, 0, 0), mod(0, 0, 1), w_in_b, seg, qg, kg, s5w, naw)
    mats = s5_matrices(s5_lam_re[0], s5_lam_im[0], s5_log_dt[0], s5_b_re[0], s5_b_im[0],
                       s5_c_re[0], s5_c_im[0], s5_d[0])
    y_s5 = s5_mixer(u_c, u_l, mats)
    na = neighbourhood_attention(q_l, k_l, v_l, k_c, v_c, na_bias_table(na_rpb[0], t // GRID_W))
    h1, f1, lg1, sh1 = _post_call(
        _even_post_kernel, [y_s5, na, h0],
        [s5_w_glu[0].astype(BF16), s5_b_glu[0].astype(F32)[None], w_mix_out[0].astype(BF16), mod(0, 0, 2),
         *ffn_weights(0)], t, d, n_exp)
    h2, a1 = moe_tail(h1, f1, lg1, sh1, router_bias[0], 0, w_exp_gate, w_exp_up, w_exp_down,
                      mod(0, 0, 5), mod(1, 0, 0), mod(1, 0, 1), True)

    zre, zim = time_dft(a1)
    cc, sc = channel_dft_tables(d // FOURIER_GROUPS)
    h3, f3, lg3, sh3 = _post_call(
        _odd_post_kernel, [zre, zim, h2],
        [cc, sc, w_fourier_out[0].astype(BF16), b_fourier_out[0].astype(F32)[None], mod(1, 0, 2),
         *ffn_weights(1)], t, d, n_exp)
    zero_row = jnp.zeros((1, d), F32)
    (out,) = moe_tail(h3, f3, lg3, sh3, router_bias[1], 1, w_exp_gate, w_exp_up, w_exp_down,
                      mod(1, 0, 5), zero_row, zero_row, False)
    return out[None]
```

```python
import functools
import math

import numpy as np
import jax
import jax.numpy as jnp
from jax import lax
from jax.experimental import pallas as pl
from jax.experimental.pallas import tpu as pltpu
from jax.experimental.pallas import tpu_sc as plsc

F32 = jnp.float32
BF16 = jnp.bfloat16
I32 = jnp.int32
U32 = jnp.uint32
HIGHEST = lax.Precision.HIGHEST

MXU_DEPTH = 256
LANES = 128
GRID_W = 64
NORM_EPS = 1e-6
S5_GROUP = 16
S5_LAMBDA_RE_MAX = -1e-4
S5_CHUNK = 16
S5_TILE = 128
NA_HEAD_DIM = 64
NA_KH = 8
NA_KW = 16
NA_STEP_ROWS = 2
FOURIER_GROUPS = 4
N_EXPERT_GROUPS = 8
TOPK_GROUPS = 4
TOP_K = 8
ROUTED_SCALE = 2.5
EXPERT_ROWS = 2048
ROW_PARTS = 2
SC_WINDOW = 128
SC_GATHER_SPLIT = 4
NEG_BIG = -1e30

VMEM_LIMIT_BYTES = 56 * 1024 * 1024


def _params(*sem):
    return pltpu.CompilerParams(dimension_semantics=sem or None,
                                vmem_limit_bytes=VMEM_LIMIT_BYTES)


def _rms(x):
    return x * lax.rsqrt(jnp.mean(x * x, axis=-1, keepdims=True) + NORM_EPS)


def _silu(x):
    return x * jax.nn.sigmoid(x)


def _pack_pair(lo, hi):
    lo = lax.bitcast_convert_type(lo.astype(BF16).astype(F32), U32)
    hi = lax.bitcast_convert_type(hi.astype(BF16).astype(F32), U32)
    return (hi & jnp.uint32(0xFFFF0000)) | (lo >> 16)


def _unpack_pair(w):
    return (lax.bitcast_convert_type(w << 16, F32), lax.bitcast_convert_type(w & jnp.uint32(0xFFFF0000), F32))


def _pack_rows(x):
    n = x.shape[1] // 2
    return _pack_pair(x[:, :n], x[:, n:])


def _unpack_rows(w):
    return jnp.concatenate(_unpack_pair(w), axis=1)


def _ada_kernel(c_ref, w_ref, b_ref, o_ref):
    o_ref[0] = jnp.dot(_silu(c_ref[...]), w_ref[0], preferred_element_type=F32,
                       precision=HIGHEST) + b_ref[0]


def adaln_all(cond8, w_ada, b_ada):
    n_layers, d, n6 = w_ada.shape
    tn = n6 // 4
    return pl.pallas_call(
        _ada_kernel,
        grid=(n_layers, n6 // tn),
        in_specs=[pl.BlockSpec((8, d), lambda l, j: (0, 0)),
                  pl.BlockSpec((1, d, tn), lambda l, j: (l, 0, j)),
                  pl.BlockSpec((1, 1, tn), lambda l, j: (l, 0, j))],
        out_specs=pl.BlockSpec((1, 8, tn), lambda l, j: (l, 0, j)),
        out_shape=jax.ShapeDtypeStruct((n_layers, 8, n6), F32),
        compiler_params=_params("parallel", "parallel"),
    )(cond8, w_ada, b_ada.reshape(n_layers, 1, n6))


def _inproj_kernel(x_ref, sh_ref, sc_ref, w_ref, seg_ref, qg_ref, kg_ref,
                   u_ref, q_ref, k_ref, v_ref):
    a = _rms(x_ref[...]) * (1.0 + sc_ref[...]) + sh_ref[...]
    z = jnp.dot(a.astype(BF16), w_ref[...], preferred_element_type=F32)
    s5w = u_ref.shape[1]
    naw = q_ref.shape[1]

    def head_norm(t, gain):
        ss = jnp.dot((t * t).astype(BF16), seg_ref[...], preferred_element_type=F32)
        return t * lax.rsqrt(ss * (1.0 / NA_HEAD_DIM) + NORM_EPS) * gain

    u_ref[...] = z[:, :s5w]
    q_ref[...] = head_norm(z[:, s5w:s5w + naw], qg_ref[...]).astype(BF16)
    k_ref[...] = head_norm(z[:, s5w + naw:s5w + 2 * naw], kg_ref[...]).astype(BF16)
    v_ref[...] = z[:, s5w + 2 * naw:].astype(BF16)


def in_projection(x, shift, scale, w_in_bf16, seg_ones, q_gain_row, k_gain_row, s5w, naw):
    t, d = x.shape
    tm = min(512, t)
    row = lambda i: (i, 0)
    fixed = lambda i: (0, 0)
    return pl.pallas_call(
        _inproj_kernel,
        grid=(t // tm,),
        in_specs=[pl.BlockSpec((tm, d), row),
                  pl.BlockSpec((1, d), fixed), pl.BlockSpec((1, d), fixed),
                  pl.BlockSpec(w_in_bf16.shape, fixed),
                  pl.BlockSpec(seg_ones.shape, fixed),
                  pl.BlockSpec((1, naw), fixed), pl.BlockSpec((1, naw), fixed)],
        out_specs=[pl.BlockSpec((tm, s5w), row), pl.BlockSpec((tm, naw), row),
                   pl.BlockSpec((tm, naw), row), pl.BlockSpec((tm, naw), row)],
        out_shape=[jax.ShapeDtypeStruct((t, s5w), F32)] + [jax.ShapeDtypeStruct((t, naw), BF16)] * 3,
        compiler_params=_params("parallel"),
    )(x, shift, scale, w_in_bf16, seg_ones, q_gain_row, k_gain_row)


def s5_matrices(lam_re, lam_im, log_dt, b_re, b_im, c_re, c_im, d_skip):
    L = S5_CHUNK
    c = S5_GROUP
    taus = jnp.arange(L + 1, dtype=F32)

    def direction(i):
        lam = lax.complex(jnp.minimum(lam_re[i].astype(F32), S5_LAMBDA_RE_MAX), lam_im[i].astype(F32))
        ldt = lam * jnp.exp(log_dt[i].astype(F32))[:, None]
        lam_bar = jnp.exp(ldt)
        b_bar = ((lam_bar - 1.0) / lam)[..., None] * lax.complex(b_re[i].astype(F32), b_im[i].astype(F32))
        cc = lax.complex(c_re[i].astype(F32), c_im[i].astype(F32))
        powers = jnp.exp(ldt[None] * taus[:, None, None])
        resp = jnp.real(jnp.einsum('gcp,tgp,gpd->gctd', cc, powers[:L], b_bar, precision=HIGHEST))
        return jnp.transpose(powers, (1, 2, 0)), b_bar, cc, resp

    pw_f, bb_f, cc_f, k_f = direction(0)
    pw_b, bb_b, cc_b, k_b = direction(1)
    g, p = pw_f.shape[:2]
    lag0 = k_f[:, :, :1] + k_b[:, :, :1] + (jnp.eye(c, dtype=F32)[None, :, None, :] * d_skip.astype(F32)[:, :, None, None])
    by_lag = jnp.concatenate([jnp.flip(k_f[:, :, 1:], axis=2), lag0, k_b[:, :, 1:]], axis=2).reshape(g, c, (2 * L - 1) * c)
    toep_t = jnp.stack([by_lag[:, :, (L - 1 - l) * c:(2 * L - 1 - l) * c] for l in range(L)], axis=1)
    toep_t = toep_t.reshape(g, L * c, L * c)

    def state_in(pw_by_s, b_bar):
        return (pw_by_s[:, :, :, None] * b_bar[:, :, None, :]).reshape(g, p, L * c)

    wf = state_in(jnp.flip(pw_f[:, :, :L], axis=2), bb_f)
    wb = state_in(pw_b[:, :, :L], bb_b)
    w_state_t = jnp.concatenate([jnp.real(wf), jnp.imag(wf), jnp.imag(wf), jnp.real(wf),
                                 jnp.real(wb), jnp.imag(wb), jnp.imag(wb), jnp.real(wb)], axis=1)

    def state_out(pw_by_l, cc):
        return (jnp.transpose(pw_by_l, (0, 2, 1))[:, :, None, :] * cc[:, None, :, :]).reshape(g, L * c, p)

    rf = state_out(pw_f[:, :, 1:], cc_f)
    rb = state_out(jnp.flip(pw_b[:, :, 1:], axis=2), cc_b)
    r_state_t = jnp.concatenate([jnp.real(rf), -jnp.imag(rf), jnp.real(rb), -jnp.imag(rb)], axis=-1)

    def mult(a):
        ar, ai = jnp.real(a), jnp.imag(a)
        return jnp.stack([jnp.concatenate([ar, ar], -1), jnp.concatenate([-ai, ai], -1),
                          jnp.concatenate([ai, -ai], -1)])

    return (toep_t.astype(BF16), w_state_t.astype(BF16), r_state_t.astype(BF16),
            mult(pw_f[:, :, L]), mult(pw_b[:, :, L]))


def _s5_pack_kernel(*refs):
    u_refs, (wt_ref, ut_ref, f1_ref, f2_ref, b1_ref, b2_ref) = refs[:-6], refs[-6:]
    L = S5_CHUNK
    g, lc, nck = ut_ref.shape
    c = lc // L
    gs = g // len(u_refs)
    for s in range(L):
        for j, u_ref in enumerate(u_refs):
            step_s = u_ref[pl.ds(s, nck, stride=L), :]
            ut_ref[j * gs:(j + 1) * gs, s * c:(s + 1) * c, :] = (
                jnp.transpose(step_s).astype(BF16).reshape(gs, c, nck))
    n = f1_ref.shape[2]
    for gi in range(g):
        inc = jnp.dot(wt_ref[gi], ut_ref[gi], preferred_element_type=F32)
        for j, ref in enumerate((f1_ref, f2_ref, b1_ref, b2_ref)):
            ref[:, gi, :] = jnp.transpose(inc[j * n:(j + 1) * n, :])


def s5_pack(u, wt_state):
    t, w = u.shape
    g, n4, lc = wt_state.shape
    nc = t // S5_CHUNK
    tile = min(S5_TILE, nc)
    inc = jax.ShapeDtypeStruct((nc, g, n4 // 4), F32)
    inc_blk = pl.BlockSpec((tile, g, n4 // 4), lambda i: (i, 0, 0))
    return pl.pallas_call(
        _s5_pack_kernel,
        grid=(nc // tile,),
        in_specs=[pl.BlockSpec((tile * S5_CHUNK, LANES), functools.partial(lambda i, j: (i, j), j=j))
                  for j in range(w // LANES)] + [pl.BlockSpec(wt_state.shape, lambda i: (0, 0, 0))],
        out_specs=[pl.BlockSpec((g, lc, tile), lambda i: (0, 0, i))] + [inc_blk] * 4,
        out_shape=[jax.ShapeDtypeStruct((g, lc, nc), BF16)] + [inc] * 4,
        compiler_params=_params("parallel"),
    )(*([u] * (w // LANES)), wt_state)


def _s5_scan_kernel(s1_ref, s2_ref, m_ref, init_ref, x_ref, last_ref, v1_ref, v2_ref, *, reverse):
    @pl.when(pl.program_id(0) == 0)
    def _():
        v1_ref[...] = init_ref[0]
        v2_ref[...] = init_ref[1]

    a1, a2, a3 = m_ref[0], m_ref[1], m_ref[2]
    cb = s1_ref.shape[0]

    def body(j, carry):
        v1, v2 = carry
        jj = cb - 1 - j if reverse else j
        x_ref[jj] = v1
        return (a1 * v1 + a2 * v2 + s1_ref[jj], a1 * v2 + a3 * v1 + s2_ref[jj])

    v1, v2 = lax.fori_loop(0, cb, body, (v1_ref[...], v2_ref[...]))
    v1_ref[...] = v1
    v2_ref[...] = v2
    last_ref[...] = v1


def s5_chunk_scan(s1, s2, mult, init, reverse):
    nc, g, n = s1.shape
    cb = min(S5_TILE, nc)
    nb = nc // cb
    blk = (lambda i: (nb - 1 - i, 0, 0)) if reverse else (lambda i: (i, 0, 0))
    return pl.pallas_call(
        functools.partial(_s5_scan_kernel, reverse=reverse),
        grid=(nb,),
        in_specs=[pl.BlockSpec((cb, g, n), blk), pl.BlockSpec((cb, g, n), blk),
                  pl.BlockSpec((3, g, n), lambda i: (0, 0, 0)), pl.BlockSpec((2, g, n), lambda i: (0, 0, 0))],
        out_specs=[pl.BlockSpec((cb, g, n), blk), pl.BlockSpec((g, n), lambda i: (0, 0))],
        out_shape=[jax.ShapeDtypeStruct((nc, g, n), F32), jax.ShapeDtypeStruct((g, n), F32)],
        scratch_shapes=[pltpu.VMEM((g, n), F32), pltpu.VMEM((g, n), F32)],
        compiler_params=_params("arbitrary"),
    )(s1, s2, mult, init)


def _s5_readout_kernel(ut_ref, tt_ref, rt_ref, xf_ref, xb_ref, y_ref, yt_ref, *slab_refs):
    L = S5_CHUNK
    g, lc, nck = ut_ref.shape
    c = lc // L
    gs = g // len(slab_refs)
    for gi in range(g):
        xin_t = jnp.concatenate([jnp.transpose(xf_ref[:, gi, :]), jnp.transpose(xb_ref[:, gi, :])], axis=0)
        yt_ref[gi] = (jnp.dot(tt_ref[gi], ut_ref[gi], preferred_element_type=F32)
                      + jnp.dot(rt_ref[gi], xin_t.astype(BF16), preferred_element_type=F32))
    for j, slab in enumerate(slab_refs):
        for l in range(L):
            step_l = yt_ref[j * gs:(j + 1) * gs, l * c:(l + 1) * c, :].reshape(gs * c, nck)
            slab[pl.ds(l, nck, stride=L), :] = jnp.transpose(step_l)
        y_ref[:, j * gs * c:(j + 1) * gs * c] = slab[...]


def s5_readout(ut, toep_t, r_state_t, xin_f, xin_b):
    g, lc, nc = ut.shape
    n = xin_f.shape[2]
    tile = min(S5_TILE, nc)
    fixed = lambda i: (0, 0, 0)
    state_blk = pl.BlockSpec((tile, g, n), lambda i: (i, 0, 0))
    return pl.pallas_call(
        _s5_readout_kernel,
        grid=(nc // tile,),
        in_specs=[pl.BlockSpec((g, lc, tile), lambda i: (0, 0, i)),
                  pl.BlockSpec(toep_t.shape, fixed), pl.BlockSpec(r_state_t.shape, fixed), state_blk, state_blk],
        out_specs=pl.BlockSpec((tile * S5_CHUNK, g * lc // S5_CHUNK), lambda i: (i, 0)),
        out_shape=jax.ShapeDtypeStruct((nc * S5_CHUNK, g * lc // S5_CHUNK), F32),
        scratch_shapes=[pltpu.VMEM((g, lc, tile), F32)]
        + [pltpu.VMEM((tile * S5_CHUNK, LANES), F32)] * (g * lc // S5_CHUNK // LANES),
        compiler_params=_params("parallel"),
    )(ut, toep_t, r_state_t, xin_f, xin_b)


def s5_mixer(u_ctx, u_lat, mats):
    toep_t, wt_state, r_state_t, mult_f, mult_b = mats
    L = S5_CHUNK
    g, n = mult_f.shape[1:]
    halves = lambda v: jnp.stack([v, jnp.roll(v, n // 2, axis=-1)])
    n_ctx = u_ctx.shape[0] // L
    ctx_chunks = -(-(n_ctx + 1) // S5_TILE) * S5_TILE
    ctx_pad = jnp.pad(u_ctx, ((0, ctx_chunks * L - u_ctx.shape[0]), (0, 0)))
    _, cf1, cf2, cb1, cb2 = s5_pack(ctx_pad, wt_state)
    zero = jnp.zeros((2, g, n), F32)
    ctx_f, _ = s5_chunk_scan(cf1, cf2, mult_f, zero, False)
    _, ctx_b_last = s5_chunk_scan(cb1, cb2, mult_b, zero, True)
    ut, f1, f2, b1, b2 = s5_pack(u_lat, wt_state)
    xin_f, _ = s5_chunk_scan(f1, f2, mult_f, halves(ctx_f[n_ctx]), False)
    xin_b, _ = s5_chunk_scan(b1, b2, mult_b, halves(ctx_b_last), True)
    return s5_readout(ut, toep_t, r_state_t, xin_f, xin_b)


def _na_window(rows):
    step = NA_STEP_ROWS
    assert rows % step == 0 and (NA_KH // 2) % step == 0 and rows >= NA_KH + step
    groups = rows // step
    n_blk = (NA_KH + step) // step
    first = np.clip(np.arange(groups) - NA_KH // 2 // step, 0, groups - n_blk)
    return groups, n_blk, first


def na_bias_table(rpb, rows):
    step = NA_STEP_ROWS
    groups, n_blk, first = _na_window(rows)
    win = n_blk * step
    q_col = np.arange(GRID_W)
    col_start = np.clip(q_col - NA_KW // 2, 0, GRID_W - NA_KW)
    key_col = np.arange(GRID_W)
    off = key_col[None, :] - col_start[:, None]
    valid_col = (off >= 0) & (off < NA_KW)
    rel_col = np.clip(key_col[None, :] - q_col[:, None] + NA_KW - 1, 0, 2 * NA_KW - 2)
    q_row = step * np.arange(groups)[:, None, None] + np.arange(step)[None, :, None]
    key_row = step * first[:, None, None] + np.arange(win)[None, None, :]
    row0 = np.clip(q_row - NA_KH // 2, 0, rows - NA_KH)
    rel_row = np.where((key_row >= row0) & (key_row < row0 + NA_KH), key_row - q_row + NA_KH - 1, -1)
    variant = np.arange(groups) - first
    reps = [int(np.argmax(variant == d)) for d in range(n_blk)]
    assert all((rel_row[g] == rel_row[reps[variant[g]]]).all() for g in range(groups))
    rel_row = rel_row[reps]
    pick_row = jnp.asarray(np.arange(2 * NA_KH - 1)[:, None, None, None] == rel_row[None], F32)
    pick_col = jnp.asarray(np.arange(2 * NA_KW - 1)[:, None, None] == rel_col[None], F32)
    table = jnp.einsum('hrx,rvji,xck->vhjcik', rpb.astype(F32), pick_row, pick_col, precision=HIGHEST)
    valid = (rel_row >= 0)[:, None, :, None, :, None] & valid_col[None, None, None, :, None, :]
    return jnp.where(valid, table, NEG_BIG).reshape(n_blk, rpb.shape[0], step * GRID_W, win * GRID_W)


def _na_kernel(*refs, scale, n_blk):
    q_ref = refs[0]
    k_refs = refs[1:1 + n_blk]
    v_refs = refs[1 + n_blk:1 + 2 * n_blk]
    kc_ref, vc_ref, b_ref, o_ref = refs[1 + 2 * n_blk:]
    hd = NA_HEAD_DIM
    width = q_ref.shape[1]
    span = min(MXU_DEPTH, width)
    nt = (((1,), (1,)), ((), ()))
    q = q_ref[...] * scale
    kk = jnp.concatenate([r[...] for r in k_refs], axis=0)
    vv = jnp.concatenate([r[...] for r in v_refs], axis=0)
    kc, vc = kc_ref[...], vc_ref[...]
    nq = q.shape[0]
    per = span // hd
    lane = lax.broadcasted_iota(I32, (nq, span), 1)
    own = [(lane >= j * hd) & (lane < (j + 1) * hd) for j in range(per)]
    out_cols = []
    for c0 in range(0, width, span):
        cols = slice(c0, c0 + span)
        qs = jnp.concatenate([jnp.where(own[j], q[:, cols], jnp.zeros_like(q[:, cols])) for j in range(per)], axis=0)
        h0 = c0 // hd
        bias = b_ref[0, h0:h0 + per].reshape(per * nq, kk.shape[0])
        s = lax.dot_general(qs, kk[:, cols], nt, preferred_element_type=F32) + bias
        sc = lax.dot_general(qs, kc[:, cols], nt, preferred_element_type=F32)
        m = jnp.maximum(jnp.max(s, axis=-1, keepdims=True), jnp.max(sc, axis=-1, keepdims=True))
        p = jnp.exp(s - m)
        pc = jnp.exp(sc - m)
        den = jnp.sum(p, axis=-1, keepdims=True) + jnp.sum(pc, axis=-1, keepdims=True)
        o = (jnp.dot(p.astype(BF16), vv[:, cols], preferred_element_type=F32)
             + jnp.dot(pc.astype(BF16), vc[:, cols], preferred_element_type=F32)) / den
        acc = jnp.zeros((nq, span), F32)
        for j in range(per):
            acc = jnp.where(own[j], o[j * nq:(j + 1) * nq], acc)
        out_cols.append(acc)
    o_ref[...] = jnp.concatenate(out_cols, axis=1).astype(o_ref.dtype)


def neighbourhood_attention(q, k, v, k_ctx, v_ctx, bias_table):
    t, w = q.shape
    groups, n_blk, _ = _na_window(t // GRID_W)
    first = lambda g: jnp.clip(g - NA_KH // 2 // NA_STEP_ROWS, 0, groups - n_blk)
    blk_tokens = NA_STEP_ROWS * GRID_W
    row_blk = pl.BlockSpec((blk_tokens, w), lambda g: (g, 0))
    key_blks = [pl.BlockSpec((blk_tokens, w), functools.partial(lambda g, i: (first(g) + i, 0), i=i))
                for i in range(n_blk)]
    ctx_blk = pl.BlockSpec(k_ctx.shape, lambda g: (0, 0))
    bias_blk = pl.BlockSpec((1,) + bias_table.shape[1:], lambda g: (g - first(g), 0, 0, 0))
    return pl.pallas_call(
        functools.partial(_na_kernel, scale=NA_HEAD_DIM ** -0.5, n_blk=n_blk),
        grid=(groups,),
        in_specs=[row_blk] + key_blks + key_blks + [ctx_blk, ctx_blk, bias_blk],
        out_specs=row_blk,
        out_shape=jax.ShapeDtypeStruct((t, w), BF16),
        compiler_params=_params("parallel"),
    )(q, *([k] * n_blk), *([v] * n_blk), k_ctx, v_ctx, bias_table)


def time_dft_tables(t):
    a_len = 1 << (int(math.log2(t)) // 2)
    b_len = t // a_len
    ka = np.arange(a_len)[:, None]
    tok = b_len * np.arange(a_len)[None, :]
    ang1 = -2.0 * np.pi * ((ka * (tok[None] + np.arange(b_len)[:, None, None])) % t) / t
    stage1 = np.concatenate([np.cos(ang1), np.sin(ang1)], axis=1) / math.sqrt(t)
    ang2 = 2.0 * np.pi * ((np.arange(b_len)[:, None] * np.arange(b_len)[None, :]) % b_len) / b_len
    c2, s2 = np.cos(ang2), np.sin(ang2)
    stage2 = np.block([[c2, s2], [-s2, c2]])
    return jnp.asarray(stage1, BF16), jnp.asarray(stage2, BF16), a_len, b_len


def _time_dft_kernel(x_ref, m_ref, w2_ref, zre_ref, zim_ref, y_ref, z_ref, *, a_len, b_len):
    i = pl.program_id(1)
    bb = m_ref.shape[0]

    def stage1(jb, carry):
        b = i * bb + jb
        xb = x_ref[pl.ds(b, a_len, stride=b_len), :]
        y = jnp.dot(m_ref[jb], xb.astype(BF16), preferred_element_type=F32)
        row = pl.multiple_of(b * a_len, a_len)
        y_ref[pl.ds(row, a_len), :] = _pack_pair(y[:a_len], y[a_len:])
        return carry

    lax.fori_loop(0, bb, stage1, 0, unroll=16)

    @pl.when(i == pl.num_programs(1) - 1)
    def _():
        def stage2(ka, carry):
            rows = pl.ds(ka, b_len, stride=a_len)
            yre, yim = _unpack_pair(y_ref[rows, :])
            y = jnp.concatenate([yre, yim], axis=0).astype(BF16)
            z = jnp.dot(w2_ref[...], y, preferred_element_type=F32)
            z_ref[rows, :] = _pack_pair(z[:b_len], z[b_len:])
            return carry

        lax.fori_loop(0, a_len, stage2, 0, unroll=16)
        zre, zim = _unpack_pair(z_ref[...])
        zre_ref[...] = zre.astype(zre_ref.dtype)
        zim_ref[...] = zim.astype(zim_ref.dtype)


def time_dft(x):
    t, d = x.shape
    stage1, stage2, a_len, b_len = time_dft_tables(t)
    lanes = 128
    bb = min(16, b_len)
    out = jax.ShapeDtypeStruct((t, d), BF16)
    return pl.pallas_call(
        functools.partial(_time_dft_kernel, a_len=a_len, b_len=b_len),
        grid=(d // lanes, b_len // bb),
        in_specs=[pl.BlockSpec((t, lanes), lambda j, i: (0, j)),
                  pl.BlockSpec((bb, 2 * a_len, a_len), lambda j, i: (i, 0, 0)),
                  pl.BlockSpec(stage2.shape, lambda j, i: (0, 0))],
        out_specs=[pl.BlockSpec((t, lanes), lambda j, i: (0, j))] * 2,
        out_shape=[out, out],
        scratch_shapes=[pltpu.VMEM((t, lanes), U32), pltpu.VMEM((t, lanes), U32)],
        compiler_params=_params("parallel", "arbitrary"),
    )(x, stage1, stage2)


def channel_dft_tables(c):
    ang = 2.0 * np.pi * ((np.arange(c)[:, None] * np.arange(c)[None, :]) % c) / c
    return (jnp.asarray(np.cos(ang) / math.sqrt(c), BF16), jnp.asarray(np.sin(ang) / math.sqrt(c), BF16))


def _ffn_prologue(h, shf_ref, scf_ref, wr_ref, wsg_ref, wsu_ref, wsd_ref, h_ref, f_ref, lg_ref, shared_ref):
    h_ref[...] = h
    f = _rms(h) * (1.0 + scf_ref[...]) + shf_ref[...]
    packed = _pack_rows(f)
    dq = packed.shape[1] // ROW_PARTS
    for i in range(ROW_PARTS):
        f_ref[i] = packed[:, i * dq:(i + 1) * dq]
    fb = f.astype(BF16)
    f_lo = (f - fb.astype(F32)).astype(BF16)
    nt = (((1,), (1,)), ((), ()))
    wr = wr_ref[...]
    wr_hi = wr.astype(BF16)
    wr_lo = (wr - wr_hi.astype(F32)).astype(BF16)
    lg_ref[...] = (lax.dot_general(wr_hi, fb, nt, preferred_element_type=F32)
                   + lax.dot_general(wr_hi, f_lo, nt, preferred_element_type=F32)
                   + lax.dot_general(wr_lo, fb, nt, preferred_element_type=F32))
    hid = (_silu(jnp.dot(fb, wsg_ref[...], preferred_element_type=F32))
           * jnp.dot(fb, wsu_ref[...], preferred_element_type=F32))
    shared_ref[...] = jnp.dot(hid.astype(BF16), wsd_ref[...], preferred_element_type=F32).astype(shared_ref.dtype)


def _gelu_tanh(x):
    return 0.5 * x * (1.0 + jnp.tanh(math.sqrt(2.0 / math.pi) * (x + 0.044715 * (x * x * x))))


def _even_post_kernel(y_ref, na_ref, x_ref, wglu_ref, bglu_ref, wo_ref, gm_ref, *rest):
    g = _gelu_tanh(y_ref[...])
    gate = jax.nn.sigmoid(jnp.dot(g.astype(BF16), wglu_ref[...], preferred_element_type=F32) + bglu_ref[...])
    s5 = (g * gate).astype(BF16)
    w = s5.shape[1]
    mix = (jnp.dot(s5, wo_ref[:w, :], preferred_element_type=F32)
           + jnp.dot(na_ref[...], wo_ref[w:, :], preferred_element_type=F32))
    _ffn_prologue(x_ref[...] + gm_ref[...] * mix, *rest)


def _odd_post_kernel(zre_ref, zim_ref, h_ref_in, cc_ref, sc_ref, wf_ref, bf_ref, gm_ref, *rest):
    c = cc_ref.shape[0]
    parts = []
    for grp in range(zre_ref.shape[1] // c):
        cols = slice(grp * c, (grp + 1) * c)
        parts.append(jnp.dot(zre_ref[:, cols], cc_ref[...], preferred_element_type=F32)
                     + jnp.dot(zim_ref[:, cols], sc_ref[...], preferred_element_type=F32))
    fr = jnp.concatenate(parts, axis=-1).astype(BF16)
    mix = jnp.dot(fr, wf_ref[...], preferred_element_type=F32) + bf_ref[...]
    _ffn_prologue(h_ref_in[...] + gm_ref[...] * mix, *rest)


def _post_call(body, row_inputs, fixed_inputs, t, d, n_exp):
    tm = min(512, t)
    dq = d // 2 // ROW_PARTS
    row = lambda i: (i, 0)
    in_specs = ([pl.BlockSpec((tm, a.shape[1]), row) for a in row_inputs]
                + [pl.BlockSpec(a.shape, functools.partial(lambda i, nd: (0,) * nd, nd=a.ndim))
                   for a in fixed_inputs])
    return pl.pallas_call(
        body,
        grid=(t // tm,),
        in_specs=in_specs,
        out_specs=[pl.BlockSpec((tm, d), row), pl.BlockSpec((ROW_PARTS, tm, dq), lambda i: (0, i, 0)),
                   pl.BlockSpec((n_exp, tm), lambda i: (0, i)), pl.BlockSpec((tm, d), row)],
        out_shape=[jax.ShapeDtypeStruct((t, d), F32), jax.ShapeDtypeStruct((ROW_PARTS, t, dq), U32),
                   jax.ShapeDtypeStruct((n_exp, t), F32), jax.ShapeDtypeStruct((t, d), BF16)],
        compiler_params=_params("parallel"),
    )(*row_inputs, *fixed_inputs)


def _route_kernel(lg_ref, bias_ref, tri_ref, idx_ref, gate_ref, rank_ref, cnt_ref, run_ref):
    @pl.when(pl.program_id(0) == 0)
    def _():
        run_ref[...] = jnp.zeros_like(run_ref)

    scores = jax.nn.sigmoid(lg_ref[...])
    n_exp, tb = scores.shape
    sel = scores + bias_ref[...]
    gsz = n_exp // N_EXPERT_GROUPS
    member = lax.broadcasted_iota(I32, (gsz, tb), 0)
    gscore = []
    for grp in range(N_EXPERT_GROUPS):
        xg = sel[grp * gsz:(grp + 1) * gsz, :]
        m1 = jnp.max(xg, axis=0, keepdims=True)
        first = jnp.min(jnp.where(xg == m1, member, gsz), axis=0, keepdims=True)
        m2 = jnp.max(jnp.where(member == first, -jnp.inf, xg), axis=0, keepdims=True)
        gscore.append(m1 + m2)
    keep_rows = []
    for grp in range(N_EXPERT_GROUPS):
        beaten = jnp.zeros((1, tb), F32)
        for other in range(N_EXPERT_GROUPS):
            if other == grp:
                continue
            wins = (gscore[other] >= gscore[grp]) if other < grp else (gscore[other] > gscore[grp])
            beaten = beaten + jnp.where(wins, 1.0, 0.0)
        keep_rows.append(jnp.broadcast_to(beaten < TOPK_GROUPS, (gsz, tb)))
    masked = jnp.where(jnp.concatenate(keep_rows, axis=0), sel, -jnp.inf)

    expert = lax.broadcasted_iota(I32, (n_exp, tb), 0)
    picks, gates, hots = [], [], []
    chosen = jnp.zeros((n_exp, tb), F32)
    for _ in range(TOP_K):
        m = jnp.max(masked, axis=0, keepdims=True)
        pick = jnp.min(jnp.where(masked == m, expert, n_exp), axis=0, keepdims=True)
        hot = expert == pick
        picks.append(pick)
        hots.append(hot)
        gates.append(jnp.sum(jnp.where(hot, scores, 0.0), axis=0, keepdims=True))
        chosen = jnp.where(hot, 1.0, chosen)
        masked = jnp.where(hot, -jnp.inf, masked)
    total = gates[0]
    for gk in gates[1:]:
        total = total + gk
    ahead = jnp.dot(chosen.astype(BF16), tri_ref[...], preferred_element_type=F32) + run_ref[...]
    for k in range(TOP_K):
        idx_ref[k:k + 1, :] = picks[k]
        gate_ref[k:k + 1, :] = ROUTED_SCALE * gates[k] / total
        rank_ref[k:k + 1, :] = jnp.sum(jnp.where(hots[k], ahead, 0.0), axis=0, keepdims=True).astype(I32)
    run_ref[...] = run_ref[...] + jnp.sum(chosen, axis=1, keepdims=True)
    cnt_ref[...] = jnp.broadcast_to(run_ref[...], cnt_ref.shape)


def route(logits_t, router_bias):
    n_exp, t = logits_t.shape
    tb = min(512, t)
    tri = jnp.asarray(np.triu(np.ones((tb, tb), np.float32), k=1), BF16)
    tok = lambda i: (0, i)
    idx, gate, rank, cnt = pl.pallas_call(
        _route_kernel,
        grid=(t // tb,),
        in_specs=[pl.BlockSpec((n_exp, tb), tok), pl.BlockSpec((n_exp, 1), lambda i: (0, 0)),
                  pl.BlockSpec((tb, tb), lambda i: (0, 0))],
        out_specs=[pl.BlockSpec((TOP_K, tb), tok)] * 3 + [pl.BlockSpec((n_exp, 128), lambda i: (0, 0))],
        out_shape=[jax.ShapeDtypeStruct((TOP_K, t), I32), jax.ShapeDtypeStruct((TOP_K, t), F32),
                   jax.ShapeDtypeStruct((TOP_K, t), I32), jax.ShapeDtypeStruct((n_exp, 128), F32)],
        scratch_shapes=[pltpu.VMEM((n_exp, 1), F32)],
        compiler_params=_params("arbitrary"),
    )(logits_t, router_bias.astype(F32).reshape(n_exp, 1), tri)
    return idx, gate, rank, cnt[:, 0].astype(I32)


def dispatch_plan(idx, rank, counts, n_blocks):
    n_exp = counts.shape[0]
    shift = EXPERT_ROWS.bit_length() - 1
    assert EXPERT_ROWS == 1 << shift
    padded = ((counts + EXPERT_ROWS - 1) >> shift) << shift
    pad_end = jnp.cumsum(padded)
    pad_start = pad_end - padded
    experts_iota = jnp.arange(n_exp, dtype=I32)
    dest = jnp.sum(jnp.where(idx[..., None] == experts_iota, pad_start, 0), axis=-1) + rank
    n_valid = (pad_end[-1] >> shift).astype(I32)
    blk = jnp.minimum(jnp.arange(n_blocks, dtype=I32), n_valid - 1)
    blk_e = jnp.sum((pad_end[None, :] <= blk[:, None] * EXPERT_ROWS).astype(I32), axis=1)
    return dest.astype(I32), jnp.minimum(blk_e, n_exp - 1).astype(I32), n_valid.reshape(1)


def _sc_mesh():
    return plsc.VectorSubcoreMesh(core_axis_name="core", subcore_axis_name="subcore")


def dispatch(f, dest, cap):
    t, w = f.shape
    n_choice = dest.shape[0]

    @functools.partial(pl.kernel, out_type=jax.ShapeDtypeStruct((cap, w), f.dtype), mesh=_sc_mesh(),
                       scratch_types=[pltpu.SemaphoreType.DMA])
    def scatter_rows(x_hbm, i_hbm, o_hbm, sem):
        def body(x_vmem, i_vmem):
            copies = [pltpu.async_copy(x_vmem, o_hbm.at[i_vmem.at[k]], sem) for k in range(n_choice)]
            for cp in copies:
                cp.wait()

        pltpu.emit_pipeline(
            body,
            grid=(t // SC_WINDOW,),
            in_specs=[pl.BlockSpec((SC_WINDOW, w), lambda i: (i, 0)),
                      pl.BlockSpec((n_choice, SC_WINDOW), lambda i: (0, i))],
            out_specs=[],
            core_axis_name=("core", "subcore"),
            dimension_semantics=(pltpu.PARALLEL,),
        )(x_hbm, i_hbm)

    return scatter_rows(f, dest)


def gather_rows(rows, index_row):
    n = index_row.shape[1]
    w = rows.shape[1]

    piece = SC_WINDOW // SC_GATHER_SPLIT

    @functools.partial(pl.kernel, out_type=jax.ShapeDtypeStruct((n, w), rows.dtype), mesh=_sc_mesh(),
                       scratch_types=[pltpu.SemaphoreType.DMA])
    def gather(y_hbm, i_hbm, o_hbm, sem):
        def body(i_vmem, o_vmem):
            copies = [pltpu.async_copy(y_hbm.at[i_vmem.at[0, pl.ds(j * piece, piece)]],
                                       o_vmem.at[pl.ds(j * piece, piece)], sem)
                      for j in range(SC_GATHER_SPLIT)]
            for cp in copies:
                cp.wait()

        pltpu.emit_pipeline(
            body,
            grid=(n // SC_WINDOW,),
            in_specs=[pl.BlockSpec((1, SC_WINDOW), lambda i: (0, i))],
            out_specs=[pl.BlockSpec((SC_WINDOW, w), lambda i: (i, 0))],
            core_axis_name=("core", "subcore"),
            dimension_semantics=(pltpu.PARALLEL,),
        )(i_hbm, o_hbm)

    return gather(rows, index_row)


def _experts_kernel(be_ref, x_ref, wg_ref, wu_ref, wd_ref, y_ref, wg_bf, wu_bf, wd_bf):
    b = pl.program_id(0)

    @pl.when((b == 0) | (be_ref[b] != be_ref[jnp.maximum(b - 1, 0)]))
    def _():
        wg_bf[...] = wg_ref[0, 0].astype(BF16)
        wu_bf[...] = wu_ref[0, 0].astype(BF16)
        wd_bf[...] = wd_ref[0, 0].astype(BF16)

    x = _unpack_rows(jnp.concatenate([x_ref[i] for i in range(ROW_PARTS)], axis=1)).astype(BF16)
    hid = (_silu(jnp.dot(x, wg_bf[...], preferred_element_type=F32))
           * jnp.dot(x, wu_bf[...], preferred_element_type=F32))
    packed = _pack_rows(jnp.dot(hid.astype(BF16), wd_bf[...], preferred_element_type=F32))
    dq = packed.shape[1] // ROW_PARTS
    for i in range(ROW_PARTS):
        y_ref[i] = packed[:, i * dq:(i + 1) * dq]


def experts(rows, blk_e, n_valid, layer, w_gate, w_up, w_down):
    parts, cap, dq = rows.shape
    _, _, d, ff = w_gate.shape
    blk = pl.BlockSpec((parts, EXPERT_ROWS, dq), lambda b, be: (0, b, 0))
    return pl.pallas_call(
        _experts_kernel,
        grid_spec=pltpu.PrefetchScalarGridSpec(
            num_scalar_prefetch=1,
            grid=(n_valid[0],),
            in_specs=[blk,
                      pl.BlockSpec((1, 1, d, ff), lambda b, be: (layer, be[b], 0, 0)),
                      pl.BlockSpec((1, 1, d, ff), lambda b, be: (layer, be[b], 0, 0)),
                      pl.BlockSpec((1, 1, ff, d), lambda b, be: (layer, be[b], 0, 0))],
            out_specs=blk,
            scratch_shapes=[pltpu.VMEM((d, ff), BF16), pltpu.VMEM((d, ff), BF16), pltpu.VMEM((ff, d), BF16)],
        ),
        out_shape=jax.ShapeDtypeStruct(rows.shape, U32),
        compiler_params=_params("arbitrary"),
    )(blk_e, rows, w_gate, w_up, w_down)


def _combine_kernel(gate_ref, h_ref, shared_ref, gf_ref, sh_ref, sc_ref, *rest):
    y_refs, outs = rest[:TOP_K], rest[TOP_K:]
    routed = None
    for k in range(TOP_K):
        y = _unpack_rows(jnp.concatenate([y_refs[k][i, 0] for i in range(ROW_PARTS)], axis=1))
        routed = gate_ref[:, k:k + 1] * y if routed is None else routed + gate_ref[:, k:k + 1] * y
    h = h_ref[...] + gf_ref[...] * (routed + shared_ref[...].astype(F32))
    outs[0][...] = h
    if len(outs) > 1:
        outs[1][...] = _rms(h) * (1.0 + sc_ref[...]) + sh_ref[...]


def combine(y_rows, dest, gate_tk, h, shared, gate_ffn, next_shift, next_scale, with_next):
    t, d = h.shape
    parts, cap, dq = y_rows.shape
    n_out = 2 if with_next else 1
    n_choice = dest.shape[0]
    flat = (dest[None] + (jnp.arange(parts, dtype=I32) * cap)[:, None, None]).reshape(1, parts * n_choice * t)
    picked = gather_rows(y_rows.reshape(parts * cap, dq), flat).reshape(parts, n_choice, t, dq)
    tm = min(256, t)
    row = lambda i: (i, 0)
    fixed = lambda i: (0, 0)
    choice = [pl.BlockSpec((parts, 1, tm, dq), functools.partial(lambda i, k: (0, k, i, 0), k=k))
              for k in range(n_choice)]
    return pl.pallas_call(
        _combine_kernel,
        grid=(t // tm,),
        in_specs=[pl.BlockSpec((tm, n_choice), row), pl.BlockSpec((tm, d), row), pl.BlockSpec((tm, d), row),
                  pl.BlockSpec((1, d), fixed), pl.BlockSpec((1, d), fixed), pl.BlockSpec((1, d), fixed)] + choice,
        out_specs=[pl.BlockSpec((tm, d), row)] * n_out,
        out_shape=[jax.ShapeDtypeStruct((t, d), F32)] * n_out,
        compiler_params=_params("parallel"),
    )(gate_tk, h, shared, gate_ffn, next_shift, next_scale, *([picked] * n_choice))


def moe_tail(h, f_rows, logits_t, shared, router_bias, layer, w_gate, w_up, w_down, gate_ffn,
             next_shift, next_scale, with_next):
    t, d = h.shape
    parts, _, dq = f_rows.shape
    n_exp = w_gate.shape[1]
    n_blocks = -(-(t * TOP_K + n_exp * (EXPERT_ROWS - 1)) // EXPERT_ROWS)
    cap = n_blocks * EXPERT_ROWS
    idx, gate, rank, counts = route(logits_t, router_bias)
    dest, blk_e, n_valid = dispatch_plan(idx, rank, counts, n_blocks)
    dest_all = jnp.concatenate([dest + i * cap for i in range(parts)], axis=1)
    rows = dispatch(f_rows.reshape(parts * t, dq), dest_all, parts * cap).reshape(parts, cap, dq)
    y_rows = experts(rows, blk_e, n_valid, layer, w_gate, w_up, w_down)
    return combine(y_rows, dest, gate.T, h, shared, gate_ffn, next_shift, next_scale, with_next)


def kernel(x, c, ctx, c_ctx, w_ada, b_ada, w_in, s5_lam_re, s5_lam_im, s5_log_dt, s5_b_re, s5_b_im,
           s5_c_re, s5_c_im, s5_d, s5_w_glu, s5_b_glu, na_q_gain, na_k_gain, na_rpb, w_mix_out,
           w_fourier_out, b_fourier_out, w_router, router_bias, w_exp_gate, w_exp_up, w_exp_down,
           w_sh_gate, w_sh_up, w_sh_down):
    bsz, t, d = x.shape
    assert bsz == 1 and w_ada.shape[0] == 2
    n_exp = w_router.shape[2]
    s5w = s5_w_glu.shape[1]
    naw = w_in.shape[2] - s5w
    naw //= 3
    heads = naw // NA_HEAD_DIM

    cond8 = jnp.concatenate([c[:1].astype(F32), c_ctx.astype(F32)[None], jnp.zeros((6, d), F32)], axis=0)
    ada = adaln_all(cond8, w_ada, b_ada)
    mod = lambda layer, who, j: ada[layer, who:who + 1, j * d:(j + 1) * d]

    def ffn_weights(i):
        return (mod(i, 0, 3), mod(i, 0, 4), jnp.transpose(w_router[i]).astype(F32),
                w_sh_gate[i].astype(BF16), w_sh_up[i].astype(BF16), w_sh_down[i].astype(BF16))

    h0 = x[0]
    seg = jnp.asarray(np.kron(np.eye(heads), np.ones((NA_HEAD_DIM, NA_HEAD_DIM))), BF16)
    w_in_b = w_in[0].astype(BF16)
    qg = jnp.tile(na_q_gain[0].astype(F32), heads)[None]
    kg = jnp.tile(na_k_gain[0].astype(F32), heads)[None]
    u_c, _, k_c, v_c = in_projection(ctx[0], mod(0, 1, 0), mod(0, 1, 1), w_in_b, seg, qg, kg, s5w, naw)
    u_l, q_l, k_l, v_l = in_projection(h0, mod(0, 0, 0), mod(0, 0, 1), w_in_b, seg, qg, kg, s5w, naw)
    mats = s5_matrices(s5_lam_re[0], s5_lam_im[0], s5_log_dt[0], s5_b_re[0], s5_b_im[0],
                       s5_c_re[0], s5_c_im[0], s5_d[0])
    y_s5 = s5_mixer(u_c, u_l, mats)
    na = neighbourhood_attention(q_l, k_l, v_l, k_c, v_c, na_bias_table(na_rpb[0], t // GRID_W))
    h1, f1, lg1, sh1 = _post_call(
        _even_post_kernel, [y_s5, na, h0],
        [s5_w_glu[0].astype(BF16), s5_b_glu[0].astype(F32)[None], w_mix_out[0].astype(BF16), mod(0, 0, 2),
         *ffn_weights(0)], t, d, n_exp)
    h2, a1 = moe_tail(h1, f1, lg1, sh1, router_bias[0], 0, w_exp_gate, w_exp_up, w_exp_down,
                      mod(0, 0, 5), mod(1, 0, 0), mod(1, 0, 1), True)

    zre, zim = time_dft(a1)
    cc, sc = channel_dft_tables(d // FOURIER_GROUPS)
    h3, f3, lg3, sh3 = _post_call(
        _odd_post_kernel, [zre, zim, h2],
        [cc, sc, w_fourier_out[0].astype(BF16), b_fourier_out[0].astype(F32)[None], mod(1, 0, 2),
         *ffn_weights(1)], t, d, n_exp)
    zero_row = jnp.zeros((1, d), F32)
    (out,) = moe_tail(h3, f3, lg3, sh3, router_bias[1], 1, w_exp_gate, w_exp_up, w_exp_down,
                      mod(1, 0, 5), zero_row, zero_row, False)
    return out[None]
```

```python
import functools
import math

import numpy as np
import jax
import jax.numpy as jnp
from jax import lax
from jax.experimental import pallas as pl
from jax.experimental.pallas import tpu as pltpu
from jax.experimental.pallas import tpu_sc as plsc

F32 = jnp.float32
BF16 = jnp.bfloat16
I32 = jnp.int32
U32 = jnp.uint32
HIGHEST = lax.Precision.HIGHEST

MXU_DEPTH = 256
LANES = 128
GRID_W = 64
NORM_EPS = 1e-6
S5_GROUP = 16
S5_LAMBDA_RE_MAX = -1e-4
S5_CHUNK = 16
S5_TILE = 128
NA_HEAD_DIM = 64
NA_KH = 8
NA_KW = 16
NA_STEP_ROWS = 2
FOURIER_GROUPS = 4
N_EXPERT_GROUPS = 8
TOPK_GROUPS = 4
TOP_K = 8
ROUTED_SCALE = 2.5
EXPERT_ROWS = 2048
ROW_PARTS = 2
SC_WINDOW = 128
SC_GATHER_SPLIT = 4
NEG_BIG = -1e30

VMEM_LIMIT_BYTES = 56 * 1024 * 1024


def _params(*sem):
    return pltpu.CompilerParams(dimension_semantics=sem or None,
                                vmem_limit_bytes=VMEM_LIMIT_BYTES)


def _rms(x):
    return x * lax.rsqrt(jnp.mean(x * x, axis=-1, keepdims=True) + NORM_EPS)


def _silu(x):
    return x * jax.nn.sigmoid(x)


def _pack_pair(lo, hi):
    lo = lax.bitcast_convert_type(lo.astype(BF16).astype(F32), U32)
    hi = lax.bitcast_convert_type(hi.astype(BF16).astype(F32), U32)
    return (hi & jnp.uint32(0xFFFF0000)) | (lo >> 16)


def _unpack_pair(w):
    return (lax.bitcast_convert_type(w << 16, F32), lax.bitcast_convert_type(w & jnp.uint32(0xFFFF0000), F32))


def _pack_rows(x):
    n = x.shape[1] // 2
    return _pack_pair(x[:, :n], x[:, n:])


def _unpack_rows(w):
    return jnp.concatenate(_unpack_pair(w), axis=1)


def _ada_kernel(c_ref, w_ref, b_ref, o_ref):
    o_ref[0] = jnp.dot(_silu(c_ref[...]), w_ref[0], preferred_element_type=F32,
                       precision=HIGHEST) + b_ref[0]


def adaln_all(cond8, w_ada, b_ada):
    n_layers, d, n6 = w_ada.shape
    tn = n6 // 4
    return pl.pallas_call(
        _ada_kernel,
        grid=(n_layers, n6 // tn),
        in_specs=[pl.BlockSpec((8, d), lambda l, j: (0, 0)),
                  pl.BlockSpec((1, d, tn), lambda l, j: (l, 0, j)),
                  pl.BlockSpec((1, 1, tn), lambda l, j: (l, 0, j))],
        out_specs=pl.BlockSpec((1, 8, tn), lambda l, j: (l, 0, j)),
        out_shape=jax.ShapeDtypeStruct((n_layers, 8, n6), F32),
        compiler_params=_params("parallel", "parallel"),
    )(cond8, w_ada, b_ada.reshape(n_layers, 1, n6))


def _inproj_kernel(x_ref, sh_ref, sc_ref, w_ref, seg_ref, qg_ref, kg_ref,
                   u_ref, q_ref, k_ref, v_ref):
    a = _rms(x_ref[...]) * (1.0 + sc_ref[...]) + sh_ref[...]
    z = jnp.dot(a.astype(BF16), w_ref[...], preferred_element_type=F32)
    s5w = u_ref.shape[1]
    naw = q_ref.shape[1]

    def head_norm(t, gain):
        ss = jnp.dot((t * t).astype(BF16), seg_ref[...], preferred_element_type=F32)
        return t * lax.rsqrt(ss * (1.0 / NA_HEAD_DIM) + NORM_EPS) * gain

    u_ref[...] = z[:, :s5w]
    q_ref[...] = head_norm(z[:, s5w:s5w + naw], qg_ref[...]).astype(BF16)
    k_ref[...] = head_norm(z[:, s5w + naw:s5w + 2 * naw], kg_ref[...]).astype(BF16)
    v_ref[...] = z[:, s5w + 2 * naw:].astype(BF16)


def in_projection(x, shift, scale, w_in_bf16, seg_ones, q_gain_row, k_gain_row, s5w, naw):
    t, d = x.shape
    tm = min(512, t)
    row = lambda i: (i, 0)
    fixed = lambda i: (0, 0)
    return pl.pallas_call(
        _inproj_kernel,
        grid=(t // tm,),
        in_specs=[pl.BlockSpec((tm, d), row),
                  pl.BlockSpec((1, d), fixed), pl.BlockSpec((1, d), fixed),
                  pl.BlockSpec(w_in_bf16.shape, fixed),
                  pl.BlockSpec(seg_ones.shape, fixed),
                  pl.BlockSpec((1, naw), fixed), pl.BlockSpec((1, naw), fixed)],
        out_specs=[pl.BlockSpec((tm, s5w), row), pl.BlockSpec((tm, naw), row),
                   pl.BlockSpec((tm, naw), row), pl.BlockSpec((tm, naw), row)],
        out_shape=[jax.ShapeDtypeStruct((t, s5w), F32)] + [jax.ShapeDtypeStruct((t, naw), BF16)] * 3,
        compiler_params=_params("parallel"),
    )(x, shift, scale, w_in_bf16, seg_ones, q_gain_row, k_gain_row)


def s5_matrices(lam_re, lam_im, log_dt, b_re, b_im, c_re, c_im, d_skip):
    L = S5_CHUNK
    c = S5_GROUP
    taus = jnp.arange(L + 1, dtype=F32)

    def direction(i):
        lam = lax.complex(jnp.minimum(lam_re[i].astype(F32), S5_LAMBDA_RE_MAX), lam_im[i].astype(F32))
        ldt = lam * jnp.exp(log_dt[i].astype(F32))[:, None]
        lam_bar = jnp.exp(ldt)
        b_bar = ((lam_bar - 1.0) / lam)[..., None] * lax.complex(b_re[i].astype(F32), b_im[i].astype(F32))
        cc = lax.complex(c_re[i].astype(F32), c_im[i].astype(F32))
        powers = jnp.exp(ldt[None] * taus[:, None, None])
        resp = jnp.real(jnp.einsum('gcp,tgp,gpd->gctd', cc, powers[:L], b_bar, precision=HIGHEST))
        return jnp.transpose(powers, (1, 2, 0)), b_bar, cc, resp

    pw_f, bb_f, cc_f, k_f = direction(0)
    pw_b, bb_b, cc_b, k_b = direction(1)
    g, p = pw_f.shape[:2]
    lag0 = k_f[:, :, :1] + k_b[:, :, :1] + (jnp.eye(c, dtype=F32)[None, :, None, :] * d_skip.astype(F32)[:, :, None, None])
    by_lag = jnp.concatenate([jnp.flip(k_f[:, :, 1:], axis=2), lag0, k_b[:, :, 1:]], axis=2).reshape(g, c, (2 * L - 1) * c)
    toep_t = jnp.stack([by_lag[:, :, (L - 1 - l) * c:(2 * L - 1 - l) * c] for l in range(L)], axis=1)
    toep_t = toep_t.reshape(g, L * c, L * c)

    def state_in(pw_by_s, b_bar):
        return (pw_by_s[:, :, :, None] * b_bar[:, :, None, :]).reshape(g, p, L * c)

    wf = state_in(jnp.flip(pw_f[:, :, :L], axis=2), bb_f)
    wb = state_in(pw_b[:, :, :L], bb_b)
    w_state_t = jnp.concatenate([jnp.real(wf), jnp.imag(wf), jnp.imag(wf), jnp.real(wf),
                                 jnp.real(wb), jnp.imag(wb), jnp.imag(wb), jnp.real(wb)], axis=1)

    def state_out(pw_by_l, cc):
        return (jnp.transpose(pw_by_l, (0, 2, 1))[:, :, None, :] * cc[:, None, :, :]).reshape(g, L * c, p)

    rf = state_out(pw_f[:, :, 1:], cc_f)
    rb = state_out(jnp.flip(pw_b[:, :, 1:], axis=2), cc_b)
    r_state_t = jnp.concatenate([jnp.real(rf), -jnp.imag(rf), jnp.real(rb), -jnp.imag(rb)], axis=-1)

    def mult(a):
        ar, ai = jnp.real(a), jnp.imag(a)
        return jnp.stack([jnp.concatenate([ar, ar], -1), jnp.concatenate([-ai, ai], -1),
                          jnp.concatenate([ai, -ai], -1)])

    return (toep_t.astype(BF16), w_state_t.astype(BF16), r_state_t.astype(BF16),
            mult(pw_f[:, :, L]), mult(pw_b[:, :, L]))


def _s5_pack_kernel(*refs):
    u_refs, (wt_ref, ut_ref, f1_ref, f2_ref, b1_ref, b2_ref) = refs[:-6], refs[-6:]
    L = S5_CHUNK
    g, lc, nck = ut_ref.shape
    c = lc // L
    gs = g // len(u_refs)
    for s in range(L):
        for j, u_ref in enumerate(u_refs):
            step_s = u_ref[pl.ds(s, nck, stride=L), :]
            ut_ref[j * gs:(j + 1) * gs, s * c:(s + 1) * c, :] = (
                jnp.transpose(step_s).astype(BF16).reshape(gs, c, nck))
    n = f1_ref.shape[2]
    for gi in range(g):
        inc = jnp.dot(wt_ref[gi], ut_ref[gi], preferred_element_type=F32)
        for j, ref in enumerate((f1_ref, f2_ref, b1_ref, b2_ref)):
            ref[:, gi, :] = jnp.transpose(inc[j * n:(j + 1) * n, :])


def s5_pack(u, wt_state):
    t, w = u.shape
    g, n4, lc = wt_state.shape
    nc = t // S5_CHUNK
    tile = min(S5_TILE, nc)
    inc = jax.ShapeDtypeStruct((nc, g, n4 // 4), F32)
    inc_blk = pl.BlockSpec((tile, g, n4 // 4), lambda i: (i, 0, 0))
    return pl.pallas_call(
        _s5_pack_kernel,
        grid=(nc // tile,),
        in_specs=[pl.BlockSpec((tile * S5_CHUNK, LANES), functools.partial(lambda i, j: (i, j), j=j))
                  for j in range(w // LANES)] + [pl.BlockSpec(wt_state.shape, lambda i: (0, 0, 0))],
        out_specs=[pl.BlockSpec((g, lc, tile), lambda i: (0, 0, i))] + [inc_blk] * 4,
        out_shape=[jax.ShapeDtypeStruct((g, lc, nc), BF16)] + [inc] * 4,
        compiler_params=_params("parallel"),
    )(*([u] * (w // LANES)), wt_state)


def _s5_scan_kernel(s1_ref, s2_ref, m_ref, init_ref, x_ref, last_ref, v1_ref, v2_ref, *, reverse):
    @pl.when(pl.program_id(0) == 0)
    def _():
        v1_ref[...] = init_ref[0]
        v2_ref[...] = init_ref[1]

    a1, a2, a3 = m_ref[0], m_ref[1], m_ref[2]
    cb = s1_ref.shape[0]

    def body(j, carry):
        v1, v2 = carry
        jj = cb - 1 - j if reverse else j
        x_ref[jj] = v1
        return (a1 * v1 + a2 * v2 + s1_ref[jj], a1 * v2 + a3 * v1 + s2_ref[jj])

    v1, v2 = lax.fori_loop(0, cb, body, (v1_ref[...], v2_ref[...]))
    v1_ref[...] = v1
    v2_ref[...] = v2
    last_ref[...] = v1


def s5_chunk_scan(s1, s2, mult, init, reverse):
    nc, g, n = s1.shape
    cb = min(S5_TILE, nc)
    nb = nc // cb
    blk = (lambda i: (nb - 1 - i, 0, 0)) if reverse else (lambda i: (i, 0, 0))
    return pl.pallas_call(
        functools.partial(_s5_scan_kernel, reverse=reverse),
        grid=(nb,),
        in_specs=[pl.BlockSpec((cb, g, n), blk), pl.BlockSpec((cb, g, n), blk),
                  pl.BlockSpec((3, g, n), lambda i: (0, 0, 0)), pl.BlockSpec((2, g, n), lambda i: (0, 0, 0))],
        out_specs=[pl.BlockSpec((cb, g, n), blk), pl.BlockSpec((g, n), lambda i: (0, 0))],
        out_shape=[jax.ShapeDtypeStruct((nc, g, n), F32), jax.ShapeDtypeStruct((g, n), F32)],
        scratch_shapes=[pltpu.VMEM((g, n), F32), pltpu.VMEM((g, n), F32)],
        compiler_params=_params("arbitrary"),
    )(s1, s2, mult, init)


def _s5_readout_kernel(ut_ref, tt_ref, rt_ref, xf_ref, xb_ref, y_ref, yt_ref, *slab_refs):
    L = S5_CHUNK
    g, lc, nck = ut_ref.shape
    c = lc // L
    gs = g // len(slab_refs)
    for gi in range(g):
        xin_t = jnp.concatenate([jnp.transpose(xf_ref[:, gi, :]), jnp.transpose(xb_ref[:, gi, :])], axis=0)
        yt_ref[gi] = (jnp.dot(tt_ref[gi], ut_ref[gi], preferred_element_type=F32)
                      + jnp.dot(rt_ref[gi], xin_t.astype(BF16), preferred_element_type=F32))
    for j, slab in enumerate(slab_refs):
        for l in range(L):
            step_l = yt_ref[j * gs:(j + 1) * gs, l * c:(l + 1) * c, :].reshape(gs * c, nck)
            slab[pl.ds(l, nck, stride=L), :] = jnp.transpose(step_l)
        y_ref[:, j * gs * c:(j + 1) * gs * c] = slab[...]


def s5_readout(ut, toep_t, r_state_t, xin_f, xin_b):
    g, lc, nc = ut.shape
    n = xin_f.shape[2]
    tile = min(S5_TILE, nc)
    fixed = lambda i: (0, 0, 0)
    state_blk = pl.BlockSpec((tile, g, n), lambda i: (i, 0, 0))
    return pl.pallas_call(
        _s5_readout_kernel,
        grid=(nc // tile,),
        in_specs=[pl.BlockSpec((g, lc, tile), lambda i: (0, 0, i)),
                  pl.BlockSpec(toep_t.shape, fixed), pl.BlockSpec(r_state_t.shape, fixed), state_blk, state_blk],
        out_specs=pl.BlockSpec((tile * S5_CHUNK, g * lc // S5_CHUNK), lambda i: (i, 0)),
        out_shape=jax.ShapeDtypeStruct((nc * S5_CHUNK, g * lc // S5_CHUNK), F32),
        scratch_shapes=[pltpu.VMEM((g, lc, tile), F32)]
        + [pltpu.VMEM((tile * S5_CHUNK, LANES), F32)] * (g * lc // S5_CHUNK // LANES),
        compiler_params=_params("parallel"),
    )(ut, toep_t, r_state_t, xin_f, xin_b)


def s5_mixer(u_ctx, u_lat, mats):
    toep_t, wt_state, r_state_t, mult_f, mult_b = mats
    L = S5_CHUNK
    g, n = mult_f.shape[1:]
    halves = lambda v: jnp.stack([v, jnp.roll(v, n // 2, axis=-1)])
    n_ctx = u_ctx.shape[0] // L
    ctx_chunks = -(-(n_ctx + 1) // S5_TILE) * S5_TILE
    ctx_pad = jnp.pad(u_ctx, ((0, ctx_chunks * L - u_ctx.shape[0]), (0, 0)))
    _, cf1, cf2, cb1, cb2 = s5_pack(ctx_pad, wt_state)
    zero = jnp.zeros((2, g, n), F32)
    ctx_f, _ = s5_chunk_scan(cf1, cf2, mult_f, zero, False)
    _, ctx_b_last = s5_chunk_scan(cb1, cb2, mult_b, zero, True)
    ut, f1, f2, b1, b2 = s5_pack(u_lat, wt_state)
    xin_f, _ = s5_chunk_scan(f1, f2, mult_f, halves(ctx_f[n_ctx]), False)
    xin_b, _ = s5_chunk_scan(b1, b2, mult_b, halves(ctx_b_last), True)
    return s5_readout(ut, toep_t, r_state_t, xin_f, xin_b)


def _na_window(rows):
    step = NA_STEP_ROWS
    assert rows % step == 0 and (NA_KH // 2) % step == 0 and rows >= NA_KH + step
    groups = rows // step
    n_blk = (NA_KH + step) // step
    first = np.clip(np.arange(groups) - NA_KH // 2 // step, 0, groups - n_blk)
    return groups, n_blk, first


def na_bias_table(rpb, rows):
    step = NA_STEP_ROWS
    groups, n_blk, first = _na_window(rows)
    win = n_blk * step
    q_col = np.arange(GRID_W)
    col_start = np.clip(q_col - NA_KW // 2, 0, GRID_W - NA_KW)
    key_col = np.arange(GRID_W)
    off = key_col[None, :] - col_start[:, None]
    valid_col = (off >= 0) & (off < NA_KW)
    rel_col = np.clip(key_col[None, :] - q_col[:, None] + NA_KW - 1, 0, 2 * NA_KW - 2)
    q_row = step * np.arange(groups)[:, None, None] + np.arange(step)[None, :, None]
    key_row = step * first[:, None, None] + np.arange(win)[None, None, :]
    row0 = np.clip(q_row - NA_KH // 2, 0, rows - NA_KH)
    rel_row = np.where((key_row >= row0) & (key_row < row0 + NA_KH), key_row - q_row + NA_KH - 1, -1)
    variant = np.arange(groups) - first
    reps = [int(np.argmax(variant == d)) for d in range(n_blk)]
    assert all((rel_row[g] == rel_row[reps[variant[g]]]).all() for g in range(groups))
    rel_row = rel_row[reps]
    pick_col = jnp.asarray(np.arange(2 * NA_KW - 1)[:, None, None] == rel_col[None], F32)
    tiles = jnp.where(valid_col[None, None], jnp.einsum('hrx,xck->rhck', rpb.astype(F32), pick_col, precision=HIGHEST),
                      NEG_BIG)
    tiles = jnp.concatenate([tiles, jnp.full_like(tiles[:1], NEG_BIG)], axis=0)
    return jnp.stack([jnp.concatenate([jnp.concatenate([tiles[int(rel_row[v, j, i])] for i in range(win)], axis=-1)
                                       for j in range(step)], axis=1) for v in range(n_blk)])


def _na_kernel(*refs, scale, n_blk):
    q_ref = refs[0]
    k_refs = refs[1:1 + n_blk]
    v_refs = refs[1 + n_blk:1 + 2 * n_blk]
    kc_ref, vc_ref, b_ref, o_ref = refs[1 + 2 * n_blk:]
    hd = NA_HEAD_DIM
    width = q_ref.shape[1]
    span = min(MXU_DEPTH, width)
    nt = (((1,), (1,)), ((), ()))
    q = q_ref[...] * scale
    kk = jnp.concatenate([r[...] for r in k_refs], axis=0)
    vv = jnp.concatenate([r[...] for r in v_refs], axis=0)
    kc, vc = kc_ref[...], vc_ref[...]
    nq = q.shape[0]
    per = span // hd
    lane = lax.broadcasted_iota(I32, (nq, span), 1)
    own = [(lane >= j * hd) & (lane < (j + 1) * hd) for j in range(per)]
    out_cols = []
    for c0 in range(0, width, span):
        cols = slice(c0, c0 + span)
        qs = jnp.concatenate([jnp.where(own[j], q[:, cols], jnp.zeros_like(q[:, cols])) for j in range(per)], axis=0)
        h0 = c0 // hd
        bias = b_ref[0, h0:h0 + per].reshape(per * nq, kk.shape[0])
        s = lax.dot_general(qs, kk[:, cols], nt, preferred_element_type=F32) + bias
        sc = lax.dot_general(qs, kc[:, cols], nt, preferred_element_type=F32)
        m = jnp.maximum(jnp.max(s, axis=-1, keepdims=True), jnp.max(sc, axis=-1, keepdims=True))
        p = jnp.exp(s - m)
        pc = jnp.exp(sc - m)
        den = jnp.sum(p, axis=-1, keepdims=True) + jnp.sum(pc, axis=-1, keepdims=True)
        o = (jnp.dot(p.astype(BF16), vv[:, cols], preferred_element_type=F32)
             + jnp.dot(pc.astype(BF16), vc[:, cols], preferred_element_type=F32)) / den
        acc = jnp.zeros((nq, span), F32)
        for j in range(per):
            acc = jnp.where(own[j], o[j * nq:(j + 1) * nq], acc)
        out_cols.append(acc)
    o_ref[...] = jnp.concatenate(out_cols, axis=1).astype(o_ref.dtype)


def neighbourhood_attention(q, k, v, k_ctx, v_ctx, bias_table):
    t, w = q.shape
    groups, n_blk, _ = _na_window(t // GRID_W)
    first = lambda g: jnp.clip(g - NA_KH // 2 // NA_STEP_ROWS, 0, groups - n_blk)
    blk_tokens = NA_STEP_ROWS * GRID_W
    row_blk = pl.BlockSpec((blk_tokens, w), lambda g: (g, 0))
    key_blks = [pl.BlockSpec((blk_tokens, w), functools.partial(lambda g, i: (first(g) + i, 0), i=i))
                for i in range(n_blk)]
    ctx_blk = pl.BlockSpec(k_ctx.shape, lambda g: (0, 0))
    bias_blk = pl.BlockSpec((1,) + bias_table.shape[1:], lambda g: (g - first(g), 0, 0, 0))
    return pl.pallas_call(
        functools.partial(_na_kernel, scale=NA_HEAD_DIM ** -0.5, n_blk=n_blk),
        grid=(groups,),
        in_specs=[row_blk] + key_blks + key_blks + [ctx_blk, ctx_blk, bias_blk],
        out_specs=row_blk,
        out_shape=jax.ShapeDtypeStruct((t, w), BF16),
        compiler_params=_params("parallel"),
    )(q, *([k] * n_blk), *([v] * n_blk), k_ctx, v_ctx, bias_table)


def time_dft_tables(t):
    a_len = 1 << (int(math.log2(t)) // 2)
    b_len = t // a_len
    ka = np.arange(a_len)[:, None]
    tok = b_len * np.arange(a_len)[None, :]
    ang1 = -2.0 * np.pi * ((ka * (tok[None] + np.arange(b_len)[:, None, None])) % t) / t
    stage1 = np.concatenate([np.cos(ang1), np.sin(ang1)], axis=1) / math.sqrt(t)
    ang2 = 2.0 * np.pi * ((np.arange(b_len)[:, None] * np.arange(b_len)[None, :]) % b_len) / b_len
    c2, s2 = np.cos(ang2), np.sin(ang2)
    stage2 = np.block([[c2, s2], [-s2, c2]])
    return jnp.asarray(stage1, BF16), jnp.asarray(stage2, BF16), a_len, b_len


def _time_dft_kernel(x_ref, m_ref, w2_ref, zre_ref, zim_ref, y_ref, z_ref, *, a_len, b_len):
    i = pl.program_id(1)
    bb = m_ref.shape[0]

    def stage1(jb, carry):
        b = i * bb + jb
        xb = x_ref[pl.ds(b, a_len, stride=b_len), :]
        y = jnp.dot(m_ref[jb], xb.astype(BF16), preferred_element_type=F32)
        row = pl.multiple_of(b * a_len, a_len)
        y_ref[pl.ds(row, a_len), :] = _pack_pair(y[:a_len], y[a_len:])
        return carry

    lax.fori_loop(0, bb, stage1, 0, unroll=16)

    @pl.when(i == pl.num_programs(1) - 1)
    def _():
        def stage2(ka, carry):
            rows = pl.ds(ka, b_len, stride=a_len)
            yre, yim = _unpack_pair(y_ref[rows, :])
            y = jnp.concatenate([yre, yim], axis=0).astype(BF16)
            z = jnp.dot(w2_ref[...], y, preferred_element_type=F32)
            z_ref[rows, :] = _pack_pair(z[:b_len], z[b_len:])
            return carry

        lax.fori_loop(0, a_len, stage2, 0, unroll=16)
        zre, zim = _unpack_pair(z_ref[...])
        zre_ref[...] = zre.astype(zre_ref.dtype)
        zim_ref[...] = zim.astype(zim_ref.dtype)


def time_dft(x):
    t, d = x.shape
    stage1, stage2, a_len, b_len = time_dft_tables(t)
    lanes = 128
    bb = min(16, b_len)
    out = jax.ShapeDtypeStruct((t, d), BF16)
    return pl.pallas_call(
        functools.partial(_time_dft_kernel, a_len=a_len, b_len=b_len),
        grid=(d // lanes, b_len // bb),
        in_specs=[pl.BlockSpec((t, lanes), lambda j, i: (0, j)),
                  pl.BlockSpec((bb, 2 * a_len, a_len), lambda j, i: (i, 0, 0)),
                  pl.BlockSpec(stage2.shape, lambda j, i: (0, 0))],
        out_specs=[pl.BlockSpec((t, lanes), lambda j, i: (0, j))] * 2,
        out_shape=[out, out],
        scratch_shapes=[pltpu.VMEM((t, lanes), U32), pltpu.VMEM((t, lanes), U32)],
        compiler_params=_params("parallel", "arbitrary"),
    )(x, stage1, stage2)


def channel_dft_tables(c):
    ang = 2.0 * np.pi * ((np.arange(c)[:, None] * np.arange(c)[None, :]) % c) / c
    return (jnp.asarray(np.cos(ang) / math.sqrt(c), BF16), jnp.asarray(np.sin(ang) / math.sqrt(c), BF16))


def _ffn_prologue(h, shf_ref, scf_ref, wr_ref, wsg_ref, wsu_ref, wsd_ref, h_ref, f_ref, lg_ref, shared_ref):
    h_ref[...] = h
    f = _rms(h) * (1.0 + scf_ref[...]) + shf_ref[...]
    packed = _pack_rows(f)
    dq = packed.shape[1] // ROW_PARTS
    for i in range(ROW_PARTS):
        f_ref[i] = packed[:, i * dq:(i + 1) * dq]
    fb = f.astype(BF16)
    f_lo = (f - fb.astype(F32)).astype(BF16)
    nt = (((1,), (1,)), ((), ()))
    wr = wr_ref[...]
    wr_hi = wr.astype(BF16)
    wr_lo = (wr - wr_hi.astype(F32)).astype(BF16)
    lg_ref[...] = (lax.dot_general(wr_hi, fb, nt, preferred_element_type=F32)
                   + lax.dot_general(wr_hi, f_lo, nt, preferred_element_type=F32)
                   + lax.dot_general(wr_lo, fb, nt, preferred_element_type=F32))
    hid = (_silu(jnp.dot(fb, wsg_ref[...], preferred_element_type=F32))
           * jnp.dot(fb, wsu_ref[...], preferred_element_type=F32))
    shared_ref[...] = jnp.dot(hid.astype(BF16), wsd_ref[...], preferred_element_type=F32).astype(shared_ref.dtype)


def _gelu_tanh(x):
    return 0.5 * x * (1.0 + jnp.tanh(math.sqrt(2.0 / math.pi) * (x + 0.044715 * (x * x * x))))


def _even_post_kernel(y_ref, na_ref, x_ref, wglu_ref, bglu_ref, wo_ref, gm_ref, *rest):
    g = _gelu_tanh(y_ref[...])
    gate = jax.nn.sigmoid(jnp.dot(g.astype(BF16), wglu_ref[...], preferred_element_type=F32) + bglu_ref[...])
    s5 = (g * gate).astype(BF16)
    w = s5.shape[1]
    mix = (jnp.dot(s5, wo_ref[:w, :], preferred_element_type=F32)
           + jnp.dot(na_ref[...], wo_ref[w:, :], preferred_element_type=F32))
    _ffn_prologue(x_ref[...] + gm_ref[...] * mix, *rest)


def _odd_post_kernel(zre_ref, zim_ref, h_ref_in, cc_ref, sc_ref, wf_ref, bf_ref, gm_ref, *rest):
    c = cc_ref.shape[0]
    parts = []
    for grp in range(zre_ref.shape[1] // c):
        cols = slice(grp * c, (grp + 1) * c)
        parts.append(jnp.dot(zre_ref[:, cols], cc_ref[...], preferred_element_type=F32)
                     + jnp.dot(zim_ref[:, cols], sc_ref[...], preferred_element_type=F32))
    fr = jnp.concatenate(parts, axis=-1).astype(BF16)
    mix = jnp.dot(fr, wf_ref[...], preferred_element_type=F32) + bf_ref[...]
    _ffn_prologue(h_ref_in[...] + gm_ref[...] * mix, *rest)


def _post_call(body, row_inputs, fixed_inputs, t, d, n_exp):
    tm = min(512, t)
    dq = d // 2 // ROW_PARTS
    row = lambda i: (i, 0)
    in_specs = ([pl.BlockSpec((tm, a.shape[1]), row) for a in row_inputs]
                + [pl.BlockSpec(a.shape, functools.partial(lambda i, nd: (0,) * nd, nd=a.ndim))
                   for a in fixed_inputs])
    return pl.pallas_call(
        body,
        grid=(t // tm,),
        in_specs=in_specs,
        out_specs=[pl.BlockSpec((tm, d), row), pl.BlockSpec((ROW_PARTS, tm, dq), lambda i: (0, i, 0)),
                   pl.BlockSpec((n_exp, tm), lambda i: (0, i)), pl.BlockSpec((tm, d), row)],
        out_shape=[jax.ShapeDtypeStruct((t, d), F32), jax.ShapeDtypeStruct((ROW_PARTS, t, dq), U32),
                   jax.ShapeDtypeStruct((n_exp, t), F32), jax.ShapeDtypeStruct((t, d), BF16)],
        compiler_params=_params("parallel"),
    )(*row_inputs, *fixed_inputs)


def _route_kernel(lg_ref, bias_ref, tri_ref, idx_ref, gate_ref, rank_ref, cnt_ref, run_ref):
    @pl.when(pl.program_id(0) == 0)
    def _():
        run_ref[...] = jnp.zeros_like(run_ref)

    scores = jax.nn.sigmoid(lg_ref[...])
    n_exp, tb = scores.shape
    sel = scores + bias_ref[...]
    gsz = n_exp // N_EXPERT_GROUPS
    member = lax.broadcasted_iota(I32, (gsz, tb), 0)
    gscore = []
    for grp in range(N_EXPERT_GROUPS):
        xg = sel[grp * gsz:(grp + 1) * gsz, :]
        m1 = jnp.max(xg, axis=0, keepdims=True)
        first = jnp.min(jnp.where(xg == m1, member, gsz), axis=0, keepdims=True)
        m2 = jnp.max(jnp.where(member == first, -jnp.inf, xg), axis=0, keepdims=True)
        gscore.append(m1 + m2)
    keep_rows = []
    for grp in range(N_EXPERT_GROUPS):
        beaten = jnp.zeros((1, tb), F32)
        for other in range(N_EXPERT_GROUPS):
            if other == grp:
                continue
            wins = (gscore[other] >= gscore[grp]) if other < grp else (gscore[other] > gscore[grp])
            beaten = beaten + jnp.where(wins, 1.0, 0.0)
        keep_rows.append(jnp.broadcast_to(beaten < TOPK_GROUPS, (gsz, tb)))
    masked = jnp.where(jnp.concatenate(keep_rows, axis=0), sel, -jnp.inf)

    expert = lax.broadcasted_iota(I32, (n_exp, tb), 0)
    picks, gates, hots = [], [], []
    chosen = jnp.zeros((n_exp, tb), F32)
    for _ in range(TOP_K):
        m = jnp.max(masked, axis=0, keepdims=True)
        pick = jnp.min(jnp.where(masked == m, expert, n_exp), axis=0, keepdims=True)
        hot = expert == pick
        picks.append(pick)
        hots.append(hot)
        gates.append(jnp.sum(jnp.where(hot, scores, 0.0), axis=0, keepdims=True))
        chosen = jnp.where(hot, 1.0, chosen)
        masked = jnp.where(hot, -jnp.inf, masked)
    total = gates[0]
    for gk in gates[1:]:
        total = total + gk
    ahead = jnp.dot(chosen.astype(BF16), tri_ref[...], preferred_element_type=F32) + run_ref[...]
    for k in range(TOP_K):
        idx_ref[k:k + 1, :] = picks[k]
        gate_ref[k:k + 1, :] = ROUTED_SCALE * gates[k] / total
        rank_ref[k:k + 1, :] = jnp.sum(jnp.where(hots[k], ahead, 0.0), axis=0, keepdims=True).astype(I32)
    run_ref[...] = run_ref[...] + jnp.sum(chosen, axis=1, keepdims=True)
    cnt_ref[...] = jnp.broadcast_to(run_ref[...], cnt_ref.shape)


def route(logits_t, router_bias):
    n_exp, t = logits_t.shape
    tb = min(512, t)
    tri = jnp.asarray(np.triu(np.ones((tb, tb), np.float32), k=1), BF16)
    tok = lambda i: (0, i)
    idx, gate, rank, cnt = pl.pallas_call(
        _route_kernel,
        grid=(t // tb,),
        in_specs=[pl.BlockSpec((n_exp, tb), tok), pl.BlockSpec((n_exp, 1), lambda i: (0, 0)),
                  pl.BlockSpec((tb, tb), lambda i: (0, 0))],
        out_specs=[pl.BlockSpec((TOP_K, tb), tok)] * 3 + [pl.BlockSpec((n_exp, 128), lambda i: (0, 0))],
        out_shape=[jax.ShapeDtypeStruct((TOP_K, t), I32), jax.ShapeDtypeStruct((TOP_K, t), F32),
                   jax.ShapeDtypeStruct((TOP_K, t), I32), jax.ShapeDtypeStruct((n_exp, 128), F32)],
        scratch_shapes=[pltpu.VMEM((n_exp, 1), F32)],
        compiler_params=_params("arbitrary"),
    )(logits_t, router_bias.astype(F32).reshape(n_exp, 1), tri)
    return idx, gate, rank, cnt[:, 0].astype(I32)


def dispatch_plan(idx, rank, counts, n_blocks):
    n_exp = counts.shape[0]
    shift = EXPERT_ROWS.bit_length() - 1
    assert EXPERT_ROWS == 1 << shift
    padded = ((counts + EXPERT_ROWS - 1) >> shift) << shift
    pad_end = jnp.cumsum(padded)
    pad_start = pad_end - padded
    experts_iota = jnp.arange(n_exp, dtype=I32)
    dest = jnp.sum(jnp.where(idx[..., None] == experts_iota, pad_start, 0), axis=-1) + rank
    n_valid = (pad_end[-1] >> shift).astype(I32)
    blk = jnp.minimum(jnp.arange(n_blocks, dtype=I32), n_valid - 1)
    blk_e = jnp.sum((pad_end[None, :] <= blk[:, None] * EXPERT_ROWS).astype(I32), axis=1)
    return dest.astype(I32), jnp.minimum(blk_e, n_exp - 1).astype(I32), n_valid.reshape(1)


def _sc_mesh():
    return plsc.VectorSubcoreMesh(core_axis_name="core", subcore_axis_name="subcore")


def dispatch(f, dest, cap):
    t, w = f.shape
    n_choice = dest.shape[0]

    @functools.partial(pl.kernel, out_type=jax.ShapeDtypeStruct((cap, w), f.dtype), mesh=_sc_mesh(),
                       scratch_types=[pltpu.SemaphoreType.DMA])
    def scatter_rows(x_hbm, i_hbm, o_hbm, sem):
        def body(x_vmem, i_vmem):
            copies = [pltpu.async_copy(x_vmem, o_hbm.at[i_vmem.at[k]], sem) for k in range(n_choice)]
            for cp in copies:
                cp.wait()

        pltpu.emit_pipeline(
            body,
            grid=(t // SC_WINDOW,),
            in_specs=[pl.BlockSpec((SC_WINDOW, w), lambda i: (i, 0)),
                      pl.BlockSpec((n_choice, SC_WINDOW), lambda i: (0, i))],
            out_specs=[],
            core_axis_name=("core", "subcore"),
            dimension_semantics=(pltpu.PARALLEL,),
        )(x_hbm, i_hbm)

    return scatter_rows(f, dest)


def gather_rows(rows, index_row):
    n = index_row.shape[1]
    w = rows.shape[1]

    piece = SC_WINDOW // SC_GATHER_SPLIT

    @functools.partial(pl.kernel, out_type=jax.ShapeDtypeStruct((n, w), rows.dtype), mesh=_sc_mesh(),
                       scratch_types=[pltpu.SemaphoreType.DMA])
    def gather(y_hbm, i_hbm, o_hbm, sem):
        def body(i_vmem, o_vmem):
            copies = [pltpu.async_copy(y_hbm.at[i_vmem.at[0, pl.ds(j * piece, piece)]],
                                       o_vmem.at[pl.ds(j * piece, piece)], sem)
                      for j in range(SC_GATHER_SPLIT)]
            for cp in copies:
                cp.wait()

        pltpu.emit_pipeline(
            body,
            grid=(n // SC_WINDOW,),
            in_specs=[pl.BlockSpec((1, SC_WINDOW), lambda i: (0, i))],
            out_specs=[pl.BlockSpec((SC_WINDOW, w), lambda i: (i, 0))],
            core_axis_name=("core", "subcore"),
            dimension_semantics=(pltpu.PARALLEL,),
        )(i_hbm, o_hbm)

    return gather(rows, index_row)


def _experts_kernel(be_ref, x_ref, wg_ref, wu_ref, wd_ref, y_ref, wg_bf, wu_bf, wd_bf):
    b = pl.program_id(0)

    @pl.when((b == 0) | (be_ref[b] != be_ref[jnp.maximum(b - 1, 0)]))
    def _():
        wg_bf[...] = wg_ref[0, 0].astype(BF16)
        wu_bf[...] = wu_ref[0, 0].astype(BF16)
        wd_bf[...] = wd_ref[0, 0].astype(BF16)

    x = _unpack_rows(jnp.concatenate([x_ref[i] for i in range(ROW_PARTS)], axis=1)).astype(BF16)
    hid = (_silu(jnp.dot(x, wg_bf[...], preferred_element_type=F32))
           * jnp.dot(x, wu_bf[...], preferred_element_type=F32))
    packed = _pack_rows(jnp.dot(hid.astype(BF16), wd_bf[...], preferred_element_type=F32))
    dq = packed.shape[1] // ROW_PARTS
    for i in range(ROW_PARTS):
        y_ref[i] = packed[:, i * dq:(i + 1) * dq]


def experts(rows, blk_e, n_valid, layer, w_gate, w_up, w_down):
    parts, cap, dq = rows.shape
    _, _, d, ff = w_gate.shape
    blk = pl.BlockSpec((parts, EXPERT_ROWS, dq), lambda b, be: (0, b, 0))
    return pl.pallas_call(
        _experts_kernel,
        grid_spec=pltpu.PrefetchScalarGridSpec(
            num_scalar_prefetch=1,
            grid=(n_valid[0],),
            in_specs=[blk,
                      pl.BlockSpec((1, 1, d, ff), lambda b, be: (layer, be[b], 0, 0)),
                      pl.BlockSpec((1, 1, d, ff), lambda b, be: (layer, be[b], 0, 0)),
                      pl.BlockSpec((1, 1, ff, d), lambda b, be: (layer, be[b], 0, 0))],
            out_specs=blk,
            scratch_shapes=[pltpu.VMEM((d, ff), BF16), pltpu.VMEM((d, ff), BF16), pltpu.VMEM((ff, d), BF16)],
        ),
        out_shape=jax.ShapeDtypeStruct(rows.shape, U32),
        compiler_params=_params("arbitrary"),
    )(blk_e, rows, w_gate, w_up, w_down)


def _combine_kernel(gate_ref, h_ref, shared_ref, gf_ref, sh_ref, sc_ref, *rest):
    y_refs, outs = rest[:TOP_K], rest[TOP_K:]
    routed = None
    for k in range(TOP_K):
        y = _unpack_rows(jnp.concatenate([y_refs[k][i, 0] for i in range(ROW_PARTS)], axis=1))
        routed = gate_ref[:, k:k + 1] * y if routed is None else routed + gate_ref[:, k:k + 1] * y
    h = h_ref[...] + gf_ref[...] * (routed + shared_ref[...].astype(F32))
    outs[0][...] = h
    if len(outs) > 1:
        outs[1][...] = _rms(h) * (1.0 + sc_ref[...]) + sh_ref[...]


def combine(y_rows, dest, gate_tk, h, shared, gate_ffn, next_shift, next_scale, with_next):
    t, d = h.shape
    parts, cap, dq = y_rows.shape
    n_out = 2 if with_next else 1
    n_choice = dest.shape[0]
    flat = (dest[None] + (jnp.arange(parts, dtype=I32) * cap)[:, None, None]).reshape(1, parts * n_choice * t)
    picked = gather_rows(y_rows.reshape(parts * cap, dq), flat).reshape(parts, n_choice, t, dq)
    tm = min(256, t)
    row = lambda i: (i, 0)
    fixed = lambda i: (0, 0)
    choice = [pl.BlockSpec((parts, 1, tm, dq), functools.partial(lambda i, k: (0, k, i, 0), k=k))
              for k in range(n_choice)]
    return pl.pallas_call(
        _combine_kernel,
        grid=(t // tm,),
        in_specs=[pl.BlockSpec((tm, n_choice), row), pl.BlockSpec((tm, d), row), pl.BlockSpec((tm, d), row),
                  pl.BlockSpec((1, d), fixed), pl.BlockSpec((1, d), fixed), pl.BlockSpec((1, d), fixed)] + choice,
        out_specs=[pl.BlockSpec((tm, d), row)] * n_out,
        out_shape=[jax.ShapeDtypeStruct((t, d), F32)] * n_out,
        compiler_params=_params("parallel"),
    )(gate_tk, h, shared, gate_ffn, next_shift, next_scale, *([picked] * n_choice))


def moe_tail(h, f_rows, logits_t, shared, router_bias, layer, w_gate, w_up, w_down, gate_ffn,
             next_shift, next_scale, with_next):
    t, d = h.shape
    parts, _, dq = f_rows.shape
    n_exp = w_gate.shape[1]
    n_blocks = -(-(t * TOP_K + n_exp * (EXPERT_ROWS - 1)) // EXPERT_ROWS)
    cap = n_blocks * EXPERT_ROWS
    idx, gate, rank, counts = route(logits_t, router_bias)
    dest, blk_e, n_valid = dispatch_plan(idx, rank, counts, n_blocks)
    dest_all = jnp.concatenate([dest + i * cap for i in range(parts)], axis=1)
    rows = dispatch(f_rows.reshape(parts * t, dq), dest_all, parts * cap).reshape(parts, cap, dq)
    y_rows = experts(rows, blk_e, n_valid, layer, w_gate, w_up, w_down)
    return combine(y_rows, dest, gate.T, h, shared, gate_ffn, next_shift, next_scale, with_next)


def kernel(x, c, ctx, c_ctx, w_ada, b_ada, w_in, s5_lam_re, s5_lam_im, s5_log_dt, s5_b_re, s5_b_im,
           s5_c_re, s5_c_im, s5_d, s5_w_glu, s5_b_glu, na_q_gain, na_k_gain, na_rpb, w_mix_out,
           w_fourier_out, b_fourier_out, w_router, router_bias, w_exp_gate, w_exp_up, w_exp_down,
           w_sh_gate, w_sh_up, w_sh_down):
    bsz, t, d = x.shape
    assert bsz == 1 and w_ada.shape[0] == 2
    n_exp = w_router.shape[2]
    s5w = s5_w_glu.shape[1]
    naw = w_in.shape[2] - s5w
    naw //= 3
    heads = naw // NA_HEAD_DIM

    cond8 = jnp.concatenate([c[:1].astype(F32), c_ctx.astype(F32)[None], jnp.zeros((6, d), F32)], axis=0)
    ada = adaln_all(cond8, w_ada, b_ada)
    mod = lambda layer, who, j: ada[layer, who:who + 1, j * d:(j + 1) * d]

    def ffn_weights(i):
        return (mod(i, 0, 3), mod(i, 0, 4), jnp.transpose(w_router[i]).astype(F32),
                w_sh_gate[i].astype(BF16), w_sh_up[i].astype(BF16), w_sh_down[i].astype(BF16))

    h0 = x[0]
    seg = jnp.asarray(np.kron(np.eye(heads), np.ones((NA_HEAD_DIM, NA_HEAD_DIM))), BF16)
    w_in_b = w_in[0].astype(BF16)
    qg = jnp.tile(na_q_gain[0].astype(F32), heads)[None]
    kg = jnp.tile(na_k_gain[0].astype(F32), heads)[None]
    u_c, _, k_c, v_c = in_projection(ctx[0], mod(0, 1, 0), mod(0, 1, 1), w_in_b, seg, qg, kg, s5w, naw)
    u_l, q_l, k_l, v_l = in_projection(h0, mod(0, 0, 0), mod(0, 0, 1), w_in_b, seg, qg, kg, s5w, naw)
    mats = s5_matrices(s5_lam_re[0], s5_lam_im[0], s5_log_dt[0], s5_b_re[0], s5_b_im[0],
                       s5_c_re[0], s5_c_im[0], s5_d[0])
    y_s5 = s5_mixer(u_c, u_l, mats)
    na = neighbourhood_attention(q_l, k_l, v_l, k_c, v_c, na_bias_table(na_rpb[0], t // GRID_W))
    h1, f1, lg1, sh1 = _post_call(
        _even_post_kernel, [y_s5, na, h0],
        [s5_w_glu[0].astype(BF16), s5_b_glu[0].astype(F32)[None], w_mix_out[0].astype(BF16), mod(0, 0, 2),
         *ffn_weights(0)], t, d, n_exp)
    h2, a1 = moe_tail(h1, f1, lg1, sh1, router_bias[0], 0, w_exp_gate, w_exp_up, w_exp_down,
                      mod(0, 0, 5), mod(1, 0, 0), mod(1, 0, 1), True)

    zre, zim = time_dft(a1)
    cc, sc = channel_dft_tables(d // FOURIER_GROUPS)
    h3, f3, lg3, sh3 = _post_call(
        _odd_post_kernel, [zre, zim, h2],
        [cc, sc, w_fourier_out[0].astype(BF16), b_fourier_out[0].astype(F32)[None], mod(1, 0, 2),
         *ffn_weights(1)], t, d, n_exp)
    zero_row = jnp.zeros((1, d), F32)
    (out,) = moe_tail(h3, f3, lg3, sh3, router_bias[1], 1, w_exp_gate, w_exp_up, w_exp_down,
                      mod(1, 0, 5), zero_row, zero_row, False)
    return out[None]
```

```python
import functools
import math

import numpy as np
import jax
import jax.numpy as jnp
from jax import lax
from jax.experimental import pallas as pl
from jax.experimental.pallas import tpu as pltpu
from jax.experimental.pallas import tpu_sc as plsc

F32 = jnp.float32
BF16 = jnp.bfloat16
I32 = jnp.int32
U32 = jnp.uint32
HIGHEST = lax.Precision.HIGHEST

MXU_DEPTH = 256
LANES = 128
GRID_W = 64
NORM_EPS = 1e-6
S5_GROUP = 16
S5_LAMBDA_RE_MAX = -1e-4
S5_CHUNK = 16
S5_TILE = 128
NA_HEAD_DIM = 64
NA_KH = 8
NA_KW = 16
NA_STEP_ROWS = 2
FOURIER_GROUPS = 4
N_EXPERT_GROUPS = 8
TOPK_GROUPS = 4
TOP_K = 8
ROUTED_SCALE = 2.5
EXPERT_ROWS = 2048
ROW_PARTS = 2
SC_WINDOW = 128
SC_GATHER_SPLIT = 4
NEG_BIG = -1e30

VMEM_LIMIT_BYTES = 56 * 1024 * 1024


def _params(*sem):
    return pltpu.CompilerParams(dimension_semantics=sem or None,
                                vmem_limit_bytes=VMEM_LIMIT_BYTES)


def _rms(x):
    return x * lax.rsqrt(jnp.mean(x * x, axis=-1, keepdims=True) + NORM_EPS)


def _silu(x):
    return x * jax.nn.sigmoid(x)


def _pack_pair(lo, hi):
    lo = lax.bitcast_convert_type(lo.astype(BF16).astype(F32), U32)
    hi = lax.bitcast_convert_type(hi.astype(BF16).astype(F32), U32)
    return (hi & jnp.uint32(0xFFFF0000)) | (lo >> 16)


def _unpack_pair(w):
    return (lax.bitcast_convert_type(w << 16, F32), lax.bitcast_convert_type(w & jnp.uint32(0xFFFF0000), F32))


def _pack_rows(x):
    n = x.shape[1] // 2
    return _pack_pair(x[:, :n], x[:, n:])


def _unpack_rows(w):
    return jnp.concatenate(_unpack_pair(w), axis=1)


def _ada_kernel(c_ref, w_ref, b_ref, o_ref):
    o_ref[0] = jnp.dot(_silu(c_ref[...]), w_ref[0], preferred_element_type=F32,
                       precision=HIGHEST) + b_ref[0]


def adaln_all(cond8, w_ada, b_ada):
    n_layers, d, n6 = w_ada.shape
    tn = n6 // 4
    return pl.pallas_call(
        _ada_kernel,
        grid=(n_layers, n6 // tn),
        in_specs=[pl.BlockSpec((8, d), lambda l, j: (0, 0)),
                  pl.BlockSpec((1, d, tn), lambda l, j: (l, 0, j)),
                  pl.BlockSpec((1, 1, tn), lambda l, j: (l, 0, j))],
        out_specs=pl.BlockSpec((1, 8, tn), lambda l, j: (l, 0, j)),
        out_shape=jax.ShapeDtypeStruct((n_layers, 8, n6), F32),
        compiler_params=_params("parallel", "parallel"),
    )(cond8, w_ada, b_ada.reshape(n_layers, 1, n6))


def _inproj_kernel(x_ref, sh_ref, sc_ref, w_ref, seg_ref, qg_ref, kg_ref,
                   u_ref, q_ref, k_ref, v_ref):
    a = _rms(x_ref[...]) * (1.0 + sc_ref[...]) + sh_ref[...]
    z = jnp.dot(a.astype(BF16), w_ref[...], preferred_element_type=F32)
    s5w = u_ref.shape[1]
    naw = q_ref.shape[1]

    def head_norm(t, gain):
        ss = jnp.dot((t * t).astype(BF16), seg_ref[...], preferred_element_type=F32)
        return t * lax.rsqrt(ss * (1.0 / NA_HEAD_DIM) + NORM_EPS) * gain

    u_ref[...] = z[:, :s5w]
    q_ref[...] = head_norm(z[:, s5w:s5w + naw], qg_ref[...]).astype(BF16)
    k_ref[...] = head_norm(z[:, s5w + naw:s5w + 2 * naw], kg_ref[...]).astype(BF16)
    v_ref[...] = z[:, s5w + 2 * naw:].astype(BF16)


def in_projection(x, shift, scale, w_in_bf16, seg_ones, q_gain_row, k_gain_row, s5w, naw):
    t, d = x.shape
    tm = min(512, t)
    row = lambda i: (i, 0)
    fixed = lambda i: (0, 0)
    return pl.pallas_call(
        _inproj_kernel,
        grid=(t // tm,),
        in_specs=[pl.BlockSpec((tm, d), row),
                  pl.BlockSpec((1, d), fixed), pl.BlockSpec((1, d), fixed),
                  pl.BlockSpec(w_in_bf16.shape, fixed),
                  pl.BlockSpec(seg_ones.shape, fixed),
                  pl.BlockSpec((1, naw), fixed), pl.BlockSpec((1, naw), fixed)],
        out_specs=[pl.BlockSpec((tm, s5w), row), pl.BlockSpec((tm, naw), row),
                   pl.BlockSpec((tm, naw), row), pl.BlockSpec((tm, naw), row)],
        out_shape=[jax.ShapeDtypeStruct((t, s5w), F32)] + [jax.ShapeDtypeStruct((t, naw), BF16)] * 3,
        compiler_params=_params("parallel"),
    )(x, shift, scale, w_in_bf16, seg_ones, q_gain_row, k_gain_row)


def s5_matrices(lam_re, lam_im, log_dt, b_re, b_im, c_re, c_im, d_skip):
    L = S5_CHUNK
    c = S5_GROUP
    taus = jnp.arange(L + 1, dtype=F32)

    def direction(i):
        lam = lax.complex(jnp.minimum(lam_re[i].astype(F32), S5_LAMBDA_RE_MAX), lam_im[i].astype(F32))
        ldt = lam * jnp.exp(log_dt[i].astype(F32))[:, None]
        lam_bar = jnp.exp(ldt)
        b_bar = ((lam_bar - 1.0) / lam)[..., None] * lax.complex(b_re[i].astype(F32), b_im[i].astype(F32))
        cc = lax.complex(c_re[i].astype(F32), c_im[i].astype(F32))
        powers = jnp.exp(ldt[None] * taus[:, None, None])
        resp = jnp.real(jnp.einsum('gcp,tgp,gpd->gctd', cc, powers[:L], b_bar, precision=HIGHEST))
        return jnp.transpose(powers, (1, 2, 0)), b_bar, cc, resp

    pw_f, bb_f, cc_f, k_f = direction(0)
    pw_b, bb_b, cc_b, k_b = direction(1)
    g, p = pw_f.shape[:2]
    lag0 = k_f[:, :, :1] + k_b[:, :, :1] + (jnp.eye(c, dtype=F32)[None, :, None, :] * d_skip.astype(F32)[:, :, None, None])
    by_lag = jnp.concatenate([jnp.flip(k_f[:, :, 1:], axis=2), lag0, k_b[:, :, 1:]], axis=2).reshape(g, c, (2 * L - 1) * c)
    n = (2 * L - 1) * c
    windows = jnp.tile(by_lag, (1, 1, L + 1))[:, :, :L * (n + c)].reshape(g, c, L, n + c)[..., :L * c]
    toep_t = jnp.transpose(jnp.flip(windows, axis=2), (0, 2, 1, 3)).reshape(g, L * c, L * c)

    def state_in(pw_by_s, b_bar):
        return (pw_by_s[:, :, :, None] * b_bar[:, :, None, :]).reshape(g, p, L * c)

    wf = state_in(jnp.flip(pw_f[:, :, :L], axis=2), bb_f)
    wb = state_in(pw_b[:, :, :L], bb_b)
    w_state_t = jnp.concatenate([jnp.real(wf), jnp.imag(wf), jnp.imag(wf), jnp.real(wf),
                                 jnp.real(wb), jnp.imag(wb), jnp.imag(wb), jnp.real(wb)], axis=1)

    def state_out(pw_by_l, cc):
        return (jnp.transpose(pw_by_l, (0, 2, 1))[:, :, None, :] * cc[:, None, :, :]).reshape(g, L * c, p)

    rf = state_out(pw_f[:, :, 1:], cc_f)
    rb = state_out(jnp.flip(pw_b[:, :, 1:], axis=2), cc_b)
    r_state_t = jnp.concatenate([jnp.real(rf), -jnp.imag(rf), jnp.real(rb), -jnp.imag(rb)], axis=-1)

    def mult(a):
        ar, ai = jnp.real(a), jnp.imag(a)
        return jnp.stack([jnp.concatenate([ar, ar], -1), jnp.concatenate([-ai, ai], -1),
                          jnp.concatenate([ai, -ai], -1)])

    return (toep_t.astype(BF16), w_state_t.astype(BF16), r_state_t.astype(BF16),
            mult(pw_f[:, :, L]), mult(pw_b[:, :, L]))


def _s5_pack_kernel(*refs):
    u_refs, (wt_ref, ut_ref, f1_ref, f2_ref, b1_ref, b2_ref) = refs[:-6], refs[-6:]
    L = S5_CHUNK
    g, lc, nck = ut_ref.shape
    c = lc // L
    gs = g // len(u_refs)
    for s in range(L):
        for j, u_ref in enumerate(u_refs):
            step_s = u_ref[pl.ds(s, nck, stride=L), :]
            ut_ref[j * gs:(j + 1) * gs, s * c:(s + 1) * c, :] = (
                jnp.transpose(step_s).astype(BF16).reshape(gs, c, nck))
    n = f1_ref.shape[2]
    for gi in range(g):
        inc = jnp.dot(wt_ref[gi], ut_ref[gi], preferred_element_type=F32)
        for j, ref in enumerate((f1_ref, f2_ref, b1_ref, b2_ref)):
            ref[:, gi, :] = jnp.transpose(inc[j * n:(j + 1) * n, :])


def s5_pack(u, wt_state):
    t, w = u.shape
    g, n4, lc = wt_state.shape
    nc = t // S5_CHUNK
    tile = min(S5_TILE, nc)
    inc = jax.ShapeDtypeStruct((nc, g, n4 // 4), F32)
    inc_blk = pl.BlockSpec((tile, g, n4 // 4), lambda i: (i, 0, 0))
    return pl.pallas_call(
        _s5_pack_kernel,
        grid=(nc // tile,),
        in_specs=[pl.BlockSpec((tile * S5_CHUNK, LANES), functools.partial(lambda i, j: (i, j), j=j))
                  for j in range(w // LANES)] + [pl.BlockSpec(wt_state.shape, lambda i: (0, 0, 0))],
        out_specs=[pl.BlockSpec((g, lc, tile), lambda i: (0, 0, i))] + [inc_blk] * 4,
        out_shape=[jax.ShapeDtypeStruct((g, lc, nc), BF16)] + [inc] * 4,
        compiler_params=_params("parallel"),
    )(*([u] * (w // LANES)), wt_state)


def _s5_scan_kernel(s1_ref, s2_ref, m_ref, init_ref, x_ref, last_ref, v1_ref, v2_ref, *, reverse):
    @pl.when(pl.program_id(0) == 0)
    def _():
        v1_ref[...] = init_ref[0]
        v2_ref[...] = init_ref[1]

    a1, a2, a3 = m_ref[0], m_ref[1], m_ref[2]
    cb = s1_ref.shape[0]

    def body(j, carry):
        v1, v2 = carry
        jj = cb - 1 - j if reverse else j
        x_ref[jj] = v1
        return (a1 * v1 + a2 * v2 + s1_ref[jj], a1 * v2 + a3 * v1 + s2_ref[jj])

    v1, v2 = lax.fori_loop(0, cb, body, (v1_ref[...], v2_ref[...]))
    v1_ref[...] = v1
    v2_ref[...] = v2
    last_ref[...] = v1


def s5_chunk_scan(s1, s2, mult, init, reverse):
    nc, g, n = s1.shape
    cb = min(S5_TILE, nc)
    nb = nc // cb
    blk = (lambda i: (nb - 1 - i, 0, 0)) if reverse else (lambda i: (i, 0, 0))
    return pl.pallas_call(
        functools.partial(_s5_scan_kernel, reverse=reverse),
        grid=(nb,),
        in_specs=[pl.BlockSpec((cb, g, n), blk), pl.BlockSpec((cb, g, n), blk),
                  pl.BlockSpec((3, g, n), lambda i: (0, 0, 0)), pl.BlockSpec((2, g, n), lambda i: (0, 0, 0))],
        out_specs=[pl.BlockSpec((cb, g, n), blk), pl.BlockSpec((g, n), lambda i: (0, 0))],
        out_shape=[jax.ShapeDtypeStruct((nc, g, n), F32), jax.ShapeDtypeStruct((g, n), F32)],
        scratch_shapes=[pltpu.VMEM((g, n), F32), pltpu.VMEM((g, n), F32)],
        compiler_params=_params("arbitrary"),
    )(s1, s2, mult, init)


def _s5_readout_kernel(ut_ref, tt_ref, rt_ref, xf_ref, xb_ref, y_ref, yt_ref, *slab_refs):
    L = S5_CHUNK
    g, lc, nck = ut_ref.shape
    c = lc // L
    gs = g // len(slab_refs)
    for gi in range(g):
        xin_t = jnp.concatenate([jnp.transpose(xf_ref[:, gi, :]), jnp.transpose(xb_ref[:, gi, :])], axis=0)
        yt_ref[gi] = (jnp.dot(tt_ref[gi], ut_ref[gi], preferred_element_type=F32)
                      + jnp.dot(rt_ref[gi], xin_t.astype(BF16), preferred_element_type=F32))
    for j, slab in enumerate(slab_refs):
        for l in range(L):
            step_l = yt_ref[j * gs:(j + 1) * gs, l * c:(l + 1) * c, :].reshape(gs * c, nck)
            slab[pl.ds(l, nck, stride=L), :] = jnp.transpose(step_l)
        y_ref[:, j * gs * c:(j + 1) * gs * c] = slab[...]


def s5_readout(ut, toep_t, r_state_t, xin_f, xin_b):
    g, lc, nc = ut.shape
    n = xin_f.shape[2]
    tile = min(S5_TILE, nc)
    fixed = lambda i: (0, 0, 0)
    state_blk = pl.BlockSpec((tile, g, n), lambda i: (i, 0, 0))
    return pl.pallas_call(
        _s5_readout_kernel,
        grid=(nc // tile,),
        in_specs=[pl.BlockSpec((g, lc, tile), lambda i: (0, 0, i)),
                  pl.BlockSpec(toep_t.shape, fixed), pl.BlockSpec(r_state_t.shape, fixed), state_blk, state_blk],
        out_specs=pl.BlockSpec((tile * S5_CHUNK, g * lc // S5_CHUNK), lambda i: (i, 0)),
        out_shape=jax.ShapeDtypeStruct((nc * S5_CHUNK, g * lc // S5_CHUNK), F32),
        scratch_shapes=[pltpu.VMEM((g, lc, tile), F32)]
        + [pltpu.VMEM((tile * S5_CHUNK, LANES), F32)] * (g * lc // S5_CHUNK // LANES),
        compiler_params=_params("parallel"),
    )(ut, toep_t, r_state_t, xin_f, xin_b)


def s5_mixer(u_ctx, u_lat, mats):
    toep_t, wt_state, r_state_t, mult_f, mult_b = mats
    L = S5_CHUNK
    g, n = mult_f.shape[1:]
    halves = lambda v: jnp.stack([v, jnp.roll(v, n // 2, axis=-1)])
    n_ctx = u_ctx.shape[0] // L
    ctx_chunks = -(-(n_ctx + 1) // S5_TILE) * S5_TILE
    ctx_pad = jnp.pad(u_ctx, ((0, ctx_chunks * L - u_ctx.shape[0]), (0, 0)))
    _, cf1, cf2, cb1, cb2 = s5_pack(ctx_pad, wt_state)
    zero = jnp.zeros((2, g, n), F32)
    ctx_f, _ = s5_chunk_scan(cf1, cf2, mult_f, zero, False)
    _, ctx_b_last = s5_chunk_scan(cb1, cb2, mult_b, zero, True)
    ut, f1, f2, b1, b2 = s5_pack(u_lat, wt_state)
    xin_f, _ = s5_chunk_scan(f1, f2, mult_f, halves(ctx_f[n_ctx]), False)
    xin_b, _ = s5_chunk_scan(b1, b2, mult_b, halves(ctx_b_last), True)
    return s5_readout(ut, toep_t, r_state_t, xin_f, xin_b)


def _na_window(rows):
    step = NA_STEP_ROWS
    assert rows % step == 0 and (NA_KH // 2) % step == 0 and rows >= NA_KH + step
    groups = rows // step
    n_blk = (NA_KH + step) // step
    first = np.clip(np.arange(groups) - NA_KH // 2 // step, 0, groups - n_blk)
    return groups, n_blk, first


def na_bias_table(rpb, rows):
    step = NA_STEP_ROWS
    groups, n_blk, first = _na_window(rows)
    win = n_blk * step
    q_col = np.arange(GRID_W)
    col_start = np.clip(q_col - NA_KW // 2, 0, GRID_W - NA_KW)
    key_col = np.arange(GRID_W)
    off = key_col[None, :] - col_start[:, None]
    valid_col = (off >= 0) & (off < NA_KW)
    rel_col = np.clip(key_col[None, :] - q_col[:, None] + NA_KW - 1, 0, 2 * NA_KW - 2)
    q_row = step * np.arange(groups)[:, None, None] + np.arange(step)[None, :, None]
    key_row = step * first[:, None, None] + np.arange(win)[None, None, :]
    row0 = np.clip(q_row - NA_KH // 2, 0, rows - NA_KH)
    rel_row = np.where((key_row >= row0) & (key_row < row0 + NA_KH), key_row - q_row + NA_KH - 1, -1)
    variant = np.arange(groups) - first
    reps = [int(np.argmax(variant == d)) for d in range(n_blk)]
    assert all((rel_row[g] == rel_row[reps[variant[g]]]).all() for g in range(groups))
    rel_row = rel_row[reps]
    pick_col = jnp.asarray(np.arange(2 * NA_KW - 1)[:, None, None] == rel_col[None], F32)
    tiles = jnp.where(valid_col[None, None], jnp.einsum('hrx,xck->rhck', rpb.astype(F32), pick_col, precision=HIGHEST),
                      NEG_BIG)
    tiles = jnp.concatenate([tiles, jnp.full_like(tiles[:1], NEG_BIG)], axis=0)
    return jnp.stack([jnp.concatenate([jnp.concatenate([tiles[int(rel_row[v, j, i])] for i in range(win)], axis=-1)
                                       for j in range(step)], axis=1) for v in range(n_blk)])


def _na_kernel(*refs, scale, n_blk):
    q_ref = refs[0]
    k_refs = refs[1:1 + n_blk]
    v_refs = refs[1 + n_blk:1 + 2 * n_blk]
    kc_ref, vc_ref, b_ref, o_ref = refs[1 + 2 * n_blk:]
    hd = NA_HEAD_DIM
    width = q_ref.shape[1]
    span = min(MXU_DEPTH, width)
    nt = (((1,), (1,)), ((), ()))
    q = q_ref[...] * scale
    kk = jnp.concatenate([r[...] for r in k_refs], axis=0)
    vv = jnp.concatenate([r[...] for r in v_refs], axis=0)
    kc, vc = kc_ref[...], vc_ref[...]
    nq = q.shape[0]
    per = span // hd
    lane = lax.broadcasted_iota(I32, (nq, span), 1)
    own = [(lane >= j * hd) & (lane < (j + 1) * hd) for j in range(per)]
    out_cols = []
    for c0 in range(0, width, span):
        cols = slice(c0, c0 + span)
        qs = jnp.concatenate([jnp.where(own[j], q[:, cols], jnp.zeros_like(q[:, cols])) for j in range(per)], axis=0)
        h0 = c0 // hd
        bias = b_ref[0, h0:h0 + per].reshape(per * nq, kk.shape[0])
        s = lax.dot_general(qs, kk[:, cols], nt, preferred_element_type=F32) + bias
        sc = lax.dot_general(qs, kc[:, cols], nt, preferred_element_type=F32)
        m = jnp.maximum(jnp.max(s, axis=-1, keepdims=True), jnp.max(sc, axis=-1, keepdims=True))
        p = jnp.exp(s - m)
        pc = jnp.exp(sc - m)
        den = jnp.sum(p, axis=-1, keepdims=True) + jnp.sum(pc, axis=-1, keepdims=True)
        o = (jnp.dot(p.astype(BF16), vv[:, cols], preferred_element_type=F32)
             + jnp.dot(pc.astype(BF16), vc[:, cols], preferred_element_type=F32)) / den
        acc = jnp.zeros((nq, span), F32)
        for j in range(per):
            acc = jnp.where(own[j], o[j * nq:(j + 1) * nq], acc)
        out_cols.append(acc)
    o_ref[...] = jnp.concatenate(out_cols, axis=1).astype(o_ref.dtype)


def neighbourhood_attention(q, k, v, k_ctx, v_ctx, bias_table):
    t, w = q.shape
    groups, n_blk, _ = _na_window(t // GRID_W)
    first = lambda g: jnp.clip(g - NA_KH // 2 // NA_STEP_ROWS, 0, groups - n_blk)
    blk_tokens = NA_STEP_ROWS * GRID_W
    row_blk = pl.BlockSpec((blk_tokens, w), lambda g: (g, 0))
    key_blks = [pl.BlockSpec((blk_tokens, w), functools.partial(lambda g, i: (first(g) + i, 0), i=i))
                for i in range(n_blk)]
    ctx_blk = pl.BlockSpec(k_ctx.shape, lambda g: (0, 0))
    bias_blk = pl.BlockSpec((1,) + bias_table.shape[1:], lambda g: (g - first(g), 0, 0, 0))
    return pl.pallas_call(
        functools.partial(_na_kernel, scale=NA_HEAD_DIM ** -0.5, n_blk=n_blk),
        grid=(groups,),
        in_specs=[row_blk] + key_blks + key_blks + [ctx_blk, ctx_blk, bias_blk],
        out_specs=row_blk,
        out_shape=jax.ShapeDtypeStruct((t, w), BF16),
        compiler_params=_params("parallel"),
    )(q, *([k] * n_blk), *([v] * n_blk), k_ctx, v_ctx, bias_table)


def time_dft_tables(t):
    a_len = 1 << (int(math.log2(t)) // 2)
    b_len = t // a_len
    ka = np.arange(a_len)[:, None]
    tok = b_len * np.arange(a_len)[None, :]
    ang1 = -2.0 * np.pi * ((ka * (tok[None] + np.arange(b_len)[:, None, None])) % t) / t
    stage1 = np.concatenate([np.cos(ang1), np.sin(ang1)], axis=1) / math.sqrt(t)
    ang2 = 2.0 * np.pi * ((np.arange(b_len)[:, None] * np.arange(b_len)[None, :]) % b_len) / b_len
    c2, s2 = np.cos(ang2), np.sin(ang2)
    stage2 = np.block([[c2, s2], [-s2, c2]])
    return jnp.asarray(stage1, BF16), jnp.asarray(stage2, BF16), a_len, b_len


def _time_dft_kernel(x_ref, m_ref, w2_ref, zre_ref, zim_ref, y_ref, z_ref, *, a_len, b_len):
    i = pl.program_id(1)
    bb = m_ref.shape[0]

    def stage1(jb, carry):
        b = i * bb + jb
        xb = x_ref[pl.ds(b, a_len, stride=b_len), :]
        y = jnp.dot(m_ref[jb], xb.astype(BF16), preferred_element_type=F32)
        row = pl.multiple_of(b * a_len, a_len)
        y_ref[pl.ds(row, a_len), :] = _pack_pair(y[:a_len], y[a_len:])
        return carry

    lax.fori_loop(0, bb, stage1, 0, unroll=16)

    @pl.when(i == pl.num_programs(1) - 1)
    def _():
        def stage2(ka, carry):
            rows = pl.ds(ka, b_len, stride=a_len)
            yre, yim = _unpack_pair(y_ref[rows, :])
            y = jnp.concatenate([yre, yim], axis=0).astype(BF16)
            z = jnp.dot(w2_ref[...], y, preferred_element_type=F32)
            z_ref[rows, :] = _pack_pair(z[:b_len], z[b_len:])
            return carry

        lax.fori_loop(0, a_len, stage2, 0, unroll=16)
        zre, zim = _unpack_pair(z_ref[...])
        zre_ref[...] = zre.astype(zre_ref.dtype)
        zim_ref[...] = zim.astype(zim_ref.dtype)


def time_dft(x):
    t, d = x.shape
    stage1, stage2, a_len, b_len = time_dft_tables(t)
    lanes = 128
    bb = min(16, b_len)
    out = jax.ShapeDtypeStruct((t, d), BF16)
    return pl.pallas_call(
        functools.partial(_time_dft_kernel, a_len=a_len, b_len=b_len),
        grid=(d // lanes, b_len // bb),
        in_specs=[pl.BlockSpec((t, lanes), lambda j, i: (0, j)),
                  pl.BlockSpec((bb, 2 * a_len, a_len), lambda j, i: (i, 0, 0)),
                  pl.BlockSpec(stage2.shape, lambda j, i: (0, 0))],
        out_specs=[pl.BlockSpec((t, lanes), lambda j, i: (0, j))] * 2,
        out_shape=[out, out],
        scratch_shapes=[pltpu.VMEM((t, lanes), U32), pltpu.VMEM((t, lanes), U32)],
        compiler_params=_params("parallel", "arbitrary"),
    )(x, stage1, stage2)


def channel_dft_tables(c):
    ang = 2.0 * np.pi * ((np.arange(c)[:, None] * np.arange(c)[None, :]) % c) / c
    return (jnp.asarray(np.cos(ang) / math.sqrt(c), BF16), jnp.asarray(np.sin(ang) / math.sqrt(c), BF16))


def _ffn_prologue(h, shf_ref, scf_ref, wr_ref, wsg_ref, wsu_ref, wsd_ref, h_ref, f_ref, lg_ref, shared_ref):
    h_ref[...] = h
    f = _rms(h) * (1.0 + scf_ref[...]) + shf_ref[...]
    packed = _pack_rows(f)
    dq = packed.shape[1] // ROW_PARTS
    for i in range(ROW_PARTS):
        f_ref[i] = packed[:, i * dq:(i + 1) * dq]
    fb = f.astype(BF16)
    f_lo = (f - fb.astype(F32)).astype(BF16)
    nt = (((1,), (1,)), ((), ()))
    wr = wr_ref[...]
    wr_hi = wr.astype(BF16)
    wr_lo = (wr - wr_hi.astype(F32)).astype(BF16)
    lg_ref[...] = (lax.dot_general(wr_hi, fb, nt, preferred_element_type=F32)
                   + lax.dot_general(wr_hi, f_lo, nt, preferred_element_type=F32)
                   + lax.dot_general(wr_lo, fb, nt, preferred_element_type=F32))
    hid = (_silu(jnp.dot(fb, wsg_ref[...], preferred_element_type=F32))
           * jnp.dot(fb, wsu_ref[...], preferred_element_type=F32))
    shared_ref[...] = jnp.dot(hid.astype(BF16), wsd_ref[...], preferred_element_type=F32).astype(shared_ref.dtype)


def _gelu_tanh(x):
    return 0.5 * x * (1.0 + jnp.tanh(math.sqrt(2.0 / math.pi) * (x + 0.044715 * (x * x * x))))


def _even_post_kernel(y_ref, na_ref, x_ref, wglu_ref, bglu_ref, wo_ref, gm_ref, *rest):
    g = _gelu_tanh(y_ref[...])
    gate = jax.nn.sigmoid(jnp.dot(g.astype(BF16), wglu_ref[...], preferred_element_type=F32) + bglu_ref[...])
    s5 = (g * gate).astype(BF16)
    w = s5.shape[1]
    mix = (jnp.dot(s5, wo_ref[:w, :], preferred_element_type=F32)
           + jnp.dot(na_ref[...], wo_ref[w:, :], preferred_element_type=F32))
    _ffn_prologue(x_ref[...] + gm_ref[...] * mix, *rest)


def _odd_post_kernel(zre_ref, zim_ref, h_ref_in, cc_ref, sc_ref, wf_ref, bf_ref, gm_ref, *rest):
    c = cc_ref.shape[0]
    parts = []
    for grp in range(zre_ref.shape[1] // c):
        cols = slice(grp * c, (grp + 1) * c)
        parts.append(jnp.dot(zre_ref[:, cols], cc_ref[...], preferred_element_type=F32)
                     + jnp.dot(zim_ref[:, cols], sc_ref[...], preferred_element_type=F32))
    fr = jnp.concatenate(parts, axis=-1).astype(BF16)
    mix = jnp.dot(fr, wf_ref[...], preferred_element_type=F32) + bf_ref[...]
    _ffn_prologue(h_ref_in[...] + gm_ref[...] * mix, *rest)


def _post_call(body, row_inputs, fixed_inputs, t, d, n_exp):
    tm = min(512, t)
    dq = d // 2 // ROW_PARTS
    row = lambda i: (i, 0)
    in_specs = ([pl.BlockSpec((tm, a.shape[1]), row) for a in row_inputs]
                + [pl.BlockSpec(a.shape, functools.partial(lambda i, nd: (0,) * nd, nd=a.ndim))
                   for a in fixed_inputs])
    return pl.pallas_call(
        body,
        grid=(t // tm,),
        in_specs=in_specs,
        out_specs=[pl.BlockSpec((tm, d), row), pl.BlockSpec((ROW_PARTS, tm, dq), lambda i: (0, i, 0)),
                   pl.BlockSpec((n_exp, tm), lambda i: (0, i)), pl.BlockSpec((tm, d), row)],
        out_shape=[jax.ShapeDtypeStruct((t, d), F32), jax.ShapeDtypeStruct((ROW_PARTS, t, dq), U32),
                   jax.ShapeDtypeStruct((n_exp, t), F32), jax.ShapeDtypeStruct((t, d), BF16)],
        compiler_params=_params("parallel"),
    )(*row_inputs, *fixed_inputs)


def _route_kernel(lg_ref, bias_ref, tri_ref, idx_ref, gate_ref, rank_ref, cnt_ref, run_ref):
    @pl.when(pl.program_id(0) == 0)
    def _():
        run_ref[...] = jnp.zeros_like(run_ref)

    scores = jax.nn.sigmoid(lg_ref[...])
    n_exp, tb = scores.shape
    sel = scores + bias_ref[...]
    gsz = n_exp // N_EXPERT_GROUPS
    member = lax.broadcasted_iota(I32, (gsz, tb), 0)
    gscore = []
    for grp in range(N_EXPERT_GROUPS):
        xg = sel[grp * gsz:(grp + 1) * gsz, :]
        m1 = jnp.max(xg, axis=0, keepdims=True)
        first = jnp.min(jnp.where(xg == m1, member, gsz), axis=0, keepdims=True)
        m2 = jnp.max(jnp.where(member == first, -jnp.inf, xg), axis=0, keepdims=True)
        gscore.append(m1 + m2)
    keep_rows = []
    for grp in range(N_EXPERT_GROUPS):
        beaten = jnp.zeros((1, tb), F32)
        for other in range(N_EXPERT_GROUPS):
            if other == grp:
                continue
            wins = (gscore[other] >= gscore[grp]) if other < grp else (gscore[other] > gscore[grp])
            beaten = beaten + jnp.where(wins, 1.0, 0.0)
        keep_rows.append(jnp.broadcast_to(beaten < TOPK_GROUPS, (gsz, tb)))
    masked = jnp.where(jnp.concatenate(keep_rows, axis=0), sel, -jnp.inf)

    expert = lax.broadcasted_iota(I32, (n_exp, tb), 0)
    picks, gates, hots = [], [], []
    chosen = jnp.zeros((n_exp, tb), F32)
    for _ in range(TOP_K):
        m = jnp.max(masked, axis=0, keepdims=True)
        pick = jnp.min(jnp.where(masked == m, expert, n_exp), axis=0, keepdims=True)
        hot = expert == pick
        picks.append(pick)
        hots.append(hot)
        gates.append(jnp.sum(jnp.where(hot, scores, 0.0), axis=0, keepdims=True))
        chosen = jnp.where(hot, 1.0, chosen)
        masked = jnp.where(hot, -jnp.inf, masked)
    total = gates[0]
    for gk in gates[1:]:
        total = total + gk
    ahead = jnp.dot(chosen.astype(BF16), tri_ref[...], preferred_element_type=F32) + run_ref[...]
    for k in range(TOP_K):
        idx_ref[k:k + 1, :] = picks[k]
        gate_ref[k:k + 1, :] = ROUTED_SCALE * gates[k] / total
        rank_ref[k:k + 1, :] = jnp.sum(jnp.where(hots[k], ahead, 0.0), axis=0, keepdims=True).astype(I32)
    run_ref[...] = run_ref[...] + jnp.sum(chosen, axis=1, keepdims=True)
    cnt_ref[...] = jnp.broadcast_to(run_ref[...], cnt_ref.shape)


def route(logits_t, router_bias):
    n_exp, t = logits_t.shape
    tb = min(512, t)
    tri = jnp.asarray(np.triu(np.ones((tb, tb), np.float32), k=1), BF16)
    tok = lambda i: (0, i)
    idx, gate, rank, cnt = pl.pallas_call(
        _route_kernel,
        grid=(t // tb,),
        in_specs=[pl.BlockSpec((n_exp, tb), tok), pl.BlockSpec((n_exp, 1), lambda i: (0, 0)),
                  pl.BlockSpec((tb, tb), lambda i: (0, 0))],
        out_specs=[pl.BlockSpec((TOP_K, tb), tok)] * 3 + [pl.BlockSpec((n_exp, 128), lambda i: (0, 0))],
        out_shape=[jax.ShapeDtypeStruct((TOP_K, t), I32), jax.ShapeDtypeStruct((TOP_K, t), F32),
                   jax.ShapeDtypeStruct((TOP_K, t), I32), jax.ShapeDtypeStruct((n_exp, 128), F32)],
        scratch_shapes=[pltpu.VMEM((n_exp, 1), F32)],
        compiler_params=_params("arbitrary"),
    )(logits_t, router_bias.astype(F32).reshape(n_exp, 1), tri)
    return idx, gate, rank, cnt[:, 0].astype(I32)


def dispatch_plan(idx, rank, counts, n_blocks):
    n_exp = counts.shape[0]
    shift = EXPERT_ROWS.bit_length() - 1
    assert EXPERT_ROWS == 1 << shift
    padded = ((counts + EXPERT_ROWS - 1) >> shift) << shift
    pad_end = jnp.cumsum(padded)
    pad_start = pad_end - padded
    experts_iota = jnp.arange(n_exp, dtype=I32)
    dest = jnp.sum(jnp.where(idx[..., None] == experts_iota, pad_start, 0), axis=-1) + rank
    n_valid = (pad_end[-1] >> shift).astype(I32)
    blk = jnp.minimum(jnp.arange(n_blocks, dtype=I32), n_valid - 1)
    blk_e = jnp.sum((pad_end[None, :] <= blk[:, None] * EXPERT_ROWS).astype(I32), axis=1)
    return dest.astype(I32), jnp.minimum(blk_e, n_exp - 1).astype(I32), n_valid.reshape(1)


def _sc_mesh():
    return plsc.VectorSubcoreMesh(core_axis_name="core", subcore_axis_name="subcore")


def dispatch(f, dest, cap):
    t, w = f.shape
    n_choice = dest.shape[0]

    @functools.partial(pl.kernel, out_type=jax.ShapeDtypeStruct((cap, w), f.dtype), mesh=_sc_mesh(),
                       scratch_types=[pltpu.SemaphoreType.DMA])
    def scatter_rows(x_hbm, i_hbm, o_hbm, sem):
        def body(x_vmem, i_vmem):
            copies = [pltpu.async_copy(x_vmem, o_hbm.at[i_vmem.at[k]], sem) for k in range(n_choice)]
            for cp in copies:
                cp.wait()

        pltpu.emit_pipeline(
            body,
            grid=(t // SC_WINDOW,),
            in_specs=[pl.BlockSpec((SC_WINDOW, w), lambda i: (i, 0)),
                      pl.BlockSpec((n_choice, SC_WINDOW), lambda i: (0, i))],
            out_specs=[],
            core_axis_name=("core", "subcore"),
            dimension_semantics=(pltpu.PARALLEL,),
        )(x_hbm, i_hbm)

    return scatter_rows(f, dest)


def gather_rows(rows, index_row):
    n = index_row.shape[1]
    w = rows.shape[1]

    piece = SC_WINDOW // SC_GATHER_SPLIT

    @functools.partial(pl.kernel, out_type=jax.ShapeDtypeStruct((n, w), rows.dtype), mesh=_sc_mesh(),
                       scratch_types=[pltpu.SemaphoreType.DMA])
    def gather(y_hbm, i_hbm, o_hbm, sem):
        def body(i_vmem, o_vmem):
            copies = [pltpu.async_copy(y_hbm.at[i_vmem.at[0, pl.ds(j * piece, piece)]],
                                       o_vmem.at[pl.ds(j * piece, piece)], sem)
                      for j in range(SC_GATHER_SPLIT)]
            for cp in copies:
                cp.wait()

        pltpu.emit_pipeline(
            body,
            grid=(n // SC_WINDOW,),
            in_specs=[pl.BlockSpec((1, SC_WINDOW), lambda i: (0, i))],
            out_specs=[pl.BlockSpec((SC_WINDOW, w), lambda i: (i, 0))],
            core_axis_name=("core", "subcore"),
            dimension_semantics=(pltpu.PARALLEL,),
        )(i_hbm, o_hbm)

    return gather(rows, index_row)


def _experts_kernel(be_ref, x_ref, wg_ref, wu_ref, wd_ref, y_ref, wg_bf, wu_bf, wd_bf):
    b = pl.program_id(0)

    @pl.when((b == 0) | (be_ref[b] != be_ref[jnp.maximum(b - 1, 0)]))
    def _():
        wg_bf[...] = wg_ref[0, 0].astype(BF16)
        wu_bf[...] = wu_ref[0, 0].astype(BF16)
        wd_bf[...] = wd_ref[0, 0].astype(BF16)

    x = _unpack_rows(jnp.concatenate([x_ref[i] for i in range(ROW_PARTS)], axis=1)).astype(BF16)
    hid = (_silu(jnp.dot(x, wg_bf[...], preferred_element_type=F32))
           * jnp.dot(x, wu_bf[...], preferred_element_type=F32))
    packed = _pack_rows(jnp.dot(hid.astype(BF16), wd_bf[...], preferred_element_type=F32))
    dq = packed.shape[1] // ROW_PARTS
    for i in range(ROW_PARTS):
        y_ref[i] = packed[:, i * dq:(i + 1) * dq]


def experts(rows, blk_e, n_valid, layer, w_gate, w_up, w_down):
    parts, cap, dq = rows.shape
    _, _, d, ff = w_gate.shape
    blk = pl.BlockSpec((parts, EXPERT_ROWS, dq), lambda b, be: (0, b, 0))
    return pl.pallas_call(
        _experts_kernel,
        grid_spec=pltpu.PrefetchScalarGridSpec(
            num_scalar_prefetch=1,
            grid=(n_valid[0],),
            in_specs=[blk,
                      pl.BlockSpec((1, 1, d, ff), lambda b, be: (layer, be[b], 0, 0)),
                      pl.BlockSpec((1, 1, d, ff), lambda b, be: (layer, be[b], 0, 0)),
                      pl.BlockSpec((1, 1, ff, d), lambda b, be: (layer, be[b], 0, 0))],
            out_specs=blk,
            scratch_shapes=[pltpu.VMEM((d, ff), BF16), pltpu.VMEM((d, ff), BF16), pltpu.VMEM((ff, d), BF16)],
        ),
        out_shape=jax.ShapeDtypeStruct(rows.shape, U32),
        compiler_params=_params("arbitrary"),
    )(blk_e, rows, w_gate, w_up, w_down)


def _combine_kernel(gate_ref, h_ref, shared_ref, gf_ref, sh_ref, sc_ref, *rest):
    y_refs, outs = rest[:TOP_K], rest[TOP_K:]
    routed = None
    for k in range(TOP_K):
        y = _unpack_rows(jnp.concatenate([y_refs[k][i, 0] for i in range(ROW_PARTS)], axis=1))
        routed = gate_ref[:, k:k + 1] * y if routed is None else routed + gate_ref[:, k:k + 1] * y
    h = h_ref[...] + gf_ref[...] * (routed + shared_ref[...].astype(F32))
    outs[0][...] = h
    if len(outs) > 1:
        outs[1][...] = _rms(h) * (1.0 + sc_ref[...]) + sh_ref[...]


def combine(y_rows, dest, gate_tk, h, shared, gate_ffn, next_shift, next_scale, with_next):
    t, d = h.shape
    parts, cap, dq = y_rows.shape
    n_out = 2 if with_next else 1
    n_choice = dest.shape[0]
    flat = (dest[None] + (jnp.arange(parts, dtype=I32) * cap)[:, None, None]).reshape(1, parts * n_choice * t)
    picked = gather_rows(y_rows.reshape(parts * cap, dq), flat).reshape(parts, n_choice, t, dq)
    tm = min(256, t)
    row = lambda i: (i, 0)
    fixed = lambda i: (0, 0)
    choice = [pl.BlockSpec((parts, 1, tm, dq), functools.partial(lambda i, k: (0, k, i, 0), k=k))
              for k in range(n_choice)]
    return pl.pallas_call(
        _combine_kernel,
        grid=(t // tm,),
        in_specs=[pl.BlockSpec((tm, n_choice), row), pl.BlockSpec((tm, d), row), pl.BlockSpec((tm, d), row),
                  pl.BlockSpec((1, d), fixed), pl.BlockSpec((1, d), fixed), pl.BlockSpec((1, d), fixed)] + choice,
        out_specs=[pl.BlockSpec((tm, d), row)] * n_out,
        out_shape=[jax.ShapeDtypeStruct((t, d), F32)] * n_out,
        compiler_params=_params("parallel"),
    )(gate_tk, h, shared, gate_ffn, next_shift, next_scale, *([picked] * n_choice))


def moe_tail(h, f_rows, logits_t, shared, router_bias, layer, w_gate, w_up, w_down, gate_ffn,
             next_shift, next_scale, with_next):
    t, d = h.shape
    parts, _, dq = f_rows.shape
    n_exp = w_gate.shape[1]
    n_blocks = -(-(t * TOP_K + n_exp * (EXPERT_ROWS - 1)) // EXPERT_ROWS)
    cap = n_blocks * EXPERT_ROWS
    idx, gate, rank, counts = route(logits_t, router_bias)
    dest, blk_e, n_valid = dispatch_plan(idx, rank, counts, n_blocks)
    dest_all = jnp.concatenate([dest + i * cap for i in range(parts)], axis=1)
    rows = dispatch(f_rows.reshape(parts * t, dq), dest_all, parts * cap).reshape(parts, cap, dq)
    y_rows = experts(rows, blk_e, n_valid, layer, w_gate, w_up, w_down)
    return combine(y_rows, dest, gate.T, h, shared, gate_ffn, next_shift, next_scale, with_next)


def kernel(x, c, ctx, c_ctx, w_ada, b_ada, w_in, s5_lam_re, s5_lam_im, s5_log_dt, s5_b_re, s5_b_im,
           s5_c_re, s5_c_im, s5_d, s5_w_glu, s5_b_glu, na_q_gain, na_k_gain, na_rpb, w_mix_out,
           w_fourier_out, b_fourier_out, w_router, router_bias, w_exp_gate, w_exp_up, w_exp_down,
           w_sh_gate, w_sh_up, w_sh_down):
    bsz, t, d = x.shape
    assert bsz == 1 and w_ada.shape[0] == 2
    n_exp = w_router.shape[2]
    s5w = s5_w_glu.shape[1]
    naw = w_in.shape[2] - s5w
    naw //= 3
    heads = naw // NA_HEAD_DIM

    cond8 = jnp.concatenate([c[:1].astype(F32), c_ctx.astype(F32)[None], jnp.zeros((6, d), F32)], axis=0)
    ada = adaln_all(cond8, w_ada, b_ada)
    mod = lambda layer, who, j: ada[layer, who:who + 1, j * d:(j + 1) * d]

    def ffn_weights(i):
        return (mod(i, 0, 3), mod(i, 0, 4), jnp.transpose(w_router[i]).astype(F32),
                w_sh_gate[i].astype(BF16), w_sh_up[i].astype(BF16), w_sh_down[i].astype(BF16))

    h0 = x[0]
    seg = jnp.asarray(np.kron(np.eye(heads), np.ones((NA_HEAD_DIM, NA_HEAD_DIM))), BF16)
    w_in_b = w_in[0].astype(BF16)
    qg = jnp.tile(na_q_gain[0].astype(F32), heads)[None]
    kg = jnp.tile(na_k_gain[0].astype(F32), heads)[None]
    u_c, _, k_c, v_c = in_projection(ctx[0], mod(0, 1, 0), mod(0, 1, 1), w_in_b, seg, qg, kg, s5w, naw)
    u_l, q_l, k_l, v_l = in_projection(h0, mod(0, 0, 0), mod(0, 0, 1), w_in_b, seg, qg, kg, s5w, naw)
    mats = s5_matrices(s5_lam_re[0], s5_lam_im[0], s5_log_dt[0], s5_b_re[0], s5_b_im[0],
                       s5_c_re[0], s5_c_im[0], s5_d[0])
    y_s5 = s5_mixer(u_c, u_l, mats)
    na = neighbourhood_attention(q_l, k_l, v_l, k_c, v_c, na_bias_table(na_rpb[0], t // GRID_W))
    h1, f1, lg1, sh1 = _post_call(
        _even_post_kernel, [y_s5, na, h0],
        [s5_w_glu[0].astype(BF16), s5_b_glu[0].astype(F32)[None], w_mix_out[0].astype(BF16), mod(0, 0, 2),
         *ffn_weights(0)], t, d, n_exp)
    h2, a1 = moe_tail(h1, f1, lg1, sh1, router_bias[0], 0, w_exp_gate, w_exp_up, w_exp_down,
                      mod(0, 0, 5), mod(1, 0, 0), mod(1, 0, 1), True)

    zre, zim = time_dft(a1)
    cc, sc = channel_dft_tables(d // FOURIER_GROUPS)
    h3, f3, lg3, sh3 = _post_call(
        _odd_post_kernel, [zre, zim, h2],
        [cc, sc, w_fourier_out[0].astype(BF16), b_fourier_out[0].astype(F32)[None], mod(1, 0, 2),
         *ffn_weights(1)], t, d, n_exp)
    zero_row = jnp.zeros((1, d), F32)
    (out,) = moe_tail(h3, f3, lg3, sh3, router_bias[1], 1, w_exp_gate, w_exp_up, w_exp_down,
                      mod(1, 0, 5), zero_row, zero_row, False)
    return out[None]
```

```python
import functools
import math

import numpy as np
import jax
import jax.numpy as jnp
from jax import lax
from jax.experimental import pallas as pl
from jax.experimental.pallas import tpu as pltpu
from jax.experimental.pallas import tpu_sc as plsc

F32 = jnp.float32
BF16 = jnp.bfloat16
I32 = jnp.int32
U32 = jnp.uint32
HIGHEST = lax.Precision.HIGHEST

MXU_DEPTH = 256
LANES = 128
GRID_W = 64
NORM_EPS = 1e-6
S5_GROUP = 16
S5_LAMBDA_RE_MAX = -1e-4
S5_CHUNK = 16
S5_TILE = 128
NA_HEAD_DIM = 64
NA_KH = 8
NA_KW = 16
NA_STEP_ROWS = 4
FOURIER_GROUPS = 4
N_EXPERT_GROUPS = 8
TOPK_GROUPS = 4
TOP_K = 8
ROUTED_SCALE = 2.5
EXPERT_ROWS = 2048
ROW_PARTS = 2
SC_WINDOW = 128
SC_GATHER_SPLIT = 4
NEG_BIG = -1e30

VMEM_LIMIT_BYTES = 56 * 1024 * 1024


def _params(*sem):
    return pltpu.CompilerParams(dimension_semantics=sem or None,
                                vmem_limit_bytes=VMEM_LIMIT_BYTES)


def _rms(x):
    return x * lax.rsqrt(jnp.mean(x * x, axis=-1, keepdims=True) + NORM_EPS)


def _silu(x):
    return x * jax.nn.sigmoid(x)


def _pack_pair(lo, hi):
    lo = lax.bitcast_convert_type(lo.astype(BF16).astype(F32), U32)
    hi = lax.bitcast_convert_type(hi.astype(BF16).astype(F32), U32)
    return (hi & jnp.uint32(0xFFFF0000)) | (lo >> 16)


def _unpack_pair(w):
    return (lax.bitcast_convert_type(w << 16, F32), lax.bitcast_convert_type(w & jnp.uint32(0xFFFF0000), F32))


def _pack_rows(x):
    n = x.shape[1] // 2
    return _pack_pair(x[:, :n], x[:, n:])


def _unpack_rows(w):
    return jnp.concatenate(_unpack_pair(w), axis=1)


def _ada_kernel(c_ref, w_ref, b_ref, o_ref):
    o_ref[0] = jnp.dot(_silu(c_ref[...]), w_ref[0], preferred_element_type=F32,
                       precision=HIGHEST) + b_ref[0]


def adaln_all(cond8, w_ada, b_ada):
    n_layers, d, n6 = w_ada.shape
    tn = n6 // 4
    return pl.pallas_call(
        _ada_kernel,
        grid=(n_layers, n6 // tn),
        in_specs=[pl.BlockSpec((8, d), lambda l, j: (0, 0)),
                  pl.BlockSpec((1, d, tn), lambda l, j: (l, 0, j)),
                  pl.BlockSpec((1, 1, tn), lambda l, j: (l, 0, j))],
        out_specs=pl.BlockSpec((1, 8, tn), lambda l, j: (l, 0, j)),
        out_shape=jax.ShapeDtypeStruct((n_layers, 8, n6), F32),
        compiler_params=_params("parallel", "parallel"),
    )(cond8, w_ada, b_ada.reshape(n_layers, 1, n6))


def _inproj_kernel(x_ref, sh_ref, sc_ref, w_ref, seg_ref, qg_ref, kg_ref,
                   u_ref, q_ref, k_ref, v_ref):
    a = _rms(x_ref[...]) * (1.0 + sc_ref[...]) + sh_ref[...]
    z = jnp.dot(a.astype(BF16), w_ref[...], preferred_element_type=F32)
    s5w = u_ref.shape[1]
    naw = q_ref.shape[1]

    def head_norm(t, gain):
        ss = jnp.dot((t * t).astype(BF16), seg_ref[...], preferred_element_type=F32)
        return t * lax.rsqrt(ss * (1.0 / NA_HEAD_DIM) + NORM_EPS) * gain

    u_ref[...] = z[:, :s5w]
    q_ref[...] = head_norm(z[:, s5w:s5w + naw], qg_ref[...]).astype(BF16)
    k_ref[...] = head_norm(z[:, s5w + naw:s5w + 2 * naw], kg_ref[...]).astype(BF16)
    v_ref[...] = z[:, s5w + 2 * naw:].astype(BF16)


def in_projection(x, shift, scale, w_in_bf16, seg_ones, q_gain_row, k_gain_row, s5w, naw):
    t, d = x.shape
    tm = min(512, t)
    row = lambda i: (i, 0)
    fixed = lambda i: (0, 0)
    return pl.pallas_call(
        _inproj_kernel,
        grid=(t // tm,),
        in_specs=[pl.BlockSpec((tm, d), row),
                  pl.BlockSpec((1, d), fixed), pl.BlockSpec((1, d), fixed),
                  pl.BlockSpec(w_in_bf16.shape, fixed),
                  pl.BlockSpec(seg_ones.shape, fixed),
                  pl.BlockSpec((1, naw), fixed), pl.BlockSpec((1, naw), fixed)],
        out_specs=[pl.BlockSpec((tm, s5w), row), pl.BlockSpec((tm, naw), row),
                   pl.BlockSpec((tm, naw), row), pl.BlockSpec((tm, naw), row)],
        out_shape=[jax.ShapeDtypeStruct((t, s5w), F32)] + [jax.ShapeDtypeStruct((t, naw), BF16)] * 3,
        compiler_params=_params("parallel"),
    )(x, shift, scale, w_in_bf16, seg_ones, q_gain_row, k_gain_row)


def s5_matrices(lam_re, lam_im, log_dt, b_re, b_im, c_re, c_im, d_skip):
    L = S5_CHUNK
    c = S5_GROUP
    taus = jnp.arange(L + 1, dtype=F32)

    def direction(i):
        lam = lax.complex(jnp.minimum(lam_re[i].astype(F32), S5_LAMBDA_RE_MAX), lam_im[i].astype(F32))
        ldt = lam * jnp.exp(log_dt[i].astype(F32))[:, None]
        lam_bar = jnp.exp(ldt)
        b_bar = ((lam_bar - 1.0) / lam)[..., None] * lax.complex(b_re[i].astype(F32), b_im[i].astype(F32))
        cc = lax.complex(c_re[i].astype(F32), c_im[i].astype(F32))
        powers = jnp.exp(ldt[None] * taus[:, None, None])
        resp = jnp.real(jnp.einsum('gcp,tgp,gpd->gctd', cc, powers[:L], b_bar, precision=HIGHEST))
        return jnp.transpose(powers, (1, 2, 0)), b_bar, cc, resp

    pw_f, bb_f, cc_f, k_f = direction(0)
    pw_b, bb_b, cc_b, k_b = direction(1)
    g, p = pw_f.shape[:2]
    lag0 = k_f[:, :, :1] + k_b[:, :, :1] + (jnp.eye(c, dtype=F32)[None, :, None, :] * d_skip.astype(F32)[:, :, None, None])
    by_lag = jnp.concatenate([jnp.flip(k_f[:, :, 1:], axis=2), lag0, k_b[:, :, 1:]], axis=2).reshape(g, c, (2 * L - 1) * c)
    n = (2 * L - 1) * c
    windows = jnp.tile(by_lag, (1, 1, L + 1))[:, :, :L * (n + c)].reshape(g, c, L, n + c)[..., :L * c]
    toep_t = jnp.transpose(jnp.flip(windows, axis=2), (0, 2, 1, 3)).reshape(g, L * c, L * c)

    def state_in(pw_by_s, b_bar):
        return (pw_by_s[:, :, :, None] * b_bar[:, :, None, :]).reshape(g, p, L * c)

    wf = state_in(jnp.flip(pw_f[:, :, :L], axis=2), bb_f)
    wb = state_in(pw_b[:, :, :L], bb_b)
    w_state_t = jnp.concatenate([jnp.real(wf), jnp.imag(wf), jnp.imag(wf), jnp.real(wf),
                                 jnp.real(wb), jnp.imag(wb), jnp.imag(wb), jnp.real(wb)], axis=1)

    def state_out(pw_by_l, cc):
        return (jnp.transpose(pw_by_l, (0, 2, 1))[:, :, None, :] * cc[:, None, :, :]).reshape(g, L * c, p)

    rf = state_out(pw_f[:, :, 1:], cc_f)
    rb = state_out(jnp.flip(pw_b[:, :, 1:], axis=2), cc_b)
    r_state_t = jnp.concatenate([jnp.real(rf), -jnp.imag(rf), jnp.real(rb), -jnp.imag(rb)], axis=-1)

    def mult(a):
        ar, ai = jnp.real(a), jnp.imag(a)
        return jnp.stack([jnp.concatenate([ar, ar], -1), jnp.concatenate([-ai, ai], -1),
                          jnp.concatenate([ai, -ai], -1)])

    return (toep_t.astype(BF16), w_state_t.astype(BF16), r_state_t.astype(BF16),
            mult(pw_f[:, :, L]), mult(pw_b[:, :, L]))


def _s5_pack_kernel(*refs):
    u_refs, (wt_ref, ut_ref, f1_ref, f2_ref, b1_ref, b2_ref) = refs[:-6], refs[-6:]
    L = S5_CHUNK
    g, lc, nck = ut_ref.shape
    c = lc // L
    gs = g // len(u_refs)
    for s in range(L):
        for j, u_ref in enumerate(u_refs):
            step_s = u_ref[pl.ds(s, nck, stride=L), :]
            ut_ref[j * gs:(j + 1) * gs, s * c:(s + 1) * c, :] = (
                jnp.transpose(step_s).astype(BF16).reshape(gs, c, nck))
    n = f1_ref.shape[2]
    for gi in range(g):
        inc = jnp.dot(wt_ref[gi], ut_ref[gi], preferred_element_type=F32)
        for j, ref in enumerate((f1_ref, f2_ref, b1_ref, b2_ref)):
            ref[:, gi, :] = jnp.transpose(inc[j * n:(j + 1) * n, :])


def s5_pack(u, wt_state):
    t, w = u.shape
    g, n4, lc = wt_state.shape
    nc = t // S5_CHUNK
    tile = min(S5_TILE, nc)
    inc = jax.ShapeDtypeStruct((nc, g, n4 // 4), F32)
    inc_blk = pl.BlockSpec((tile, g, n4 // 4), lambda i: (i, 0, 0))
    return pl.pallas_call(
        _s5_pack_kernel,
        grid=(nc // tile,),
        in_specs=[pl.BlockSpec((tile * S5_CHUNK, LANES), functools.partial(lambda i, j: (i, j), j=j))
                  for j in range(w // LANES)] + [pl.BlockSpec(wt_state.shape, lambda i: (0, 0, 0))],
        out_specs=[pl.BlockSpec((g, lc, tile), lambda i: (0, 0, i))] + [inc_blk] * 4,
        out_shape=[jax.ShapeDtypeStruct((g, lc, nc), BF16)] + [inc] * 4,
        compiler_params=_params("parallel"),
    )(*([u] * (w // LANES)), wt_state)


def _s5_scan_kernel(s1_ref, s2_ref, m_ref, init_ref, x_ref, last_ref, v1_ref, v2_ref, *, reverse):
    @pl.when(pl.program_id(0) == 0)
    def _():
        v1_ref[...] = init_ref[0]
        v2_ref[...] = init_ref[1]

    a1, a2, a3 = m_ref[0], m_ref[1], m_ref[2]
    cb = s1_ref.shape[0]

    def body(j, carry):
        v1, v2 = carry
        jj = cb - 1 - j if reverse else j
        x_ref[jj] = v1
        return (a1 * v1 + a2 * v2 + s1_ref[jj], a1 * v2 + a3 * v1 + s2_ref[jj])

    v1, v2 = lax.fori_loop(0, cb, body, (v1_ref[...], v2_ref[...]))
    v1_ref[...] = v1
    v2_ref[...] = v2
    last_ref[...] = v1


def s5_chunk_scan(s1, s2, mult, init, reverse):
    nc, g, n = s1.shape
    cb = min(S5_TILE, nc)
    nb = nc // cb
    blk = (lambda i: (nb - 1 - i, 0, 0)) if reverse else (lambda i: (i, 0, 0))
    return pl.pallas_call(
        functools.partial(_s5_scan_kernel, reverse=reverse),
        grid=(nb,),
        in_specs=[pl.BlockSpec((cb, g, n), blk), pl.BlockSpec((cb, g, n), blk),
                  pl.BlockSpec((3, g, n), lambda i: (0, 0, 0)), pl.BlockSpec((2, g, n), lambda i: (0, 0, 0))],
        out_specs=[pl.BlockSpec((cb, g, n), blk), pl.BlockSpec((g, n), lambda i: (0, 0))],
        out_shape=[jax.ShapeDtypeStruct((nc, g, n), F32), jax.ShapeDtypeStruct((g, n), F32)],
        scratch_shapes=[pltpu.VMEM((g, n), F32), pltpu.VMEM((g, n), F32)],
        compiler_params=_params("arbitrary"),
    )(s1, s2, mult, init)


def _s5_readout_kernel(ut_ref, tt_ref, rt_ref, xf_ref, xb_ref, y_ref, yt_ref, *slab_refs):
    L = S5_CHUNK
    g, lc, nck = ut_ref.shape
    c = lc // L
    gs = g // len(slab_refs)
    for gi in range(g):
        xin_t = jnp.concatenate([jnp.transpose(xf_ref[:, gi, :]), jnp.transpose(xb_ref[:, gi, :])], axis=0)
        yt_ref[gi] = (jnp.dot(tt_ref[gi], ut_ref[gi], preferred_element_type=F32)
                      + jnp.dot(rt_ref[gi], xin_t.astype(BF16), preferred_element_type=F32))
    for j, slab in enumerate(slab_refs):
        for l in range(L):
            step_l = yt_ref[j * gs:(j + 1) * gs, l * c:(l + 1) * c, :].reshape(gs * c, nck)
            slab[pl.ds(l, nck, stride=L), :] = jnp.transpose(step_l)
        y_ref[:, j * gs * c:(j + 1) * gs * c] = slab[...]


def s5_readout(ut, toep_t, r_state_t, xin_f, xin_b):
    g, lc, nc = ut.shape
    n = xin_f.shape[2]
    tile = min(S5_TILE, nc)
    fixed = lambda i: (0, 0, 0)
    state_blk = pl.BlockSpec((tile, g, n), lambda i: (i, 0, 0))
    return pl.pallas_call(
        _s5_readout_kernel,
        grid=(nc // tile,),
        in_specs=[pl.BlockSpec((g, lc, tile), lambda i: (0, 0, i)),
                  pl.BlockSpec(toep_t.shape, fixed), pl.BlockSpec(r_state_t.shape, fixed), state_blk, state_blk],
        out_specs=pl.BlockSpec((tile * S5_CHUNK, g * lc // S5_CHUNK), lambda i: (i, 0)),
        out_shape=jax.ShapeDtypeStruct((nc * S5_CHUNK, g * lc // S5_CHUNK), F32),
        scratch_shapes=[pltpu.VMEM((g, lc, tile), F32)]
        + [pltpu.VMEM((tile * S5_CHUNK, LANES), F32)] * (g * lc // S5_CHUNK // LANES),
        compiler_params=_params("parallel"),
    )(ut, toep_t, r_state_t, xin_f, xin_b)


def s5_mixer(u_ctx, u_lat, mats):
    toep_t, wt_state, r_state_t, mult_f, mult_b = mats
    L = S5_CHUNK
    g, n = mult_f.shape[1:]
    halves = lambda v: jnp.stack([v, jnp.roll(v, n // 2, axis=-1)])
    n_ctx = u_ctx.shape[0] // L
    ctx_chunks = -(-(n_ctx + 1) // S5_TILE) * S5_TILE
    ctx_pad = jnp.pad(u_ctx, ((0, ctx_chunks * L - u_ctx.shape[0]), (0, 0)))
    _, cf1, cf2, cb1, cb2 = s5_pack(ctx_pad, wt_state)
    zero = jnp.zeros((2, g, n), F32)
    ctx_f, _ = s5_chunk_scan(cf1, cf2, mult_f, zero, False)
    _, ctx_b_last = s5_chunk_scan(cb1, cb2, mult_b, zero, True)
    ut, f1, f2, b1, b2 = s5_pack(u_lat, wt_state)
    xin_f, _ = s5_chunk_scan(f1, f2, mult_f, halves(ctx_f[n_ctx]), False)
    xin_b, _ = s5_chunk_scan(b1, b2, mult_b, halves(ctx_b_last), True)
    return s5_readout(ut, toep_t, r_state_t, xin_f, xin_b)


def _na_window(rows):
    step = NA_STEP_ROWS
    assert rows % step == 0 and (NA_KH // 2) % step == 0 and rows >= NA_KH + step
    groups = rows // step
    n_blk = (NA_KH + step) // step
    first = np.clip(np.arange(groups) - NA_KH // 2 // step, 0, groups - n_blk)
    return groups, n_blk, first


def na_bias_table(rpb, rows):
    step = NA_STEP_ROWS
    groups, n_blk, first = _na_window(rows)
    win = n_blk * step
    q_col = np.arange(GRID_W)
    col_start = np.clip(q_col - NA_KW // 2, 0, GRID_W - NA_KW)
    key_col = np.arange(GRID_W)
    off = key_col[None, :] - col_start[:, None]
    valid_col = (off >= 0) & (off < NA_KW)
    rel_col = np.clip(key_col[None, :] - q_col[:, None] + NA_KW - 1, 0, 2 * NA_KW - 2)
    q_row = step * np.arange(groups)[:, None, None] + np.arange(step)[None, :, None]
    key_row = step * first[:, None, None] + np.arange(win)[None, None, :]
    row0 = np.clip(q_row - NA_KH // 2, 0, rows - NA_KH)
    rel_row = np.where((key_row >= row0) & (key_row < row0 + NA_KH), key_row - q_row + NA_KH - 1, -1)
    variant = np.arange(groups) - first
    reps = [int(np.argmax(variant == d)) for d in range(n_blk)]
    assert all((rel_row[g] == rel_row[reps[variant[g]]]).all() for g in range(groups))
    rel_row = rel_row[reps]
    pick_col = jnp.asarray(np.arange(2 * NA_KW - 1)[:, None, None] == rel_col[None], F32)
    tiles = jnp.where(valid_col[None, None], jnp.einsum('hrx,xck->rhck', rpb.astype(F32), pick_col, precision=HIGHEST),
                      NEG_BIG)
    tiles = jnp.concatenate([tiles, jnp.full_like(tiles[:1], NEG_BIG)], axis=0)
    return jnp.stack([jnp.concatenate([jnp.concatenate([tiles[int(rel_row[v, j, i])] for i in range(win)], axis=-1)
                                       for j in range(step)], axis=1) for v in range(n_blk)])


def _na_kernel(*refs, scale, n_blk):
    q_ref = refs[0]
    k_refs = refs[1:1 + n_blk]
    v_refs = refs[1 + n_blk:1 + 2 * n_blk]
    kc_ref, vc_ref, b_ref, o_ref = refs[1 + 2 * n_blk:]
    hd = NA_HEAD_DIM
    width = q_ref.shape[1]
    span = min(MXU_DEPTH, width)
    nt = (((1,), (1,)), ((), ()))
    q = q_ref[...] * scale
    kk = jnp.concatenate([r[...] for r in k_refs], axis=0)
    vv = jnp.concatenate([r[...] for r in v_refs], axis=0)
    kc, vc = kc_ref[...], vc_ref[...]
    nq = q.shape[0]
    per = span // hd
    lane = lax.broadcasted_iota(I32, (nq, span), 1)
    own = [(lane >= j * hd) & (lane < (j + 1) * hd) for j in range(per)]
    out_cols = []
    for c0 in range(0, width, span):
        cols = slice(c0, c0 + span)
        qs = jnp.concatenate([jnp.where(own[j], q[:, cols], jnp.zeros_like(q[:, cols])) for j in range(per)], axis=0)
        h0 = c0 // hd
        bias = b_ref[0, h0:h0 + per].reshape(per * nq, kk.shape[0])
        s = lax.dot_general(qs, kk[:, cols], nt, preferred_element_type=F32) + bias
        sc = lax.dot_general(qs, kc[:, cols], nt, preferred_element_type=F32)
        m = jnp.maximum(jnp.max(s, axis=-1, keepdims=True), jnp.max(sc, axis=-1, keepdims=True))
        p = jnp.exp(s - m)
        pc = jnp.exp(sc - m)
        den = jnp.sum(p, axis=-1, keepdims=True) + jnp.sum(pc, axis=-1, keepdims=True)
        o = (jnp.dot(p.astype(BF16), vv[:, cols], preferred_element_type=F32)
             + jnp.dot(pc.astype(BF16), vc[:, cols], preferred_element_type=F32)) / den
        acc = jnp.zeros((nq, span), F32)
        for j in range(per):
            acc = jnp.where(own[j], o[j * nq:(j + 1) * nq], acc)
        out_cols.append(acc)
    o_ref[...] = jnp.concatenate(out_cols, axis=1).astype(o_ref.dtype)


def neighbourhood_attention(q, k, v, k_ctx, v_ctx, bias_table):
    t, w = q.shape
    groups, n_blk, _ = _na_window(t // GRID_W)
    first = lambda g: jnp.clip(g - NA_KH // 2 // NA_STEP_ROWS, 0, groups - n_blk)
    blk_tokens = NA_STEP_ROWS * GRID_W
    row_blk = pl.BlockSpec((blk_tokens, w), lambda g: (g, 0))
    key_blks = [pl.BlockSpec((blk_tokens, w), functools.partial(lambda g, i: (first(g) + i, 0), i=i))
                for i in range(n_blk)]
    ctx_blk = pl.BlockSpec(k_ctx.shape, lambda g: (0, 0))
    bias_blk = pl.BlockSpec((1,) + bias_table.shape[1:], lambda g: (g - first(g), 0, 0, 0))
    return pl.pallas_call(
        functools.partial(_na_kernel, scale=NA_HEAD_DIM ** -0.5, n_blk=n_blk),
        grid=(groups,),
        in_specs=[row_blk] + key_blks + key_blks + [ctx_blk, ctx_blk, bias_blk],
        out_specs=row_blk,
        out_shape=jax.ShapeDtypeStruct((t, w), BF16),
        compiler_params=_params("parallel"),
    )(q, *([k] * n_blk), *([v] * n_blk), k_ctx, v_ctx, bias_table)


def time_dft_tables(t):
    a_len = 1 << (int(math.log2(t)) // 2)
    b_len = t // a_len
    ka = np.arange(a_len)[:, None]
    tok = b_len * np.arange(a_len)[None, :]
    ang1 = -2.0 * np.pi * ((ka * (tok[None] + np.arange(b_len)[:, None, None])) % t) / t
    stage1 = np.concatenate([np.cos(ang1), np.sin(ang1)], axis=1) / math.sqrt(t)
    ang2 = 2.0 * np.pi * ((np.arange(b_len)[:, None] * np.arange(b_len)[None, :]) % b_len) / b_len
    c2, s2 = np.cos(ang2), np.sin(ang2)
    stage2 = np.block([[c2, s2], [-s2, c2]])
    return jnp.asarray(stage1, BF16), jnp.asarray(stage2, BF16), a_len, b_len


def _time_dft_kernel(x_ref, m_ref, w2_ref, zre_ref, zim_ref, y_ref, z_ref, *, a_len, b_len):
    i = pl.program_id(1)
    bb = m_ref.shape[0]

    def stage1(jb, carry):
        b = i * bb + jb
        xb = x_ref[pl.ds(b, a_len, stride=b_len), :]
        y = jnp.dot(m_ref[jb], xb.astype(BF16), preferred_element_type=F32)
        row = pl.multiple_of(b * a_len, a_len)
        y_ref[pl.ds(row, a_len), :] = _pack_pair(y[:a_len], y[a_len:])
        return carry

    lax.fori_loop(0, bb, stage1, 0, unroll=16)

    @pl.when(i == pl.num_programs(1) - 1)
    def _():
        def stage2(ka, carry):
            rows = pl.ds(ka, b_len, stride=a_len)
            yre, yim = _unpack_pair(y_ref[rows, :])
            y = jnp.concatenate([yre, yim], axis=0).astype(BF16)
            z = jnp.dot(w2_ref[...], y, preferred_element_type=F32)
            z_ref[rows, :] = _pack_pair(z[:b_len], z[b_len:])
            return carry

        lax.fori_loop(0, a_len, stage2, 0, unroll=16)
        zre, zim = _unpack_pair(z_ref[...])
        zre_ref[...] = zre.astype(zre_ref.dtype)
        zim_ref[...] = zim.astype(zim_ref.dtype)


def time_dft(x):
    t, d = x.shape
    stage1, stage2, a_len, b_len = time_dft_tables(t)
    lanes = 128
    bb = min(16, b_len)
    out = jax.ShapeDtypeStruct((t, d), BF16)
    return pl.pallas_call(
        functools.partial(_time_dft_kernel, a_len=a_len, b_len=b_len),
        grid=(d // lanes, b_len // bb),
        in_specs=[pl.BlockSpec((t, lanes), lambda j, i: (0, j)),
                  pl.BlockSpec((bb, 2 * a_len, a_len), lambda j, i: (i, 0, 0)),
                  pl.BlockSpec(stage2.shape, lambda j, i: (0, 0))],
        out_specs=[pl.BlockSpec((t, lanes), lambda j, i: (0, j))] * 2,
        out_shape=[out, out],
        scratch_shapes=[pltpu.VMEM((t, lanes), U32), pltpu.VMEM((t, lanes), U32)],
        compiler_params=_params("parallel", "arbitrary"),
    )(x, stage1, stage2)


def channel_dft_tables(c):
    ang = 2.0 * np.pi * ((np.arange(c)[:, None] * np.arange(c)[None, :]) % c) / c
    return (jnp.asarray(np.cos(ang) / math.sqrt(c), BF16), jnp.asarray(np.sin(ang) / math.sqrt(c), BF16))


def _ffn_prologue(h, shf_ref, scf_ref, wr_ref, wsg_ref, wsu_ref, wsd_ref, h_ref, f_ref, lg_ref, shared_ref):
    h_ref[...] = h
    f = _rms(h) * (1.0 + scf_ref[...]) + shf_ref[...]
    packed = _pack_rows(f)
    dq = packed.shape[1] // ROW_PARTS
    for i in range(ROW_PARTS):
        f_ref[i] = packed[:, i * dq:(i + 1) * dq]
    fb = f.astype(BF16)
    f_lo = (f - fb.astype(F32)).astype(BF16)
    nt = (((1,), (1,)), ((), ()))
    wr = wr_ref[...]
    wr_hi = wr.astype(BF16)
    wr_lo = (wr - wr_hi.astype(F32)).astype(BF16)
    lg_ref[...] = (lax.dot_general(wr_hi, fb, nt, preferred_element_type=F32)
                   + lax.dot_general(wr_hi, f_lo, nt, preferred_element_type=F32)
                   + lax.dot_general(wr_lo, fb, nt, preferred_element_type=F32))
    hid = (_silu(jnp.dot(fb, wsg_ref[...], preferred_element_type=F32))
           * jnp.dot(fb, wsu_ref[...], preferred_element_type=F32))
    shared_ref[...] = jnp.dot(hid.astype(BF16), wsd_ref[...], preferred_element_type=F32).astype(shared_ref.dtype)


def _gelu_tanh(x):
    return 0.5 * x * (1.0 + jnp.tanh(math.sqrt(2.0 / math.pi) * (x + 0.044715 * (x * x * x))))


def _even_post_kernel(y_ref, na_ref, x_ref, wglu_ref, bglu_ref, wo_ref, gm_ref, *rest):
    g = _gelu_tanh(y_ref[...])
    gate = jax.nn.sigmoid(jnp.dot(g.astype(BF16), wglu_ref[...], preferred_element_type=F32) + bglu_ref[...])
    s5 = (g * gate).astype(BF16)
    w = s5.shape[1]
    mix = (jnp.dot(s5, wo_ref[:w, :], preferred_element_type=F32)
           + jnp.dot(na_ref[...], wo_ref[w:, :], preferred_element_type=F32))
    _ffn_prologue(x_ref[...] + gm_ref[...] * mix, *rest)


def _odd_post_kernel(zre_ref, zim_ref, h_ref_in, cc_ref, sc_ref, wf_ref, bf_ref, gm_ref, *rest):
    c = cc_ref.shape[0]
    parts = []
    for grp in range(zre_ref.shape[1] // c):
        cols = slice(grp * c, (grp + 1) * c)
        parts.append(jnp.dot(zre_ref[:, cols], cc_ref[...], preferred_element_type=F32)
                     + jnp.dot(zim_ref[:, cols], sc_ref[...], preferred_element_type=F32))
    fr = jnp.concatenate(parts, axis=-1).astype(BF16)
    mix = jnp.dot(fr, wf_ref[...], preferred_element_type=F32) + bf_ref[...]
    _ffn_prologue(h_ref_in[...] + gm_ref[...] * mix, *rest)


def _post_call(body, row_inputs, fixed_inputs, t, d, n_exp):
    tm = min(512, t)
    dq = d // 2 // ROW_PARTS
    row = lambda i: (i, 0)
    in_specs = ([pl.BlockSpec((tm, a.shape[1]), row) for a in row_inputs]
                + [pl.BlockSpec(a.shape, functools.partial(lambda i, nd: (0,) * nd, nd=a.ndim))
                   for a in fixed_inputs])
    return pl.pallas_call(
        body,
        grid=(t // tm,),
        in_specs=in_specs,
        out_specs=[pl.BlockSpec((tm, d), row), pl.BlockSpec((ROW_PARTS, tm, dq), lambda i: (0, i, 0)),
                   pl.BlockSpec((n_exp, tm), lambda i: (0, i)), pl.BlockSpec((tm, d), row)],
        out_shape=[jax.ShapeDtypeStruct((t, d), F32), jax.ShapeDtypeStruct((ROW_PARTS, t, dq), U32),
                   jax.ShapeDtypeStruct((n_exp, t), F32), jax.ShapeDtypeStruct((t, d), BF16)],
        compiler_params=_params("parallel"),
    )(*row_inputs, *fixed_inputs)


def _route_kernel(lg_ref, bias_ref, tri_ref, idx_ref, gate_ref, rank_ref, cnt_ref, run_ref):
    @pl.when(pl.program_id(0) == 0)
    def _():
        run_ref[...] = jnp.zeros_like(run_ref)

    scores = jax.nn.sigmoid(lg_ref[...])
    n_exp, tb = scores.shape
    sel = scores + bias_ref[...]
    gsz = n_exp // N_EXPERT_GROUPS
    member = lax.broadcasted_iota(I32, (gsz, tb), 0)
    gscore = []
    for grp in range(N_EXPERT_GROUPS):
        xg = sel[grp * gsz:(grp + 1) * gsz, :]
        m1 = jnp.max(xg, axis=0, keepdims=True)
        first = jnp.min(jnp.where(xg == m1, member, gsz), axis=0, keepdims=True)
        m2 = jnp.max(jnp.where(member == first, -jnp.inf, xg), axis=0, keepdims=True)
        gscore.append(m1 + m2)
    keep_rows = []
    for grp in range(N_EXPERT_GROUPS):
        beaten = jnp.zeros((1, tb), F32)
        for other in range(N_EXPERT_GROUPS):
            if other == grp:
                continue
            wins = (gscore[other] >= gscore[grp]) if other < grp else (gscore[other] > gscore[grp])
            beaten = beaten + jnp.where(wins, 1.0, 0.0)
        keep_rows.append(jnp.broadcast_to(beaten < TOPK_GROUPS, (gsz, tb)))
    masked = jnp.where(jnp.concatenate(keep_rows, axis=0), sel, -jnp.inf)

    expert = lax.broadcasted_iota(I32, (n_exp, tb), 0)
    picks, gates, hots = [], [], []
    chosen = jnp.zeros((n_exp, tb), F32)
    for _ in range(TOP_K):
        m = jnp.max(masked, axis=0, keepdims=True)
        pick = jnp.min(jnp.where(masked == m, expert, n_exp), axis=0, keepdims=True)
        hot = expert == pick
        picks.append(pick)
        hots.append(hot)
        gates.append(jnp.sum(jnp.where(hot, scores, 0.0), axis=0, keepdims=True))
        chosen = jnp.where(hot, 1.0, chosen)
        masked = jnp.where(hot, -jnp.inf, masked)
    total = gates[0]
    for gk in gates[1:]:
        total = total + gk
    ahead = jnp.dot(chosen.astype(BF16), tri_ref[...], preferred_element_type=F32) + run_ref[...]
    for k in range(TOP_K):
        idx_ref[k:k + 1, :] = picks[k]
        gate_ref[k:k + 1, :] = ROUTED_SCALE * gates[k] / total
        rank_ref[k:k + 1, :] = jnp.sum(jnp.where(hots[k], ahead, 0.0), axis=0, keepdims=True).astype(I32)
    run_ref[...] = run_ref[...] + jnp.sum(chosen, axis=1, keepdims=True)
    cnt_ref[...] = jnp.broadcast_to(run_ref[...], cnt_ref.shape)


def route(logits_t, router_bias):
    n_exp, t = logits_t.shape
    tb = min(512, t)
    tri = jnp.asarray(np.triu(np.ones((tb, tb), np.float32), k=1), BF16)
    tok = lambda i: (0, i)
    idx, gate, rank, cnt = pl.pallas_call(
        _route_kernel,
        grid=(t // tb,),
        in_specs=[pl.BlockSpec((n_exp, tb), tok), pl.BlockSpec((n_exp, 1), lambda i: (0, 0)),
                  pl.BlockSpec((tb, tb), lambda i: (0, 0))],
        out_specs=[pl.BlockSpec((TOP_K, tb), tok)] * 3 + [pl.BlockSpec((n_exp, 128), lambda i: (0, 0))],
        out_shape=[jax.ShapeDtypeStruct((TOP_K, t), I32), jax.ShapeDtypeStruct((TOP_K, t), F32),
                   jax.ShapeDtypeStruct((TOP_K, t), I32), jax.ShapeDtypeStruct((n_exp, 128), F32)],
        scratch_shapes=[pltpu.VMEM((n_exp, 1), F32)],
        compiler_params=_params("arbitrary"),
    )(logits_t, router_bias.astype(F32).reshape(n_exp, 1), tri)
    return idx, gate, rank, cnt[:, 0].astype(I32)


def dispatch_plan(idx, rank, counts, n_blocks):
    n_exp = counts.shape[0]
    shift = EXPERT_ROWS.bit_length() - 1
    assert EXPERT_ROWS == 1 << shift
    padded = ((counts + EXPERT_ROWS - 1) >> shift) << shift
    pad_end = jnp.cumsum(padded)
    pad_start = pad_end - padded
    experts_iota = jnp.arange(n_exp, dtype=I32)
    dest = jnp.sum(jnp.where(idx[..., None] == experts_iota, pad_start, 0), axis=-1) + rank
    n_valid = (pad_end[-1] >> shift).astype(I32)
    blk = jnp.minimum(jnp.arange(n_blocks, dtype=I32), n_valid - 1)
    blk_e = jnp.sum((pad_end[None, :] <= blk[:, None] * EXPERT_ROWS).astype(I32), axis=1)
    return dest.astype(I32), jnp.minimum(blk_e, n_exp - 1).astype(I32), n_valid.reshape(1)


def _sc_mesh():
    return plsc.VectorSubcoreMesh(core_axis_name="core", subcore_axis_name="subcore")


def dispatch(f, dest, cap):
    t, w = f.shape
    n_choice = dest.shape[0]

    @functools.partial(pl.kernel, out_type=jax.ShapeDtypeStruct((cap, w), f.dtype), mesh=_sc_mesh(),
                       scratch_types=[pltpu.SemaphoreType.DMA])
    def scatter_rows(x_hbm, i_hbm, o_hbm, sem):
        def body(x_vmem, i_vmem):
            copies = [pltpu.async_copy(x_vmem, o_hbm.at[i_vmem.at[k]], sem) for k in range(n_choice)]
            for cp in copies:
                cp.wait()

        pltpu.emit_pipeline(
            body,
            grid=(t // SC_WINDOW,),
            in_specs=[pl.BlockSpec((SC_WINDOW, w), lambda i: (i, 0)),
                      pl.BlockSpec((n_choice, SC_WINDOW), lambda i: (0, i))],
            out_specs=[],
            core_axis_name=("core", "subcore"),
            dimension_semantics=(pltpu.PARALLEL,),
        )(x_hbm, i_hbm)

    return scatter_rows(f, dest)


def gather_rows(rows, index_row):
    n = index_row.shape[1]
    w = rows.shape[1]

    piece = SC_WINDOW // SC_GATHER_SPLIT

    @functools.partial(pl.kernel, out_type=jax.ShapeDtypeStruct((n, w), rows.dtype), mesh=_sc_mesh(),
                       scratch_types=[pltpu.SemaphoreType.DMA])
    def gather(y_hbm, i_hbm, o_hbm, sem):
        def body(i_vmem, o_vmem):
            copies = [pltpu.async_copy(y_hbm.at[i_vmem.at[0, pl.ds(j * piece, piece)]],
                                       o_vmem.at[pl.ds(j * piece, piece)], sem)
                      for j in range(SC_GATHER_SPLIT)]
            for cp in copies:
                cp.wait()

        pltpu.emit_pipeline(
            body,
            grid=(n // SC_WINDOW,),
            in_specs=[pl.BlockSpec((1, SC_WINDOW), lambda i: (0, i))],
            out_specs=[pl.BlockSpec((SC_WINDOW, w), lambda i: (i, 0))],
            core_axis_name=("core", "subcore"),
            dimension_semantics=(pltpu.PARALLEL,),
        )(i_hbm, o_hbm)

    return gather(rows, index_row)


def _experts_kernel(be_ref, x_ref, wg_ref, wu_ref, wd_ref, y_ref, wg_bf, wu_bf, wd_bf):
    b = pl.program_id(0)

    @pl.when((b == 0) | (be_ref[b] != be_ref[jnp.maximum(b - 1, 0)]))
    def _():
        wg_bf[...] = wg_ref[0, 0].astype(BF16)
        wu_bf[...] = wu_ref[0, 0].astype(BF16)
        wd_bf[...] = wd_ref[0, 0].astype(BF16)

    x = _unpack_rows(jnp.concatenate([x_ref[i] for i in range(ROW_PARTS)], axis=1)).astype(BF16)
    hid = (_silu(jnp.dot(x, wg_bf[...], preferred_element_type=F32))
           * jnp.dot(x, wu_bf[...], preferred_element_type=F32))
    packed = _pack_rows(jnp.dot(hid.astype(BF16), wd_bf[...], preferred_element_type=F32))
    dq = packed.shape[1] // ROW_PARTS
    for i in range(ROW_PARTS):
        y_ref[i] = packed[:, i * dq:(i + 1) * dq]


def experts(rows, blk_e, n_valid, layer, w_gate, w_up, w_down):
    parts, cap, dq = rows.shape
    _, _, d, ff = w_gate.shape
    blk = pl.BlockSpec((parts, EXPERT_ROWS, dq), lambda b, be: (0, b, 0))
    return pl.pallas_call(
        _experts_kernel,
        grid_spec=pltpu.PrefetchScalarGridSpec(
            num_scalar_prefetch=1,
            grid=(n_valid[0],),
            in_specs=[blk,
                      pl.BlockSpec((1, 1, d, ff), lambda b, be: (layer, be[b], 0, 0)),
                      pl.BlockSpec((1, 1, d, ff), lambda b, be: (layer, be[b], 0, 0)),
                      pl.BlockSpec((1, 1, ff, d), lambda b, be: (layer, be[b], 0, 0))],
            out_specs=blk,
            scratch_shapes=[pltpu.VMEM((d, ff), BF16), pltpu.VMEM((d, ff), BF16), pltpu.VMEM((ff, d), BF16)],
        ),
        out_shape=jax.ShapeDtypeStruct(rows.shape, U32),
        compiler_params=_params("arbitrary"),
    )(blk_e, rows, w_gate, w_up, w_down)


def _combine_kernel(gate_ref, h_ref, shared_ref, gf_ref, sh_ref, sc_ref, *rest):
    y_refs, outs = rest[:TOP_K], rest[TOP_K:]
    routed = None
    for k in range(TOP_K):
        y = _unpack_rows(jnp.concatenate([y_refs[k][i, 0] for i in range(ROW_PARTS)], axis=1))
        routed = gate_ref[:, k:k + 1] * y if routed is None else routed + gate_ref[:, k:k + 1] * y
    h = h_ref[...] + gf_ref[...] * (routed + shared_ref[...].astype(F32))
    outs[0][...] = h
    if len(outs) > 1:
        outs[1][...] = _rms(h) * (1.0 + sc_ref[...]) + sh_ref[...]


def combine(y_rows, dest, gate_tk, h, shared, gate_ffn, next_shift, next_scale, with_next):
    t, d = h.shape
    parts, cap, dq = y_rows.shape
    n_out = 2 if with_next else 1
    n_choice = dest.shape[0]
    flat = (dest[None] + (jnp.arange(parts, dtype=I32) * cap)[:, None, None]).reshape(1, parts * n_choice * t)
    picked = gather_rows(y_rows.reshape(parts * cap, dq), flat).reshape(parts, n_choice, t, dq)
    tm = min(256, t)
    row = lambda i: (i, 0)
    fixed = lambda i: (0, 0)
    choice = [pl.BlockSpec((parts, 1, tm, dq), functools.partial(lambda i, k: (0, k, i, 0), k=k))
              for k in range(n_choice)]
    return pl.pallas_call(
        _combine_kernel,
        grid=(t // tm,),
        in_specs=[pl.BlockSpec((tm, n_choice), row), pl.BlockSpec((tm, d), row), pl.BlockSpec((tm, d), row),
                  pl.BlockSpec((1, d), fixed), pl.BlockSpec((1, d), fixed), pl.BlockSpec((1, d), fixed)] + choice,
        out_specs=[pl.BlockSpec((tm, d), row)] * n_out,
        out_shape=[jax.ShapeDtypeStruct((t, d), F32)] * n_out,
        compiler_params=_params("parallel"),
    )(gate_tk, h, shared, gate_ffn, next_shift, next_scale, *([picked] * n_choice))


def moe_tail(h, f_rows, logits_t, shared, router_bias, layer, w_gate, w_up, w_down, gate_ffn,
             next_shift, next_scale, with_next):
    t, d = h.shape
    parts, _, dq = f_rows.shape
    n_exp = w_gate.shape[1]
    n_blocks = -(-(t * TOP_K + n_exp * (EXPERT_ROWS - 1)) // EXPERT_ROWS)
    cap = n_blocks * EXPERT_ROWS
    idx, gate, rank, counts = route(logits_t, router_bias)
    dest, blk_e, n_valid = dispatch_plan(idx, rank, counts, n_blocks)
    dest_all = jnp.concatenate([dest + i * cap for i in range(parts)], axis=1)
    rows = dispatch(f_rows.reshape(parts * t, dq), dest_all, parts * cap).reshape(parts, cap, dq)
    y_rows = experts(rows, blk_e, n_valid, layer, w_gate, w_up, w_down)
    return combine(y_rows, dest, gate.T, h, shared, gate_ffn, next_shift, next_scale, with_next)


def kernel(x, c, ctx, c_ctx, w_ada, b_ada, w_in, s5_lam_re, s5_lam_im, s5_log_dt, s5_b_re, s5_b_im,
           s5_c_re, s5_c_im, s5_d, s5_w_glu, s5_b_glu, na_q_gain, na_k_gain, na_rpb, w_mix_out,
           w_fourier_out, b_fourier_out, w_router, router_bias, w_exp_gate, w_exp_up, w_exp_down,
           w_sh_gate, w_sh_up, w_sh_down):
    bsz, t, d = x.shape
    assert bsz == 1 and w_ada.shape[0] == 2
    n_exp = w_router.shape[2]
    s5w = s5_w_glu.shape[1]
    naw = w_in.shape[2] - s5w
    naw //= 3
    heads = naw // NA_HEAD_DIM

    cond8 = jnp.concatenate([c[:1].astype(F32), c_ctx.astype(F32)[None], jnp.zeros((6, d), F32)], axis=0)
    ada = adaln_all(cond8, w_ada, b_ada)
    mod = lambda layer, who, j: ada[layer, who:who + 1, j * d:(j + 1) * d]

    def ffn_weights(i):
        return (mod(i, 0, 3), mod(i, 0, 4), jnp.transpose(w_router[i]).astype(F32),
                w_sh_gate[i].astype(BF16), w_sh_up[i].astype(BF16), w_sh_down[i].astype(BF16))

    h0 = x[0]
    seg = jnp.asarray(np.kron(np.eye(heads), np.ones((NA_HEAD_DIM, NA_HEAD_DIM))), BF16)
    w_in_b = w_in[0].astype(BF16)
    qg = jnp.tile(na_q_gain[0].astype(F32), heads)[None]
    kg = jnp.tile(na_k_gain[0].astype(F32), heads)[None]
    u_c, _, k_c, v_c = in_projection(ctx[0], mod(0, 1, 0), mod(0, 1, 1), w_in_b, seg, qg, kg, s5w, naw)
    u_l, q_l, k_l, v_l = in_projection(h0, mod(0, 0, 0), mod(0, 0, 1), w_in_b, seg, qg, kg, s5w, naw)
    mats = s5_matrices(s5_lam_re[0], s5_lam_im[0], s5_log_dt[0], s5_b_re[0], s5_b_im[0],
                       s5_c_re[0], s5_c_im[0], s5_d[0])
    y_s5 = s5_mixer(u_c, u_l, mats)
    na = neighbourhood_attention(q_l, k_l, v_l, k_c, v_c, na_bias_table(na_rpb[0], t // GRID_W))
    h1, f1, lg1, sh1 = _post_call(
        _even_post_kernel, [y_s5, na, h0],
        [s5_w_glu[0].astype(BF16), s5_b_glu[0].astype(F32)[None], w_mix_out[0].astype(BF16), mod(0, 0, 2),
         *ffn_weights(0)], t, d, n_exp)
    h2, a1 = moe_tail(h1, f1, lg1, sh1, router_bias[0], 0, w_exp_gate, w_exp_up, w_exp_down,
                      mod(0, 0, 5), mod(1, 0, 0), mod(1, 0, 1), True)

    zre, zim = time_dft(a1)
    cc, sc = channel_dft_tables(d // FOURIER_GROUPS)
    h3, f3, lg3, sh3 = _post_call(
        _odd_post_kernel, [zre, zim, h2],
        [cc, sc, w_fourier_out[0].astype(BF16), b_fourier_out[0].astype(F32)[None], mod(1, 0, 2),
         *ffn_weights(1)], t, d, n_exp)
    zero_row = jnp.zeros((1, d), F32)
    (out,) = moe_tail(h3, f3, lg3, sh3, router_bias[1], 1, w_exp_gate, w_exp_up, w_exp_down,
                      mod(1, 0, 5), zero_row, zero_row, False)
    return out[None]
```

```python
import functools
import math

import numpy as np
import jax
import jax.numpy as jnp
from jax import lax
from jax.experimental import pallas as pl
from jax.experimental.pallas import tpu as pltpu
from jax.experimental.pallas import tpu_sc as plsc

F32 = jnp.float32
BF16 = jnp.bfloat16
I32 = jnp.int32
U32 = jnp.uint32
HIGHEST = lax.Precision.HIGHEST

MXU_DEPTH = 256
LANES = 128
GRID_W = 64
NORM_EPS = 1e-6
S5_GROUP = 16
S5_LAMBDA_RE_MAX = -1e-4
S5_CHUNK = 16
S5_TILE = 128
NA_HEAD_DIM = 64
NA_KH = 8
NA_KW = 16
NA_STEP_ROWS = 4
FOURIER_GROUPS = 4
N_EXPERT_GROUPS = 8
TOPK_GROUPS = 4
TOP_K = 8
ROUTED_SCALE = 2.5
EXPERT_ROWS = 2048
ROW_PARTS = 2
SC_WINDOW = 128
SC_GATHER_SPLIT = 4
NEG_BIG = -1e30

VMEM_LIMIT_BYTES = 56 * 1024 * 1024


def _params(*sem):
    return pltpu.CompilerParams(dimension_semantics=sem or None,
                                vmem_limit_bytes=VMEM_LIMIT_BYTES)


def _rms(x):
    return x * lax.rsqrt(jnp.mean(x * x, axis=-1, keepdims=True) + NORM_EPS)


def _silu(x):
    return x * jax.nn.sigmoid(x)


def _pack_pair(lo, hi):
    lo = lax.bitcast_convert_type(lo.astype(BF16).astype(F32), U32)
    hi = lax.bitcast_convert_type(hi.astype(BF16).astype(F32), U32)
    return (hi & jnp.uint32(0xFFFF0000)) | (lo >> 16)


def _unpack_pair(w):
    return (lax.bitcast_convert_type(w << 16, F32), lax.bitcast_convert_type(w & jnp.uint32(0xFFFF0000), F32))


def _pack_rows(x):
    n = x.shape[1] // 2
    return _pack_pair(x[:, :n], x[:, n:])


def _unpack_rows(w):
    return jnp.concatenate(_unpack_pair(w), axis=1)


def _ada_kernel(c_ref, w_ref, b_ref, o_ref):
    o_ref[0] = jnp.dot(_silu(c_ref[...]), w_ref[0], preferred_element_type=F32,
                       precision=HIGHEST) + b_ref[0]


def adaln_all(cond8, w_ada, b_ada):
    n_layers, d, n6 = w_ada.shape
    tn = n6 // 4
    return pl.pallas_call(
        _ada_kernel,
        grid=(n_layers, n6 // tn),
        in_specs=[pl.BlockSpec((8, d), lambda l, j: (0, 0)),
                  pl.BlockSpec((1, d, tn), lambda l, j: (l, 0, j)),
                  pl.BlockSpec((1, 1, tn), lambda l, j: (l, 0, j))],
        out_specs=pl.BlockSpec((1, 8, tn), lambda l, j: (l, 0, j)),
        out_shape=jax.ShapeDtypeStruct((n_layers, 8, n6), F32),
        compiler_params=_params("parallel", "parallel"),
    )(cond8, w_ada, b_ada.reshape(n_layers, 1, n6))


def _inproj_kernel(x_ref, sh_ref, sc_ref, w_ref, seg_ref, qg_ref, kg_ref,
                   u_ref, q_ref, k_ref, v_ref):
    a = _rms(x_ref[...]) * (1.0 + sc_ref[...]) + sh_ref[...]
    z = jnp.dot(a.astype(BF16), w_ref[...], preferred_element_type=F32)
    s5w = u_ref.shape[1]
    naw = q_ref.shape[1]

    def head_norm(t, gain):
        ss = jnp.dot((t * t).astype(BF16), seg_ref[...], preferred_element_type=F32)
        return t * lax.rsqrt(ss * (1.0 / NA_HEAD_DIM) + NORM_EPS) * gain

    u_ref[...] = z[:, :s5w]
    q_ref[...] = head_norm(z[:, s5w:s5w + naw], qg_ref[...]).astype(BF16)
    k_ref[...] = head_norm(z[:, s5w + naw:s5w + 2 * naw], kg_ref[...]).astype(BF16)
    v_ref[...] = z[:, s5w + 2 * naw:].astype(BF16)


def in_projection(x, shift, scale, w_in_bf16, seg_ones, q_gain_row, k_gain_row, s5w, naw):
    t, d = x.shape
    tm = min(512, t)
    row = lambda i: (i, 0)
    fixed = lambda i: (0, 0)
    return pl.pallas_call(
        _inproj_kernel,
        grid=(t // tm,),
        in_specs=[pl.BlockSpec((tm, d), row),
                  pl.BlockSpec((1, d), fixed), pl.BlockSpec((1, d), fixed),
                  pl.BlockSpec(w_in_bf16.shape, fixed),
                  pl.BlockSpec(seg_ones.shape, fixed),
                  pl.BlockSpec((1, naw), fixed), pl.BlockSpec((1, naw), fixed)],
        out_specs=[pl.BlockSpec((tm, s5w), row), pl.BlockSpec((tm, naw), row),
                   pl.BlockSpec((tm, naw), row), pl.BlockSpec((tm, naw), row)],
        out_shape=[jax.ShapeDtypeStruct((t, s5w), F32)] + [jax.ShapeDtypeStruct((t, naw), BF16)] * 3,
        compiler_params=_params("parallel"),
    )(x, shift, scale, w_in_bf16, seg_ones, q_gain_row, k_gain_row)


def s5_matrices(lam_re, lam_im, log_dt, b_re, b_im, c_re, c_im, d_skip):
    L = S5_CHUNK
    c = S5_GROUP
    taus = jnp.arange(L + 1, dtype=F32)

    def direction(i):
        lam = lax.complex(jnp.minimum(lam_re[i].astype(F32), S5_LAMBDA_RE_MAX), lam_im[i].astype(F32))
        ldt = lam * jnp.exp(log_dt[i].astype(F32))[:, None]
        lam_bar = jnp.exp(ldt)
        b_bar = ((lam_bar - 1.0) / lam)[..., None] * lax.complex(b_re[i].astype(F32), b_im[i].astype(F32))
        cc = lax.complex(c_re[i].astype(F32), c_im[i].astype(F32))
        powers = jnp.exp(ldt[None] * taus[:, None, None])
        resp = jnp.real(jnp.einsum('gcp,tgp,gpd->gctd', cc, powers[:L], b_bar, precision=HIGHEST))
        return jnp.transpose(powers, (1, 2, 0)), b_bar, cc, resp

    pw_f, bb_f, cc_f, k_f = direction(0)
    pw_b, bb_b, cc_b, k_b = direction(1)
    g, p = pw_f.shape[:2]
    lag0 = k_f[:, :, :1] + k_b[:, :, :1] + (jnp.eye(c, dtype=F32)[None, :, None, :] * d_skip.astype(F32)[:, :, None, None])
    by_lag = jnp.concatenate([jnp.flip(k_f[:, :, 1:], axis=2), lag0, k_b[:, :, 1:]], axis=2).reshape(g, c, (2 * L - 1) * c)
    n = (2 * L - 1) * c
    windows = jnp.tile(by_lag, (1, 1, L + 1))[:, :, :L * (n + c)].reshape(g, c, L, n + c)[..., :L * c]
    toep_t = jnp.transpose(jnp.flip(windows, axis=2), (0, 2, 1, 3)).reshape(g, L * c, L * c)

    def state_in(pw_by_s, b_bar):
        return (pw_by_s[:, :, :, None] * b_bar[:, :, None, :]).reshape(g, p, L * c)

    wf = state_in(jnp.flip(pw_f[:, :, :L], axis=2), bb_f)
    wb = state_in(pw_b[:, :, :L], bb_b)
    w_state_t = jnp.concatenate([jnp.real(wf), jnp.imag(wf), jnp.imag(wf), jnp.real(wf),
                                 jnp.real(wb), jnp.imag(wb), jnp.imag(wb), jnp.real(wb)], axis=1)

    def state_out(pw_by_l, cc):
        return (jnp.transpose(pw_by_l, (0, 2, 1))[:, :, None, :] * cc[:, None, :, :]).reshape(g, L * c, p)

    rf = state_out(pw_f[:, :, 1:], cc_f)
    rb = state_out(jnp.flip(pw_b[:, :, 1:], axis=2), cc_b)
    r_state_t = jnp.concatenate([jnp.real(rf), -jnp.imag(rf), jnp.real(rb), -jnp.imag(rb)], axis=-1)

    def mult(a):
        ar, ai = jnp.real(a), jnp.imag(a)
        return jnp.stack([jnp.concatenate([ar, ar], -1), jnp.concatenate([-ai, ai], -1),
                          jnp.concatenate([ai, -ai], -1)])

    return (toep_t.astype(BF16), w_state_t.astype(BF16), r_state_t.astype(BF16),
            mult(pw_f[:, :, L]), mult(pw_b[:, :, L]))


def _s5_pack_kernel(*refs):
    u_refs, (wt_ref, ut_ref, f1_ref, f2_ref, b1_ref, b2_ref) = refs[:-6], refs[-6:]
    L = S5_CHUNK
    g, lc, nck = ut_ref.shape
    c = lc // L
    gs = g // len(u_refs)
    for s in range(L):
        for j, u_ref in enumerate(u_refs):
            step_s = u_ref[pl.ds(s, nck, stride=L), :]
            ut_ref[j * gs:(j + 1) * gs, s * c:(s + 1) * c, :] = (
                jnp.transpose(step_s).astype(BF16).reshape(gs, c, nck))
    n = f1_ref.shape[2]
    for gi in range(g):
        inc = jnp.dot(wt_ref[gi], ut_ref[gi], preferred_element_type=F32)
        for j, ref in enumerate((f1_ref, f2_ref, b1_ref, b2_ref)):
            ref[:, gi, :] = jnp.transpose(inc[j * n:(j + 1) * n, :])


def s5_pack(u, wt_state):
    t, w = u.shape
    g, n4, lc = wt_state.shape
    nc = t // S5_CHUNK
    tile = min(S5_TILE, nc)
    inc = jax.ShapeDtypeStruct((nc, g, n4 // 4), F32)
    inc_blk = pl.BlockSpec((tile, g, n4 // 4), lambda i: (i, 0, 0))
    return pl.pallas_call(
        _s5_pack_kernel,
        grid=(nc // tile,),
        in_specs=[pl.BlockSpec((tile * S5_CHUNK, LANES), functools.partial(lambda i, j: (i, j), j=j))
                  for j in range(w // LANES)] + [pl.BlockSpec(wt_state.shape, lambda i: (0, 0, 0))],
        out_specs=[pl.BlockSpec((g, lc, tile), lambda i: (0, 0, i))] + [inc_blk] * 4,
        out_shape=[jax.ShapeDtypeStruct((g, lc, nc), BF16)] + [inc] * 4,
        compiler_params=_params("parallel"),
    )(*([u] * (w // LANES)), wt_state)


def _s5_scan_kernel(s1_ref, s2_ref, m_ref, init_ref, x_ref, last_ref, v1_ref, v2_ref, *, reverse):
    @pl.when(pl.program_id(0) == 0)
    def _():
        v1_ref[...] = init_ref[0]
        v2_ref[...] = init_ref[1]

    a1, a2, a3 = m_ref[0], m_ref[1], m_ref[2]
    cb = s1_ref.shape[0]

    def body(j, carry):
        v1, v2 = carry
        jj = cb - 1 - j if reverse else j
        x_ref[jj] = v1
        return (a1 * v1 + a2 * v2 + s1_ref[jj], a1 * v2 + a3 * v1 + s2_ref[jj])

    v1, v2 = lax.fori_loop(0, cb, body, (v1_ref[...], v2_ref[...]))
    v1_ref[...] = v1
    v2_ref[...] = v2
    last_ref[...] = v1


def s5_chunk_scan(s1, s2, mult, init, reverse):
    nc, g, n = s1.shape
    cb = min(S5_TILE, nc)
    nb = nc // cb
    blk = (lambda i: (nb - 1 - i, 0, 0)) if reverse else (lambda i: (i, 0, 0))
    return pl.pallas_call(
        functools.partial(_s5_scan_kernel, reverse=reverse),
        grid=(nb,),
        in_specs=[pl.BlockSpec((cb, g, n), blk), pl.BlockSpec((cb, g, n), blk),
                  pl.BlockSpec((3, g, n), lambda i: (0, 0, 0)), pl.BlockSpec((2, g, n), lambda i: (0, 0, 0))],
        out_specs=[pl.BlockSpec((cb, g, n), blk), pl.BlockSpec((g, n), lambda i: (0, 0))],
        out_shape=[jax.ShapeDtypeStruct((nc, g, n), F32), jax.ShapeDtypeStruct((g, n), F32)],
        scratch_shapes=[pltpu.VMEM((g, n), F32), pltpu.VMEM((g, n), F32)],
        compiler_params=_params("arbitrary"),
    )(s1, s2, mult, init)


def _s5_readout_kernel(ut_ref, tt_ref, rt_ref, xf_ref, xb_ref, y_ref, yt_ref, *slab_refs):
    L = S5_CHUNK
    g, lc, nck = ut_ref.shape
    c = lc // L
    gs = g // len(slab_refs)
    for gi in range(g):
        xin_t = jnp.concatenate([jnp.transpose(xf_ref[:, gi, :]), jnp.transpose(xb_ref[:, gi, :])], axis=0)
        yt_ref[gi] = (jnp.dot(tt_ref[gi], ut_ref[gi], preferred_element_type=F32)
                      + jnp.dot(rt_ref[gi], xin_t.astype(BF16), preferred_element_type=F32))
    for j, slab in enumerate(slab_refs):
        for l in range(L):
            step_l = yt_ref[j * gs:(j + 1) * gs, l * c:(l + 1) * c, :].reshape(gs * c, nck)
            slab[pl.ds(l, nck, stride=L), :] = jnp.transpose(step_l)
        y_ref[:, j * gs * c:(j + 1) * gs * c] = slab[...]


def s5_readout(ut, toep_t, r_state_t, xin_f, xin_b):
    g, lc, nc = ut.shape
    n = xin_f.shape[2]
    tile = min(S5_TILE, nc)
    fixed = lambda i: (0, 0, 0)
    state_blk = pl.BlockSpec((tile, g, n), lambda i: (i, 0, 0))
    return pl.pallas_call(
        _s5_readout_kernel,
        grid=(nc // tile,),
        in_specs=[pl.BlockSpec((g, lc, tile), lambda i: (0, 0, i)),
                  pl.BlockSpec(toep_t.shape, fixed), pl.BlockSpec(r_state_t.shape, fixed), state_blk, state_blk],
        out_specs=pl.BlockSpec((tile * S5_CHUNK, g * lc // S5_CHUNK), lambda i: (i, 0)),
        out_shape=jax.ShapeDtypeStruct((nc * S5_CHUNK, g * lc // S5_CHUNK), F32),
        scratch_shapes=[pltpu.VMEM((g, lc, tile), F32)]
        + [pltpu.VMEM((tile * S5_CHUNK, LANES), F32)] * (g * lc // S5_CHUNK // LANES),
        compiler_params=_params("parallel"),
    )(ut, toep_t, r_state_t, xin_f, xin_b)


def s5_mixer(u_ctx, u_lat, mats):
    toep_t, wt_state, r_state_t, mult_f, mult_b = mats
    L = S5_CHUNK
    g, n = mult_f.shape[1:]
    halves = lambda v: jnp.stack([v, jnp.roll(v, n // 2, axis=-1)])
    n_ctx = u_ctx.shape[0] // L
    ctx_chunks = -(-(n_ctx + 1) // S5_TILE) * S5_TILE
    ctx_pad = jnp.pad(u_ctx, ((0, ctx_chunks * L - u_ctx.shape[0]), (0, 0)))
    _, cf1, cf2, cb1, cb2 = s5_pack(ctx_pad, wt_state)
    zero = jnp.zeros((2, g, n), F32)
    ctx_f, _ = s5_chunk_scan(cf1, cf2, mult_f, zero, False)
    _, ctx_b_last = s5_chunk_scan(cb1, cb2, mult_b, zero, True)
    ut, f1, f2, b1, b2 = s5_pack(u_lat, wt_state)
    xin_f, _ = s5_chunk_scan(f1, f2, mult_f, halves(ctx_f[n_ctx]), False)
    xin_b, _ = s5_chunk_scan(b1, b2, mult_b, halves(ctx_b_last), True)
    return s5_readout(ut, toep_t, r_state_t, xin_f, xin_b)


def _na_window(rows):
    step = NA_STEP_ROWS
    assert rows % step == 0 and (NA_KH // 2) % step == 0 and rows >= NA_KH + step
    groups = rows // step
    n_blk = (NA_KH + step) // step
    first = np.clip(np.arange(groups) - NA_KH // 2 // step, 0, groups - n_blk)
    return groups, n_blk, first


def na_bias_table(rpb, rows):
    step = NA_STEP_ROWS
    groups, n_blk, first = _na_window(rows)
    win = n_blk * step
    q_col = np.arange(GRID_W)
    col_start = np.clip(q_col - NA_KW // 2, 0, GRID_W - NA_KW)
    key_col = np.arange(GRID_W)
    off = key_col[None, :] - col_start[:, None]
    valid_col = (off >= 0) & (off < NA_KW)
    rel_col = np.clip(key_col[None, :] - q_col[:, None] + NA_KW - 1, 0, 2 * NA_KW - 2)
    q_row = step * np.arange(groups)[:, None, None] + np.arange(step)[None, :, None]
    key_row = step * first[:, None, None] + np.arange(win)[None, None, :]
    row0 = np.clip(q_row - NA_KH // 2, 0, rows - NA_KH)
    rel_row = np.where((key_row >= row0) & (key_row < row0 + NA_KH), key_row - q_row + NA_KH - 1, -1)
    variant = np.arange(groups) - first
    reps = [int(np.argmax(variant == d)) for d in range(n_blk)]
    assert all((rel_row[g] == rel_row[reps[variant[g]]]).all() for g in range(groups))
    rel_row = rel_row[reps]
    pick_col = jnp.asarray(np.arange(2 * NA_KW - 1)[:, None, None] == rel_col[None], F32)
    tiles = jnp.where(valid_col[None, None], jnp.einsum('hrx,xck->rhck', rpb.astype(F32), pick_col, precision=HIGHEST),
                      NEG_BIG)
    tiles = jnp.concatenate([tiles, jnp.full_like(tiles[:1], NEG_BIG)], axis=0)
    return jnp.stack([jnp.concatenate([jnp.concatenate([tiles[int(rel_row[v, j, i])] for i in range(win)], axis=-1)
                                       for j in range(step)], axis=1) for v in range(n_blk)])


def _na_kernel(*refs, scale, n_blk):
    q_ref = refs[0]
    k_refs = refs[1:1 + n_blk]
    v_refs = refs[1 + n_blk:1 + 2 * n_blk]
    kc_ref, vc_ref, b_ref, o_ref = refs[1 + 2 * n_blk:]
    hd = NA_HEAD_DIM
    width = q_ref.shape[1]
    span = min(MXU_DEPTH, width)
    nt = (((1,), (1,)), ((), ()))
    q = q_ref[...] * scale
    kk = jnp.concatenate([r[...] for r in k_refs], axis=0)
    vv = jnp.concatenate([r[...] for r in v_refs], axis=0)
    kc, vc = kc_ref[...], vc_ref[...]
    nq = q.shape[0]
    per = span // hd
    lane = lax.broadcasted_iota(I32, (nq, span), 1)
    own = [(lane >= j * hd) & (lane < (j + 1) * hd) for j in range(per)]
    out_cols = []
    for c0 in range(0, width, span):
        cols = slice(c0, c0 + span)
        qs = jnp.concatenate([jnp.where(own[j], q[:, cols], jnp.zeros_like(q[:, cols])) for j in range(per)], axis=0)
        h0 = c0 // hd
        bias = b_ref[0, h0:h0 + per].reshape(per * nq, kk.shape[0])
        s = lax.dot_general(qs, kk[:, cols], nt, preferred_element_type=F32) + bias
        sc = lax.dot_general(qs, kc[:, cols], nt, preferred_element_type=F32)
        m = jnp.maximum(jnp.max(s, axis=-1, keepdims=True), jnp.max(sc, axis=-1, keepdims=True))
        p = jnp.exp(s - m)
        pc = jnp.exp(sc - m)
        den = jnp.sum(p, axis=-1, keepdims=True) + jnp.sum(pc, axis=-1, keepdims=True)
        o = (jnp.dot(p.astype(BF16), vv[:, cols], preferred_element_type=F32)
             + jnp.dot(pc.astype(BF16), vc[:, cols], preferred_element_type=F32)) / den
        acc = jnp.zeros((nq, span), F32)
        for j in range(per):
            acc = jnp.where(own[j], o[j * nq:(j + 1) * nq], acc)
        out_cols.append(acc)
    o_ref[...] = jnp.concatenate(out_cols, axis=1).astype(o_ref.dtype)


def neighbourhood_attention(q, k, v, k_ctx, v_ctx, bias_table):
    t, w = q.shape
    groups, n_blk, _ = _na_window(t // GRID_W)
    first = lambda g: jnp.clip(g - NA_KH // 2 // NA_STEP_ROWS, 0, groups - n_blk)
    blk_tokens = NA_STEP_ROWS * GRID_W
    row_blk = pl.BlockSpec((blk_tokens, w), lambda g: (g, 0))
    key_blks = [pl.BlockSpec((blk_tokens, w), functools.partial(lambda g, i: (first(g) + i, 0), i=i))
                for i in range(n_blk)]
    ctx_blk = pl.BlockSpec(k_ctx.shape, lambda g: (0, 0))
    bias_blk = pl.BlockSpec((1,) + bias_table.shape[1:], lambda g: (g - first(g), 0, 0, 0))
    return pl.pallas_call(
        functools.partial(_na_kernel, scale=NA_HEAD_DIM ** -0.5, n_blk=n_blk),
        grid=(groups,),
        in_specs=[row_blk] + key_blks + key_blks + [ctx_blk, ctx_blk, bias_blk],
        out_specs=row_blk,
        out_shape=jax.ShapeDtypeStruct((t, w), BF16),
        compiler_params=_params("parallel"),
    )(q, *([k] * n_blk), *([v] * n_blk), k_ctx, v_ctx, bias_table)


def time_dft_tables(t):
    a_len = 1 << (int(math.log2(t)) // 2)
    b_len = t // a_len
    ka = np.arange(a_len)[:, None]
    tok = b_len * np.arange(a_len)[None, :]
    ang1 = -2.0 * np.pi * ((ka * (tok[None] + np.arange(b_len)[:, None, None])) % t) / t
    stage1 = np.concatenate([np.cos(ang1), np.sin(ang1)], axis=1) / math.sqrt(t)
    ang2 = 2.0 * np.pi * ((np.arange(b_len)[:, None] * np.arange(b_len)[None, :]) % b_len) / b_len
    c2, s2 = np.cos(ang2), np.sin(ang2)
    stage2 = np.block([[c2, s2], [-s2, c2]])
    return jnp.asarray(stage1, BF16), jnp.asarray(stage2, BF16), a_len, b_len


def _time_dft_kernel(x_ref, m_ref, w2_ref, zre_ref, zim_ref, y_ref, z_ref, *, a_len, b_len):
    i = pl.program_id(1)
    bb = m_ref.shape[0]

    def stage1(jb, carry):
        b = i * bb + jb
        xb = x_ref[pl.ds(b, a_len, stride=b_len), :]
        y = jnp.dot(m_ref[jb], xb.astype(BF16), preferred_element_type=F32)
        row = pl.multiple_of(b * a_len, a_len)
        y_ref[pl.ds(row, a_len), :] = _pack_pair(y[:a_len], y[a_len:])
        return carry

    lax.fori_loop(0, bb, stage1, 0, unroll=16)

    @pl.when(i == pl.num_programs(1) - 1)
    def _():
        def stage2(ka, carry):
            rows = pl.ds(ka, b_len, stride=a_len)
            yre, yim = _unpack_pair(y_ref[rows, :])
            y = jnp.concatenate([yre, yim], axis=0).astype(BF16)
            z = jnp.dot(w2_ref[...], y, preferred_element_type=F32)
            z_ref[rows, :] = _pack_pair(z[:b_len], z[b_len:])
            return carry

        lax.fori_loop(0, a_len, stage2, 0, unroll=16)
        zre, zim = _unpack_pair(z_ref[...])
        zre_ref[...] = zre.astype(zre_ref.dtype)
        zim_ref[...] = zim.astype(zim_ref.dtype)


def time_dft(x):
    t, d = x.shape
    stage1, stage2, a_len, b_len = time_dft_tables(t)
    lanes = 128
    bb = min(16, b_len)
    out = jax.ShapeDtypeStruct((t, d), BF16)
    return pl.pallas_call(
        functools.partial(_time_dft_kernel, a_len=a_len, b_len=b_len),
        grid=(d // lanes, b_len // bb),
        in_specs=[pl.BlockSpec((t, lanes), lambda j, i: (0, j)),
                  pl.BlockSpec((bb, 2 * a_len, a_len), lambda j, i: (i, 0, 0)),
                  pl.BlockSpec(stage2.shape, lambda j, i: (0, 0))],
        out_specs=[pl.BlockSpec((t, lanes), lambda j, i: (0, j))] * 2,
        out_shape=[out, out],
        scratch_shapes=[pltpu.VMEM((t, lanes), U32), pltpu.VMEM((t, lanes), U32)],
        compiler_params=_params("parallel", "arbitrary"),
    )(x, stage1, stage2)


def channel_dft_tables(c):
    ang = 2.0 * np.pi * ((np.arange(c)[:, None] * np.arange(c)[None, :]) % c) / c
    return (jnp.asarray(np.cos(ang) / math.sqrt(c), BF16), jnp.asarray(np.sin(ang) / math.sqrt(c), BF16))


def _ffn_prologue(h, shf_ref, scf_ref, wr_ref, wsg_ref, wsu_ref, wsd_ref, h_ref, f_ref, lg_ref, shared_ref):
    h_ref[...] = h
    f = _rms(h) * (1.0 + scf_ref[...]) + shf_ref[...]
    packed = _pack_rows(f)
    dq = packed.shape[1] // ROW_PARTS
    for i in range(ROW_PARTS):
        f_ref[i] = packed[:, i * dq:(i + 1) * dq]
    fb = f.astype(BF16)
    f_lo = (f - fb.astype(F32)).astype(BF16)
    nt = (((1,), (1,)), ((), ()))
    wr = wr_ref[...]
    wr_hi = wr.astype(BF16)
    wr_lo = (wr - wr_hi.astype(F32)).astype(BF16)
    lg_ref[...] = (lax.dot_general(wr_hi, fb, nt, preferred_element_type=F32)
                   + lax.dot_general(wr_hi, f_lo, nt, preferred_element_type=F32)
                   + lax.dot_general(wr_lo, fb, nt, preferred_element_type=F32))
    hid = (_silu(jnp.dot(fb, wsg_ref[...], preferred_element_type=F32))
           * jnp.dot(fb, wsu_ref[...], preferred_element_type=F32))
    shared_ref[...] = jnp.dot(hid.astype(BF16), wsd_ref[...], preferred_element_type=F32).astype(shared_ref.dtype)


def _gelu_tanh(x):
    return 0.5 * x * (1.0 + jnp.tanh(math.sqrt(2.0 / math.pi) * (x + 0.044715 * (x * x * x))))


def _even_post_kernel(y_ref, na_ref, x_ref, wglu_ref, bglu_ref, wo_ref, gm_ref, *rest):
    g = _gelu_tanh(y_ref[...])
    gate = jax.nn.sigmoid(jnp.dot(g.astype(BF16), wglu_ref[...], preferred_element_type=F32) + bglu_ref[...])
    s5 = (g * gate).astype(BF16)
    w = s5.shape[1]
    mix = (jnp.dot(s5, wo_ref[:w, :], preferred_element_type=F32)
           + jnp.dot(na_ref[...], wo_ref[w:, :], preferred_element_type=F32))
    _ffn_prologue(x_ref[...] + gm_ref[...] * mix, *rest)


def _odd_post_kernel(zre_ref, zim_ref, h_ref_in, cc_ref, sc_ref, wf_ref, bf_ref, gm_ref, *rest):
    c = cc_ref.shape[0]
    parts = []
    for grp in range(zre_ref.shape[1] // c):
        cols = slice(grp * c, (grp + 1) * c)
        parts.append(jnp.dot(zre_ref[:, cols], cc_ref[...], preferred_element_type=F32)
                     + jnp.dot(zim_ref[:, cols], sc_ref[...], preferred_element_type=F32))
    fr = jnp.concatenate(parts, axis=-1).astype(BF16)
    mix = jnp.dot(fr, wf_ref[...], preferred_element_type=F32) + bf_ref[...]
    _ffn_prologue(h_ref_in[...] + gm_ref[...] * mix, *rest)


def _post_call(body, row_inputs, fixed_inputs, t, d, n_exp):
    tm = min(512, t)
    dq = d // 2 // ROW_PARTS
    row = lambda i: (i, 0)
    in_specs = ([pl.BlockSpec((tm, a.shape[1]), row) for a in row_inputs]
                + [pl.BlockSpec(a.shape, functools.partial(lambda i, nd: (0,) * nd, nd=a.ndim))
                   for a in fixed_inputs])
    return pl.pallas_call(
        body,
        grid=(t // tm,),
        in_specs=in_specs,
        out_specs=[pl.BlockSpec((tm, d), row), pl.BlockSpec((ROW_PARTS, tm, dq), lambda i: (0, i, 0)),
                   pl.BlockSpec((n_exp, tm), lambda i: (0, i)), pl.BlockSpec((tm, d), row)],
        out_shape=[jax.ShapeDtypeStruct((t, d), F32), jax.ShapeDtypeStruct((ROW_PARTS, t, dq), U32),
                   jax.ShapeDtypeStruct((n_exp, t), F32), jax.ShapeDtypeStruct((t, d), BF16)],
        compiler_params=_params("parallel"),
    )(*row_inputs, *fixed_inputs)


def _route_kernel(lg_ref, bias_ref, tri_ref, idx_ref, gate_ref, rank_ref, cnt_ref, run_ref):
    @pl.when(pl.program_id(0) == 0)
    def _():
        run_ref[...] = jnp.zeros_like(run_ref)

    scores = jax.nn.sigmoid(lg_ref[...])
    n_exp, tb = scores.shape
    sel = scores + bias_ref[...]
    gsz = n_exp // N_EXPERT_GROUPS
    member = lax.broadcasted_iota(I32, (gsz, tb), 0)
    gscore = []
    for grp in range(N_EXPERT_GROUPS):
        xg = sel[grp * gsz:(grp + 1) * gsz, :]
        m1 = jnp.max(xg, axis=0, keepdims=True)
        first = jnp.min(jnp.where(xg == m1, member, gsz), axis=0, keepdims=True)
        m2 = jnp.max(jnp.where(member == first, -jnp.inf, xg), axis=0, keepdims=True)
        gscore.append(m1 + m2)
    keep_rows = []
    for grp in range(N_EXPERT_GROUPS):
        beaten = jnp.zeros((1, tb), F32)
        for other in range(N_EXPERT_GROUPS):
            if other == grp:
                continue
            wins = (gscore[other] >= gscore[grp]) if other < grp else (gscore[other] > gscore[grp])
            beaten = beaten + jnp.where(wins, 1.0, 0.0)
        keep_rows.append(jnp.broadcast_to(beaten < TOPK_GROUPS, (gsz, tb)))
    masked = jnp.where(jnp.concatenate(keep_rows, axis=0), sel, -jnp.inf)

    expert = lax.broadcasted_iota(I32, (n_exp, tb), 0)
    picks, gates, hots = [], [], []
    chosen = jnp.zeros((n_exp, tb), F32)
    for _ in range(TOP_K):
        m = jnp.max(masked, axis=0, keepdims=True)
        pick = jnp.min(jnp.where(masked == m, expert, n_exp), axis=0, keepdims=True)
        hot = expert == pick
        picks.append(pick)
        hots.append(hot)
        gates.append(jnp.sum(jnp.where(hot, scores, 0.0), axis=0, keepdims=True))
        chosen = jnp.where(hot, 1.0, chosen)
        masked = jnp.where(hot, -jnp.inf, masked)
    total = gates[0]
    for gk in gates[1:]:
        total = total + gk
    ahead = jnp.dot(chosen.astype(BF16), tri_ref[...], preferred_element_type=F32) + run_ref[...]
    for k in range(TOP_K):
        idx_ref[k:k + 1, :] = picks[k]
        gate_ref[k:k + 1, :] = ROUTED_SCALE * gates[k] / total
        rank_ref[k:k + 1, :] = jnp.sum(jnp.where(hots[k], ahead, 0.0), axis=0, keepdims=True).astype(I32)
    run_ref[...] = run_ref[...] + jnp.sum(chosen, axis=1, keepdims=True)
    cnt_ref[...] = jnp.broadcast_to(run_ref[...], cnt_ref.shape)


def route(logits_t, router_bias):
    n_exp, t = logits_t.shape
    tb = min(1024, t)
    tri = jnp.asarray(np.triu(np.ones((tb, tb), np.float32), k=1), BF16)
    tok = lambda i: (0, i)
    idx, gate, rank, cnt = pl.pallas_call(
        _route_kernel,
        grid=(t // tb,),
        in_specs=[pl.BlockSpec((n_exp, tb), tok), pl.BlockSpec((n_exp, 1), lambda i: (0, 0)),
                  pl.BlockSpec((tb, tb), lambda i: (0, 0))],
        out_specs=[pl.BlockSpec((TOP_K, tb), tok)] * 3 + [pl.BlockSpec((n_exp, 128), lambda i: (0, 0))],
        out_shape=[jax.ShapeDtypeStruct((TOP_K, t), I32), jax.ShapeDtypeStruct((TOP_K, t), F32),
                   jax.ShapeDtypeStruct((TOP_K, t), I32), jax.ShapeDtypeStruct((n_exp, 128), F32)],
        scratch_shapes=[pltpu.VMEM((n_exp, 1), F32)],
        compiler_params=_params("arbitrary"),
    )(logits_t, router_bias.astype(F32).reshape(n_exp, 1), tri)
    return idx, gate, rank, cnt[:, 0].astype(I32)


def dispatch_plan(idx, rank, counts, n_blocks):
    n_exp = counts.shape[0]
    shift = EXPERT_ROWS.bit_length() - 1
    assert EXPERT_ROWS == 1 << shift
    padded = ((counts + EXPERT_ROWS - 1) >> shift) << shift
    pad_end = jnp.cumsum(padded)
    pad_start = pad_end - padded
    experts_iota = jnp.arange(n_exp, dtype=I32)
    dest = jnp.sum(jnp.where(idx[..., None] == experts_iota, pad_start, 0), axis=-1) + rank
    n_valid = (pad_end[-1] >> shift).astype(I32)
    blk = jnp.minimum(jnp.arange(n_blocks, dtype=I32), n_valid - 1)
    blk_e = jnp.sum((pad_end[None, :] <= blk[:, None] * EXPERT_ROWS).astype(I32), axis=1)
    return dest.astype(I32), jnp.minimum(blk_e, n_exp - 1).astype(I32), n_valid.reshape(1)


def _sc_mesh():
    return plsc.VectorSubcoreMesh(core_axis_name="core", subcore_axis_name="subcore")


def dispatch(f, dest, cap):
    t, w = f.shape
    n_choice = dest.shape[0]

    @functools.partial(pl.kernel, out_type=jax.ShapeDtypeStruct((cap, w), f.dtype), mesh=_sc_mesh(),
                       scratch_types=[pltpu.SemaphoreType.DMA])
    def scatter_rows(x_hbm, i_hbm, o_hbm, sem):
        def body(x_vmem, i_vmem):
            copies = [pltpu.async_copy(x_vmem, o_hbm.at[i_vmem.at[k]], sem) for k in range(n_choice)]
            for cp in copies:
                cp.wait()

        pltpu.emit_pipeline(
            body,
            grid=(t // SC_WINDOW,),
            in_specs=[pl.BlockSpec((SC_WINDOW, w), lambda i: (i, 0)),
                      pl.BlockSpec((n_choice, SC_WINDOW), lambda i: (0, i))],
            out_specs=[],
            core_axis_name=("core", "subcore"),
            dimension_semantics=(pltpu.PARALLEL,),
        )(x_hbm, i_hbm)

    return scatter_rows(f, dest)


def gather_rows(rows, index_row):
    n = index_row.shape[1]
    w = rows.shape[1]

    piece = SC_WINDOW // SC_GATHER_SPLIT

    @functools.partial(pl.kernel, out_type=jax.ShapeDtypeStruct((n, w), rows.dtype), mesh=_sc_mesh(),
                       scratch_types=[pltpu.SemaphoreType.DMA])
    def gather(y_hbm, i_hbm, o_hbm, sem):
        def body(i_vmem, o_vmem):
            copies = [pltpu.async_copy(y_hbm.at[i_vmem.at[0, pl.ds(j * piece, piece)]],
                                       o_vmem.at[pl.ds(j * piece, piece)], sem)
                      for j in range(SC_GATHER_SPLIT)]
            for cp in copies:
                cp.wait()

        pltpu.emit_pipeline(
            body,
            grid=(n // SC_WINDOW,),
            in_specs=[pl.BlockSpec((1, SC_WINDOW), lambda i: (0, i))],
            out_specs=[pl.BlockSpec((SC_WINDOW, w), lambda i: (i, 0))],
            core_axis_name=("core", "subcore"),
            dimension_semantics=(pltpu.PARALLEL,),
        )(i_hbm, o_hbm)

    return gather(rows, index_row)


def _experts_kernel(be_ref, x_ref, wg_ref, wu_ref, wd_ref, y_ref, wg_bf, wu_bf, wd_bf):
    b = pl.program_id(0)

    @pl.when((b == 0) | (be_ref[b] != be_ref[jnp.maximum(b - 1, 0)]))
    def _():
        wg_bf[...] = wg_ref[0, 0].astype(BF16)
        wu_bf[...] = wu_ref[0, 0].astype(BF16)
        wd_bf[...] = wd_ref[0, 0].astype(BF16)

    x = _unpack_rows(jnp.concatenate([x_ref[i] for i in range(ROW_PARTS)], axis=1)).astype(BF16)
    hid = (_silu(jnp.dot(x, wg_bf[...], preferred_element_type=F32))
           * jnp.dot(x, wu_bf[...], preferred_element_type=F32))
    packed = _pack_rows(jnp.dot(hid.astype(BF16), wd_bf[...], preferred_element_type=F32))
    dq = packed.shape[1] // ROW_PARTS
    for i in range(ROW_PARTS):
        y_ref[i] = packed[:, i * dq:(i + 1) * dq]


def experts(rows, blk_e, n_valid, layer, w_gate, w_up, w_down):
    parts, cap, dq = rows.shape
    _, _, d, ff = w_gate.shape
    blk = pl.BlockSpec((parts, EXPERT_ROWS, dq), lambda b, be: (0, b, 0))
    return pl.pallas_call(
        _experts_kernel,
        grid_spec=pltpu.PrefetchScalarGridSpec(
            num_scalar_prefetch=1,
            grid=(n_valid[0],),
            in_specs=[blk,
                      pl.BlockSpec((1, 1, d, ff), lambda b, be: (layer, be[b], 0, 0)),
                      pl.BlockSpec((1, 1, d, ff), lambda b, be: (layer, be[b], 0, 0)),
                      pl.BlockSpec((1, 1, ff, d), lambda b, be: (layer, be[b], 0, 0))],
            out_specs=blk,
            scratch_shapes=[pltpu.VMEM((d, ff), BF16), pltpu.VMEM((d, ff), BF16), pltpu.VMEM((ff, d), BF16)],
        ),
        out_shape=jax.ShapeDtypeStruct(rows.shape, U32),
        compiler_params=_params("arbitrary"),
    )(blk_e, rows, w_gate, w_up, w_down)


def _combine_kernel(gate_ref, h_ref, shared_ref, gf_ref, sh_ref, sc_ref, *rest):
    y_refs, outs = rest[:TOP_K], rest[TOP_K:]
    routed = None
    for k in range(TOP_K):
        y = _unpack_rows(jnp.concatenate([y_refs[k][i, 0] for i in range(ROW_PARTS)], axis=1))
        routed = gate_ref[:, k:k + 1] * y if routed is None else routed + gate_ref[:, k:k + 1] * y
    h = h_ref[...] + gf_ref[...] * (routed + shared_ref[...].astype(F32))
    outs[0][...] = h
    if len(outs) > 1:
        outs[1][...] = _rms(h) * (1.0 + sc_ref[...]) + sh_ref[...]


def combine(y_rows, dest, gate_tk, h, shared, gate_ffn, next_shift, next_scale, with_next):
    t, d = h.shape
    parts, cap, dq = y_rows.shape
    n_out = 2 if with_next else 1
    n_choice = dest.shape[0]
    flat = (dest[None] + (jnp.arange(parts, dtype=I32) * cap)[:, None, None]).reshape(1, parts * n_choice * t)
    picked = gather_rows(y_rows.reshape(parts * cap, dq), flat).reshape(parts, n_choice, t, dq)
    tm = min(512, t)
    row = lambda i: (i, 0)
    fixed = lambda i: (0, 0)
    choice = [pl.BlockSpec((parts, 1, tm, dq), functools.partial(lambda i, k: (0, k, i, 0), k=k))
              for k in range(n_choice)]
    return pl.pallas_call(
        _combine_kernel,
        grid=(t // tm,),
        in_specs=[pl.BlockSpec((tm, n_choice), row), pl.BlockSpec((tm, d), row), pl.BlockSpec((tm, d), row),
                  pl.BlockSpec((1, d), fixed), pl.BlockSpec((1, d), fixed), pl.BlockSpec((1, d), fixed)] + choice,
        out_specs=[pl.BlockSpec((tm, d), row)] * n_out,
        out_shape=[jax.ShapeDtypeStruct((t, d), F32)] * n_out,
        compiler_params=_params("parallel"),
    )(gate_tk, h, shared, gate_ffn, next_shift, next_scale, *([picked] * n_choice))


def moe_tail(h, f_rows, logits_t, shared, router_bias, layer, w_gate, w_up, w_down, gate_ffn,
             next_shift, next_scale, with_next):
    t, d = h.shape
    parts, _, dq = f_rows.shape
    n_exp = w_gate.shape[1]
    n_blocks = -(-(t * TOP_K + n_exp * (EXPERT_ROWS - 1)) // EXPERT_ROWS)
    cap = n_blocks * EXPERT_ROWS
    idx, gate, rank, counts = route(logits_t, router_bias)
    dest, blk_e, n_valid = dispatch_plan(idx, rank, counts, n_blocks)
    dest_all = jnp.concatenate([dest + i * cap for i in range(parts)], axis=1)
    rows = dispatch(f_rows.reshape(parts * t, dq), dest_all, parts * cap).reshape(parts, cap, dq)
    y_rows = experts(rows, blk_e, n_valid, layer, w_gate, w_up, w_down)
    return combine(y_rows, dest, gate.T, h, shared, gate_ffn, next_shift, next_scale, with_next)


def kernel(x, c, ctx, c_ctx, w_ada, b_ada, w_in, s5_lam_re, s5_lam_im, s5_log_dt, s5_b_re, s5_b_im,
           s5_c_re, s5_c_im, s5_d, s5_w_glu, s5_b_glu, na_q_gain, na_k_gain, na_rpb, w_mix_out,
           w_fourier_out, b_fourier_out, w_router, router_bias, w_exp_gate, w_exp_up, w_exp_down,
           w_sh_gate, w_sh_up, w_sh_down):
    bsz, t, d = x.shape
    assert bsz == 1 and w_ada.shape[0] == 2
    n_exp = w_router.shape[2]
    s5w = s5_w_glu.shape[1]
    naw = w_in.shape[2] - s5w
    naw //= 3
    heads = naw // NA_HEAD_DIM

    cond8 = jnp.concatenate([c[:1].astype(F32), c_ctx.astype(F32)[None], jnp.zeros((6, d), F32)], axis=0)
    ada = adaln_all(cond8, w_ada, b_ada)
    mod = lambda layer, who, j: ada[layer, who:who + 1, j * d:(j + 1) * d]

    def ffn_weights(i):
        return (mod(i, 0, 3), mod(i, 0, 4), jnp.transpose(w_router[i]).astype(F32),
                w_sh_gate[i].astype(BF16), w_sh_up[i].astype(BF16), w_sh_down[i].astype(BF16))

    h0 = x[0]
    seg = jnp.asarray(np.kron(np.eye(heads), np.ones((NA_HEAD_DIM, NA_HEAD_DIM))), BF16)
    w_in_b = w_in[0].astype(BF16)
    qg = jnp.tile(na_q_gain[0].astype(F32), heads)[None]
    kg = jnp.tile(na_k_gain[0].astype(F32), heads)[None]
    u_c, _, k_c, v_c = in_projection(ctx[0], mod(0, 1, 0), mod(0, 1, 1), w_in_b, seg, qg, kg, s5w, naw)
    u_l, q_l, k_l, v_l = in_projection(h0, mod(0, 0, 0), mod(0, 0, 1), w_in_b, seg, qg, kg, s5w, naw)
    mats = s5_matrices(s5_lam_re[0], s5_lam_im[0], s5_log_dt[0], s5_b_re[0], s5_b_im[0],
                       s5_c_re[0], s5_c_im[0], s5_d[0])
    y_s5 = s5_mixer(u_c, u_l, mats)
    na = neighbourhood_attention(q_l, k_l, v_l, k_c, v_c, na_bias_table(na_rpb[0], t // GRID_W))
    h1, f1, lg1, sh1 = _post_call(
        _even_post_kernel, [y_s5, na, h0],
        [s5_w_glu[0].astype(BF16), s5_b_glu[0].astype(F32)[None], w_mix_out[0].astype(BF16), mod(0, 0, 2),
         *ffn_weights(0)], t, d, n_exp)
    h2, a1 = moe_tail(h1, f1, lg1, sh1, router_bias[0], 0, w_exp_gate, w_exp_up, w_exp_down,
                      mod(0, 0, 5), mod(1, 0, 0), mod(1, 0, 1), True)

    zre, zim = time_dft(a1)
    cc, sc = channel_dft_tables(d // FOURIER_GROUPS)
    h3, f3, lg3, sh3 = _post_call(
        _odd_post_kernel, [zre, zim, h2],
        [cc, sc, w_fourier_out[0].astype(BF16), b_fourier_out[0].astype(F32)[None], mod(1, 0, 2),
         *ffn_weights(1)], t, d, n_exp)
    zero_row = jnp.zeros((1, d), F32)
    (out,) = moe_tail(h3, f3, lg3, sh3, router_bias[1], 1, w_exp_gate, w_exp_up, w_exp_down,
                      mod(1, 0, 5), zero_row, zero_row, False)
    return out[None]
```

```python
import functools
import math

import numpy as np
import jax
import jax.numpy as jnp
from jax import lax
from jax.experimental import pallas as pl
from jax.experimental.pallas import tpu as pltpu
from jax.experimental.pallas import tpu_sc as plsc

F32 = jnp.float32
BF16 = jnp.bfloat16
I32 = jnp.int32
U32 = jnp.uint32
HIGHEST = lax.Precision.HIGHEST

MXU_DEPTH = 256
LANES = 128
GRID_W = 64
NORM_EPS = 1e-6
S5_GROUP = 16
S5_LAMBDA_RE_MAX = -1e-4
S5_CHUNK = 16
S5_TILE = 128
NA_HEAD_DIM = 64
NA_KH = 8
NA_KW = 16
NA_STEP_ROWS = 4
FOURIER_GROUPS = 4
N_EXPERT_GROUPS = 8
TOPK_GROUPS = 4
TOP_K = 8
ROUTED_SCALE = 2.5
EXPERT_ROWS = 2048
ROW_PARTS = 2
SC_WINDOW = 128
SC_GATHER_SPLIT = 4
NEG_BIG = -1e30

VMEM_LIMIT_BYTES = 56 * 1024 * 1024


def _params(*sem):
    return pltpu.CompilerParams(dimension_semantics=sem or None,
                                vmem_limit_bytes=VMEM_LIMIT_BYTES)


def _rms(x):
    return x * lax.rsqrt(jnp.mean(x * x, axis=-1, keepdims=True) + NORM_EPS)


def _silu(x):
    return x * jax.nn.sigmoid(x)


def _pack_pair(lo, hi):
    lo = lax.bitcast_convert_type(lo.astype(BF16).astype(F32), U32)
    hi = lax.bitcast_convert_type(hi.astype(BF16).astype(F32), U32)
    return (hi & jnp.uint32(0xFFFF0000)) | (lo >> 16)


def _unpack_pair(w):
    return (lax.bitcast_convert_type(w << 16, F32), lax.bitcast_convert_type(w & jnp.uint32(0xFFFF0000), F32))


def _pack_rows(x):
    n = x.shape[1] // 2
    return _pack_pair(x[:, :n], x[:, n:])


def _unpack_rows(w):
    return jnp.concatenate(_unpack_pair(w), axis=1)


def _ada_kernel(c_ref, w_ref, b_ref, o_ref):
    o_ref[0] = jnp.dot(_silu(c_ref[...]), w_ref[0], preferred_element_type=F32,
                       precision=HIGHEST) + b_ref[0]


def adaln_all(cond8, w_ada, b_ada):
    n_layers, d, n6 = w_ada.shape
    tn = n6 // 4
    return pl.pallas_call(
        _ada_kernel,
        grid=(n_layers, n6 // tn),
        in_specs=[pl.BlockSpec((8, d), lambda l, j: (0, 0)),
                  pl.BlockSpec((1, d, tn), lambda l, j: (l, 0, j)),
                  pl.BlockSpec((1, 1, tn), lambda l, j: (l, 0, j))],
        out_specs=pl.BlockSpec((1, 8, tn), lambda l, j: (l, 0, j)),
        out_shape=jax.ShapeDtypeStruct((n_layers, 8, n6), F32),
        compiler_params=_params("parallel", "parallel"),
    )(cond8, w_ada, b_ada.reshape(n_layers, 1, n6))


def _inproj_kernel(x_ref, sh_ref, sc_ref, w_ref, seg_ref, qg_ref, kg_ref,
                   u_ref, q_ref, k_ref, v_ref):
    a = _rms(x_ref[...]) * (1.0 + sc_ref[...]) + sh_ref[...]
    z = jnp.dot(a.astype(BF16), w_ref[...], preferred_element_type=F32)
    s5w = u_ref.shape[1]
    naw = q_ref.shape[1]

    def head_norm(t, gain):
        ss = jnp.dot((t * t).astype(BF16), seg_ref[...], preferred_element_type=F32)
        return t * lax.rsqrt(ss * (1.0 / NA_HEAD_DIM) + NORM_EPS) * gain

    u_ref[...] = z[:, :s5w]
    q_ref[...] = head_norm(z[:, s5w:s5w + naw], qg_ref[...]).astype(BF16)
    k_ref[...] = head_norm(z[:, s5w + naw:s5w + 2 * naw], kg_ref[...]).astype(BF16)
    v_ref[...] = z[:, s5w + 2 * naw:].astype(BF16)


def in_projection(x, shift, scale, w_in_bf16, seg_ones, q_gain_row, k_gain_row, s5w, naw):
    t, d = x.shape
    tm = min(512, t)
    row = lambda i: (i, 0)
    fixed = lambda i: (0, 0)
    return pl.pallas_call(
        _inproj_kernel,
        grid=(t // tm,),
        in_specs=[pl.BlockSpec((tm, d), row),
                  pl.BlockSpec((1, d), fixed), pl.BlockSpec((1, d), fixed),
                  pl.BlockSpec(w_in_bf16.shape, fixed),
                  pl.BlockSpec(seg_ones.shape, fixed),
                  pl.BlockSpec((1, naw), fixed), pl.BlockSpec((1, naw), fixed)],
        out_specs=[pl.BlockSpec((tm, s5w), row), pl.BlockSpec((tm, naw), row),
                   pl.BlockSpec((tm, naw), row), pl.BlockSpec((tm, naw), row)],
        out_shape=[jax.ShapeDtypeStruct((t, s5w), F32)] + [jax.ShapeDtypeStruct((t, naw), BF16)] * 3,
        compiler_params=_params("parallel"),
    )(x, shift, scale, w_in_bf16, seg_ones, q_gain_row, k_gain_row)


def s5_matrices(lam_re, lam_im, log_dt, b_re, b_im, c_re, c_im, d_skip):
    L = S5_CHUNK
    c = S5_GROUP
    taus = jnp.arange(L + 1, dtype=F32)

    def direction(i):
        lam = lax.complex(jnp.minimum(lam_re[i].astype(F32), S5_LAMBDA_RE_MAX), lam_im[i].astype(F32))
        ldt = lam * jnp.exp(log_dt[i].astype(F32))[:, None]
        lam_bar = jnp.exp(ldt)
        b_bar = ((lam_bar - 1.0) / lam)[..., None] * lax.complex(b_re[i].astype(F32), b_im[i].astype(F32))
        cc = lax.complex(c_re[i].astype(F32), c_im[i].astype(F32))
        powers = jnp.exp(ldt[None] * taus[:, None, None])
        resp = jnp.real(jnp.einsum('gcp,tgp,gpd->gctd', cc, powers[:L], b_bar, precision=HIGHEST))
        return jnp.transpose(powers, (1, 2, 0)), b_bar, cc, resp

    pw_f, bb_f, cc_f, k_f = direction(0)
    pw_b, bb_b, cc_b, k_b = direction(1)
    g, p = pw_f.shape[:2]
    lag0 = k_f[:, :, :1] + k_b[:, :, :1] + (jnp.eye(c, dtype=F32)[None, :, None, :] * d_skip.astype(F32)[:, :, None, None])
    by_lag = jnp.concatenate([jnp.flip(k_f[:, :, 1:], axis=2), lag0, k_b[:, :, 1:]], axis=2).reshape(g, c, (2 * L - 1) * c)
    n = (2 * L - 1) * c
    windows = jnp.tile(by_lag, (1, 1, L + 1))[:, :, :L * (n + c)].reshape(g, c, L, n + c)[..., :L * c]
    toep_t = jnp.transpose(jnp.flip(windows, axis=2), (0, 2, 1, 3)).reshape(g, L * c, L * c)

    def state_in(pw_by_s, b_bar):
        return (pw_by_s[:, :, :, None] * b_bar[:, :, None, :]).reshape(g, p, L * c)

    wf = state_in(jnp.flip(pw_f[:, :, :L], axis=2), bb_f)
    wb = state_in(pw_b[:, :, :L], bb_b)
    w_state_t = jnp.concatenate([jnp.real(wf), jnp.imag(wf), jnp.imag(wf), jnp.real(wf),
                                 jnp.real(wb), jnp.imag(wb), jnp.imag(wb), jnp.real(wb)], axis=1)

    def state_out(pw_by_l, cc):
        return (jnp.transpose(pw_by_l, (0, 2, 1))[:, :, None, :] * cc[:, None, :, :]).reshape(g, L * c, p)

    rf = state_out(pw_f[:, :, 1:], cc_f)
    rb = state_out(jnp.flip(pw_b[:, :, 1:], axis=2), cc_b)
    r_state_t = jnp.concatenate([jnp.real(rf), -jnp.imag(rf), jnp.real(rb), -jnp.imag(rb)], axis=-1)

    def mult(a):
        ar, ai = jnp.real(a), jnp.imag(a)
        return jnp.stack([jnp.concatenate([ar, ar], -1), jnp.concatenate([-ai, ai], -1),
                          jnp.concatenate([ai, -ai], -1)])

    return (toep_t.astype(BF16), w_state_t.astype(BF16), r_state_t.astype(BF16),
            mult(pw_f[:, :, L]), mult(pw_b[:, :, L]))


def _s5_pack_kernel(*refs):
    u_refs, (wt_ref, ut_ref, f1_ref, f2_ref, b1_ref, b2_ref) = refs[:-6], refs[-6:]
    L = S5_CHUNK
    g, lc, nck = ut_ref.shape
    c = lc // L
    gs = g // len(u_refs)
    for s in range(L):
        for j, u_ref in enumerate(u_refs):
            step_s = u_ref[pl.ds(s, nck, stride=L), :]
            ut_ref[j * gs:(j + 1) * gs, s * c:(s + 1) * c, :] = (
                jnp.transpose(step_s).astype(BF16).reshape(gs, c, nck))
    n = f1_ref.shape[2]
    for gi in range(g):
        inc = jnp.dot(wt_ref[gi], ut_ref[gi], preferred_element_type=F32)
        for j, ref in enumerate((f1_ref, f2_ref, b1_ref, b2_ref)):
            ref[:, gi, :] = jnp.transpose(inc[j * n:(j + 1) * n, :])


def s5_pack(u, wt_state):
    t, w = u.shape
    g, n4, lc = wt_state.shape
    nc = t // S5_CHUNK
    tile = min(S5_TILE, nc)
    inc = jax.ShapeDtypeStruct((nc, g, n4 // 4), F32)
    inc_blk = pl.BlockSpec((tile, g, n4 // 4), lambda i: (i, 0, 0))
    return pl.pallas_call(
        _s5_pack_kernel,
        grid=(nc // tile,),
        in_specs=[pl.BlockSpec((tile * S5_CHUNK, LANES), functools.partial(lambda i, j: (i, j), j=j))
                  for j in range(w // LANES)] + [pl.BlockSpec(wt_state.shape, lambda i: (0, 0, 0))],
        out_specs=[pl.BlockSpec((g, lc, tile), lambda i: (0, 0, i))] + [inc_blk] * 4,
        out_shape=[jax.ShapeDtypeStruct((g, lc, nc), BF16)] + [inc] * 4,
        compiler_params=_params("parallel"),
    )(*([u] * (w // LANES)), wt_state)


def _s5_scan_kernel(s1_ref, s2_ref, m_ref, init_ref, x_ref, last_ref, v1_ref, v2_ref, *, reverse):
    @pl.when(pl.program_id(0) == 0)
    def _():
        v1_ref[...] = init_ref[0]
        v2_ref[...] = init_ref[1]

    a1, a2, a3 = m_ref[0], m_ref[1], m_ref[2]
    cb = s1_ref.shape[0]

    def body(j, carry):
        v1, v2 = carry
        jj = cb - 1 - j if reverse else j
        x_ref[jj] = v1
        return (a1 * v1 + a2 * v2 + s1_ref[jj], a1 * v2 + a3 * v1 + s2_ref[jj])

    v1, v2 = lax.fori_loop(0, cb, body, (v1_ref[...], v2_ref[...]))
    v1_ref[...] = v1
    v2_ref[...] = v2
    last_ref[...] = v1


def s5_chunk_scan(s1, s2, mult, init, reverse):
    nc, g, n = s1.shape
    cb = min(S5_TILE, nc)
    nb = nc // cb
    blk = (lambda i: (nb - 1 - i, 0, 0)) if reverse else (lambda i: (i, 0, 0))
    return pl.pallas_call(
        functools.partial(_s5_scan_kernel, reverse=reverse),
        grid=(nb,),
        in_specs=[pl.BlockSpec((cb, g, n), blk), pl.BlockSpec((cb, g, n), blk),
                  pl.BlockSpec((3, g, n), lambda i: (0, 0, 0)), pl.BlockSpec((2, g, n), lambda i: (0, 0, 0))],
        out_specs=[pl.BlockSpec((cb, g, n), blk), pl.BlockSpec((g, n), lambda i: (0, 0))],
        out_shape=[jax.ShapeDtypeStruct((nc, g, n), F32), jax.ShapeDtypeStruct((g, n), F32)],
        scratch_shapes=[pltpu.VMEM((g, n), F32), pltpu.VMEM((g, n), F32)],
        compiler_params=_params("arbitrary"),
    )(s1, s2, mult, init)


def _s5_readout_kernel(ut_ref, tt_ref, rt_ref, xf_ref, xb_ref, y_ref, yt_ref, *slab_refs):
    L = S5_CHUNK
    g, lc, nck = ut_ref.shape
    c = lc // L
    gs = g // len(slab_refs)
    for gi in range(g):
        xin_t = jnp.concatenate([jnp.transpose(xf_ref[:, gi, :]), jnp.transpose(xb_ref[:, gi, :])], axis=0)
        yt_ref[gi] = (jnp.dot(tt_ref[gi], ut_ref[gi], preferred_element_type=F32)
                      + jnp.dot(rt_ref[gi], xin_t.astype(BF16), preferred_element_type=F32))
    for j, slab in enumerate(slab_refs):
        for l in range(L):
            step_l = yt_ref[j * gs:(j + 1) * gs, l * c:(l + 1) * c, :].reshape(gs * c, nck)
            slab[pl.ds(l, nck, stride=L), :] = jnp.transpose(step_l)
        y_ref[:, j * gs * c:(j + 1) * gs * c] = slab[...]


def s5_readout(ut, toep_t, r_state_t, xin_f, xin_b):
    g, lc, nc = ut.shape
    n = xin_f.shape[2]
    tile = min(S5_TILE, nc)
    fixed = lambda i: (0, 0, 0)
    state_blk = pl.BlockSpec((tile, g, n), lambda i: (i, 0, 0))
    return pl.pallas_call(
        _s5_readout_kernel,
        grid=(nc // tile,),
        in_specs=[pl.BlockSpec((g, lc, tile), lambda i: (0, 0, i)),
                  pl.BlockSpec(toep_t.shape, fixed), pl.BlockSpec(r_state_t.shape, fixed), state_blk, state_blk],
        out_specs=pl.BlockSpec((tile * S5_CHUNK, g * lc // S5_CHUNK), lambda i: (i, 0)),
        out_shape=jax.ShapeDtypeStruct((nc * S5_CHUNK, g * lc // S5_CHUNK), F32),
        scratch_shapes=[pltpu.VMEM((g, lc, tile), F32)]
        + [pltpu.VMEM((tile * S5_CHUNK, LANES), F32)] * (g * lc // S5_CHUNK // LANES),
        compiler_params=_params("parallel"),
    )(ut, toep_t, r_state_t, xin_f, xin_b)


def s5_mixer(u_ctx, u_lat, mats):
    toep_t, wt_state, r_state_t, mult_f, mult_b = mats
    L = S5_CHUNK
    g, n = mult_f.shape[1:]
    halves = lambda v: jnp.stack([v, jnp.roll(v, n // 2, axis=-1)])
    n_ctx = u_ctx.shape[0] // L
    ctx_chunks = -(-(n_ctx + 1) // S5_TILE) * S5_TILE
    ctx_pad = jnp.pad(u_ctx, ((0, ctx_chunks * L - u_ctx.shape[0]), (0, 0)))
    _, cf1, cf2, cb1, cb2 = s5_pack(ctx_pad, wt_state)
    zero = jnp.zeros((2, g, n), F32)
    ctx_f, _ = s5_chunk_scan(cf1, cf2, mult_f, zero, False)
    _, ctx_b_last = s5_chunk_scan(cb1, cb2, mult_b, zero, True)
    ut, f1, f2, b1, b2 = s5_pack(u_lat, wt_state)
    xin_f, _ = s5_chunk_scan(f1, f2, mult_f, halves(ctx_f[n_ctx]), False)
    xin_b, _ = s5_chunk_scan(b1, b2, mult_b, halves(ctx_b_last), True)
    return s5_readout(ut, toep_t, r_state_t, xin_f, xin_b)


def _na_window(rows):
    step = NA_STEP_ROWS
    assert rows % step == 0 and (NA_KH // 2) % step == 0 and rows >= NA_KH + step
    groups = rows // step
    n_blk = (NA_KH + step) // step
    first = np.clip(np.arange(groups) - NA_KH // 2 // step, 0, groups - n_blk)
    return groups, n_blk, first


def na_bias_table(rpb, rows):
    step = NA_STEP_ROWS
    groups, n_blk, first = _na_window(rows)
    win = n_blk * step
    q_col = np.arange(GRID_W)
    col_start = np.clip(q_col - NA_KW // 2, 0, GRID_W - NA_KW)
    key_col = np.arange(GRID_W)
    off = key_col[None, :] - col_start[:, None]
    valid_col = (off >= 0) & (off < NA_KW)
    rel_col = np.clip(key_col[None, :] - q_col[:, None] + NA_KW - 1, 0, 2 * NA_KW - 2)
    q_row = step * np.arange(groups)[:, None, None] + np.arange(step)[None, :, None]
    key_row = step * first[:, None, None] + np.arange(win)[None, None, :]
    row0 = np.clip(q_row - NA_KH // 2, 0, rows - NA_KH)
    rel_row = np.where((key_row >= row0) & (key_row < row0 + NA_KH), key_row - q_row + NA_KH - 1, -1)
    variant = np.arange(groups) - first
    reps = [int(np.argmax(variant == d)) for d in range(n_blk)]
    assert all((rel_row[g] == rel_row[reps[variant[g]]]).all() for g in range(groups))
    rel_row = rel_row[reps]
    pick_col = jnp.asarray(np.arange(2 * NA_KW - 1)[:, None, None] == rel_col[None], F32)
    tiles = jnp.where(valid_col[None, None], jnp.einsum('hrx,xck->rhck', rpb.astype(F32), pick_col, precision=HIGHEST),
                      NEG_BIG)
    tiles = jnp.concatenate([tiles, jnp.full_like(tiles[:1], NEG_BIG)], axis=0)
    return jnp.stack([jnp.concatenate([jnp.concatenate([tiles[int(rel_row[v, j, i])] for i in range(win)], axis=-1)
                                       for j in range(step)], axis=1) for v in range(n_blk)])


def _na_kernel(*refs, scale, n_blk):
    q_ref = refs[0]
    k_refs = refs[1:1 + n_blk]
    v_refs = refs[1 + n_blk:1 + 2 * n_blk]
    kc_ref, vc_ref, b_ref, o_ref = refs[1 + 2 * n_blk:]
    hd = NA_HEAD_DIM
    width = q_ref.shape[1]
    span = min(MXU_DEPTH, width)
    nt = (((1,), (1,)), ((), ()))
    q = q_ref[...] * scale
    kk = jnp.concatenate([r[...] for r in k_refs], axis=0)
    vv = jnp.concatenate([r[...] for r in v_refs], axis=0)
    kc, vc = kc_ref[...], vc_ref[...]
    nq = q.shape[0]
    per = span // hd
    lane = lax.broadcasted_iota(I32, (nq, span), 1)
    own = [(lane >= j * hd) & (lane < (j + 1) * hd) for j in range(per)]
    out_cols = []
    for c0 in range(0, width, span):
        cols = slice(c0, c0 + span)
        qs = jnp.concatenate([jnp.where(own[j], q[:, cols], jnp.zeros_like(q[:, cols])) for j in range(per)], axis=0)
        h0 = c0 // hd
        bias = b_ref[0, h0:h0 + per].reshape(per * nq, kk.shape[0])
        s = lax.dot_general(qs, kk[:, cols], nt, preferred_element_type=F32) + bias
        sc = lax.dot_general(qs, kc[:, cols], nt, preferred_element_type=F32)
        m = jnp.maximum(jnp.max(s, axis=-1, keepdims=True), jnp.max(sc, axis=-1, keepdims=True))
        p = jnp.exp(s - m)
        pc = jnp.exp(sc - m)
        den = jnp.sum(p, axis=-1, keepdims=True) + jnp.sum(pc, axis=-1, keepdims=True)
        o = (jnp.dot(p.astype(BF16), vv[:, cols], preferred_element_type=F32)
             + jnp.dot(pc.astype(BF16), vc[:, cols], preferred_element_type=F32)) / den
        acc = jnp.zeros((nq, span), F32)
        for j in range(per):
            acc = jnp.where(own[j], o[j * nq:(j + 1) * nq], acc)
        out_cols.append(acc)
    o_ref[...] = jnp.concatenate(out_cols, axis=1).astype(o_ref.dtype)


def neighbourhood_attention(q, k, v, k_ctx, v_ctx, bias_table):
    t, w = q.shape
    groups, n_blk, _ = _na_window(t // GRID_W)
    first = lambda g: jnp.clip(g - NA_KH // 2 // NA_STEP_ROWS, 0, groups - n_blk)
    blk_tokens = NA_STEP_ROWS * GRID_W
    row_blk = pl.BlockSpec((blk_tokens, w), lambda g: (g, 0))
    key_blks = [pl.BlockSpec((blk_tokens, w), functools.partial(lambda g, i: (first(g) + i, 0), i=i))
                for i in range(n_blk)]
    ctx_blk = pl.BlockSpec(k_ctx.shape, lambda g: (0, 0))
    bias_blk = pl.BlockSpec((1,) + bias_table.shape[1:], lambda g: (g - first(g), 0, 0, 0))
    return pl.pallas_call(
        functools.partial(_na_kernel, scale=NA_HEAD_DIM ** -0.5, n_blk=n_blk),
        grid=(groups,),
        in_specs=[row_blk] + key_blks + key_blks + [ctx_blk, ctx_blk, bias_blk],
        out_specs=row_blk,
        out_shape=jax.ShapeDtypeStruct((t, w), BF16),
        compiler_params=_params("parallel"),
    )(q, *([k] * n_blk), *([v] * n_blk), k_ctx, v_ctx, bias_table)


def time_dft_tables(t):
    a_len = 1 << (int(math.log2(t)) // 2)
    b_len = t // a_len
    ka = np.arange(a_len)[:, None]
    tok = b_len * np.arange(a_len)[None, :]
    ang1 = -2.0 * np.pi * ((ka * (tok[None] + np.arange(b_len)[:, None, None])) % t) / t
    stage1 = np.concatenate([np.cos(ang1), np.sin(ang1)], axis=1) / math.sqrt(t)
    ang2 = 2.0 * np.pi * ((np.arange(b_len)[:, None] * np.arange(b_len)[None, :]) % b_len) / b_len
    c2, s2 = np.cos(ang2), np.sin(ang2)
    stage2 = np.block([[c2, s2], [-s2, c2]])
    return jnp.asarray(stage1, BF16), jnp.asarray(stage2, BF16), a_len, b_len


def _time_dft_kernel(x_ref, m_ref, w2_ref, zre_ref, zim_ref, y_ref, z_ref, *, a_len, b_len):
    i = pl.program_id(1)
    bb = m_ref.shape[0]

    def stage1(jb, carry):
        b = i * bb + jb
        xb = x_ref[pl.ds(b, a_len, stride=b_len), :]
        y = jnp.dot(m_ref[jb], xb.astype(BF16), preferred_element_type=F32)
        row = pl.multiple_of(b * a_len, a_len)
        y_ref[pl.ds(row, a_len), :] = _pack_pair(y[:a_len], y[a_len:])
        return carry

    lax.fori_loop(0, bb, stage1, 0, unroll=16)

    @pl.when(i == pl.num_programs(1) - 1)
    def _():
        def stage2(ka, carry):
            rows = pl.ds(ka, b_len, stride=a_len)
            yre, yim = _unpack_pair(y_ref[rows, :])
            y = jnp.concatenate([yre, yim], axis=0).astype(BF16)
            z = jnp.dot(w2_ref[...], y, preferred_element_type=F32)
            z_ref[rows, :] = _pack_pair(z[:b_len], z[b_len:])
            return carry

        lax.fori_loop(0, a_len, stage2, 0, unroll=16)
        zre, zim = _unpack_pair(z_ref[...])
        zre_ref[...] = zre.astype(zre_ref.dtype)
        zim_ref[...] = zim.astype(zim_ref.dtype)


def time_dft(x):
    t, d = x.shape
    stage1, stage2, a_len, b_len = time_dft_tables(t)
    lanes = 128
    bb = min(32, b_len)
    out = jax.ShapeDtypeStruct((t, d), BF16)
    return pl.pallas_call(
        functools.partial(_time_dft_kernel, a_len=a_len, b_len=b_len),
        grid=(d // lanes, b_len // bb),
        in_specs=[pl.BlockSpec((t, lanes), lambda j, i: (0, j)),
                  pl.BlockSpec((bb, 2 * a_len, a_len), lambda j, i: (i, 0, 0)),
                  pl.BlockSpec(stage2.shape, lambda j, i: (0, 0))],
        out_specs=[pl.BlockSpec((t, lanes), lambda j, i: (0, j))] * 2,
        out_shape=[out, out],
        scratch_shapes=[pltpu.VMEM((t, lanes), U32), pltpu.VMEM((t, lanes), U32)],
        compiler_params=_params("parallel", "arbitrary"),
    )(x, stage1, stage2)


def channel_dft_tables(c):
    ang = 2.0 * np.pi * ((np.arange(c)[:, None] * np.arange(c)[None, :]) % c) / c
    return (jnp.asarray(np.cos(ang) / math.sqrt(c), BF16), jnp.asarray(np.sin(ang) / math.sqrt(c), BF16))


def _ffn_prologue(h, shf_ref, scf_ref, wr_ref, wsg_ref, wsu_ref, wsd_ref, h_ref, f_ref, lg_ref, shared_ref):
    h_ref[...] = h
    f = _rms(h) * (1.0 + scf_ref[...]) + shf_ref[...]
    packed = _pack_rows(f)
    dq = packed.shape[1] // ROW_PARTS
    for i in range(ROW_PARTS):
        f_ref[i] = packed[:, i * dq:(i + 1) * dq]
    fb = f.astype(BF16)
    f_lo = (f - fb.astype(F32)).astype(BF16)
    nt = (((1,), (1,)), ((), ()))
    wr = wr_ref[...]
    wr_hi = wr.astype(BF16)
    wr_lo = (wr - wr_hi.astype(F32)).astype(BF16)
    lg_ref[...] = (lax.dot_general(wr_hi, fb, nt, preferred_element_type=F32)
                   + lax.dot_general(wr_hi, f_lo, nt, preferred_element_type=F32)
                   + lax.dot_general(wr_lo, fb, nt, preferred_element_type=F32))
    hid = (_silu(jnp.dot(fb, wsg_ref[...], preferred_element_type=F32))
           * jnp.dot(fb, wsu_ref[...], preferred_element_type=F32))
    shared_ref[...] = jnp.dot(hid.astype(BF16), wsd_ref[...], preferred_element_type=F32).astype(shared_ref.dtype)


def _gelu_tanh(x):
    return 0.5 * x * (1.0 + jnp.tanh(math.sqrt(2.0 / math.pi) * (x + 0.044715 * (x * x * x))))


def _even_post_kernel(y_ref, na_ref, x_ref, wglu_ref, bglu_ref, wo_ref, gm_ref, *rest):
    g = _gelu_tanh(y_ref[...])
    gate = jax.nn.sigmoid(jnp.dot(g.astype(BF16), wglu_ref[...], preferred_element_type=F32) + bglu_ref[...])
    s5 = (g * gate).astype(BF16)
    w = s5.shape[1]
    mix = (jnp.dot(s5, wo_ref[:w, :], preferred_element_type=F32)
           + jnp.dot(na_ref[...], wo_ref[w:, :], preferred_element_type=F32))
    _ffn_prologue(x_ref[...] + gm_ref[...] * mix, *rest)


def _odd_post_kernel(zre_ref, zim_ref, h_ref_in, cc_ref, sc_ref, wf_ref, bf_ref, gm_ref, *rest):
    c = cc_ref.shape[0]
    parts = []
    for grp in range(zre_ref.shape[1] // c):
        cols = slice(grp * c, (grp + 1) * c)
        parts.append(jnp.dot(zre_ref[:, cols], cc_ref[...], preferred_element_type=F32)
                     + jnp.dot(zim_ref[:, cols], sc_ref[...], preferred_element_type=F32))
    fr = jnp.concatenate(parts, axis=-1).astype(BF16)
    mix = jnp.dot(fr, wf_ref[...], preferred_element_type=F32) + bf_ref[...]
    _ffn_prologue(h_ref_in[...] + gm_ref[...] * mix, *rest)


def _post_call(body, row_inputs, fixed_inputs, t, d, n_exp):
    tm = min(512, t)
    dq = d // 2 // ROW_PARTS
    row = lambda i: (i, 0)
    in_specs = ([pl.BlockSpec((tm, a.shape[1]), row) for a in row_inputs]
                + [pl.BlockSpec(a.shape, functools.partial(lambda i, nd: (0,) * nd, nd=a.ndim))
                   for a in fixed_inputs])
    return pl.pallas_call(
        body,
        grid=(t // tm,),
        in_specs=in_specs,
        out_specs=[pl.BlockSpec((tm, d), row), pl.BlockSpec((ROW_PARTS, tm, dq), lambda i: (0, i, 0)),
                   pl.BlockSpec((n_exp, tm), lambda i: (0, i)), pl.BlockSpec((tm, d), row)],
        out_shape=[jax.ShapeDtypeStruct((t, d), F32), jax.ShapeDtypeStruct((ROW_PARTS, t, dq), U32),
                   jax.ShapeDtypeStruct((n_exp, t), F32), jax.ShapeDtypeStruct((t, d), BF16)],
        compiler_params=_params("parallel"),
    )(*row_inputs, *fixed_inputs)


def _route_kernel(lg_ref, bias_ref, tri_ref, idx_ref, gate_ref, rank_ref, cnt_ref, run_ref):
    @pl.when(pl.program_id(0) == 0)
    def _():
        run_ref[...] = jnp.zeros_like(run_ref)

    scores = jax.nn.sigmoid(lg_ref[...])
    n_exp, tb = scores.shape
    sel = scores + bias_ref[...]
    gsz = n_exp // N_EXPERT_GROUPS
    member = lax.broadcasted_iota(I32, (gsz, tb), 0)
    gscore = []
    for grp in range(N_EXPERT_GROUPS):
        xg = sel[grp * gsz:(grp + 1) * gsz, :]
        m1 = jnp.max(xg, axis=0, keepdims=True)
        first = jnp.min(jnp.where(xg == m1, member, gsz), axis=0, keepdims=True)
        m2 = jnp.max(jnp.where(member == first, -jnp.inf, xg), axis=0, keepdims=True)
        gscore.append(m1 + m2)
    keep_rows = []
    for grp in range(N_EXPERT_GROUPS):
        beaten = jnp.zeros((1, tb), F32)
        for other in range(N_EXPERT_GROUPS):
            if other == grp:
                continue
            wins = (gscore[other] >= gscore[grp]) if other < grp else (gscore[other] > gscore[grp])
            beaten = beaten + jnp.where(wins, 1.0, 0.0)
        keep_rows.append(jnp.broadcast_to(beaten < TOPK_GROUPS, (gsz, tb)))
    masked = jnp.where(jnp.concatenate(keep_rows, axis=0), sel, -jnp.inf)

    expert = lax.broadcasted_iota(I32, (n_exp, tb), 0)
    picks, gates, hots = [], [], []
    chosen = jnp.zeros((n_exp, tb), F32)
    for _ in range(TOP_K):
        m = jnp.max(masked, axis=0, keepdims=True)
        pick = jnp.min(jnp.where(masked == m, expert, n_exp), axis=0, keepdims=True)
        hot = expert == pick
        picks.append(pick)
        hots.append(hot)
        gates.append(jnp.sum(jnp.where(hot, scores, 0.0), axis=0, keepdims=True))
        chosen = jnp.where(hot, 1.0, chosen)
        masked = jnp.where(hot, -jnp.inf, masked)
    total = gates[0]
    for gk in gates[1:]:
        total = total + gk
    ahead = jnp.dot(chosen.astype(BF16), tri_ref[...], preferred_element_type=F32) + run_ref[...]
    for k in range(TOP_K):
        idx_ref[k:k + 1, :] = picks[k]
        gate_ref[k:k + 1, :] = ROUTED_SCALE * gates[k] / total
        rank_ref[k:k + 1, :] = jnp.sum(jnp.where(hots[k], ahead, 0.0), axis=0, keepdims=True).astype(I32)
    run_ref[...] = run_ref[...] + jnp.sum(chosen, axis=1, keepdims=True)
    cnt_ref[...] = jnp.broadcast_to(run_ref[...], cnt_ref.shape)


def route(logits_t, router_bias):
    n_exp, t = logits_t.shape
    tb = min(1024, t)
    tri = jnp.asarray(np.triu(np.ones((tb, tb), np.float32), k=1), BF16)
    tok = lambda i: (0, i)
    idx, gate, rank, cnt = pl.pallas_call(
        _route_kernel,
        grid=(t // tb,),
        in_specs=[pl.BlockSpec((n_exp, tb), tok), pl.BlockSpec((n_exp, 1), lambda i: (0, 0)),
                  pl.BlockSpec((tb, tb), lambda i: (0, 0))],
        out_specs=[pl.BlockSpec((TOP_K, tb), tok)] * 3 + [pl.BlockSpec((n_exp, 128), lambda i: (0, 0))],
        out_shape=[jax.ShapeDtypeStruct((TOP_K, t), I32), jax.ShapeDtypeStruct((TOP_K, t), F32),
                   jax.ShapeDtypeStruct((TOP_K, t), I32), jax.ShapeDtypeStruct((n_exp, 128), F32)],
        scratch_shapes=[pltpu.VMEM((n_exp, 1), F32)],
        compiler_params=_params("arbitrary"),
    )(logits_t, router_bias.astype(F32).reshape(n_exp, 1), tri)
    return idx, gate, rank, cnt[:, 0].astype(I32)


def dispatch_plan(idx, rank, counts, n_blocks):
    n_exp = counts.shape[0]
    shift = EXPERT_ROWS.bit_length() - 1
    assert EXPERT_ROWS == 1 << shift
    padded = ((counts + EXPERT_ROWS - 1) >> shift) << shift
    pad_end = jnp.cumsum(padded)
    pad_start = pad_end - padded
    experts_iota = jnp.arange(n_exp, dtype=I32)
    dest = jnp.sum(jnp.where(idx[..., None] == experts_iota, pad_start, 0), axis=-1) + rank
    n_valid = (pad_end[-1] >> shift).astype(I32)
    blk = jnp.minimum(jnp.arange(n_blocks, dtype=I32), n_valid - 1)
    blk_e = jnp.sum((pad_end[None, :] <= blk[:, None] * EXPERT_ROWS).astype(I32), axis=1)
    return dest.astype(I32), jnp.minimum(blk_e, n_exp - 1).astype(I32), n_valid.reshape(1)


def _sc_mesh():
    return plsc.VectorSubcoreMesh(core_axis_name="core", subcore_axis_name="subcore")


def dispatch(f, dest, cap):
    t, w = f.shape
    n_choice = dest.shape[0]

    @functools.partial(pl.kernel, out_type=jax.ShapeDtypeStruct((cap, w), f.dtype), mesh=_sc_mesh(),
                       scratch_types=[pltpu.SemaphoreType.DMA])
    def scatter_rows(x_hbm, i_hbm, o_hbm, sem):
        def body(x_vmem, i_vmem):
            copies = [pltpu.async_copy(x_vmem, o_hbm.at[i_vmem.at[k]], sem) for k in range(n_choice)]
            for cp in copies:
                cp.wait()

        pltpu.emit_pipeline(
            body,
            grid=(t // SC_WINDOW,),
            in_specs=[pl.BlockSpec((SC_WINDOW, w), lambda i: (i, 0)),
                      pl.BlockSpec((n_choice, SC_WINDOW), lambda i: (0, i))],
            out_specs=[],
            core_axis_name=("core", "subcore"),
            dimension_semantics=(pltpu.PARALLEL,),
        )(x_hbm, i_hbm)

    return scatter_rows(f, dest)


def gather_rows(rows, index_row):
    n = index_row.shape[1]
    w = rows.shape[1]

    piece = SC_WINDOW // SC_GATHER_SPLIT

    @functools.partial(pl.kernel, out_type=jax.ShapeDtypeStruct((n, w), rows.dtype), mesh=_sc_mesh(),
                       scratch_types=[pltpu.SemaphoreType.DMA])
    def gather(y_hbm, i_hbm, o_hbm, sem):
        def body(i_vmem, o_vmem):
            copies = [pltpu.async_copy(y_hbm.at[i_vmem.at[0, pl.ds(j * piece, piece)]],
                                       o_vmem.at[pl.ds(j * piece, piece)], sem)
                      for j in range(SC_GATHER_SPLIT)]
            for cp in copies:
                cp.wait()

        pltpu.emit_pipeline(
            body,
            grid=(n // SC_WINDOW,),
            in_specs=[pl.BlockSpec((1, SC_WINDOW), lambda i: (0, i))],
            out_specs=[pl.BlockSpec((SC_WINDOW, w), lambda i: (i, 0))],
            core_axis_name=("core", "subcore"),
            dimension_semantics=(pltpu.PARALLEL,),
        )(i_hbm, o_hbm)

    return gather(rows, index_row)


def _experts_kernel(be_ref, x_ref, wg_ref, wu_ref, wd_ref, y_ref, wg_bf, wu_bf, wd_bf):
    b = pl.program_id(0)

    @pl.when((b == 0) | (be_ref[b] != be_ref[jnp.maximum(b - 1, 0)]))
    def _():
        wg_bf[...] = wg_ref[0, 0].astype(BF16)
        wu_bf[...] = wu_ref[0, 0].astype(BF16)
        wd_bf[...] = wd_ref[0, 0].astype(BF16)

    x = _unpack_rows(jnp.concatenate([x_ref[i] for i in range(ROW_PARTS)], axis=1)).astype(BF16)
    hid = (_silu(jnp.dot(x, wg_bf[...], preferred_element_type=F32))
           * jnp.dot(x, wu_bf[...], preferred_element_type=F32))
    packed = _pack_rows(jnp.dot(hid.astype(BF16), wd_bf[...], preferred_element_type=F32))
    dq = packed.shape[1] // ROW_PARTS
    for i in range(ROW_PARTS):
        y_ref[i] = packed[:, i * dq:(i + 1) * dq]


def experts(rows, blk_e, n_valid, layer, w_gate, w_up, w_down):
    parts, cap, dq = rows.shape
    _, _, d, ff = w_gate.shape
    blk = pl.BlockSpec((parts, EXPERT_ROWS, dq), lambda b, be: (0, b, 0))
    return pl.pallas_call(
        _experts_kernel,
        grid_spec=pltpu.PrefetchScalarGridSpec(
            num_scalar_prefetch=1,
            grid=(n_valid[0],),
            in_specs=[blk,
                      pl.BlockSpec((1, 1, d, ff), lambda b, be: (layer, be[b], 0, 0)),
                      pl.BlockSpec((1, 1, d, ff), lambda b, be: (layer, be[b], 0, 0)),
                      pl.BlockSpec((1, 1, ff, d), lambda b, be: (layer, be[b], 0, 0))],
            out_specs=blk,
            scratch_shapes=[pltpu.VMEM((d, ff), BF16), pltpu.VMEM((d, ff), BF16), pltpu.VMEM((ff, d), BF16)],
        ),
        out_shape=jax.ShapeDtypeStruct(rows.shape, U32),
        compiler_params=_params("arbitrary"),
    )(blk_e, rows, w_gate, w_up, w_down)


def _combine_kernel(gate_ref, h_ref, shared_ref, gf_ref, sh_ref, sc_ref, *rest):
    y_refs, outs = rest[:TOP_K], rest[TOP_K:]
    routed = None
    for k in range(TOP_K):
        y = _unpack_rows(jnp.concatenate([y_refs[k][i, 0] for i in range(ROW_PARTS)], axis=1))
        routed = gate_ref[:, k:k + 1] * y if routed is None else routed + gate_ref[:, k:k + 1] * y
    h = h_ref[...] + gf_ref[...] * (routed + shared_ref[...].astype(F32))
    outs[0][...] = h
    if len(outs) > 1:
        outs[1][...] = _rms(h) * (1.0 + sc_ref[...]) + sh_ref[...]


def combine(y_rows, dest, gate_tk, h, shared, gate_ffn, next_shift, next_scale, with_next):
    t, d = h.shape
    parts, cap, dq = y_rows.shape
    n_out = 2 if with_next else 1
    n_choice = dest.shape[0]
    flat = (dest[None] + (jnp.arange(parts, dtype=I32) * cap)[:, None, None]).reshape(1, parts * n_choice * t)
    picked = gather_rows(y_rows.reshape(parts * cap, dq), flat).reshape(parts, n_choice, t, dq)
    tm = min(512, t)
    row = lambda i: (i, 0)
    fixed = lambda i: (0, 0)
    choice = [pl.BlockSpec((parts, 1, tm, dq), functools.partial(lambda i, k: (0, k, i, 0), k=k))
              for k in range(n_choice)]
    return pl.pallas_call(
        _combine_kernel,
        grid=(t // tm,),
        in_specs=[pl.BlockSpec((tm, n_choice), row), pl.BlockSpec((tm, d), row), pl.BlockSpec((tm, d), row),
                  pl.BlockSpec((1, d), fixed), pl.BlockSpec((1, d), fixed), pl.BlockSpec((1, d), fixed)] + choice,
        out_specs=[pl.BlockSpec((tm, d), row)] * n_out,
        out_shape=[jax.ShapeDtypeStruct((t, d), F32)] * n_out,
        compiler_params=_params("parallel"),
    )(gate_tk, h, shared, gate_ffn, next_shift, next_scale, *([picked] * n_choice))


def moe_tail(h, f_rows, logits_t, shared, router_bias, layer, w_gate, w_up, w_down, gate_ffn,
             next_shift, next_scale, with_next):
    t, d = h.shape
    parts, _, dq = f_rows.shape
    n_exp = w_gate.shape[1]
    n_blocks = -(-(t * TOP_K + n_exp * (EXPERT_ROWS - 1)) // EXPERT_ROWS)
    cap = n_blocks * EXPERT_ROWS
    idx, gate, rank, counts = route(logits_t, router_bias)
    dest, blk_e, n_valid = dispatch_plan(idx, rank, counts, n_blocks)
    dest_all = jnp.concatenate([dest + i * cap for i in range(parts)], axis=1)
    rows = dispatch(f_rows.reshape(parts * t, dq), dest_all, parts * cap).reshape(parts, cap, dq)
    y_rows = experts(rows, blk_e, n_valid, layer, w_gate, w_up, w_down)
    return combine(y_rows, dest, gate.T, h, shared, gate_ffn, next_shift, next_scale, with_next)


def kernel(x, c, ctx, c_ctx, w_ada, b_ada, w_in, s5_lam_re, s5_lam_im, s5_log_dt, s5_b_re, s5_b_im,
           s5_c_re, s5_c_im, s5_d, s5_w_glu, s5_b_glu, na_q_gain, na_k_gain, na_rpb, w_mix_out,
           w_fourier_out, b_fourier_out, w_router, router_bias, w_exp_gate, w_exp_up, w_exp_down,
           w_sh_gate, w_sh_up, w_sh_down):
    bsz, t, d = x.shape
    assert bsz == 1 and w_ada.shape[0] == 2
    n_exp = w_router.shape[2]
    s5w = s5_w_glu.shape[1]
    naw = w_in.shape[2] - s5w
    naw //= 3
    heads = naw // NA_HEAD_DIM

    cond8 = jnp.concatenate([c[:1].astype(F32), c_ctx.astype(F32)[None], jnp.zeros((6, d), F32)], axis=0)
    ada = adaln_all(cond8, w_ada, b_ada)
    mod = lambda layer, who, j: ada[layer, who:who + 1, j * d:(j + 1) * d]

    def ffn_weights(i):
        return (mod(i, 0, 3), mod(i, 0, 4), jnp.transpose(w_router[i]).astype(F32),
                w_sh_gate[i].astype(BF16), w_sh_up[i].astype(BF16), w_sh_down[i].astype(BF16))

    h0 = x[0]
    seg = jnp.asarray(np.kron(np.eye(heads), np.ones((NA_HEAD_DIM, NA_HEAD_DIM))), BF16)
    w_in_b = w_in[0].astype(BF16)
    qg = jnp.tile(na_q_gain[0].astype(F32), heads)[None]
    kg = jnp.tile(na_k_gain[0].astype(F32), heads)[None]
    u_c, _, k_c, v_c = in_projection(ctx[0], mod(0, 1, 0), mod(0, 1, 1), w_in_b, seg, qg, kg, s5w, naw)
    u_l, q_l, k_l, v_l = in_projection(h0, mod(0, 0, 0), mod(0, 0, 1), w_in_b, seg, qg, kg, s5w, naw)
    mats = s5_matrices(s5_lam_re[0], s5_lam_im[0], s5_log_dt[0], s5_b_re[0], s5_b_im[0],
                       s5_c_re[0], s5_c_im[0], s5_d[0])
    y_s5 = s5_mixer(u_c, u_l, mats)
    na = neighbourhood_attention(q_l, k_l, v_l, k_c, v_c, na_bias_table(na_rpb[0], t // GRID_W))
    h1, f1, lg1, sh1 = _post_call(
        _even_post_kernel, [y_s5, na, h0],
        [s5_w_glu[0].astype(BF16), s5_b_glu[0].astype(F32)[None], w_mix_out[0].astype(BF16), mod(0, 0, 2),
         *ffn_weights(0)], t, d, n_exp)
    h2, a1 = moe_tail(h1, f1, lg1, sh1, router_bias[0], 0, w_exp_gate, w_exp_up, w_exp_down,
                      mod(0, 0, 5), mod(1, 0, 0), mod(1, 0, 1), True)

    zre, zim = time_dft(a1)
    cc, sc = channel_dft_tables(d // FOURIER_GROUPS)
    h3, f3, lg3, sh3 = _post_call(
        _odd_post_kernel, [zre, zim, h2],
        [cc, sc, w_fourier_out[0].astype(BF16), b_fourier_out[0].astype(F32)[None], mod(1, 0, 2),
         *ffn_weights(1)], t, d, n_exp)
    zero_row = jnp.zeros((1, d), F32)
    (out,) = moe_tail(h3, f3, lg3, sh3, router_bias[1], 1, w_exp_gate, w_exp_up, w_exp_down,
                      mod(1, 0, 5), zero_row, zero_row, False)
    return out[None]
```
